```python
import math
import jax, jax.numpy as jnp
from jax import lax
import numpy as np

D_MODEL = 1024
BATCH = 8
SEQ = 4096
DEPTH = 4

N_MIXERS = 3
BLOCK = 128
RMS_EPS = 1e-6

SWA_HEADS = 16
SWA_KV_HEADS = 2
SWA_HEAD_DIM = 64
SWA_WINDOW = 128

REL_BUCKETS = 32
REL_MAX_DIST = 128

SB_HEADS = 16
SB_HEAD_DIM = 64

MLA_HEADS = 16
MLA_NOPE_DIM = 64
MLA_ROPE_DIM = 32
MLA_V_DIM = 64
MLA_Q_RANK = 384
MLA_KV_RANK = 256
ROPE_THETA = 10000.0

FFN_DIM = 2816
N_EXPERTS = 8
TOP_K = 2
EXPERT_DIM = 3584

kernel_name = 'hybrid_swa_stickbreak_mla_moe_trunk'


def rms_norm(x, g):
    xf = x.astype(jnp.float32)
    y = xf * lax.rsqrt(jnp.mean(xf * xf, axis=-1, keepdims=True) + RMS_EPS)
    return (y * g.astype(jnp.float32)).astype(x.dtype)


def t5_causal_bucket(dist):
    max_exact = REL_BUCKETS // 2
    d = jnp.maximum(dist, 1).astype(jnp.float32)
    large = max_exact + (jnp.log(d / max_exact) / math.log(REL_MAX_DIST / max_exact)
                         * (REL_BUCKETS - max_exact)).astype(jnp.int32)
    large = jnp.minimum(large, REL_BUCKETS - 1)
    return jnp.where(dist < max_exact, dist, large)


def rope_tables(seq, dim):
    inv = ROPE_THETA ** (-jnp.arange(0, dim, 2, dtype=jnp.float32) / dim)
    ang = jnp.arange(seq, dtype=jnp.float32)[:, None] * inv[None, :]
    return jnp.cos(ang), jnp.sin(ang)


def apply_rope(x, cos, sin):
    x1, x2 = jnp.split(x, 2, axis=-1)
    c = cos[None, :, None, :].astype(x.dtype)
    s = sin[None, :, None, :].astype(x.dtype)
    return jnp.concatenate([x1 * c - x2 * s, x1 * s + x2 * c], axis=-1)


def swa_sink_attention(h, w_qkv, b_qkv, sinks, w_o, rel_bias):
    B, S, _ = h.shape
    nb = S // BLOCK
    G = SWA_HEADS // SWA_KV_HEADS
    qkv = h @ w_qkv + b_qkv
    q, k, v = jnp.split(qkv, [SWA_HEADS * SWA_HEAD_DIM,
                              (SWA_HEADS + SWA_KV_HEADS) * SWA_HEAD_DIM], axis=-1)
    q = q.reshape(B, nb, BLOCK, SWA_KV_HEADS, G, SWA_HEAD_DIM)
    k = k.reshape(B, nb, BLOCK, SWA_KV_HEADS, SWA_HEAD_DIM)
    v = v.reshape(B, nb, BLOCK, SWA_KV_HEADS, SWA_HEAD_DIM)

    def with_prev(t):
        prev = jnp.pad(t[:, :-1], ((0, 0), (1, 0), (0, 0), (0, 0), (0, 0)))
        return jnp.concatenate([prev, t], axis=2)

    kb, vb = with_prev(k), with_prev(v)
    s = jnp.einsum('bnqhgd,bnkhd->bnhgqk', q, kb).astype(jnp.float32) * (SWA_HEAD_DIM ** -0.5)

    qi = jnp.arange(BLOCK)[:, None]
    kj = jnp.arange(2 * BLOCK)[None, :]
    dist = qi + BLOCK - kj
    bias = rel_bias.astype(jnp.float32)[t5_causal_bucket(jnp.maximum(dist, 0))]
    bias = bias.transpose(2, 0, 1).reshape(SWA_KV_HEADS, G, BLOCK, 2 * BLOCK)
    band = (dist >= 0) & (dist < SWA_WINDOW)
    valid = band[None] & ((jnp.arange(nb)[:, None, None] > 0) | (kj[None] >= BLOCK))
    s = jnp.where(valid[None, :, None, None], s + bias, -jnp.inf)

    sink = jnp.broadcast_to(sinks.astype(jnp.float32).reshape(1, 1, SWA_KV_HEADS, G, 1, 1),
                            s.shape[:-1] + (1,))
    p = jax.nn.softmax(jnp.concatenate([s, sink], axis=-1), axis=-1)[..., :-1]
    o = jnp.einsum('bnhgqk,bnkhd->bnqhgd', p.astype(vb.dtype), vb)
    return o.reshape(B, S, SWA_HEADS * SWA_HEAD_DIM) @ w_o


def stick_breaking_attention(h, w_qkv, w_o):
    B, S, _ = h.shape
    nb = S // BLOCK
    q, k, v = jnp.split(h @ w_qkv, 3, axis=-1)
    q = q.reshape(B, nb, BLOCK, SB_HEADS, SB_HEAD_DIM).transpose(1, 0, 3, 2, 4)
    k = k.reshape(B, S, SB_HEADS, SB_HEAD_DIM)
    v = v.reshape(B, S, SB_HEADS, SB_HEAD_DIM)
    key_pos = jnp.arange(S)
    scale = SB_HEAD_DIM ** -0.5

    def block(args):
        qb, b_idx = args
        z = jnp.einsum('bhqd,bkhd->bhqk', qb, k).astype(jnp.float32) * scale
        q_pos = b_idx * BLOCK + jnp.arange(BLOCK)
        before = key_pos[None, :] < q_pos[:, None]
        log_fail = jnp.where(before, jax.nn.log_sigmoid(-z), 0.0)
        between = lax.cumsum(log_fail, axis=3, reverse=True) - log_fail
        a = jnp.where(before, jnp.exp(jax.nn.log_sigmoid(z) + between), 0.0)
        return jnp.einsum('bhqk,bkhd->bqhd', a.astype(v.dtype), v)

    o = lax.map(block, (q, jnp.arange(nb)))
    o = o.transpose(1, 0, 2, 3, 4).reshape(B, S, SB_HEADS * SB_HEAD_DIM)
    return o @ w_o


def causal_attention_blocked(q, k, v, scale):
    B, S, H, _ = q.shape
    nb = S // BLOCK
    qb = q.reshape(B, nb, BLOCK, H, -1).transpose(1, 0, 2, 3, 4)
    key_pos = jnp.arange(S)

    def block(args):
        qblk, b_idx = args
        s = jnp.einsum('bqhd,bkhd->bhqk', qblk, k).astype(jnp.float32) * scale
        q_pos = b_idx * BLOCK + jnp.arange(BLOCK)
        s = jnp.where(key_pos[None, :] <= q_pos[:, None], s, -jnp.inf)
        p = jax.nn.softmax(s, axis=-1).astype(v.dtype)
        return jnp.einsum('bhqk,bkhd->bqhd', p, v)

    o = lax.map(block, (qb, jnp.arange(nb)))
    return o.transpose(1, 0, 2, 3, 4).reshape(B, S, H * v.shape[-1])


def mla_attention(h, w_down, q_norm, w_uq, kv_norm, w_ukv, w_o):
    B, S, _ = h.shape
    c = h @ w_down
    c_q, c_kv, k_rope = jnp.split(c, [MLA_Q_RANK, MLA_Q_RANK + MLA_KV_RANK], axis=-1)
    cos, sin = rope_tables(S, MLA_ROPE_DIM)
    q = (rms_norm(c_q, q_norm) @ w_uq).reshape(B, S, MLA_HEADS, MLA_NOPE_DIM + MLA_ROPE_DIM)
    q_nope, q_rope = jnp.split(q, [MLA_NOPE_DIM], axis=-1)
    q_rope = apply_rope(q_rope, cos, sin)
    kv = (rms_norm(c_kv, kv_norm) @ w_ukv).reshape(B, S, MLA_HEADS, MLA_NOPE_DIM + MLA_V_DIM)
    k_nope, v = jnp.split(kv, [MLA_NOPE_DIM], axis=-1)
    k_rope = apply_rope(k_rope[:, :, None, :], cos, sin)
    q_cat = jnp.concatenate([q_nope, q_rope], axis=-1)
    k_cat = jnp.concatenate([k_nope, jnp.broadcast_to(k_rope, (B, S, MLA_HEADS, MLA_ROPE_DIM))], axis=-1)
    o = causal_attention_blocked(q_cat, k_cat, v, (MLA_NOPE_DIM + MLA_ROPE_DIM) ** -0.5)
    return o @ w_o


def swiglu(h, w_gate, w_up, w_down):
    return (jax.nn.silu(h @ w_gate) * (h @ w_up)) @ w_down


def moe_swiglu(h, router, w_gate, w_up, w_down):
    B, S, D = h.shape
    t = h.reshape(B * S, D)
    logits = (t @ router).astype(jnp.float32)
    top_vals, top_idx = lax.top_k(logits, TOP_K)
    gates = jax.nn.softmax(top_vals, axis=-1)
    combine = jnp.sum(jax.nn.one_hot(top_idx, N_EXPERTS, dtype=jnp.float32) * gates[..., None], axis=1)
    out = jnp.zeros_like(t)
    for e in range(N_EXPERTS):
        out = out + combine[:, e:e + 1].astype(t.dtype) * swiglu(t, w_gate[e], w_up[e], w_down[e])
    return out.reshape(B, S, D)


def _n_layers_of(mixer_id):
    return len(range(mixer_id, DEPTH, N_MIXERS))


def setup_inputs(seed: int = 0) -> dict:
    key = jax.random.key(seed)
    ks = jax.random.split(key, 24)
    f32 = jnp.float32
    nA, nB, nC = _n_layers_of(0), _n_layers_of(1), _n_layers_of(2)
    n_dense, n_moe = len(range(0, DEPTH, 2)), len(range(1, DEPTH, 2))

    def w(k, shape, fan_in):
        return jax.random.normal(k, shape, f32) * (fan_in ** -0.5)

    def gain(k, shape):
        return 1.0 + 0.02 * jax.random.normal(k, shape, f32)

    qkv_a = (SWA_HEADS + 2 * SWA_KV_HEADS) * SWA_HEAD_DIM
    mla_down = MLA_Q_RANK + MLA_KV_RANK + MLA_ROPE_DIM
    return {
        'x': jax.random.normal(ks[0], (BATCH, SEQ, D_MODEL), f32),
        'rel_bias': 0.5 * jax.random.normal(ks[1], (REL_BUCKETS, SWA_HEADS), f32),
        'attn_norm': gain(ks[2], (DEPTH, D_MODEL)),
        'ffn_norm': gain(ks[3], (DEPTH, D_MODEL)),
        'final_norm': gain(ks[4], (D_MODEL,)),
        'swa_w_qkv': w(ks[5], (nA, D_MODEL, qkv_a), D_MODEL),
        'swa_b_qkv': 0.02 * jax.random.normal(ks[6], (nA, qkv_a), f32),
        'swa_sinks': 0.5 * jax.random.normal(ks[7], (nA, SWA_HEADS), f32),
        'swa_w_o': w(ks[8], (nA, SWA_HEADS * SWA_HEAD_DIM, D_MODEL), SWA_HEADS * SWA_HEAD_DIM),
        'sb_w_qkv': w(ks[9], (nB, D_MODEL, 3 * SB_HEADS * SB_HEAD_DIM), D_MODEL),
        'sb_w_o': w(ks[10], (nB, SB_HEADS * SB_HEAD_DIM, D_MODEL), SB_HEADS * SB_HEAD_DIM),
        'mla_w_down': w(ks[11], (nC, D_MODEL, mla_down), D_MODEL),
        'mla_q_norm': gain(ks[12], (nC, MLA_Q_RANK)),
        'mla_w_uq': w(ks[13], (nC, MLA_Q_RANK, MLA_HEADS * (MLA_NOPE_DIM + MLA_ROPE_DIM)), MLA_Q_RANK),
        'mla_kv_norm': gain(ks[14], (nC, MLA_KV_RANK)),
        'mla_w_ukv': w(ks[15], (nC, MLA_KV_RANK, MLA_HEADS * (MLA_NOPE_DIM + MLA_V_DIM)), MLA_KV_RANK),
        'mla_w_o': w(ks[16], (nC, MLA_HEADS * MLA_V_DIM, D_MODEL), MLA_HEADS * MLA_V_DIM),
        'ffn_w_gate': w(ks[17], (n_dense, D_MODEL, FFN_DIM), D_MODEL),
        'ffn_w_up': w(ks[18], (n_dense, D_MODEL, FFN_DIM), D_MODEL),
        'ffn_w_down': w(ks[19], (n_dense, FFN_DIM, D_MODEL), FFN_DIM),
        'moe_router': w(ks[20], (n_moe, D_MODEL, N_EXPERTS), D_MODEL),
        'moe_w_gate': w(ks[21], (n_moe, N_EXPERTS, D_MODEL, EXPERT_DIM), D_MODEL),
        'moe_w_up': w(ks[22], (n_moe, N_EXPERTS, D_MODEL, EXPERT_DIM), D_MODEL),
        'moe_w_down': w(ks[23], (n_moe, N_EXPERTS, EXPERT_DIM, D_MODEL), EXPERT_DIM),
    }


def reference(x, rel_bias, attn_norm, ffn_norm, final_norm,
              swa_w_qkv, swa_b_qkv, swa_sinks, swa_w_o,
              sb_w_qkv, sb_w_o,
              mla_w_down, mla_q_norm, mla_w_uq, mla_kv_norm, mla_w_ukv, mla_w_o,
              ffn_w_gate, ffn_w_up, ffn_w_down,
              moe_router, moe_w_gate, moe_w_up, moe_w_down):
    for i in range(DEPTH):
        h = rms_norm(x, attn_norm[i])
        mixer, j = i % N_MIXERS, i // N_MIXERS
        if mixer == 0:
            mix = swa_sink_attention(h, swa_w_qkv[j], swa_b_qkv[j], swa_sinks[j], swa_w_o[j], rel_bias)
        elif mixer == 1:
            mix = stick_breaking_attention(h, sb_w_qkv[j], sb_w_o[j])
        else:
            mix = mla_attention(h, mla_w_down[j], mla_q_norm[j], mla_w_uq[j],
                                mla_kv_norm[j], mla_w_ukv[j], mla_w_o[j])
        x = x + mix
        h = rms_norm(x, ffn_norm[i])
        f = i // 2
        if i % 2 == 0:
            x = x + swiglu(h, ffn_w_gate[f], ffn_w_up[f], ffn_w_down[f])
        else:
            x = x + moe_swiglu(h, moe_router[f], moe_w_gate[f], moe_w_up[f], moe_w_down[f])
    return rms_norm(x, final_norm)
```

```python
import functools
import math

import numpy as np
import jax
import jax.numpy as jnp
from jax import lax
from jax.experimental import pallas as pl
from jax.experimental.pallas import tpu as pltpu

F32 = jnp.float32
BF16 = jnp.bfloat16

N_MIXERS = 3
RMS_EPS = 1e-6
SWA_HEADS = 16
SWA_KV_HEADS = 2
HEAD_DIM = 64
SWA_BLOCK = 128
SWA_WINDOW = 128
REL_BUCKETS = 32
REL_MAX_DIST = 128
SB_HEADS = 16
MLA_HEADS = 16
MLA_NOPE_DIM = 64
MLA_ROPE_DIM = 32
MLA_Q_RANK = 384
MLA_KV_RANK = 256
ROPE_THETA = 10000.0
N_EXPERTS = 8

LANES = 128
NEG = -1e30
VMEM_LIMIT = 56 * 1024 * 1024

_NT = (((1,), (1,)), ((), ()))


def _dot(a, b):
    return jnp.dot(a, b, preferred_element_type=F32)


def _dot_nt(a, b):
    return lax.dot_general(a, b, _NT, preferred_element_type=F32)


def _rms(x, g):
    return x * lax.rsqrt(jnp.mean(x * x, axis=-1, keepdims=True) + RMS_EPS) * g


def _params(*sem):
    return pltpu.CompilerParams(dimension_semantics=sem, vmem_limit_bytes=VMEM_LIMIT)


def _tile(n, pref):
    return pref if n % pref == 0 else n


def _norm_proj_kernel(x_ref, g_ref, w_ref, b_ref, cs_ref, o_ref):
    h = _rms(x_ref[...], g_ref[...]).astype(BF16)
    y = (_dot(h, w_ref[...]) + b_ref[...]) * cs_ref[...]
    o_ref[...] = y.astype(o_ref.dtype)


def _norm_proj(x, g, w, b, colscale, name):
    t, d = x.shape
    n = w.shape[1]
    tm = _tile(t, 512)
    return pl.pallas_call(
        _norm_proj_kernel,
        grid=(t // tm,),
        in_specs=[pl.BlockSpec((tm, d), lambda i: (i, 0)),
                  pl.BlockSpec((1, d), lambda i: (0, 0)),
                  pl.BlockSpec((d, n), lambda i: (0, 0)),
                  pl.BlockSpec((1, n), lambda i: (0, 0)),
                  pl.BlockSpec((1, n), lambda i: (0, 0))],
        out_specs=pl.BlockSpec((tm, n), lambda i: (i, 0)),
        out_shape=jax.ShapeDtypeStruct((t, n), BF16),
        compiler_params=_params("parallel"),
        name=name,
    )(x, g.reshape(1, d), w, b.reshape(1, n), colscale.reshape(1, n))


def _out_proj_kernel(x_ref, o_ref, w_ref, y_ref):
    y_ref[...] = x_ref[...] + _dot(o_ref[...], w_ref[...])


def _out_proj(x, o, w, name):
    t, d = x.shape
    k = o.shape[1]
    tm = _tile(t, 512)
    return pl.pallas_call(
        _out_proj_kernel,
        grid=(t // tm,),
        in_specs=[pl.BlockSpec((tm, d), lambda i: (i, 0)),
                  pl.BlockSpec((tm, k), lambda i: (i, 0)),
                  pl.BlockSpec((k, d), lambda i: (0, 0))],
        out_specs=pl.BlockSpec((tm, d), lambda i: (i, 0)),
        out_shape=jax.ShapeDtypeStruct((t, d), F32),
        compiler_params=_params("parallel"),
        name=name,
    )(x, o, w)


def _t5_bucket_table():
    qi = np.arange(SWA_BLOCK)[:, None]
    kj = np.arange(2 * SWA_BLOCK)[None, :]
    dist = qi + SWA_BLOCK - kj
    d0 = np.maximum(dist, 0)
    max_exact = REL_BUCKETS // 2
    d = np.maximum(d0, 1).astype(np.float32)
    large = max_exact + (np.log(d / max_exact) / math.log(REL_MAX_DIST / max_exact)
                         * (REL_BUCKETS - max_exact)).astype(np.int32)
    large = np.minimum(large, REL_BUCKETS - 1)
    bucket = np.where(d0 < max_exact, d0, large)
    band = (dist >= 0) & (dist < SWA_WINDOW)
    return np.where(band, bucket, -1).astype(np.int32)


def _swa_bias_kernel(rel_ref, bucket_ref, o_ref):
    h = pl.program_id(0)
    bucket = bucket_ref[...]
    acc = jnp.full(bucket.shape, NEG, F32)
    for b in range(REL_BUCKETS):
        acc = jnp.where(bucket == b, rel_ref[b, h], acc)
    o_ref[0] = acc


def _swa_bias(rel_bias):
    bucket = jnp.asarray(_t5_bucket_table())
    q, k = bucket.shape
    return pl.pallas_call(
        _swa_bias_kernel,
        grid=(SWA_HEADS,),
        in_specs=[pl.BlockSpec(memory_space=pltpu.SMEM),
                  pl.BlockSpec((q, k), lambda h: (0, 0))],
        out_specs=pl.BlockSpec((1, q, k), lambda h: (h, 0, 0)),
        out_shape=jax.ShapeDtypeStruct((SWA_HEADS, q, k), F32),
        compiler_params=_params("arbitrary"),
        name="swa_bias",
    )(rel_bias, bucket)


def _swa_kernel(sink_ref, q_ref, kc_ref, kp_ref, vc_ref, vp_ref, bias_ref, o_ref, *, tq):
    i = pl.program_id(1)
    p = pl.program_id(2)
    blk = SWA_BLOCK
    lo_half = lax.broadcasted_iota(jnp.int32, (blk, LANES), 1) < HEAD_DIM
    in_prev = lax.broadcasted_iota(jnp.int32, (blk, 2 * blk), 1) < blk
    for sb in range(tq // blk):
        q = q_ref[sb * blk:(sb + 1) * blk, :]
        if sb == 0:
            kw = jnp.concatenate([kp_ref[...], kc_ref[0:blk, :]], axis=0)
            vw = jnp.concatenate([vp_ref[...], vc_ref[0:blk, :]], axis=0)
        else:
            kw = kc_ref[(sb - 1) * blk:(sb + 1) * blk, :]
            vw = vc_ref[(sb - 1) * blk:(sb + 1) * blk, :]
        outs = []
        for j in range(2):
            qm = jnp.where(lo_half if j == 0 else jnp.logical_not(lo_half), q, jnp.zeros_like(q))
            s = _dot_nt(qm, kw) + bias_ref[j]
            if sb == 0:
                s = jnp.where(jnp.logical_and(in_prev, i == 0), NEG, s)
            sink = sink_ref[2 * p + j]
            m = jnp.maximum(jnp.max(s, axis=1, keepdims=True), sink)
            e = jnp.exp(s - m)
            den = jnp.sum(e, axis=1, keepdims=True) + jnp.exp(sink - m)
            outs.append(_dot(e.astype(BF16), vw) / den)
        o_ref[sb * blk:(sb + 1) * blk, :] = jnp.where(lo_half, outs[0], outs[1]).astype(o_ref.dtype)


def _swa_attention(qkv, bias, sinks, batch, seq):
    t = qkv.shape[0]
    tq = _tile(seq, 512)
    nt = seq // tq
    pairs = SWA_HEADS // 2
    group = pairs // SWA_KV_HEADS
    kcol = pairs
    vcol = pairs + SWA_KV_HEADS
    per = tq // SWA_BLOCK

    def prev_idx(b, i, p):
        return jnp.maximum(b * (seq // SWA_BLOCK) + i * per - 1, 0)

    return pl.pallas_call(
        functools.partial(_swa_kernel, tq=tq),
        grid=(batch, nt, pairs),
        in_specs=[pl.BlockSpec(memory_space=pltpu.SMEM),
                  pl.BlockSpec((tq, LANES), lambda b, i, p: (b * nt + i, p)),
                  pl.BlockSpec((tq, LANES), lambda b, i, p: (b * nt + i, kcol + p // group)),
                  pl.BlockSpec((SWA_BLOCK, LANES), lambda b, i, p: (prev_idx(b, i, p), kcol + p // group)),
                  pl.BlockSpec((tq, LANES), lambda b, i, p: (b * nt + i, vcol + p // group)),
                  pl.BlockSpec((SWA_BLOCK, LANES), lambda b, i, p: (prev_idx(b, i, p), vcol + p // group)),
                  pl.BlockSpec((2, SWA_BLOCK, 2 * SWA_BLOCK), lambda b, i, p: (p, 0, 0))],
        out_specs=pl.BlockSpec((tq, LANES), lambda b, i, p: (b * nt + i, p)),
        out_shape=jax.ShapeDtypeStruct((t, SWA_HEADS * HEAD_DIM), BF16),
        compiler_params=_params("parallel", "parallel", "arbitrary"),
        name="swa_attention",
    )(sinks, qkv, qkv, qkv, qkv, qkv, bias)


def _swa_weights(w_qkv, b_qkv):
    nq = SWA_HEADS * HEAD_DIM
    kv = SWA_KV_HEADS * HEAD_DIM
    dup = np.concatenate([np.tile(np.arange(HEAD_DIM), 2) + h * HEAD_DIM for h in range(SWA_KV_HEADS)])
    cols = np.concatenate([np.arange(nq), nq + dup, nq + kv + dup])
    scale = np.concatenate([np.full(nq, HEAD_DIM ** -0.5), np.ones(2 * dup.size)]).astype(np.float32)
    return w_qkv[:, cols].astype(BF16), b_qkv[cols], jnp.asarray(scale)


def _sb_kernel(q_ref, k_ref, v_ref, tri_ref, o_ref, *, tq):
    i = pl.program_id(2)
    q = q_ref[...]
    lo_half = lax.broadcasted_iota(jnp.int32, (tq, LANES), 1) < HEAD_DIM
    zero = jnp.zeros_like(q)
    qms = (jnp.where(lo_half, q, zero), jnp.where(lo_half, zero, q))
    tri = tri_ref[...]
    before = (lax.broadcasted_iota(jnp.int32, (tq, tq), 1)
              < lax.broadcasted_iota(jnp.int32, (tq, tq), 0))

    def tile(j, state, masked):
        start = pl.multiple_of(j * tq, tq)
        kt = k_ref[pl.ds(start, tq), :]
        vt = v_ref[pl.ds(start, tq), :]
        new = []
        for h in range(2):
            c, acc = state[h]
            z = _dot_nt(qms[h], kt)
            lf = -(jnp.maximum(z, 0.0) + jnp.log(1.0 + jnp.exp(-jnp.abs(z))))
            if masked:
                lf = jnp.where(before, lf, 0.0)
            hi = lf.astype(BF16)
            lo = (lf - hi.astype(F32)).astype(BF16)
            r = _dot(hi, tri) + _dot(lo, tri)
            la = z + r + c
            if masked:
                la = jnp.where(before, la, NEG)
            acc = acc + _dot(jnp.exp(la).astype(BF16), vt)
            new.append((c + r[:, 0:1], acc))
        return tuple(new)

    init = tuple((jnp.zeros((tq, 1), F32), jnp.zeros((tq, LANES), F32)) for _ in range(2))
    state = tile(i, init, True)
    state = lax.fori_loop(0, i, lambda n, st: tile(i - 1 - n, st, False), state)
    o_ref[...] = jnp.where(lo_half, state[0][1], state[1][1]).astype(o_ref.dtype)


def _sb_attention(qkv, batch, seq):
    t = qkv.shape[0]
    tq = _tile(seq, 256)
    nq = seq // tq
    pairs = SB_HEADS // 2
    tri = jnp.asarray(np.tril(np.ones((tq, tq), np.float32)), BF16)
    return pl.pallas_call(
        functools.partial(_sb_kernel, tq=tq),
        grid=(batch, pairs, nq),
        in_specs=[pl.BlockSpec((tq, LANES), lambda b, p, i: (b * nq + i, p)),
                  pl.BlockSpec((seq, LANES), lambda b, p, i: (b, pairs + p)),
                  pl.BlockSpec((seq, LANES), lambda b, p, i: (b, 2 * pairs + p)),
                  pl.BlockSpec((tq, tq), lambda b, p, i: (0, 0))],
        out_specs=pl.BlockSpec((tq, LANES), lambda b, p, i: (b * nq + i, p)),
        out_shape=jax.ShapeDtypeStruct((t, SB_HEADS * HEAD_DIM), BF16),
        compiler_params=_params("parallel", "parallel", "arbitrary"),
        name="sb_attention",
    )(qkv, qkv, qkv, tri)


MLA_QN = MLA_HEADS * MLA_NOPE_DIM
MLA_QR = MLA_HEADS * MLA_ROPE_DIM
MLA_OUT = 2 * MLA_QN + MLA_QR + MLA_QN + LANES


def _mla_proj_kernel(x_ref, g_ref, wd_ref, qg_ref, wuq_ref, kvg_ref, wukv_ref, cs_ref, o_ref, *, scale):
    h = _rms(x_ref[...], g_ref[...]).astype(BF16)
    c = _dot(h, wd_ref[...])
    cos = cs_ref[:, :LANES]
    sin = cs_ref[:, LANES:]
    cq = _rms(c[:, :MLA_Q_RANK], qg_ref[...]).astype(BF16)
    q = _dot(cq, wuq_ref[...])
    o_ref[:, :MLA_QN] = (q[:, :MLA_QN] * scale).astype(o_ref.dtype)
    for m in range(MLA_QR // LANES):
        a = MLA_QN + m * LANES
        rot = q[:, a:a + LANES] * cos + q[:, a + MLA_QR:a + MLA_QR + LANES] * sin
        o_ref[:, a:a + LANES] = (rot * scale).astype(o_ref.dtype)
    kv0 = MLA_Q_RANK + MLA_KV_RANK
    ckv = _rms(c[:, MLA_Q_RANK:kv0], kvg_ref[...]).astype(BF16)
    o_ref[:, MLA_QN + MLA_QR:MLA_QN + MLA_QR + 2 * MLA_QN] = _dot(ckv, wukv_ref[...]).astype(o_ref.dtype)
    kr = c[:, kv0:kv0 + LANES] * cos + c[:, kv0 + LANES:kv0 + 2 * LANES] * sin
    o_ref[:, MLA_OUT - LANES:] = kr.astype(o_ref.dtype)


def _mla_rope_layout():
    half = MLA_ROPE_DIM // 2
    per = LANES // MLA_ROPE_DIM
    dq = MLA_NOPE_DIM + MLA_ROPE_DIM
    q_nope = np.concatenate([h * dq + np.arange(MLA_NOPE_DIM) for h in range(MLA_HEADS)])
    q_rope = np.zeros(MLA_QR, np.int64)
    q_swap = np.zeros(MLA_QR, np.int64)
    for h in range(MLA_HEADS):
        m, r = divmod(h, per)
        x1 = h * dq + MLA_NOPE_DIM + np.arange(half)
        x2 = x1 + half
        first = m * LANES + r * half + np.arange(half)
        second = first + LANES // 2
        q_rope[first], q_rope[second] = x1, x2
        q_swap[first], q_swap[second] = x2, x1
    k_rope = np.zeros(LANES, np.int64)
    k_swap = np.zeros(LANES, np.int64)
    base = MLA_Q_RANK + MLA_KV_RANK
    for r in range(per):
        first = r * half + np.arange(half)
        second = first + LANES // 2
        k_rope[first], k_rope[second] = base + np.arange(half), base + half + np.arange(half)
        k_swap[first], k_swap[second] = base + half + np.arange(half), base + np.arange(half)
    dkv = MLA_NOPE_DIM + HEAD_DIM
    k_nope = np.concatenate([h * dkv + np.arange(MLA_NOPE_DIM) for h in range(MLA_HEADS)])
    v = k_nope + MLA_NOPE_DIM
    down = np.concatenate([np.arange(base), k_rope, k_swap])
    return np.concatenate([q_nope, q_rope, q_swap]), np.concatenate([k_nope, v]), down


def _mla_rope_tables(seq):
    half = MLA_ROPE_DIM // 2
    inv = ROPE_THETA ** (-jnp.arange(0, MLA_ROPE_DIM, 2, dtype=F32) / MLA_ROPE_DIM)
    ang = jnp.arange(seq, dtype=F32)[:, None] * inv[None, :]
    cos, sin = jnp.cos(ang), jnp.sin(ang)
    reps = LANES // 2 // half
    cos_t = jnp.tile(cos, (1, 2 * reps))
    sin_t = jnp.concatenate([jnp.tile(-sin, (1, reps)), jnp.tile(sin, (1, reps))], axis=1)
    return jnp.concatenate([cos_t, sin_t], axis=1)


def _mla_proj(x, g, w_down, q_norm, w_uq, kv_norm, w_ukv, seq):
    t, d = x.shape
    tm = _tile(seq, 512)
    ns = seq // tm
    uq_cols, ukv_cols, down_cols = _mla_rope_layout()
    wd = w_down[:, down_cols].astype(BF16)
    wuq = w_uq[:, uq_cols].astype(BF16)
    wukv = w_ukv[:, ukv_cols].astype(BF16)
    cs = _mla_rope_tables(seq)
    scale = (MLA_NOPE_DIM + MLA_ROPE_DIM) ** -0.5
    full = lambda a: pl.BlockSpec(a.shape, lambda i: (0, 0))
    g2, qg, kvg = g.reshape(1, d), q_norm.reshape(1, -1), kv_norm.reshape(1, -1)
    return pl.pallas_call(
        functools.partial(_mla_proj_kernel, scale=scale),
        grid=(t // tm,),
        in_specs=[pl.BlockSpec((tm, d), lambda i: (i, 0)), full(g2), full(wd), full(qg), full(wuq),
                  full(kvg), full(wukv), pl.BlockSpec((tm, 2 * LANES), lambda i: (i % ns, 0))],
        out_specs=pl.BlockSpec((tm, MLA_OUT), lambda i: (i, 0)),
        out_shape=jax.ShapeDtypeStruct((t, MLA_OUT), BF16),
        compiler_params=_params("parallel"),
        name="mla_proj",
    )(x, g2, wd, qg, wuq, kvg, wukv, cs)


def _mla_kernel(qn_ref, qr_ref, kn_ref, kr_ref, v_ref, o_ref, *, tq):
    p = pl.program_id(1)
    i = pl.program_id(2)
    q = jnp.concatenate([qn_ref[...], qr_ref[...]], axis=1)
    lane = lax.broadcasted_iota(jnp.int32, (tq, 2 * LANES), 1)
    half = MLA_ROPE_DIM // 2
    zero = jnp.zeros_like(q)
    qms = []
    for j in range(2):
        r0 = LANES + (2 * (p % 2) + j) * half
        sel = jnp.logical_and(lane >= j * HEAD_DIM, lane < (j + 1) * HEAD_DIM)
        sel = jnp.logical_or(sel, jnp.logical_and(lane >= r0, lane < r0 + half))
        r1 = r0 + LANES // 2
        sel = jnp.logical_or(sel, jnp.logical_and(lane >= r1, lane < r1 + half))
        qms.append(jnp.where(sel, q, zero))
    causal = (lax.broadcasted_iota(jnp.int32, (tq, tq), 1)
              <= lax.broadcasted_iota(jnp.int32, (tq, tq), 0))
    lo_half = lax.broadcasted_iota(jnp.int32, (tq, LANES), 1) < HEAD_DIM

    def tile(j, state, masked):
        start = pl.multiple_of(j * tq, tq)
        kt = jnp.concatenate([kn_ref[pl.ds(start, tq), :], kr_ref[pl.ds(start, tq), :]], axis=1)
        vt = v_ref[pl.ds(start, tq), :]
        new = []
        for h in range(2):
            m, l, acc = state[h]
            s = _dot_nt(qms[h], kt)
            if masked:
                s = jnp.where(causal, s, NEG)
            m_new = jnp.maximum(m, jnp.max(s, axis=1, keepdims=True))
            alpha = jnp.exp(m - m_new)
            e = jnp.exp(s - m_new)
            l = alpha * l + jnp.sum(e, axis=1, keepdims=True)
            acc = alpha * acc + _dot(e.astype(BF16), vt)
            new.append((m_new, l, acc))
        return tuple(new)

    init = tuple((jnp.full((tq, 1), NEG, F32), jnp.zeros((tq, 1), F32), jnp.zeros((tq, LANES), F32))
                 for _ in range(2))
    state = tile(i, init, True)
    state = lax.fori_loop(0, i, lambda n, st: tile(n, st, False), state)
    outs = [acc / l for (_, l, acc) in state]
    o_ref[...] = jnp.where(lo_half, outs[0], outs[1]).astype(o_ref.dtype)


def _mla_attention(a, batch, seq):
    t = a.shape[0]
    tq = _tile(seq, 256)
    nq = seq // tq
    pairs = MLA_HEADS // 2
    qr0 = MLA_QN // LANES
    kn0 = (MLA_QN + MLA_QR) // LANES
    v0 = kn0 + pairs
    kr0 = v0 + pairs
    return pl.pallas_call(
        functools.partial(_mla_kernel, tq=tq),
        grid=(batch, pairs, nq),
        in_specs=[pl.BlockSpec((tq, LANES), lambda b, p, i: (b * nq + i, p)),
                  pl.BlockSpec((tq, LANES), lambda b, p, i: (b * nq + i, qr0 + p // 2)),
                  pl.BlockSpec((seq, LANES), lambda b, p, i: (b, kn0 + p)),
                  pl.BlockSpec((seq, LANES), lambda b, p, i: (b, kr0)),
                  pl.BlockSpec((seq, LANES), lambda b, p, i: (b, v0 + p))],
        out_specs=pl.BlockSpec((tq, LANES), lambda b, p, i: (b * nq + i, p)),
        out_shape=jax.ShapeDtypeStruct((t, MLA_HEADS * HEAD_DIM), BF16),
        compiler_params=_params("parallel", "parallel", "arbitrary"),
        name="mla_attention",
    )(a, a, a, a, a)


def _swiglu(h, wg, wu):
    g = _dot(h, wg)
    u = _dot(h, wu)
    return (g * jax.nn.sigmoid(g) * u).astype(BF16)


def _ffn_kernel(x_ref, g_ref, wg_ref, wu_ref, wd_ref, o_ref, h_ref):
    f = pl.program_id(1)

    @pl.when(f == 0)
    def _():
        x = x_ref[...]
        h_ref[...] = _rms(x, g_ref[...]).astype(BF16)
        o_ref[...] = x

    o_ref[...] += _dot(_swiglu(h_ref[...], wg_ref[...], wu_ref[...]), wd_ref[...])


def _ffn(x, g, wg, wu, wd):
    t, d = x.shape
    fdim = wg.shape[1]
    tm = _tile(t, 512)
    tf = _tile(fdim, 1408)
    return pl.pallas_call(
        _ffn_kernel,
        grid=(t // tm, fdim // tf),
        in_specs=[pl.BlockSpec((tm, d), lambda i, f: (i, 0)),
                  pl.BlockSpec((1, d), lambda i, f: (0, 0)),
                  pl.BlockSpec((d, tf), lambda i, f: (0, f)),
                  pl.BlockSpec((d, tf), lambda i, f: (0, f)),
                  pl.BlockSpec((tf, d), lambda i, f: (f, 0))],
        out_specs=pl.BlockSpec((tm, d), lambda i, f: (i, 0)),
        out_shape=jax.ShapeDtypeStruct((t, d), F32),
        scratch_shapes=[pltpu.VMEM((tm, d), BF16)],
        compiler_params=_params("parallel", "arbitrary"),
        name="dense_ffn",
    )(x, g.reshape(1, d), wg.astype(BF16), wu.astype(BF16), wd.astype(BF16))


MOE_TILE = 512
META_ROWS = 8


def _router_kernel(x_ref, g_ref, rt_ref, su_ref, eye_ref, hb_ref, meta_ref, metac_ref, cnt_ref, carry_ref):
    @pl.when(pl.program_id(0) == 0)
    def _():
        carry_ref[...] = jnp.zeros_like(carry_ref)

    h = _rms(x_ref[...], g_ref[...])
    hb_ref[...] = h.astype(BF16)
    logits = lax.dot_general(rt_ref[...], h, _NT, precision=lax.Precision.HIGHEST,
                             preferred_element_type=F32)
    ne, ct = logits.shape
    eio = lax.broadcasted_iota(jnp.int32, (ne, ct), 0).astype(F32)
    m1 = jnp.max(logits, axis=0, keepdims=True)
    i1 = jnp.min(jnp.where(logits == m1, eio, float(ne)), axis=0, keepdims=True)
    rest = jnp.where(eio == i1, -jnp.inf, logits)
    m2 = jnp.max(rest, axis=0, keepdims=True)
    i2 = jnp.min(jnp.where(rest == m2, eio, float(ne)), axis=0, keepdims=True)
    e2 = jnp.exp(m2 - m1)
    g1 = 1.0 / (1.0 + e2)
    g2 = e2 / (1.0 + e2)
    oh1 = (eio == i1).astype(F32)
    oh2 = (eio == i2).astype(F32)
    both = oh1 + oh2
    seen = carry_ref[:, 0:1] + _dot(both.astype(BF16), su_ref[...])
    r1 = jnp.sum(oh1 * seen, axis=0, keepdims=True)
    r2 = jnp.sum(oh2 * seen, axis=0, keepdims=True)
    meta = jnp.zeros((META_ROWS, ct), F32)
    for row, val in enumerate((i1, i2, r1, r2, g1, g2)):
        meta = jnp.where(eio == float(row), val, meta)
    meta_ref[...] = meta
    metac_ref[...] = lax.dot_general(eye_ref[...], meta, _NT, precision=lax.Precision.HIGHEST,
                                     preferred_element_type=F32)
    carry_ref[...] = carry_ref[...] + jnp.sum(both, axis=1, keepdims=True)
    cnt_ref[0] = carry_ref[...]


def _router(x, g, router):
    t, d = x.shape
    ct = _tile(t, MOE_TILE)
    nc = t // ct
    su = jnp.asarray(np.triu(np.ones((ct, ct), np.float32), 1), BF16)
    eye = jnp.eye(ct, dtype=F32)
    return pl.pallas_call(
        _router_kernel,
        grid=(nc,),
        in_specs=[pl.BlockSpec((ct, d), lambda i: (i, 0)),
                  pl.BlockSpec((1, d), lambda i: (0, 0)),
                  pl.BlockSpec((N_EXPERTS, d), lambda i: (0, 0)),
                  pl.BlockSpec((ct, ct), lambda i: (0, 0)),
                  pl.BlockSpec((ct, ct), lambda i: (0, 0))],
        out_specs=[pl.BlockSpec((ct, d), lambda i: (i, 0)),
                   pl.BlockSpec((META_ROWS, ct), lambda i: (0, i)),
                   pl.BlockSpec((ct, META_ROWS), lambda i: (i, 0)),
                   pl.BlockSpec((1, N_EXPERTS, LANES), lambda i: (i, 0, 0))],
        out_shape=[jax.ShapeDtypeStruct((t, d), BF16),
                   jax.ShapeDtypeStruct((META_ROWS, t), F32),
                   jax.ShapeDtypeStruct((t, META_ROWS), F32),
                   jax.ShapeDtypeStruct((nc, N_EXPERTS, LANES), F32)],
        scratch_shapes=[pltpu.VMEM((N_EXPERTS, LANES), F32)],
        compiler_params=_params("arbitrary"),
        name="moe_router",
    )(x, g.reshape(1, d), router.T, su, eye)


def _moe_plan(cnt_after, t, ct):
    nc = t // ct
    rows = 2 * t + N_EXPERTS * ct
    nrt = rows // ct
    counts = cnt_after[:, :, 0].astype(jnp.int32)
    total = counts[-1]
    padded = (total + ct - 1) // ct * ct
    off_end = jnp.cumsum(padded)
    off = off_end - padded
    before = jnp.concatenate([jnp.zeros((1, N_EXPERTS), jnp.int32), counts[:-1]], axis=0)
    chunk_start = (off[:, None] + before.T).reshape(-1)
    tile_start = jnp.arange(nrt, dtype=jnp.int32) * ct
    starts = jnp.sort(jnp.concatenate([chunk_start, tile_start]))
    ends = jnp.concatenate([starts[1:], jnp.array([rows], jnp.int32)])
    item_r = jnp.minimum(starts // ct, nrt - 1)
    item_c = (jnp.searchsorted(chunk_start, starts, side="right") - 1) % nc
    first = lambda key: jnp.concatenate([jnp.ones((1,), jnp.int32), (key[1:] != key[:-1]).astype(jnp.int32)])
    dispatch = (item_r, item_c, starts, ends, first(item_r))
    order = jnp.argsort(item_c * (rows + 1) + starts)
    c2 = item_c[order]
    combine = (c2, item_r[order], starts[order], ends[order], first(c2))
    tile_expert = jnp.minimum(jnp.searchsorted(off_end, tile_start, side="right"), N_EXPERTS - 1)
    tile_valid = (tile_start < off_end[-1]).astype(jnp.int32)
    return rows, off.astype(jnp.int32), dispatch, combine, tile_expert.astype(jnp.int32), tile_valid


def _positions(expert, rank, off_ref):
    base = jnp.zeros_like(rank)
    for e in range(N_EXPERTS):
        base = jnp.where(expert == float(e), off_ref[e].astype(F32), base)
    return base + rank


def _dispatch_kernel(r_ref, c_ref, s_ref, e_ref, first_ref, off_ref, meta_ref, h_ref, xs_ref, gate_ref):
    n = pl.program_id(0)
    rt, ct = xs_ref.shape[0], h_ref.shape[0]

    @pl.when(first_ref[n] == 1)
    def _():
        xs_ref[...] = jnp.zeros_like(xs_ref)
        gate_ref[...] = jnp.zeros_like(gate_ref)

    meta = meta_ref[...]
    row = (lax.broadcasted_iota(jnp.int32, (rt, ct), 0) + r_ref[n] * rt)
    live = jnp.logical_and(row >= s_ref[n], row < e_ref[n])
    rowf = row.astype(F32)
    sel0 = jnp.logical_and(_positions(meta[0:1], meta[2:3], off_ref) == rowf, live)
    sel1 = jnp.logical_and(_positions(meta[1:2], meta[3:4], off_ref) == rowf, live)
    sel = jnp.logical_or(sel0, sel1).astype(F32).astype(BF16)
    xs_ref[...] += _dot(sel, h_ref[...]).astype(xs_ref.dtype)
    gate = jnp.where(sel0, meta[4:5], 0.0) + jnp.where(sel1, meta[5:6], 0.0)
    gate_ref[...] += jnp.sum(gate, axis=1, keepdims=True)


def _dispatch(plan, off, meta, hb, rows, ct):
    t, d = hb.shape
    n_items = plan[0].shape[0]
    return pl.pallas_call(
        _dispatch_kernel,
        grid_spec=pltpu.PrefetchScalarGridSpec(
            num_scalar_prefetch=6,
            grid=(n_items,),
            in_specs=[pl.BlockSpec((META_ROWS, ct), lambda n, r, c, *_: (0, c[n])),
                      pl.BlockSpec((ct, d), lambda n, r, c, *_: (c[n], 0))],
            out_specs=[pl.BlockSpec((ct, d), lambda n, r, c, *_: (r[n], 0)),
                       pl.BlockSpec((ct, 1), lambda n, r, c, *_: (r[n], 0))]),
        out_shape=[jax.ShapeDtypeStruct((rows, d), BF16),
                   jax.ShapeDtypeStruct((rows, 1), F32)],
        compiler_params=_params("arbitrary"),
        name="moe_dispatch",
    )(*plan, off, meta, hb)


def _expert_kernel(te_ref, tv_ref, xs_ref, gate_ref, wg_ref, wu_ref, wd_ref, y_ref, acc_ref):
    r = pl.program_id(0)
    f = pl.program_id(1)

    @pl.when(f == 0)
    def _():
        acc_ref[...] = jnp.zeros_like(acc_ref)

    @pl.when(tv_ref[r] == 1)
    def _():
        acc_ref[...] += _dot(_swiglu(xs_ref[...], wg_ref[0], wu_ref[0]), wd_ref[0])

    @pl.when(f == pl.num_programs(1) - 1)
    def _():
        y_ref[...] = (acc_ref[...] * gate_ref[...]).astype(y_ref.dtype)


def _experts(tile_expert, tile_valid, xs, gate, wg, wu, wd, ct):
    rows, d = xs.shape
    fdim = wg.shape[2]
    tf = _tile(fdim, 896)
    nf = fdim // tf

    def wcol(r, f, te, tv):
        return (te[r], 0, jnp.where(tv[r] == 1, f, nf - 1))

    def wrow(r, f, te, tv):
        return (te[r], jnp.where(tv[r] == 1, f, nf - 1), 0)

    return pl.pallas_call(
        _expert_kernel,
        grid_spec=pltpu.PrefetchScalarGridSpec(
            num_scalar_prefetch=2,
            grid=(rows // ct, nf),
            in_specs=[pl.BlockSpec((ct, d), lambda r, f, te, tv: (r, 0)),
                      pl.BlockSpec((ct, 1), lambda r, f, te, tv: (r, 0)),
                      pl.BlockSpec((1, d, tf), wcol),
                      pl.BlockSpec((1, d, tf), wcol),
                      pl.BlockSpec((1, tf, d), wrow)],
            out_specs=pl.BlockSpec((ct, d), lambda r, f, te, tv: (r, 0)),
            scratch_shapes=[pltpu.VMEM((ct, d), F32)]),
        out_shape=jax.ShapeDtypeStruct((rows, d), BF16),
        compiler_params=_params("arbitrary", "arbitrary"),
        name="moe_experts",
    )(tile_expert, tile_valid, xs, gate, wg, wu, wd)


def _combine_kernel(c_ref, r_ref, s_ref, e_ref, first_ref, off_ref, x_ref, metac_ref, y_ref, o_ref):
    n = pl.program_id(0)
    ct, rt = x_ref.shape[0], y_ref.shape[0]

    @pl.when(first_ref[n] == 1)
    def _():
        o_ref[...] = x_ref[...]

    metac = metac_ref[...]
    row = lax.broadcasted_iota(jnp.int32, (ct, rt), 1) + r_ref[n] * rt
    live = jnp.logical_and(row >= s_ref[n], row < e_ref[n])
    rowf = row.astype(F32)
    sel0 = _positions(metac[:, 0:1], metac[:, 2:3], off_ref) == rowf
    sel1 = _positions(metac[:, 1:2], metac[:, 3:4], off_ref) == rowf
    sel = jnp.logical_and(jnp.logical_or(sel0, sel1), live).astype(F32).astype(BF16)
    o_ref[...] += _dot(sel, y_ref[...])


def _combine(plan, off, x, metac, y, ct):
    t, d = x.shape
    n_items = plan[0].shape[0]
    return pl.pallas_call(
        _combine_kernel,
        grid_spec=pltpu.PrefetchScalarGridSpec(
            num_scalar_prefetch=6,
            grid=(n_items,),
            in_specs=[pl.BlockSpec((ct, d), lambda n, c, r, *_: (c[n], 0)),
                      pl.BlockSpec((ct, META_ROWS), lambda n, c, r, *_: (c[n], 0)),
                      pl.BlockSpec((ct, d), lambda n, c, r, *_: (r[n], 0))],
            out_specs=pl.BlockSpec((ct, d), lambda n, c, r, *_: (c[n], 0))),
        out_shape=jax.ShapeDtypeStruct((t, d), F32),
        compiler_params=_params("arbitrary"),
        name="moe_combine",
    )(*plan, off, x, metac, y)


def _moe(x, g, router, wg, wu, wd):
    t = x.shape[0]
    ct = _tile(t, MOE_TILE)
    hb, meta, metac, cnt = _router(x, g, router)
    rows, off, dispatch, combine, tile_expert, tile_valid = _moe_plan(cnt, t, ct)
    xs, gate = _dispatch(dispatch, off, meta, hb, rows, ct)
    y = _experts(tile_expert, tile_valid, xs, gate, wg.astype(BF16), wu.astype(BF16), wd.astype(BF16), ct)
    return _combine(combine, off, x, metac, y, ct)


def _final_norm_kernel(x_ref, g_ref, o_ref):
    o_ref[...] = _rms(x_ref[...], g_ref[...])


def _final_norm(x, g):
    t, d = x.shape
    tm = _tile(t, 1024)
    return pl.pallas_call(
        _final_norm_kernel,
        grid=(t // tm,),
        in_specs=[pl.BlockSpec((tm, d), lambda i: (i, 0)), pl.BlockSpec((1, d), lambda i: (0, 0))],
        out_specs=pl.BlockSpec((tm, d), lambda i: (i, 0)),
        out_shape=jax.ShapeDtypeStruct((t, d), F32),
        compiler_params=_params("parallel"),
        name="final_norm",
    )(x, g.reshape(1, d))


def kernel(x, rel_bias, attn_norm, ffn_norm, final_norm, swa_w_qkv, swa_b_qkv, swa_sinks, swa_w_o, sb_w_qkv, sb_w_o, mla_w_down, mla_q_norm, mla_w_uq, mla_kv_norm, mla_w_ukv, mla_w_o, ffn_w_gate, ffn_w_up, ffn_w_down, moe_router, moe_w_gate, moe_w_up, moe_w_down):
    batch, seq, d = x.shape
    depth = attn_norm.shape[0]
    xt = x.reshape(batch * seq, d)
    swa_bias = _swa_bias(rel_bias)
    for i in range(depth):
        mixer, j = i % N_MIXERS, i // N_MIXERS
        if mixer == 0:
            w, b, cs = _swa_weights(swa_w_qkv[j], swa_b_qkv[j])
            qkv = _norm_proj(xt, attn_norm[i], w, b, cs, "swa_proj")
            o = _swa_attention(qkv, swa_bias, swa_sinks[j], batch, seq)
            xt = _out_proj(xt, o, swa_w_o[j].astype(BF16), "swa_out")
        elif mixer == 1:
            n = sb_w_qkv.shape[2]
            cs = jnp.concatenate([jnp.full((n // 3,), HEAD_DIM ** -0.5, F32), jnp.ones((2 * n // 3,), F32)])
            qkv = _norm_proj(xt, attn_norm[i], sb_w_qkv[j].astype(BF16), jnp.zeros((n,), F32), cs, "sb_proj")
            o = _sb_attention(qkv, batch, seq)
            xt = _out_proj(xt, o, sb_w_o[j].astype(BF16), "sb_out")
        else:
            a = _mla_proj(xt, attn_norm[i], mla_w_down[j], mla_q_norm[j], mla_w_uq[j],
                          mla_kv_norm[j], mla_w_ukv[j], seq)
            o = _mla_attention(a, batch, seq)
            xt = _out_proj(xt, o, mla_w_o[j].astype(BF16), "mla_out")
        f = i // 2
        if i % 2 == 0:
            xt = _ffn(xt, ffn_norm[i], ffn_w_gate[f], ffn_w_up[f], ffn_w_down[f])
        else:
            xt = _moe(xt, ffn_norm[i], moe_router[f], moe_w_gate[f], moe_w_up[f], moe_w_down[f])
    return _final_norm(xt, final_norm).reshape(batch, seq, d)
```

```python
import functools
import math

import numpy as np
import jax
import jax.numpy as jnp
from jax import lax
from jax.experimental import pallas as pl
from jax.experimental.pallas import tpu as pltpu

F32 = jnp.float32
BF16 = jnp.bfloat16

N_MIXERS = 3
RMS_EPS = 1e-6
SWA_HEADS = 16
SWA_KV_HEADS = 2
HEAD_DIM = 64
SWA_BLOCK = 128
SWA_WINDOW = 128
REL_BUCKETS = 32
REL_MAX_DIST = 128
SB_HEADS = 16
MLA_HEADS = 16
MLA_NOPE_DIM = 64
MLA_ROPE_DIM = 32
MLA_Q_RANK = 384
MLA_KV_RANK = 256
ROPE_THETA = 10000.0
N_EXPERTS = 8

LANES = 128
NEG = -1e30
VMEM_LIMIT = 56 * 1024 * 1024

_NT = (((1,), (1,)), ((), ()))


def _dot(a, b):
    return jnp.dot(a, b, preferred_element_type=F32)


def _dot_nt(a, b):
    return lax.dot_general(a, b, _NT, preferred_element_type=F32)


def _rms(x, g):
    return x * lax.rsqrt(jnp.mean(x * x, axis=-1, keepdims=True) + RMS_EPS) * g


def _params(*sem):
    return pltpu.CompilerParams(dimension_semantics=sem, vmem_limit_bytes=VMEM_LIMIT)


def _tile(n, pref):
    return pref if n % pref == 0 else n


def _norm_proj_kernel(x_ref, g_ref, w_ref, b_ref, cs_ref, o_ref):
    h = _rms(x_ref[...], g_ref[...]).astype(BF16)
    y = (_dot(h, w_ref[...]) + b_ref[...]) * cs_ref[...]
    o_ref[...] = y.astype(o_ref.dtype)


def _norm_proj(x, g, w, b, colscale, name):
    t, d = x.shape
    n = w.shape[1]
    tm = _tile(t, 512)
    return pl.pallas_call(
        _norm_proj_kernel,
        grid=(t // tm,),
        in_specs=[pl.BlockSpec((tm, d), lambda i: (i, 0)),
                  pl.BlockSpec((1, d), lambda i: (0, 0)),
                  pl.BlockSpec((d, n), lambda i: (0, 0)),
                  pl.BlockSpec((1, n), lambda i: (0, 0)),
                  pl.BlockSpec((1, n), lambda i: (0, 0))],
        out_specs=pl.BlockSpec((tm, n), lambda i: (i, 0)),
        out_shape=jax.ShapeDtypeStruct((t, n), BF16),
        compiler_params=_params("parallel"),
        name=name,
    )(x, g.reshape(1, d), w, b.reshape(1, n), colscale.reshape(1, n))


ATT_TQ = 512
ATT_TK = 256


def _store_vt(vt_ref, v, tk):
    for s in range(v.shape[0] // tk):
        vt_ref[s] = v[s * tk:(s + 1) * tk, :].T.astype(vt_ref.dtype)


def _norm_proj_vt_kernel(x_ref, g_ref, w_ref, cs_ref, o_ref, vt_ref, *, tk):
    h = _rms(x_ref[...], g_ref[...]).astype(BF16)
    y = _dot(h, w_ref[...])
    n = o_ref.shape[1]
    o_ref[...] = (y[:, :n] * cs_ref[...]).astype(o_ref.dtype)
    _store_vt(vt_ref, y[:, n:], tk)


def _norm_proj_vt(x, g, w, colscale, nv, seq, name):
    t, d = x.shape
    n = w.shape[1] - nv
    tm = _tile(seq, 512)
    tk = _tile(seq, ATT_TK)
    per = tm // tk
    return pl.pallas_call(
        functools.partial(_norm_proj_vt_kernel, tk=tk),
        grid=(t // tm,),
        in_specs=[pl.BlockSpec((tm, d), lambda i: (i, 0)),
                  pl.BlockSpec((1, d), lambda i: (0, 0)),
                  pl.BlockSpec((d, n + nv), lambda i: (0, 0)),
                  pl.BlockSpec((1, n), lambda i: (0, 0))],
        out_specs=[pl.BlockSpec((tm, n), lambda i: (i, 0)),
                   pl.BlockSpec((per, nv, tk), lambda i: (i, 0, 0))],
        out_shape=[jax.ShapeDtypeStruct((t, n), BF16),
                   jax.ShapeDtypeStruct((t // tk, nv, tk), BF16)],
        compiler_params=_params("parallel"),
        name=name,
    )(x, g.reshape(1, d), w, colscale.reshape(1, n))


def _out_proj_kernel(x_ref, o_ref, w_ref, y_ref):
    y_ref[...] = x_ref[...] + _dot(o_ref[...], w_ref[...])


def _out_proj(x, o, w, name):
    t, d = x.shape
    k = o.shape[1]
    tm = _tile(t, 512)
    return pl.pallas_call(
        _out_proj_kernel,
        grid=(t // tm,),
        in_specs=[pl.BlockSpec((tm, d), lambda i: (i, 0)),
                  pl.BlockSpec((tm, k), lambda i: (i, 0)),
                  pl.BlockSpec((k, d), lambda i: (0, 0))],
        out_specs=pl.BlockSpec((tm, d), lambda i: (i, 0)),
        out_shape=jax.ShapeDtypeStruct((t, d), F32),
        compiler_params=_params("parallel"),
        name=name,
    )(x, o, w)


def _t5_bucket_table():
    qi = np.arange(SWA_BLOCK)[:, None]
    kj = np.arange(2 * SWA_BLOCK)[None, :]
    dist = qi + SWA_BLOCK - kj
    d0 = np.maximum(dist, 0)
    max_exact = REL_BUCKETS // 2
    d = np.maximum(d0, 1).astype(np.float32)
    large = max_exact + (np.log(d / max_exact) / math.log(REL_MAX_DIST / max_exact)
                         * (REL_BUCKETS - max_exact)).astype(np.int32)
    large = np.minimum(large, REL_BUCKETS - 1)
    bucket = np.where(d0 < max_exact, d0, large)
    band = (dist >= 0) & (dist < SWA_WINDOW)
    return np.where(band, bucket, -1).astype(np.int32)


def _swa_bias_kernel(rel_ref, bucket_ref, o_ref):
    h = pl.program_id(0)
    bucket = bucket_ref[...]
    acc = jnp.full(bucket.shape, NEG, F32)
    for b in range(REL_BUCKETS):
        acc = jnp.where(bucket == b, rel_ref[b, h], acc)
    o_ref[0] = acc


def _swa_bias(rel_bias):
    bucket = jnp.asarray(_t5_bucket_table())
    q, k = bucket.shape
    return pl.pallas_call(
        _swa_bias_kernel,
        grid=(SWA_HEADS,),
        in_specs=[pl.BlockSpec(memory_space=pltpu.SMEM),
                  pl.BlockSpec((q, k), lambda h: (0, 0))],
        out_specs=pl.BlockSpec((1, q, k), lambda h: (h, 0, 0)),
        out_shape=jax.ShapeDtypeStruct((SWA_HEADS, q, k), F32),
        compiler_params=_params("arbitrary"),
        name="swa_bias",
    )(rel_bias, bucket)


def _swa_kernel(sink_ref, q_ref, kc_ref, kp_ref, vc_ref, vp_ref, bias_ref, o_ref, *, tq):
    i = pl.program_id(1)
    p = pl.program_id(2)
    blk = SWA_BLOCK
    lo_half = lax.broadcasted_iota(jnp.int32, (blk, LANES), 1) < HEAD_DIM
    in_prev = lax.broadcasted_iota(jnp.int32, (blk, 2 * blk), 1) < blk
    for sb in range(tq // blk):
        q = q_ref[sb * blk:(sb + 1) * blk, :]
        if sb == 0:
            kw = jnp.concatenate([kp_ref[...], kc_ref[0:blk, :]], axis=0)
            vw = jnp.concatenate([vp_ref[...], vc_ref[0:blk, :]], axis=0)
        else:
            kw = kc_ref[(sb - 1) * blk:(sb + 1) * blk, :]
            vw = vc_ref[(sb - 1) * blk:(sb + 1) * blk, :]
        outs = []
        for j in range(2):
            qm = jnp.where(lo_half if j == 0 else jnp.logical_not(lo_half), q, jnp.zeros_like(q))
            s = _dot_nt(qm, kw) + bias_ref[j]
            if sb == 0:
                s = jnp.where(jnp.logical_and(in_prev, i == 0), NEG, s)
            sink = sink_ref[2 * p + j]
            m = jnp.maximum(jnp.max(s, axis=1, keepdims=True), sink)
            e = jnp.exp(s - m)
            den = jnp.sum(e, axis=1, keepdims=True) + jnp.exp(sink - m)
            outs.append(_dot(e.astype(BF16), vw) / den)
        o_ref[sb * blk:(sb + 1) * blk, :] = jnp.where(lo_half, outs[0], outs[1]).astype(o_ref.dtype)


def _swa_attention(qkv, bias, sinks, batch, seq):
    t = qkv.shape[0]
    tq = _tile(seq, 512)
    nt = seq // tq
    pairs = SWA_HEADS // 2
    group = pairs // SWA_KV_HEADS
    kcol = pairs
    vcol = pairs + SWA_KV_HEADS
    per = tq // SWA_BLOCK

    def prev_idx(b, i, p):
        return jnp.maximum(b * (seq // SWA_BLOCK) + i * per - 1, 0)

    return pl.pallas_call(
        functools.partial(_swa_kernel, tq=tq),
        grid=(batch, nt, pairs),
        in_specs=[pl.BlockSpec(memory_space=pltpu.SMEM),
                  pl.BlockSpec((tq, LANES), lambda b, i, p: (b * nt + i, p)),
                  pl.BlockSpec((tq, LANES), lambda b, i, p: (b * nt + i, kcol + p // group)),
                  pl.BlockSpec((SWA_BLOCK, LANES), lambda b, i, p: (prev_idx(b, i, p), kcol + p // group)),
                  pl.BlockSpec((tq, LANES), lambda b, i, p: (b * nt + i, vcol + p // group)),
                  pl.BlockSpec((SWA_BLOCK, LANES), lambda b, i, p: (prev_idx(b, i, p), vcol + p // group)),
                  pl.BlockSpec((2, SWA_BLOCK, 2 * SWA_BLOCK), lambda b, i, p: (p, 0, 0))],
        out_specs=pl.BlockSpec((tq, LANES), lambda b, i, p: (b * nt + i, p)),
        out_shape=jax.ShapeDtypeStruct((t, SWA_HEADS * HEAD_DIM), BF16),
        compiler_params=_params("parallel", "parallel", "arbitrary"),
        name="swa_attention",
    )(sinks, qkv, qkv, qkv, qkv, qkv, bias)


def _swa_weights(w_qkv, b_qkv):
    nq = SWA_HEADS * HEAD_DIM
    kv = SWA_KV_HEADS * HEAD_DIM
    dup = np.concatenate([np.tile(np.arange(HEAD_DIM), 2) + h * HEAD_DIM for h in range(SWA_KV_HEADS)])
    cols = np.concatenate([np.arange(nq), nq + dup, nq + kv + dup])
    scale = np.concatenate([np.full(nq, HEAD_DIM ** -0.5), np.ones(2 * dup.size)]).astype(np.float32)
    return w_qkv[:, cols].astype(BF16), b_qkv[cols], jnp.asarray(scale)


def _staggered(units, stages):
    vals = list(units)
    for t in range(len(units) + len(stages) - 1):
        for k, stage in enumerate(stages):
            u = t - k
            if 0 <= u < len(units):
                vals[u] = stage(units[u], vals[u])


def _causal_sweep(i, tq, tk, stages, strict):
    nsub = tq // tk
    units = []
    for sb in reversed(range(nsub)):
        nl = tq - sb * tk
        key = lax.broadcasted_iota(jnp.int32, (tk, nl), 0)
        qry = lax.broadcasted_iota(jnp.int32, (tk, nl), 1)
        mask = key < qry if strict else key <= qry
        units += [(i * nsub + sb, h, slice(sb * tk, tq), mask) for h in range(2)]
    _staggered(units, stages)

    def body(n, carry):
        units = [((i - 1 - n) * nsub + sb, h, slice(0, tq), None)
                 for sb in reversed(range(nsub)) for h in range(2)]
        _staggered(units, stages)
        return carry

    lax.fori_loop(0, i, body, 0)


def _split_pair(q):
    lo_half = lax.broadcasted_iota(jnp.int32, q.shape, 1) < HEAD_DIM
    zero = jnp.zeros_like(q)
    return jnp.where(lo_half, q, zero), jnp.where(lo_half, zero, q)


def _sb_kernel(q_ref, k_ref, vt_ref, u_ref, o_ref, acc_ref, c_ref, *, tq, tk):
    i = pl.program_id(2)
    qms = _split_pair(q_ref[...])
    u = u_ref[...]
    acc_ref[...] = jnp.zeros_like(acc_ref)
    c_ref[...] = jnp.zeros_like(c_ref)

    def scores(unit, _):
        jb, h, lanes, _ = unit
        kt = k_ref[pl.ds(pl.multiple_of(jb * tk, tk), tk), :]
        return _dot_nt(kt, qms[h][lanes])

    def log_fail(unit, z):
        lf = -(jnp.maximum(z, 0.0) + jnp.log(1.0 + jnp.exp(-jnp.abs(z))))
        if unit[3] is not None:
            lf = jnp.where(unit[3], lf, 0.0)
        hi = lf.astype(BF16)
        return z, hi, (lf - hi.astype(F32)).astype(BF16)

    def suffix_sum(unit, val):
        z, hi, lo = val
        return z, _dot(u, hi) + _dot(u, lo)

    def weights(unit, val):
        _, h, lanes, mask = unit
        z, r = val
        c = c_ref[h:h + 1, lanes]
        la = z + r + c
        if mask is not None:
            la = jnp.where(mask, la, NEG)
        c_ref[h:h + 1, lanes] = c + r[0:1, :]
        return jnp.exp(la).astype(BF16)

    def values(unit, a):
        jb, h, lanes, _ = unit
        rows = slice(h * HEAD_DIM, (h + 1) * HEAD_DIM)
        acc_ref[rows, lanes] += _dot(vt_ref[jb, rows, :], a)

    _causal_sweep(i, tq, tk, (scores, log_fail, suffix_sum, weights, values), strict=True)
    o_ref[...] = acc_ref[...].T.astype(o_ref.dtype)


def _sb_attention(qk, vt, batch, seq):
    t = qk.shape[0]
    tq = _tile(seq, ATT_TQ)
    tk = vt.shape[2]
    nq = seq // tq
    pairs = SB_HEADS // 2
    u = jnp.asarray(np.triu(np.ones((tk, tk), np.float32)), BF16)
    return pl.pallas_call(
        functools.partial(_sb_kernel, tq=tq, tk=tk),
        grid=(batch, pairs, nq),
        in_specs=[pl.BlockSpec((tq, LANES), lambda b, p, i: (b * nq + i, p)),
                  pl.BlockSpec((seq, LANES), lambda b, p, i: (b, pairs + p)),
                  pl.BlockSpec((seq // tk, LANES, tk), lambda b, p, i: (b, p, 0)),
                  pl.BlockSpec((tk, tk), lambda b, p, i: (0, 0))],
        out_specs=pl.BlockSpec((tq, LANES), lambda b, p, i: (b * nq + i, p)),
        out_shape=jax.ShapeDtypeStruct((t, SB_HEADS * HEAD_DIM), BF16),
        scratch_shapes=[pltpu.VMEM((LANES, tq), F32), pltpu.VMEM((8, tq), F32)],
        compiler_params=_params("parallel", "parallel", "arbitrary"),
        name="sb_attention",
    )(qk, qk, vt, u)


MLA_QN = MLA_HEADS * MLA_NOPE_DIM
MLA_QR = MLA_HEADS * MLA_ROPE_DIM
MLA_OUT = 2 * MLA_QN + MLA_QR + LANES


def _mla_proj_kernel(x_ref, g_ref, wd_ref, qg_ref, wuq_ref, kvg_ref, wukv_ref, cs_ref, o_ref, vt_ref,
                     *, scale, tk):
    h = _rms(x_ref[...], g_ref[...]).astype(BF16)
    c = _dot(h, wd_ref[...])
    cos = cs_ref[:, :LANES]
    sin = cs_ref[:, LANES:]
    cq = _rms(c[:, :MLA_Q_RANK], qg_ref[...]).astype(BF16)
    q = _dot(cq, wuq_ref[...])
    o_ref[:, :MLA_QN] = (q[:, :MLA_QN] * scale).astype(o_ref.dtype)
    for m in range(MLA_QR // LANES):
        a = MLA_QN + m * LANES
        rot = q[:, a:a + LANES] * cos + q[:, a + MLA_QR:a + MLA_QR + LANES] * sin
        o_ref[:, a:a + LANES] = (rot * scale).astype(o_ref.dtype)
    kv0 = MLA_Q_RANK + MLA_KV_RANK
    ckv = _rms(c[:, MLA_Q_RANK:kv0], kvg_ref[...]).astype(BF16)
    kv = _dot(ckv, wukv_ref[...])
    kn0 = MLA_QN + MLA_QR
    o_ref[:, kn0:kn0 + MLA_QN] = kv[:, :MLA_QN].astype(o_ref.dtype)
    kr = c[:, kv0:kv0 + LANES] * cos + c[:, kv0 + LANES:kv0 + 2 * LANES] * sin
    o_ref[:, MLA_OUT - LANES:] = kr.astype(o_ref.dtype)
    _store_vt(vt_ref, kv[:, MLA_QN:], tk)


def _mla_rope_layout():
    half = MLA_ROPE_DIM // 2
    per = LANES // MLA_ROPE_DIM
    dq = MLA_NOPE_DIM + MLA_ROPE_DIM
    q_nope = np.concatenate([h * dq + np.arange(MLA_NOPE_DIM) for h in range(MLA_HEADS)])
    q_rope = np.zeros(MLA_QR, np.int64)
    q_swap = np.zeros(MLA_QR, np.int64)
    for h in range(MLA_HEADS):
        m, r = divmod(h, per)
        x1 = h * dq + MLA_NOPE_DIM + np.arange(half)
        x2 = x1 + half
        first = m * LANES + r * half + np.arange(half)
        second = first + LANES // 2
        q_rope[first], q_rope[second] = x1, x2
        q_swap[first], q_swap[second] = x2, x1
    k_rope = np.zeros(LANES, np.int64)
    k_swap = np.zeros(LANES, np.int64)
    base = MLA_Q_RANK + MLA_KV_RANK
    for r in range(per):
        first = r * half + np.arange(half)
        second = first + LANES // 2
        k_rope[first], k_rope[second] = base + np.arange(half), base + half + np.arange(half)
        k_swap[first], k_swap[second] = base + half + np.arange(half), base + np.arange(half)
    dkv = MLA_NOPE_DIM + HEAD_DIM
    k_nope = np.concatenate([h * dkv + np.arange(MLA_NOPE_DIM) for h in range(MLA_HEADS)])
    v = k_nope + MLA_NOPE_DIM
    down = np.concatenate([np.arange(base), k_rope, k_swap])
    return np.concatenate([q_nope, q_rope, q_swap]), np.concatenate([k_nope, v]), down


def _mla_rope_tables(seq):
    half = MLA_ROPE_DIM // 2
    inv = ROPE_THETA ** (-jnp.arange(0, MLA_ROPE_DIM, 2, dtype=F32) / MLA_ROPE_DIM)
    ang = jnp.arange(seq, dtype=F32)[:, None] * inv[None, :]
    cos, sin = jnp.cos(ang), jnp.sin(ang)
    reps = LANES // 2 // half
    cos_t = jnp.tile(cos, (1, 2 * reps))
    sin_t = jnp.concatenate([jnp.tile(-sin, (1, reps)), jnp.tile(sin, (1, reps))], axis=1)
    return jnp.concatenate([cos_t, sin_t], axis=1)


def _mla_proj(x, g, w_down, q_norm, w_uq, kv_norm, w_ukv, seq):
    t, d = x.shape
    tm = _tile(seq, 512)
    tk = _tile(seq, ATT_TK)
    ns = seq // tm
    uq_cols, ukv_cols, down_cols = _mla_rope_layout()
    wd = w_down[:, down_cols].astype(BF16)
    wuq = w_uq[:, uq_cols].astype(BF16)
    wukv = w_ukv[:, ukv_cols].astype(BF16)
    cs = _mla_rope_tables(seq)
    scale = (MLA_NOPE_DIM + MLA_ROPE_DIM) ** -0.5
    full = lambda a: pl.BlockSpec(a.shape, lambda i: (0, 0))
    g2, qg, kvg = g.reshape(1, d), q_norm.reshape(1, -1), kv_norm.reshape(1, -1)
    return pl.pallas_call(
        functools.partial(_mla_proj_kernel, scale=scale, tk=tk),
        grid=(t // tm,),
        in_specs=[pl.BlockSpec((tm, d), lambda i: (i, 0)), full(g2), full(wd), full(qg), full(wuq),
                  full(kvg), full(wukv), pl.BlockSpec((tm, 2 * LANES), lambda i: (i % ns, 0))],
        out_specs=[pl.BlockSpec((tm, MLA_OUT), lambda i: (i, 0)),
                   pl.BlockSpec((tm // tk, MLA_QN, tk), lambda i: (i, 0, 0))],
        out_shape=[jax.ShapeDtypeStruct((t, MLA_OUT), BF16),
                   jax.ShapeDtypeStruct((t // tk, MLA_QN, tk), BF16)],
        compiler_params=_params("parallel"),
        name="mla_proj",
    )(x, g2, wd, qg, wuq, kvg, wukv, cs)


def _mla_kernel(qn_ref, qr_ref, kn_ref, kr_ref, vt_ref, o_ref, acc_ref, ml_ref, *, tq, tk):
    p = pl.program_id(1)
    i = pl.program_id(2)
    q = jnp.concatenate([qn_ref[...], qr_ref[...]], axis=1)
    lane = lax.broadcasted_iota(jnp.int32, (tq, 2 * LANES), 1)
    half = MLA_ROPE_DIM // 2
    zero = jnp.zeros_like(q)
    qms = []
    for j in range(2):
        r0 = LANES + (2 * (p % 2) + j) * half
        sel = jnp.logical_and(lane >= j * HEAD_DIM, lane < (j + 1) * HEAD_DIM)
        sel = jnp.logical_or(sel, jnp.logical_and(lane >= r0, lane < r0 + half))
        r1 = r0 + LANES // 2
        sel = jnp.logical_or(sel, jnp.logical_and(lane >= r1, lane < r1 + half))
        qms.append(jnp.where(sel, q, zero))
    acc_ref[...] = jnp.zeros_like(acc_ref)
    row = lax.broadcasted_iota(jnp.int32, ml_ref.shape, 0)
    ml_ref[...] = jnp.where(row < 2, NEG, 0.0)

    def scores(unit, _):
        jb, h, lanes, _ = unit
        start = pl.multiple_of(jb * tk, tk)
        kt = jnp.concatenate([kn_ref[pl.ds(start, tk), :], kr_ref[pl.ds(start, tk), :]], axis=1)
        return _dot_nt(kt, qms[h][lanes])

    def softmax(unit, s):
        _, h, lanes, mask = unit
        if mask is not None:
            s = jnp.where(mask, s, NEG)
        m_old = ml_ref[h:h + 1, lanes]
        m_new = jnp.maximum(m_old, jnp.max(s, axis=0, keepdims=True))
        alpha = jnp.exp(m_old - m_new)
        e = jnp.exp(s - m_new)
        ml_ref[h:h + 1, lanes] = m_new
        ml_ref[2 + h:3 + h, lanes] = alpha * ml_ref[2 + h:3 + h, lanes] + jnp.sum(e, axis=0, keepdims=True)
        return alpha, e.astype(BF16)

    def values(unit, val):
        jb, h, lanes, _ = unit
        alpha, e = val
        rows = slice(h * HEAD_DIM, (h + 1) * HEAD_DIM)
        acc_ref[rows, lanes] = alpha * acc_ref[rows, lanes] + _dot(vt_ref[jb, rows, :], e)

    _causal_sweep(i, tq, tk, (scores, softmax, values), strict=False)
    head = lax.broadcasted_iota(jnp.int32, acc_ref.shape, 0) < HEAD_DIM
    denom = jnp.where(head, ml_ref[2:3, :], ml_ref[3:4, :])
    o_ref[...] = (acc_ref[...] / denom).T.astype(o_ref.dtype)


def _mla_attention(a, vt, batch, seq):
    t = a.shape[0]
    tq = _tile(seq, ATT_TQ)
    tk = vt.shape[2]
    nq = seq // tq
    pairs = MLA_HEADS // 2
    qr0 = MLA_QN // LANES
    kn0 = (MLA_QN + MLA_QR) // LANES
    kr0 = kn0 + pairs
    return pl.pallas_call(
        functools.partial(_mla_kernel, tq=tq, tk=tk),
        grid=(batch, pairs, nq),
        in_specs=[pl.BlockSpec((tq, LANES), lambda b, p, i: (b * nq + i, p)),
                  pl.BlockSpec((tq, LANES), lambda b, p, i: (b * nq + i, qr0 + p // 2)),
                  pl.BlockSpec((seq, LANES), lambda b, p, i: (b, kn0 + p)),
                  pl.BlockSpec((seq, LANES), lambda b, p, i: (b, kr0)),
                  pl.BlockSpec((seq // tk, LANES, tk), lambda b, p, i: (b, p, 0))],
        out_specs=pl.BlockSpec((tq, LANES), lambda b, p, i: (b * nq + i, p)),
        out_shape=jax.ShapeDtypeStruct((t, MLA_HEADS * HEAD_DIM), BF16),
        scratch_shapes=[pltpu.VMEM((LANES, tq), F32), pltpu.VMEM((8, tq), F32)],
        compiler_params=_params("parallel", "parallel", "arbitrary"),
        name="mla_attention",
    )(a, a, a, a, vt)


def _swiglu(h, wg, wu):
    g = _dot(h, wg)
    u = _dot(h, wu)
    return (g * jax.nn.sigmoid(g) * u).astype(BF16)


def _ffn_kernel(x_ref, g_ref, wg_ref, wu_ref, wd_ref, o_ref, h_ref):
    f = pl.program_id(1)

    @pl.when(f == 0)
    def _():
        x = x_ref[...]
        h_ref[...] = _rms(x, g_ref[...]).astype(BF16)
        o_ref[...] = x

    o_ref[...] += _dot(_swiglu(h_ref[...], wg_ref[...], wu_ref[...]), wd_ref[...])


def _ffn(x, g, wg, wu, wd):
    t, d = x.shape
    fdim = wg.shape[1]
    tm = _tile(t, 512)
    tf = _tile(fdim, 1408)
    return pl.pallas_call(
        _ffn_kernel,
        grid=(t // tm, fdim // tf),
        in_specs=[pl.BlockSpec((tm, d), lambda i, f: (i, 0)),
                  pl.BlockSpec((1, d), lambda i, f: (0, 0)),
                  pl.BlockSpec((d, tf), lambda i, f: (0, f)),
                  pl.BlockSpec((d, tf), lambda i, f: (0, f)),
                  pl.BlockSpec((tf, d), lambda i, f: (f, 0))],
        out_specs=pl.BlockSpec((tm, d), lambda i, f: (i, 0)),
        out_shape=jax.ShapeDtypeStruct((t, d), F32),
        scratch_shapes=[pltpu.VMEM((tm, d), BF16)],
        compiler_params=_params("parallel", "arbitrary"),
        name="dense_ffn",
    )(x, g.reshape(1, d), wg.astype(BF16), wu.astype(BF16), wd.astype(BF16))


MOE_TILE = 512
META_ROWS = 8


def _router_kernel(x_ref, g_ref, rt_ref, su_ref, eye_ref, hb_ref, meta_ref, metac_ref, cnt_ref, carry_ref):
    @pl.when(pl.program_id(0) == 0)
    def _():
        carry_ref[...] = jnp.zeros_like(carry_ref)

    h = _rms(x_ref[...], g_ref[...])
    hb_ref[...] = h.astype(BF16)
    logits = lax.dot_general(rt_ref[...], h, _NT, precision=lax.Precision.HIGHEST,
                             preferred_element_type=F32)
    ne, ct = logits.shape
    eio = lax.broadcasted_iota(jnp.int32, (ne, ct), 0).astype(F32)
    m1 = jnp.max(logits, axis=0, keepdims=True)
    i1 = jnp.min(jnp.where(logits == m1, eio, float(ne)), axis=0, keepdims=True)
    rest = jnp.where(eio == i1, -jnp.inf, logits)
    m2 = jnp.max(rest, axis=0, keepdims=True)
    i2 = jnp.min(jnp.where(rest == m2, eio, float(ne)), axis=0, keepdims=True)
    e2 = jnp.exp(m2 - m1)
    g1 = 1.0 / (1.0 + e2)
    g2 = e2 / (1.0 + e2)
    oh1 = (eio == i1).astype(F32)
    oh2 = (eio == i2).astype(F32)
    both = oh1 + oh2
    seen = carry_ref[:, 0:1] + _dot(both.astype(BF16), su_ref[...])
    r1 = jnp.sum(oh1 * seen, axis=0, keepdims=True)
    r2 = jnp.sum(oh2 * seen, axis=0, keepdims=True)
    meta = jnp.zeros((META_ROWS, ct), F32)
    for row, val in enumerate((i1, i2, r1, r2, g1, g2)):
        meta = jnp.where(eio == float(row), val, meta)
    meta_ref[...] = meta
    metac_ref[...] = lax.dot_general(eye_ref[...], meta, _NT, precision=lax.Precision.HIGHEST,
                                     preferred_element_type=F32)
    carry_ref[...] = carry_ref[...] + jnp.sum(both, axis=1, keepdims=True)
    cnt_ref[0] = carry_ref[...]


def _router(x, g, router):
    t, d = x.shape
    ct = _tile(t, MOE_TILE)
    nc = t // ct
    su = jnp.asarray(np.triu(np.ones((ct, ct), np.float32), 1), BF16)
    eye = jnp.eye(ct, dtype=F32)
    return pl.pallas_call(
        _router_kernel,
        grid=(nc,),
        in_specs=[pl.BlockSpec((ct, d), lambda i: (i, 0)),
                  pl.BlockSpec((1, d), lambda i: (0, 0)),
                  pl.BlockSpec((N_EXPERTS, d), lambda i: (0, 0)),
                  pl.BlockSpec((ct, ct), lambda i: (0, 0)),
                  pl.BlockSpec((ct, ct), lambda i: (0, 0))],
        out_specs=[pl.BlockSpec((ct, d), lambda i: (i, 0)),
                   pl.BlockSpec((META_ROWS, ct), lambda i: (0, i)),
                   pl.BlockSpec((ct, META_ROWS), lambda i: (i, 0)),
                   pl.BlockSpec((1, N_EXPERTS, LANES), lambda i: (i, 0, 0))],
        out_shape=[jax.ShapeDtypeStruct((t, d), BF16),
                   jax.ShapeDtypeStruct((META_ROWS, t), F32),
                   jax.ShapeDtypeStruct((t, META_ROWS), F32),
                   jax.ShapeDtypeStruct((nc, N_EXPERTS, LANES), F32)],
        scratch_shapes=[pltpu.VMEM((N_EXPERTS, LANES), F32)],
        compiler_params=_params("arbitrary"),
        name="moe_router",
    )(x, g.reshape(1, d), router.T, su, eye)


def _moe_plan(cnt_after, t, ct):
    nc = t // ct
    rows = 2 * t + N_EXPERTS * ct
    nrt = rows // ct
    counts = cnt_after[:, :, 0].astype(jnp.int32)
    total = counts[-1]
    padded = (total + ct - 1) // ct * ct
    off_end = jnp.cumsum(padded)
    off = off_end - padded
    before = jnp.concatenate([jnp.zeros((1, N_EXPERTS), jnp.int32), counts[:-1]], axis=0)
    chunk_start = (off[:, None] + before.T).reshape(-1)
    tile_start = jnp.arange(nrt, dtype=jnp.int32) * ct
    starts = jnp.sort(jnp.concatenate([chunk_start, tile_start]))
    ends = jnp.concatenate([starts[1:], jnp.array([rows], jnp.int32)])
    item_r = jnp.minimum(starts // ct, nrt - 1)
    item_c = (jnp.searchsorted(chunk_start, starts, side="right") - 1) % nc
    first = lambda key: jnp.concatenate([jnp.ones((1,), jnp.int32), (key[1:] != key[:-1]).astype(jnp.int32)])
    dispatch = (item_r, item_c, starts, ends, first(item_r))
    order = jnp.argsort(item_c * (rows + 1) + starts)
    c2 = item_c[order]
    combine = (c2, item_r[order], starts[order], ends[order], first(c2))
    tile_expert = jnp.minimum(jnp.searchsorted(off_end, tile_start, side="right"), N_EXPERTS - 1)
    tile_valid = (tile_start < off_end[-1]).astype(jnp.int32)
    return rows, off.astype(jnp.int32), dispatch, combine, tile_expert.astype(jnp.int32), tile_valid


def _positions(expert, rank, off_ref):
    base = jnp.zeros_like(rank)
    for e in range(N_EXPERTS):
        base = jnp.where(expert == float(e), off_ref[e].astype(F32), base)
    return base + rank


def _dispatch_kernel(r_ref, c_ref, s_ref, e_ref, first_ref, off_ref, meta_ref, h_ref, xs_ref, gate_ref):
    n = pl.program_id(0)
    rt, ct = xs_ref.shape[0], h_ref.shape[0]

    @pl.when(first_ref[n] == 1)
    def _():
        xs_ref[...] = jnp.zeros_like(xs_ref)
        gate_ref[...] = jnp.zeros_like(gate_ref)

    meta = meta_ref[...]
    row = (lax.broadcasted_iota(jnp.int32, (rt, ct), 0) + r_ref[n] * rt)
    live = jnp.logical_and(row >= s_ref[n], row < e_ref[n])
    rowf = row.astype(F32)
    sel0 = jnp.logical_and(_positions(meta[0:1], meta[2:3], off_ref) == rowf, live)
    sel1 = jnp.logical_and(_positions(meta[1:2], meta[3:4], off_ref) == rowf, live)
    sel = jnp.logical_or(sel0, sel1).astype(F32).astype(BF16)
    xs_ref[...] += _dot(sel, h_ref[...]).astype(xs_ref.dtype)
    gate = jnp.where(sel0, meta[4:5], 0.0) + jnp.where(sel1, meta[5:6], 0.0)
    gate_ref[...] += jnp.sum(gate, axis=1, keepdims=True)


def _dispatch(plan, off, meta, hb, rows, ct):
    t, d = hb.shape
    n_items = plan[0].shape[0]
    return pl.pallas_call(
        _dispatch_kernel,
        grid_spec=pltpu.PrefetchScalarGridSpec(
            num_scalar_prefetch=6,
            grid=(n_items,),
            in_specs=[pl.BlockSpec((META_ROWS, ct), lambda n, r, c, *_: (0, c[n])),
                      pl.BlockSpec((ct, d), lambda n, r, c, *_: (c[n], 0))],
            out_specs=[pl.BlockSpec((ct, d), lambda n, r, c, *_: (r[n], 0)),
                       pl.BlockSpec((ct, 1), lambda n, r, c, *_: (r[n], 0))]),
        out_shape=[jax.ShapeDtypeStruct((rows, d), BF16),
                   jax.ShapeDtypeStruct((rows, 1), F32)],
        compiler_params=_params("arbitrary"),
        name="moe_dispatch",
    )(*plan, off, meta, hb)


def _expert_kernel(te_ref, tv_ref, xs_ref, gate_ref, wg_ref, wu_ref, wd_ref, y_ref, acc_ref):
    r = pl.program_id(0)
    f = pl.program_id(1)

    @pl.when(f == 0)
    def _():
        acc_ref[...] = jnp.zeros_like(acc_ref)

    @pl.when(tv_ref[r] == 1)
    def _():
        acc_ref[...] += _dot(_swiglu(xs_ref[...], wg_ref[0], wu_ref[0]), wd_ref[0])

    @pl.when(f == pl.num_programs(1) - 1)
    def _():
        y_ref[...] = (acc_ref[...] * gate_ref[...]).astype(y_ref.dtype)


def _experts(tile_expert, tile_valid, xs, gate, wg, wu, wd, ct):
    rows, d = xs.shape
    fdim = wg.shape[2]
    tf = _tile(fdim, 896)
    nf = fdim // tf

    def wcol(r, f, te, tv):
        return (te[r], 0, jnp.where(tv[r] == 1, f, nf - 1))

    def wrow(r, f, te, tv):
        return (te[r], jnp.where(tv[r] == 1, f, nf - 1), 0)

    return pl.pallas_call(
        _expert_kernel,
        grid_spec=pltpu.PrefetchScalarGridSpec(
            num_scalar_prefetch=2,
            grid=(rows // ct, nf),
            in_specs=[pl.BlockSpec((ct, d), lambda r, f, te, tv: (r, 0)),
                      pl.BlockSpec((ct, 1), lambda r, f, te, tv: (r, 0)),
                      pl.BlockSpec((1, d, tf), wcol),
                      pl.BlockSpec((1, d, tf), wcol),
                      pl.BlockSpec((1, tf, d), wrow)],
            out_specs=pl.BlockSpec((ct, d), lambda r, f, te, tv: (r, 0)),
            scratch_shapes=[pltpu.VMEM((ct, d), F32)]),
        out_shape=jax.ShapeDtypeStruct((rows, d), BF16),
        compiler_params=_params("arbitrary", "arbitrary"),
        name="moe_experts",
    )(tile_expert, tile_valid, xs, gate, wg, wu, wd)


def _combine_kernel(c_ref, r_ref, s_ref, e_ref, first_ref, off_ref, x_ref, metac_ref, y_ref, o_ref):
    n = pl.program_id(0)
    ct, rt = x_ref.shape[0], y_ref.shape[0]

    @pl.when(first_ref[n] == 1)
    def _():
        o_ref[...] = x_ref[...]

    metac = metac_ref[...]
    row = lax.broadcasted_iota(jnp.int32, (ct, rt), 1) + r_ref[n] * rt
    live = jnp.logical_and(row >= s_ref[n], row < e_ref[n])
    rowf = row.astype(F32)
    sel0 = _positions(metac[:, 0:1], metac[:, 2:3], off_ref) == rowf
    sel1 = _positions(metac[:, 1:2], metac[:, 3:4], off_ref) == rowf
    sel = jnp.logical_and(jnp.logical_or(sel0, sel1), live).astype(F32).astype(BF16)
    o_ref[...] += _dot(sel, y_ref[...])


def _combine(plan, off, x, metac, y, ct):
    t, d = x.shape
    n_items = plan[0].shape[0]
    return pl.pallas_call(
        _combine_kernel,
        grid_spec=pltpu.PrefetchScalarGridSpec(
            num_scalar_prefetch=6,
            grid=(n_items,),
            in_specs=[pl.BlockSpec((ct, d), lambda n, c, r, *_: (c[n], 0)),
                      pl.BlockSpec((ct, META_ROWS), lambda n, c, r, *_: (c[n], 0)),
                      pl.BlockSpec((ct, d), lambda n, c, r, *_: (r[n], 0))],
            out_specs=pl.BlockSpec((ct, d), lambda n, c, r, *_: (c[n], 0))),
        out_shape=jax.ShapeDtypeStruct((t, d), F32),
        compiler_params=_params("arbitrary"),
        name="moe_combine",
    )(*plan, off, x, metac, y)


def _moe(x, g, router, wg, wu, wd):
    t = x.shape[0]
    ct = _tile(t, MOE_TILE)
    hb, meta, metac, cnt = _router(x, g, router)
    rows, off, dispatch, combine, tile_expert, tile_valid = _moe_plan(cnt, t, ct)
    xs, gate = _dispatch(dispatch, off, meta, hb, rows, ct)
    y = _experts(tile_expert, tile_valid, xs, gate, wg.astype(BF16), wu.astype(BF16), wd.astype(BF16), ct)
    return _combine(combine, off, x, metac, y, ct)


def _final_norm_kernel(x_ref, g_ref, o_ref):
    o_ref[...] = _rms(x_ref[...], g_ref[...])


def _final_norm(x, g):
    t, d = x.shape
    tm = _tile(t, 1024)
    return pl.pallas_call(
        _final_norm_kernel,
        grid=(t // tm,),
        in_specs=[pl.BlockSpec((tm, d), lambda i: (i, 0)), pl.BlockSpec((1, d), lambda i: (0, 0))],
        out_specs=pl.BlockSpec((tm, d), lambda i: (i, 0)),
        out_shape=jax.ShapeDtypeStruct((t, d), F32),
        compiler_params=_params("parallel"),
        name="final_norm",
    )(x, g.reshape(1, d))


def kernel(x, rel_bias, attn_norm, ffn_norm, final_norm, swa_w_qkv, swa_b_qkv, swa_sinks, swa_w_o, sb_w_qkv, sb_w_o, mla_w_down, mla_q_norm, mla_w_uq, mla_kv_norm, mla_w_ukv, mla_w_o, ffn_w_gate, ffn_w_up, ffn_w_down, moe_router, moe_w_gate, moe_w_up, moe_w_down):
    batch, seq, d = x.shape
    depth = attn_norm.shape[0]
    xt = x.reshape(batch * seq, d)
    swa_bias = _swa_bias(rel_bias)
    for i in range(depth):
        mixer, j = i % N_MIXERS, i // N_MIXERS
        if mixer == 0:
            w, b, cs = _swa_weights(swa_w_qkv[j], swa_b_qkv[j])
            qkv = _norm_proj(xt, attn_norm[i], w, b, cs, "swa_proj")
            o = _swa_attention(qkv, swa_bias, swa_sinks[j], batch, seq)
            xt = _out_proj(xt, o, swa_w_o[j].astype(BF16), "swa_out")
        elif mixer == 1:
            n = sb_w_qkv.shape[2] // 3
            cs = jnp.concatenate([jnp.full((n,), HEAD_DIM ** -0.5, F32), jnp.ones((n,), F32)])
            qk, vt = _norm_proj_vt(xt, attn_norm[i], sb_w_qkv[j].astype(BF16), cs, n, seq, "sb_proj")
            o = _sb_attention(qk, vt, batch, seq)
            xt = _out_proj(xt, o, sb_w_o[j].astype(BF16), "sb_out")
        else:
            a, vt = _mla_proj(xt, attn_norm[i], mla_w_down[j], mla_q_norm[j], mla_w_uq[j],
                              mla_kv_norm[j], mla_w_ukv[j], seq)
            o = _mla_attention(a, vt, batch, seq)
            xt = _out_proj(xt, o, mla_w_o[j].astype(BF16), "mla_out")
        f = i // 2
        if i % 2 == 0:
            xt = _ffn(xt, ffn_norm[i], ffn_w_gate[f], ffn_w_up[f], ffn_w_down[f])
        else:
            xt = _moe(xt, ffn_norm[i], moe_router[f], moe_w_gate[f], moe_w_up[f], moe_w_down[f])
    return _final_norm(xt, final_norm).reshape(batch, seq, d)
```

```python
import functools
import math

import numpy as np
import jax
import jax.numpy as jnp
from jax import lax
from jax.experimental import pallas as pl
from jax.experimental.pallas import tpu as pltpu

F32 = jnp.float32
BF16 = jnp.bfloat16

N_MIXERS = 3
RMS_EPS = 1e-6
SWA_HEADS = 16
SWA_KV_HEADS = 2
HEAD_DIM = 64
SWA_BLOCK = 128
SWA_WINDOW = 128
REL_BUCKETS = 32
REL_MAX_DIST = 128
SB_HEADS = 16
MLA_HEADS = 16
MLA_NOPE_DIM = 64
MLA_ROPE_DIM = 32
MLA_Q_RANK = 384
MLA_KV_RANK = 256
ROPE_THETA = 10000.0
N_EXPERTS = 8

LANES = 128
NEG = -1e30
LOG2E = math.log2(math.e)
VMEM_LIMIT = 56 * 1024 * 1024

_NT = (((1,), (1,)), ((), ()))


def _dot(a, b):
    return jnp.dot(a, b, preferred_element_type=F32)


def _dot_nt(a, b):
    return lax.dot_general(a, b, _NT, preferred_element_type=F32)


def _rms(x, g):
    return x * lax.rsqrt(jnp.mean(x * x, axis=-1, keepdims=True) + RMS_EPS) * g


def _params(*sem):
    return pltpu.CompilerParams(dimension_semantics=sem, vmem_limit_bytes=VMEM_LIMIT)


def _tile(n, pref):
    return pref if n % pref == 0 else n


def _norm_proj_kernel(x_ref, g_ref, w_ref, b_ref, cs_ref, o_ref):
    h = _rms(x_ref[...], g_ref[...]).astype(BF16)
    y = (_dot(h, w_ref[...]) + b_ref[...]) * cs_ref[...]
    o_ref[...] = y.astype(o_ref.dtype)


def _norm_proj(x, g, w, b, colscale, name):
    t, d = x.shape
    n = w.shape[1]
    tm = _tile(t, 512)
    return pl.pallas_call(
        _norm_proj_kernel,
        grid=(t // tm,),
        in_specs=[pl.BlockSpec((tm, d), lambda i: (i, 0)),
                  pl.BlockSpec((1, d), lambda i: (0, 0)),
                  pl.BlockSpec((d, n), lambda i: (0, 0)),
                  pl.BlockSpec((1, n), lambda i: (0, 0)),
                  pl.BlockSpec((1, n), lambda i: (0, 0))],
        out_specs=pl.BlockSpec((tm, n), lambda i: (i, 0)),
        out_shape=jax.ShapeDtypeStruct((t, n), BF16),
        compiler_params=_params("parallel"),
        name=name,
    )(x, g.reshape(1, d), w, b.reshape(1, n), colscale.reshape(1, n))


ATT_TQ = 512
ATT_TK = 256


def _store_vt(vt_ref, v, tk):
    for s in range(v.shape[0] // tk):
        vt_ref[s] = v[s * tk:(s + 1) * tk, :].T.astype(vt_ref.dtype)


def _norm_proj_vt_kernel(x_ref, g_ref, w_ref, cs_ref, o_ref, vt_ref, *, tk):
    h = _rms(x_ref[...], g_ref[...]).astype(BF16)
    y = _dot(h, w_ref[...])
    n = o_ref.shape[1]
    o_ref[...] = (y[:, :n] * cs_ref[...]).astype(o_ref.dtype)
    _store_vt(vt_ref, y[:, n:], tk)


def _norm_proj_vt(x, g, w, colscale, nv, seq, name):
    t, d = x.shape
    n = w.shape[1] - nv
    tm = _tile(seq, 512)
    tk = _tile(seq, ATT_TK)
    per = tm // tk
    return pl.pallas_call(
        functools.partial(_norm_proj_vt_kernel, tk=tk),
        grid=(t // tm,),
        in_specs=[pl.BlockSpec((tm, d), lambda i: (i, 0)),
                  pl.BlockSpec((1, d), lambda i: (0, 0)),
                  pl.BlockSpec((d, n + nv), lambda i: (0, 0)),
                  pl.BlockSpec((1, n), lambda i: (0, 0))],
        out_specs=[pl.BlockSpec((tm, n), lambda i: (i, 0)),
                   pl.BlockSpec((per, nv, tk), lambda i: (i, 0, 0))],
        out_shape=[jax.ShapeDtypeStruct((t, n), BF16),
                   jax.ShapeDtypeStruct((t // tk, nv, tk), BF16)],
        compiler_params=_params("parallel"),
        name=name,
    )(x, g.reshape(1, d), w, colscale.reshape(1, n))


def _out_proj_kernel(x_ref, o_ref, w_ref, y_ref):
    y_ref[...] = x_ref[...] + _dot(o_ref[...], w_ref[...])


def _out_proj(x, o, w, name):
    t, d = x.shape
    k = o.shape[1]
    tm = _tile(t, 512)
    return pl.pallas_call(
        _out_proj_kernel,
        grid=(t // tm,),
        in_specs=[pl.BlockSpec((tm, d), lambda i: (i, 0)),
                  pl.BlockSpec((tm, k), lambda i: (i, 0)),
                  pl.BlockSpec((k, d), lambda i: (0, 0))],
        out_specs=pl.BlockSpec((tm, d), lambda i: (i, 0)),
        out_shape=jax.ShapeDtypeStruct((t, d), F32),
        compiler_params=_params("parallel"),
        name=name,
    )(x, o, w)


def _t5_bucket_table():
    qi = np.arange(SWA_BLOCK)[:, None]
    kj = np.arange(2 * SWA_BLOCK)[None, :]
    dist = qi + SWA_BLOCK - kj
    d0 = np.maximum(dist, 0)
    max_exact = REL_BUCKETS // 2
    d = np.maximum(d0, 1).astype(np.float32)
    large = max_exact + (np.log(d / max_exact) / math.log(REL_MAX_DIST / max_exact)
                         * (REL_BUCKETS - max_exact)).astype(np.int32)
    large = np.minimum(large, REL_BUCKETS - 1)
    bucket = np.where(d0 < max_exact, d0, large)
    band = (dist >= 0) & (dist < SWA_WINDOW)
    return np.where(band, bucket, -1).astype(np.int32)


def _swa_bias_kernel(rel_ref, bucket_ref, o_ref):
    h = pl.program_id(0)
    bucket = bucket_ref[...]
    acc = jnp.full(bucket.shape, NEG, F32)
    for b in range(REL_BUCKETS):
        acc = jnp.where(bucket == b, rel_ref[b, h], acc)
    o_ref[0] = acc


def _swa_bias(rel_bias):
    bucket = jnp.asarray(_t5_bucket_table())
    q, k = bucket.shape
    return pl.pallas_call(
        _swa_bias_kernel,
        grid=(SWA_HEADS,),
        in_specs=[pl.BlockSpec(memory_space=pltpu.SMEM),
                  pl.BlockSpec((q, k), lambda h: (0, 0))],
        out_specs=pl.BlockSpec((1, q, k), lambda h: (h, 0, 0)),
        out_shape=jax.ShapeDtypeStruct((SWA_HEADS, q, k), F32),
        compiler_params=_params("arbitrary"),
        name="swa_bias",
    )(rel_bias, bucket)


def _swa_kernel(sink_ref, q_ref, kc_ref, kp_ref, vc_ref, vp_ref, bias_ref, o_ref, *, tq):
    i = pl.program_id(1)
    p = pl.program_id(2)
    blk = SWA_BLOCK
    lo_half = lax.broadcasted_iota(jnp.int32, (blk, LANES), 1) < HEAD_DIM
    in_prev = lax.broadcasted_iota(jnp.int32, (blk, 2 * blk), 1) < blk
    for sb in range(tq // blk):
        q = q_ref[sb * blk:(sb + 1) * blk, :]
        if sb == 0:
            kw = jnp.concatenate([kp_ref[...], kc_ref[0:blk, :]], axis=0)
            vw = jnp.concatenate([vp_ref[...], vc_ref[0:blk, :]], axis=0)
        else:
            kw = kc_ref[(sb - 1) * blk:(sb + 1) * blk, :]
            vw = vc_ref[(sb - 1) * blk:(sb + 1) * blk, :]
        outs = []
        for j in range(2):
            qm = jnp.where(lo_half if j == 0 else jnp.logical_not(lo_half), q, jnp.zeros_like(q))
            s = _dot_nt(qm, kw) + bias_ref[j]
            if sb == 0:
                s = jnp.where(jnp.logical_and(in_prev, i == 0), NEG, s)
            sink = sink_ref[2 * p + j]
            m = jnp.maximum(jnp.max(s, axis=1, keepdims=True), sink)
            e = jnp.exp(s - m)
            den = jnp.sum(e, axis=1, keepdims=True) + jnp.exp(sink - m)
            outs.append(_dot(e.astype(BF16), vw) / den)
        o_ref[sb * blk:(sb + 1) * blk, :] = jnp.where(lo_half, outs[0], outs[1]).astype(o_ref.dtype)


def _swa_attention(qkv, bias, sinks, batch, seq):
    t = qkv.shape[0]
    tq = _tile(seq, 512)
    nt = seq // tq
    pairs = SWA_HEADS // 2
    group = pairs // SWA_KV_HEADS
    kcol = pairs
    vcol = pairs + SWA_KV_HEADS
    per = tq // SWA_BLOCK

    def prev_idx(b, i, p):
        return jnp.maximum(b * (seq // SWA_BLOCK) + i * per - 1, 0)

    return pl.pallas_call(
        functools.partial(_swa_kernel, tq=tq),
        grid=(batch, nt, pairs),
        in_specs=[pl.BlockSpec(memory_space=pltpu.SMEM),
                  pl.BlockSpec((tq, LANES), lambda b, i, p: (b * nt + i, p)),
                  pl.BlockSpec((tq, LANES), lambda b, i, p: (b * nt + i, kcol + p // group)),
                  pl.BlockSpec((SWA_BLOCK, LANES), lambda b, i, p: (prev_idx(b, i, p), kcol + p // group)),
                  pl.BlockSpec((tq, LANES), lambda b, i, p: (b * nt + i, vcol + p // group)),
                  pl.BlockSpec((SWA_BLOCK, LANES), lambda b, i, p: (prev_idx(b, i, p), vcol + p // group)),
                  pl.BlockSpec((2, SWA_BLOCK, 2 * SWA_BLOCK), lambda b, i, p: (p, 0, 0))],
        out_specs=pl.BlockSpec((tq, LANES), lambda b, i, p: (b * nt + i, p)),
        out_shape=jax.ShapeDtypeStruct((t, SWA_HEADS * HEAD_DIM), BF16),
        compiler_params=_params("parallel", "parallel", "arbitrary"),
        name="swa_attention",
    )(sinks, qkv, qkv, qkv, qkv, qkv, bias)


def _swa_weights(w_qkv, b_qkv):
    nq = SWA_HEADS * HEAD_DIM
    kv = SWA_KV_HEADS * HEAD_DIM
    dup = np.concatenate([np.tile(np.arange(HEAD_DIM), 2) + h * HEAD_DIM for h in range(SWA_KV_HEADS)])
    cols = np.concatenate([np.arange(nq), nq + dup, nq + kv + dup])
    scale = np.concatenate([np.full(nq, HEAD_DIM ** -0.5), np.ones(2 * dup.size)]).astype(np.float32)
    return w_qkv[:, cols].astype(BF16), b_qkv[cols], jnp.asarray(scale)


def _staggered(units, stages):
    vals = list(units)
    for t in range(len(units) + len(stages) - 1):
        for k, stage in enumerate(stages):
            u = t - k
            if 0 <= u < len(units):
                vals[u] = stage(units[u], vals[u])


def _causal_sweep(i, tq, tk, stages, strict):
    nsub = tq // tk
    units = []
    for sb in reversed(range(nsub)):
        nl = tq - sb * tk
        key = lax.broadcasted_iota(jnp.int32, (tk, nl), 0)
        qry = lax.broadcasted_iota(jnp.int32, (tk, nl), 1)
        mask = key < qry if strict else key <= qry
        units += [(i * nsub + sb, h, slice(sb * tk, tq), mask) for h in range(2)]
    _staggered(units, stages)

    def body(n, carry):
        units = [((i - 1 - n) * nsub + sb, h, slice(0, tq), None)
                 for sb in reversed(range(nsub)) for h in range(2)]
        _staggered(units, stages)
        return carry

    lax.fori_loop(0, i, body, 0)


def _split_pair(q):
    lo_half = lax.broadcasted_iota(jnp.int32, q.shape, 1) < HEAD_DIM
    zero = jnp.zeros_like(q)
    return jnp.where(lo_half, q, zero), jnp.where(lo_half, zero, q)


def _sb_kernel(q_ref, k_ref, vt_ref, u_ref, o_ref, acc_ref, c_ref, *, tq, tk):
    i = pl.program_id(2)
    qms = _split_pair(q_ref[...])
    u = u_ref[...]
    acc_ref[...] = jnp.zeros_like(acc_ref)
    c_ref[...] = jnp.zeros_like(c_ref)

    def scores(unit, _):
        jb, h, lanes, _ = unit
        kt = k_ref[pl.ds(pl.multiple_of(jb * tk, tk), tk), :]
        return _dot_nt(kt, qms[h][lanes])

    def log_fail(unit, z):
        neg_abs = lax.bitcast_convert_type(lax.bitcast_convert_type(z, jnp.uint32) | jnp.uint32(1 << 31), F32)
        sp = jnp.maximum(z, 0.0) + jnp.log2(1.0 + jnp.exp2(neg_abs))
        if unit[3] is not None:
            sp = jnp.where(unit[3], sp, 0.0)
        hi = sp.astype(BF16)
        return z, jnp.concatenate([hi, (sp - hi.astype(F32)).astype(BF16)], axis=0)

    def suffix_sum(unit, val):
        z, hilo = val
        return z, _dot(u, hilo)

    def weights(unit, val):
        _, h, lanes, mask = unit
        z, r = val
        c = c_ref[h:h + 1, lanes]
        la = z - r - c
        if mask is not None:
            la = jnp.where(mask, la, NEG)
        c_ref[h:h + 1, lanes] = c + r[0:1, :]
        return jnp.exp2(la).astype(BF16)

    def values(unit, a):
        jb, h, lanes, _ = unit
        rows = slice(h * HEAD_DIM, (h + 1) * HEAD_DIM)
        acc_ref[rows, lanes] += _dot(vt_ref[jb, rows, :], a)

    _causal_sweep(i, tq, tk, (scores, log_fail, suffix_sum, weights, values), strict=True)
    o_ref[...] = acc_ref[...].T.astype(o_ref.dtype)


def _sb_attention(qk, vt, batch, seq):
    t = qk.shape[0]
    tq = _tile(seq, ATT_TQ)
    tk = vt.shape[2]
    nq = seq // tq
    pairs = SB_HEADS // 2
    tri = np.triu(np.ones((tk, tk), np.float32))
    u = jnp.asarray(np.concatenate([tri, tri], axis=1), BF16)
    return pl.pallas_call(
        functools.partial(_sb_kernel, tq=tq, tk=tk),
        grid=(batch, pairs, nq),
        in_specs=[pl.BlockSpec((tq, LANES), lambda b, p, i: (b * nq + i, p)),
                  pl.BlockSpec((seq, LANES), lambda b, p, i: (b, pairs + p)),
                  pl.BlockSpec((seq // tk, LANES, tk), lambda b, p, i: (b, p, 0)),
                  pl.BlockSpec((tk, 2 * tk), lambda b, p, i: (0, 0))],
        out_specs=pl.BlockSpec((tq, LANES), lambda b, p, i: (b * nq + i, p)),
        out_shape=jax.ShapeDtypeStruct((t, SB_HEADS * HEAD_DIM), BF16),
        scratch_shapes=[pltpu.VMEM((LANES, tq), F32), pltpu.VMEM((8, tq), F32)],
        compiler_params=_params("parallel", "parallel", "arbitrary"),
        name="sb_attention",
    )(qk, qk, vt, u)


MLA_QN = MLA_HEADS * MLA_NOPE_DIM
MLA_QR = MLA_HEADS * MLA_ROPE_DIM
MLA_OUT = 2 * MLA_QN + MLA_QR + LANES


def _mla_proj_kernel(x_ref, g_ref, wd_ref, qg_ref, wuq_ref, kvg_ref, wukv_ref, cs_ref, o_ref, vt_ref,
                     *, scale, tk):
    h = _rms(x_ref[...], g_ref[...]).astype(BF16)
    c = _dot(h, wd_ref[...])
    cos = cs_ref[:, :LANES]
    sin = cs_ref[:, LANES:]
    cq = _rms(c[:, :MLA_Q_RANK], qg_ref[...]).astype(BF16)
    q = _dot(cq, wuq_ref[...])
    o_ref[:, :MLA_QN] = (q[:, :MLA_QN] * scale).astype(o_ref.dtype)
    for m in range(MLA_QR // LANES):
        a = MLA_QN + m * LANES
        rot = q[:, a:a + LANES] * cos + q[:, a + MLA_QR:a + MLA_QR + LANES] * sin
        o_ref[:, a:a + LANES] = (rot * scale).astype(o_ref.dtype)
    kv0 = MLA_Q_RANK + MLA_KV_RANK
    ckv = _rms(c[:, MLA_Q_RANK:kv0], kvg_ref[...]).astype(BF16)
    kv = _dot(ckv, wukv_ref[...])
    kn0 = MLA_QN + MLA_QR
    o_ref[:, kn0:kn0 + MLA_QN] = kv[:, :MLA_QN].astype(o_ref.dtype)
    kr = c[:, kv0:kv0 + LANES] * cos + c[:, kv0 + LANES:kv0 + 2 * LANES] * sin
    o_ref[:, MLA_OUT - LANES:] = kr.astype(o_ref.dtype)
    _store_vt(vt_ref, kv[:, MLA_QN:], tk)


def _mla_rope_layout():
    half = MLA_ROPE_DIM // 2
    per = LANES // MLA_ROPE_DIM
    dq = MLA_NOPE_DIM + MLA_ROPE_DIM
    q_nope = np.concatenate([h * dq + np.arange(MLA_NOPE_DIM) for h in range(MLA_HEADS)])
    q_rope = np.zeros(MLA_QR, np.int64)
    q_swap = np.zeros(MLA_QR, np.int64)
    for h in range(MLA_HEADS):
        m, r = divmod(h, per)
        x1 = h * dq + MLA_NOPE_DIM + np.arange(half)
        x2 = x1 + half
        first = m * LANES + r * half + np.arange(half)
        second = first + LANES // 2
        q_rope[first], q_rope[second] = x1, x2
        q_swap[first], q_swap[second] = x2, x1
    k_rope = np.zeros(LANES, np.int64)
    k_swap = np.zeros(LANES, np.int64)
    base = MLA_Q_RANK + MLA_KV_RANK
    for r in range(per):
        first = r * half + np.arange(half)
        second = first + LANES // 2
        k_rope[first], k_rope[second] = base + np.arange(half), base + half + np.arange(half)
        k_swap[first], k_swap[second] = base + half + np.arange(half), base + np.arange(half)
    dkv = MLA_NOPE_DIM + HEAD_DIM
    k_nope = np.concatenate([h * dkv + np.arange(MLA_NOPE_DIM) for h in range(MLA_HEADS)])
    v = k_nope + MLA_NOPE_DIM
    down = np.concatenate([np.arange(base), k_rope, k_swap])
    return np.concatenate([q_nope, q_rope, q_swap]), np.concatenate([k_nope, v]), down


def _mla_rope_tables(seq):
    half = MLA_ROPE_DIM // 2
    inv = ROPE_THETA ** (-jnp.arange(0, MLA_ROPE_DIM, 2, dtype=F32) / MLA_ROPE_DIM)
    ang = jnp.arange(seq, dtype=F32)[:, None] * inv[None, :]
    cos, sin = jnp.cos(ang), jnp.sin(ang)
    reps = LANES // 2 // half
    cos_t = jnp.tile(cos, (1, 2 * reps))
    sin_t = jnp.concatenate([jnp.tile(-sin, (1, reps)), jnp.tile(sin, (1, reps))], axis=1)
    return jnp.concatenate([cos_t, sin_t], axis=1)


def _mla_proj(x, g, w_down, q_norm, w_uq, kv_norm, w_ukv, seq):
    t, d = x.shape
    tm = _tile(seq, 512)
    tk = _tile(seq, ATT_TK)
    ns = seq // tm
    uq_cols, ukv_cols, down_cols = _mla_rope_layout()
    wd = w_down[:, down_cols].astype(BF16)
    wuq = w_uq[:, uq_cols].astype(BF16)
    wukv = w_ukv[:, ukv_cols].astype(BF16)
    cs = _mla_rope_tables(seq)
    scale = (MLA_NOPE_DIM + MLA_ROPE_DIM) ** -0.5 * LOG2E
    full = lambda a: pl.BlockSpec(a.shape, lambda i: (0, 0))
    g2, qg, kvg = g.reshape(1, d), q_norm.reshape(1, -1), kv_norm.reshape(1, -1)
    return pl.pallas_call(
        functools.partial(_mla_proj_kernel, scale=scale, tk=tk),
        grid=(t // tm,),
        in_specs=[pl.BlockSpec((tm, d), lambda i: (i, 0)), full(g2), full(wd), full(qg), full(wuq),
                  full(kvg), full(wukv), pl.BlockSpec((tm, 2 * LANES), lambda i: (i % ns, 0))],
        out_specs=[pl.BlockSpec((tm, MLA_OUT), lambda i: (i, 0)),
                   pl.BlockSpec((tm // tk, MLA_QN, tk), lambda i: (i, 0, 0))],
        out_shape=[jax.ShapeDtypeStruct((t, MLA_OUT), BF16),
                   jax.ShapeDtypeStruct((t // tk, MLA_QN, tk), BF16)],
        compiler_params=_params("parallel"),
        name="mla_proj",
    )(x, g2, wd, qg, wuq, kvg, wukv, cs)


def _mla_kernel(qn_ref, qr_ref, kn_ref, kr_ref, vt_ref, o_ref, acc_ref, ml_ref, *, tq, tk):
    p = pl.program_id(1)
    i = pl.program_id(2)
    q = jnp.concatenate([qn_ref[...], qr_ref[...]], axis=1)
    lane = lax.broadcasted_iota(jnp.int32, (tq, 2 * LANES), 1)
    half = MLA_ROPE_DIM // 2
    zero = jnp.zeros_like(q)
    qms = []
    for j in range(2):
        r0 = LANES + (2 * (p % 2) + j) * half
        sel = jnp.logical_and(lane >= j * HEAD_DIM, lane < (j + 1) * HEAD_DIM)
        sel = jnp.logical_or(sel, jnp.logical_and(lane >= r0, lane < r0 + half))
        r1 = r0 + LANES // 2
        sel = jnp.logical_or(sel, jnp.logical_and(lane >= r1, lane < r1 + half))
        qms.append(jnp.where(sel, q, zero))
    acc_ref[...] = jnp.zeros_like(acc_ref)
    row = lax.broadcasted_iota(jnp.int32, ml_ref.shape, 0)
    ml_ref[...] = jnp.where(row < 2, NEG, 0.0)

    def scores(unit, _):
        jb, h, lanes, _ = unit
        start = pl.multiple_of(jb * tk, tk)
        kt = jnp.concatenate([kn_ref[pl.ds(start, tk), :], kr_ref[pl.ds(start, tk), :]], axis=1)
        return _dot_nt(kt, qms[h][lanes])

    def softmax(unit, s):
        _, h, lanes, mask = unit
        if mask is not None:
            s = jnp.where(mask, s, NEG)
        m_old = ml_ref[h:h + 1, lanes]
        m_new = jnp.maximum(m_old, jnp.max(s, axis=0, keepdims=True))
        alpha = jnp.exp2(m_old - m_new)
        e = jnp.exp2(s - m_new)
        ml_ref[h:h + 1, lanes] = m_new
        ml_ref[2 + h:3 + h, lanes] = alpha * ml_ref[2 + h:3 + h, lanes] + jnp.sum(e, axis=0, keepdims=True)
        return alpha, e.astype(BF16)

    def values(unit, val):
        jb, h, lanes, _ = unit
        alpha, e = val
        rows = slice(h * HEAD_DIM, (h + 1) * HEAD_DIM)
        acc_ref[rows, lanes] = alpha * acc_ref[rows, lanes] + _dot(vt_ref[jb, rows, :], e)

    _causal_sweep(i, tq, tk, (scores, softmax, values), strict=False)
    head = lax.broadcasted_iota(jnp.int32, acc_ref.shape, 0) < HEAD_DIM
    denom = jnp.where(head, ml_ref[2:3, :], ml_ref[3:4, :])
    o_ref[...] = (acc_ref[...] / denom).T.astype(o_ref.dtype)


def _mla_attention(a, vt, batch, seq):
    t = a.shape[0]
    tq = _tile(seq, ATT_TQ)
    tk = vt.shape[2]
    nq = seq // tq
    pairs = MLA_HEADS // 2
    qr0 = MLA_QN // LANES
    kn0 = (MLA_QN + MLA_QR) // LANES
    kr0 = kn0 + pairs
    return pl.pallas_call(
        functools.partial(_mla_kernel, tq=tq, tk=tk),
        grid=(batch, pairs, nq),
        in_specs=[pl.BlockSpec((tq, LANES), lambda b, p, i: (b * nq + i, p)),
                  pl.BlockSpec((tq, LANES), lambda b, p, i: (b * nq + i, qr0 + p // 2)),
                  pl.BlockSpec((seq, LANES), lambda b, p, i: (b, kn0 + p)),
                  pl.BlockSpec((seq, LANES), lambda b, p, i: (b, kr0)),
                  pl.BlockSpec((seq // tk, LANES, tk), lambda b, p, i: (b, p, 0))],
        out_specs=pl.BlockSpec((tq, LANES), lambda b, p, i: (b * nq + i, p)),
        out_shape=jax.ShapeDtypeStruct((t, MLA_HEADS * HEAD_DIM), BF16),
        scratch_shapes=[pltpu.VMEM((LANES, tq), F32), pltpu.VMEM((8, tq), F32)],
        compiler_params=_params("parallel", "parallel", "arbitrary"),
        name="mla_attention",
    )(a, a, a, a, vt)


FFN_CHUNK = 256


def _swiglu_into(acc_ref, h, wg, wu, wd):
    def gate_up(cols, _):
        return _dot(h, wg(cols)), _dot(h, wu(cols))

    def activate(cols, gu):
        g, u = gu
        return (g * jax.nn.sigmoid(g) * u).astype(BF16)

    def down(cols, a):
        acc_ref[...] += _dot(a, wd(cols))

    return gate_up, activate, down


def _run_swiglu(acc_ref, h, wg, wu, wd, width):
    fc = FFN_CHUNK if width % FFN_CHUNK == 0 else width
    chunks = [slice(c, c + fc) for c in range(0, width, fc)]
    _staggered(chunks, _swiglu_into(acc_ref, h, wg, wu, wd))


def _ffn_kernel(x_ref, g_ref, wg_ref, wu_ref, wd_ref, o_ref, h_ref):
    f = pl.program_id(1)

    @pl.when(f == 0)
    def _():
        x = x_ref[...]
        h_ref[...] = _rms(x, g_ref[...]).astype(BF16)
        o_ref[...] = x

    _run_swiglu(o_ref, h_ref[...], lambda c: wg_ref[:, c], lambda c: wu_ref[:, c], lambda c: wd_ref[c, :],
                wg_ref.shape[1])


def _ffn(x, g, wg, wu, wd):
    t, d = x.shape
    fdim = wg.shape[1]
    tm = _tile(t, 512)
    tf = fdim
    return pl.pallas_call(
        _ffn_kernel,
        grid=(t // tm, fdim // tf),
        in_specs=[pl.BlockSpec((tm, d), lambda i, f: (i, 0)),
                  pl.BlockSpec((1, d), lambda i, f: (0, 0)),
                  pl.BlockSpec((d, tf), lambda i, f: (0, f)),
                  pl.BlockSpec((d, tf), lambda i, f: (0, f)),
                  pl.BlockSpec((tf, d), lambda i, f: (f, 0))],
        out_specs=pl.BlockSpec((tm, d), lambda i, f: (i, 0)),
        out_shape=jax.ShapeDtypeStruct((t, d), F32),
        scratch_shapes=[pltpu.VMEM((tm, d), BF16)],
        compiler_params=_params("parallel", "arbitrary"),
        name="dense_ffn",
    )(x, g.reshape(1, d), wg.astype(BF16), wu.astype(BF16), wd.astype(BF16))


MOE_TILE = 512
META_ROWS = 8
ROUTE_SUB = 128
COMBINE_SUB = 256


def _router_kernel(x_ref, g_ref, rt_ref, su_ref, eye_ref, hb_ref, meta_ref, metac_ref, cnt_ref, carry_ref):
    @pl.when(pl.program_id(0) == 0)
    def _():
        carry_ref[...] = jnp.zeros_like(carry_ref)

    h = _rms(x_ref[...], g_ref[...])
    hb_ref[...] = h.astype(BF16)
    logits = lax.dot_general(rt_ref[...], h, _NT, precision=lax.Precision.HIGHEST,
                             preferred_element_type=F32)
    ne, ct = logits.shape
    eio = lax.broadcasted_iota(jnp.int32, (ne, ct), 0).astype(F32)
    m1 = jnp.max(logits, axis=0, keepdims=True)
    i1 = jnp.min(jnp.where(logits == m1, eio, float(ne)), axis=0, keepdims=True)
    rest = jnp.where(eio == i1, -jnp.inf, logits)
    m2 = jnp.max(rest, axis=0, keepdims=True)
    i2 = jnp.min(jnp.where(rest == m2, eio, float(ne)), axis=0, keepdims=True)
    e2 = jnp.exp(m2 - m1)
    g1 = 1.0 / (1.0 + e2)
    g2 = e2 / (1.0 + e2)
    oh1 = (eio == i1).astype(F32)
    oh2 = (eio == i2).astype(F32)
    both = oh1 + oh2
    seen = carry_ref[:, 0:1] + _dot(both.astype(BF16), su_ref[...])
    r1 = jnp.sum(oh1 * seen, axis=0, keepdims=True)
    r2 = jnp.sum(oh2 * seen, axis=0, keepdims=True)
    meta = jnp.zeros((META_ROWS, ct), F32)
    for row, val in enumerate((i1, i2, r1, r2, g1, g2)):
        meta = jnp.where(eio == float(row), val, meta)
    meta_ref[...] = meta
    metac_ref[...] = lax.dot_general(eye_ref[...], meta, _NT, precision=lax.Precision.HIGHEST,
                                     preferred_element_type=F32)
    carry_ref[...] = carry_ref[...] + jnp.sum(both, axis=1, keepdims=True)
    cnt_ref[0] = carry_ref[...]


def _router(x, g, router):
    t, d = x.shape
    ct = _tile(t, MOE_TILE)
    nc = t // ct
    su = jnp.asarray(np.triu(np.ones((ct, ct), np.float32), 1), BF16)
    eye = jnp.eye(ct, dtype=F32)
    return pl.pallas_call(
        _router_kernel,
        grid=(nc,),
        in_specs=[pl.BlockSpec((ct, d), lambda i: (i, 0)),
                  pl.BlockSpec((1, d), lambda i: (0, 0)),
                  pl.BlockSpec((N_EXPERTS, d), lambda i: (0, 0)),
                  pl.BlockSpec((ct, ct), lambda i: (0, 0)),
                  pl.BlockSpec((ct, ct), lambda i: (0, 0))],
        out_specs=[pl.BlockSpec((ct, d), lambda i: (i, 0)),
                   pl.BlockSpec((META_ROWS, ct), lambda i: (0, i)),
                   pl.BlockSpec((ct, META_ROWS), lambda i: (i, 0)),
                   pl.BlockSpec((1, N_EXPERTS, LANES), lambda i: (i, 0, 0))],
        out_shape=[jax.ShapeDtypeStruct((t, d), BF16),
                   jax.ShapeDtypeStruct((META_ROWS, t), F32),
                   jax.ShapeDtypeStruct((t, META_ROWS), F32),
                   jax.ShapeDtypeStruct((nc, N_EXPERTS, LANES), F32)],
        scratch_shapes=[pltpu.VMEM((N_EXPERTS, LANES), F32)],
        compiler_params=_params("arbitrary"),
        name="moe_router",
    )(x, g.reshape(1, d), router.T, su, eye)


def _moe_plan(cnt_after, t, ct):
    nc = t // ct
    rows = 2 * t + N_EXPERTS * ct
    nrt = rows // ct
    counts = cnt_after[:, :, 0].astype(jnp.int32)
    total = counts[-1]
    padded = (total + ct - 1) // ct * ct
    off_end = jnp.cumsum(padded)
    off = off_end - padded
    before = jnp.concatenate([jnp.zeros((1, N_EXPERTS), jnp.int32), counts[:-1]], axis=0)
    chunk_start = (off[:, None] + before.T).reshape(-1)
    tile_start = jnp.arange(nrt, dtype=jnp.int32) * ct
    starts = jnp.sort(jnp.concatenate([chunk_start, tile_start]))
    ends = jnp.concatenate([starts[1:], jnp.array([rows], jnp.int32)])
    item_r = jnp.minimum(starts // ct, nrt - 1)
    count_le = lambda sorted_vals, x: jnp.sum((sorted_vals[None, :] <= x[:, None]).astype(jnp.int32), axis=1)
    item_c = (count_le(chunk_start, starts) - 1) % nc
    first = lambda key: jnp.concatenate([jnp.ones((1,), jnp.int32), (key[1:] != key[:-1]).astype(jnp.int32)])
    dispatch = (item_r, item_c, starts, ends, first(item_r))
    order = jnp.argsort(item_c * (rows + 1) + starts)
    c2 = item_c[order]
    combine = (c2, item_r[order], starts[order], ends[order], first(c2))
    tile_expert = jnp.minimum(count_le(off_end, tile_start), N_EXPERTS - 1)
    tile_valid = (tile_start < off_end[-1]).astype(jnp.int32)
    return rows, off.astype(jnp.int32), dispatch, combine, tile_expert.astype(jnp.int32), tile_valid


def _positions(expert, rank, off_ref):
    base = jnp.zeros_like(rank)
    for e in range(N_EXPERTS):
        base = jnp.where(expert == float(e), off_ref[e].astype(F32), base)
    return base + rank


def _dispatch_kernel(r_ref, c_ref, s_ref, e_ref, first_ref, off_ref, meta_ref, h_ref, xs_ref, gate_ref):
    n = pl.program_id(0)
    rt, ct = xs_ref.shape[0], h_ref.shape[0]

    @pl.when(first_ref[n] == 1)
    def _():
        xs_ref[...] = jnp.zeros_like(xs_ref)
        gate_ref[...] = jnp.zeros_like(gate_ref)

    meta = meta_ref[...]
    pos0 = _positions(meta[0:1], meta[2:3], off_ref)
    pos1 = _positions(meta[1:2], meta[3:4], off_ref)
    sub = min(rt, ROUTE_SUB)
    for sb in range(rt // sub):
        base = r_ref[n] * rt + sb * sub

        @pl.when(jnp.logical_and(base < e_ref[n], base + sub > s_ref[n]))
        def _():
            row = lax.broadcasted_iota(jnp.int32, (sub, ct), 0) + base
            live = jnp.logical_and(row >= s_ref[n], row < e_ref[n])
            rowf = row.astype(F32)
            sel0 = jnp.logical_and(pos0 == rowf, live)
            sel1 = jnp.logical_and(pos1 == rowf, live)
            sel = jnp.logical_or(sel0, sel1).astype(F32).astype(BF16)
            rows = slice(sb * sub, (sb + 1) * sub)
            xs_ref[rows, :] += _dot(sel, h_ref[...]).astype(xs_ref.dtype)
            gate = jnp.where(sel0, meta[4:5], 0.0) + jnp.where(sel1, meta[5:6], 0.0)
            gate_ref[rows, :] += jnp.sum(gate, axis=1, keepdims=True)


def _dispatch(plan, off, meta, hb, rows, ct):
    t, d = hb.shape
    n_items = plan[0].shape[0]
    return pl.pallas_call(
        _dispatch_kernel,
        grid_spec=pltpu.PrefetchScalarGridSpec(
            num_scalar_prefetch=6,
            grid=(n_items,),
            in_specs=[pl.BlockSpec((META_ROWS, ct), lambda n, r, c, *_: (0, c[n])),
                      pl.BlockSpec((ct, d), lambda n, r, c, *_: (c[n], 0))],
            out_specs=[pl.BlockSpec((ct, d), lambda n, r, c, *_: (r[n], 0)),
                       pl.BlockSpec((ct, 1), lambda n, r, c, *_: (r[n], 0))]),
        out_shape=[jax.ShapeDtypeStruct((rows, d), BF16),
                   jax.ShapeDtypeStruct((rows, 1), F32)],
        compiler_params=_params("arbitrary"),
        name="moe_dispatch",
    )(*plan, off, meta, hb)


def _expert_kernel(te_ref, tv_ref, xs_ref, gate_ref, wg_ref, wu_ref, wd_ref, y_ref, acc_ref):
    r = pl.program_id(0)
    f = pl.program_id(1)

    @pl.when(f == 0)
    def _():
        acc_ref[...] = jnp.zeros_like(acc_ref)

    @pl.when(tv_ref[r] == 1)
    def _():
        _run_swiglu(acc_ref, xs_ref[...], lambda c: wg_ref[0, :, c], lambda c: wu_ref[0, :, c],
                    lambda c: wd_ref[0, c, :], wg_ref.shape[2])

    @pl.when(f == pl.num_programs(1) - 1)
    def _():
        y_ref[...] = (acc_ref[...] * gate_ref[...]).astype(y_ref.dtype)


def _experts(tile_expert, tile_valid, xs, gate, wg, wu, wd, ct):
    rows, d = xs.shape
    fdim = wg.shape[2]
    tf = _tile(fdim, 1792)
    nf = fdim // tf

    def wcol(r, f, te, tv):
        return (te[r], 0, jnp.where(tv[r] == 1, f, nf - 1))

    def wrow(r, f, te, tv):
        return (te[r], jnp.where(tv[r] == 1, f, nf - 1), 0)

    return pl.pallas_call(
        _expert_kernel,
        grid_spec=pltpu.PrefetchScalarGridSpec(
            num_scalar_prefetch=2,
            grid=(rows // ct, nf),
            in_specs=[pl.BlockSpec((ct, d), lambda r, f, te, tv: (r, 0)),
                      pl.BlockSpec((ct, 1), lambda r, f, te, tv: (r, 0)),
                      pl.BlockSpec((1, d, tf), wcol),
                      pl.BlockSpec((1, d, tf), wcol),
                      pl.BlockSpec((1, tf, d), wrow)],
            out_specs=pl.BlockSpec((ct, d), lambda r, f, te, tv: (r, 0)),
            scratch_shapes=[pltpu.VMEM((ct, d), F32)]),
        out_shape=jax.ShapeDtypeStruct((rows, d), BF16),
        compiler_params=_params("arbitrary", "arbitrary"),
        name="moe_experts",
    )(tile_expert, tile_valid, xs, gate, wg, wu, wd)


def _combine_kernel(c_ref, r_ref, s_ref, e_ref, first_ref, off_ref, x_ref, metac_ref, y_ref, o_ref):
    n = pl.program_id(0)
    ct, rt = x_ref.shape[0], y_ref.shape[0]

    @pl.when(first_ref[n] == 1)
    def _():
        o_ref[...] = x_ref[...]

    metac = metac_ref[...]
    pos0 = _positions(metac[:, 0:1], metac[:, 2:3], off_ref)
    pos1 = _positions(metac[:, 1:2], metac[:, 3:4], off_ref)
    sub = min(rt, COMBINE_SUB)
    for sb in range(rt // sub):
        base = r_ref[n] * rt + sb * sub

        @pl.when(jnp.logical_and(base < e_ref[n], base + sub > s_ref[n]))
        def _():
            row = lax.broadcasted_iota(jnp.int32, (ct, sub), 1) + base
            live = jnp.logical_and(row >= s_ref[n], row < e_ref[n])
            rowf = row.astype(F32)
            sel = jnp.logical_and(jnp.logical_or(pos0 == rowf, pos1 == rowf), live).astype(F32).astype(BF16)
            o_ref[...] += _dot(sel, y_ref[sb * sub:(sb + 1) * sub, :])


def _combine(plan, off, x, metac, y, ct):
    t, d = x.shape
    n_items = plan[0].shape[0]
    return pl.pallas_call(
        _combine_kernel,
        grid_spec=pltpu.PrefetchScalarGridSpec(
            num_scalar_prefetch=6,
            grid=(n_items,),
            in_specs=[pl.BlockSpec((ct, d), lambda n, c, r, *_: (c[n], 0)),
                      pl.BlockSpec((ct, META_ROWS), lambda n, c, r, *_: (c[n], 0)),
                      pl.BlockSpec((ct, d), lambda n, c, r, *_: (r[n], 0))],
            out_specs=pl.BlockSpec((ct, d), lambda n, c, r, *_: (c[n], 0))),
        out_shape=jax.ShapeDtypeStruct((t, d), F32),
        compiler_params=_params("arbitrary"),
        name="moe_combine",
    )(*plan, off, x, metac, y)


def _moe(x, g, router, wg, wu, wd):
    t = x.shape[0]
    ct = _tile(t, MOE_TILE)
    hb, meta, metac, cnt = _router(x, g, router)
    rows, off, dispatch, combine, tile_expert, tile_valid = _moe_plan(cnt, t, ct)
    xs, gate = _dispatch(dispatch, off, meta, hb, rows, ct)
    y = _experts(tile_expert, tile_valid, xs, gate, wg.astype(BF16), wu.astype(BF16), wd.astype(BF16), ct)
    return _combine(combine, off, x, metac, y, ct)


def _final_norm_kernel(x_ref, g_ref, o_ref):
    o_ref[...] = _rms(x_ref[...], g_ref[...])


def _final_norm(x, g):
    t, d = x.shape
    tm = _tile(t, 1024)
    return pl.pallas_call(
        _final_norm_kernel,
        grid=(t // tm,),
        in_specs=[pl.BlockSpec((tm, d), lambda i: (i, 0)), pl.BlockSpec((1, d), lambda i: (0, 0))],
        out_specs=pl.BlockSpec((tm, d), lambda i: (i, 0)),
        out_shape=jax.ShapeDtypeStruct((t, d), F32),
        compiler_params=_params("parallel"),
        name="final_norm",
    )(x, g.reshape(1, d))


def kernel(x, rel_bias, attn_norm, ffn_norm, final_norm, swa_w_qkv, swa_b_qkv, swa_sinks, swa_w_o, sb_w_qkv, sb_w_o, mla_w_down, mla_q_norm, mla_w_uq, mla_kv_norm, mla_w_ukv, mla_w_o, ffn_w_gate, ffn_w_up, ffn_w_down, moe_router, moe_w_gate, moe_w_up, moe_w_down):
    batch, seq, d = x.shape
    depth = attn_norm.shape[0]
    xt = x.reshape(batch * seq, d)
    swa_bias = _swa_bias(rel_bias)
    for i in range(depth):
        mixer, j = i % N_MIXERS, i // N_MIXERS
        if mixer == 0:
            w, b, cs = _swa_weights(swa_w_qkv[j], swa_b_qkv[j])
            qkv = _norm_proj(xt, attn_norm[i], w, b, cs, "swa_proj")
            o = _swa_attention(qkv, swa_bias, swa_sinks[j], batch, seq)
            xt = _out_proj(xt, o, swa_w_o[j].astype(BF16), "swa_out")
        elif mixer == 1:
            n = sb_w_qkv.shape[2] // 3
            cs = jnp.concatenate([jnp.full((n,), HEAD_DIM ** -0.5 * LOG2E, F32), jnp.ones((n,), F32)])
            qk, vt = _norm_proj_vt(xt, attn_norm[i], sb_w_qkv[j].astype(BF16), cs, n, seq, "sb_proj")
            o = _sb_attention(qk, vt, batch, seq)
            xt = _out_proj(xt, o, sb_w_o[j].astype(BF16), "sb_out")
        else:
            a, vt = _mla_proj(xt, attn_norm[i], mla_w_down[j], mla_q_norm[j], mla_w_uq[j],
                              mla_kv_norm[j], mla_w_ukv[j], seq)
            o = _mla_attention(a, vt, batch, seq)
            xt = _out_proj(xt, o, mla_w_o[j].astype(BF16), "mla_out")
        f = i // 2
        if i % 2 == 0:
            xt = _ffn(xt, ffn_norm[i], ffn_w_gate[f], ffn_w_up[f], ffn_w_down[f])
        else:
            xt = _moe(xt, ffn_norm[i], moe_router[f], moe_w_gate[f], moe_w_up[f], moe_w_down[f])
    return _final_norm(xt, final_norm).reshape(batch, seq, d)
```

```python
import functools
import math

import numpy as np
import jax
import jax.numpy as jnp
from jax import lax
from jax.experimental import pallas as pl
from jax.experimental.pallas import tpu as pltpu

F32 = jnp.float32
BF16 = jnp.bfloat16

N_MIXERS = 3
RMS_EPS = 1e-6
SWA_HEADS = 16
SWA_KV_HEADS = 2
HEAD_DIM = 64
SWA_BLOCK = 128
SWA_WINDOW = 128
REL_BUCKETS = 32
REL_MAX_DIST = 128
SB_HEADS = 16
MLA_HEADS = 16
MLA_NOPE_DIM = 64
MLA_ROPE_DIM = 32
MLA_Q_RANK = 384
MLA_KV_RANK = 256
ROPE_THETA = 10000.0
N_EXPERTS = 8

LANES = 128
NEG = -1e30
LOG2E = math.log2(math.e)
VMEM_LIMIT = 56 * 1024 * 1024

_NT = (((1,), (1,)), ((), ()))


def _dot(a, b):
    return jnp.dot(a, b, preferred_element_type=F32)


def _dot_nt(a, b):
    return lax.dot_general(a, b, _NT, preferred_element_type=F32)


def _rms(x, g):
    return x * lax.rsqrt(jnp.mean(x * x, axis=-1, keepdims=True) + RMS_EPS) * g


def _params(*sem):
    return pltpu.CompilerParams(dimension_semantics=sem, vmem_limit_bytes=VMEM_LIMIT)


def _tile(n, pref):
    return pref if n % pref == 0 else n


def _norm_proj_kernel(x_ref, g_ref, w_ref, b_ref, cs_ref, o_ref):
    h = _rms(x_ref[...], g_ref[...]).astype(BF16)
    y = (_dot(h, w_ref[...]) + b_ref[...]) * cs_ref[...]
    o_ref[...] = y.astype(o_ref.dtype)


def _norm_proj(x, g, w, b, colscale, name):
    t, d = x.shape
    n = w.shape[1]
    tm = _tile(t, 512)
    return pl.pallas_call(
        _norm_proj_kernel,
        grid=(t // tm,),
        in_specs=[pl.BlockSpec((tm, d), lambda i: (i, 0)),
                  pl.BlockSpec((1, d), lambda i: (0, 0)),
                  pl.BlockSpec((d, n), lambda i: (0, 0)),
                  pl.BlockSpec((1, n), lambda i: (0, 0)),
                  pl.BlockSpec((1, n), lambda i: (0, 0))],
        out_specs=pl.BlockSpec((tm, n), lambda i: (i, 0)),
        out_shape=jax.ShapeDtypeStruct((t, n), BF16),
        compiler_params=_params("parallel"),
        name=name,
    )(x, g.reshape(1, d), w, b.reshape(1, n), colscale.reshape(1, n))


ATT_TQ = 512
ATT_TK = 256


def _store_vt(vt_ref, v, tk):
    for s in range(v.shape[0] // tk):
        vt_ref[s] = v[s * tk:(s + 1) * tk, :].T.astype(vt_ref.dtype)


def _norm_proj_vt_kernel(x_ref, g_ref, w_ref, cs_ref, o_ref, vt_ref, *, tk):
    h = _rms(x_ref[...], g_ref[...]).astype(BF16)
    y = _dot(h, w_ref[...])
    n = o_ref.shape[1]
    o_ref[...] = (y[:, :n] * cs_ref[...]).astype(o_ref.dtype)
    _store_vt(vt_ref, y[:, n:], tk)


def _norm_proj_vt(x, g, w, colscale, nv, seq, name):
    t, d = x.shape
    n = w.shape[1] - nv
    tm = _tile(seq, 512)
    tk = _tile(seq, ATT_TK)
    per = tm // tk
    return pl.pallas_call(
        functools.partial(_norm_proj_vt_kernel, tk=tk),
        grid=(t // tm,),
        in_specs=[pl.BlockSpec((tm, d), lambda i: (i, 0)),
                  pl.BlockSpec((1, d), lambda i: (0, 0)),
                  pl.BlockSpec((d, n + nv), lambda i: (0, 0)),
                  pl.BlockSpec((1, n), lambda i: (0, 0))],
        out_specs=[pl.BlockSpec((tm, n), lambda i: (i, 0)),
                   pl.BlockSpec((per, nv, tk), lambda i: (i, 0, 0))],
        out_shape=[jax.ShapeDtypeStruct((t, n), BF16),
                   jax.ShapeDtypeStruct((t // tk, nv, tk), BF16)],
        compiler_params=_params("parallel"),
        name=name,
    )(x, g.reshape(1, d), w, colscale.reshape(1, n))


def _out_proj_kernel(x_ref, o_ref, w_ref, y_ref):
    y_ref[...] = x_ref[...] + _dot(o_ref[...], w_ref[...])


def _out_proj(x, o, w, name):
    t, d = x.shape
    k = o.shape[1]
    tm = _tile(t, 512)
    return pl.pallas_call(
        _out_proj_kernel,
        grid=(t // tm,),
        in_specs=[pl.BlockSpec((tm, d), lambda i: (i, 0)),
                  pl.BlockSpec((tm, k), lambda i: (i, 0)),
                  pl.BlockSpec((k, d), lambda i: (0, 0))],
        out_specs=pl.BlockSpec((tm, d), lambda i: (i, 0)),
        out_shape=jax.ShapeDtypeStruct((t, d), F32),
        compiler_params=_params("parallel"),
        name=name,
    )(x, o, w)


def _t5_bucket_table():
    qi = np.arange(SWA_BLOCK)[:, None]
    kj = np.arange(2 * SWA_BLOCK)[None, :]
    dist = qi + SWA_BLOCK - kj
    d0 = np.maximum(dist, 0)
    max_exact = REL_BUCKETS // 2
    d = np.maximum(d0, 1).astype(np.float32)
    large = max_exact + (np.log(d / max_exact) / math.log(REL_MAX_DIST / max_exact)
                         * (REL_BUCKETS - max_exact)).astype(np.int32)
    large = np.minimum(large, REL_BUCKETS - 1)
    bucket = np.where(d0 < max_exact, d0, large)
    band = (dist >= 0) & (dist < SWA_WINDOW)
    return np.where(band, bucket, -1).astype(np.int32)


def _swa_bias_kernel(rel_ref, bucket_ref, o_ref):
    h = pl.program_id(0)
    bucket = bucket_ref[...]
    acc = jnp.full(bucket.shape, NEG, F32)
    for b in range(REL_BUCKETS):
        acc = jnp.where(bucket == b, rel_ref[b, h], acc)
    o_ref[0] = acc


def _swa_bias(rel_bias):
    bucket = jnp.asarray(_t5_bucket_table())
    q, k = bucket.shape
    return pl.pallas_call(
        _swa_bias_kernel,
        grid=(SWA_HEADS,),
        in_specs=[pl.BlockSpec(memory_space=pltpu.SMEM),
                  pl.BlockSpec((q, k), lambda h: (0, 0))],
        out_specs=pl.BlockSpec((1, q, k), lambda h: (h, 0, 0)),
        out_shape=jax.ShapeDtypeStruct((SWA_HEADS, q, k), F32),
        compiler_params=_params("arbitrary"),
        name="swa_bias",
    )(rel_bias, bucket)


def _swa_kernel(sink_ref, q_ref, kc_ref, kp_ref, vc_ref, vp_ref, bias_ref, o_ref, *, tq):
    i = pl.program_id(1)
    g = pl.program_id(2)
    blk = SWA_BLOCK
    heads = bias_ref.shape[0]
    lo_half = lax.broadcasted_iota(jnp.int32, (blk, LANES), 1) < HEAD_DIM
    in_prev = lax.broadcasted_iota(jnp.int32, (heads * blk, 2 * blk), 1) < blk
    bias = bias_ref[...].reshape(heads * blk, 2 * blk)

    def window(cur_ref, prev_ref, sb):
        if sb == 0:
            return jnp.concatenate([prev_ref[...], cur_ref[0:blk, :]], axis=0)
        return cur_ref[(sb - 1) * blk:(sb + 1) * blk, :]

    def scores(sb, _):
        q = q_ref[sb * blk:(sb + 1) * blk, :]
        parts = []
        for h in range(heads):
            qp = q[:, (h // 2) * LANES:(h // 2 + 1) * LANES]
            keep = lo_half if h % 2 == 0 else jnp.logical_not(lo_half)
            parts.append(jnp.where(keep, qp, jnp.zeros_like(qp)))
        s = _dot_nt(jnp.concatenate(parts, axis=0), window(kc_ref, kp_ref, sb)) + bias
        if sb == 0:
            s = jnp.where(jnp.logical_and(in_prev, i == 0), NEG, s)
        return s

    def softmax(sb, s):
        ps, inv = [], []
        for h in range(heads):
            sh = s[h * blk:(h + 1) * blk]
            sink = sink_ref[g * heads + h]
            m = jnp.maximum(jnp.max(sh, axis=1, keepdims=True), sink)
            e = jnp.exp(sh - m)
            inv.append(1.0 / (jnp.sum(e, axis=1, keepdims=True) + jnp.exp(sink - m)))
            ps.append(e.astype(BF16))
        return jnp.concatenate(ps, axis=0), inv

    def values(sb, val):
        p, inv = val
        o = _dot(p, window(vc_ref, vp_ref, sb))
        outs = [jnp.where(lo_half, o[h * blk:(h + 1) * blk] * inv[h], o[(h + 1) * blk:(h + 2) * blk] * inv[h + 1])
                for h in range(0, heads, 2)]
        o_ref[sb * blk:(sb + 1) * blk, :] = jnp.concatenate(outs, axis=1).astype(o_ref.dtype)

    _staggered(list(range(tq // blk)), (scores, softmax, values))


def _swa_attention(qkv, bias, sinks, batch, seq):
    t = qkv.shape[0]
    tq = _tile(seq, 512)
    nt = seq // tq
    group = SWA_HEADS // SWA_KV_HEADS
    gw = group * HEAD_DIM
    kcol = SWA_HEADS * HEAD_DIM // LANES
    vcol = kcol + SWA_KV_HEADS
    per = tq // SWA_BLOCK

    def prev_idx(b, i, g):
        return jnp.maximum(b * (seq // SWA_BLOCK) + i * per - 1, 0)

    return pl.pallas_call(
        functools.partial(_swa_kernel, tq=tq),
        grid=(batch, nt, SWA_KV_HEADS),
        in_specs=[pl.BlockSpec(memory_space=pltpu.SMEM),
                  pl.BlockSpec((tq, gw), lambda b, i, g: (b * nt + i, g)),
                  pl.BlockSpec((tq, LANES), lambda b, i, g: (b * nt + i, kcol + g)),
                  pl.BlockSpec((SWA_BLOCK, LANES), lambda b, i, g: (prev_idx(b, i, g), kcol + g)),
                  pl.BlockSpec((tq, LANES), lambda b, i, g: (b * nt + i, vcol + g)),
                  pl.BlockSpec((SWA_BLOCK, LANES), lambda b, i, g: (prev_idx(b, i, g), vcol + g)),
                  pl.BlockSpec((group, SWA_BLOCK, 2 * SWA_BLOCK), lambda b, i, g: (g, 0, 0))],
        out_specs=pl.BlockSpec((tq, gw), lambda b, i, g: (b * nt + i, g)),
        out_shape=jax.ShapeDtypeStruct((t, SWA_HEADS * HEAD_DIM), BF16),
        compiler_params=_params("parallel", "parallel", "arbitrary"),
        name="swa_attention",
    )(sinks, qkv, qkv, qkv, qkv, qkv, bias)


def _swa_weights(w_qkv, b_qkv):
    nq = SWA_HEADS * HEAD_DIM
    kv = SWA_KV_HEADS * HEAD_DIM
    dup = np.concatenate([np.tile(np.arange(HEAD_DIM), 2) + h * HEAD_DIM for h in range(SWA_KV_HEADS)])
    cols = np.concatenate([np.arange(nq), nq + dup, nq + kv + dup])
    scale = np.concatenate([np.full(nq, HEAD_DIM ** -0.5), np.ones(2 * dup.size)]).astype(np.float32)
    return w_qkv[:, cols].astype(BF16), b_qkv[cols], jnp.asarray(scale)


def _staggered(units, stages):
    vals = list(units)
    for t in range(len(units) + len(stages) - 1):
        for k, stage in enumerate(stages):
            u = t - k
            if 0 <= u < len(units):
                vals[u] = stage(units[u], vals[u])


def _score_ahead(units, next_units, stages, s_ref):
    rest = stages[1:]
    vals = [None] * len(units)
    for t in range(len(units) + len(rest) - 1):
        ahead = stages[0](next_units[t], None) if t < len(units) else None
        for k, stage in enumerate(rest):
            u = t - k
            if 0 <= u < len(units):
                if k == 0:
                    lanes = units[u][2]
                    vals[u] = stage(units[u], s_ref[u, :, :lanes.stop - lanes.start])
                    s_ref[u] = ahead
                else:
                    vals[u] = stage(units[u], vals[u])


def _causal_sweep(i, tq, tk, stages, strict, s_ref):
    nsub = tq // tk

    def group(j):
        first = jnp.maximum(j, 0) * nsub
        return [(first + sb, h, slice(0, tq), None) for sb in reversed(range(nsub)) for h in range(2)]

    diagonal = []
    for sb in reversed(range(nsub)):
        nl = tq - sb * tk
        key = lax.broadcasted_iota(jnp.int32, (tk, nl), 0)
        qry = lax.broadcasted_iota(jnp.int32, (tk, nl), 1)
        mask = key < qry if strict else key <= qry
        diagonal += [(i * nsub + sb, h, slice(sb * tk, tq), mask) for h in range(2)]
    for u, unit in enumerate(diagonal):
        s_ref[u, :, :tq - unit[2].start] = stages[0](unit, None)
    _score_ahead(diagonal, group(i - 1), stages, s_ref)

    def body(n, carry):
        _score_ahead(group(i - 1 - n), group(i - 2 - n), stages, s_ref)
        return carry

    lax.fori_loop(0, i, body, 0)


def _split_pair(q):
    lo_half = lax.broadcasted_iota(jnp.int32, q.shape, 1) < HEAD_DIM
    zero = jnp.zeros_like(q)
    return jnp.where(lo_half, q, zero), jnp.where(lo_half, zero, q)


def _sb_kernel(q_ref, k_ref, vt_ref, u_ref, o_ref, acc_ref, c_ref, s_ref, *, tq, tk):
    i = pl.program_id(2)
    qms = _split_pair(q_ref[...])
    u = u_ref[...]
    acc_ref[...] = jnp.zeros_like(acc_ref)
    c_ref[...] = jnp.zeros_like(c_ref)

    def scores(unit, _):
        jb, h, lanes, _ = unit
        kt = k_ref[pl.ds(pl.multiple_of(jb * tk, tk), tk), :]
        return _dot_nt(kt, qms[h][lanes])

    def log_fail(unit, z):
        neg_abs = lax.bitcast_convert_type(lax.bitcast_convert_type(z, jnp.uint32) | jnp.uint32(1 << 31), F32)
        sp = jnp.maximum(z, 0.0) + jnp.log2(1.0 + jnp.exp2(neg_abs))
        if unit[3] is not None:
            sp = jnp.where(unit[3], sp, 0.0)
        return z, sp.astype(BF16)

    def suffix_sum(unit, val):
        z, sp = val
        return z, _dot(u, sp)

    def weights(unit, val):
        _, h, lanes, mask = unit
        z, r = val
        c = c_ref[h:h + 1, lanes]
        la = z - r - c
        if mask is not None:
            la = jnp.where(mask, la, NEG)
        c_ref[h:h + 1, lanes] = c + r[0:1, :]
        return jnp.exp2(la).astype(BF16)

    def values(unit, a):
        jb, h, lanes, _ = unit
        rows = slice(h * HEAD_DIM, (h + 1) * HEAD_DIM)
        acc_ref[rows, lanes] += _dot(vt_ref[jb, rows, :], a)

    _causal_sweep(i, tq, tk, (scores, log_fail, suffix_sum, weights, values), True, s_ref)
    o_ref[...] = acc_ref[...].T.astype(o_ref.dtype)


def _sb_attention(qk, vt, batch, seq):
    t = qk.shape[0]
    tq = _tile(seq, ATT_TQ)
    tk = vt.shape[2]
    nq = seq // tq
    pairs = SB_HEADS // 2
    u = jnp.asarray(np.triu(np.ones((tk, tk), np.float32)), BF16)
    return pl.pallas_call(
        functools.partial(_sb_kernel, tq=tq, tk=tk),
        grid=(batch, pairs, nq),
        in_specs=[pl.BlockSpec((tq, LANES), lambda b, p, i: (b * nq + i, p)),
                  pl.BlockSpec((seq, LANES), lambda b, p, i: (b, pairs + p)),
                  pl.BlockSpec((seq // tk, LANES, tk), lambda b, p, i: (b, p, 0)),
                  pl.BlockSpec((tk, tk), lambda b, p, i: (0, 0))],
        out_specs=pl.BlockSpec((tq, LANES), lambda b, p, i: (b * nq + i, p)),
        out_shape=jax.ShapeDtypeStruct((t, SB_HEADS * HEAD_DIM), BF16),
        scratch_shapes=[pltpu.VMEM((LANES, tq), F32), pltpu.VMEM((8, tq), F32),
                        pltpu.VMEM((2 * (tq // tk), tk, tq), F32)],
        compiler_params=_params("parallel", "parallel", "arbitrary"),
        name="sb_attention",
    )(qk, qk, vt, u)


MLA_QN = MLA_HEADS * MLA_NOPE_DIM
MLA_QR = MLA_HEADS * MLA_ROPE_DIM
MLA_OUT = 2 * MLA_QN + MLA_QR + LANES


def _mla_proj_kernel(x_ref, g_ref, wd_ref, qg_ref, wuq_ref, kvg_ref, wukv_ref, cs_ref, o_ref, vt_ref,
                     *, scale, tk):
    h = _rms(x_ref[...], g_ref[...]).astype(BF16)
    c = _dot(h, wd_ref[...])
    cos = cs_ref[:, :LANES]
    sin = cs_ref[:, LANES:]
    cq = _rms(c[:, :MLA_Q_RANK], qg_ref[...]).astype(BF16)
    q = _dot(cq, wuq_ref[...])
    o_ref[:, :MLA_QN] = (q[:, :MLA_QN] * scale).astype(o_ref.dtype)
    for m in range(MLA_QR // LANES):
        a = MLA_QN + m * LANES
        rot = q[:, a:a + LANES] * cos + q[:, a + MLA_QR:a + MLA_QR + LANES] * sin
        o_ref[:, a:a + LANES] = (rot * scale).astype(o_ref.dtype)
    kv0 = MLA_Q_RANK + MLA_KV_RANK
    ckv = _rms(c[:, MLA_Q_RANK:kv0], kvg_ref[...]).astype(BF16)
    kv = _dot(ckv, wukv_ref[...])
    kn0 = MLA_QN + MLA_QR
    o_ref[:, kn0:kn0 + MLA_QN] = kv[:, :MLA_QN].astype(o_ref.dtype)
    kr = c[:, kv0:kv0 + LANES] * cos + c[:, kv0 + LANES:kv0 + 2 * LANES] * sin
    o_ref[:, MLA_OUT - LANES:] = kr.astype(o_ref.dtype)
    _store_vt(vt_ref, kv[:, MLA_QN:], tk)


def _mla_rope_layout():
    half = MLA_ROPE_DIM // 2
    per = LANES // MLA_ROPE_DIM
    dq = MLA_NOPE_DIM + MLA_ROPE_DIM
    q_nope = np.concatenate([h * dq + np.arange(MLA_NOPE_DIM) for h in range(MLA_HEADS)])
    q_rope = np.zeros(MLA_QR, np.int64)
    q_swap = np.zeros(MLA_QR, np.int64)
    for h in range(MLA_HEADS):
        m, r = divmod(h, per)
        x1 = h * dq + MLA_NOPE_DIM + np.arange(half)
        x2 = x1 + half
        first = m * LANES + r * half + np.arange(half)
        second = first + LANES // 2
        q_rope[first], q_rope[second] = x1, x2
        q_swap[first], q_swap[second] = x2, x1
    k_rope = np.zeros(LANES, np.int64)
    k_swap = np.zeros(LANES, np.int64)
    base = MLA_Q_RANK + MLA_KV_RANK
    for r in range(per):
        first = r * half + np.arange(half)
        second = first + LANES // 2
        k_rope[first], k_rope[second] = base + np.arange(half), base + half + np.arange(half)
        k_swap[first], k_swap[second] = base + half + np.arange(half), base + np.arange(half)
    dkv = MLA_NOPE_DIM + HEAD_DIM
    k_nope = np.concatenate([h * dkv + np.arange(MLA_NOPE_DIM) for h in range(MLA_HEADS)])
    v = k_nope + MLA_NOPE_DIM
    down = np.concatenate([np.arange(base), k_rope, k_swap])
    return np.concatenate([q_nope, q_rope, q_swap]), np.concatenate([k_nope, v]), down


def _mla_rope_tables(seq):
    half = MLA_ROPE_DIM // 2
    inv = ROPE_THETA ** (-jnp.arange(0, MLA_ROPE_DIM, 2, dtype=F32) / MLA_ROPE_DIM)
    ang = jnp.arange(seq, dtype=F32)[:, None] * inv[None, :]
    cos, sin = jnp.cos(ang), jnp.sin(ang)
    reps = LANES // 2 // half
    cos_t = jnp.tile(cos, (1, 2 * reps))
    sin_t = jnp.concatenate([jnp.tile(-sin, (1, reps)), jnp.tile(sin, (1, reps))], axis=1)
    return jnp.concatenate([cos_t, sin_t], axis=1)


def _mla_proj(x, g, w_down, q_norm, w_uq, kv_norm, w_ukv, seq):
    t, d = x.shape
    tm = _tile(seq, 512)
    tk = _tile(seq, ATT_TK)
    ns = seq // tm
    uq_cols, ukv_cols, down_cols = _mla_rope_layout()
    wd = w_down[:, down_cols].astype(BF16)
    wuq = w_uq[:, uq_cols].astype(BF16)
    wukv = w_ukv[:, ukv_cols].astype(BF16)
    cs = _mla_rope_tables(seq)
    scale = (MLA_NOPE_DIM + MLA_ROPE_DIM) ** -0.5 * LOG2E
    full = lambda a: pl.BlockSpec(a.shape, lambda i: (0, 0))
    g2, qg, kvg = g.reshape(1, d), q_norm.reshape(1, -1), kv_norm.reshape(1, -1)
    return pl.pallas_call(
        functools.partial(_mla_proj_kernel, scale=scale, tk=tk),
        grid=(t // tm,),
        in_specs=[pl.BlockSpec((tm, d), lambda i: (i, 0)), full(g2), full(wd), full(qg), full(wuq),
                  full(kvg), full(wukv), pl.BlockSpec((tm, 2 * LANES), lambda i: (i % ns, 0))],
        out_specs=[pl.BlockSpec((tm, MLA_OUT), lambda i: (i, 0)),
                   pl.BlockSpec((tm // tk, MLA_QN, tk), lambda i: (i, 0, 0))],
        out_shape=[jax.ShapeDtypeStruct((t, MLA_OUT), BF16),
                   jax.ShapeDtypeStruct((t // tk, MLA_QN, tk), BF16)],
        compiler_params=_params("parallel"),
        name="mla_proj",
    )(x, g2, wd, qg, wuq, kvg, wukv, cs)


def _mla_kernel(qn_ref, qr_ref, kn_ref, kr_ref, vt_ref, o_ref, acc_ref, ml_ref, s_ref, *, tq, tk):
    p = pl.program_id(1)
    i = pl.program_id(2)
    q = jnp.concatenate([qn_ref[...], qr_ref[...]], axis=1)
    lane = lax.broadcasted_iota(jnp.int32, (tq, 2 * LANES), 1)
    half = MLA_ROPE_DIM // 2
    zero = jnp.zeros_like(q)
    qms = []
    for j in range(2):
        r0 = LANES + (2 * (p % 2) + j) * half
        sel = jnp.logical_and(lane >= j * HEAD_DIM, lane < (j + 1) * HEAD_DIM)
        sel = jnp.logical_or(sel, jnp.logical_and(lane >= r0, lane < r0 + half))
        r1 = r0 + LANES // 2
        sel = jnp.logical_or(sel, jnp.logical_and(lane >= r1, lane < r1 + half))
        qms.append(jnp.where(sel, q, zero))
    acc_ref[...] = jnp.zeros_like(acc_ref)
    row = lax.broadcasted_iota(jnp.int32, ml_ref.shape, 0)
    ml_ref[...] = jnp.where(row < 2, NEG, 0.0)

    def scores(unit, _):
        jb, h, lanes, _ = unit
        start = pl.multiple_of(jb * tk, tk)
        kt = jnp.concatenate([kn_ref[pl.ds(start, tk), :], kr_ref[pl.ds(start, tk), :]], axis=1)
        return _dot_nt(kt, qms[h][lanes])

    def softmax(unit, s):
        _, h, lanes, mask = unit
        if mask is not None:
            s = jnp.where(mask, s, NEG)
        m_old = ml_ref[h:h + 1, lanes]
        m_new = jnp.maximum(m_old, jnp.max(s, axis=0, keepdims=True))
        alpha = jnp.exp2(m_old - m_new)
        e = jnp.exp2(s - m_new)
        ml_ref[h:h + 1, lanes] = m_new
        ml_ref[2 + h:3 + h, lanes] = alpha * ml_ref[2 + h:3 + h, lanes] + jnp.sum(e, axis=0, keepdims=True)
        return alpha, e.astype(BF16)

    def values(unit, val):
        jb, h, lanes, _ = unit
        alpha, e = val
        rows = slice(h * HEAD_DIM, (h + 1) * HEAD_DIM)
        acc_ref[rows, lanes] = alpha * acc_ref[rows, lanes] + _dot(vt_ref[jb, rows, :], e)

    _causal_sweep(i, tq, tk, (scores, softmax, values), False, s_ref)
    head = lax.broadcasted_iota(jnp.int32, acc_ref.shape, 0) < HEAD_DIM
    denom = jnp.where(head, ml_ref[2:3, :], ml_ref[3:4, :])
    o_ref[...] = (acc_ref[...] / denom).T.astype(o_ref.dtype)


def _mla_attention(a, vt, batch, seq):
    t = a.shape[0]
    tq = _tile(seq, ATT_TQ)
    tk = vt.shape[2]
    nq = seq // tq
    pairs = MLA_HEADS // 2
    qr0 = MLA_QN // LANES
    kn0 = (MLA_QN + MLA_QR) // LANES
    kr0 = kn0 + pairs
    return pl.pallas_call(
        functools.partial(_mla_kernel, tq=tq, tk=tk),
        grid=(batch, pairs, nq),
        in_specs=[pl.BlockSpec((tq, LANES), lambda b, p, i: (b * nq + i, p)),
                  pl.BlockSpec((tq, LANES), lambda b, p, i: (b * nq + i, qr0 + p // 2)),
                  pl.BlockSpec((seq, LANES), lambda b, p, i: (b, kn0 + p)),
                  pl.BlockSpec((seq, LANES), lambda b, p, i: (b, kr0)),
                  pl.BlockSpec((seq // tk, LANES, tk), lambda b, p, i: (b, p, 0))],
        out_specs=pl.BlockSpec((tq, LANES), lambda b, p, i: (b * nq + i, p)),
        out_shape=jax.ShapeDtypeStruct((t, MLA_HEADS * HEAD_DIM), BF16),
        scratch_shapes=[pltpu.VMEM((LANES, tq), F32), pltpu.VMEM((8, tq), F32),
                        pltpu.VMEM((2 * (tq // tk), tk, tq), F32)],
        compiler_params=_params("parallel", "parallel", "arbitrary"),
        name="mla_attention",
    )(a, a, a, a, vt)


FFN_CHUNK = 256


def _swiglu_into(acc_ref, h, wg, wu, wd):
    def gate_up(cols, _):
        return _dot(h, wg(cols)), _dot(h, wu(cols))

    def activate(cols, gu):
        g, u = gu
        return (g * jax.nn.sigmoid(g) * u).astype(BF16)

    def down(cols, a):
        acc_ref[...] += _dot(a, wd(cols))

    return gate_up, activate, down


def _run_swiglu(acc_ref, h, wg, wu, wd, width):
    fc = FFN_CHUNK if width % FFN_CHUNK == 0 else width
    chunks = [slice(c, c + fc) for c in range(0, width, fc)]
    _staggered(chunks, _swiglu_into(acc_ref, h, wg, wu, wd))


def _ffn_kernel(x_ref, g_ref, wg_ref, wu_ref, wd_ref, o_ref, h_ref):
    f = pl.program_id(1)

    @pl.when(f == 0)
    def _():
        x = x_ref[...]
        h_ref[...] = _rms(x, g_ref[...]).astype(BF16)
        o_ref[...] = x

    _run_swiglu(o_ref, h_ref[...], lambda c: wg_ref[:, c], lambda c: wu_ref[:, c], lambda c: wd_ref[c, :],
                wg_ref.shape[1])


def _ffn(x, g, wg, wu, wd):
    t, d = x.shape
    fdim = wg.shape[1]
    tm = _tile(t, 512)
    tf = fdim
    return pl.pallas_call(
        _ffn_kernel,
        grid=(t // tm, fdim // tf),
        in_specs=[pl.BlockSpec((tm, d), lambda i, f: (i, 0)),
                  pl.BlockSpec((1, d), lambda i, f: (0, 0)),
                  pl.BlockSpec((d, tf), lambda i, f: (0, f)),
                  pl.BlockSpec((d, tf), lambda i, f: (0, f)),
                  pl.BlockSpec((tf, d), lambda i, f: (f, 0))],
        out_specs=pl.BlockSpec((tm, d), lambda i, f: (i, 0)),
        out_shape=jax.ShapeDtypeStruct((t, d), F32),
        scratch_shapes=[pltpu.VMEM((tm, d), BF16)],
        compiler_params=_params("parallel", "arbitrary"),
        name="dense_ffn",
    )(x, g.reshape(1, d), wg.astype(BF16), wu.astype(BF16), wd.astype(BF16))


MOE_TILE = 512
META_ROWS = 8
ROUTE_SUB = 128
COMBINE_SUB = 256


def _router_kernel(x_ref, g_ref, rt_ref, su_ref, eye_ref, hb_ref, meta_ref, metac_ref, cnt_ref, carry_ref):
    @pl.when(pl.program_id(0) == 0)
    def _():
        carry_ref[...] = jnp.zeros_like(carry_ref)

    h = _rms(x_ref[...], g_ref[...])
    hb_ref[...] = h.astype(BF16)
    logits = lax.dot_general(rt_ref[...], h, _NT, precision=lax.Precision.HIGHEST,
                             preferred_element_type=F32)
    ne, ct = logits.shape
    eio = lax.broadcasted_iota(jnp.int32, (ne, ct), 0).astype(F32)
    m1 = jnp.max(logits, axis=0, keepdims=True)
    i1 = jnp.min(jnp.where(logits == m1, eio, float(ne)), axis=0, keepdims=True)
    rest = jnp.where(eio == i1, -jnp.inf, logits)
    m2 = jnp.max(rest, axis=0, keepdims=True)
    i2 = jnp.min(jnp.where(rest == m2, eio, float(ne)), axis=0, keepdims=True)
    e2 = jnp.exp(m2 - m1)
    g1 = 1.0 / (1.0 + e2)
    g2 = e2 / (1.0 + e2)
    oh1 = (eio == i1).astype(F32)
    oh2 = (eio == i2).astype(F32)
    both = oh1 + oh2
    seen = carry_ref[:, 0:1] + _dot(both.astype(BF16), su_ref[...])
    r1 = jnp.sum(oh1 * seen, axis=0, keepdims=True)
    r2 = jnp.sum(oh2 * seen, axis=0, keepdims=True)
    meta = jnp.zeros((META_ROWS, ct), F32)
    for row, val in enumerate((i1, i2, r1, r2, g1, g2)):
        meta = jnp.where(eio == float(row), val, meta)
    meta_ref[...] = meta
    metac_ref[...] = lax.dot_general(eye_ref[...], meta, _NT, precision=lax.Precision.HIGHEST,
                                     preferred_element_type=F32)
    carry_ref[...] = carry_ref[...] + jnp.sum(both, axis=1, keepdims=True)
    cnt_ref[0] = carry_ref[...]


def _router(x, g, router):
    t, d = x.shape
    ct = _tile(t, MOE_TILE)
    nc = t // ct
    su = jnp.asarray(np.triu(np.ones((ct, ct), np.float32), 1), BF16)
    eye = jnp.eye(ct, dtype=F32)
    return pl.pallas_call(
        _router_kernel,
        grid=(nc,),
        in_specs=[pl.BlockSpec((ct, d), lambda i: (i, 0)),
                  pl.BlockSpec((1, d), lambda i: (0, 0)),
                  pl.BlockSpec((N_EXPERTS, d), lambda i: (0, 0)),
                  pl.BlockSpec((ct, ct), lambda i: (0, 0)),
                  pl.BlockSpec((ct, ct), lambda i: (0, 0))],
        out_specs=[pl.BlockSpec((ct, d), lambda i: (i, 0)),
                   pl.BlockSpec((META_ROWS, ct), lambda i: (0, i)),
                   pl.BlockSpec((ct, META_ROWS), lambda i: (i, 0)),
                   pl.BlockSpec((1, N_EXPERTS, LANES), lambda i: (i, 0, 0))],
        out_shape=[jax.ShapeDtypeStruct((t, d), BF16),
                   jax.ShapeDtypeStruct((META_ROWS, t), F32),
                   jax.ShapeDtypeStruct((t, META_ROWS), F32),
                   jax.ShapeDtypeStruct((nc, N_EXPERTS, LANES), F32)],
        scratch_shapes=[pltpu.VMEM((N_EXPERTS, LANES), F32)],
        compiler_params=_params("arbitrary"),
        name="moe_router",
    )(x, g.reshape(1, d), router.T, su, eye)


def _moe_plan(cnt_after, t, ct):
    nc = t // ct
    rows = 2 * t + N_EXPERTS * ct
    nrt = rows // ct
    counts = cnt_after[:, :, 0].astype(jnp.int32)
    total = counts[-1]
    padded = (total + ct - 1) // ct * ct
    off_end = jnp.cumsum(padded)
    off = off_end - padded
    before = jnp.concatenate([jnp.zeros((1, N_EXPERTS), jnp.int32), counts[:-1]], axis=0)
    chunk_start = (off[:, None] + before.T).reshape(-1)
    tile_start = jnp.arange(nrt, dtype=jnp.int32) * ct
    starts = jnp.sort(jnp.concatenate([chunk_start, tile_start]))
    ends = jnp.concatenate([starts[1:], jnp.array([rows], jnp.int32)])
    item_r = jnp.minimum(starts // ct, nrt - 1)
    count_le = lambda sorted_vals, x: jnp.sum((sorted_vals[None, :] <= x[:, None]).astype(jnp.int32), axis=1)
    item_c = (count_le(chunk_start, starts) - 1) % nc
    first = lambda key: jnp.concatenate([jnp.ones((1,), jnp.int32), (key[1:] != key[:-1]).astype(jnp.int32)])
    dispatch = (item_r, item_c, starts, ends, first(item_r))
    order = jnp.argsort(item_c * (rows + 1) + starts)
    c2 = item_c[order]
    combine = (c2, item_r[order], starts[order], ends[order], first(c2))
    tile_expert = jnp.minimum(count_le(off_end, tile_start), N_EXPERTS - 1)
    tile_valid = (tile_start < off_end[-1]).astype(jnp.int32)
    return rows, off.astype(jnp.int32), dispatch, combine, tile_expert.astype(jnp.int32), tile_valid


def _positions(expert, rank, off_ref):
    base = jnp.zeros_like(rank)
    for e in range(N_EXPERTS):
        base = jnp.where(expert == float(e), off_ref[e].astype(F32), base)
    return base + rank


def _dispatch_kernel(r_ref, c_ref, s_ref, e_ref, first_ref, off_ref, meta_ref, h_ref, xs_ref, gate_ref):
    n = pl.program_id(0)
    rt, ct = xs_ref.shape[0], h_ref.shape[0]

    @pl.when(first_ref[n] == 1)
    def _():
        xs_ref[...] = jnp.zeros_like(xs_ref)
        gate_ref[...] = jnp.zeros_like(gate_ref)

    meta = meta_ref[...]
    pos0 = _positions(meta[0:1], meta[2:3], off_ref)
    pos1 = _positions(meta[1:2], meta[3:4], off_ref)
    sub = min(rt, ROUTE_SUB)
    for sb in range(rt // sub):
        base = r_ref[n] * rt + sb * sub

        @pl.when(jnp.logical_and(base < e_ref[n], base + sub > s_ref[n]))
        def _():
            row = lax.broadcasted_iota(jnp.int32, (sub, ct), 0) + base
            live = jnp.logical_and(row >= s_ref[n], row < e_ref[n])
            rowf = row.astype(F32)
            sel0 = jnp.logical_and(pos0 == rowf, live)
            sel1 = jnp.logical_and(pos1 == rowf, live)
            sel = jnp.logical_or(sel0, sel1).astype(F32).astype(BF16)
            rows = slice(sb * sub, (sb + 1) * sub)
            xs_ref[rows, :] += _dot(sel, h_ref[...]).astype(xs_ref.dtype)
            gate = jnp.where(sel0, meta[4:5], 0.0) + jnp.where(sel1, meta[5:6], 0.0)
            gate_ref[rows, :] += jnp.sum(gate, axis=1, keepdims=True)


def _dispatch(plan, off, meta, hb, rows, ct):
    t, d = hb.shape
    n_items = plan[0].shape[0]
    return pl.pallas_call(
        _dispatch_kernel,
        grid_spec=pltpu.PrefetchScalarGridSpec(
            num_scalar_prefetch=6,
            grid=(n_items,),
            in_specs=[pl.BlockSpec((META_ROWS, ct), lambda n, r, c, *_: (0, c[n])),
                      pl.BlockSpec((ct, d), lambda n, r, c, *_: (c[n], 0))],
            out_specs=[pl.BlockSpec((ct, d), lambda n, r, c, *_: (r[n], 0)),
                       pl.BlockSpec((ct, 1), lambda n, r, c, *_: (r[n], 0))]),
        out_shape=[jax.ShapeDtypeStruct((rows, d), BF16),
                   jax.ShapeDtypeStruct((rows, 1), F32)],
        compiler_params=_params("arbitrary"),
        name="moe_dispatch",
    )(*plan, off, meta, hb)


def _expert_kernel(te_ref, tv_ref, xs_ref, gate_ref, wg_ref, wu_ref, wd_ref, y_ref, acc_ref):
    r = pl.program_id(0)
    f = pl.program_id(1)

    @pl.when(f == 0)
    def _():
        acc_ref[...] = jnp.zeros_like(acc_ref)

    @pl.when(tv_ref[r] == 1)
    def _():
        _run_swiglu(acc_ref, xs_ref[...], lambda c: wg_ref[0, :, c], lambda c: wu_ref[0, :, c],
                    lambda c: wd_ref[0, c, :], wg_ref.shape[2])

    @pl.when(f == pl.num_programs(1) - 1)
    def _():
        y_ref[...] = (acc_ref[...] * gate_ref[...]).astype(y_ref.dtype)


def _experts(tile_expert, tile_valid, xs, gate, wg, wu, wd, ct):
    rows, d = xs.shape
    fdim = wg.shape[2]
    tf = _tile(fdim, 1792)
    nf = fdim // tf

    def wcol(r, f, te, tv):
        return (te[r], 0, jnp.where(tv[r] == 1, f, nf - 1))

    def wrow(r, f, te, tv):
        return (te[r], jnp.where(tv[r] == 1, f, nf - 1), 0)

    return pl.pallas_call(
        _expert_kernel,
        grid_spec=pltpu.PrefetchScalarGridSpec(
            num_scalar_prefetch=2,
            grid=(rows // ct, nf),
            in_specs=[pl.BlockSpec((ct, d), lambda r, f, te, tv: (r, 0)),
                      pl.BlockSpec((ct, 1), lambda r, f, te, tv: (r, 0)),
                      pl.BlockSpec((1, d, tf), wcol),
                      pl.BlockSpec((1, d, tf), wcol),
                      pl.BlockSpec((1, tf, d), wrow)],
            out_specs=pl.BlockSpec((ct, d), lambda r, f, te, tv: (r, 0)),
            scratch_shapes=[pltpu.VMEM((ct, d), F32)]),
        out_shape=jax.ShapeDtypeStruct((rows, d), BF16),
        compiler_params=_params("arbitrary", "arbitrary"),
        name="moe_experts",
    )(tile_expert, tile_valid, xs, gate, wg, wu, wd)


def _combine_kernel(c_ref, r_ref, s_ref, e_ref, first_ref, off_ref, x_ref, metac_ref, y_ref, o_ref):
    n = pl.program_id(0)
    ct, rt = x_ref.shape[0], y_ref.shape[0]

    @pl.when(first_ref[n] == 1)
    def _():
        o_ref[...] = x_ref[...]

    metac = metac_ref[...]
    pos0 = _positions(metac[:, 0:1], metac[:, 2:3], off_ref)
    pos1 = _positions(metac[:, 1:2], metac[:, 3:4], off_ref)
    sub = min(rt, COMBINE_SUB)
    for sb in range(rt // sub):
        base = r_ref[n] * rt + sb * sub

        @pl.when(jnp.logical_and(base < e_ref[n], base + sub > s_ref[n]))
        def _():
            row = lax.broadcasted_iota(jnp.int32, (ct, sub), 1) + base
            live = jnp.logical_and(row >= s_ref[n], row < e_ref[n])
            rowf = row.astype(F32)
            sel = jnp.logical_and(jnp.logical_or(pos0 == rowf, pos1 == rowf), live).astype(F32).astype(BF16)
            o_ref[...] += _dot(sel, y_ref[sb * sub:(sb + 1) * sub, :])


def _combine(plan, off, x, metac, y, ct):
    t, d = x.shape
    n_items = plan[0].shape[0]
    return pl.pallas_call(
        _combine_kernel,
        grid_spec=pltpu.PrefetchScalarGridSpec(
            num_scalar_prefetch=6,
            grid=(n_items,),
            in_specs=[pl.BlockSpec((ct, d), lambda n, c, r, *_: (c[n], 0)),
                      pl.BlockSpec((ct, META_ROWS), lambda n, c, r, *_: (c[n], 0)),
                      pl.BlockSpec((ct, d), lambda n, c, r, *_: (r[n], 0))],
            out_specs=pl.BlockSpec((ct, d), lambda n, c, r, *_: (c[n], 0))),
        out_shape=jax.ShapeDtypeStruct((t, d), F32),
        compiler_params=_params("arbitrary"),
        name="moe_combine",
    )(*plan, off, x, metac, y)


def _moe(x, g, router, wg, wu, wd):
    t = x.shape[0]
    ct = _tile(t, MOE_TILE)
    hb, meta, metac, cnt = _router(x, g, router)
    rows, off, dispatch, combine, tile_expert, tile_valid = _moe_plan(cnt, t, ct)
    xs, gate = _dispatch(dispatch, off, meta, hb, rows, ct)
    y = _experts(tile_expert, tile_valid, xs, gate, wg.astype(BF16), wu.astype(BF16), wd.astype(BF16), ct)
    return _combine(combine, off, x, metac, y, ct)


def _final_norm_kernel(x_ref, g_ref, o_ref):
    o_ref[...] = _rms(x_ref[...], g_ref[...])


def _final_norm(x, g):
    t, d = x.shape
    tm = _tile(t, 1024)
    return pl.pallas_call(
        _final_norm_kernel,
        grid=(t // tm,),
        in_specs=[pl.BlockSpec((tm, d), lambda i: (i, 0)), pl.BlockSpec((1, d), lambda i: (0, 0))],
        out_specs=pl.BlockSpec((tm, d), lambda i: (i, 0)),
        out_shape=jax.ShapeDtypeStruct((t, d), F32),
        compiler_params=_params("parallel"),
        name="final_norm",
    )(x, g.reshape(1, d))


def kernel(x, rel_bias, attn_norm, ffn_norm, final_norm, swa_w_qkv, swa_b_qkv, swa_sinks, swa_w_o, sb_w_qkv, sb_w_o, mla_w_down, mla_q_norm, mla_w_uq, mla_kv_norm, mla_w_ukv, mla_w_o, ffn_w_gate, ffn_w_up, ffn_w_down, moe_router, moe_w_gate, moe_w_up, moe_w_down):
    batch, seq, d = x.shape
    depth = attn_norm.shape[0]
    xt = x.reshape(batch * seq, d)
    swa_bias = _swa_bias(rel_bias)
    for i in range(depth):
        mixer, j = i % N_MIXERS, i // N_MIXERS
        if mixer == 0:
            w, b, cs = _swa_weights(swa_w_qkv[j], swa_b_qkv[j])
            qkv = _norm_proj(xt, attn_norm[i], w, b, cs, "swa_proj")
            o = _swa_attention(qkv, swa_bias, swa_sinks[j], batch, seq)
            xt = _out_proj(xt, o, swa_w_o[j].astype(BF16), "swa_out")
        elif mixer == 1:
            n = sb_w_qkv.shape[2] // 3
            cs = jnp.concatenate([jnp.full((n,), HEAD_DIM ** -0.5 * LOG2E, F32), jnp.ones((n,), F32)])
            qk, vt = _norm_proj_vt(xt, attn_norm[i], sb_w_qkv[j].astype(BF16), cs, n, seq, "sb_proj")
            o = _sb_attention(qk, vt, batch, seq)
            xt = _out_proj(xt, o, sb_w_o[j].astype(BF16), "sb_out")
        else:
            a, vt = _mla_proj(xt, attn_norm[i], mla_w_down[j], mla_q_norm[j], mla_w_uq[j],
                              mla_kv_norm[j], mla_w_ukv[j], seq)
            o = _mla_attention(a, vt, batch, seq)
            xt = _out_proj(xt, o, mla_w_o[j].astype(BF16), "mla_out")
        f = i // 2
        if i % 2 == 0:
            xt = _ffn(xt, ffn_norm[i], ffn_w_gate[f], ffn_w_up[f], ffn_w_down[f])
        else:
            xt = _moe(xt, ffn_norm[i], moe_router[f], moe_w_gate[f], moe_w_up[f], moe_w_down[f])
    return _final_norm(xt, final_norm).reshape(batch, seq, d)
```

```python
import functools
import math

import numpy as np
import jax
import jax.numpy as jnp
from jax import lax
from jax.experimental import pallas as pl
from jax.experimental.pallas import tpu as pltpu
from jax.experimental.pallas import tpu_sc as plsc

F32 = jnp.float32
BF16 = jnp.bfloat16

N_MIXERS = 3
RMS_EPS = 1e-6
SWA_HEADS = 16
SWA_KV_HEADS = 2
HEAD_DIM = 64
SWA_BLOCK = 128
SWA_WINDOW = 128
REL_BUCKETS = 32
REL_MAX_DIST = 128
SB_HEADS = 16
MLA_HEADS = 16
MLA_NOPE_DIM = 64
MLA_ROPE_DIM = 32
MLA_Q_RANK = 384
MLA_KV_RANK = 256
ROPE_THETA = 10000.0
N_EXPERTS = 8

LANES = 128
NEG = -1e30
LOG2E = math.log2(math.e)
VMEM_LIMIT = 56 * 1024 * 1024

_NT = (((1,), (1,)), ((), ()))


def _dot(a, b):
    return jnp.dot(a, b, preferred_element_type=F32)


def _dot_nt(a, b):
    return lax.dot_general(a, b, _NT, preferred_element_type=F32)


def _rms(x, g):
    return x * lax.rsqrt(jnp.mean(x * x, axis=-1, keepdims=True) + RMS_EPS) * g


def _params(*sem):
    return pltpu.CompilerParams(dimension_semantics=sem, vmem_limit_bytes=VMEM_LIMIT)


def _tile(n, pref):
    return pref if n % pref == 0 else n


def _norm_proj_kernel(x_ref, g_ref, w_ref, b_ref, cs_ref, o_ref):
    h = _rms(x_ref[...], g_ref[...]).astype(BF16)
    y = (_dot(h, w_ref[...]) + b_ref[...]) * cs_ref[...]
    o_ref[...] = y.astype(o_ref.dtype)


def _norm_proj(x, g, w, b, colscale, name):
    t, d = x.shape
    n = w.shape[1]
    tm = _tile(t, 512)
    return pl.pallas_call(
        _norm_proj_kernel,
        grid=(t // tm,),
        in_specs=[pl.BlockSpec((tm, d), lambda i: (i, 0)),
                  pl.BlockSpec((1, d), lambda i: (0, 0)),
                  pl.BlockSpec((d, n), lambda i: (0, 0)),
                  pl.BlockSpec((1, n), lambda i: (0, 0)),
                  pl.BlockSpec((1, n), lambda i: (0, 0))],
        out_specs=pl.BlockSpec((tm, n), lambda i: (i, 0)),
        out_shape=jax.ShapeDtypeStruct((t, n), BF16),
        compiler_params=_params("parallel"),
        name=name,
    )(x, g.reshape(1, d), w, b.reshape(1, n), colscale.reshape(1, n))


ATT_TQ = 512
ATT_TK = 256


def _store_vt(vt_ref, v, tk):
    for s in range(v.shape[0] // tk):
        vt_ref[s] = v[s * tk:(s + 1) * tk, :].T.astype(vt_ref.dtype)


def _norm_proj_vt_kernel(x_ref, g_ref, w_ref, cs_ref, o_ref, vt_ref, *, tk):
    h = _rms(x_ref[...], g_ref[...]).astype(BF16)
    y = _dot(h, w_ref[...])
    n = o_ref.shape[1]
    o_ref[...] = (y[:, :n] * cs_ref[...]).astype(o_ref.dtype)
    _store_vt(vt_ref, y[:, n:], tk)


def _norm_proj_vt(x, g, w, colscale, nv, seq, name):
    t, d = x.shape
    n = w.shape[1] - nv
    tm = _tile(seq, 512)
    tk = _tile(seq, ATT_TK)
    per = tm // tk
    return pl.pallas_call(
        functools.partial(_norm_proj_vt_kernel, tk=tk),
        grid=(t // tm,),
        in_specs=[pl.BlockSpec((tm, d), lambda i: (i, 0)),
                  pl.BlockSpec((1, d), lambda i: (0, 0)),
                  pl.BlockSpec((d, n + nv), lambda i: (0, 0)),
                  pl.BlockSpec((1, n), lambda i: (0, 0))],
        out_specs=[pl.BlockSpec((tm, n), lambda i: (i, 0)),
                   pl.BlockSpec((per, nv, tk), lambda i: (i, 0, 0))],
        out_shape=[jax.ShapeDtypeStruct((t, n), BF16),
                   jax.ShapeDtypeStruct((t // tk, nv, tk), BF16)],
        compiler_params=_params("parallel"),
        name=name,
    )(x, g.reshape(1, d), w, colscale.reshape(1, n))


def _out_proj_kernel(x_ref, o_ref, w_ref, y_ref):
    y_ref[...] = x_ref[...] + _dot(o_ref[...], w_ref[...])


def _out_proj(x, o, w, name):
    t, d = x.shape
    k = o.shape[1]
    tm = _tile(t, 512)
    return pl.pallas_call(
        _out_proj_kernel,
        grid=(t // tm,),
        in_specs=[pl.BlockSpec((tm, d), lambda i: (i, 0)),
                  pl.BlockSpec((tm, k), lambda i: (i, 0)),
                  pl.BlockSpec((k, d), lambda i: (0, 0))],
        out_specs=pl.BlockSpec((tm, d), lambda i: (i, 0)),
        out_shape=jax.ShapeDtypeStruct((t, d), F32),
        compiler_params=_params("parallel"),
        name=name,
    )(x, o, w)


def _t5_bucket_table():
    qi = np.arange(SWA_BLOCK)[:, None]
    kj = np.arange(2 * SWA_BLOCK)[None, :]
    dist = qi + SWA_BLOCK - kj
    d0 = np.maximum(dist, 0)
    max_exact = REL_BUCKETS // 2
    d = np.maximum(d0, 1).astype(np.float32)
    large = max_exact + (np.log(d / max_exact) / math.log(REL_MAX_DIST / max_exact)
                         * (REL_BUCKETS - max_exact)).astype(np.int32)
    large = np.minimum(large, REL_BUCKETS - 1)
    bucket = np.where(d0 < max_exact, d0, large)
    band = (dist >= 0) & (dist < SWA_WINDOW)
    return np.where(band, bucket, -1).astype(np.int32)


def _swa_bias_kernel(rel_ref, bucket_ref, o_ref):
    h = pl.program_id(0)
    bucket = bucket_ref[...]
    acc = jnp.full(bucket.shape, NEG, F32)
    for b in range(REL_BUCKETS):
        acc = jnp.where(bucket == b, rel_ref[b, h], acc)
    o_ref[0] = acc


def _swa_bias(rel_bias):
    bucket = jnp.asarray(_t5_bucket_table())
    q, k = bucket.shape
    return pl.pallas_call(
        _swa_bias_kernel,
        grid=(SWA_HEADS,),
        in_specs=[pl.BlockSpec(memory_space=pltpu.SMEM),
                  pl.BlockSpec((q, k), lambda h: (0, 0))],
        out_specs=pl.BlockSpec((1, q, k), lambda h: (h, 0, 0)),
        out_shape=jax.ShapeDtypeStruct((SWA_HEADS, q, k), F32),
        compiler_params=_params("arbitrary"),
        name="swa_bias",
    )(rel_bias, bucket)


def _swa_kernel(sink_ref, q_ref, kc_ref, kp_ref, vc_ref, vp_ref, bias_ref, o_ref, *, tq):
    i = pl.program_id(1)
    g = pl.program_id(2)
    blk = SWA_BLOCK
    heads = bias_ref.shape[0]
    lo_half = lax.broadcasted_iota(jnp.int32, (blk, LANES), 1) < HEAD_DIM
    in_prev = lax.broadcasted_iota(jnp.int32, (heads * blk, 2 * blk), 1) < blk
    bias = bias_ref[...].reshape(heads * blk, 2 * blk)

    def window(cur_ref, prev_ref, sb):
        if sb == 0:
            return jnp.concatenate([prev_ref[...], cur_ref[0:blk, :]], axis=0)
        return cur_ref[(sb - 1) * blk:(sb + 1) * blk, :]

    def scores(sb, _):
        q = q_ref[sb * blk:(sb + 1) * blk, :]
        parts = []
        for h in range(heads):
            qp = q[:, (h // 2) * LANES:(h // 2 + 1) * LANES]
            keep = lo_half if h % 2 == 0 else jnp.logical_not(lo_half)
            parts.append(jnp.where(keep, qp, jnp.zeros_like(qp)))
        s = _dot_nt(jnp.concatenate(parts, axis=0), window(kc_ref, kp_ref, sb)) + bias
        if sb == 0:
            s = jnp.where(jnp.logical_and(in_prev, i == 0), NEG, s)
        return s

    def softmax(sb, s):
        ps, inv = [], []
        for h in range(heads):
            sh = s[h * blk:(h + 1) * blk]
            sink = sink_ref[g * heads + h]
            m = jnp.maximum(jnp.max(sh, axis=1, keepdims=True), sink)
            e = jnp.exp(sh - m)
            inv.append(1.0 / (jnp.sum(e, axis=1, keepdims=True) + jnp.exp(sink - m)))
            ps.append(e.astype(BF16))
        return jnp.concatenate(ps, axis=0), inv

    def values(sb, val):
        p, inv = val
        o = _dot(p, window(vc_ref, vp_ref, sb))
        outs = [jnp.where(lo_half, o[h * blk:(h + 1) * blk] * inv[h], o[(h + 1) * blk:(h + 2) * blk] * inv[h + 1])
                for h in range(0, heads, 2)]
        o_ref[sb * blk:(sb + 1) * blk, :] = jnp.concatenate(outs, axis=1).astype(o_ref.dtype)

    _staggered(list(range(tq // blk)), (scores, softmax, values))


def _swa_attention(qkv, bias, sinks, batch, seq):
    t = qkv.shape[0]
    tq = _tile(seq, 512)
    nt = seq // tq
    group = SWA_HEADS // SWA_KV_HEADS
    gw = group * HEAD_DIM
    kcol = SWA_HEADS * HEAD_DIM // LANES
    vcol = kcol + SWA_KV_HEADS
    per = tq // SWA_BLOCK

    def prev_idx(b, i, g):
        return jnp.maximum(b * (seq // SWA_BLOCK) + i * per - 1, 0)

    return pl.pallas_call(
        functools.partial(_swa_kernel, tq=tq),
        grid=(batch, nt, SWA_KV_HEADS),
        in_specs=[pl.BlockSpec(memory_space=pltpu.SMEM),
                  pl.BlockSpec((tq, gw), lambda b, i, g: (b * nt + i, g)),
                  pl.BlockSpec((tq, LANES), lambda b, i, g: (b * nt + i, kcol + g)),
                  pl.BlockSpec((SWA_BLOCK, LANES), lambda b, i, g: (prev_idx(b, i, g), kcol + g)),
                  pl.BlockSpec((tq, LANES), lambda b, i, g: (b * nt + i, vcol + g)),
                  pl.BlockSpec((SWA_BLOCK, LANES), lambda b, i, g: (prev_idx(b, i, g), vcol + g)),
                  pl.BlockSpec((group, SWA_BLOCK, 2 * SWA_BLOCK), lambda b, i, g: (g, 0, 0))],
        out_specs=pl.BlockSpec((tq, gw), lambda b, i, g: (b * nt + i, g)),
        out_shape=jax.ShapeDtypeStruct((t, SWA_HEADS * HEAD_DIM), BF16),
        compiler_params=_params("parallel", "parallel", "arbitrary"),
        name="swa_attention",
    )(sinks, qkv, qkv, qkv, qkv, qkv, bias)


def _swa_weights(w_qkv, b_qkv):
    nq = SWA_HEADS * HEAD_DIM
    kv = SWA_KV_HEADS * HEAD_DIM
    dup = np.concatenate([np.tile(np.arange(HEAD_DIM), 2) + h * HEAD_DIM for h in range(SWA_KV_HEADS)])
    cols = np.concatenate([np.arange(nq), nq + dup, nq + kv + dup])
    scale = np.concatenate([np.full(nq, HEAD_DIM ** -0.5), np.ones(2 * dup.size)]).astype(np.float32)
    return w_qkv[:, cols].astype(BF16), b_qkv[cols], jnp.asarray(scale)


def _staggered(units, stages):
    vals = list(units)
    for t in range(len(units) + len(stages) - 1):
        for k, stage in enumerate(stages):
            u = t - k
            if 0 <= u < len(units):
                vals[u] = stage(units[u], vals[u])


def _score_ahead(units, next_units, stages, s_ref):
    rest = stages[1:]
    vals = [None] * len(units)
    for t in range(len(units) + len(rest) - 1):
        ahead = stages[0](next_units[t], None) if t < len(units) else None
        for k, stage in enumerate(rest):
            u = t - k
            if 0 <= u < len(units):
                if k == 0:
                    lanes = units[u][2]
                    vals[u] = stage(units[u], s_ref[u, :, :lanes.stop - lanes.start])
                    s_ref[u] = ahead
                else:
                    vals[u] = stage(units[u], vals[u])


def _causal_sweep(i, tq, tk, stages, strict, s_ref):
    nsub = tq // tk

    def group(j):
        first = jnp.maximum(j, 0) * nsub
        return [(first + sb, h, slice(0, tq), None) for sb in reversed(range(nsub)) for h in range(2)]

    diagonal = []
    for sb in reversed(range(nsub)):
        nl = tq - sb * tk
        key = lax.broadcasted_iota(jnp.int32, (tk, nl), 0)
        qry = lax.broadcasted_iota(jnp.int32, (tk, nl), 1)
        mask = key < qry if strict else key <= qry
        diagonal += [(i * nsub + sb, h, slice(sb * tk, tq), mask) for h in range(2)]
    for u, unit in enumerate(diagonal):
        s_ref[u, :, :tq - unit[2].start] = stages[0](unit, None)
    _score_ahead(diagonal, group(i - 1), stages, s_ref)

    def body(n, carry):
        _score_ahead(group(i - 1 - n), group(i - 2 - n), stages, s_ref)
        return carry

    lax.fori_loop(0, i, body, 0)


def _split_pair(q):
    lo_half = lax.broadcasted_iota(jnp.int32, q.shape, 1) < HEAD_DIM
    zero = jnp.zeros_like(q)
    return jnp.where(lo_half, q, zero), jnp.where(lo_half, zero, q)


def _sb_kernel(q_ref, k_ref, vt_ref, u_ref, o_ref, acc_ref, c_ref, s_ref, *, tq, tk):
    i = pl.program_id(2)
    qms = _split_pair(q_ref[...])
    u = u_ref[...]
    acc_ref[...] = jnp.zeros_like(acc_ref)
    c_ref[...] = jnp.zeros_like(c_ref)

    def scores(unit, _):
        jb, h, lanes, _ = unit
        kt = k_ref[pl.ds(pl.multiple_of(jb * tk, tk), tk), :]
        return _dot_nt(kt, qms[h][lanes])

    def log_fail(unit, z):
        neg_abs = lax.bitcast_convert_type(lax.bitcast_convert_type(z, jnp.uint32) | jnp.uint32(1 << 31), F32)
        sp = jnp.maximum(z, 0.0) + jnp.log2(1.0 + jnp.exp2(neg_abs))
        if unit[3] is not None:
            sp = jnp.where(unit[3], sp, 0.0)
        return z, sp.astype(BF16)

    def suffix_sum(unit, val):
        z, sp = val
        return z, _dot(u, sp)

    def weights(unit, val):
        _, h, lanes, mask = unit
        z, r = val
        c = c_ref[h:h + 1, lanes]
        la = z - r - c
        if mask is not None:
            la = jnp.where(mask, la, NEG)
        c_ref[h:h + 1, lanes] = c + r[0:1, :]
        return jnp.exp2(la).astype(BF16)

    def values(unit, a):
        jb, h, lanes, _ = unit
        rows = slice(h * HEAD_DIM, (h + 1) * HEAD_DIM)
        acc_ref[rows, lanes] += _dot(vt_ref[jb, rows, :], a)

    _causal_sweep(i, tq, tk, (scores, log_fail, suffix_sum, weights, values), True, s_ref)
    o_ref[...] = acc_ref[...].T.astype(o_ref.dtype)


def _sb_attention(qk, vt, batch, seq):
    t = qk.shape[0]
    tq = _tile(seq, ATT_TQ)
    tk = vt.shape[2]
    nq = seq // tq
    pairs = SB_HEADS // 2
    u = jnp.asarray(np.triu(np.ones((tk, tk), np.float32)), BF16)
    return pl.pallas_call(
        functools.partial(_sb_kernel, tq=tq, tk=tk),
        grid=(batch, pairs, nq),
        in_specs=[pl.BlockSpec((tq, LANES), lambda b, p, i: (b * nq + i, p)),
                  pl.BlockSpec((seq, LANES), lambda b, p, i: (b, pairs + p)),
                  pl.BlockSpec((seq // tk, LANES, tk), lambda b, p, i: (b, p, 0)),
                  pl.BlockSpec((tk, tk), lambda b, p, i: (0, 0))],
        out_specs=pl.BlockSpec((tq, LANES), lambda b, p, i: (b * nq + i, p)),
        out_shape=jax.ShapeDtypeStruct((t, SB_HEADS * HEAD_DIM), BF16),
        scratch_shapes=[pltpu.VMEM((LANES, tq), F32), pltpu.VMEM((8, tq), F32),
                        pltpu.VMEM((2 * (tq // tk), tk, tq), F32)],
        compiler_params=_params("parallel", "parallel", "arbitrary"),
        name="sb_attention",
    )(qk, qk, vt, u)


MLA_QN = MLA_HEADS * MLA_NOPE_DIM
MLA_QR = MLA_HEADS * MLA_ROPE_DIM
MLA_OUT = 2 * MLA_QN + MLA_QR + LANES


def _mla_proj_kernel(x_ref, g_ref, wd_ref, qg_ref, wuq_ref, kvg_ref, wukv_ref, cs_ref, o_ref, vt_ref,
                     *, scale, tk):
    h = _rms(x_ref[...], g_ref[...]).astype(BF16)
    c = _dot(h, wd_ref[...])
    cos = cs_ref[:, :LANES]
    sin = cs_ref[:, LANES:]
    cq = _rms(c[:, :MLA_Q_RANK], qg_ref[...]).astype(BF16)
    q = _dot(cq, wuq_ref[...])
    o_ref[:, :MLA_QN] = (q[:, :MLA_QN] * scale).astype(o_ref.dtype)
    for m in range(MLA_QR // LANES):
        a = MLA_QN + m * LANES
        rot = q[:, a:a + LANES] * cos + q[:, a + MLA_QR:a + MLA_QR + LANES] * sin
        o_ref[:, a:a + LANES] = (rot * scale).astype(o_ref.dtype)
    kv0 = MLA_Q_RANK + MLA_KV_RANK
    ckv = _rms(c[:, MLA_Q_RANK:kv0], kvg_ref[...]).astype(BF16)
    kv = _dot(ckv, wukv_ref[...])
    kn0 = MLA_QN + MLA_QR
    o_ref[:, kn0:kn0 + MLA_QN] = kv[:, :MLA_QN].astype(o_ref.dtype)
    kr = c[:, kv0:kv0 + LANES] * cos + c[:, kv0 + LANES:kv0 + 2 * LANES] * sin
    o_ref[:, MLA_OUT - LANES:] = kr.astype(o_ref.dtype)
    _store_vt(vt_ref, kv[:, MLA_QN:], tk)


def _mla_rope_layout():
    half = MLA_ROPE_DIM // 2
    per = LANES // MLA_ROPE_DIM
    dq = MLA_NOPE_DIM + MLA_ROPE_DIM
    q_nope = np.concatenate([h * dq + np.arange(MLA_NOPE_DIM) for h in range(MLA_HEADS)])
    q_rope = np.zeros(MLA_QR, np.int64)
    q_swap = np.zeros(MLA_QR, np.int64)
    for h in range(MLA_HEADS):
        m, r = divmod(h, per)
        x1 = h * dq + MLA_NOPE_DIM + np.arange(half)
        x2 = x1 + half
        first = m * LANES + r * half + np.arange(half)
        second = first + LANES // 2
        q_rope[first], q_rope[second] = x1, x2
        q_swap[first], q_swap[second] = x2, x1
    k_rope = np.zeros(LANES, np.int64)
    k_swap = np.zeros(LANES, np.int64)
    base = MLA_Q_RANK + MLA_KV_RANK
    for r in range(per):
        first = r * half + np.arange(half)
        second = first + LANES // 2
        k_rope[first], k_rope[second] = base + np.arange(half), base + half + np.arange(half)
        k_swap[first], k_swap[second] = base + half + np.arange(half), base + np.arange(half)
    dkv = MLA_NOPE_DIM + HEAD_DIM
    k_nope = np.concatenate([h * dkv + np.arange(MLA_NOPE_DIM) for h in range(MLA_HEADS)])
    v = k_nope + MLA_NOPE_DIM
    down = np.concatenate([np.arange(base), k_rope, k_swap])
    return np.concatenate([q_nope, q_rope, q_swap]), np.concatenate([k_nope, v]), down


def _mla_rope_tables(seq):
    half = MLA_ROPE_DIM // 2
    inv = ROPE_THETA ** (-jnp.arange(0, MLA_ROPE_DIM, 2, dtype=F32) / MLA_ROPE_DIM)
    ang = jnp.arange(seq, dtype=F32)[:, None] * inv[None, :]
    cos, sin = jnp.cos(ang), jnp.sin(ang)
    reps = LANES // 2 // half
    cos_t = jnp.tile(cos, (1, 2 * reps))
    sin_t = jnp.concatenate([jnp.tile(-sin, (1, reps)), jnp.tile(sin, (1, reps))], axis=1)
    return jnp.concatenate([cos_t, sin_t], axis=1)


def _mla_proj(x, g, w_down, q_norm, w_uq, kv_norm, w_ukv, seq):
    t, d = x.shape
    tm = _tile(seq, 512)
    tk = _tile(seq, ATT_TK)
    ns = seq // tm
    uq_cols, ukv_cols, down_cols = _mla_rope_layout()
    wd = w_down[:, down_cols].astype(BF16)
    wuq = w_uq[:, uq_cols].astype(BF16)
    wukv = w_ukv[:, ukv_cols].astype(BF16)
    cs = _mla_rope_tables(seq)
    scale = (MLA_NOPE_DIM + MLA_ROPE_DIM) ** -0.5 * LOG2E
    full = lambda a: pl.BlockSpec(a.shape, lambda i: (0, 0))
    g2, qg, kvg = g.reshape(1, d), q_norm.reshape(1, -1), kv_norm.reshape(1, -1)
    return pl.pallas_call(
        functools.partial(_mla_proj_kernel, scale=scale, tk=tk),
        grid=(t // tm,),
        in_specs=[pl.BlockSpec((tm, d), lambda i: (i, 0)), full(g2), full(wd), full(qg), full(wuq),
                  full(kvg), full(wukv), pl.BlockSpec((tm, 2 * LANES), lambda i: (i % ns, 0))],
        out_specs=[pl.BlockSpec((tm, MLA_OUT), lambda i: (i, 0)),
                   pl.BlockSpec((tm // tk, MLA_QN, tk), lambda i: (i, 0, 0))],
        out_shape=[jax.ShapeDtypeStruct((t, MLA_OUT), BF16),
                   jax.ShapeDtypeStruct((t // tk, MLA_QN, tk), BF16)],
        compiler_params=_params("parallel"),
        name="mla_proj",
    )(x, g2, wd, qg, wuq, kvg, wukv, cs)


def _mla_kernel(qn_ref, qr_ref, kn_ref, kr_ref, vt_ref, o_ref, acc_ref, ml_ref, s_ref, *, tq, tk):
    p = pl.program_id(1)
    i = pl.program_id(2)
    q = jnp.concatenate([qn_ref[...], qr_ref[...]], axis=1)
    lane = lax.broadcasted_iota(jnp.int32, (tq, 2 * LANES), 1)
    half = MLA_ROPE_DIM // 2
    zero = jnp.zeros_like(q)
    qms = []
    for j in range(2):
        r0 = LANES + (2 * (p % 2) + j) * half
        sel = jnp.logical_and(lane >= j * HEAD_DIM, lane < (j + 1) * HEAD_DIM)
        sel = jnp.logical_or(sel, jnp.logical_and(lane >= r0, lane < r0 + half))
        r1 = r0 + LANES // 2
        sel = jnp.logical_or(sel, jnp.logical_and(lane >= r1, lane < r1 + half))
        qms.append(jnp.where(sel, q, zero))
    acc_ref[...] = jnp.zeros_like(acc_ref)
    row = lax.broadcasted_iota(jnp.int32, ml_ref.shape, 0)
    ml_ref[...] = jnp.where(row < 2, NEG, 0.0)

    def scores(unit, _):
        jb, h, lanes, _ = unit
        start = pl.multiple_of(jb * tk, tk)
        kt = jnp.concatenate([kn_ref[pl.ds(start, tk), :], kr_ref[pl.ds(start, tk), :]], axis=1)
        return _dot_nt(kt, qms[h][lanes])

    def softmax(unit, s):
        _, h, lanes, mask = unit
        if mask is not None:
            s = jnp.where(mask, s, NEG)
        m_old = ml_ref[h:h + 1, lanes]
        m_new = jnp.maximum(m_old, jnp.max(s, axis=0, keepdims=True))
        alpha = jnp.exp2(m_old - m_new)
        e = jnp.exp2(s - m_new)
        ml_ref[h:h + 1, lanes] = m_new
        ml_ref[2 + h:3 + h, lanes] = alpha * ml_ref[2 + h:3 + h, lanes] + jnp.sum(e, axis=0, keepdims=True)
        return alpha, e.astype(BF16)

    def values(unit, val):
        jb, h, lanes, _ = unit
        alpha, e = val
        rows = slice(h * HEAD_DIM, (h + 1) * HEAD_DIM)
        acc_ref[rows, lanes] = alpha * acc_ref[rows, lanes] + _dot(vt_ref[jb, rows, :], e)

    _causal_sweep(i, tq, tk, (scores, softmax, values), False, s_ref)
    head = lax.broadcasted_iota(jnp.int32, acc_ref.shape, 0) < HEAD_DIM
    denom = jnp.where(head, ml_ref[2:3, :], ml_ref[3:4, :])
    o_ref[...] = (acc_ref[...] / denom).T.astype(o_ref.dtype)


def _mla_attention(a, vt, batch, seq):
    t = a.shape[0]
    tq = _tile(seq, ATT_TQ)
    tk = vt.shape[2]
    nq = seq // tq
    pairs = MLA_HEADS // 2
    qr0 = MLA_QN // LANES
    kn0 = (MLA_QN + MLA_QR) // LANES
    kr0 = kn0 + pairs
    return pl.pallas_call(
        functools.partial(_mla_kernel, tq=tq, tk=tk),
        grid=(batch, pairs, nq),
        in_specs=[pl.BlockSpec((tq, LANES), lambda b, p, i: (b * nq + i, p)),
                  pl.BlockSpec((tq, LANES), lambda b, p, i: (b * nq + i, qr0 + p // 2)),
                  pl.BlockSpec((seq, LANES), lambda b, p, i: (b, kn0 + p)),
                  pl.BlockSpec((seq, LANES), lambda b, p, i: (b, kr0)),
                  pl.BlockSpec((seq // tk, LANES, tk), lambda b, p, i: (b, p, 0))],
        out_specs=pl.BlockSpec((tq, LANES), lambda b, p, i: (b * nq + i, p)),
        out_shape=jax.ShapeDtypeStruct((t, MLA_HEADS * HEAD_DIM), BF16),
        scratch_shapes=[pltpu.VMEM((LANES, tq), F32), pltpu.VMEM((8, tq), F32),
                        pltpu.VMEM((2 * (tq // tk), tk, tq), F32)],
        compiler_params=_params("parallel", "parallel", "arbitrary"),
        name="mla_attention",
    )(a, a, a, a, vt)


FFN_CHUNK = 256


def _swiglu_into(acc_ref, h, wg, wu, wd):
    def gate_up(cols, _):
        return _dot(h, wg(cols)), _dot(h, wu(cols))

    def activate(cols, gu):
        g, u = gu
        return (g * jax.nn.sigmoid(g) * u).astype(BF16)

    def down(cols, a):
        acc_ref[...] += _dot(a, wd(cols))

    return gate_up, activate, down


def _run_swiglu(acc_ref, h, wg, wu, wd, width):
    fc = FFN_CHUNK if width % FFN_CHUNK == 0 else width
    chunks = [slice(c, c + fc) for c in range(0, width, fc)]
    _staggered(chunks, _swiglu_into(acc_ref, h, wg, wu, wd))


def _ffn_kernel(x_ref, g_ref, wg_ref, wu_ref, wd_ref, o_ref, h_ref):
    f = pl.program_id(1)

    @pl.when(f == 0)
    def _():
        x = x_ref[...]
        h_ref[...] = _rms(x, g_ref[...]).astype(BF16)
        o_ref[...] = x

    _run_swiglu(o_ref, h_ref[...], lambda c: wg_ref[:, c], lambda c: wu_ref[:, c], lambda c: wd_ref[c, :],
                wg_ref.shape[1])


def _ffn(x, g, wg, wu, wd):
    t, d = x.shape
    fdim = wg.shape[1]
    tm = _tile(t, 512)
    tf = fdim
    return pl.pallas_call(
        _ffn_kernel,
        grid=(t // tm, fdim // tf),
        in_specs=[pl.BlockSpec((tm, d), lambda i, f: (i, 0)),
                  pl.BlockSpec((1, d), lambda i, f: (0, 0)),
                  pl.BlockSpec((d, tf), lambda i, f: (0, f)),
                  pl.BlockSpec((d, tf), lambda i, f: (0, f)),
                  pl.BlockSpec((tf, d), lambda i, f: (f, 0))],
        out_specs=pl.BlockSpec((tm, d), lambda i, f: (i, 0)),
        out_shape=jax.ShapeDtypeStruct((t, d), F32),
        scratch_shapes=[pltpu.VMEM((tm, d), BF16)],
        compiler_params=_params("parallel", "arbitrary"),
        name="dense_ffn",
    )(x, g.reshape(1, d), wg, wu, wd)


MOE_TILE = 512
META_ROWS = 8
ROUTE_SUB = 128
COMBINE_SUB = 256


def _pack_halves(x):
    half = x.shape[1] // 2
    bits = lax.bitcast_convert_type(x.astype(BF16).astype(F32), jnp.uint32)
    return bits[:, :half] | (bits[:, half:] >> 16)


def _unpack_halves(w):
    hi = lax.bitcast_convert_type(w & jnp.uint32(0xFFFF0000), F32)
    lo = lax.bitcast_convert_type(w << 16, F32)
    return jnp.concatenate([hi, lo], axis=1)


def _router_kernel(x_ref, g_ref, rt_ref, su_ref, eye_ref, hb_ref, meta_ref, metac_ref, cnt_ref, carry_ref):
    @pl.when(pl.program_id(0) == 0)
    def _():
        carry_ref[...] = jnp.zeros_like(carry_ref)

    h = _rms(x_ref[...], g_ref[...])
    hb_ref[...] = _pack_halves(h)
    logits = lax.dot_general(rt_ref[...], h, _NT, precision=lax.Precision.HIGHEST,
                             preferred_element_type=F32)
    ne, ct = logits.shape
    eio = lax.broadcasted_iota(jnp.int32, (ne, ct), 0).astype(F32)
    m1 = jnp.max(logits, axis=0, keepdims=True)
    i1 = jnp.min(jnp.where(logits == m1, eio, float(ne)), axis=0, keepdims=True)
    rest = jnp.where(eio == i1, -jnp.inf, logits)
    m2 = jnp.max(rest, axis=0, keepdims=True)
    i2 = jnp.min(jnp.where(rest == m2, eio, float(ne)), axis=0, keepdims=True)
    e2 = jnp.exp(m2 - m1)
    g1 = 1.0 / (1.0 + e2)
    g2 = e2 / (1.0 + e2)
    oh1 = (eio == i1).astype(F32)
    oh2 = (eio == i2).astype(F32)
    both = oh1 + oh2
    seen = carry_ref[:, 0:1] + _dot(both.astype(BF16), su_ref[...])
    r1 = jnp.sum(oh1 * seen, axis=0, keepdims=True)
    r2 = jnp.sum(oh2 * seen, axis=0, keepdims=True)
    meta = jnp.zeros((META_ROWS, ct), F32)
    for row, val in enumerate((i1, i2, r1, r2, g1, g2)):
        meta = jnp.where(eio == float(row), val, meta)
    meta_ref[...] = meta
    metac_ref[...] = lax.dot_general(eye_ref[...], meta, _NT, precision=lax.Precision.HIGHEST,
                                     preferred_element_type=F32)
    carry_ref[...] = carry_ref[...] + jnp.sum(both, axis=1, keepdims=True)
    cnt_ref[0] = carry_ref[...]


def _router(x, g, router):
    t, d = x.shape
    ct = _tile(t, MOE_TILE)
    nc = t // ct
    su = jnp.asarray(np.triu(np.ones((ct, ct), np.float32), 1), BF16)
    eye = jnp.eye(ct, dtype=F32)
    return pl.pallas_call(
        _router_kernel,
        grid=(nc,),
        in_specs=[pl.BlockSpec((ct, d), lambda i: (i, 0)),
                  pl.BlockSpec((1, d), lambda i: (0, 0)),
                  pl.BlockSpec((N_EXPERTS, d), lambda i: (0, 0)),
                  pl.BlockSpec((ct, ct), lambda i: (0, 0)),
                  pl.BlockSpec((ct, ct), lambda i: (0, 0))],
        out_specs=[pl.BlockSpec((ct, d // 2), lambda i: (i, 0)),
                   pl.BlockSpec((META_ROWS, ct), lambda i: (0, i)),
                   pl.BlockSpec((ct, META_ROWS), lambda i: (i, 0)),
                   pl.BlockSpec((1, N_EXPERTS, LANES), lambda i: (i, 0, 0))],
        out_shape=[jax.ShapeDtypeStruct((t, d // 2), jnp.uint32),
                   jax.ShapeDtypeStruct((META_ROWS, t), F32),
                   jax.ShapeDtypeStruct((t, META_ROWS), F32),
                   jax.ShapeDtypeStruct((nc, N_EXPERTS, LANES), F32)],
        scratch_shapes=[pltpu.VMEM((N_EXPERTS, LANES), F32)],
        compiler_params=_params("arbitrary"),
        name="moe_router",
    )(x, g.reshape(1, d), router.T, su, eye)


def _moe_plan(cnt_after, t, ct):
    nc = t // ct
    rows = 2 * t + N_EXPERTS * ct
    nrt = rows // ct
    counts = cnt_after[:, :, 0].astype(jnp.int32)
    total = counts[-1]
    padded = (total + ct - 1) // ct * ct
    off_end = jnp.cumsum(padded)
    off = off_end - padded
    before = jnp.concatenate([jnp.zeros((1, N_EXPERTS), jnp.int32), counts[:-1]], axis=0)
    chunk_start = (off[:, None] + before.T).reshape(-1)
    tile_start = jnp.arange(nrt, dtype=jnp.int32) * ct
    starts = jnp.sort(jnp.concatenate([chunk_start, tile_start]))
    ends = jnp.concatenate([starts[1:], jnp.array([rows], jnp.int32)])
    item_r = jnp.minimum(starts // ct, nrt - 1)
    count_le = lambda sorted_vals, x: jnp.sum((sorted_vals[None, :] <= x[:, None]).astype(jnp.int32), axis=1)
    item_c = (count_le(chunk_start, starts) - 1) % nc
    first = lambda key: jnp.concatenate([jnp.ones((1,), jnp.int32), (key[1:] != key[:-1]).astype(jnp.int32)])
    dispatch = (item_r, item_c, starts, ends, first(item_r))
    order = jnp.argsort(item_c * (rows + 1) + starts)
    c2 = item_c[order]
    combine = (c2, item_r[order], starts[order], ends[order], first(c2))
    tile_expert = jnp.minimum(count_le(off_end, tile_start), N_EXPERTS - 1)
    tile_valid = (tile_start < off_end[-1]).astype(jnp.int32)
    return rows, off.astype(jnp.int32), dispatch, combine, tile_expert.astype(jnp.int32), tile_valid


def _positions(expert, rank, off_ref):
    base = jnp.zeros_like(rank)
    for e in range(N_EXPERTS):
        base = jnp.where(expert == float(e), off_ref[e].astype(F32), base)
    return base + rank


def _dispatch_kernel(r_ref, c_ref, s_ref, e_ref, first_ref, off_ref, meta_ref, h_ref, xs_ref, gate_ref):
    n = pl.program_id(0)
    rt, ct = xs_ref.shape[0], h_ref.shape[0]

    @pl.when(first_ref[n] == 1)
    def _():
        xs_ref[...] = jnp.zeros_like(xs_ref)
        gate_ref[...] = jnp.zeros_like(gate_ref)

    meta = meta_ref[...]
    pos0 = _positions(meta[0:1], meta[2:3], off_ref)
    pos1 = _positions(meta[1:2], meta[3:4], off_ref)
    sub = min(rt, ROUTE_SUB)
    for sb in range(rt // sub):
        base = r_ref[n] * rt + sb * sub

        @pl.when(jnp.logical_and(base < e_ref[n], base + sub > s_ref[n]))
        def _():
            row = lax.broadcasted_iota(jnp.int32, (sub, ct), 0) + base
            live = jnp.logical_and(row >= s_ref[n], row < e_ref[n])
            rowf = row.astype(F32)
            sel0 = jnp.logical_and(pos0 == rowf, live)
            sel1 = jnp.logical_and(pos1 == rowf, live)
            sel = jnp.logical_or(sel0, sel1).astype(F32).astype(BF16)
            rows = slice(sb * sub, (sb + 1) * sub)
            xs_ref[rows, :] += _pack_halves(_dot(sel, _unpack_halves(h_ref[...]).astype(BF16)))
            gate = jnp.where(sel0, meta[4:5], 0.0) + jnp.where(sel1, meta[5:6], 0.0)
            gate_ref[rows, :] += jnp.sum(gate, axis=1, keepdims=True)


def _dispatch(plan, off, meta, hb, rows, ct):
    t, d = hb.shape
    n_items = plan[0].shape[0]
    return pl.pallas_call(
        _dispatch_kernel,
        grid_spec=pltpu.PrefetchScalarGridSpec(
            num_scalar_prefetch=6,
            grid=(n_items,),
            in_specs=[pl.BlockSpec((META_ROWS, ct), lambda n, r, c, *_: (0, c[n])),
                      pl.BlockSpec((ct, d), lambda n, r, c, *_: (c[n], 0))],
            out_specs=[pl.BlockSpec((ct, d), lambda n, r, c, *_: (r[n], 0)),
                       pl.BlockSpec((ct, 1), lambda n, r, c, *_: (r[n], 0))]),
        out_shape=[jax.ShapeDtypeStruct((rows, d), jnp.uint32),
                   jax.ShapeDtypeStruct((rows, 1), F32)],
        compiler_params=_params("arbitrary"),
        name="moe_dispatch",
    )(*plan, off, meta, hb)


def _expert_kernel(te_ref, tv_ref, xs_ref, gate_ref, wg_ref, wu_ref, wd_ref, y_ref, acc_ref):
    r = pl.program_id(0)
    f = pl.program_id(1)

    @pl.when(f == 0)
    def _():
        acc_ref[...] = jnp.zeros_like(acc_ref)

    @pl.when(tv_ref[r] == 1)
    def _():
        _run_swiglu(acc_ref, _unpack_halves(xs_ref[...]).astype(BF16), lambda c: wg_ref[0, :, c],
                    lambda c: wu_ref[0, :, c], lambda c: wd_ref[0, c, :], wg_ref.shape[2])

    @pl.when(f == pl.num_programs(1) - 1)
    def _():
        y_ref[...] = _pack_halves(acc_ref[...] * gate_ref[...])


def _experts(tile_expert, tile_valid, xs, gate, wg, wu, wd, ct):
    rows, dp = xs.shape
    d = 2 * dp
    fdim = wg.shape[2]
    tf = _tile(fdim, 1792)
    nf = fdim // tf

    def wcol(r, f, te, tv):
        return (te[r], 0, jnp.where(tv[r] == 1, f, nf - 1))

    def wrow(r, f, te, tv):
        return (te[r], jnp.where(tv[r] == 1, f, nf - 1), 0)

    return pl.pallas_call(
        _expert_kernel,
        grid_spec=pltpu.PrefetchScalarGridSpec(
            num_scalar_prefetch=2,
            grid=(rows // ct, nf),
            in_specs=[pl.BlockSpec((ct, dp), lambda r, f, te, tv: (r, 0)),
                      pl.BlockSpec((ct, 1), lambda r, f, te, tv: (r, 0)),
                      pl.BlockSpec((1, d, tf), wcol),
                      pl.BlockSpec((1, d, tf), wcol),
                      pl.BlockSpec((1, tf, d), wrow)],
            out_specs=pl.BlockSpec((ct, dp), lambda r, f, te, tv: (r, 0)),
            scratch_shapes=[pltpu.VMEM((ct, d), F32)]),
        out_shape=jax.ShapeDtypeStruct((rows, dp), jnp.uint32),
        compiler_params=_params("arbitrary", "arbitrary"),
        name="moe_experts",
    )(tile_expert, tile_valid, xs, gate, wg, wu, wd)


def _combine_kernel(c_ref, r_ref, s_ref, e_ref, first_ref, off_ref, x_ref, metac_ref, y_ref, o_ref):
    n = pl.program_id(0)
    ct, rt = x_ref.shape[0], y_ref.shape[0]

    @pl.when(first_ref[n] == 1)
    def _():
        o_ref[...] = x_ref[...]

    metac = metac_ref[...]
    pos0 = _positions(metac[:, 0:1], metac[:, 2:3], off_ref)
    pos1 = _positions(metac[:, 1:2], metac[:, 3:4], off_ref)
    sub = min(rt, COMBINE_SUB)
    for sb in range(rt // sub):
        base = r_ref[n] * rt + sb * sub

        @pl.when(jnp.logical_and(base < e_ref[n], base + sub > s_ref[n]))
        def _():
            row = lax.broadcasted_iota(jnp.int32, (ct, sub), 1) + base
            live = jnp.logical_and(row >= s_ref[n], row < e_ref[n])
            rowf = row.astype(F32)
            sel = jnp.logical_and(jnp.logical_or(pos0 == rowf, pos1 == rowf), live).astype(F32).astype(BF16)
            o_ref[...] += _dot(sel, _unpack_halves(y_ref[sb * sub:(sb + 1) * sub, :]).astype(BF16))


def _combine(plan, off, x, metac, y, ct):
    t, d = x.shape
    n_items = plan[0].shape[0]
    return pl.pallas_call(
        _combine_kernel,
        grid_spec=pltpu.PrefetchScalarGridSpec(
            num_scalar_prefetch=6,
            grid=(n_items,),
            in_specs=[pl.BlockSpec((ct, d), lambda n, c, r, *_: (c[n], 0)),
                      pl.BlockSpec((ct, META_ROWS), lambda n, c, r, *_: (c[n], 0)),
                      pl.BlockSpec((ct, d // 2), lambda n, c, r, *_: (r[n], 0))],
            out_specs=pl.BlockSpec((ct, d), lambda n, c, r, *_: (c[n], 0))),
        out_shape=jax.ShapeDtypeStruct((t, d), F32),
        compiler_params=_params("arbitrary"),
        name="moe_combine",
    )(*plan, off, x, metac, y)


def _cast_kernel(w_ref, o_ref):
    o_ref[...] = w_ref[...].astype(o_ref.dtype)


CAST_BLOCK_BYTES = 8 * 1024 * 1024


def _layer_bf16(w, layer):
    shape = w.shape[1:]
    cols = shape[-1]
    rows = math.prod(shape[:-1])
    tr = rows
    while tr * cols * 4 > CAST_BLOCK_BYTES and tr % 16 == 0:
        tr //= 2
    out = pl.pallas_call(
        _cast_kernel,
        grid=(rows // tr,),
        in_specs=[pl.BlockSpec((None, tr, cols), lambda i: (layer, i, 0))],
        out_specs=pl.BlockSpec((tr, cols), lambda i: (i, 0)),
        out_shape=jax.ShapeDtypeStruct((rows, cols), BF16),
        compiler_params=_params("parallel"),
        name="weights_bf16",
    )(w.reshape(w.shape[0], rows, cols))
    return out.reshape(shape)


SC_CORES = 2
SC_SUBCORES = 16
SC_CHUNK = 128


def _sc_gather(table, idx):
    v, d = table.shape
    b = idx.shape[0]
    workers = SC_CORES * SC_SUBCORES
    per = b // workers

    def body(table_hbm, idx_hbm, out_hbm, idx_v, rows_v, sem):
        wid = lax.axis_index("s") * SC_CORES + lax.axis_index("c")

        @pl.loop(0, per // SC_CHUNK)
        def _(j):
            start = wid * per + j * SC_CHUNK
            pltpu.sync_copy(idx_hbm.at[pl.ds(start, SC_CHUNK)], idx_v)
            pltpu.async_copy(table_hbm.at[idx_v], rows_v, sem).wait()
            pltpu.sync_copy(rows_v, out_hbm.at[pl.ds(start, SC_CHUNK)])

    out = pl.kernel(
        body,
        out_type=jax.ShapeDtypeStruct((b, d), table.dtype),
        mesh=plsc.VectorSubcoreMesh(core_axis_name="c", subcore_axis_name="s"),
        scratch_types=[pltpu.VMEM((SC_CHUNK,), jnp.int32),
                       pltpu.VMEM((SC_CHUNK, d), table.dtype),
                       pltpu.SemaphoreType.DMA],
        name="sc_row_gather",
    )(table, idx)
    return out


def _weighted_add_kernel(x_ref, metac_ref, y0_ref, y1_ref, o_ref):
    gates = metac_ref[...]
    o_ref[...] = (x_ref[...] + gates[:, 4:5] * _unpack_halves(y0_ref[...])
                  + gates[:, 5:6] * _unpack_halves(y1_ref[...]))


def _weighted_add(x, metac, ysel):
    t, d = x.shape
    tm = _tile(t, 512)
    nt = t // tm
    return pl.pallas_call(
        _weighted_add_kernel,
        grid=(nt,),
        in_specs=[pl.BlockSpec((tm, d), lambda i: (i, 0)),
                  pl.BlockSpec((tm, META_ROWS), lambda i: (i, 0)),
                  pl.BlockSpec((tm, d // 2), lambda i: (i, 0)),
                  pl.BlockSpec((tm, d // 2), lambda i: (nt + i, 0))],
        out_specs=pl.BlockSpec((tm, d), lambda i: (i, 0)),
        out_shape=jax.ShapeDtypeStruct((t, d), F32),
        compiler_params=_params("parallel"),
        name="moe_weighted_add",
    )(x, metac, ysel, ysel)


def _moe(x, g, router, wg, wu, wd):
    t = x.shape[0]
    ct = _tile(t, MOE_TILE)
    hb, meta, metac, cnt = _router(x, g, router)
    rows, off, dispatch, combine, tile_expert, tile_valid = _moe_plan(cnt, t, ct)
    if rows % (SC_CORES * SC_SUBCORES * SC_CHUNK) == 0 and (2 * t) % (SC_CORES * SC_SUBCORES * SC_CHUNK) == 0:
        expert = meta[0:2].astype(jnp.int32)
        pos = (off[expert] + meta[2:4].astype(jnp.int32)).reshape(-1)
        token = jnp.tile(jnp.arange(t, dtype=jnp.int32), 2)
        row_token = jnp.zeros((rows,), jnp.int32).at[pos].set(token)
        xs = _sc_gather(hb, row_token)
        y = _experts(tile_expert, tile_valid, xs, jnp.ones((rows, 1), F32), wg, wu, wd, ct)
        return _weighted_add(x, metac, _sc_gather(y, pos))
    xs, gate = _dispatch(dispatch, off, meta, hb, rows, ct)
    y = _experts(tile_expert, tile_valid, xs, gate, wg, wu, wd, ct)
    return _combine(combine, off, x, metac, y, ct)


def _final_norm_kernel(x_ref, g_ref, o_ref):
    o_ref[...] = _rms(x_ref[...], g_ref[...])


def _final_norm(x, g):
    t, d = x.shape
    tm = _tile(t, 1024)
    return pl.pallas_call(
        _final_norm_kernel,
        grid=(t // tm,),
        in_specs=[pl.BlockSpec((tm, d), lambda i: (i, 0)), pl.BlockSpec((1, d), lambda i: (0, 0))],
        out_specs=pl.BlockSpec((tm, d), lambda i: (i, 0)),
        out_shape=jax.ShapeDtypeStruct((t, d), F32),
        compiler_params=_params("parallel"),
        name="final_norm",
    )(x, g.reshape(1, d))


def kernel(x, rel_bias, attn_norm, ffn_norm, final_norm, swa_w_qkv, swa_b_qkv, swa_sinks, swa_w_o, sb_w_qkv, sb_w_o, mla_w_down, mla_q_norm, mla_w_uq, mla_kv_norm, mla_w_ukv, mla_w_o, ffn_w_gate, ffn_w_up, ffn_w_down, moe_router, moe_w_gate, moe_w_up, moe_w_down):
    batch, seq, d = x.shape
    depth = attn_norm.shape[0]
    xt = x.reshape(batch * seq, d)
    swa_bias = _swa_bias(rel_bias)
    for i in range(depth):
        mixer, j = i % N_MIXERS, i // N_MIXERS
        if mixer == 0:
            w, b, cs = _swa_weights(swa_w_qkv[j], swa_b_qkv[j])
            qkv = _norm_proj(xt, attn_norm[i], w, b, cs, "swa_proj")
            o = _swa_attention(qkv, swa_bias, swa_sinks[j], batch, seq)
            xt = _out_proj(xt, o, swa_w_o[j].astype(BF16), "swa_out")
        elif mixer == 1:
            n = sb_w_qkv.shape[2] // 3
            cs = jnp.concatenate([jnp.full((n,), HEAD_DIM ** -0.5 * LOG2E, F32), jnp.ones((n,), F32)])
            qk, vt = _norm_proj_vt(xt, attn_norm[i], sb_w_qkv[j].astype(BF16), cs, n, seq, "sb_proj")
            o = _sb_attention(qk, vt, batch, seq)
            xt = _out_proj(xt, o, sb_w_o[j].astype(BF16), "sb_out")
        else:
            a, vt = _mla_proj(xt, attn_norm[i], mla_w_down[j], mla_q_norm[j], mla_w_uq[j],
                              mla_kv_norm[j], mla_w_ukv[j], seq)
            o = _mla_attention(a, vt, batch, seq)
            xt = _out_proj(xt, o, mla_w_o[j].astype(BF16), "mla_out")
        f = i // 2
        if i % 2 == 0:
            xt = _ffn(xt, ffn_norm[i], _layer_bf16(ffn_w_gate, f), _layer_bf16(ffn_w_up, f),
                      _layer_bf16(ffn_w_down, f))
        else:
            xt = _moe(xt, ffn_norm[i], moe_router[f], _layer_bf16(moe_w_gate, f), _layer_bf16(moe_w_up, f),
                      _layer_bf16(moe_w_down, f))
    return _final_norm(xt, final_norm).reshape(batch, seq, d)
```

```python
import functools
import math

import numpy as np
import jax
import jax.numpy as jnp
from jax import lax
from jax.experimental import pallas as pl
from jax.experimental.pallas import tpu as pltpu
from jax.experimental.pallas import tpu_sc as plsc

F32 = jnp.float32
BF16 = jnp.bfloat16

N_MIXERS = 3
RMS_EPS = 1e-6
SWA_HEADS = 16
SWA_KV_HEADS = 2
HEAD_DIM = 64
SWA_BLOCK = 128
SWA_WINDOW = 128
REL_BUCKETS = 32
REL_MAX_DIST = 128
SB_HEADS = 16
MLA_HEADS = 16
MLA_NOPE_DIM = 64
MLA_ROPE_DIM = 32
MLA_Q_RANK = 384
MLA_KV_RANK = 256
ROPE_THETA = 10000.0
N_EXPERTS = 8

LANES = 128
NEG = -1e30
LOG2E = math.log2(math.e)
VMEM_LIMIT = 56 * 1024 * 1024

_NT = (((1,), (1,)), ((), ()))


def _dot(a, b):
    return jnp.dot(a, b, preferred_element_type=F32)


def _dot_nt(a, b):
    return lax.dot_general(a, b, _NT, preferred_element_type=F32)


def _rms(x, g):
    return x * lax.rsqrt(jnp.mean(x * x, axis=-1, keepdims=True) + RMS_EPS) * g


def _params(*sem):
    return pltpu.CompilerParams(dimension_semantics=sem, vmem_limit_bytes=VMEM_LIMIT)


def _tile(n, pref):
    return pref if n % pref == 0 else n


def _norm_proj_kernel(x_ref, g_ref, w_ref, b_ref, cs_ref, o_ref):
    h = _rms(x_ref[...], g_ref[...]).astype(BF16)
    y = (_dot(h, w_ref[...]) + b_ref[...]) * cs_ref[...]
    o_ref[...] = y.astype(o_ref.dtype)


def _norm_proj(x, g, w, b, colscale, name):
    t, d = x.shape
    n = w.shape[1]
    tm = _tile(t, 512)
    return pl.pallas_call(
        _norm_proj_kernel,
        grid=(t // tm,),
        in_specs=[pl.BlockSpec((tm, d), lambda i: (i, 0)),
                  pl.BlockSpec((1, d), lambda i: (0, 0)),
                  pl.BlockSpec((d, n), lambda i: (0, 0)),
                  pl.BlockSpec((1, n), lambda i: (0, 0)),
                  pl.BlockSpec((1, n), lambda i: (0, 0))],
        out_specs=pl.BlockSpec((tm, n), lambda i: (i, 0)),
        out_shape=jax.ShapeDtypeStruct((t, n), BF16),
        compiler_params=_params("parallel"),
        name=name,
    )(x, g.reshape(1, d), w, b.reshape(1, n), colscale.reshape(1, n))


ATT_TQ = 512
ATT_TK = 256


def _store_vt(vt_ref, v, tk):
    for s in range(v.shape[0] // tk):
        vt_ref[s] = v[s * tk:(s + 1) * tk, :].T.astype(vt_ref.dtype)


def _norm_proj_vt_kernel(x_ref, g_ref, w_ref, cs_ref, o_ref, vt_ref, *, tk):
    h = _rms(x_ref[...], g_ref[...]).astype(BF16)
    y = _dot(h, w_ref[...])
    n = o_ref.shape[1]
    o_ref[...] = (y[:, :n] * cs_ref[...]).astype(o_ref.dtype)
    _store_vt(vt_ref, y[:, n:], tk)


def _norm_proj_vt(x, g, w, colscale, nv, seq, name):
    t, d = x.shape
    n = w.shape[1] - nv
    tm = _tile(seq, 512)
    tk = _tile(seq, ATT_TK)
    per = tm // tk
    return pl.pallas_call(
        functools.partial(_norm_proj_vt_kernel, tk=tk),
        grid=(t // tm,),
        in_specs=[pl.BlockSpec((tm, d), lambda i: (i, 0)),
                  pl.BlockSpec((1, d), lambda i: (0, 0)),
                  pl.BlockSpec((d, n + nv), lambda i: (0, 0)),
                  pl.BlockSpec((1, n), lambda i: (0, 0))],
        out_specs=[pl.BlockSpec((tm, n), lambda i: (i, 0)),
                   pl.BlockSpec((per, nv, tk), lambda i: (i, 0, 0))],
        out_shape=[jax.ShapeDtypeStruct((t, n), BF16),
                   jax.ShapeDtypeStruct((t // tk, nv, tk), BF16)],
        compiler_params=_params("parallel"),
        name=name,
    )(x, g.reshape(1, d), w, colscale.reshape(1, n))


def _out_proj_kernel(x_ref, o_ref, w_ref, y_ref):
    y_ref[...] = x_ref[...] + _dot(o_ref[...], w_ref[...])


def _out_proj(x, o, w, name):
    t, d = x.shape
    k = o.shape[1]
    tm = _tile(t, 512)
    return pl.pallas_call(
        _out_proj_kernel,
        grid=(t // tm,),
        in_specs=[pl.BlockSpec((tm, d), lambda i: (i, 0)),
                  pl.BlockSpec((tm, k), lambda i: (i, 0)),
                  pl.BlockSpec((k, d), lambda i: (0, 0))],
        out_specs=pl.BlockSpec((tm, d), lambda i: (i, 0)),
        out_shape=jax.ShapeDtypeStruct((t, d), F32),
        compiler_params=_params("parallel"),
        name=name,
    )(x, o, w)


def _t5_bucket_table():
    qi = np.arange(SWA_BLOCK)[:, None]
    kj = np.arange(2 * SWA_BLOCK)[None, :]
    dist = qi + SWA_BLOCK - kj
    d0 = np.maximum(dist, 0)
    max_exact = REL_BUCKETS // 2
    d = np.maximum(d0, 1).astype(np.float32)
    large = max_exact + (np.log(d / max_exact) / math.log(REL_MAX_DIST / max_exact)
                         * (REL_BUCKETS - max_exact)).astype(np.int32)
    large = np.minimum(large, REL_BUCKETS - 1)
    bucket = np.where(d0 < max_exact, d0, large)
    band = (dist >= 0) & (dist < SWA_WINDOW)
    return np.where(band, bucket, -1).astype(np.int32)


def _swa_bias_kernel(rel_ref, bucket_ref, o_ref):
    h = pl.program_id(0)
    bucket = bucket_ref[...]
    acc = jnp.full(bucket.shape, NEG, F32)
    for b in range(REL_BUCKETS):
        acc = jnp.where(bucket == b, rel_ref[b, h], acc)
    o_ref[0] = acc


def _swa_bias(rel_bias):
    bucket = jnp.asarray(_t5_bucket_table())
    q, k = bucket.shape
    return pl.pallas_call(
        _swa_bias_kernel,
        grid=(SWA_HEADS,),
        in_specs=[pl.BlockSpec(memory_space=pltpu.SMEM),
                  pl.BlockSpec((q, k), lambda h: (0, 0))],
        out_specs=pl.BlockSpec((1, q, k), lambda h: (h, 0, 0)),
        out_shape=jax.ShapeDtypeStruct((SWA_HEADS, q, k), F32),
        compiler_params=_params("arbitrary"),
        name="swa_bias",
    )(rel_bias, bucket)


def _swa_kernel(sink_ref, q_ref, kc_ref, kp_ref, vc_ref, vp_ref, bias_ref, o_ref, *, tq):
    i = pl.program_id(1)
    g = pl.program_id(2)
    blk = SWA_BLOCK
    heads = bias_ref.shape[0]
    lo_half = lax.broadcasted_iota(jnp.int32, (blk, LANES), 1) < HEAD_DIM
    in_prev = lax.broadcasted_iota(jnp.int32, (heads * blk, 2 * blk), 1) < blk
    bias = bias_ref[...].reshape(heads * blk, 2 * blk)

    def window(cur_ref, prev_ref, sb):
        if sb == 0:
            return jnp.concatenate([prev_ref[...], cur_ref[0:blk, :]], axis=0)
        return cur_ref[(sb - 1) * blk:(sb + 1) * blk, :]

    def scores(sb, _):
        q = q_ref[sb * blk:(sb + 1) * blk, :]
        parts = []
        for h in range(heads):
            qp = q[:, (h // 2) * LANES:(h // 2 + 1) * LANES]
            keep = lo_half if h % 2 == 0 else jnp.logical_not(lo_half)
            parts.append(jnp.where(keep, qp, jnp.zeros_like(qp)))
        s = _dot_nt(jnp.concatenate(parts, axis=0), window(kc_ref, kp_ref, sb)) + bias
        if sb == 0:
            s = jnp.where(jnp.logical_and(in_prev, i == 0), NEG, s)
        return s

    def softmax(sb, s):
        ps, inv = [], []
        for h in range(heads):
            sh = s[h * blk:(h + 1) * blk]
            sink = sink_ref[g * heads + h]
            m = jnp.maximum(jnp.max(sh, axis=1, keepdims=True), sink)
            e = jnp.exp(sh - m)
            inv.append(1.0 / (jnp.sum(e, axis=1, keepdims=True) + jnp.exp(sink - m)))
            ps.append(e.astype(BF16))
        return jnp.concatenate(ps, axis=0), inv

    def values(sb, val):
        p, inv = val
        o = _dot(p, window(vc_ref, vp_ref, sb))
        outs = [jnp.where(lo_half, o[h * blk:(h + 1) * blk] * inv[h], o[(h + 1) * blk:(h + 2) * blk] * inv[h + 1])
                for h in range(0, heads, 2)]
        o_ref[sb * blk:(sb + 1) * blk, :] = jnp.concatenate(outs, axis=1).astype(o_ref.dtype)

    _staggered(list(range(tq // blk)), (scores, softmax, values))


def _swa_attention(qkv, bias, sinks, batch, seq):
    t = qkv.shape[0]
    tq = _tile(seq, 512)
    nt = seq // tq
    group = SWA_HEADS // SWA_KV_HEADS
    gw = group * HEAD_DIM
    kcol = SWA_HEADS * HEAD_DIM // LANES
    vcol = kcol + SWA_KV_HEADS
    per = tq // SWA_BLOCK

    def prev_idx(b, i, g):
        return jnp.maximum(b * (seq // SWA_BLOCK) + i * per - 1, 0)

    return pl.pallas_call(
        functools.partial(_swa_kernel, tq=tq),
        grid=(batch, nt, SWA_KV_HEADS),
        in_specs=[pl.BlockSpec(memory_space=pltpu.SMEM),
                  pl.BlockSpec((tq, gw), lambda b, i, g: (b * nt + i, g)),
                  pl.BlockSpec((tq, LANES), lambda b, i, g: (b * nt + i, kcol + g)),
                  pl.BlockSpec((SWA_BLOCK, LANES), lambda b, i, g: (prev_idx(b, i, g), kcol + g)),
                  pl.BlockSpec((tq, LANES), lambda b, i, g: (b * nt + i, vcol + g)),
                  pl.BlockSpec((SWA_BLOCK, LANES), lambda b, i, g: (prev_idx(b, i, g), vcol + g)),
                  pl.BlockSpec((group, SWA_BLOCK, 2 * SWA_BLOCK), lambda b, i, g: (g, 0, 0))],
        out_specs=pl.BlockSpec((tq, gw), lambda b, i, g: (b * nt + i, g)),
        out_shape=jax.ShapeDtypeStruct((t, SWA_HEADS * HEAD_DIM), BF16),
        compiler_params=_params("parallel", "parallel", "arbitrary"),
        name="swa_attention",
    )(sinks, qkv, qkv, qkv, qkv, qkv, bias)


def _swa_weights(w_qkv, b_qkv):
    nq = SWA_HEADS * HEAD_DIM
    kv = SWA_KV_HEADS * HEAD_DIM
    dup = np.concatenate([np.tile(np.arange(HEAD_DIM), 2) + h * HEAD_DIM for h in range(SWA_KV_HEADS)])
    cols = np.concatenate([np.arange(nq), nq + dup, nq + kv + dup])
    scale = np.concatenate([np.full(nq, HEAD_DIM ** -0.5), np.ones(2 * dup.size)]).astype(np.float32)
    return w_qkv[:, cols].astype(BF16), b_qkv[cols], jnp.asarray(scale)


def _staggered(units, stages):
    vals = list(units)
    for t in range(len(units) + len(stages) - 1):
        for k, stage in enumerate(stages):
            u = t - k
            if 0 <= u < len(units):
                vals[u] = stage(units[u], vals[u])


def _score_ahead(units, next_units, stages, s_ref):
    rest = stages[1:]
    vals = [None] * len(units)
    for t in range(len(units) + len(rest) - 1):
        ahead = stages[0](next_units[t], None) if t < len(units) else None
        for k, stage in enumerate(rest):
            u = t - k
            if 0 <= u < len(units):
                if k == 0:
                    lanes = units[u][2]
                    vals[u] = stage(units[u], s_ref[u, :, :lanes.stop - lanes.start])
                    s_ref[u] = ahead
                else:
                    vals[u] = stage(units[u], vals[u])


def _causal_sweep(i, tq, tk, stages, strict, s_ref):
    nsub = tq // tk

    def group(j):
        first = jnp.maximum(j, 0) * nsub
        return [(first + sb, h, slice(0, tq), None) for sb in reversed(range(nsub)) for h in range(2)]

    diagonal = []
    for sb in reversed(range(nsub)):
        nl = tq - sb * tk
        key = lax.broadcasted_iota(jnp.int32, (tk, nl), 0)
        qry = lax.broadcasted_iota(jnp.int32, (tk, nl), 1)
        mask = key < qry if strict else key <= qry
        diagonal += [(i * nsub + sb, h, slice(sb * tk, tq), mask) for h in range(2)]
    for u, unit in enumerate(diagonal):
        s_ref[u, :, :tq - unit[2].start] = stages[0](unit, None)
    _score_ahead(diagonal, group(i - 1), stages, s_ref)

    def body(n, carry):
        _score_ahead(group(i - 1 - n), group(i - 2 - n), stages, s_ref)
        return carry

    lax.fori_loop(0, i, body, 0)


def _split_pair(q):
    lo_half = lax.broadcasted_iota(jnp.int32, q.shape, 1) < HEAD_DIM
    zero = jnp.zeros_like(q)
    return jnp.where(lo_half, q, zero), jnp.where(lo_half, zero, q)


def _sb_kernel(q_ref, k_ref, vt_ref, u_ref, o_ref, acc_ref, c_ref, s_ref, *, tq, tk):
    i = pl.program_id(2)
    qms = _split_pair(q_ref[...])
    u = u_ref[...]
    acc_ref[...] = jnp.zeros_like(acc_ref)
    c_ref[...] = jnp.zeros_like(c_ref)

    def scores(unit, _):
        jb, h, lanes, _ = unit
        kt = k_ref[pl.ds(pl.multiple_of(jb * tk, tk), tk), :]
        return _dot_nt(kt, qms[h][lanes])

    def log_fail(unit, z):
        neg_abs = lax.bitcast_convert_type(lax.bitcast_convert_type(z, jnp.uint32) | jnp.uint32(1 << 31), F32)
        sp = jnp.maximum(z, 0.0) + jnp.log2(1.0 + jnp.exp2(neg_abs))
        if unit[3] is not None:
            sp = jnp.where(unit[3], sp, 0.0)
        return z, sp.astype(BF16)

    def suffix_sum(unit, val):
        z, sp = val
        return z, _dot(u, sp)

    def weights(unit, val):
        _, h, lanes, mask = unit
        z, r = val
        c = c_ref[h:h + 1, lanes]
        la = z - r - c
        if mask is not None:
            la = jnp.where(mask, la, NEG)
        c_ref[h:h + 1, lanes] = c + r[0:1, :]
        return jnp.exp2(la).astype(BF16)

    def values(unit, a):
        jb, h, lanes, _ = unit
        rows = slice(h * HEAD_DIM, (h + 1) * HEAD_DIM)
        acc_ref[rows, lanes] += _dot(vt_ref[jb, rows, :], a)

    _causal_sweep(i, tq, tk, (scores, log_fail, suffix_sum, weights, values), True, s_ref)
    o_ref[...] = acc_ref[...].T.astype(o_ref.dtype)


def _sb_attention(qk, vt, batch, seq):
    t = qk.shape[0]
    tq = _tile(seq, ATT_TQ)
    tk = vt.shape[2]
    nq = seq // tq
    pairs = SB_HEADS // 2
    u = jnp.asarray(np.triu(np.ones((tk, tk), np.float32)), BF16)
    return pl.pallas_call(
        functools.partial(_sb_kernel, tq=tq, tk=tk),
        grid=(batch, pairs, nq),
        in_specs=[pl.BlockSpec((tq, LANES), lambda b, p, i: (b * nq + i, p)),
                  pl.BlockSpec((seq, LANES), lambda b, p, i: (b, pairs + p)),
                  pl.BlockSpec((seq // tk, LANES, tk), lambda b, p, i: (b, p, 0)),
                  pl.BlockSpec((tk, tk), lambda b, p, i: (0, 0))],
        out_specs=pl.BlockSpec((tq, LANES), lambda b, p, i: (b * nq + i, p)),
        out_shape=jax.ShapeDtypeStruct((t, SB_HEADS * HEAD_DIM), BF16),
        scratch_shapes=[pltpu.VMEM((LANES, tq), F32), pltpu.VMEM((8, tq), F32),
                        pltpu.VMEM((2 * (tq // tk), tk, tq), F32)],
        compiler_params=_params("parallel", "parallel", "arbitrary"),
        name="sb_attention",
    )(qk, qk, vt, u)


MLA_QN = MLA_HEADS * MLA_NOPE_DIM
MLA_QR = MLA_HEADS * MLA_ROPE_DIM
MLA_OUT = 2 * MLA_QN + MLA_QR + LANES


def _mla_proj_kernel(x_ref, g_ref, wd_ref, qg_ref, wuq_ref, kvg_ref, wukv_ref, cs_ref, o_ref, vt_ref,
                     *, scale, tk):
    h = _rms(x_ref[...], g_ref[...]).astype(BF16)
    c = _dot(h, wd_ref[...])
    cos = cs_ref[:, :LANES]
    sin = cs_ref[:, LANES:]
    cq = _rms(c[:, :MLA_Q_RANK], qg_ref[...]).astype(BF16)
    q = _dot(cq, wuq_ref[...])
    o_ref[:, :MLA_QN] = (q[:, :MLA_QN] * scale).astype(o_ref.dtype)
    for m in range(MLA_QR // LANES):
        a = MLA_QN + m * LANES
        rot = q[:, a:a + LANES] * cos + q[:, a + MLA_QR:a + MLA_QR + LANES] * sin
        o_ref[:, a:a + LANES] = (rot * scale).astype(o_ref.dtype)
    kv0 = MLA_Q_RANK + MLA_KV_RANK
    ckv = _rms(c[:, MLA_Q_RANK:kv0], kvg_ref[...]).astype(BF16)
    kv = _dot(ckv, wukv_ref[...])
    kn0 = MLA_QN + MLA_QR
    o_ref[:, kn0:kn0 + MLA_QN] = kv[:, :MLA_QN].astype(o_ref.dtype)
    kr = c[:, kv0:kv0 + LANES] * cos + c[:, kv0 + LANES:kv0 + 2 * LANES] * sin
    o_ref[:, MLA_OUT - LANES:] = kr.astype(o_ref.dtype)
    _store_vt(vt_ref, kv[:, MLA_QN:], tk)


def _mla_rope_layout():
    half = MLA_ROPE_DIM // 2
    per = LANES // MLA_ROPE_DIM
    dq = MLA_NOPE_DIM + MLA_ROPE_DIM
    q_nope = np.concatenate([h * dq + np.arange(MLA_NOPE_DIM) for h in range(MLA_HEADS)])
    q_rope = np.zeros(MLA_QR, np.int64)
    q_swap = np.zeros(MLA_QR, np.int64)
    for h in range(MLA_HEADS):
        m, r = divmod(h, per)
        x1 = h * dq + MLA_NOPE_DIM + np.arange(half)
        x2 = x1 + half
        first = m * LANES + r * half + np.arange(half)
        second = first + LANES // 2
        q_rope[first], q_rope[second] = x1, x2
        q_swap[first], q_swap[second] = x2, x1
    k_rope = np.zeros(LANES, np.int64)
    k_swap = np.zeros(LANES, np.int64)
    base = MLA_Q_RANK + MLA_KV_RANK
    for r in range(per):
        first = r * half + np.arange(half)
        second = first + LANES // 2
        k_rope[first], k_rope[second] = base + np.arange(half), base + half + np.arange(half)
        k_swap[first], k_swap[second] = base + half + np.arange(half), base + np.arange(half)
    dkv = MLA_NOPE_DIM + HEAD_DIM
    k_nope = np.concatenate([h * dkv + np.arange(MLA_NOPE_DIM) for h in range(MLA_HEADS)])
    v = k_nope + MLA_NOPE_DIM
    down = np.concatenate([np.arange(base), k_rope, k_swap])
    return np.concatenate([q_nope, q_rope, q_swap]), np.concatenate([k_nope, v]), down


def _mla_rope_tables(seq):
    half = MLA_ROPE_DIM // 2
    inv = ROPE_THETA ** (-jnp.arange(0, MLA_ROPE_DIM, 2, dtype=F32) / MLA_ROPE_DIM)
    ang = jnp.arange(seq, dtype=F32)[:, None] * inv[None, :]
    cos, sin = jnp.cos(ang), jnp.sin(ang)
    reps = LANES // 2 // half
    cos_t = jnp.tile(cos, (1, 2 * reps))
    sin_t = jnp.concatenate([jnp.tile(-sin, (1, reps)), jnp.tile(sin, (1, reps))], axis=1)
    return jnp.concatenate([cos_t, sin_t], axis=1)


def _mla_proj(x, g, w_down, q_norm, w_uq, kv_norm, w_ukv, seq):
    t, d = x.shape
    tm = _tile(seq, 512)
    tk = _tile(seq, ATT_TK)
    ns = seq // tm
    uq_cols, ukv_cols, down_cols = _mla_rope_layout()
    wd = w_down[:, down_cols].astype(BF16)
    wuq = w_uq[:, uq_cols].astype(BF16)
    wukv = w_ukv[:, ukv_cols].astype(BF16)
    cs = _mla_rope_tables(seq)
    scale = (MLA_NOPE_DIM + MLA_ROPE_DIM) ** -0.5 * LOG2E
    full = lambda a: pl.BlockSpec(a.shape, lambda i: (0, 0))
    g2, qg, kvg = g.reshape(1, d), q_norm.reshape(1, -1), kv_norm.reshape(1, -1)
    return pl.pallas_call(
        functools.partial(_mla_proj_kernel, scale=scale, tk=tk),
        grid=(t // tm,),
        in_specs=[pl.BlockSpec((tm, d), lambda i: (i, 0)), full(g2), full(wd), full(qg), full(wuq),
                  full(kvg), full(wukv), pl.BlockSpec((tm, 2 * LANES), lambda i: (i % ns, 0))],
        out_specs=[pl.BlockSpec((tm, MLA_OUT), lambda i: (i, 0)),
                   pl.BlockSpec((tm // tk, MLA_QN, tk), lambda i: (i, 0, 0))],
        out_shape=[jax.ShapeDtypeStruct((t, MLA_OUT), BF16),
                   jax.ShapeDtypeStruct((t // tk, MLA_QN, tk), BF16)],
        compiler_params=_params("parallel"),
        name="mla_proj",
    )(x, g2, wd, qg, wuq, kvg, wukv, cs)


def _mla_kernel(qn_ref, qr_ref, kn_ref, kr_ref, vt_ref, o_ref, acc_ref, ml_ref, s_ref, *, tq, tk):
    p = pl.program_id(1)
    i = pl.program_id(2)
    q = jnp.concatenate([qn_ref[...], qr_ref[...]], axis=1)
    lane = lax.broadcasted_iota(jnp.int32, (tq, 2 * LANES), 1)
    half = MLA_ROPE_DIM // 2
    zero = jnp.zeros_like(q)
    qms = []
    for j in range(2):
        r0 = LANES + (2 * (p % 2) + j) * half
        sel = jnp.logical_and(lane >= j * HEAD_DIM, lane < (j + 1) * HEAD_DIM)
        sel = jnp.logical_or(sel, jnp.logical_and(lane >= r0, lane < r0 + half))
        r1 = r0 + LANES // 2
        sel = jnp.logical_or(sel, jnp.logical_and(lane >= r1, lane < r1 + half))
        qms.append(jnp.where(sel, q, zero))
    acc_ref[...] = jnp.zeros_like(acc_ref)
    row = lax.broadcasted_iota(jnp.int32, ml_ref.shape, 0)
    ml_ref[...] = jnp.where(row < 2, NEG, 0.0)

    def scores(unit, _):
        jb, h, lanes, _ = unit
        start = pl.multiple_of(jb * tk, tk)
        kt = jnp.concatenate([kn_ref[pl.ds(start, tk), :], kr_ref[pl.ds(start, tk), :]], axis=1)
        return _dot_nt(kt, qms[h][lanes])

    def softmax(unit, s):
        _, h, lanes, mask = unit
        if mask is not None:
            s = jnp.where(mask, s, NEG)
        m_old = ml_ref[h:h + 1, lanes]
        m_new = jnp.maximum(m_old, jnp.max(s, axis=0, keepdims=True))
        alpha = jnp.exp2(m_old - m_new)
        e = jnp.exp2(s - m_new)
        ml_ref[h:h + 1, lanes] = m_new
        ml_ref[2 + h:3 + h, lanes] = alpha * ml_ref[2 + h:3 + h, lanes] + jnp.sum(e, axis=0, keepdims=True)
        return alpha, e.astype(BF16)

    def values(unit, val):
        jb, h, lanes, _ = unit
        alpha, e = val
        rows = slice(h * HEAD_DIM, (h + 1) * HEAD_DIM)
        acc_ref[rows, lanes] = alpha * acc_ref[rows, lanes] + _dot(vt_ref[jb, rows, :], e)

    _causal_sweep(i, tq, tk, (scores, softmax, values), False, s_ref)
    head = lax.broadcasted_iota(jnp.int32, acc_ref.shape, 0) < HEAD_DIM
    denom = jnp.where(head, ml_ref[2:3, :], ml_ref[3:4, :])
    o_ref[...] = (acc_ref[...] / denom).T.astype(o_ref.dtype)


def _mla_attention(a, vt, batch, seq):
    t = a.shape[0]
    tq = _tile(seq, ATT_TQ)
    tk = vt.shape[2]
    nq = seq // tq
    pairs = MLA_HEADS // 2
    qr0 = MLA_QN // LANES
    kn0 = (MLA_QN + MLA_QR) // LANES
    kr0 = kn0 + pairs
    return pl.pallas_call(
        functools.partial(_mla_kernel, tq=tq, tk=tk),
        grid=(batch, pairs, nq),
        in_specs=[pl.BlockSpec((tq, LANES), lambda b, p, i: (b * nq + i, p)),
                  pl.BlockSpec((tq, LANES), lambda b, p, i: (b * nq + i, qr0 + p // 2)),
                  pl.BlockSpec((seq, LANES), lambda b, p, i: (b, kn0 + p)),
                  pl.BlockSpec((seq, LANES), lambda b, p, i: (b, kr0)),
                  pl.BlockSpec((seq // tk, LANES, tk), lambda b, p, i: (b, p, 0))],
        out_specs=pl.BlockSpec((tq, LANES), lambda b, p, i: (b * nq + i, p)),
        out_shape=jax.ShapeDtypeStruct((t, MLA_HEADS * HEAD_DIM), BF16),
        scratch_shapes=[pltpu.VMEM((LANES, tq), F32), pltpu.VMEM((8, tq), F32),
                        pltpu.VMEM((2 * (tq // tk), tk, tq), F32)],
        compiler_params=_params("parallel", "parallel", "arbitrary"),
        name="mla_attention",
    )(a, a, a, a, vt)


FFN_CHUNK = 256


def _swiglu_into(acc_ref, h, wg, wu, wd):
    def gate_up(cols, _):
        return _dot(h, wg(cols)), _dot(h, wu(cols))

    def activate(cols, gu):
        g, u = gu
        return (g * jax.nn.sigmoid(g) * u).astype(BF16)

    def down(cols, a):
        acc_ref[...] += _dot(a, wd(cols))

    return gate_up, activate, down


def _run_swiglu(acc_ref, h, wg, wu, wd, width):
    fc = FFN_CHUNK if width % FFN_CHUNK == 0 else width
    chunks = [slice(c, c + fc) for c in range(0, width, fc)]
    _staggered(chunks, _swiglu_into(acc_ref, h, wg, wu, wd))


def _ffn_kernel(x_ref, g_ref, wg_ref, wu_ref, wd_ref, o_ref, h_ref):
    f = pl.program_id(1)

    @pl.when(f == 0)
    def _():
        x = x_ref[...]
        h_ref[...] = _rms(x, g_ref[...]).astype(BF16)
        o_ref[...] = x

    _run_swiglu(o_ref, h_ref[...], lambda c: wg_ref[:, c], lambda c: wu_ref[:, c], lambda c: wd_ref[c, :],
                wg_ref.shape[1])


def _ffn(x, g, wg, wu, wd):
    t, d = x.shape
    fdim = wg.shape[1]
    tm = _tile(t, 512)
    tf = fdim
    return pl.pallas_call(
        _ffn_kernel,
        grid=(t // tm, fdim // tf),
        in_specs=[pl.BlockSpec((tm, d), lambda i, f: (i, 0)),
                  pl.BlockSpec((1, d), lambda i, f: (0, 0)),
                  pl.BlockSpec((d, tf), lambda i, f: (0, f)),
                  pl.BlockSpec((d, tf), lambda i, f: (0, f)),
                  pl.BlockSpec((tf, d), lambda i, f: (f, 0))],
        out_specs=pl.BlockSpec((tm, d), lambda i, f: (i, 0)),
        out_shape=jax.ShapeDtypeStruct((t, d), F32),
        scratch_shapes=[pltpu.VMEM((tm, d), BF16)],
        compiler_params=_params("parallel", "arbitrary"),
        name="dense_ffn",
    )(x, g.reshape(1, d), wg, wu, wd)


MOE_TILE = 512
META_ROWS = 8
ROUTE_SUB = 128
COMBINE_SUB = 256


def _pack_halves(x):
    half = x.shape[1] // 2
    bits = lax.bitcast_convert_type(x.astype(BF16).astype(F32), jnp.uint32)
    return bits[:, :half] | (bits[:, half:] >> 16)


def _unpack_halves(w):
    hi = lax.bitcast_convert_type(w & jnp.uint32(0xFFFF0000), F32)
    lo = lax.bitcast_convert_type(w << 16, F32)
    return jnp.concatenate([hi, lo], axis=1)


def _router_kernel(x_ref, g_ref, rt_ref, su_ref, eye_ref, hb_ref, meta_ref, metac_ref, cnt_ref, carry_ref):
    @pl.when(pl.program_id(0) == 0)
    def _():
        carry_ref[...] = jnp.zeros_like(carry_ref)

    h = _rms(x_ref[...], g_ref[...])
    hb_ref[...] = _pack_halves(h)
    logits = lax.dot_general(rt_ref[...], h, _NT, precision=lax.Precision.HIGHEST,
                             preferred_element_type=F32)
    ne, ct = logits.shape
    eio = lax.broadcasted_iota(jnp.int32, (ne, ct), 0).astype(F32)
    m1 = jnp.max(logits, axis=0, keepdims=True)
    i1 = jnp.min(jnp.where(logits == m1, eio, float(ne)), axis=0, keepdims=True)
    rest = jnp.where(eio == i1, -jnp.inf, logits)
    m2 = jnp.max(rest, axis=0, keepdims=True)
    i2 = jnp.min(jnp.where(rest == m2, eio, float(ne)), axis=0, keepdims=True)
    e2 = jnp.exp(m2 - m1)
    g1 = 1.0 / (1.0 + e2)
    g2 = e2 / (1.0 + e2)
    oh1 = (eio == i1).astype(F32)
    oh2 = (eio == i2).astype(F32)
    both = oh1 + oh2
    seen = carry_ref[:, 0:1] + _dot(both.astype(BF16), su_ref[...])
    r1 = jnp.sum(oh1 * seen, axis=0, keepdims=True)
    r2 = jnp.sum(oh2 * seen, axis=0, keepdims=True)
    meta = jnp.zeros((META_ROWS, ct), F32)
    for row, val in enumerate((i1, i2, r1, r2, g1, g2)):
        meta = jnp.where(eio == float(row), val, meta)
    meta_ref[...] = meta
    metac_ref[...] = lax.dot_general(eye_ref[...], meta, _NT, precision=lax.Precision.HIGHEST,
                                     preferred_element_type=F32)
    carry_ref[...] = carry_ref[...] + jnp.sum(both, axis=1, keepdims=True)
    cnt_ref[0] = carry_ref[...]


def _router(x, g, router):
    t, d = x.shape
    ct = _tile(t, MOE_TILE)
    nc = t // ct
    su = jnp.asarray(np.triu(np.ones((ct, ct), np.float32), 1), BF16)
    eye = jnp.eye(ct, dtype=F32)
    return pl.pallas_call(
        _router_kernel,
        grid=(nc,),
        in_specs=[pl.BlockSpec((ct, d), lambda i: (i, 0)),
                  pl.BlockSpec((1, d), lambda i: (0, 0)),
                  pl.BlockSpec((N_EXPERTS, d), lambda i: (0, 0)),
                  pl.BlockSpec((ct, ct), lambda i: (0, 0)),
                  pl.BlockSpec((ct, ct), lambda i: (0, 0))],
        out_specs=[pl.BlockSpec((ct, d // 2), lambda i: (i, 0)),
                   pl.BlockSpec((META_ROWS, ct), lambda i: (0, i)),
                   pl.BlockSpec((ct, META_ROWS), lambda i: (i, 0)),
                   pl.BlockSpec((1, N_EXPERTS, LANES), lambda i: (i, 0, 0))],
        out_shape=[jax.ShapeDtypeStruct((t, d // 2), jnp.uint32),
                   jax.ShapeDtypeStruct((META_ROWS, t), F32),
                   jax.ShapeDtypeStruct((t, META_ROWS), F32),
                   jax.ShapeDtypeStruct((nc, N_EXPERTS, LANES), F32)],
        scratch_shapes=[pltpu.VMEM((N_EXPERTS, LANES), F32)],
        compiler_params=_params("arbitrary"),
        name="moe_router",
    )(x, g.reshape(1, d), router.T, su, eye)


def _moe_plan(cnt_after, t, ct):
    nc = t // ct
    rows = 2 * t + N_EXPERTS * ct
    nrt = rows // ct
    counts = cnt_after[:, :, 0].astype(jnp.int32)
    total = counts[-1]
    padded = (total + ct - 1) // ct * ct
    off_end = jnp.cumsum(padded)
    off = off_end - padded
    before = jnp.concatenate([jnp.zeros((1, N_EXPERTS), jnp.int32), counts[:-1]], axis=0)
    chunk_start = (off[:, None] + before.T).reshape(-1)
    tile_start = jnp.arange(nrt, dtype=jnp.int32) * ct
    starts = jnp.sort(jnp.concatenate([chunk_start, tile_start]))
    ends = jnp.concatenate([starts[1:], jnp.array([rows], jnp.int32)])
    item_r = jnp.minimum(starts // ct, nrt - 1)
    count_le = lambda sorted_vals, x: jnp.sum((sorted_vals[None, :] <= x[:, None]).astype(jnp.int32), axis=1)
    item_c = (count_le(chunk_start, starts) - 1) % nc
    first = lambda key: jnp.concatenate([jnp.ones((1,), jnp.int32), (key[1:] != key[:-1]).astype(jnp.int32)])
    dispatch = (item_r, item_c, starts, ends, first(item_r))
    order = jnp.argsort(item_c * (rows + 1) + starts)
    c2 = item_c[order]
    combine = (c2, item_r[order], starts[order], ends[order], first(c2))
    tile_expert = jnp.minimum(count_le(off_end, tile_start), N_EXPERTS - 1)
    tile_valid = (tile_start < off_end[-1]).astype(jnp.int32)
    return rows, off.astype(jnp.int32), dispatch, combine, tile_expert.astype(jnp.int32), tile_valid


def _positions(expert, rank, off_ref):
    base = jnp.zeros_like(rank)
    for e in range(N_EXPERTS):
        base = jnp.where(expert == float(e), off_ref[e].astype(F32), base)
    return base + rank


def _dispatch_kernel(r_ref, c_ref, s_ref, e_ref, first_ref, off_ref, meta_ref, h_ref, xs_ref, gate_ref):
    n = pl.program_id(0)
    rt, ct = xs_ref.shape[0], h_ref.shape[0]

    @pl.when(first_ref[n] == 1)
    def _():
        xs_ref[...] = jnp.zeros_like(xs_ref)
        gate_ref[...] = jnp.zeros_like(gate_ref)

    meta = meta_ref[...]
    pos0 = _positions(meta[0:1], meta[2:3], off_ref)
    pos1 = _positions(meta[1:2], meta[3:4], off_ref)
    sub = min(rt, ROUTE_SUB)
    for sb in range(rt // sub):
        base = r_ref[n] * rt + sb * sub

        @pl.when(jnp.logical_and(base < e_ref[n], base + sub > s_ref[n]))
        def _():
            row = lax.broadcasted_iota(jnp.int32, (sub, ct), 0) + base
            live = jnp.logical_and(row >= s_ref[n], row < e_ref[n])
            rowf = row.astype(F32)
            sel0 = jnp.logical_and(pos0 == rowf, live)
            sel1 = jnp.logical_and(pos1 == rowf, live)
            sel = jnp.logical_or(sel0, sel1).astype(F32).astype(BF16)
            rows = slice(sb * sub, (sb + 1) * sub)
            xs_ref[rows, :] += _pack_halves(_dot(sel, _unpack_halves(h_ref[...]).astype(BF16)))
            gate = jnp.where(sel0, meta[4:5], 0.0) + jnp.where(sel1, meta[5:6], 0.0)
            gate_ref[rows, :] += jnp.sum(gate, axis=1, keepdims=True)


def _dispatch(plan, off, meta, hb, rows, ct):
    t, d = hb.shape
    n_items = plan[0].shape[0]
    return pl.pallas_call(
        _dispatch_kernel,
        grid_spec=pltpu.PrefetchScalarGridSpec(
            num_scalar_prefetch=6,
            grid=(n_items,),
            in_specs=[pl.BlockSpec((META_ROWS, ct), lambda n, r, c, *_: (0, c[n])),
                      pl.BlockSpec((ct, d), lambda n, r, c, *_: (c[n], 0))],
            out_specs=[pl.BlockSpec((ct, d), lambda n, r, c, *_: (r[n], 0)),
                       pl.BlockSpec((ct, 1), lambda n, r, c, *_: (r[n], 0))]),
        out_shape=[jax.ShapeDtypeStruct((rows, d), jnp.uint32),
                   jax.ShapeDtypeStruct((rows, 1), F32)],
        compiler_params=_params("arbitrary"),
        name="moe_dispatch",
    )(*plan, off, meta, hb)


def _expert_kernel(te_ref, tv_ref, xs_ref, gate_ref, wg_ref, wu_ref, wd_ref, y_ref, acc_ref):
    r = pl.program_id(0)
    f = pl.program_id(1)

    @pl.when(f == 0)
    def _():
        acc_ref[...] = jnp.zeros_like(acc_ref)

    @pl.when(tv_ref[r] == 1)
    def _():
        _run_swiglu(acc_ref, _unpack_halves(xs_ref[...]).astype(BF16), lambda c: wg_ref[0, :, c],
                    lambda c: wu_ref[0, :, c], lambda c: wd_ref[0, c, :], wg_ref.shape[2])

    @pl.when(f == pl.num_programs(1) - 1)
    def _():
        y_ref[...] = _pack_halves(acc_ref[...] * gate_ref[...])


def _experts(tile_expert, tile_valid, xs, gate, wg, wu, wd, ct):
    rows, dp = xs.shape
    d = 2 * dp
    fdim = wg.shape[2]
    tf = _tile(fdim, 1792)
    nf = fdim // tf

    def wcol(r, f, te, tv):
        return (te[r], 0, jnp.where(tv[r] == 1, f, nf - 1))

    def wrow(r, f, te, tv):
        return (te[r], jnp.where(tv[r] == 1, f, nf - 1), 0)

    return pl.pallas_call(
        _expert_kernel,
        grid_spec=pltpu.PrefetchScalarGridSpec(
            num_scalar_prefetch=2,
            grid=(rows // ct, nf),
            in_specs=[pl.BlockSpec((ct, dp), lambda r, f, te, tv: (r, 0)),
                      pl.BlockSpec((ct, 1), lambda r, f, te, tv: (r, 0)),
                      pl.BlockSpec((1, d, tf), wcol),
                      pl.BlockSpec((1, d, tf), wcol),
                      pl.BlockSpec((1, tf, d), wrow)],
            out_specs=pl.BlockSpec((ct, dp), lambda r, f, te, tv: (r, 0)),
            scratch_shapes=[pltpu.VMEM((ct, d), F32)]),
        out_shape=jax.ShapeDtypeStruct((rows, dp), jnp.uint32),
        compiler_params=_params("arbitrary", "arbitrary"),
        name="moe_experts",
    )(tile_expert, tile_valid, xs, gate, wg, wu, wd)


def _combine_kernel(c_ref, r_ref, s_ref, e_ref, first_ref, off_ref, x_ref, metac_ref, y_ref, o_ref):
    n = pl.program_id(0)
    ct, rt = x_ref.shape[0], y_ref.shape[0]

    @pl.when(first_ref[n] == 1)
    def _():
        o_ref[...] = x_ref[...]

    metac = metac_ref[...]
    pos0 = _positions(metac[:, 0:1], metac[:, 2:3], off_ref)
    pos1 = _positions(metac[:, 1:2], metac[:, 3:4], off_ref)
    sub = min(rt, COMBINE_SUB)
    for sb in range(rt // sub):
        base = r_ref[n] * rt + sb * sub

        @pl.when(jnp.logical_and(base < e_ref[n], base + sub > s_ref[n]))
        def _():
            row = lax.broadcasted_iota(jnp.int32, (ct, sub), 1) + base
            live = jnp.logical_and(row >= s_ref[n], row < e_ref[n])
            rowf = row.astype(F32)
            sel = jnp.logical_and(jnp.logical_or(pos0 == rowf, pos1 == rowf), live).astype(F32).astype(BF16)
            o_ref[...] += _dot(sel, _unpack_halves(y_ref[sb * sub:(sb + 1) * sub, :]).astype(BF16))


def _combine(plan, off, x, metac, y, ct):
    t, d = x.shape
    n_items = plan[0].shape[0]
    return pl.pallas_call(
        _combine_kernel,
        grid_spec=pltpu.PrefetchScalarGridSpec(
            num_scalar_prefetch=6,
            grid=(n_items,),
            in_specs=[pl.BlockSpec((ct, d), lambda n, c, r, *_: (c[n], 0)),
                      pl.BlockSpec((ct, META_ROWS), lambda n, c, r, *_: (c[n], 0)),
                      pl.BlockSpec((ct, d // 2), lambda n, c, r, *_: (r[n], 0))],
            out_specs=pl.BlockSpec((ct, d), lambda n, c, r, *_: (c[n], 0))),
        out_shape=jax.ShapeDtypeStruct((t, d), F32),
        compiler_params=_params("arbitrary"),
        name="moe_combine",
    )(*plan, off, x, metac, y)


def _cast_kernel(w_ref, o_ref):
    o_ref[...] = w_ref[...].astype(o_ref.dtype)


CAST_BLOCK_BYTES = 8 * 1024 * 1024


def _layer_bf16(w, layer):
    shape = w.shape[1:]
    cols = shape[-1]
    rows = math.prod(shape[:-1])
    tr = rows
    while tr * cols * 4 > CAST_BLOCK_BYTES and tr % 16 == 0:
        tr //= 2
    out = pl.pallas_call(
        _cast_kernel,
        grid=(rows // tr,),
        in_specs=[pl.BlockSpec((None, tr, cols), lambda i: (layer, i, 0))],
        out_specs=pl.BlockSpec((tr, cols), lambda i: (i, 0)),
        out_shape=jax.ShapeDtypeStruct((rows, cols), BF16),
        compiler_params=_params("parallel"),
        name="weights_bf16",
    )(w.reshape(w.shape[0], rows, cols))
    return out.reshape(shape)


SC_CORES = 2
SC_SUBCORES = 16
SC_CHUNK = 128


SC_LANES = 16


def _sc_worker_chunks(total):
    per = total // (SC_CORES * SC_SUBCORES)
    wid = lax.axis_index("s") * SC_CORES + lax.axis_index("c")
    return wid * per, per // SC_CHUNK


def _sc_mesh():
    return plsc.VectorSubcoreMesh(core_axis_name="c", subcore_axis_name="s")


def _sc_gather(table, idx, clamp=False):
    v, d = table.shape
    b = idx.shape[0]

    def body(table_hbm, idx_hbm, out_hbm, idx_v, rows_v, sem):
        base, chunks = _sc_worker_chunks(b)

        @pl.loop(0, chunks)
        def _(j):
            start = base + j * SC_CHUNK
            pltpu.sync_copy(idx_hbm.at[pl.ds(start, SC_CHUNK)], idx_v)
            if clamp:
                for c in range(SC_CHUNK // SC_LANES):
                    lanes = pl.ds(c * SC_LANES, SC_LANES)
                    idx_v[lanes] = jnp.minimum(jnp.maximum(idx_v[lanes], 0), v - 1)
            pltpu.async_copy(table_hbm.at[idx_v], rows_v, sem).wait()
            pltpu.sync_copy(rows_v, out_hbm.at[pl.ds(start, SC_CHUNK)])

    return pl.kernel(
        body,
        out_type=jax.ShapeDtypeStruct((b, d), table.dtype),
        mesh=_sc_mesh(),
        scratch_types=[pltpu.VMEM((SC_CHUNK,), jnp.int32),
                       pltpu.VMEM((SC_CHUNK, d), table.dtype),
                       pltpu.SemaphoreType.DMA],
        name="sc_row_gather",
    )(table, idx)


def _sc_scatter(vals, idx, n):
    b = idx.shape[0]

    def body(vals_hbm, idx_hbm, out_hbm, idx_v, vals_v):
        base, chunks = _sc_worker_chunks(b)

        @pl.loop(0, chunks)
        def _(j):
            start = base + j * SC_CHUNK
            pltpu.sync_copy(idx_hbm.at[pl.ds(start, SC_CHUNK)], idx_v)
            pltpu.sync_copy(vals_hbm.at[pl.ds(start, SC_CHUNK)], vals_v)
            pltpu.sync_copy(vals_v, out_hbm.at[idx_v])

    return pl.kernel(
        body,
        out_type=jax.ShapeDtypeStruct((n,), vals.dtype),
        mesh=_sc_mesh(),
        scratch_types=[pltpu.VMEM((SC_CHUNK,), jnp.int32), pltpu.VMEM((SC_CHUNK,), vals.dtype)],
        name="sc_scatter",
    )(vals, idx)


def _weighted_add_kernel(x_ref, metac_ref, y0_ref, y1_ref, o_ref):
    gates = metac_ref[...]
    o_ref[...] = (x_ref[...] + gates[:, 4:5] * _unpack_halves(y0_ref[...])
                  + gates[:, 5:6] * _unpack_halves(y1_ref[...]))


def _weighted_add(x, metac, ysel):
    t, d = x.shape
    tm = _tile(t, 512)
    nt = t // tm
    return pl.pallas_call(
        _weighted_add_kernel,
        grid=(nt,),
        in_specs=[pl.BlockSpec((tm, d), lambda i: (i, 0)),
                  pl.BlockSpec((tm, META_ROWS), lambda i: (i, 0)),
                  pl.BlockSpec((tm, d // 2), lambda i: (i, 0)),
                  pl.BlockSpec((tm, d // 2), lambda i: (nt + i, 0))],
        out_specs=pl.BlockSpec((tm, d), lambda i: (i, 0)),
        out_shape=jax.ShapeDtypeStruct((t, d), F32),
        compiler_params=_params("parallel"),
        name="moe_weighted_add",
    )(x, metac, ysel, ysel)


def _moe(x, g, router, wg, wu, wd):
    t = x.shape[0]
    ct = _tile(t, MOE_TILE)
    hb, meta, metac, cnt = _router(x, g, router)
    rows, off, dispatch, combine, tile_expert, tile_valid = _moe_plan(cnt, t, ct)
    if rows % (SC_CORES * SC_SUBCORES * SC_CHUNK) == 0 and (2 * t) % (SC_CORES * SC_SUBCORES * SC_CHUNK) == 0:
        base = sum(jnp.where(meta[0:2] == float(e), off[e], 0) for e in range(N_EXPERTS))
        pos = (base + meta[2:4].astype(jnp.int32)).reshape(-1)
        token = jnp.tile(jnp.arange(t, dtype=jnp.int32), 2)
        row_token = _sc_scatter(token, pos, rows)
        xs = _sc_gather(hb, row_token, clamp=True)
        y = _experts(tile_expert, tile_valid, xs, jnp.ones((rows, 1), F32), wg, wu, wd, ct)
        return _weighted_add(x, metac, _sc_gather(y, pos))
    xs, gate = _dispatch(dispatch, off, meta, hb, rows, ct)
    y = _experts(tile_expert, tile_valid, xs, gate, wg, wu, wd, ct)
    return _combine(combine, off, x, metac, y, ct)


def _final_norm_kernel(x_ref, g_ref, o_ref):
    o_ref[...] = _rms(x_ref[...], g_ref[...])


def _final_norm(x, g):
    t, d = x.shape
    tm = _tile(t, 1024)
    return pl.pallas_call(
        _final_norm_kernel,
        grid=(t // tm,),
        in_specs=[pl.BlockSpec((tm, d), lambda i: (i, 0)), pl.BlockSpec((1, d), lambda i: (0, 0))],
        out_specs=pl.BlockSpec((tm, d), lambda i: (i, 0)),
        out_shape=jax.ShapeDtypeStruct((t, d), F32),
        compiler_params=_params("parallel"),
        name="final_norm",
    )(x, g.reshape(1, d))


def kernel(x, rel_bias, attn_norm, ffn_norm, final_norm, swa_w_qkv, swa_b_qkv, swa_sinks, swa_w_o, sb_w_qkv, sb_w_o, mla_w_down, mla_q_norm, mla_w_uq, mla_kv_norm, mla_w_ukv, mla_w_o, ffn_w_gate, ffn_w_up, ffn_w_down, moe_router, moe_w_gate, moe_w_up, moe_w_down):
    batch, seq, d = x.shape
    depth = attn_norm.shape[0]
    xt = x.reshape(batch * seq, d)
    swa_bias = _swa_bias(rel_bias)
    for i in range(depth):
        mixer, j = i % N_MIXERS, i // N_MIXERS
        if mixer == 0:
            w, b, cs = _swa_weights(swa_w_qkv[j], swa_b_qkv[j])
            qkv = _norm_proj(xt, attn_norm[i], w, b, cs, "swa_proj")
            o = _swa_attention(qkv, swa_bias, swa_sinks[j], batch, seq)
            xt = _out_proj(xt, o, swa_w_o[j].astype(BF16), "swa_out")
        elif mixer == 1:
            n = sb_w_qkv.shape[2] // 3
            cs = jnp.concatenate([jnp.full((n,), HEAD_DIM ** -0.5 * LOG2E, F32), jnp.ones((n,), F32)])
            qk, vt = _norm_proj_vt(xt, attn_norm[i], sb_w_qkv[j].astype(BF16), cs, n, seq, "sb_proj")
            o = _sb_attention(qk, vt, batch, seq)
            xt = _out_proj(xt, o, sb_w_o[j].astype(BF16), "sb_out")
        else:
            a, vt = _mla_proj(xt, attn_norm[i], mla_w_down[j], mla_q_norm[j], mla_w_uq[j],
                              mla_kv_norm[j], mla_w_ukv[j], seq)
            o = _mla_attention(a, vt, batch, seq)
            xt = _out_proj(xt, o, mla_w_o[j].astype(BF16), "mla_out")
        f = i // 2
        if i % 2 == 0:
            xt = _ffn(xt, ffn_norm[i], _layer_bf16(ffn_w_gate, f), _layer_bf16(ffn_w_up, f),
                      _layer_bf16(ffn_w_down, f))
        else:
            xt = _moe(xt, ffn_norm[i], moe_router[f], _layer_bf16(moe_w_gate, f), _layer_bf16(moe_w_up, f),
                      _layer_bf16(moe_w_down, f))
    return _final_norm(xt, final_norm).reshape(batch, seq, d)
```

```python
import functools
import math

import numpy as np
import jax
import jax.numpy as jnp
from jax import lax
from jax.experimental import pallas as pl
from jax.experimental.pallas import tpu as pltpu
from jax.experimental.pallas import tpu_sc as plsc

F32 = jnp.float32
BF16 = jnp.bfloat16

N_MIXERS = 3
RMS_EPS = 1e-6
SWA_HEADS = 16
SWA_KV_HEADS = 2
HEAD_DIM = 64
SWA_BLOCK = 128
SWA_WINDOW = 128
REL_BUCKETS = 32
REL_MAX_DIST = 128
SB_HEADS = 16
MLA_HEADS = 16
MLA_NOPE_DIM = 64
MLA_ROPE_DIM = 32
MLA_Q_RANK = 384
MLA_KV_RANK = 256
ROPE_THETA = 10000.0
N_EXPERTS = 8

LANES = 128
NEG = -1e30
LOG2E = math.log2(math.e)
VMEM_LIMIT = 56 * 1024 * 1024

_NT = (((1,), (1,)), ((), ()))


def _dot(a, b):
    return jnp.dot(a, b, preferred_element_type=F32)


def _dot_nt(a, b):
    return lax.dot_general(a, b, _NT, preferred_element_type=F32)


def _rms(x, g):
    return x * lax.rsqrt(jnp.mean(x * x, axis=-1, keepdims=True) + RMS_EPS) * g


def _params(*sem):
    return pltpu.CompilerParams(dimension_semantics=sem, vmem_limit_bytes=VMEM_LIMIT)


def _tile(n, pref):
    return pref if n % pref == 0 else n


def _norm_proj_kernel(x_ref, g_ref, w_ref, b_ref, cs_ref, o_ref):
    h = _rms(x_ref[...], g_ref[...]).astype(BF16)
    y = (_dot(h, w_ref[...]) + b_ref[...]) * cs_ref[...]
    o_ref[...] = y.astype(o_ref.dtype)


def _norm_proj(x, g, w, b, colscale, name):
    t, d = x.shape
    n = w.shape[1]
    tm = _tile(t, 512)
    return pl.pallas_call(
        _norm_proj_kernel,
        grid=(t // tm,),
        in_specs=[pl.BlockSpec((tm, d), lambda i: (i, 0)),
                  pl.BlockSpec((1, d), lambda i: (0, 0)),
                  pl.BlockSpec((d, n), lambda i: (0, 0)),
                  pl.BlockSpec((1, n), lambda i: (0, 0)),
                  pl.BlockSpec((1, n), lambda i: (0, 0))],
        out_specs=pl.BlockSpec((tm, n), lambda i: (i, 0)),
        out_shape=jax.ShapeDtypeStruct((t, n), BF16),
        compiler_params=_params("parallel"),
        name=name,
    )(x, g.reshape(1, d), w, b.reshape(1, n), colscale.reshape(1, n))


ATT_TQ = 512
ATT_TK = 256


def _store_vt(vt_ref, v, tk):
    for s in range(v.shape[0] // tk):
        vt_ref[s] = v[s * tk:(s + 1) * tk, :].T.astype(vt_ref.dtype)


def _norm_proj_vt_kernel(x_ref, g_ref, w_ref, cs_ref, o_ref, vt_ref, *, tk):
    h = _rms(x_ref[...], g_ref[...]).astype(BF16)
    y = _dot(h, w_ref[...])
    n = o_ref.shape[1]
    o_ref[...] = (y[:, :n] * cs_ref[...]).astype(o_ref.dtype)
    _store_vt(vt_ref, y[:, n:], tk)


def _norm_proj_vt(x, g, w, colscale, nv, seq, name):
    t, d = x.shape
    n = w.shape[1] - nv
    tm = _tile(seq, 512)
    tk = _tile(seq, ATT_TK)
    per = tm // tk
    return pl.pallas_call(
        functools.partial(_norm_proj_vt_kernel, tk=tk),
        grid=(t // tm,),
        in_specs=[pl.BlockSpec((tm, d), lambda i: (i, 0)),
                  pl.BlockSpec((1, d), lambda i: (0, 0)),
                  pl.BlockSpec((d, n + nv), lambda i: (0, 0)),
                  pl.BlockSpec((1, n), lambda i: (0, 0))],
        out_specs=[pl.BlockSpec((tm, n), lambda i: (i, 0)),
                   pl.BlockSpec((per, nv, tk), lambda i: (i, 0, 0))],
        out_shape=[jax.ShapeDtypeStruct((t, n), BF16),
                   jax.ShapeDtypeStruct((t // tk, nv, tk), BF16)],
        compiler_params=_params("parallel"),
        name=name,
    )(x, g.reshape(1, d), w, colscale.reshape(1, n))


def _out_proj_kernel(x_ref, o_ref, w_ref, y_ref):
    y_ref[...] = x_ref[...] + _dot(o_ref[...], w_ref[...])


def _out_proj(x, o, w, name):
    t, d = x.shape
    k = o.shape[1]
    tm = _tile(t, 512)
    return pl.pallas_call(
        _out_proj_kernel,
        grid=(t // tm,),
        in_specs=[pl.BlockSpec((tm, d), lambda i: (i, 0)),
                  pl.BlockSpec((tm, k), lambda i: (i, 0)),
                  pl.BlockSpec((k, d), lambda i: (0, 0))],
        out_specs=pl.BlockSpec((tm, d), lambda i: (i, 0)),
        out_shape=jax.ShapeDtypeStruct((t, d), F32),
        compiler_params=_params("parallel"),
        name=name,
    )(x, o, w)


def _t5_bucket_table():
    qi = np.arange(SWA_BLOCK)[:, None]
    kj = np.arange(2 * SWA_BLOCK)[None, :]
    dist = qi + SWA_BLOCK - kj
    d0 = np.maximum(dist, 0)
    max_exact = REL_BUCKETS // 2
    d = np.maximum(d0, 1).astype(np.float32)
    large = max_exact + (np.log(d / max_exact) / math.log(REL_MAX_DIST / max_exact)
                         * (REL_BUCKETS - max_exact)).astype(np.int32)
    large = np.minimum(large, REL_BUCKETS - 1)
    bucket = np.where(d0 < max_exact, d0, large)
    band = (dist >= 0) & (dist < SWA_WINDOW)
    return np.where(band, bucket, -1).astype(np.int32)


def _swa_bias_kernel(rel_ref, bucket_ref, o_ref):
    h = pl.program_id(0)
    bucket = bucket_ref[...]
    acc = jnp.full(bucket.shape, NEG, F32)
    for b in range(REL_BUCKETS):
        acc = jnp.where(bucket == b, rel_ref[b, h], acc)
    o_ref[0] = acc


def _swa_bias(rel_bias):
    bucket = jnp.asarray(_t5_bucket_table())
    q, k = bucket.shape
    return pl.pallas_call(
        _swa_bias_kernel,
        grid=(SWA_HEADS,),
        in_specs=[pl.BlockSpec(memory_space=pltpu.SMEM),
                  pl.BlockSpec((q, k), lambda h: (0, 0))],
        out_specs=pl.BlockSpec((1, q, k), lambda h: (h, 0, 0)),
        out_shape=jax.ShapeDtypeStruct((SWA_HEADS, q, k), F32),
        compiler_params=_params("arbitrary"),
        name="swa_bias",
    )(rel_bias, bucket)


def _swa_kernel(sink_ref, q_ref, kc_ref, kp_ref, vc_ref, vp_ref, bias_ref, o_ref, *, tq):
    i = pl.program_id(1)
    g = pl.program_id(2)
    blk = SWA_BLOCK
    heads = bias_ref.shape[0]
    lo_half = lax.broadcasted_iota(jnp.int32, (blk, LANES), 1) < HEAD_DIM
    in_prev = lax.broadcasted_iota(jnp.int32, (heads * blk, 2 * blk), 1) < blk
    bias = bias_ref[...].reshape(heads * blk, 2 * blk)

    def window(cur_ref, prev_ref, sb):
        if sb == 0:
            return jnp.concatenate([prev_ref[...], cur_ref[0:blk, :]], axis=0)
        return cur_ref[(sb - 1) * blk:(sb + 1) * blk, :]

    def scores(sb, _):
        q = q_ref[sb * blk:(sb + 1) * blk, :]
        parts = []
        for h in range(heads):
            qp = q[:, (h // 2) * LANES:(h // 2 + 1) * LANES]
            keep = lo_half if h % 2 == 0 else jnp.logical_not(lo_half)
            parts.append(jnp.where(keep, qp, jnp.zeros_like(qp)))
        s = _dot_nt(jnp.concatenate(parts, axis=0), window(kc_ref, kp_ref, sb)) + bias
        if sb == 0:
            s = jnp.where(jnp.logical_and(in_prev, i == 0), NEG, s)
        return s

    def softmax(sb, s):
        ps, inv = [], []
        for h in range(heads):
            sh = s[h * blk:(h + 1) * blk]
            sink = sink_ref[g * heads + h]
            m = jnp.maximum(jnp.max(sh, axis=1, keepdims=True), sink)
            e = jnp.exp(sh - m)
            inv.append(1.0 / (jnp.sum(e, axis=1, keepdims=True) + jnp.exp(sink - m)))
            ps.append(e.astype(BF16))
        return jnp.concatenate(ps, axis=0), inv

    def values(sb, val):
        p, inv = val
        o = _dot(p, window(vc_ref, vp_ref, sb))
        outs = [jnp.where(lo_half, o[h * blk:(h + 1) * blk] * inv[h], o[(h + 1) * blk:(h + 2) * blk] * inv[h + 1])
                for h in range(0, heads, 2)]
        o_ref[sb * blk:(sb + 1) * blk, :] = jnp.concatenate(outs, axis=1).astype(o_ref.dtype)

    _staggered(list(range(tq // blk)), (scores, softmax, values))


def _swa_attention(qkv, bias, sinks, batch, seq):
    t = qkv.shape[0]
    tq = _tile(seq, 512)
    nt = seq // tq
    group = SWA_HEADS // SWA_KV_HEADS
    gw = group * HEAD_DIM
    kcol = SWA_HEADS * HEAD_DIM // LANES
    vcol = kcol + SWA_KV_HEADS
    per = tq // SWA_BLOCK

    def prev_idx(b, i, g):
        return jnp.maximum(b * (seq // SWA_BLOCK) + i * per - 1, 0)

    return pl.pallas_call(
        functools.partial(_swa_kernel, tq=tq),
        grid=(batch, nt, SWA_KV_HEADS),
        in_specs=[pl.BlockSpec(memory_space=pltpu.SMEM),
                  pl.BlockSpec((tq, gw), lambda b, i, g: (b * nt + i, g)),
                  pl.BlockSpec((tq, LANES), lambda b, i, g: (b * nt + i, kcol + g)),
                  pl.BlockSpec((SWA_BLOCK, LANES), lambda b, i, g: (prev_idx(b, i, g), kcol + g)),
                  pl.BlockSpec((tq, LANES), lambda b, i, g: (b * nt + i, vcol + g)),
                  pl.BlockSpec((SWA_BLOCK, LANES), lambda b, i, g: (prev_idx(b, i, g), vcol + g)),
                  pl.BlockSpec((group, SWA_BLOCK, 2 * SWA_BLOCK), lambda b, i, g: (g, 0, 0))],
        out_specs=pl.BlockSpec((tq, gw), lambda b, i, g: (b * nt + i, g)),
        out_shape=jax.ShapeDtypeStruct((t, SWA_HEADS * HEAD_DIM), BF16),
        compiler_params=_params("parallel", "parallel", "arbitrary"),
        name="swa_attention",
    )(sinks, qkv, qkv, qkv, qkv, qkv, bias)


def _swa_weights(w_qkv, b_qkv):
    nq = SWA_HEADS * HEAD_DIM
    kv = SWA_KV_HEADS * HEAD_DIM
    dup = np.concatenate([np.tile(np.arange(HEAD_DIM), 2) + h * HEAD_DIM for h in range(SWA_KV_HEADS)])
    cols = np.concatenate([np.arange(nq), nq + dup, nq + kv + dup])
    scale = np.concatenate([np.full(nq, HEAD_DIM ** -0.5), np.ones(2 * dup.size)]).astype(np.float32)
    return w_qkv[:, cols].astype(BF16), b_qkv[cols], jnp.asarray(scale)


def _staggered(units, stages):
    vals = list(units)
    for t in range(len(units) + len(stages) - 1):
        for k, stage in enumerate(stages):
            u = t - k
            if 0 <= u < len(units):
                vals[u] = stage(units[u], vals[u])


def _score_ahead(units, next_units, stages, s_ref):
    rest = stages[1:]
    vals = [None] * len(units)
    for t in range(len(units) + len(rest) - 1):
        ahead = stages[0](next_units[t], None) if t < len(units) else None
        for k, stage in enumerate(rest):
            u = t - k
            if 0 <= u < len(units):
                if k == 0:
                    lanes = units[u][2]
                    vals[u] = stage(units[u], s_ref[u, :, :lanes.stop - lanes.start])
                    s_ref[u] = ahead
                else:
                    vals[u] = stage(units[u], vals[u])


def _causal_sweep(i, tq, tk, stages, strict, s_ref):
    nsub = tq // tk

    def group(j):
        first = jnp.maximum(j, 0) * nsub
        return [(first + sb, h, slice(0, tq), None) for sb in reversed(range(nsub)) for h in range(2)]

    diagonal = []
    for sb in reversed(range(nsub)):
        nl = tq - sb * tk
        key = lax.broadcasted_iota(jnp.int32, (tk, nl), 0)
        qry = lax.broadcasted_iota(jnp.int32, (tk, nl), 1)
        mask = key < qry if strict else key <= qry
        diagonal += [(i * nsub + sb, h, slice(sb * tk, tq), mask) for h in range(2)]
    for u, unit in enumerate(diagonal):
        s_ref[u, :, :tq - unit[2].start] = stages[0](unit, None)
    _score_ahead(diagonal, group(i - 1), stages, s_ref)

    def body(n, carry):
        _score_ahead(group(i - 1 - n), group(i - 2 - n), stages, s_ref)
        return carry

    lax.fori_loop(0, i, body, 0)


def _split_pair(q):
    lo_half = lax.broadcasted_iota(jnp.int32, q.shape, 1) < HEAD_DIM
    zero = jnp.zeros_like(q)
    return jnp.where(lo_half, q, zero), jnp.where(lo_half, zero, q)


def _sb_kernel(q_ref, k_ref, vt_ref, u_ref, o_ref, acc_ref, c_ref, s_ref, *, tq, tk):
    i = pl.program_id(2)
    qms = _split_pair(q_ref[...])
    u = u_ref[...]
    acc_ref[...] = jnp.zeros_like(acc_ref)
    c_ref[...] = jnp.zeros_like(c_ref)

    def scores(unit, _):
        jb, h, lanes, _ = unit
        kt = k_ref[pl.ds(pl.multiple_of(jb * tk, tk), tk), :]
        return _dot_nt(kt, qms[h][lanes])

    def log_fail(unit, z):
        neg_abs = lax.bitcast_convert_type(lax.bitcast_convert_type(z, jnp.uint32) | jnp.uint32(1 << 31), F32)
        sp = jnp.maximum(z, 0.0) + jnp.log2(1.0 + jnp.exp2(neg_abs))
        if unit[3] is not None:
            sp = jnp.where(unit[3], sp, 0.0)
        return z, sp.astype(BF16)

    def suffix_sum(unit, val):
        z, sp = val
        return z, _dot(u, sp)

    def weights(unit, val):
        _, h, lanes, mask = unit
        z, r = val
        c = c_ref[h:h + 1, lanes]
        la = z - r - c
        if mask is not None:
            la = jnp.where(mask, la, NEG)
        c_ref[h:h + 1, lanes] = c + r[0:1, :]
        return jnp.exp2(la).astype(BF16)

    def values(unit, a):
        jb, h, lanes, _ = unit
        rows = slice(h * HEAD_DIM, (h + 1) * HEAD_DIM)
        acc_ref[rows, lanes] += _dot(vt_ref[jb, rows, :], a)

    _causal_sweep(i, tq, tk, (scores, log_fail, suffix_sum, weights, values), True, s_ref)
    o_ref[...] = acc_ref[...].T.astype(o_ref.dtype)


def _sb_attention(qk, vt, batch, seq):
    t = qk.shape[0]
    tq = _tile(seq, ATT_TQ)
    tk = vt.shape[2]
    nq = seq // tq
    pairs = SB_HEADS // 2
    u = jnp.asarray(np.triu(np.ones((tk, tk), np.float32)), BF16)
    return pl.pallas_call(
        functools.partial(_sb_kernel, tq=tq, tk=tk),
        grid=(batch, pairs, nq),
        in_specs=[pl.BlockSpec((tq, LANES), lambda b, p, i: (b * nq + i, p)),
                  pl.BlockSpec((seq, LANES), lambda b, p, i: (b, pairs + p)),
                  pl.BlockSpec((seq // tk, LANES, tk), lambda b, p, i: (b, p, 0)),
                  pl.BlockSpec((tk, tk), lambda b, p, i: (0, 0))],
        out_specs=pl.BlockSpec((tq, LANES), lambda b, p, i: (b * nq + i, p)),
        out_shape=jax.ShapeDtypeStruct((t, SB_HEADS * HEAD_DIM), BF16),
        scratch_shapes=[pltpu.VMEM((LANES, tq), F32), pltpu.VMEM((8, tq), F32),
                        pltpu.VMEM((2 * (tq // tk), tk, tq), F32)],
        compiler_params=_params("parallel", "parallel", "arbitrary"),
        name="sb_attention",
    )(qk, qk, vt, u)


MLA_QN = MLA_HEADS * MLA_NOPE_DIM
MLA_QR = MLA_HEADS * MLA_ROPE_DIM
MLA_OUT = 2 * MLA_QN + MLA_QR + LANES


def _mla_proj_kernel(x_ref, g_ref, wd_ref, qg_ref, wuq_ref, kvg_ref, wukv_ref, cs_ref, o_ref, vt_ref,
                     *, scale, tk):
    h = _rms(x_ref[...], g_ref[...]).astype(BF16)
    c = _dot(h, wd_ref[...])
    cos = cs_ref[:, :LANES]
    sin = cs_ref[:, LANES:]
    cq = _rms(c[:, :MLA_Q_RANK], qg_ref[...]).astype(BF16)
    q = _dot(cq, wuq_ref[...])
    o_ref[:, :MLA_QN] = (q[:, :MLA_QN] * scale).astype(o_ref.dtype)
    for m in range(MLA_QR // LANES):
        a = MLA_QN + m * LANES
        rot = q[:, a:a + LANES] * cos + q[:, a + MLA_QR:a + MLA_QR + LANES] * sin
        o_ref[:, a:a + LANES] = (rot * scale).astype(o_ref.dtype)
    kv0 = MLA_Q_RANK + MLA_KV_RANK
    ckv = _rms(c[:, MLA_Q_RANK:kv0], kvg_ref[...]).astype(BF16)
    kv = _dot(ckv, wukv_ref[...])
    kn0 = MLA_QN + MLA_QR
    o_ref[:, kn0:kn0 + MLA_QN] = kv[:, :MLA_QN].astype(o_ref.dtype)
    kr = c[:, kv0:kv0 + LANES] * cos + c[:, kv0 + LANES:kv0 + 2 * LANES] * sin
    o_ref[:, MLA_OUT - LANES:] = kr.astype(o_ref.dtype)
    _store_vt(vt_ref, kv[:, MLA_QN:], tk)


def _mla_rope_layout():
    half = MLA_ROPE_DIM // 2
    per = LANES // MLA_ROPE_DIM
    dq = MLA_NOPE_DIM + MLA_ROPE_DIM
    q_nope = np.concatenate([h * dq + np.arange(MLA_NOPE_DIM) for h in range(MLA_HEADS)])
    q_rope = np.zeros(MLA_QR, np.int64)
    q_swap = np.zeros(MLA_QR, np.int64)
    for h in range(MLA_HEADS):
        m, r = divmod(h, per)
        x1 = h * dq + MLA_NOPE_DIM + np.arange(half)
        x2 = x1 + half
        first = m * LANES + r * half + np.arange(half)
        second = first + LANES // 2
        q_rope[first], q_rope[second] = x1, x2
        q_swap[first], q_swap[second] = x2, x1
    k_rope = np.zeros(LANES, np.int64)
    k_swap = np.zeros(LANES, np.int64)
    base = MLA_Q_RANK + MLA_KV_RANK
    for r in range(per):
        first = r * half + np.arange(half)
        second = first + LANES // 2
        k_rope[first], k_rope[second] = base + np.arange(half), base + half + np.arange(half)
        k_swap[first], k_swap[second] = base + half + np.arange(half), base + np.arange(half)
    dkv = MLA_NOPE_DIM + HEAD_DIM
    k_nope = np.concatenate([h * dkv + np.arange(MLA_NOPE_DIM) for h in range(MLA_HEADS)])
    v = k_nope + MLA_NOPE_DIM
    down = np.concatenate([np.arange(base), k_rope, k_swap])
    return np.concatenate([q_nope, q_rope, q_swap]), np.concatenate([k_nope, v]), down


def _mla_rope_tables(seq):
    half = MLA_ROPE_DIM // 2
    inv = ROPE_THETA ** (-jnp.arange(0, MLA_ROPE_DIM, 2, dtype=F32) / MLA_ROPE_DIM)
    ang = jnp.arange(seq, dtype=F32)[:, None] * inv[None, :]
    cos, sin = jnp.cos(ang), jnp.sin(ang)
    reps = LANES // 2 // half
    cos_t = jnp.tile(cos, (1, 2 * reps))
    sin_t = jnp.concatenate([jnp.tile(-sin, (1, reps)), jnp.tile(sin, (1, reps))], axis=1)
    return jnp.concatenate([cos_t, sin_t], axis=1)


def _mla_proj(x, g, w_down, q_norm, w_uq, kv_norm, w_ukv, seq):
    t, d = x.shape
    tm = _tile(seq, 512)
    tk = _tile(seq, ATT_TK)
    ns = seq // tm
    uq_cols, ukv_cols, down_cols = _mla_rope_layout()
    wd = w_down[:, down_cols].astype(BF16)
    wuq = w_uq[:, uq_cols].astype(BF16)
    wukv = w_ukv[:, ukv_cols].astype(BF16)
    cs = _mla_rope_tables(seq)
    scale = (MLA_NOPE_DIM + MLA_ROPE_DIM) ** -0.5 * LOG2E
    full = lambda a: pl.BlockSpec(a.shape, lambda i: (0, 0))
    g2, qg, kvg = g.reshape(1, d), q_norm.reshape(1, -1), kv_norm.reshape(1, -1)
    return pl.pallas_call(
        functools.partial(_mla_proj_kernel, scale=scale, tk=tk),
        grid=(t // tm,),
        in_specs=[pl.BlockSpec((tm, d), lambda i: (i, 0)), full(g2), full(wd), full(qg), full(wuq),
                  full(kvg), full(wukv), pl.BlockSpec((tm, 2 * LANES), lambda i: (i % ns, 0))],
        out_specs=[pl.BlockSpec((tm, MLA_OUT), lambda i: (i, 0)),
                   pl.BlockSpec((tm // tk, MLA_QN, tk), lambda i: (i, 0, 0))],
        out_shape=[jax.ShapeDtypeStruct((t, MLA_OUT), BF16),
                   jax.ShapeDtypeStruct((t // tk, MLA_QN, tk), BF16)],
        compiler_params=_params("parallel"),
        name="mla_proj",
    )(x, g2, wd, qg, wuq, kvg, wukv, cs)


def _mla_kernel(qn_ref, qr_ref, kn_ref, kr_ref, vt_ref, o_ref, acc_ref, ml_ref, s_ref, *, tq, tk):
    p = pl.program_id(1)
    i = pl.program_id(2)
    q = jnp.concatenate([qn_ref[...], qr_ref[...]], axis=1)
    lane = lax.broadcasted_iota(jnp.int32, (tq, 2 * LANES), 1)
    half = MLA_ROPE_DIM // 2
    zero = jnp.zeros_like(q)
    qms = []
    for j in range(2):
        r0 = LANES + (2 * (p % 2) + j) * half
        sel = jnp.logical_and(lane >= j * HEAD_DIM, lane < (j + 1) * HEAD_DIM)
        sel = jnp.logical_or(sel, jnp.logical_and(lane >= r0, lane < r0 + half))
        r1 = r0 + LANES // 2
        sel = jnp.logical_or(sel, jnp.logical_and(lane >= r1, lane < r1 + half))
        qms.append(jnp.where(sel, q, zero))
    acc_ref[...] = jnp.zeros_like(acc_ref)
    row = lax.broadcasted_iota(jnp.int32, ml_ref.shape, 0)
    ml_ref[...] = jnp.where(row < 2, NEG, 0.0)

    def scores(unit, _):
        jb, h, lanes, _ = unit
        start = pl.multiple_of(jb * tk, tk)
        kt = jnp.concatenate([kn_ref[pl.ds(start, tk), :], kr_ref[pl.ds(start, tk), :]], axis=1)
        return _dot_nt(kt, qms[h][lanes])

    def softmax(unit, s):
        _, h, lanes, mask = unit
        if mask is not None:
            s = jnp.where(mask, s, NEG)
        m_old = ml_ref[h:h + 1, lanes]
        m_new = jnp.maximum(m_old, jnp.max(s, axis=0, keepdims=True))
        alpha = jnp.exp2(m_old - m_new)
        e = jnp.exp2(s - m_new)
        ml_ref[h:h + 1, lanes] = m_new
        ml_ref[2 + h:3 + h, lanes] = alpha * ml_ref[2 + h:3 + h, lanes] + jnp.sum(e, axis=0, keepdims=True)
        return alpha, e.astype(BF16)

    def values(unit, val):
        jb, h, lanes, _ = unit
        alpha, e = val
        rows = slice(h * HEAD_DIM, (h + 1) * HEAD_DIM)
        acc_ref[rows, lanes] = alpha * acc_ref[rows, lanes] + _dot(vt_ref[jb, rows, :], e)

    _causal_sweep(i, tq, tk, (scores, softmax, values), False, s_ref)
    head = lax.broadcasted_iota(jnp.int32, acc_ref.shape, 0) < HEAD_DIM
    denom = jnp.where(head, ml_ref[2:3, :], ml_ref[3:4, :])
    o_ref[...] = (acc_ref[...] / denom).T.astype(o_ref.dtype)


def _mla_attention(a, vt, batch, seq):
    t = a.shape[0]
    tq = _tile(seq, ATT_TQ)
    tk = vt.shape[2]
    nq = seq // tq
    pairs = MLA_HEADS // 2
    qr0 = MLA_QN // LANES
    kn0 = (MLA_QN + MLA_QR) // LANES
    kr0 = kn0 + pairs
    return pl.pallas_call(
        functools.partial(_mla_kernel, tq=tq, tk=tk),
        grid=(batch, pairs, nq),
        in_specs=[pl.BlockSpec((tq, LANES), lambda b, p, i: (b * nq + i, p)),
                  pl.BlockSpec((tq, LANES), lambda b, p, i: (b * nq + i, qr0 + p // 2)),
                  pl.BlockSpec((seq, LANES), lambda b, p, i: (b, kn0 + p)),
                  pl.BlockSpec((seq, LANES), lambda b, p, i: (b, kr0)),
                  pl.BlockSpec((seq // tk, LANES, tk), lambda b, p, i: (b, p, 0))],
        out_specs=pl.BlockSpec((tq, LANES), lambda b, p, i: (b * nq + i, p)),
        out_shape=jax.ShapeDtypeStruct((t, MLA_HEADS * HEAD_DIM), BF16),
        scratch_shapes=[pltpu.VMEM((LANES, tq), F32), pltpu.VMEM((8, tq), F32),
                        pltpu.VMEM((2 * (tq // tk), tk, tq), F32)],
        compiler_params=_params("parallel", "parallel", "arbitrary"),
        name="mla_attention",
    )(a, a, a, a, vt)


FFN_CHUNK = 256


def _swiglu_into(acc_ref, h, wg, wu, wd):
    def gate_up(cols, _):
        return _dot(h, wg(cols)), _dot(h, wu(cols))

    def activate(cols, gu):
        g, u = gu
        return (g * jax.nn.sigmoid(g) * u).astype(BF16)

    def down(cols, a):
        acc_ref[...] += _dot(a, wd(cols))

    return gate_up, activate, down


def _run_swiglu(acc_ref, h, wg, wu, wd, width):
    chunks = [slice(c, min(c + FFN_CHUNK, width)) for c in range(0, width, FFN_CHUNK)]
    _staggered(chunks, _swiglu_into(acc_ref, h, wg, wu, wd))


def _ffn_kernel(x_ref, g_ref, wg_ref, wu_ref, wd_ref, o_ref, h_ref):
    f = pl.program_id(1)

    @pl.when(f == 0)
    def _():
        x = x_ref[...]
        h_ref[...] = _rms(x, g_ref[...]).astype(BF16)
        o_ref[...] = x

    _run_swiglu(o_ref, h_ref[...], lambda c: wg_ref[:, c], lambda c: wu_ref[:, c], lambda c: wd_ref[c, :],
                wg_ref.shape[1])


def _ffn(x, g, wg, wu, wd):
    t, d = x.shape
    fdim = wg.shape[1]
    tm = _tile(t, 512)
    tf = fdim
    return pl.pallas_call(
        _ffn_kernel,
        grid=(t // tm, fdim // tf),
        in_specs=[pl.BlockSpec((tm, d), lambda i, f: (i, 0)),
                  pl.BlockSpec((1, d), lambda i, f: (0, 0)),
                  pl.BlockSpec((d, tf), lambda i, f: (0, f)),
                  pl.BlockSpec((d, tf), lambda i, f: (0, f)),
                  pl.BlockSpec((tf, d), lambda i, f: (f, 0))],
        out_specs=pl.BlockSpec((tm, d), lambda i, f: (i, 0)),
        out_shape=jax.ShapeDtypeStruct((t, d), F32),
        scratch_shapes=[pltpu.VMEM((tm, d), BF16)],
        compiler_params=_params("parallel", "arbitrary"),
        name="dense_ffn",
    )(x, g.reshape(1, d), wg, wu, wd)


MOE_TILE = 512
META_ROWS = 8
ROUTE_SUB = 128
COMBINE_SUB = 256


def _pack_halves(x):
    half = x.shape[1] // 2
    bits = lax.bitcast_convert_type(x.astype(BF16).astype(F32), jnp.uint32)
    return bits[:, :half] | (bits[:, half:] >> 16)


def _unpack_halves(w):
    hi = lax.bitcast_convert_type(w & jnp.uint32(0xFFFF0000), F32)
    lo = lax.bitcast_convert_type(w << 16, F32)
    return jnp.concatenate([hi, lo], axis=1)


def _router_kernel(x_ref, g_ref, rt_ref, su_ref, eye_ref, hb_ref, meta_ref, metac_ref, cnt_ref, carry_ref):
    @pl.when(pl.program_id(0) == 0)
    def _():
        carry_ref[...] = jnp.zeros_like(carry_ref)

    h = _rms(x_ref[...], g_ref[...])
    hb_ref[...] = _pack_halves(h)
    logits = lax.dot_general(rt_ref[...], h, _NT, precision=lax.Precision.HIGHEST,
                             preferred_element_type=F32)
    ne, ct = logits.shape
    eio = lax.broadcasted_iota(jnp.int32, (ne, ct), 0).astype(F32)
    m1 = jnp.max(logits, axis=0, keepdims=True)
    i1 = jnp.min(jnp.where(logits == m1, eio, float(ne)), axis=0, keepdims=True)
    rest = jnp.where(eio == i1, -jnp.inf, logits)
    m2 = jnp.max(rest, axis=0, keepdims=True)
    i2 = jnp.min(jnp.where(rest == m2, eio, float(ne)), axis=0, keepdims=True)
    e2 = jnp.exp(m2 - m1)
    g1 = 1.0 / (1.0 + e2)
    g2 = e2 / (1.0 + e2)
    oh1 = (eio == i1).astype(F32)
    oh2 = (eio == i2).astype(F32)
    both = oh1 + oh2
    seen = carry_ref[:, 0:1] + _dot(both.astype(BF16), su_ref[...])
    r1 = jnp.sum(oh1 * seen, axis=0, keepdims=True)
    r2 = jnp.sum(oh2 * seen, axis=0, keepdims=True)
    meta = jnp.zeros((META_ROWS, ct), F32)
    for row, val in enumerate((i1, i2, r1, r2, g1, g2)):
        meta = jnp.where(eio == float(row), val, meta)
    meta_ref[...] = meta
    metac_ref[...] = lax.dot_general(eye_ref[...], meta, _NT, precision=lax.Precision.HIGHEST,
                                     preferred_element_type=F32)
    carry_ref[...] = carry_ref[...] + jnp.sum(both, axis=1, keepdims=True)
    cnt_ref[0] = carry_ref[...]


def _router(x, g, router):
    t, d = x.shape
    ct = _tile(t, MOE_TILE)
    nc = t // ct
    su = jnp.asarray(np.triu(np.ones((ct, ct), np.float32), 1), BF16)
    eye = jnp.eye(ct, dtype=F32)
    return pl.pallas_call(
        _router_kernel,
        grid=(nc,),
        in_specs=[pl.BlockSpec((ct, d), lambda i: (i, 0)),
                  pl.BlockSpec((1, d), lambda i: (0, 0)),
                  pl.BlockSpec((N_EXPERTS, d), lambda i: (0, 0)),
                  pl.BlockSpec((ct, ct), lambda i: (0, 0)),
                  pl.BlockSpec((ct, ct), lambda i: (0, 0))],
        out_specs=[pl.BlockSpec((ct, d // 2), lambda i: (i, 0)),
                   pl.BlockSpec((META_ROWS, ct), lambda i: (0, i)),
                   pl.BlockSpec((ct, META_ROWS), lambda i: (i, 0)),
                   pl.BlockSpec((1, N_EXPERTS, LANES), lambda i: (i, 0, 0))],
        out_shape=[jax.ShapeDtypeStruct((t, d // 2), jnp.uint32),
                   jax.ShapeDtypeStruct((META_ROWS, t), F32),
                   jax.ShapeDtypeStruct((t, META_ROWS), F32),
                   jax.ShapeDtypeStruct((nc, N_EXPERTS, LANES), F32)],
        scratch_shapes=[pltpu.VMEM((N_EXPERTS, LANES), F32)],
        compiler_params=_params("arbitrary"),
        name="moe_router",
    )(x, g.reshape(1, d), router.T, su, eye)


def _group_layout(cnt_after, t, rt):
    rows = 2 * t + N_EXPERTS * rt
    counts = cnt_after[:, :, 0].astype(jnp.int32)
    padded = (counts[-1] + rt - 1) // rt * rt
    off_end = jnp.cumsum(padded)
    off = (off_end - padded).astype(jnp.int32)
    tile_start = jnp.arange(rows // rt, dtype=jnp.int32) * rt
    tile_expert = jnp.minimum(_count_le(off_end, tile_start), N_EXPERTS - 1).astype(jnp.int32)
    tile_valid = (tile_start < off_end[-1]).astype(jnp.int32)
    return rows, off, counts, tile_expert, tile_valid


def _count_le(sorted_vals, x):
    return jnp.sum((sorted_vals[None, :] <= x[:, None]).astype(jnp.int32), axis=1)


def _work_lists(counts, off, rows, t, ct):
    nc = t // ct
    nrt = rows // ct
    before = jnp.concatenate([jnp.zeros((1, N_EXPERTS), jnp.int32), counts[:-1]], axis=0)
    chunk_start = (off[:, None] + before.T).reshape(-1)
    tile_start = jnp.arange(nrt, dtype=jnp.int32) * ct
    starts = jnp.sort(jnp.concatenate([chunk_start, tile_start]))
    ends = jnp.concatenate([starts[1:], jnp.array([rows], jnp.int32)])
    item_r = jnp.minimum(starts // ct, nrt - 1)
    item_c = (_count_le(chunk_start, starts) - 1) % nc
    first = lambda key: jnp.concatenate([jnp.ones((1,), jnp.int32), (key[1:] != key[:-1]).astype(jnp.int32)])
    dispatch = (item_r, item_c, starts, ends, first(item_r))
    order = jnp.argsort(item_c * (rows + 1) + starts)
    c2 = item_c[order]
    combine = (c2, item_r[order], starts[order], ends[order], first(c2))
    return dispatch, combine


def _positions(expert, rank, off_ref):
    base = jnp.zeros_like(rank)
    for e in range(N_EXPERTS):
        base = jnp.where(expert == float(e), off_ref[e].astype(F32), base)
    return base + rank


def _dispatch_kernel(r_ref, c_ref, s_ref, e_ref, first_ref, off_ref, meta_ref, h_ref, xs_ref, gate_ref):
    n = pl.program_id(0)
    rt, ct = xs_ref.shape[0], h_ref.shape[0]

    @pl.when(first_ref[n] == 1)
    def _():
        xs_ref[...] = jnp.zeros_like(xs_ref)
        gate_ref[...] = jnp.zeros_like(gate_ref)

    meta = meta_ref[...]
    pos0 = _positions(meta[0:1], meta[2:3], off_ref)
    pos1 = _positions(meta[1:2], meta[3:4], off_ref)
    sub = min(rt, ROUTE_SUB)
    for sb in range(rt // sub):
        base = r_ref[n] * rt + sb * sub

        @pl.when(jnp.logical_and(base < e_ref[n], base + sub > s_ref[n]))
        def _():
            row = lax.broadcasted_iota(jnp.int32, (sub, ct), 0) + base
            live = jnp.logical_and(row >= s_ref[n], row < e_ref[n])
            rowf = row.astype(F32)
            sel0 = jnp.logical_and(pos0 == rowf, live)
            sel1 = jnp.logical_and(pos1 == rowf, live)
            sel = jnp.logical_or(sel0, sel1).astype(F32).astype(BF16)
            rows = slice(sb * sub, (sb + 1) * sub)
            xs_ref[rows, :] += _pack_halves(_dot(sel, _unpack_halves(h_ref[...]).astype(BF16)))
            gate = jnp.where(sel0, meta[4:5], 0.0) + jnp.where(sel1, meta[5:6], 0.0)
            gate_ref[rows, :] += jnp.sum(gate, axis=1, keepdims=True)


def _dispatch(plan, off, meta, hb, rows, ct):
    t, d = hb.shape
    n_items = plan[0].shape[0]
    return pl.pallas_call(
        _dispatch_kernel,
        grid_spec=pltpu.PrefetchScalarGridSpec(
            num_scalar_prefetch=6,
            grid=(n_items,),
            in_specs=[pl.BlockSpec((META_ROWS, ct), lambda n, r, c, *_: (0, c[n])),
                      pl.BlockSpec((ct, d), lambda n, r, c, *_: (c[n], 0))],
            out_specs=[pl.BlockSpec((ct, d), lambda n, r, c, *_: (r[n], 0)),
                       pl.BlockSpec((ct, 1), lambda n, r, c, *_: (r[n], 0))]),
        out_shape=[jax.ShapeDtypeStruct((rows, d), jnp.uint32),
                   jax.ShapeDtypeStruct((rows, 1), F32)],
        compiler_params=_params("arbitrary"),
        name="moe_dispatch",
    )(*plan, off, meta, hb)


def _expert_kernel(te_ref, tv_ref, xs_ref, gate_ref, wg_ref, wu_ref, wd_ref, y_ref, acc_ref):
    r = pl.program_id(0)
    f = pl.program_id(1)

    @pl.when(f == 0)
    def _():
        acc_ref[...] = jnp.zeros_like(acc_ref)

    @pl.when(tv_ref[r] == 1)
    def _():
        _run_swiglu(acc_ref, _unpack_halves(xs_ref[...]).astype(BF16), lambda c: wg_ref[0, :, c],
                    lambda c: wu_ref[0, :, c], lambda c: wd_ref[0, c, :], wg_ref.shape[2])

    @pl.when(f == pl.num_programs(1) - 1)
    def _():
        y_ref[...] = _pack_halves(acc_ref[...] * gate_ref[...])


def _experts(tile_expert, tile_valid, xs, gate, wg, wu, wd, ct):
    rows, dp = xs.shape
    d = 2 * dp
    fdim = wg.shape[2]
    tf = _tile(fdim, 1792)
    nf = fdim // tf

    def wcol(r, f, te, tv):
        return (te[r], 0, jnp.where(tv[r] == 1, f, nf - 1))

    def wrow(r, f, te, tv):
        return (te[r], jnp.where(tv[r] == 1, f, nf - 1), 0)

    return pl.pallas_call(
        _expert_kernel,
        grid_spec=pltpu.PrefetchScalarGridSpec(
            num_scalar_prefetch=2,
            grid=(rows // ct, nf),
            in_specs=[pl.BlockSpec((ct, dp), lambda r, f, te, tv: (r, 0)),
                      pl.BlockSpec((ct, 1), lambda r, f, te, tv: (r, 0)),
                      pl.BlockSpec((1, d, tf), wcol),
                      pl.BlockSpec((1, d, tf), wcol),
                      pl.BlockSpec((1, tf, d), wrow)],
            out_specs=pl.BlockSpec((ct, dp), lambda r, f, te, tv: (r, 0)),
            scratch_shapes=[pltpu.VMEM((ct, d), F32)]),
        out_shape=jax.ShapeDtypeStruct((rows, dp), jnp.uint32),
        compiler_params=_params("arbitrary", "arbitrary"),
        name="moe_experts",
    )(tile_expert, tile_valid, xs, gate, wg, wu, wd)


def _combine_kernel(c_ref, r_ref, s_ref, e_ref, first_ref, off_ref, x_ref, metac_ref, y_ref, o_ref):
    n = pl.program_id(0)
    ct, rt = x_ref.shape[0], y_ref.shape[0]

    @pl.when(first_ref[n] == 1)
    def _():
        o_ref[...] = x_ref[...]

    metac = metac_ref[...]
    pos0 = _positions(metac[:, 0:1], metac[:, 2:3], off_ref)
    pos1 = _positions(metac[:, 1:2], metac[:, 3:4], off_ref)
    sub = min(rt, COMBINE_SUB)
    for sb in range(rt // sub):
        base = r_ref[n] * rt + sb * sub

        @pl.when(jnp.logical_and(base < e_ref[n], base + sub > s_ref[n]))
        def _():
            row = lax.broadcasted_iota(jnp.int32, (ct, sub), 1) + base
            live = jnp.logical_and(row >= s_ref[n], row < e_ref[n])
            rowf = row.astype(F32)
            sel = jnp.logical_and(jnp.logical_or(pos0 == rowf, pos1 == rowf), live).astype(F32).astype(BF16)
            o_ref[...] += _dot(sel, _unpack_halves(y_ref[sb * sub:(sb + 1) * sub, :]).astype(BF16))


def _combine(plan, off, x, metac, y, ct):
    t, d = x.shape
    n_items = plan[0].shape[0]
    return pl.pallas_call(
        _combine_kernel,
        grid_spec=pltpu.PrefetchScalarGridSpec(
            num_scalar_prefetch=6,
            grid=(n_items,),
            in_specs=[pl.BlockSpec((ct, d), lambda n, c, r, *_: (c[n], 0)),
                      pl.BlockSpec((ct, META_ROWS), lambda n, c, r, *_: (c[n], 0)),
                      pl.BlockSpec((ct, d // 2), lambda n, c, r, *_: (r[n], 0))],
            out_specs=pl.BlockSpec((ct, d), lambda n, c, r, *_: (c[n], 0))),
        out_shape=jax.ShapeDtypeStruct((t, d), F32),
        compiler_params=_params("arbitrary"),
        name="moe_combine",
    )(*plan, off, x, metac, y)


def _cast_kernel(w_ref, o_ref):
    o_ref[...] = w_ref[...].astype(o_ref.dtype)


CAST_BLOCK_BYTES = 8 * 1024 * 1024


def _layer_bf16(w, layer):
    shape = w.shape[1:]
    cols = shape[-1]
    rows = math.prod(shape[:-1])
    tr = rows
    while tr * cols * 4 > CAST_BLOCK_BYTES and tr % 16 == 0:
        tr //= 2
    out = pl.pallas_call(
        _cast_kernel,
        grid=(rows // tr,),
        in_specs=[pl.BlockSpec((None, tr, cols), lambda i: (layer, i, 0))],
        out_specs=pl.BlockSpec((tr, cols), lambda i: (i, 0)),
        out_shape=jax.ShapeDtypeStruct((rows, cols), BF16),
        compiler_params=_params("parallel"),
        name="weights_bf16",
    )(w.reshape(w.shape[0], rows, cols))
    return out.reshape(shape)


SC_CORES = 2
SC_SUBCORES = 16
SC_CHUNK = 128


def _sc_worker_chunks(total):
    per = total // (SC_CORES * SC_SUBCORES)
    wid = lax.axis_index("s") * SC_CORES + lax.axis_index("c")
    return wid * per, per // SC_CHUNK


def _sc_mesh():
    return plsc.VectorSubcoreMesh(core_axis_name="c", subcore_axis_name="s")


def _sc_gather(table, idx):
    v, d = table.shape
    b = idx.shape[0]

    def body(table_hbm, idx_hbm, out_hbm, idx_v, rows_v, sem):
        base, chunks = _sc_worker_chunks(b)

        @pl.loop(0, chunks)
        def _(j):
            start = base + j * SC_CHUNK
            pltpu.sync_copy(idx_hbm.at[pl.ds(start, SC_CHUNK)], idx_v)
            pltpu.async_copy(table_hbm.at[idx_v], rows_v, sem).wait()
            pltpu.sync_copy(rows_v, out_hbm.at[pl.ds(start, SC_CHUNK)])

    return pl.kernel(
        body,
        out_type=jax.ShapeDtypeStruct((b, d), table.dtype),
        mesh=_sc_mesh(),
        scratch_types=[pltpu.VMEM((SC_CHUNK,), jnp.int32),
                       pltpu.VMEM((SC_CHUNK, d), table.dtype),
                       pltpu.SemaphoreType.DMA],
        name="sc_row_gather",
    )(table, idx)


def _sc_scatter_rows(src, idx, n):
    t, d = src.shape
    b = idx.shape[0]

    def body(src_hbm, idx_hbm, out_hbm, idx_v, rows_v):
        base, chunks = _sc_worker_chunks(b)

        @pl.loop(0, chunks)
        def _(j):
            start = base + j * SC_CHUNK
            first = lax.rem(start, t)
            pltpu.sync_copy(idx_hbm.at[pl.ds(start, SC_CHUNK)], idx_v)
            pltpu.sync_copy(src_hbm.at[pl.ds(first, SC_CHUNK)], rows_v)
            pltpu.sync_copy(rows_v, out_hbm.at[idx_v])

    return pl.kernel(
        body,
        out_type=jax.ShapeDtypeStruct((n, d), src.dtype),
        mesh=_sc_mesh(),
        scratch_types=[pltpu.VMEM((SC_CHUNK,), jnp.int32), pltpu.VMEM((SC_CHUNK, d), src.dtype)],
        name="sc_row_scatter",
    )(src, idx)


def _weighted_add_kernel(x_ref, metac_ref, y0_ref, y1_ref, g_ref, o_ref, *, normed):
    gates = metac_ref[...]
    out = (x_ref[...] + gates[:, 4:5] * _unpack_halves(y0_ref[...])
           + gates[:, 5:6] * _unpack_halves(y1_ref[...]))
    o_ref[...] = _rms(out, g_ref[...]) if normed else out


def _weighted_add(x, metac, ysel, out_gain=None):
    t, d = x.shape
    tm = _tile(t, 512)
    nt = t // tm
    gain = jnp.ones((d,), F32) if out_gain is None else out_gain
    return pl.pallas_call(
        functools.partial(_weighted_add_kernel, normed=out_gain is not None),
        grid=(nt,),
        in_specs=[pl.BlockSpec((tm, d), lambda i: (i, 0)),
                  pl.BlockSpec((tm, META_ROWS), lambda i: (i, 0)),
                  pl.BlockSpec((tm, d // 2), lambda i: (i, 0)),
                  pl.BlockSpec((tm, d // 2), lambda i: (nt + i, 0)),
                  pl.BlockSpec((1, d), lambda i: (0, 0))],
        out_specs=pl.BlockSpec((tm, d), lambda i: (i, 0)),
        out_shape=jax.ShapeDtypeStruct((t, d), F32),
        compiler_params=_params("parallel"),
        name="moe_weighted_add",
    )(x, metac, ysel, ysel, gain.reshape(1, d))


def _moe(x, g, router, wg, wu, wd, out_gain=None):
    t = x.shape[0]
    ct = _tile(t, MOE_TILE)
    hb, meta, metac, cnt = _router(x, g, router)
    sc_rows = SC_CORES * SC_SUBCORES * SC_CHUNK
    rows, off, counts, tile_expert, tile_valid = _group_layout(cnt, t, ct)
    if (2 * t) % sc_rows == 0 and rows % sc_rows == 0:
        base = sum(jnp.where(meta[0:2] == float(e), off[e], 0) for e in range(N_EXPERTS))
        pos = (base + meta[2:4].astype(jnp.int32)).reshape(-1)
        xs = _sc_scatter_rows(hb, pos, rows)
        y = _experts(tile_expert, tile_valid, xs, jnp.ones((rows, 1), F32), wg, wu, wd, ct)
        return _weighted_add(x, metac, _sc_gather(y, pos), out_gain)
    dispatch, combine = _work_lists(counts, off, rows, t, ct)
    xs, gate = _dispatch(dispatch, off, meta, hb, rows, ct)
    y = _experts(tile_expert, tile_valid, xs, gate, wg, wu, wd, ct)
    out = _combine(combine, off, x, metac, y, ct)
    return out if out_gain is None else _final_norm(out, out_gain)


def _final_norm_kernel(x_ref, g_ref, o_ref):
    o_ref[...] = _rms(x_ref[...], g_ref[...])


def _final_norm(x, g):
    t, d = x.shape
    tm = _tile(t, 1024)
    return pl.pallas_call(
        _final_norm_kernel,
        grid=(t // tm,),
        in_specs=[pl.BlockSpec((tm, d), lambda i: (i, 0)), pl.BlockSpec((1, d), lambda i: (0, 0))],
        out_specs=pl.BlockSpec((tm, d), lambda i: (i, 0)),
        out_shape=jax.ShapeDtypeStruct((t, d), F32),
        compiler_params=_params("parallel"),
        name="final_norm",
    )(x, g.reshape(1, d))


def kernel(x, rel_bias, attn_norm, ffn_norm, final_norm, swa_w_qkv, swa_b_qkv, swa_sinks, swa_w_o, sb_w_qkv, sb_w_o, mla_w_down, mla_q_norm, mla_w_uq, mla_kv_norm, mla_w_ukv, mla_w_o, ffn_w_gate, ffn_w_up, ffn_w_down, moe_router, moe_w_gate, moe_w_up, moe_w_down):
    batch, seq, d = x.shape
    depth = attn_norm.shape[0]
    xt = x.reshape(batch * seq, d)
    swa_bias = _swa_bias(rel_bias)
    for i in range(depth):
        mixer, j = i % N_MIXERS, i // N_MIXERS
        if mixer == 0:
            w, b, cs = _swa_weights(swa_w_qkv[j], swa_b_qkv[j])
            qkv = _norm_proj(xt, attn_norm[i], w, b, cs, "swa_proj")
            o = _swa_attention(qkv, swa_bias, swa_sinks[j], batch, seq)
            xt = _out_proj(xt, o, swa_w_o[j].astype(BF16), "swa_out")
        elif mixer == 1:
            n = sb_w_qkv.shape[2] // 3
            cs = jnp.concatenate([jnp.full((n,), HEAD_DIM ** -0.5 * LOG2E, F32), jnp.ones((n,), F32)])
            qk, vt = _norm_proj_vt(xt, attn_norm[i], sb_w_qkv[j].astype(BF16), cs, n, seq, "sb_proj")
            o = _sb_attention(qk, vt, batch, seq)
            xt = _out_proj(xt, o, sb_w_o[j].astype(BF16), "sb_out")
        else:
            a, vt = _mla_proj(xt, attn_norm[i], mla_w_down[j], mla_q_norm[j], mla_w_uq[j],
                              mla_kv_norm[j], mla_w_ukv[j], seq)
            o = _mla_attention(a, vt, batch, seq)
            xt = _out_proj(xt, o, mla_w_o[j].astype(BF16), "mla_out")
        f = i // 2
        if i % 2 == 0:
            xt = _ffn(xt, ffn_norm[i], _layer_bf16(ffn_w_gate, f), _layer_bf16(ffn_w_up, f),
                      _layer_bf16(ffn_w_down, f))
        else:
            last = final_norm if i == depth - 1 else None
            xt = _moe(xt, ffn_norm[i], moe_router[f], _layer_bf16(moe_w_gate, f), _layer_bf16(moe_w_up, f),
                      _layer_bf16(moe_w_down, f), last)
    if depth % 2 == 1:
        xt = _final_norm(xt, final_norm)
    return xt.reshape(batch, seq, d)
```

```python
import functools
import math

import numpy as np
import jax
import jax.numpy as jnp
from jax import lax
from jax.experimental import pallas as pl
from jax.experimental.pallas import tpu as pltpu
from jax.experimental.pallas import tpu_sc as plsc

F32 = jnp.float32
BF16 = jnp.bfloat16

N_MIXERS = 3
RMS_EPS = 1e-6
SWA_HEADS = 16
SWA_KV_HEADS = 2
HEAD_DIM = 64
SWA_BLOCK = 128
SWA_WINDOW = 128
REL_BUCKETS = 32
REL_MAX_DIST = 128
SB_HEADS = 16
MLA_HEADS = 16
MLA_NOPE_DIM = 64
MLA_ROPE_DIM = 32
MLA_Q_RANK = 384
MLA_KV_RANK = 256
ROPE_THETA = 10000.0
N_EXPERTS = 8

LANES = 128
NEG = -1e30
LOG2E = math.log2(math.e)
VMEM_LIMIT = 56 * 1024 * 1024

_NT = (((1,), (1,)), ((), ()))


def _dot(a, b):
    return jnp.dot(a, b, preferred_element_type=F32)


def _dot_nt(a, b):
    return lax.dot_general(a, b, _NT, preferred_element_type=F32)


def _rms(x, g):
    return x * lax.rsqrt(jnp.mean(x * x, axis=-1, keepdims=True) + RMS_EPS) * g


def _params(*sem):
    return pltpu.CompilerParams(dimension_semantics=sem, vmem_limit_bytes=VMEM_LIMIT)


def _tile(n, pref):
    return pref if n % pref == 0 else n


def _norm_proj_kernel(x_ref, g_ref, w_ref, b_ref, cs_ref, o_ref):
    h = _rms(x_ref[...], g_ref[...]).astype(BF16)
    y = (_dot(h, w_ref[...]) + b_ref[...]) * cs_ref[...]
    o_ref[...] = y.astype(o_ref.dtype)


def _norm_proj(x, g, w, b, colscale, name):
    t, d = x.shape
    n = w.shape[1]
    tm = _tile(t, 512)
    return pl.pallas_call(
        _norm_proj_kernel,
        grid=(t // tm,),
        in_specs=[pl.BlockSpec((tm, d), lambda i: (i, 0)),
                  pl.BlockSpec((1, d), lambda i: (0, 0)),
                  pl.BlockSpec((d, n), lambda i: (0, 0)),
                  pl.BlockSpec((1, n), lambda i: (0, 0)),
                  pl.BlockSpec((1, n), lambda i: (0, 0))],
        out_specs=pl.BlockSpec((tm, n), lambda i: (i, 0)),
        out_shape=jax.ShapeDtypeStruct((t, n), BF16),
        compiler_params=_params("parallel"),
        name=name,
    )(x, g.reshape(1, d), w, b.reshape(1, n), colscale.reshape(1, n))


ATT_TQ = 512
ATT_TK = 256


def _store_vt(vt_ref, v, tk):
    for s in range(v.shape[0] // tk):
        vt_ref[s] = v[s * tk:(s + 1) * tk, :].T.astype(vt_ref.dtype)


def _norm_proj_vt_kernel(x_ref, g_ref, w_ref, cs_ref, o_ref, vt_ref, *, tk):
    h = _rms(x_ref[...], g_ref[...]).astype(BF16)
    y = _dot(h, w_ref[...])
    n = o_ref.shape[1]
    o_ref[...] = (y[:, :n] * cs_ref[...]).astype(o_ref.dtype)
    _store_vt(vt_ref, y[:, n:], tk)


def _norm_proj_vt(x, g, w, colscale, nv, seq, name):
    t, d = x.shape
    n = w.shape[1] - nv
    tm = _tile(seq, 512)
    tk = _tile(seq, ATT_TK)
    per = tm // tk
    return pl.pallas_call(
        functools.partial(_norm_proj_vt_kernel, tk=tk),
        grid=(t // tm,),
        in_specs=[pl.BlockSpec((tm, d), lambda i: (i, 0)),
                  pl.BlockSpec((1, d), lambda i: (0, 0)),
                  pl.BlockSpec((d, n + nv), lambda i: (0, 0)),
                  pl.BlockSpec((1, n), lambda i: (0, 0))],
        out_specs=[pl.BlockSpec((tm, n), lambda i: (i, 0)),
                   pl.BlockSpec((per, nv, tk), lambda i: (i, 0, 0))],
        out_shape=[jax.ShapeDtypeStruct((t, n), BF16),
                   jax.ShapeDtypeStruct((t // tk, nv, tk), BF16)],
        compiler_params=_params("parallel"),
        name=name,
    )(x, g.reshape(1, d), w, colscale.reshape(1, n))


def _t5_bucket_table():
    qi = np.arange(SWA_BLOCK)[:, None]
    kj = np.arange(2 * SWA_BLOCK)[None, :]
    dist = qi + SWA_BLOCK - kj
    d0 = np.maximum(dist, 0)
    max_exact = REL_BUCKETS // 2
    d = np.maximum(d0, 1).astype(np.float32)
    large = max_exact + (np.log(d / max_exact) / math.log(REL_MAX_DIST / max_exact)
                         * (REL_BUCKETS - max_exact)).astype(np.int32)
    large = np.minimum(large, REL_BUCKETS - 1)
    bucket = np.where(d0 < max_exact, d0, large)
    band = (dist >= 0) & (dist < SWA_WINDOW)
    return np.where(band, bucket, -1).astype(np.int32)


def _swa_bias_kernel(rel_ref, bucket_ref, o_ref):
    h = pl.program_id(0)
    bucket = bucket_ref[...]
    acc = jnp.full(bucket.shape, NEG, F32)
    for b in range(REL_BUCKETS):
        acc = jnp.where(bucket == b, rel_ref[b, h], acc)
    o_ref[0] = acc


def _swa_bias(rel_bias):
    bucket = jnp.asarray(_t5_bucket_table())
    q, k = bucket.shape
    return pl.pallas_call(
        _swa_bias_kernel,
        grid=(SWA_HEADS,),
        in_specs=[pl.BlockSpec(memory_space=pltpu.SMEM),
                  pl.BlockSpec((q, k), lambda h: (0, 0))],
        out_specs=pl.BlockSpec((1, q, k), lambda h: (h, 0, 0)),
        out_shape=jax.ShapeDtypeStruct((SWA_HEADS, q, k), F32),
        compiler_params=_params("arbitrary"),
        name="swa_bias",
    )(rel_bias, bucket)


def _swa_kernel(sink_ref, q_ref, kc_ref, kp_ref, vc_ref, vp_ref, bias_ref, o_ref, *, tq):
    i = pl.program_id(1)
    g = pl.program_id(2)
    blk = SWA_BLOCK
    heads = bias_ref.shape[0]
    lo_half = lax.broadcasted_iota(jnp.int32, (blk, LANES), 1) < HEAD_DIM
    in_prev = lax.broadcasted_iota(jnp.int32, (heads * blk, 2 * blk), 1) < blk
    bias = bias_ref[...].reshape(heads * blk, 2 * blk)

    def window(cur_ref, prev_ref, sb):
        if sb == 0:
            return jnp.concatenate([prev_ref[...], cur_ref[0:blk, :]], axis=0)
        return cur_ref[(sb - 1) * blk:(sb + 1) * blk, :]

    def scores(sb, _):
        q = q_ref[sb * blk:(sb + 1) * blk, :]
        parts = []
        for h in range(heads):
            qp = q[:, (h // 2) * LANES:(h // 2 + 1) * LANES]
            keep = lo_half if h % 2 == 0 else jnp.logical_not(lo_half)
            parts.append(jnp.where(keep, qp, jnp.zeros_like(qp)))
        s = _dot_nt(jnp.concatenate(parts, axis=0), window(kc_ref, kp_ref, sb)) + bias
        if sb == 0:
            s = jnp.where(jnp.logical_and(in_prev, i == 0), NEG, s)
        return s

    def softmax(sb, s):
        ps, inv = [], []
        for h in range(heads):
            sh = s[h * blk:(h + 1) * blk]
            sink = sink_ref[g * heads + h]
            m = jnp.maximum(jnp.max(sh, axis=1, keepdims=True), sink)
            e = jnp.exp(sh - m)
            inv.append(1.0 / (jnp.sum(e, axis=1, keepdims=True) + jnp.exp(sink - m)))
            ps.append(e.astype(BF16))
        return jnp.concatenate(ps, axis=0), inv

    def values(sb, val):
        p, inv = val
        o = _dot(p, window(vc_ref, vp_ref, sb))
        outs = [jnp.where(lo_half, o[h * blk:(h + 1) * blk] * inv[h], o[(h + 1) * blk:(h + 2) * blk] * inv[h + 1])
                for h in range(0, heads, 2)]
        o_ref[sb * blk:(sb + 1) * blk, :] = jnp.concatenate(outs, axis=1).astype(o_ref.dtype)

    _staggered(list(range(tq // blk)), (scores, softmax, values))


def _swa_attention(qkv, bias, sinks, batch, seq):
    t = qkv.shape[0]
    tq = _tile(seq, 512)
    nt = seq // tq
    group = SWA_HEADS // SWA_KV_HEADS
    gw = group * HEAD_DIM
    kcol = SWA_HEADS * HEAD_DIM // LANES
    vcol = kcol + SWA_KV_HEADS
    per = tq // SWA_BLOCK

    def prev_idx(b, i, g):
        return jnp.maximum(b * (seq // SWA_BLOCK) + i * per - 1, 0)

    return pl.pallas_call(
        functools.partial(_swa_kernel, tq=tq),
        grid=(batch, nt, SWA_KV_HEADS),
        in_specs=[pl.BlockSpec(memory_space=pltpu.SMEM),
                  pl.BlockSpec((tq, gw), lambda b, i, g: (b * nt + i, g)),
                  pl.BlockSpec((tq, LANES), lambda b, i, g: (b * nt + i, kcol + g)),
                  pl.BlockSpec((SWA_BLOCK, LANES), lambda b, i, g: (prev_idx(b, i, g), kcol + g)),
                  pl.BlockSpec((tq, LANES), lambda b, i, g: (b * nt + i, vcol + g)),
                  pl.BlockSpec((SWA_BLOCK, LANES), lambda b, i, g: (prev_idx(b, i, g), vcol + g)),
                  pl.BlockSpec((group, SWA_BLOCK, 2 * SWA_BLOCK), lambda b, i, g: (g, 0, 0))],
        out_specs=pl.BlockSpec((tq, gw), lambda b, i, g: (b * nt + i, g)),
        out_shape=jax.ShapeDtypeStruct((t, SWA_HEADS * HEAD_DIM), BF16),
        compiler_params=_params("parallel", "parallel", "arbitrary"),
        name="swa_attention",
    )(sinks, qkv, qkv, qkv, qkv, qkv, bias)


def _swa_weights(w_qkv, b_qkv):
    nq = SWA_HEADS * HEAD_DIM
    kv = SWA_KV_HEADS * HEAD_DIM
    dup = np.concatenate([np.tile(np.arange(HEAD_DIM), 2) + h * HEAD_DIM for h in range(SWA_KV_HEADS)])
    cols = np.concatenate([np.arange(nq), nq + dup, nq + kv + dup])
    scale = np.concatenate([np.full(nq, HEAD_DIM ** -0.5), np.ones(2 * dup.size)]).astype(np.float32)
    return w_qkv[:, cols].astype(BF16), b_qkv[cols], jnp.asarray(scale)


def _staggered(units, stages):
    vals = list(units)
    for t in range(len(units) + len(stages) - 1):
        for k, stage in enumerate(stages):
            u = t - k
            if 0 <= u < len(units):
                vals[u] = stage(units[u], vals[u])


def _score_ahead(units, next_units, stages, s_ref):
    rest = stages[1:]
    vals = [None] * len(units)
    for t in range(len(units) + len(rest) - 1):
        ahead = stages[0](next_units[t], None) if t < len(units) else None
        for k, stage in enumerate(rest):
            u = t - k
            if 0 <= u < len(units):
                if k == 0:
                    lanes = units[u][2]
                    vals[u] = stage(units[u], s_ref[u, :, :lanes.stop - lanes.start])
                    s_ref[u] = ahead
                else:
                    vals[u] = stage(units[u], vals[u])


def _causal_sweep(i, tq, tk, stages, strict, s_ref):
    nsub = tq // tk

    def group(j):
        first = jnp.maximum(j, 0) * nsub
        return [(first + sb, h, slice(0, tq), None) for sb in reversed(range(nsub)) for h in range(2)]

    diagonal = []
    for sb in reversed(range(nsub)):
        nl = tq - sb * tk
        key = lax.broadcasted_iota(jnp.int32, (tk, nl), 0)
        qry = lax.broadcasted_iota(jnp.int32, (tk, nl), 1)
        mask = key < qry if strict else key <= qry
        diagonal += [(i * nsub + sb, h, slice(sb * tk, tq), mask) for h in range(2)]
    for u, unit in enumerate(diagonal):
        s_ref[u, :, :tq - unit[2].start] = stages[0](unit, None)
    _score_ahead(diagonal, group(i - 1), stages, s_ref)

    def body(n, carry):
        _score_ahead(group(i - 1 - n), group(i - 2 - n), stages, s_ref)
        return carry

    lax.fori_loop(0, i, body, 0)


def _transposed(x):
    return x.astype(F32).T.astype(x.dtype)


def _split_pair(q):
    lo_half = lax.broadcasted_iota(jnp.int32, q.shape, 1) < HEAD_DIM
    zero = jnp.zeros_like(q)
    return jnp.where(lo_half, q, zero), jnp.where(lo_half, zero, q)


def _sb_kernel(q_ref, k_ref, vt_ref, u_ref, o_ref, acc_ref, c_ref, s_ref, *, tq, tk):
    i = pl.program_id(2)
    qts = [_transposed(qm) for qm in _split_pair(q_ref[...])]
    u = u_ref[...]
    acc_ref[...] = jnp.zeros_like(acc_ref)
    c_ref[...] = jnp.zeros_like(c_ref)

    def scores(unit, _):
        jb, h, lanes, _ = unit
        kt = k_ref[pl.ds(pl.multiple_of(jb * tk, tk), tk), :]
        return _dot(kt, qts[h][:, lanes])

    def log_fail(unit, z):
        neg_abs = lax.bitcast_convert_type(lax.bitcast_convert_type(z, jnp.uint32) | jnp.uint32(1 << 31), F32)
        sp = jnp.maximum(z, 0.0) + jnp.log2(1.0 + jnp.exp2(neg_abs))
        if unit[3] is not None:
            sp = jnp.where(unit[3], sp, 0.0)
        return z, sp.astype(BF16)

    def suffix_sum(unit, val):
        z, sp = val
        return z, _dot(u, sp)

    def weights(unit, val):
        _, h, lanes, mask = unit
        z, r = val
        c = c_ref[h:h + 1, lanes]
        la = z - r - c
        if mask is not None:
            la = jnp.where(mask, la, NEG)
        c_ref[h:h + 1, lanes] = c + r[0:1, :]
        return jnp.exp2(la).astype(BF16)

    def values(unit, a):
        jb, h, lanes, _ = unit
        rows = slice(h * HEAD_DIM, (h + 1) * HEAD_DIM)
        acc_ref[rows, lanes] += _dot(vt_ref[jb, rows, :], a)

    _causal_sweep(i, tq, tk, (scores, log_fail, suffix_sum, weights, values), True, s_ref)
    o_ref[...] = acc_ref[...].T.astype(o_ref.dtype)


def _sb_attention(qk, vt, batch, seq):
    t = qk.shape[0]
    tq = _tile(seq, ATT_TQ)
    tk = vt.shape[2]
    nq = seq // tq
    pairs = SB_HEADS // 2
    u = jnp.asarray(np.triu(np.ones((tk, tk), np.float32)), BF16)
    return pl.pallas_call(
        functools.partial(_sb_kernel, tq=tq, tk=tk),
        grid=(batch, pairs, nq),
        in_specs=[pl.BlockSpec((tq, LANES), lambda b, p, i: (b * nq + i, p)),
                  pl.BlockSpec((seq, LANES), lambda b, p, i: (b, pairs + p)),
                  pl.BlockSpec((seq // tk, LANES, tk), lambda b, p, i: (b, p, 0)),
                  pl.BlockSpec((tk, tk), lambda b, p, i: (0, 0))],
        out_specs=pl.BlockSpec((tq, LANES), lambda b, p, i: (b * nq + i, p)),
        out_shape=jax.ShapeDtypeStruct((t, SB_HEADS * HEAD_DIM), BF16),
        scratch_shapes=[pltpu.VMEM((LANES, tq), F32), pltpu.VMEM((8, tq), F32),
                        pltpu.VMEM((2 * (tq // tk), tk, tq), F32)],
        compiler_params=_params("parallel", "parallel", "arbitrary"),
        name="sb_attention",
    )(qk, qk, vt, u)


MLA_QN = MLA_HEADS * MLA_NOPE_DIM
MLA_QR = MLA_HEADS * MLA_ROPE_DIM
MLA_OUT = 2 * MLA_QN + MLA_QR + LANES


def _mla_proj_kernel(x_ref, g_ref, wd_ref, qg_ref, wuq_ref, kvg_ref, wukv_ref, cs_ref, o_ref, vt_ref,
                     *, scale, tk):
    h = _rms(x_ref[...], g_ref[...]).astype(BF16)
    c = _dot(h, wd_ref[...])
    cos = cs_ref[:, :LANES]
    sin = cs_ref[:, LANES:]
    cq = _rms(c[:, :MLA_Q_RANK], qg_ref[...]).astype(BF16)
    q = _dot(cq, wuq_ref[...])
    o_ref[:, :MLA_QN] = (q[:, :MLA_QN] * scale).astype(o_ref.dtype)
    for m in range(MLA_QR // LANES):
        a = MLA_QN + m * LANES
        rot = q[:, a:a + LANES] * cos + q[:, a + MLA_QR:a + MLA_QR + LANES] * sin
        o_ref[:, a:a + LANES] = (rot * scale).astype(o_ref.dtype)
    kv0 = MLA_Q_RANK + MLA_KV_RANK
    ckv = _rms(c[:, MLA_Q_RANK:kv0], kvg_ref[...]).astype(BF16)
    kv = _dot(ckv, wukv_ref[...])
    kn0 = MLA_QN + MLA_QR
    o_ref[:, kn0:kn0 + MLA_QN] = kv[:, :MLA_QN].astype(o_ref.dtype)
    kr = c[:, kv0:kv0 + LANES] * cos + c[:, kv0 + LANES:kv0 + 2 * LANES] * sin
    o_ref[:, MLA_OUT - LANES:] = kr.astype(o_ref.dtype)
    _store_vt(vt_ref, kv[:, MLA_QN:], tk)


def _mla_rope_layout():
    half = MLA_ROPE_DIM // 2
    per = LANES // MLA_ROPE_DIM
    dq = MLA_NOPE_DIM + MLA_ROPE_DIM
    q_nope = np.concatenate([h * dq + np.arange(MLA_NOPE_DIM) for h in range(MLA_HEADS)])
    q_rope = np.zeros(MLA_QR, np.int64)
    q_swap = np.zeros(MLA_QR, np.int64)
    for h in range(MLA_HEADS):
        m, r = divmod(h, per)
        x1 = h * dq + MLA_NOPE_DIM + np.arange(half)
        x2 = x1 + half
        first = m * LANES + r * half + np.arange(half)
        second = first + LANES // 2
        q_rope[first], q_rope[second] = x1, x2
        q_swap[first], q_swap[second] = x2, x1
    k_rope = np.zeros(LANES, np.int64)
    k_swap = np.zeros(LANES, np.int64)
    base = MLA_Q_RANK + MLA_KV_RANK
    for r in range(per):
        first = r * half + np.arange(half)
        second = first + LANES // 2
        k_rope[first], k_rope[second] = base + np.arange(half), base + half + np.arange(half)
        k_swap[first], k_swap[second] = base + half + np.arange(half), base + np.arange(half)
    dkv = MLA_NOPE_DIM + HEAD_DIM
    k_nope = np.concatenate([h * dkv + np.arange(MLA_NOPE_DIM) for h in range(MLA_HEADS)])
    v = k_nope + MLA_NOPE_DIM
    down = np.concatenate([np.arange(base), k_rope, k_swap])
    return np.concatenate([q_nope, q_rope, q_swap]), np.concatenate([k_nope, v]), down


def _mla_rope_tables(seq):
    half = MLA_ROPE_DIM // 2
    inv = ROPE_THETA ** (-jnp.arange(0, MLA_ROPE_DIM, 2, dtype=F32) / MLA_ROPE_DIM)
    ang = jnp.arange(seq, dtype=F32)[:, None] * inv[None, :]
    cos, sin = jnp.cos(ang), jnp.sin(ang)
    reps = LANES // 2 // half
    cos_t = jnp.tile(cos, (1, 2 * reps))
    sin_t = jnp.concatenate([jnp.tile(-sin, (1, reps)), jnp.tile(sin, (1, reps))], axis=1)
    return jnp.concatenate([cos_t, sin_t], axis=1)


def _mla_proj(x, g, w_down, q_norm, w_uq, kv_norm, w_ukv, seq):
    t, d = x.shape
    tm = _tile(seq, 512)
    tk = _tile(seq, ATT_TK)
    ns = seq // tm
    uq_cols, ukv_cols, down_cols = _mla_rope_layout()
    wd = w_down[:, down_cols].astype(BF16)
    wuq = w_uq[:, uq_cols].astype(BF16)
    wukv = w_ukv[:, ukv_cols].astype(BF16)
    cs = _mla_rope_tables(seq)
    scale = (MLA_NOPE_DIM + MLA_ROPE_DIM) ** -0.5 * LOG2E
    full = lambda a: pl.BlockSpec(a.shape, lambda i: (0, 0))
    g2, qg, kvg = g.reshape(1, d), q_norm.reshape(1, -1), kv_norm.reshape(1, -1)
    return pl.pallas_call(
        functools.partial(_mla_proj_kernel, scale=scale, tk=tk),
        grid=(t // tm,),
        in_specs=[pl.BlockSpec((tm, d), lambda i: (i, 0)), full(g2), full(wd), full(qg), full(wuq),
                  full(kvg), full(wukv), pl.BlockSpec((tm, 2 * LANES), lambda i: (i % ns, 0))],
        out_specs=[pl.BlockSpec((tm, MLA_OUT), lambda i: (i, 0)),
                   pl.BlockSpec((tm // tk, MLA_QN, tk), lambda i: (i, 0, 0))],
        out_shape=[jax.ShapeDtypeStruct((t, MLA_OUT), BF16),
                   jax.ShapeDtypeStruct((t // tk, MLA_QN, tk), BF16)],
        compiler_params=_params("parallel"),
        name="mla_proj",
    )(x, g2, wd, qg, wuq, kvg, wukv, cs)


def _mla_kernel(qn_ref, qr_ref, kn_ref, kr_ref, vt_ref, o_ref, acc_ref, ml_ref, s_ref, *, tq, tk):
    p = pl.program_id(1)
    i = pl.program_id(2)
    q = jnp.concatenate([qn_ref[...], qr_ref[...]], axis=1)
    lane = lax.broadcasted_iota(jnp.int32, (tq, 2 * LANES), 1)
    half = MLA_ROPE_DIM // 2
    zero = jnp.zeros_like(q)
    qts = []
    for j in range(2):
        r0 = LANES + (2 * (p % 2) + j) * half
        sel = jnp.logical_and(lane >= j * HEAD_DIM, lane < (j + 1) * HEAD_DIM)
        sel = jnp.logical_or(sel, jnp.logical_and(lane >= r0, lane < r0 + half))
        r1 = r0 + LANES // 2
        sel = jnp.logical_or(sel, jnp.logical_and(lane >= r1, lane < r1 + half))
        qts.append(_transposed(jnp.where(sel, q, zero)))
    acc_ref[...] = jnp.zeros_like(acc_ref)
    row = lax.broadcasted_iota(jnp.int32, ml_ref.shape, 0)
    ml_ref[...] = jnp.where(row < 2, NEG, 0.0)

    def scores(unit, _):
        jb, h, lanes, _ = unit
        start = pl.multiple_of(jb * tk, tk)
        kt = jnp.concatenate([kn_ref[pl.ds(start, tk), :], kr_ref[pl.ds(start, tk), :]], axis=1)
        return _dot(kt, qts[h][:, lanes])

    def softmax(unit, s):
        _, h, lanes, mask = unit
        if mask is not None:
            s = jnp.where(mask, s, NEG)
        m_old = ml_ref[h:h + 1, lanes]
        m_new = jnp.maximum(m_old, jnp.max(s, axis=0, keepdims=True))
        alpha = jnp.exp2(m_old - m_new)
        e = jnp.exp2(s - m_new)
        ml_ref[h:h + 1, lanes] = m_new
        ml_ref[2 + h:3 + h, lanes] = alpha * ml_ref[2 + h:3 + h, lanes] + jnp.sum(e, axis=0, keepdims=True)
        return alpha, e.astype(BF16)

    def values(unit, val):
        jb, h, lanes, _ = unit
        alpha, e = val
        rows = slice(h * HEAD_DIM, (h + 1) * HEAD_DIM)
        acc_ref[rows, lanes] = alpha * acc_ref[rows, lanes] + _dot(vt_ref[jb, rows, :], e)

    _causal_sweep(i, tq, tk, (scores, softmax, values), False, s_ref)
    head = lax.broadcasted_iota(jnp.int32, acc_ref.shape, 0) < HEAD_DIM
    denom = jnp.where(head, ml_ref[2:3, :], ml_ref[3:4, :])
    o_ref[...] = (acc_ref[...] / denom).T.astype(o_ref.dtype)


def _mla_attention(a, vt, batch, seq):
    t = a.shape[0]
    tq = _tile(seq, ATT_TQ)
    tk = vt.shape[2]
    nq = seq // tq
    pairs = MLA_HEADS // 2
    qr0 = MLA_QN // LANES
    kn0 = (MLA_QN + MLA_QR) // LANES
    kr0 = kn0 + pairs
    return pl.pallas_call(
        functools.partial(_mla_kernel, tq=tq, tk=tk),
        grid=(batch, pairs, nq),
        in_specs=[pl.BlockSpec((tq, LANES), lambda b, p, i: (b * nq + i, p)),
                  pl.BlockSpec((tq, LANES), lambda b, p, i: (b * nq + i, qr0 + p // 2)),
                  pl.BlockSpec((seq, LANES), lambda b, p, i: (b, kn0 + p)),
                  pl.BlockSpec((seq, LANES), lambda b, p, i: (b, kr0)),
                  pl.BlockSpec((seq // tk, LANES, tk), lambda b, p, i: (b, p, 0))],
        out_specs=pl.BlockSpec((tq, LANES), lambda b, p, i: (b * nq + i, p)),
        out_shape=jax.ShapeDtypeStruct((t, MLA_HEADS * HEAD_DIM), BF16),
        scratch_shapes=[pltpu.VMEM((LANES, tq), F32), pltpu.VMEM((8, tq), F32),
                        pltpu.VMEM((2 * (tq // tk), tk, tq), F32)],
        compiler_params=_params("parallel", "parallel", "arbitrary"),
        name="mla_attention",
    )(a, a, a, a, vt)


FFN_CHUNK = 256


def _swiglu_into(acc_ref, h, wg, wu, wd):
    def gate_up(cols, _):
        return _dot(h, wg(cols)), _dot(h, wu(cols))

    def activate(cols, gu):
        g, u = gu
        return (g * jax.nn.sigmoid(g) * u).astype(BF16)

    def down(cols, a):
        acc_ref[...] += _dot(a, wd(cols))

    return gate_up, activate, down


def _run_swiglu(acc_ref, h, wg, wu, wd, width):
    chunks = [slice(c, min(c + FFN_CHUNK, width)) for c in range(0, width, FFN_CHUNK)]
    _staggered(chunks, _swiglu_into(acc_ref, h, wg, wu, wd))


def _ffn_kernel(x_ref, a_ref, wo_ref, g_ref, wg_ref, wu_ref, wd_ref, o_ref, h_ref):
    f = pl.program_id(1)

    @pl.when(f == 0)
    def _():
        x = x_ref[...] + _dot(a_ref[...], wo_ref[...])
        h_ref[...] = _rms(x, g_ref[...]).astype(BF16)
        o_ref[...] = x

    _run_swiglu(o_ref, h_ref[...], lambda c: wg_ref[:, c], lambda c: wu_ref[:, c], lambda c: wd_ref[c, :],
                wg_ref.shape[1])


def _ffn(x, a, wo, g, wg, wu, wd):
    t, d = x.shape
    fdim = wg.shape[1]
    tm = _tile(t, 512)
    tf = fdim
    return pl.pallas_call(
        _ffn_kernel,
        grid=(t // tm, fdim // tf),
        in_specs=[pl.BlockSpec((tm, d), lambda i, f: (i, 0)),
                  pl.BlockSpec((tm, a.shape[1]), lambda i, f: (i, 0)),
                  pl.BlockSpec(wo.shape, lambda i, f: (0, 0)),
                  pl.BlockSpec((1, d), lambda i, f: (0, 0)),
                  pl.BlockSpec((d, tf), lambda i, f: (0, f)),
                  pl.BlockSpec((d, tf), lambda i, f: (0, f)),
                  pl.BlockSpec((tf, d), lambda i, f: (f, 0))],
        out_specs=pl.BlockSpec((tm, d), lambda i, f: (i, 0)),
        out_shape=jax.ShapeDtypeStruct((t, d), F32),
        scratch_shapes=[pltpu.VMEM((tm, d), BF16)],
        compiler_params=_params("parallel", "arbitrary"),
        name="dense_ffn",
    )(x, a, wo, g.reshape(1, d), wg, wu, wd)


MOE_TILE = 512
META_ROWS = 8
ROUTE_SUB = 128
COMBINE_SUB = 256


def _pack_halves(x):
    half = x.shape[1] // 2
    bits = lax.bitcast_convert_type(x.astype(BF16).astype(F32), jnp.uint32)
    return bits[:, :half] | (bits[:, half:] >> 16)


def _unpack_halves(w):
    hi = lax.bitcast_convert_type(w & jnp.uint32(0xFFFF0000), F32)
    lo = lax.bitcast_convert_type(w << 16, F32)
    return jnp.concatenate([hi, lo], axis=1)


def _router_kernel(x_ref, a_ref, wo_ref, g_ref, rt_ref, su_ref, eye_ref,
                   xo_ref, hb_ref, meta_ref, metac_ref, cnt_ref, carry_ref):
    @pl.when(pl.program_id(0) == 0)
    def _():
        carry_ref[...] = jnp.zeros_like(carry_ref)

    x = x_ref[...] + _dot(a_ref[...], wo_ref[...])
    xo_ref[...] = x
    h = _rms(x, g_ref[...])
    hb_ref[...] = _pack_halves(h)
    logits = lax.dot_general(rt_ref[...], h, _NT, precision=lax.Precision.HIGHEST,
                             preferred_element_type=F32)
    ne, ct = logits.shape
    eio = lax.broadcasted_iota(jnp.int32, (ne, ct), 0).astype(F32)
    m1 = jnp.max(logits, axis=0, keepdims=True)
    i1 = jnp.min(jnp.where(logits == m1, eio, float(ne)), axis=0, keepdims=True)
    rest = jnp.where(eio == i1, -jnp.inf, logits)
    m2 = jnp.max(rest, axis=0, keepdims=True)
    i2 = jnp.min(jnp.where(rest == m2, eio, float(ne)), axis=0, keepdims=True)
    e2 = jnp.exp(m2 - m1)
    g1 = 1.0 / (1.0 + e2)
    g2 = e2 / (1.0 + e2)
    oh1 = (eio == i1).astype(F32)
    oh2 = (eio == i2).astype(F32)
    both = oh1 + oh2
    seen = carry_ref[:, 0:1] + _dot(both.astype(BF16), su_ref[...])
    r1 = jnp.sum(oh1 * seen, axis=0, keepdims=True)
    r2 = jnp.sum(oh2 * seen, axis=0, keepdims=True)
    meta = jnp.zeros((META_ROWS, ct), F32)
    for row, val in enumerate((i1, i2, r1, r2, g1, g2)):
        meta = jnp.where(eio == float(row), val, meta)
    meta_ref[...] = meta
    metac_ref[...] = lax.dot_general(eye_ref[...], meta, _NT, precision=lax.Precision.HIGHEST,
                                     preferred_element_type=F32)
    carry_ref[...] = carry_ref[...] + jnp.sum(both, axis=1, keepdims=True)
    cnt_ref[0] = carry_ref[...]


def _router(x, a, wo, g, router):
    t, d = x.shape
    ct = _tile(t, MOE_TILE)
    nc = t // ct
    su = jnp.asarray(np.triu(np.ones((ct, ct), np.float32), 1), BF16)
    eye = jnp.eye(ct, dtype=F32)
    return pl.pallas_call(
        _router_kernel,
        grid=(nc,),
        in_specs=[pl.BlockSpec((ct, d), lambda i: (i, 0)),
                  pl.BlockSpec((ct, a.shape[1]), lambda i: (i, 0)),
                  pl.BlockSpec(wo.shape, lambda i: (0, 0)),
                  pl.BlockSpec((1, d), lambda i: (0, 0)),
                  pl.BlockSpec((N_EXPERTS, d), lambda i: (0, 0)),
                  pl.BlockSpec((ct, ct), lambda i: (0, 0)),
                  pl.BlockSpec((ct, ct), lambda i: (0, 0))],
        out_specs=[pl.BlockSpec((ct, d), lambda i: (i, 0)),
                   pl.BlockSpec((ct, d // 2), lambda i: (i, 0)),
                   pl.BlockSpec((META_ROWS, ct), lambda i: (0, i)),
                   pl.BlockSpec((ct, META_ROWS), lambda i: (i, 0)),
                   pl.BlockSpec((1, N_EXPERTS, LANES), lambda i: (i, 0, 0))],
        out_shape=[jax.ShapeDtypeStruct((t, d), F32),
                   jax.ShapeDtypeStruct((t, d // 2), jnp.uint32),
                   jax.ShapeDtypeStruct((META_ROWS, t), F32),
                   jax.ShapeDtypeStruct((t, META_ROWS), F32),
                   jax.ShapeDtypeStruct((nc, N_EXPERTS, LANES), F32)],
        scratch_shapes=[pltpu.VMEM((N_EXPERTS, LANES), F32)],
        compiler_params=_params("arbitrary"),
        name="moe_router",
    )(x, a, wo, g.reshape(1, d), router.T, su, eye)


def _group_layout(cnt_after, t, rt):
    rows = 2 * t + N_EXPERTS * rt
    counts = cnt_after[:, :, 0].astype(jnp.int32)
    padded = (counts[-1] + rt - 1) // rt * rt
    off_end = jnp.cumsum(padded)
    off = (off_end - padded).astype(jnp.int32)
    tile_start = jnp.arange(rows // rt, dtype=jnp.int32) * rt
    tile_expert = jnp.minimum(_count_le(off_end, tile_start), N_EXPERTS - 1).astype(jnp.int32)
    tile_valid = (tile_start < off_end[-1]).astype(jnp.int32)
    return rows, off, counts, tile_expert, tile_valid


def _count_le(sorted_vals, x):
    return jnp.sum((sorted_vals[None, :] <= x[:, None]).astype(jnp.int32), axis=1)


def _work_lists(counts, off, rows, t, ct):
    nc = t // ct
    nrt = rows // ct
    before = jnp.concatenate([jnp.zeros((1, N_EXPERTS), jnp.int32), counts[:-1]], axis=0)
    chunk_start = (off[:, None] + before.T).reshape(-1)
    tile_start = jnp.arange(nrt, dtype=jnp.int32) * ct
    starts = jnp.sort(jnp.concatenate([chunk_start, tile_start]))
    ends = jnp.concatenate([starts[1:], jnp.array([rows], jnp.int32)])
    item_r = jnp.minimum(starts // ct, nrt - 1)
    item_c = (_count_le(chunk_start, starts) - 1) % nc
    first = lambda key: jnp.concatenate([jnp.ones((1,), jnp.int32), (key[1:] != key[:-1]).astype(jnp.int32)])
    dispatch = (item_r, item_c, starts, ends, first(item_r))
    order = jnp.argsort(item_c * (rows + 1) + starts)
    c2 = item_c[order]
    combine = (c2, item_r[order], starts[order], ends[order], first(c2))
    return dispatch, combine


def _positions(expert, rank, off_ref):
    base = jnp.zeros_like(rank)
    for e in range(N_EXPERTS):
        base = jnp.where(expert == float(e), off_ref[e].astype(F32), base)
    return base + rank


def _dispatch_kernel(r_ref, c_ref, s_ref, e_ref, first_ref, off_ref, meta_ref, h_ref, xs_ref, gate_ref):
    n = pl.program_id(0)
    rt, ct = xs_ref.shape[0], h_ref.shape[0]

    @pl.when(first_ref[n] == 1)
    def _():
        xs_ref[...] = jnp.zeros_like(xs_ref)
        gate_ref[...] = jnp.zeros_like(gate_ref)

    meta = meta_ref[...]
    pos0 = _positions(meta[0:1], meta[2:3], off_ref)
    pos1 = _positions(meta[1:2], meta[3:4], off_ref)
    sub = min(rt, ROUTE_SUB)
    for sb in range(rt // sub):
        base = r_ref[n] * rt + sb * sub

        @pl.when(jnp.logical_and(base < e_ref[n], base + sub > s_ref[n]))
        def _():
            row = lax.broadcasted_iota(jnp.int32, (sub, ct), 0) + base
            live = jnp.logical_and(row >= s_ref[n], row < e_ref[n])
            rowf = row.astype(F32)
            sel0 = jnp.logical_and(pos0 == rowf, live)
            sel1 = jnp.logical_and(pos1 == rowf, live)
            sel = jnp.logical_or(sel0, sel1).astype(F32).astype(BF16)
            rows = slice(sb * sub, (sb + 1) * sub)
            xs_ref[rows, :] += _pack_halves(_dot(sel, _unpack_halves(h_ref[...]).astype(BF16)))
            gate = jnp.where(sel0, meta[4:5], 0.0) + jnp.where(sel1, meta[5:6], 0.0)
            gate_ref[rows, :] += jnp.sum(gate, axis=1, keepdims=True)


def _dispatch(plan, off, meta, hb, rows, ct):
    t, d = hb.shape
    n_items = plan[0].shape[0]
    return pl.pallas_call(
        _dispatch_kernel,
        grid_spec=pltpu.PrefetchScalarGridSpec(
            num_scalar_prefetch=6,
            grid=(n_items,),
            in_specs=[pl.BlockSpec((META_ROWS, ct), lambda n, r, c, *_: (0, c[n])),
                      pl.BlockSpec((ct, d), lambda n, r, c, *_: (c[n], 0))],
            out_specs=[pl.BlockSpec((ct, d), lambda n, r, c, *_: (r[n], 0)),
                       pl.BlockSpec((ct, 1), lambda n, r, c, *_: (r[n], 0))]),
        out_shape=[jax.ShapeDtypeStruct((rows, d), jnp.uint32),
                   jax.ShapeDtypeStruct((rows, 1), F32)],
        compiler_params=_params("arbitrary"),
        name="moe_dispatch",
    )(*plan, off, meta, hb)


def _expert_kernel(te_ref, tv_ref, xs_ref, gate_ref, wg_ref, wu_ref, wd_ref, y_ref, acc_ref):
    r = pl.program_id(0)
    f = pl.program_id(1)

    @pl.when(f == 0)
    def _():
        acc_ref[...] = jnp.zeros_like(acc_ref)

    @pl.when(tv_ref[r] == 1)
    def _():
        _run_swiglu(acc_ref, _unpack_halves(xs_ref[...]).astype(BF16), lambda c: wg_ref[0, :, c],
                    lambda c: wu_ref[0, :, c], lambda c: wd_ref[0, c, :], wg_ref.shape[2])

    @pl.when(f == pl.num_programs(1) - 1)
    def _():
        y_ref[...] = _pack_halves(acc_ref[...] * gate_ref[...])


def _experts(tile_expert, tile_valid, xs, gate, wg, wu, wd, ct):
    rows, dp = xs.shape
    d = 2 * dp
    fdim = wg.shape[2]
    tf = _tile(fdim, 1792)
    nf = fdim // tf

    def wcol(r, f, te, tv):
        return (te[r], 0, jnp.where(tv[r] == 1, f, nf - 1))

    def wrow(r, f, te, tv):
        return (te[r], jnp.where(tv[r] == 1, f, nf - 1), 0)

    return pl.pallas_call(
        _expert_kernel,
        grid_spec=pltpu.PrefetchScalarGridSpec(
            num_scalar_prefetch=2,
            grid=(rows // ct, nf),
            in_specs=[pl.BlockSpec((ct, dp), lambda r, f, te, tv: (r, 0)),
                      pl.BlockSpec((ct, 1), lambda r, f, te, tv: (r, 0)),
                      pl.BlockSpec((1, d, tf), wcol),
                      pl.BlockSpec((1, d, tf), wcol),
                      pl.BlockSpec((1, tf, d), wrow)],
            out_specs=pl.BlockSpec((ct, dp), lambda r, f, te, tv: (r, 0)),
            scratch_shapes=[pltpu.VMEM((ct, d), F32)]),
        out_shape=jax.ShapeDtypeStruct((rows, dp), jnp.uint32),
        compiler_params=_params("arbitrary", "arbitrary"),
        name="moe_experts",
    )(tile_expert, tile_valid, xs, gate, wg, wu, wd)


def _combine_kernel(c_ref, r_ref, s_ref, e_ref, first_ref, off_ref, x_ref, metac_ref, y_ref, o_ref):
    n = pl.program_id(0)
    ct, rt = x_ref.shape[0], y_ref.shape[0]

    @pl.when(first_ref[n] == 1)
    def _():
        o_ref[...] = x_ref[...]

    metac = metac_ref[...]
    pos0 = _positions(metac[:, 0:1], metac[:, 2:3], off_ref)
    pos1 = _positions(metac[:, 1:2], metac[:, 3:4], off_ref)
    sub = min(rt, COMBINE_SUB)
    for sb in range(rt // sub):
        base = r_ref[n] * rt + sb * sub

        @pl.when(jnp.logical_and(base < e_ref[n], base + sub > s_ref[n]))
        def _():
            row = lax.broadcasted_iota(jnp.int32, (ct, sub), 1) + base
            live = jnp.logical_and(row >= s_ref[n], row < e_ref[n])
            rowf = row.astype(F32)
            sel = jnp.logical_and(jnp.logical_or(pos0 == rowf, pos1 == rowf), live).astype(F32).astype(BF16)
            o_ref[...] += _dot(sel, _unpack_halves(y_ref[sb * sub:(sb + 1) * sub, :]).astype(BF16))


def _combine(plan, off, x, metac, y, ct):
    t, d = x.shape
    n_items = plan[0].shape[0]
    return pl.pallas_call(
        _combine_kernel,
        grid_spec=pltpu.PrefetchScalarGridSpec(
            num_scalar_prefetch=6,
            grid=(n_items,),
            in_specs=[pl.BlockSpec((ct, d), lambda n, c, r, *_: (c[n], 0)),
                      pl.BlockSpec((ct, META_ROWS), lambda n, c, r, *_: (c[n], 0)),
                      pl.BlockSpec((ct, d // 2), lambda n, c, r, *_: (r[n], 0))],
            out_specs=pl.BlockSpec((ct, d), lambda n, c, r, *_: (c[n], 0))),
        out_shape=jax.ShapeDtypeStruct((t, d), F32),
        compiler_params=_params("arbitrary"),
        name="moe_combine",
    )(*plan, off, x, metac, y)


def _cast_kernel(w_ref, o_ref):
    o_ref[...] = w_ref[...].astype(o_ref.dtype)


CAST_BLOCK_BYTES = 8 * 1024 * 1024


def _layer_bf16(w, layer):
    shape = w.shape[1:]
    cols = shape[-1]
    rows = math.prod(shape[:-1])
    tr = rows
    while tr * cols * 4 > CAST_BLOCK_BYTES and tr % 16 == 0:
        tr //= 2
    out = pl.pallas_call(
        _cast_kernel,
        grid=(rows // tr,),
        in_specs=[pl.BlockSpec((None, tr, cols), lambda i: (layer, i, 0))],
        out_specs=pl.BlockSpec((tr, cols), lambda i: (i, 0)),
        out_shape=jax.ShapeDtypeStruct((rows, cols), BF16),
        compiler_params=_params("parallel"),
        name="weights_bf16",
    )(w.reshape(w.shape[0], rows, cols))
    return out.reshape(shape)


SC_CORES = 2
SC_SUBCORES = 16
SC_CHUNK = 128


def _sc_worker_chunks(total):
    per = total // (SC_CORES * SC_SUBCORES)
    wid = lax.axis_index("s") * SC_CORES + lax.axis_index("c")
    return wid * per, per // SC_CHUNK


def _sc_mesh():
    return plsc.VectorSubcoreMesh(core_axis_name="c", subcore_axis_name="s")


def _sc_gather(table, idx):
    v, d = table.shape
    b = idx.shape[0]

    def body(table_hbm, idx_hbm, out_hbm, idx_v, rows_v, sem):
        base, chunks = _sc_worker_chunks(b)

        @pl.loop(0, chunks)
        def _(j):
            start = base + j * SC_CHUNK
            pltpu.sync_copy(idx_hbm.at[pl.ds(start, SC_CHUNK)], idx_v)
            pltpu.async_copy(table_hbm.at[idx_v], rows_v, sem).wait()
            pltpu.sync_copy(rows_v, out_hbm.at[pl.ds(start, SC_CHUNK)])

    return pl.kernel(
        body,
        out_type=jax.ShapeDtypeStruct((b, d), table.dtype),
        mesh=_sc_mesh(),
        scratch_types=[pltpu.VMEM((SC_CHUNK,), jnp.int32),
                       pltpu.VMEM((SC_CHUNK, d), table.dtype),
                       pltpu.SemaphoreType.DMA],
        name="sc_row_gather",
    )(table, idx)


def _sc_scatter_rows(src, idx, n):
    t, d = src.shape
    b = idx.shape[0]

    def body(src_hbm, idx_hbm, out_hbm, idx_v, rows_v):
        base, chunks = _sc_worker_chunks(b)

        @pl.loop(0, chunks)
        def _(j):
            start = base + j * SC_CHUNK
            first = lax.rem(start, t)
            pltpu.sync_copy(idx_hbm.at[pl.ds(start, SC_CHUNK)], idx_v)
            pltpu.sync_copy(src_hbm.at[pl.ds(first, SC_CHUNK)], rows_v)
            pltpu.sync_copy(rows_v, out_hbm.at[idx_v])

    return pl.kernel(
        body,
        out_type=jax.ShapeDtypeStruct((n, d), src.dtype),
        mesh=_sc_mesh(),
        scratch_types=[pltpu.VMEM((SC_CHUNK,), jnp.int32), pltpu.VMEM((SC_CHUNK, d), src.dtype)],
        name="sc_row_scatter",
    )(src, idx)


def _weighted_add_kernel(x_ref, metac_ref, y0_ref, y1_ref, g_ref, o_ref, *, normed):
    gates = metac_ref[...]
    out = (x_ref[...] + gates[:, 4:5] * _unpack_halves(y0_ref[...])
           + gates[:, 5:6] * _unpack_halves(y1_ref[...]))
    o_ref[...] = _rms(out, g_ref[...]) if normed else out


def _weighted_add(x, metac, ysel, out_gain=None):
    t, d = x.shape
    tm = _tile(t, 512)
    nt = t // tm
    gain = jnp.ones((d,), F32) if out_gain is None else out_gain
    return pl.pallas_call(
        functools.partial(_weighted_add_kernel, normed=out_gain is not None),
        grid=(nt,),
        in_specs=[pl.BlockSpec((tm, d), lambda i: (i, 0)),
                  pl.BlockSpec((tm, META_ROWS), lambda i: (i, 0)),
                  pl.BlockSpec((tm, d // 2), lambda i: (i, 0)),
                  pl.BlockSpec((tm, d // 2), lambda i: (nt + i, 0)),
                  pl.BlockSpec((1, d), lambda i: (0, 0))],
        out_specs=pl.BlockSpec((tm, d), lambda i: (i, 0)),
        out_shape=jax.ShapeDtypeStruct((t, d), F32),
        compiler_params=_params("parallel"),
        name="moe_weighted_add",
    )(x, metac, ysel, ysel, gain.reshape(1, d))


def _moe(x, a, wo, g, router, wg, wu, wd, out_gain=None):
    t = x.shape[0]
    ct = _tile(t, MOE_TILE)
    x, hb, meta, metac, cnt = _router(x, a, wo, g, router)
    sc_rows = SC_CORES * SC_SUBCORES * SC_CHUNK
    rows, off, counts, tile_expert, tile_valid = _group_layout(cnt, t, ct)
    if (2 * t) % sc_rows == 0 and rows % sc_rows == 0:
        base = sum(jnp.where(meta[0:2] == float(e), off[e], 0) for e in range(N_EXPERTS))
        pos = (base + meta[2:4].astype(jnp.int32)).reshape(-1)
        xs = _sc_scatter_rows(hb, pos, rows)
        y = _experts(tile_expert, tile_valid, xs, jnp.ones((rows, 1), F32), wg, wu, wd, ct)
        return _weighted_add(x, metac, _sc_gather(y, pos), out_gain)
    dispatch, combine = _work_lists(counts, off, rows, t, ct)
    xs, gate = _dispatch(dispatch, off, meta, hb, rows, ct)
    y = _experts(tile_expert, tile_valid, xs, gate, wg, wu, wd, ct)
    out = _combine(combine, off, x, metac, y, ct)
    return out if out_gain is None else _final_norm(out, out_gain)


def _final_norm_kernel(x_ref, g_ref, o_ref):
    o_ref[...] = _rms(x_ref[...], g_ref[...])


def _final_norm(x, g):
    t, d = x.shape
    tm = _tile(t, 1024)
    return pl.pallas_call(
        _final_norm_kernel,
        grid=(t // tm,),
        in_specs=[pl.BlockSpec((tm, d), lambda i: (i, 0)), pl.BlockSpec((1, d), lambda i: (0, 0))],
        out_specs=pl.BlockSpec((tm, d), lambda i: (i, 0)),
        out_shape=jax.ShapeDtypeStruct((t, d), F32),
        compiler_params=_params("parallel"),
        name="final_norm",
    )(x, g.reshape(1, d))


def kernel(x, rel_bias, attn_norm, ffn_norm, final_norm, swa_w_qkv, swa_b_qkv, swa_sinks, swa_w_o, sb_w_qkv, sb_w_o, mla_w_down, mla_q_norm, mla_w_uq, mla_kv_norm, mla_w_ukv, mla_w_o, ffn_w_gate, ffn_w_up, ffn_w_down, moe_router, moe_w_gate, moe_w_up, moe_w_down):
    batch, seq, d = x.shape
    depth = attn_norm.shape[0]
    xt = x.reshape(batch * seq, d)
    swa_bias = _swa_bias(rel_bias)
    for i in range(depth):
        mixer, j = i % N_MIXERS, i // N_MIXERS
        if mixer == 0:
            w, b, cs = _swa_weights(swa_w_qkv[j], swa_b_qkv[j])
            qkv = _norm_proj(xt, attn_norm[i], w, b, cs, "swa_proj")
            o = _swa_attention(qkv, swa_bias, swa_sinks[j], batch, seq)
            wo = swa_w_o[j].astype(BF16)
        elif mixer == 1:
            n = sb_w_qkv.shape[2] // 3
            cs = jnp.concatenate([jnp.full((n,), HEAD_DIM ** -0.5 * LOG2E, F32), jnp.ones((n,), F32)])
            qk, vt = _norm_proj_vt(xt, attn_norm[i], sb_w_qkv[j].astype(BF16), cs, n, seq, "sb_proj")
            o = _sb_attention(qk, vt, batch, seq)
            wo = sb_w_o[j].astype(BF16)
        else:
            a, vt = _mla_proj(xt, attn_norm[i], mla_w_down[j], mla_q_norm[j], mla_w_uq[j],
                              mla_kv_norm[j], mla_w_ukv[j], seq)
            o = _mla_attention(a, vt, batch, seq)
            wo = mla_w_o[j].astype(BF16)
        f = i // 2
        if i % 2 == 0:
            xt = _ffn(xt, o, wo, ffn_norm[i], _layer_bf16(ffn_w_gate, f), _layer_bf16(ffn_w_up, f),
                      _layer_bf16(ffn_w_down, f))
        else:
            last = final_norm if i == depth - 1 else None
            xt = _moe(xt, o, wo, ffn_norm[i], moe_router[f], _layer_bf16(moe_w_gate, f),
                      _layer_bf16(moe_w_up, f), _layer_bf16(moe_w_down, f), last)
    if depth % 2 == 1:
        xt = _final_norm(xt, final_norm)
    return xt.reshape(batch, seq, d)
```

```python
import functools
import math

import numpy as np
import jax
import jax.numpy as jnp
from jax import lax
from jax.experimental import pallas as pl
from jax.experimental.pallas import tpu as pltpu
from jax.experimental.pallas import tpu_sc as plsc

F32 = jnp.float32
BF16 = jnp.bfloat16

N_MIXERS = 3
RMS_EPS = 1e-6
SWA_HEADS = 16
SWA_KV_HEADS = 2
HEAD_DIM = 64
SWA_BLOCK = 128
SWA_WINDOW = 128
REL_BUCKETS = 32
REL_MAX_DIST = 128
SB_HEADS = 16
MLA_HEADS = 16
MLA_NOPE_DIM = 64
MLA_ROPE_DIM = 32
MLA_Q_RANK = 384
MLA_KV_RANK = 256
ROPE_THETA = 10000.0
N_EXPERTS = 8

LANES = 128
NEG = -1e30
LOG2E = math.log2(math.e)
VMEM_LIMIT = 56 * 1024 * 1024

_NT = (((1,), (1,)), ((), ()))


def _dot(a, b):
    return jnp.dot(a, b, preferred_element_type=F32)


def _dot_nt(a, b):
    return lax.dot_general(a, b, _NT, preferred_element_type=F32)


def _rms(x, g):
    return x * lax.rsqrt(jnp.mean(x * x, axis=-1, keepdims=True) + RMS_EPS) * g


def _params(*sem):
    return pltpu.CompilerParams(dimension_semantics=sem, vmem_limit_bytes=VMEM_LIMIT)


def _tile(n, pref):
    return pref if n % pref == 0 else n


def _norm_proj_kernel(x_ref, g_ref, w_ref, b_ref, cs_ref, o_ref):
    h = _rms(x_ref[...], g_ref[...]).astype(BF16)
    y = (_dot(h, w_ref[...]) + b_ref[...]) * cs_ref[...]
    o_ref[...] = y.astype(o_ref.dtype)


def _norm_proj(x, g, w, b, colscale, name):
    t, d = x.shape
    n = w.shape[1]
    tm = _tile(t, 512)
    return pl.pallas_call(
        _norm_proj_kernel,
        grid=(t // tm,),
        in_specs=[pl.BlockSpec((tm, d), lambda i: (i, 0)),
                  pl.BlockSpec((1, d), lambda i: (0, 0)),
                  pl.BlockSpec((d, n), lambda i: (0, 0)),
                  pl.BlockSpec((1, n), lambda i: (0, 0)),
                  pl.BlockSpec((1, n), lambda i: (0, 0))],
        out_specs=pl.BlockSpec((tm, n), lambda i: (i, 0)),
        out_shape=jax.ShapeDtypeStruct((t, n), BF16),
        compiler_params=_params("parallel"),
        name=name,
    )(x, g.reshape(1, d), w, b.reshape(1, n), colscale.reshape(1, n))


ATT_TQ = 512
ATT_TK = 256


def _store_vt(vt_ref, v, tk):
    for s in range(v.shape[0] // tk):
        vt_ref[s] = v[s * tk:(s + 1) * tk, :].T.astype(vt_ref.dtype)


def _norm_proj_vt_kernel(x_ref, g_ref, w_ref, cs_ref, o_ref, vt_ref, *, tk):
    h = _rms(x_ref[...], g_ref[...]).astype(BF16)
    y = _dot(h, w_ref[...])
    n = o_ref.shape[1]
    o_ref[...] = (y[:, :n] * cs_ref[...]).astype(o_ref.dtype)
    _store_vt(vt_ref, y[:, n:], tk)


def _norm_proj_vt(x, g, w, colscale, nv, seq, name):
    t, d = x.shape
    n = w.shape[1] - nv
    tm = _tile(seq, 512)
    tk = _tile(seq, ATT_TK)
    per = tm // tk
    return pl.pallas_call(
        functools.partial(_norm_proj_vt_kernel, tk=tk),
        grid=(t // tm,),
        in_specs=[pl.BlockSpec((tm, d), lambda i: (i, 0)),
                  pl.BlockSpec((1, d), lambda i: (0, 0)),
                  pl.BlockSpec((d, n + nv), lambda i: (0, 0)),
                  pl.BlockSpec((1, n), lambda i: (0, 0))],
        out_specs=[pl.BlockSpec((tm, n), lambda i: (i, 0)),
                   pl.BlockSpec((per, nv, tk), lambda i: (i, 0, 0))],
        out_shape=[jax.ShapeDtypeStruct((t, n), BF16),
                   jax.ShapeDtypeStruct((t // tk, nv, tk), BF16)],
        compiler_params=_params("parallel"),
        name=name,
    )(x, g.reshape(1, d), w, colscale.reshape(1, n))


def _out_proj_kernel(x_ref, o_ref, w_ref, y_ref):
    y_ref[...] = x_ref[...] + _dot(o_ref[...], w_ref[...])


def _out_proj(x, o, w, name):
    t, d = x.shape
    k = o.shape[1]
    tm = _tile(t, 512)
    return pl.pallas_call(
        _out_proj_kernel,
        grid=(t // tm,),
        in_specs=[pl.BlockSpec((tm, d), lambda i: (i, 0)),
                  pl.BlockSpec((tm, k), lambda i: (i, 0)),
                  pl.BlockSpec((k, d), lambda i: (0, 0))],
        out_specs=pl.BlockSpec((tm, d), lambda i: (i, 0)),
        out_shape=jax.ShapeDtypeStruct((t, d), F32),
        compiler_params=_params("parallel"),
        name=name,
    )(x, o, w)


def _t5_bucket_table():
    qi = np.arange(SWA_BLOCK)[:, None]
    kj = np.arange(2 * SWA_BLOCK)[None, :]
    dist = qi + SWA_BLOCK - kj
    d0 = np.maximum(dist, 0)
    max_exact = REL_BUCKETS // 2
    d = np.maximum(d0, 1).astype(np.float32)
    large = max_exact + (np.log(d / max_exact) / math.log(REL_MAX_DIST / max_exact)
                         * (REL_BUCKETS - max_exact)).astype(np.int32)
    large = np.minimum(large, REL_BUCKETS - 1)
    bucket = np.where(d0 < max_exact, d0, large)
    band = (dist >= 0) & (dist < SWA_WINDOW)
    return np.where(band, bucket, -1).astype(np.int32)


def _swa_bias_kernel(rel_ref, bucket_ref, o_ref):
    h = pl.program_id(0)
    bucket = bucket_ref[...]
    acc = jnp.full(bucket.shape, NEG, F32)
    for b in range(REL_BUCKETS):
        acc = jnp.where(bucket == b, rel_ref[b, h], acc)
    o_ref[0] = acc


def _swa_bias(rel_bias):
    bucket = jnp.asarray(_t5_bucket_table())
    q, k = bucket.shape
    return pl.pallas_call(
        _swa_bias_kernel,
        grid=(SWA_HEADS,),
        in_specs=[pl.BlockSpec(memory_space=pltpu.SMEM),
                  pl.BlockSpec((q, k), lambda h: (0, 0))],
        out_specs=pl.BlockSpec((1, q, k), lambda h: (h, 0, 0)),
        out_shape=jax.ShapeDtypeStruct((SWA_HEADS, q, k), F32),
        compiler_params=_params("arbitrary"),
        name="swa_bias",
    )(rel_bias, bucket)


def _swa_kernel(sink_ref, q_ref, kc_ref, kp_ref, vc_ref, vp_ref, bias_ref, o_ref, *, tq):
    i = pl.program_id(1)
    g = pl.program_id(2)
    blk = SWA_BLOCK
    heads = bias_ref.shape[0]
    lo_half = lax.broadcasted_iota(jnp.int32, (blk, LANES), 1) < HEAD_DIM
    in_prev = lax.broadcasted_iota(jnp.int32, (heads * blk, 2 * blk), 1) < blk
    bias = bias_ref[...].reshape(heads * blk, 2 * blk)

    def window(cur_ref, prev_ref, sb):
        if sb == 0:
            return jnp.concatenate([prev_ref[...], cur_ref[0:blk, :]], axis=0)
        return cur_ref[(sb - 1) * blk:(sb + 1) * blk, :]

    def scores(sb, _):
        q = q_ref[sb * blk:(sb + 1) * blk, :]
        parts = []
        for h in range(heads):
            qp = q[:, (h // 2) * LANES:(h // 2 + 1) * LANES]
            keep = lo_half if h % 2 == 0 else jnp.logical_not(lo_half)
            parts.append(jnp.where(keep, qp, jnp.zeros_like(qp)))
        s = _dot_nt(jnp.concatenate(parts, axis=0), window(kc_ref, kp_ref, sb)) + bias
        if sb == 0:
            s = jnp.where(jnp.logical_and(in_prev, i == 0), NEG, s)
        return s

    def softmax(sb, s):
        ps, inv = [], []
        for h in range(heads):
            sh = s[h * blk:(h + 1) * blk]
            sink = sink_ref[g * heads + h]
            m = jnp.maximum(jnp.max(sh, axis=1, keepdims=True), sink)
            e = jnp.exp(sh - m)
            inv.append(1.0 / (jnp.sum(e, axis=1, keepdims=True) + jnp.exp(sink - m)))
            ps.append(e.astype(BF16))
        return jnp.concatenate(ps, axis=0), inv

    def values(sb, val):
        p, inv = val
        o = _dot(p, window(vc_ref, vp_ref, sb))
        outs = [jnp.where(lo_half, o[h * blk:(h + 1) * blk] * inv[h], o[(h + 1) * blk:(h + 2) * blk] * inv[h + 1])
                for h in range(0, heads, 2)]
        o_ref[sb * blk:(sb + 1) * blk, :] = jnp.concatenate(outs, axis=1).astype(o_ref.dtype)

    _staggered(list(range(tq // blk)), (scores, softmax, values))


def _swa_attention(qkv, bias, sinks, batch, seq):
    t = qkv.shape[0]
    tq = _tile(seq, 512)
    nt = seq // tq
    group = SWA_HEADS // SWA_KV_HEADS
    gw = group * HEAD_DIM
    kcol = SWA_HEADS * HEAD_DIM // LANES
    vcol = kcol + SWA_KV_HEADS
    per = tq // SWA_BLOCK

    def prev_idx(b, i, g):
        return jnp.maximum(b * (seq // SWA_BLOCK) + i * per - 1, 0)

    return pl.pallas_call(
        functools.partial(_swa_kernel, tq=tq),
        grid=(batch, nt, SWA_KV_HEADS),
        in_specs=[pl.BlockSpec(memory_space=pltpu.SMEM),
                  pl.BlockSpec((tq, gw), lambda b, i, g: (b * nt + i, g)),
                  pl.BlockSpec((tq, LANES), lambda b, i, g: (b * nt + i, kcol + g)),
                  pl.BlockSpec((SWA_BLOCK, LANES), lambda b, i, g: (prev_idx(b, i, g), kcol + g)),
                  pl.BlockSpec((tq, LANES), lambda b, i, g: (b * nt + i, vcol + g)),
                  pl.BlockSpec((SWA_BLOCK, LANES), lambda b, i, g: (prev_idx(b, i, g), vcol + g)),
                  pl.BlockSpec((group, SWA_BLOCK, 2 * SWA_BLOCK), lambda b, i, g: (g, 0, 0))],
        out_specs=pl.BlockSpec((tq, gw), lambda b, i, g: (b * nt + i, g)),
        out_shape=jax.ShapeDtypeStruct((t, SWA_HEADS * HEAD_DIM), BF16),
        compiler_params=_params("parallel", "parallel", "arbitrary"),
        name="swa_attention",
    )(sinks, qkv, qkv, qkv, qkv, qkv, bias)


def _swa_weights(w_qkv, b_qkv):
    nq = SWA_HEADS * HEAD_DIM
    kv = SWA_KV_HEADS * HEAD_DIM
    dup = np.concatenate([np.tile(np.arange(HEAD_DIM), 2) + h * HEAD_DIM for h in range(SWA_KV_HEADS)])
    cols = np.concatenate([np.arange(nq), nq + dup, nq + kv + dup])
    scale = np.concatenate([np.full(nq, HEAD_DIM ** -0.5), np.ones(2 * dup.size)]).astype(np.float32)
    return w_qkv[:, cols].astype(BF16), b_qkv[cols], jnp.asarray(scale)


def _staggered(units, stages):
    vals = list(units)
    for t in range(len(units) + len(stages) - 1):
        for k, stage in enumerate(stages):
            u = t - k
            if 0 <= u < len(units):
                vals[u] = stage(units[u], vals[u])


def _score_ahead(units, next_units, stages, s_ref):
    rest = stages[1:]
    vals = [None] * len(units)
    for t in range(len(units) + len(rest) - 1):
        ahead = stages[0](next_units[t], None) if t < len(units) else None
        for k, stage in enumerate(rest):
            u = t - k
            if 0 <= u < len(units):
                if k == 0:
                    lanes = units[u][2]
                    vals[u] = stage(units[u], s_ref[u, :, :lanes.stop - lanes.start])
                    s_ref[u] = ahead
                else:
                    vals[u] = stage(units[u], vals[u])


def _causal_sweep(i, tq, tk, stages, strict, s_ref):
    nsub = tq // tk

    def group(j):
        first = jnp.maximum(j, 0) * nsub
        return [(first + sb, h, slice(0, tq), None) for sb in reversed(range(nsub)) for h in range(2)]

    diagonal = []
    for sb in reversed(range(nsub)):
        nl = tq - sb * tk
        key = lax.broadcasted_iota(jnp.int32, (tk, nl), 0)
        qry = lax.broadcasted_iota(jnp.int32, (tk, nl), 1)
        mask = key < qry if strict else key <= qry
        diagonal += [(i * nsub + sb, h, slice(sb * tk, tq), mask) for h in range(2)]
    for u, unit in enumerate(diagonal):
        s_ref[u, :, :tq - unit[2].start] = stages[0](unit, None)
    _score_ahead(diagonal, group(i - 1), stages, s_ref)

    def body(n, carry):
        _score_ahead(group(i - 1 - n), group(i - 2 - n), stages, s_ref)
        return carry

    lax.fori_loop(0, i, body, 0)


def _transposed(x):
    return x.astype(F32).T.astype(x.dtype)


def _split_pair(q):
    lo_half = lax.broadcasted_iota(jnp.int32, q.shape, 1) < HEAD_DIM
    zero = jnp.zeros_like(q)
    return jnp.where(lo_half, q, zero), jnp.where(lo_half, zero, q)


def _sb_kernel(q_ref, k_ref, vt_ref, u_ref, o_ref, acc_ref, c_ref, s_ref, *, tq, tk):
    i = pl.program_id(2)
    qts = [_transposed(qm) for qm in _split_pair(q_ref[...])]
    u = u_ref[...]
    acc_ref[...] = jnp.zeros_like(acc_ref)
    c_ref[...] = jnp.zeros_like(c_ref)

    def scores(unit, _):
        jb, h, lanes, _ = unit
        kt = k_ref[pl.ds(pl.multiple_of(jb * tk, tk), tk), :]
        return _dot(kt, qts[h][:, lanes])

    def log_fail(unit, z):
        neg_abs = lax.bitcast_convert_type(lax.bitcast_convert_type(z, jnp.uint32) | jnp.uint32(1 << 31), F32)
        sp = jnp.maximum(z, 0.0) + jnp.log2(1.0 + jnp.exp2(neg_abs))
        if unit[3] is not None:
            sp = jnp.where(unit[3], sp, 0.0)
        return z, sp.astype(BF16)

    def suffix_sum(unit, val):
        z, sp = val
        return z, _dot(u, sp)

    def weights(unit, val):
        _, h, lanes, mask = unit
        z, r = val
        c = c_ref[h:h + 1, lanes]
        la = z - r - c
        if mask is not None:
            la = jnp.where(mask, la, NEG)
        c_ref[h:h + 1, lanes] = c + r[0:1, :]
        return jnp.exp2(la).astype(BF16)

    def values(unit, a):
        jb, h, lanes, _ = unit
        rows = slice(h * HEAD_DIM, (h + 1) * HEAD_DIM)
        acc_ref[rows, lanes] += _dot(vt_ref[jb, rows, :], a)

    _causal_sweep(i, tq, tk, (scores, log_fail, suffix_sum, weights, values), True, s_ref)
    o_ref[...] = acc_ref[...].T.astype(o_ref.dtype)


def _sb_attention(qk, vt, batch, seq):
    t = qk.shape[0]
    tq = _tile(seq, ATT_TQ)
    tk = vt.shape[2]
    nq = seq // tq
    pairs = SB_HEADS // 2
    u = jnp.asarray(np.triu(np.ones((tk, tk), np.float32)), BF16)
    return pl.pallas_call(
        functools.partial(_sb_kernel, tq=tq, tk=tk),
        grid=(batch, pairs, nq),
        in_specs=[pl.BlockSpec((tq, LANES), lambda b, p, i: (b * nq + i, p)),
                  pl.BlockSpec((seq, LANES), lambda b, p, i: (b, pairs + p)),
                  pl.BlockSpec((seq // tk, LANES, tk), lambda b, p, i: (b, p, 0)),
                  pl.BlockSpec((tk, tk), lambda b, p, i: (0, 0))],
        out_specs=pl.BlockSpec((tq, LANES), lambda b, p, i: (b * nq + i, p)),
        out_shape=jax.ShapeDtypeStruct((t, SB_HEADS * HEAD_DIM), BF16),
        scratch_shapes=[pltpu.VMEM((LANES, tq), F32), pltpu.VMEM((8, tq), F32),
                        pltpu.VMEM((2 * (tq // tk), tk, tq), F32)],
        compiler_params=_params("parallel", "parallel", "arbitrary"),
        name="sb_attention",
    )(qk, qk, vt, u)


MLA_QN = MLA_HEADS * MLA_NOPE_DIM
MLA_QR = MLA_HEADS * MLA_ROPE_DIM
MLA_OUT = 2 * MLA_QN + MLA_QR + LANES


def _mla_proj_kernel(x_ref, g_ref, wd_ref, qg_ref, wuq_ref, kvg_ref, wukv_ref, cs_ref, o_ref, vt_ref,
                     *, scale, tk):
    h = _rms(x_ref[...], g_ref[...]).astype(BF16)
    c = _dot(h, wd_ref[...])
    cos = cs_ref[:, :LANES]
    sin = cs_ref[:, LANES:]
    cq = _rms(c[:, :MLA_Q_RANK], qg_ref[...]).astype(BF16)
    q = _dot(cq, wuq_ref[...])
    o_ref[:, :MLA_QN] = (q[:, :MLA_QN] * scale).astype(o_ref.dtype)
    for m in range(MLA_QR // LANES):
        a = MLA_QN + m * LANES
        rot = q[:, a:a + LANES] * cos + q[:, a + MLA_QR:a + MLA_QR + LANES] * sin
        o_ref[:, a:a + LANES] = (rot * scale).astype(o_ref.dtype)
    kv0 = MLA_Q_RANK + MLA_KV_RANK
    ckv = _rms(c[:, MLA_Q_RANK:kv0], kvg_ref[...]).astype(BF16)
    kv = _dot(ckv, wukv_ref[...])
    kn0 = MLA_QN + MLA_QR
    o_ref[:, kn0:kn0 + MLA_QN] = kv[:, :MLA_QN].astype(o_ref.dtype)
    kr = c[:, kv0:kv0 + LANES] * cos + c[:, kv0 + LANES:kv0 + 2 * LANES] * sin
    o_ref[:, MLA_OUT - LANES:] = kr.astype(o_ref.dtype)
    _store_vt(vt_ref, kv[:, MLA_QN:], tk)


def _mla_rope_layout():
    half = MLA_ROPE_DIM // 2
    per = LANES // MLA_ROPE_DIM
    dq = MLA_NOPE_DIM + MLA_ROPE_DIM
    q_nope = np.concatenate([h * dq + np.arange(MLA_NOPE_DIM) for h in range(MLA_HEADS)])
    q_rope = np.zeros(MLA_QR, np.int64)
    q_swap = np.zeros(MLA_QR, np.int64)
    for h in range(MLA_HEADS):
        m, r = divmod(h, per)
        x1 = h * dq + MLA_NOPE_DIM + np.arange(half)
        x2 = x1 + half
        first = m * LANES + r * half + np.arange(half)
        second = first + LANES // 2
        q_rope[first], q_rope[second] = x1, x2
        q_swap[first], q_swap[second] = x2, x1
    k_rope = np.zeros(LANES, np.int64)
    k_swap = np.zeros(LANES, np.int64)
    base = MLA_Q_RANK + MLA_KV_RANK
    for r in range(per):
        first = r * half + np.arange(half)
        second = first + LANES // 2
        k_rope[first], k_rope[second] = base + np.arange(half), base + half + np.arange(half)
        k_swap[first], k_swap[second] = base + half + np.arange(half), base + np.arange(half)
    dkv = MLA_NOPE_DIM + HEAD_DIM
    k_nope = np.concatenate([h * dkv + np.arange(MLA_NOPE_DIM) for h in range(MLA_HEADS)])
    v = k_nope + MLA_NOPE_DIM
    down = np.concatenate([np.arange(base), k_rope, k_swap])
    return np.concatenate([q_nope, q_rope, q_swap]), np.concatenate([k_nope, v]), down


def _mla_rope_tables(seq):
    half = MLA_ROPE_DIM // 2
    inv = ROPE_THETA ** (-jnp.arange(0, MLA_ROPE_DIM, 2, dtype=F32) / MLA_ROPE_DIM)
    ang = jnp.arange(seq, dtype=F32)[:, None] * inv[None, :]
    cos, sin = jnp.cos(ang), jnp.sin(ang)
    reps = LANES // 2 // half
    cos_t = jnp.tile(cos, (1, 2 * reps))
    sin_t = jnp.concatenate([jnp.tile(-sin, (1, reps)), jnp.tile(sin, (1, reps))], axis=1)
    return jnp.concatenate([cos_t, sin_t], axis=1)


def _mla_proj(x, g, w_down, q_norm, w_uq, kv_norm, w_ukv, seq):
    t, d = x.shape
    tm = _tile(seq, 512)
    tk = _tile(seq, ATT_TK)
    ns = seq // tm
    uq_cols, ukv_cols, down_cols = _mla_rope_layout()
    wd = w_down[:, down_cols].astype(BF16)
    wuq = w_uq[:, uq_cols].astype(BF16)
    wukv = w_ukv[:, ukv_cols].astype(BF16)
    cs = _mla_rope_tables(seq)
    scale = (MLA_NOPE_DIM + MLA_ROPE_DIM) ** -0.5 * LOG2E
    full = lambda a: pl.BlockSpec(a.shape, lambda i: (0, 0))
    g2, qg, kvg = g.reshape(1, d), q_norm.reshape(1, -1), kv_norm.reshape(1, -1)
    return pl.pallas_call(
        functools.partial(_mla_proj_kernel, scale=scale, tk=tk),
        grid=(t // tm,),
        in_specs=[pl.BlockSpec((tm, d), lambda i: (i, 0)), full(g2), full(wd), full(qg), full(wuq),
                  full(kvg), full(wukv), pl.BlockSpec((tm, 2 * LANES), lambda i: (i % ns, 0))],
        out_specs=[pl.BlockSpec((tm, MLA_OUT), lambda i: (i, 0)),
                   pl.BlockSpec((tm // tk, MLA_QN, tk), lambda i: (i, 0, 0))],
        out_shape=[jax.ShapeDtypeStruct((t, MLA_OUT), BF16),
                   jax.ShapeDtypeStruct((t // tk, MLA_QN, tk), BF16)],
        compiler_params=_params("parallel"),
        name="mla_proj",
    )(x, g2, wd, qg, wuq, kvg, wukv, cs)


def _mla_kernel(qn_ref, qr_ref, kn_ref, kr_ref, vt_ref, o_ref, acc_ref, ml_ref, s_ref, *, tq, tk):
    p = pl.program_id(1)
    i = pl.program_id(2)
    q = jnp.concatenate([qn_ref[...], qr_ref[...]], axis=1)
    lane = lax.broadcasted_iota(jnp.int32, (tq, 2 * LANES), 1)
    half = MLA_ROPE_DIM // 2
    zero = jnp.zeros_like(q)
    qts = []
    for j in range(2):
        r0 = LANES + (2 * (p % 2) + j) * half
        sel = jnp.logical_and(lane >= j * HEAD_DIM, lane < (j + 1) * HEAD_DIM)
        sel = jnp.logical_or(sel, jnp.logical_and(lane >= r0, lane < r0 + half))
        r1 = r0 + LANES // 2
        sel = jnp.logical_or(sel, jnp.logical_and(lane >= r1, lane < r1 + half))
        qts.append(_transposed(jnp.where(sel, q, zero)))
    acc_ref[...] = jnp.zeros_like(acc_ref)
    row = lax.broadcasted_iota(jnp.int32, ml_ref.shape, 0)
    ml_ref[...] = jnp.where(row < 2, NEG, 0.0)

    def scores(unit, _):
        jb, h, lanes, _ = unit
        start = pl.multiple_of(jb * tk, tk)
        kt = jnp.concatenate([kn_ref[pl.ds(start, tk), :], kr_ref[pl.ds(start, tk), :]], axis=1)
        return _dot(kt, qts[h][:, lanes])

    def softmax(unit, s):
        _, h, lanes, mask = unit
        if mask is not None:
            s = jnp.where(mask, s, NEG)
        m_old = ml_ref[h:h + 1, lanes]
        m_new = jnp.maximum(m_old, jnp.max(s, axis=0, keepdims=True))
        alpha = jnp.exp2(m_old - m_new)
        e = jnp.exp2(s - m_new)
        ml_ref[h:h + 1, lanes] = m_new
        ml_ref[2 + h:3 + h, lanes] = alpha * ml_ref[2 + h:3 + h, lanes] + jnp.sum(e, axis=0, keepdims=True)
        return alpha, e.astype(BF16)

    def values(unit, val):
        jb, h, lanes, _ = unit
        alpha, e = val
        rows = slice(h * HEAD_DIM, (h + 1) * HEAD_DIM)
        acc_ref[rows, lanes] = alpha * acc_ref[rows, lanes] + _dot(vt_ref[jb, rows, :], e)

    _causal_sweep(i, tq, tk, (scores, softmax, values), False, s_ref)
    head = lax.broadcasted_iota(jnp.int32, acc_ref.shape, 0) < HEAD_DIM
    denom = jnp.where(head, ml_ref[2:3, :], ml_ref[3:4, :])
    o_ref[...] = (acc_ref[...] / denom).T.astype(o_ref.dtype)


def _mla_attention(a, vt, batch, seq):
    t = a.shape[0]
    tq = _tile(seq, ATT_TQ)
    tk = vt.shape[2]
    nq = seq // tq
    pairs = MLA_HEADS // 2
    qr0 = MLA_QN // LANES
    kn0 = (MLA_QN + MLA_QR) // LANES
    kr0 = kn0 + pairs
    return pl.pallas_call(
        functools.partial(_mla_kernel, tq=tq, tk=tk),
        grid=(batch, pairs, nq),
        in_specs=[pl.BlockSpec((tq, LANES), lambda b, p, i: (b * nq + i, p)),
                  pl.BlockSpec((tq, LANES), lambda b, p, i: (b * nq + i, qr0 + p // 2)),
                  pl.BlockSpec((seq, LANES), lambda b, p, i: (b, kn0 + p)),
                  pl.BlockSpec((seq, LANES), lambda b, p, i: (b, kr0)),
                  pl.BlockSpec((seq // tk, LANES, tk), lambda b, p, i: (b, p, 0))],
        out_specs=pl.BlockSpec((tq, LANES), lambda b, p, i: (b * nq + i, p)),
        out_shape=jax.ShapeDtypeStruct((t, MLA_HEADS * HEAD_DIM), BF16),
        scratch_shapes=[pltpu.VMEM((LANES, tq), F32), pltpu.VMEM((8, tq), F32),
                        pltpu.VMEM((2 * (tq // tk), tk, tq), F32)],
        compiler_params=_params("parallel", "parallel", "arbitrary"),
        name="mla_attention",
    )(a, a, a, a, vt)


FFN_CHUNK = 256


def _swiglu_into(acc_ref, h, wg, wu, wd):
    def gate_up(cols, _):
        return _dot(h, wg(cols)), _dot(h, wu(cols))

    def activate(cols, gu):
        g, u = gu
        return (g * jax.nn.sigmoid(g) * u).astype(BF16)

    def down(cols, a):
        acc_ref[...] += _dot(a, wd(cols))

    return gate_up, activate, down


def _run_swiglu(acc_ref, h, wg, wu, wd, width):
    chunks = [slice(c, min(c + FFN_CHUNK, width)) for c in range(0, width, FFN_CHUNK)]
    _staggered(chunks, _swiglu_into(acc_ref, h, wg, wu, wd))


def _ffn_kernel(x_ref, a_ref, wo_ref, g_ref, wg_ref, wu_ref, wd_ref, o_ref, h_ref):
    f = pl.program_id(1)

    @pl.when(f == 0)
    def _():
        x = x_ref[...] + _dot(a_ref[...], wo_ref[...])
        h_ref[...] = _rms(x, g_ref[...]).astype(BF16)
        o_ref[...] = x

    _run_swiglu(o_ref, h_ref[...], lambda c: wg_ref[:, c], lambda c: wu_ref[:, c], lambda c: wd_ref[c, :],
                wg_ref.shape[1])


def _ffn(x, a, wo, g, wg, wu, wd):
    t, d = x.shape
    fdim = wg.shape[1]
    tm = _tile(t, 512)
    tf = fdim
    return pl.pallas_call(
        _ffn_kernel,
        grid=(t // tm, fdim // tf),
        in_specs=[pl.BlockSpec((tm, d), lambda i, f: (i, 0)),
                  pl.BlockSpec((tm, a.shape[1]), lambda i, f: (i, 0)),
                  pl.BlockSpec(wo.shape, lambda i, f: (0, 0)),
                  pl.BlockSpec((1, d), lambda i, f: (0, 0)),
                  pl.BlockSpec((d, tf), lambda i, f: (0, f)),
                  pl.BlockSpec((d, tf), lambda i, f: (0, f)),
                  pl.BlockSpec((tf, d), lambda i, f: (f, 0))],
        out_specs=pl.BlockSpec((tm, d), lambda i, f: (i, 0)),
        out_shape=jax.ShapeDtypeStruct((t, d), F32),
        scratch_shapes=[pltpu.VMEM((tm, d), BF16)],
        compiler_params=_params("parallel", "arbitrary"),
        name="dense_ffn",
    )(x, a, wo, g.reshape(1, d), wg, wu, wd)


MOE_TILE = 512
META_ROWS = 8
ROUTE_SUB = 128
COMBINE_SUB = 256


def _pack_halves(x):
    half = x.shape[1] // 2
    bits = lax.bitcast_convert_type(x.astype(BF16).astype(F32), jnp.uint32)
    return bits[:, :half] | (bits[:, half:] >> 16)


def _unpack_halves(w):
    hi = lax.bitcast_convert_type(w & jnp.uint32(0xFFFF0000), F32)
    lo = lax.bitcast_convert_type(w << 16, F32)
    return jnp.concatenate([hi, lo], axis=1)


def _router_kernel(x_ref, g_ref, rt_ref, su_ref, eye_ref, hb_ref, meta_ref, metac_ref, cnt_ref, carry_ref):
    @pl.when(pl.program_id(0) == 0)
    def _():
        carry_ref[...] = jnp.zeros_like(carry_ref)

    h = _rms(x_ref[...], g_ref[...])
    hb_ref[...] = _pack_halves(h)
    ne = rt_ref.shape[0]
    r = rt_ref[...]
    r_hi = r.astype(BF16).astype(F32)
    h_hi = h.astype(BF16)
    h_lo = (h - h_hi.astype(F32)).astype(BF16)
    both_r = _dot_nt(jnp.concatenate([r_hi, r - r_hi], axis=0).astype(BF16), h_hi)
    logits = both_r[:ne] + both_r[ne:] + _dot_nt(r_hi.astype(BF16), h_lo)
    ne, ct = logits.shape
    eio = lax.broadcasted_iota(jnp.int32, (ne, ct), 0).astype(F32)
    m1 = jnp.max(logits, axis=0, keepdims=True)
    i1 = jnp.min(jnp.where(logits == m1, eio, float(ne)), axis=0, keepdims=True)
    rest = jnp.where(eio == i1, -jnp.inf, logits)
    m2 = jnp.max(rest, axis=0, keepdims=True)
    i2 = jnp.min(jnp.where(rest == m2, eio, float(ne)), axis=0, keepdims=True)
    e2 = jnp.exp(m2 - m1)
    g1 = 1.0 / (1.0 + e2)
    g2 = e2 / (1.0 + e2)
    oh1 = (eio == i1).astype(F32)
    oh2 = (eio == i2).astype(F32)
    both = oh1 + oh2
    seen = carry_ref[:, 0:1] + _dot(both.astype(BF16), su_ref[...])
    r1 = jnp.sum(oh1 * seen, axis=0, keepdims=True)
    r2 = jnp.sum(oh2 * seen, axis=0, keepdims=True)
    meta = jnp.zeros((META_ROWS, ct), F32)
    for row, val in enumerate((i1, i2, r1, r2, g1, g2)):
        meta = jnp.where(eio == float(row), val, meta)
    meta_ref[...] = meta
    metac_ref[...] = lax.dot_general(eye_ref[...], meta, _NT, precision=lax.Precision.HIGHEST,
                                     preferred_element_type=F32)
    carry_ref[...] = carry_ref[...] + jnp.sum(both, axis=1, keepdims=True)
    cnt_ref[0] = carry_ref[...]


def _router(x, g, router):
    t, d = x.shape
    ct = _tile(t, MOE_TILE)
    nc = t // ct
    su = jnp.asarray(np.triu(np.ones((ct, ct), np.float32), 1), BF16)
    eye = jnp.eye(ct, dtype=F32)
    return pl.pallas_call(
        _router_kernel,
        grid=(nc,),
        in_specs=[pl.BlockSpec((ct, d), lambda i: (i, 0)),
                  pl.BlockSpec((1, d), lambda i: (0, 0)),
                  pl.BlockSpec((N_EXPERTS, d), lambda i: (0, 0)),
                  pl.BlockSpec((ct, ct), lambda i: (0, 0)),
                  pl.BlockSpec((ct, ct), lambda i: (0, 0))],
        out_specs=[pl.BlockSpec((ct, d // 2), lambda i: (i, 0)),
                   pl.BlockSpec((META_ROWS, ct), lambda i: (0, i)),
                   pl.BlockSpec((ct, META_ROWS), lambda i: (i, 0)),
                   pl.BlockSpec((1, N_EXPERTS, LANES), lambda i: (i, 0, 0))],
        out_shape=[jax.ShapeDtypeStruct((t, d // 2), jnp.uint32),
                   jax.ShapeDtypeStruct((META_ROWS, t), F32),
                   jax.ShapeDtypeStruct((t, META_ROWS), F32),
                   jax.ShapeDtypeStruct((nc, N_EXPERTS, LANES), F32)],
        scratch_shapes=[pltpu.VMEM((N_EXPERTS, LANES), F32)],
        compiler_params=_params("arbitrary"),
        name="moe_router",
    )(x, g.reshape(1, d), router.T, su, eye)


def _group_layout(cnt_after, t, rt):
    rows = 2 * t + N_EXPERTS * rt
    counts = cnt_after[:, :, 0].astype(jnp.int32)
    padded = (counts[-1] + rt - 1) // rt * rt
    off_end = jnp.cumsum(padded)
    off = (off_end - padded).astype(jnp.int32)
    tile_start = jnp.arange(rows // rt, dtype=jnp.int32) * rt
    tile_expert = jnp.minimum(_count_le(off_end, tile_start), N_EXPERTS - 1).astype(jnp.int32)
    tile_valid = (tile_start < off_end[-1]).astype(jnp.int32)
    return rows, off, counts, tile_expert, tile_valid


def _count_le(sorted_vals, x):
    return jnp.sum((sorted_vals[None, :] <= x[:, None]).astype(jnp.int32), axis=1)


def _work_lists(counts, off, rows, t, ct):
    nc = t // ct
    nrt = rows // ct
    before = jnp.concatenate([jnp.zeros((1, N_EXPERTS), jnp.int32), counts[:-1]], axis=0)
    chunk_start = (off[:, None] + before.T).reshape(-1)
    tile_start = jnp.arange(nrt, dtype=jnp.int32) * ct
    starts = jnp.sort(jnp.concatenate([chunk_start, tile_start]))
    ends = jnp.concatenate([starts[1:], jnp.array([rows], jnp.int32)])
    item_r = jnp.minimum(starts // ct, nrt - 1)
    item_c = (_count_le(chunk_start, starts) - 1) % nc
    first = lambda key: jnp.concatenate([jnp.ones((1,), jnp.int32), (key[1:] != key[:-1]).astype(jnp.int32)])
    dispatch = (item_r, item_c, starts, ends, first(item_r))
    order = jnp.argsort(item_c * (rows + 1) + starts)
    c2 = item_c[order]
    combine = (c2, item_r[order], starts[order], ends[order], first(c2))
    return dispatch, combine


def _positions(expert, rank, off_ref):
    base = jnp.zeros_like(rank)
    for e in range(N_EXPERTS):
        base = jnp.where(expert == float(e), off_ref[e].astype(F32), base)
    return base + rank


def _dispatch_kernel(r_ref, c_ref, s_ref, e_ref, first_ref, off_ref, meta_ref, h_ref, xs_ref, gate_ref):
    n = pl.program_id(0)
    rt, ct = xs_ref.shape[0], h_ref.shape[0]

    @pl.when(first_ref[n] == 1)
    def _():
        xs_ref[...] = jnp.zeros_like(xs_ref)
        gate_ref[...] = jnp.zeros_like(gate_ref)

    meta = meta_ref[...]
    pos0 = _positions(meta[0:1], meta[2:3], off_ref)
    pos1 = _positions(meta[1:2], meta[3:4], off_ref)
    sub = min(rt, ROUTE_SUB)
    for sb in range(rt // sub):
        base = r_ref[n] * rt + sb * sub

        @pl.when(jnp.logical_and(base < e_ref[n], base + sub > s_ref[n]))
        def _():
            row = lax.broadcasted_iota(jnp.int32, (sub, ct), 0) + base
            live = jnp.logical_and(row >= s_ref[n], row < e_ref[n])
            rowf = row.astype(F32)
            sel0 = jnp.logical_and(pos0 == rowf, live)
            sel1 = jnp.logical_and(pos1 == rowf, live)
            sel = jnp.logical_or(sel0, sel1).astype(F32).astype(BF16)
            rows = slice(sb * sub, (sb + 1) * sub)
            xs_ref[rows, :] += _pack_halves(_dot(sel, _unpack_halves(h_ref[...]).astype(BF16)))
            gate = jnp.where(sel0, meta[4:5], 0.0) + jnp.where(sel1, meta[5:6], 0.0)
            gate_ref[rows, :] += jnp.sum(gate, axis=1, keepdims=True)


def _dispatch(plan, off, meta, hb, rows, ct):
    t, d = hb.shape
    n_items = plan[0].shape[0]
    return pl.pallas_call(
        _dispatch_kernel,
        grid_spec=pltpu.PrefetchScalarGridSpec(
            num_scalar_prefetch=6,
            grid=(n_items,),
            in_specs=[pl.BlockSpec((META_ROWS, ct), lambda n, r, c, *_: (0, c[n])),
                      pl.BlockSpec((ct, d), lambda n, r, c, *_: (c[n], 0))],
            out_specs=[pl.BlockSpec((ct, d), lambda n, r, c, *_: (r[n], 0)),
                       pl.BlockSpec((ct, 1), lambda n, r, c, *_: (r[n], 0))]),
        out_shape=[jax.ShapeDtypeStruct((rows, d), jnp.uint32),
                   jax.ShapeDtypeStruct((rows, 1), F32)],
        compiler_params=_params("arbitrary"),
        name="moe_dispatch",
    )(*plan, off, meta, hb)


def _expert_kernel(te_ref, tv_ref, xs_ref, gate_ref, wg_ref, wu_ref, wd_ref, y_ref, acc_ref):
    r = pl.program_id(0)
    f = pl.program_id(1)

    @pl.when(f == 0)
    def _():
        acc_ref[...] = jnp.zeros_like(acc_ref)

    @pl.when(tv_ref[r] == 1)
    def _():
        _run_swiglu(acc_ref, _unpack_halves(xs_ref[...]).astype(BF16), lambda c: wg_ref[0, :, c],
                    lambda c: wu_ref[0, :, c], lambda c: wd_ref[0, c, :], wg_ref.shape[2])

    @pl.when(f == pl.num_programs(1) - 1)
    def _():
        y_ref[...] = _pack_halves(acc_ref[...] * gate_ref[...])


def _experts(tile_expert, tile_valid, xs, gate, wg, wu, wd, ct):
    rows, dp = xs.shape
    d = 2 * dp
    fdim = wg.shape[2]
    tf = _tile(fdim, 1792)
    nf = fdim // tf

    def wcol(r, f, te, tv):
        return (te[r], 0, jnp.where(tv[r] == 1, f, nf - 1))

    def wrow(r, f, te, tv):
        return (te[r], jnp.where(tv[r] == 1, f, nf - 1), 0)

    return pl.pallas_call(
        _expert_kernel,
        grid_spec=pltpu.PrefetchScalarGridSpec(
            num_scalar_prefetch=2,
            grid=(rows // ct, nf),
            in_specs=[pl.BlockSpec((ct, dp), lambda r, f, te, tv: (r, 0)),
                      pl.BlockSpec((ct, 1), lambda r, f, te, tv: (r, 0)),
                      pl.BlockSpec((1, d, tf), wcol),
                      pl.BlockSpec((1, d, tf), wcol),
                      pl.BlockSpec((1, tf, d), wrow)],
            out_specs=pl.BlockSpec((ct, dp), lambda r, f, te, tv: (r, 0)),
            scratch_shapes=[pltpu.VMEM((ct, d), F32)]),
        out_shape=jax.ShapeDtypeStruct((rows, dp), jnp.uint32),
        compiler_params=_params("arbitrary", "arbitrary"),
        name="moe_experts",
    )(tile_expert, tile_valid, xs, gate, wg, wu, wd)


def _combine_kernel(c_ref, r_ref, s_ref, e_ref, first_ref, off_ref, x_ref, metac_ref, y_ref, o_ref):
    n = pl.program_id(0)
    ct, rt = x_ref.shape[0], y_ref.shape[0]

    @pl.when(first_ref[n] == 1)
    def _():
        o_ref[...] = x_ref[...]

    metac = metac_ref[...]
    pos0 = _positions(metac[:, 0:1], metac[:, 2:3], off_ref)
    pos1 = _positions(metac[:, 1:2], metac[:, 3:4], off_ref)
    sub = min(rt, COMBINE_SUB)
    for sb in range(rt // sub):
        base = r_ref[n] * rt + sb * sub

        @pl.when(jnp.logical_and(base < e_ref[n], base + sub > s_ref[n]))
        def _():
            row = lax.broadcasted_iota(jnp.int32, (ct, sub), 1) + base
            live = jnp.logical_and(row >= s_ref[n], row < e_ref[n])
            rowf = row.astype(F32)
            sel = jnp.logical_and(jnp.logical_or(pos0 == rowf, pos1 == rowf), live).astype(F32).astype(BF16)
            o_ref[...] += _dot(sel, _unpack_halves(y_ref[sb * sub:(sb + 1) * sub, :]).astype(BF16))


def _combine(plan, off, x, metac, y, ct):
    t, d = x.shape
    n_items = plan[0].shape[0]
    return pl.pallas_call(
        _combine_kernel,
        grid_spec=pltpu.PrefetchScalarGridSpec(
            num_scalar_prefetch=6,
            grid=(n_items,),
            in_specs=[pl.BlockSpec((ct, d), lambda n, c, r, *_: (c[n], 0)),
                      pl.BlockSpec((ct, META_ROWS), lambda n, c, r, *_: (c[n], 0)),
                      pl.BlockSpec((ct, d // 2), lambda n, c, r, *_: (r[n], 0))],
            out_specs=pl.BlockSpec((ct, d), lambda n, c, r, *_: (c[n], 0))),
        out_shape=jax.ShapeDtypeStruct((t, d), F32),
        compiler_params=_params("arbitrary"),
        name="moe_combine",
    )(*plan, off, x, metac, y)


def _cast_kernel(w_ref, o_ref):
    o_ref[...] = w_ref[...].astype(o_ref.dtype)


CAST_BLOCK_BYTES = 8 * 1024 * 1024


def _layer_bf16(w, layer):
    shape = w.shape[1:]
    cols = shape[-1]
    rows = math.prod(shape[:-1])
    tr = rows
    while tr * cols * 4 > CAST_BLOCK_BYTES and tr % 16 == 0:
        tr //= 2
    out = pl.pallas_call(
        _cast_kernel,
        grid=(rows // tr,),
        in_specs=[pl.BlockSpec((None, tr, cols), lambda i: (layer, i, 0))],
        out_specs=pl.BlockSpec((tr, cols), lambda i: (i, 0)),
        out_shape=jax.ShapeDtypeStruct((rows, cols), BF16),
        compiler_params=_params("parallel"),
        name="weights_bf16",
    )(w.reshape(w.shape[0], rows, cols))
    return out.reshape(shape)


SC_CORES = 2
SC_SUBCORES = 16
SC_CHUNK = 128


def _sc_worker_chunks(total):
    per = total // (SC_CORES * SC_SUBCORES)
    wid = lax.axis_index("s") * SC_CORES + lax.axis_index("c")
    return wid * per, per // SC_CHUNK


def _sc_mesh():
    return plsc.VectorSubcoreMesh(core_axis_name="c", subcore_axis_name="s")


def _sc_gather(table, idx):
    v, d = table.shape
    b = idx.shape[0]

    def body(table_hbm, idx_hbm, out_hbm, idx_v, rows_v, sem):
        base, chunks = _sc_worker_chunks(b)

        @pl.loop(0, chunks)
        def _(j):
            start = base + j * SC_CHUNK
            pltpu.sync_copy(idx_hbm.at[pl.ds(start, SC_CHUNK)], idx_v)
            pltpu.async_copy(table_hbm.at[idx_v], rows_v, sem).wait()
            pltpu.sync_copy(rows_v, out_hbm.at[pl.ds(start, SC_CHUNK)])

    return pl.kernel(
        body,
        out_type=jax.ShapeDtypeStruct((b, d), table.dtype),
        mesh=_sc_mesh(),
        scratch_types=[pltpu.VMEM((SC_CHUNK,), jnp.int32),
                       pltpu.VMEM((SC_CHUNK, d), table.dtype),
                       pltpu.SemaphoreType.DMA],
        name="sc_row_gather",
    )(table, idx)


def _sc_scatter_rows(src, idx, n):
    t, d = src.shape
    b = idx.shape[0]

    def body(src_hbm, idx_hbm, out_hbm, idx_v, rows_v):
        base, chunks = _sc_worker_chunks(b)

        @pl.loop(0, chunks)
        def _(j):
            start = base + j * SC_CHUNK
            first = lax.rem(start, t)
            pltpu.sync_copy(idx_hbm.at[pl.ds(start, SC_CHUNK)], idx_v)
            pltpu.sync_copy(src_hbm.at[pl.ds(first, SC_CHUNK)], rows_v)
            pltpu.sync_copy(rows_v, out_hbm.at[idx_v])

    return pl.kernel(
        body,
        out_type=jax.ShapeDtypeStruct((n, d), src.dtype),
        mesh=_sc_mesh(),
        scratch_types=[pltpu.VMEM((SC_CHUNK,), jnp.int32), pltpu.VMEM((SC_CHUNK, d), src.dtype)],
        name="sc_row_scatter",
    )(src, idx)


def _weighted_add_kernel(x_ref, metac_ref, y0_ref, y1_ref, g_ref, o_ref, *, normed):
    gates = metac_ref[...]
    out = (x_ref[...] + gates[:, 4:5] * _unpack_halves(y0_ref[...])
           + gates[:, 5:6] * _unpack_halves(y1_ref[...]))
    o_ref[...] = _rms(out, g_ref[...]) if normed else out


def _weighted_add(x, metac, ysel, out_gain=None):
    t, d = x.shape
    tm = _tile(t, 512)
    nt = t // tm
    gain = jnp.ones((d,), F32) if out_gain is None else out_gain
    return pl.pallas_call(
        functools.partial(_weighted_add_kernel, normed=out_gain is not None),
        grid=(nt,),
        in_specs=[pl.BlockSpec((tm, d), lambda i: (i, 0)),
                  pl.BlockSpec((tm, META_ROWS), lambda i: (i, 0)),
                  pl.BlockSpec((tm, d // 2), lambda i: (i, 0)),
                  pl.BlockSpec((tm, d // 2), lambda i: (nt + i, 0)),
                  pl.BlockSpec((1, d), lambda i: (0, 0))],
        out_specs=pl.BlockSpec((tm, d), lambda i: (i, 0)),
        out_shape=jax.ShapeDtypeStruct((t, d), F32),
        compiler_params=_params("parallel"),
        name="moe_weighted_add",
    )(x, metac, ysel, ysel, gain.reshape(1, d))


def _moe(x, g, router, wg, wu, wd, out_gain=None):
    t = x.shape[0]
    ct = _tile(t, MOE_TILE)
    hb, meta, metac, cnt = _router(x, g, router)
    sc_rows = SC_CORES * SC_SUBCORES * SC_CHUNK
    rows, off, counts, tile_expert, tile_valid = _group_layout(cnt, t, ct)
    if (2 * t) % sc_rows == 0 and rows % sc_rows == 0:
        base = sum(jnp.where(meta[0:2] == float(e), off[e], 0) for e in range(N_EXPERTS))
        pos = (base + meta[2:4].astype(jnp.int32)).reshape(-1)
        xs = _sc_scatter_rows(hb, pos, rows)
        y = _experts(tile_expert, tile_valid, xs, jnp.ones((rows, 1), F32), wg, wu, wd, ct)
        return _weighted_add(x, metac, _sc_gather(y, pos), out_gain)
    dispatch, combine = _work_lists(counts, off, rows, t, ct)
    xs, gate = _dispatch(dispatch, off, meta, hb, rows, ct)
    y = _experts(tile_expert, tile_valid, xs, gate, wg, wu, wd, ct)
    out = _combine(combine, off, x, metac, y, ct)
    return out if out_gain is None else _final_norm(out, out_gain)


def _final_norm_kernel(x_ref, g_ref, o_ref):
    o_ref[...] = _rms(x_ref[...], g_ref[...])


def _final_norm(x, g):
    t, d = x.shape
    tm = _tile(t, 1024)
    return pl.pallas_call(
        _final_norm_kernel,
        grid=(t // tm,),
        in_specs=[pl.BlockSpec((tm, d), lambda i: (i, 0)), pl.BlockSpec((1, d), lambda i: (0, 0))],
        out_specs=pl.BlockSpec((tm, d), lambda i: (i, 0)),
        out_shape=jax.ShapeDtypeStruct((t, d), F32),
        compiler_params=_params("parallel"),
        name="final_norm",
    )(x, g.reshape(1, d))


def kernel(x, rel_bias, attn_norm, ffn_norm, final_norm, swa_w_qkv, swa_b_qkv, swa_sinks, swa_w_o, sb_w_qkv, sb_w_o, mla_w_down, mla_q_norm, mla_w_uq, mla_kv_norm, mla_w_ukv, mla_w_o, ffn_w_gate, ffn_w_up, ffn_w_down, moe_router, moe_w_gate, moe_w_up, moe_w_down):
    batch, seq, d = x.shape
    depth = attn_norm.shape[0]
    xt = x.reshape(batch * seq, d)
    swa_bias = _swa_bias(rel_bias)
    for i in range(depth):
        mixer, j = i % N_MIXERS, i // N_MIXERS
        if mixer == 0:
            w, b, cs = _swa_weights(swa_w_qkv[j], swa_b_qkv[j])
            qkv = _norm_proj(xt, attn_norm[i], w, b, cs, "swa_proj")
            o = _swa_attention(qkv, swa_bias, swa_sinks[j], batch, seq)
            wo = swa_w_o[j].astype(BF16)
        elif mixer == 1:
            n = sb_w_qkv.shape[2] // 3
            cs = jnp.concatenate([jnp.full((n,), HEAD_DIM ** -0.5 * LOG2E, F32), jnp.ones((n,), F32)])
            qk, vt = _norm_proj_vt(xt, attn_norm[i], sb_w_qkv[j].astype(BF16), cs, n, seq, "sb_proj")
            o = _sb_attention(qk, vt, batch, seq)
            wo = sb_w_o[j].astype(BF16)
        else:
            a, vt = _mla_proj(xt, attn_norm[i], mla_w_down[j], mla_q_norm[j], mla_w_uq[j],
                              mla_kv_norm[j], mla_w_ukv[j], seq)
            o = _mla_attention(a, vt, batch, seq)
            wo = mla_w_o[j].astype(BF16)
        f = i // 2
        if i % 2 == 0:
            xt = _ffn(xt, o, wo, ffn_norm[i], _layer_bf16(ffn_w_gate, f), _layer_bf16(ffn_w_up, f),
                      _layer_bf16(ffn_w_down, f))
        else:
            last = final_norm if i == depth - 1 else None
            xt = _out_proj(xt, o, wo, "mixer_out")
            xt = _moe(xt, ffn_norm[i], moe_router[f], _layer_bf16(moe_w_gate, f), _layer_bf16(moe_w_up, f),
                      _layer_bf16(moe_w_down, f), last)
    if depth % 2 == 1:
        xt = _final_norm(xt, final_norm)
    return xt.reshape(batch, seq, d)
```

```python
import functools
import math

import numpy as np
import jax
import jax.numpy as jnp
from jax import lax
from jax.experimental import pallas as pl
from jax.experimental.pallas import tpu as pltpu
from jax.experimental.pallas import tpu_sc as plsc

F32 = jnp.float32
BF16 = jnp.bfloat16

N_MIXERS = 3
RMS_EPS = 1e-6
SWA_HEADS = 16
SWA_KV_HEADS = 2
HEAD_DIM = 64
SWA_BLOCK = 128
SWA_WINDOW = 128
REL_BUCKETS = 32
REL_MAX_DIST = 128
SB_HEADS = 16
MLA_HEADS = 16
MLA_NOPE_DIM = 64
MLA_ROPE_DIM = 32
MLA_Q_RANK = 384
MLA_KV_RANK = 256
ROPE_THETA = 10000.0
N_EXPERTS = 8

LANES = 128
NEG = -1e30
LOG2E = math.log2(math.e)
VMEM_LIMIT = 56 * 1024 * 1024

_NT = (((1,), (1,)), ((), ()))


def _dot(a, b):
    return jnp.dot(a, b, preferred_element_type=F32)


def _dot_nt(a, b):
    return lax.dot_general(a, b, _NT, preferred_element_type=F32)


def _rms(x, g):
    return x * lax.rsqrt(jnp.mean(x * x, axis=-1, keepdims=True) + RMS_EPS) * g


def _params(*sem):
    return pltpu.CompilerParams(dimension_semantics=sem, vmem_limit_bytes=VMEM_LIMIT)


def _tile(n, pref):
    return pref if n % pref == 0 else n


def _norm_proj_kernel(x_ref, g_ref, w_ref, b_ref, cs_ref, o_ref):
    h = _rms(x_ref[...], g_ref[...]).astype(BF16)
    y = (_dot(h, w_ref[...]) + b_ref[...]) * cs_ref[...]
    o_ref[...] = y.astype(o_ref.dtype)


def _norm_proj(x, g, w, b, colscale, name):
    t, d = x.shape
    n = w.shape[1]
    tm = _tile(t, 512)
    return pl.pallas_call(
        _norm_proj_kernel,
        grid=(t // tm,),
        in_specs=[pl.BlockSpec((tm, d), lambda i: (i, 0)),
                  pl.BlockSpec((1, d), lambda i: (0, 0)),
                  pl.BlockSpec((d, n), lambda i: (0, 0)),
                  pl.BlockSpec((1, n), lambda i: (0, 0)),
                  pl.BlockSpec((1, n), lambda i: (0, 0))],
        out_specs=pl.BlockSpec((tm, n), lambda i: (i, 0)),
        out_shape=jax.ShapeDtypeStruct((t, n), BF16),
        compiler_params=_params("parallel"),
        name=name,
    )(x, g.reshape(1, d), w, b.reshape(1, n), colscale.reshape(1, n))


ATT_TQ = 512
ATT_TK = 256


def _store_vt(vt_ref, v, tk):
    for s in range(v.shape[0] // tk):
        vt_ref[s] = v[s * tk:(s + 1) * tk, :].T.astype(vt_ref.dtype)


def _norm_proj_vt_kernel(x_ref, g_ref, w_ref, cs_ref, o_ref, vt_ref, *, tk):
    h = _rms(x_ref[...], g_ref[...]).astype(BF16)
    y = _dot(h, w_ref[...])
    n = o_ref.shape[1]
    o_ref[...] = (y[:, :n] * cs_ref[...]).astype(o_ref.dtype)
    _store_vt(vt_ref, y[:, n:], tk)


def _norm_proj_vt(x, g, w, colscale, nv, seq, name):
    t, d = x.shape
    n = w.shape[1] - nv
    tm = _tile(seq, 512)
    tk = _tile(seq, ATT_TK)
    per = tm // tk
    return pl.pallas_call(
        functools.partial(_norm_proj_vt_kernel, tk=tk),
        grid=(t // tm,),
        in_specs=[pl.BlockSpec((tm, d), lambda i: (i, 0)),
                  pl.BlockSpec((1, d), lambda i: (0, 0)),
                  pl.BlockSpec((d, n + nv), lambda i: (0, 0)),
                  pl.BlockSpec((1, n), lambda i: (0, 0))],
        out_specs=[pl.BlockSpec((tm, n), lambda i: (i, 0)),
                   pl.BlockSpec((per, nv, tk), lambda i: (i, 0, 0))],
        out_shape=[jax.ShapeDtypeStruct((t, n), BF16),
                   jax.ShapeDtypeStruct((t // tk, nv, tk), BF16)],
        compiler_params=_params("parallel"),
        name=name,
    )(x, g.reshape(1, d), w, colscale.reshape(1, n))


def _out_proj_kernel(x_ref, o_ref, w_ref, y_ref):
    y_ref[...] = x_ref[...] + _dot(o_ref[...], w_ref[...])


def _out_proj(x, o, w, name):
    t, d = x.shape
    k = o.shape[1]
    tm = _tile(t, 512)
    return pl.pallas_call(
        _out_proj_kernel,
        grid=(t // tm,),
        in_specs=[pl.BlockSpec((tm, d), lambda i: (i, 0)),
                  pl.BlockSpec((tm, k), lambda i: (i, 0)),
                  pl.BlockSpec((k, d), lambda i: (0, 0))],
        out_specs=pl.BlockSpec((tm, d), lambda i: (i, 0)),
        out_shape=jax.ShapeDtypeStruct((t, d), F32),
        compiler_params=_params("parallel"),
        name=name,
    )(x, o, w)


def _t5_bucket_table():
    qi = np.arange(SWA_BLOCK)[:, None]
    kj = np.arange(2 * SWA_BLOCK)[None, :]
    dist = qi + SWA_BLOCK - kj
    d0 = np.maximum(dist, 0)
    max_exact = REL_BUCKETS // 2
    d = np.maximum(d0, 1).astype(np.float32)
    large = max_exact + (np.log(d / max_exact) / math.log(REL_MAX_DIST / max_exact)
                         * (REL_BUCKETS - max_exact)).astype(np.int32)
    large = np.minimum(large, REL_BUCKETS - 1)
    bucket = np.where(d0 < max_exact, d0, large)
    band = (dist >= 0) & (dist < SWA_WINDOW)
    return np.where(band, bucket, -1).astype(np.int32)


def _swa_bias_kernel(rel_ref, bucket_ref, o_ref):
    h = pl.program_id(0)
    bucket = bucket_ref[...]
    acc = jnp.full(bucket.shape, NEG, F32)
    for b in range(REL_BUCKETS):
        acc = jnp.where(bucket == b, rel_ref[b, h], acc)
    o_ref[0] = acc


def _swa_bias(rel_bias):
    bucket = jnp.asarray(_t5_bucket_table())
    q, k = bucket.shape
    return pl.pallas_call(
        _swa_bias_kernel,
        grid=(SWA_HEADS,),
        in_specs=[pl.BlockSpec(memory_space=pltpu.SMEM),
                  pl.BlockSpec((q, k), lambda h: (0, 0))],
        out_specs=pl.BlockSpec((1, q, k), lambda h: (h, 0, 0)),
        out_shape=jax.ShapeDtypeStruct((SWA_HEADS, q, k), F32),
        compiler_params=_params("arbitrary"),
        name="swa_bias",
    )(rel_bias, bucket)


def _swa_kernel(sink_ref, q_ref, kc_ref, kp_ref, vc_ref, vp_ref, bias_ref, o_ref, *, tq):
    i = pl.program_id(1)
    g = pl.program_id(2)
    blk = SWA_BLOCK
    heads = bias_ref.shape[0]
    lo_half = lax.broadcasted_iota(jnp.int32, (blk, LANES), 1) < HEAD_DIM
    in_prev = lax.broadcasted_iota(jnp.int32, (heads * blk, 2 * blk), 1) < blk
    bias = bias_ref[...].reshape(heads * blk, 2 * blk)

    def window(cur_ref, prev_ref, sb):
        if sb == 0:
            return jnp.concatenate([prev_ref[...], cur_ref[0:blk, :]], axis=0)
        return cur_ref[(sb - 1) * blk:(sb + 1) * blk, :]

    def scores(sb, _):
        q = q_ref[sb * blk:(sb + 1) * blk, :]
        parts = []
        for h in range(heads):
            qp = q[:, (h // 2) * LANES:(h // 2 + 1) * LANES]
            keep = lo_half if h % 2 == 0 else jnp.logical_not(lo_half)
            parts.append(jnp.where(keep, qp, jnp.zeros_like(qp)))
        s = _dot_nt(jnp.concatenate(parts, axis=0), window(kc_ref, kp_ref, sb)) + bias
        if sb == 0:
            s = jnp.where(jnp.logical_and(in_prev, i == 0), NEG, s)
        return s

    def softmax(sb, s):
        ps, inv = [], []
        for h in range(heads):
            sh = s[h * blk:(h + 1) * blk]
            sink = sink_ref[g * heads + h]
            m = jnp.maximum(jnp.max(sh, axis=1, keepdims=True), sink)
            e = jnp.exp(sh - m)
            inv.append(1.0 / (jnp.sum(e, axis=1, keepdims=True) + jnp.exp(sink - m)))
            ps.append(e.astype(BF16))
        return jnp.concatenate(ps, axis=0), inv

    def values(sb, val):
        p, inv = val
        o = _dot(p, window(vc_ref, vp_ref, sb))
        outs = [jnp.where(lo_half, o[h * blk:(h + 1) * blk] * inv[h], o[(h + 1) * blk:(h + 2) * blk] * inv[h + 1])
                for h in range(0, heads, 2)]
        o_ref[sb * blk:(sb + 1) * blk, :] = jnp.concatenate(outs, axis=1).astype(o_ref.dtype)

    _staggered(list(range(tq // blk)), (scores, softmax, values))


def _swa_attention(qkv, bias, sinks, batch, seq):
    t = qkv.shape[0]
    tq = _tile(seq, 512)
    nt = seq // tq
    group = SWA_HEADS // SWA_KV_HEADS
    gw = group * HEAD_DIM
    kcol = SWA_HEADS * HEAD_DIM // LANES
    vcol = kcol + SWA_KV_HEADS
    per = tq // SWA_BLOCK

    def prev_idx(b, i, g):
        return jnp.maximum(b * (seq // SWA_BLOCK) + i * per - 1, 0)

    return pl.pallas_call(
        functools.partial(_swa_kernel, tq=tq),
        grid=(batch, nt, SWA_KV_HEADS),
        in_specs=[pl.BlockSpec(memory_space=pltpu.SMEM),
                  pl.BlockSpec((tq, gw), lambda b, i, g: (b * nt + i, g)),
                  pl.BlockSpec((tq, LANES), lambda b, i, g: (b * nt + i, kcol + g)),
                  pl.BlockSpec((SWA_BLOCK, LANES), lambda b, i, g: (prev_idx(b, i, g), kcol + g)),
                  pl.BlockSpec((tq, LANES), lambda b, i, g: (b * nt + i, vcol + g)),
                  pl.BlockSpec((SWA_BLOCK, LANES), lambda b, i, g: (prev_idx(b, i, g), vcol + g)),
                  pl.BlockSpec((group, SWA_BLOCK, 2 * SWA_BLOCK), lambda b, i, g: (g, 0, 0))],
        out_specs=pl.BlockSpec((tq, gw), lambda b, i, g: (b * nt + i, g)),
        out_shape=jax.ShapeDtypeStruct((t, SWA_HEADS * HEAD_DIM), BF16),
        compiler_params=_params("parallel", "parallel", "arbitrary"),
        name="swa_attention",
    )(sinks, qkv, qkv, qkv, qkv, qkv, bias)


def _swa_weights(w_qkv, b_qkv):
    nq = SWA_HEADS * HEAD_DIM
    kv = SWA_KV_HEADS * HEAD_DIM
    dup = np.concatenate([np.tile(np.arange(HEAD_DIM), 2) + h * HEAD_DIM for h in range(SWA_KV_HEADS)])
    cols = np.concatenate([np.arange(nq), nq + dup, nq + kv + dup])
    scale = np.concatenate([np.full(nq, HEAD_DIM ** -0.5), np.ones(2 * dup.size)]).astype(np.float32)
    return w_qkv[:, cols].astype(BF16), b_qkv[cols], jnp.asarray(scale)


def _staggered(units, stages):
    vals = list(units)
    for t in range(len(units) + len(stages) - 1):
        for k, stage in enumerate(stages):
            u = t - k
            if 0 <= u < len(units):
                vals[u] = stage(units[u], vals[u])


def _score_ahead(units, next_units, stages, s_ref):
    rest = stages[1:]
    vals = [None] * len(units)
    for t in range(len(units) + len(rest) - 1):
        ahead = stages[0](next_units[t], None) if next_units is not None and t < len(units) else None
        for k, stage in enumerate(rest):
            u = t - k
            if 0 <= u < len(units):
                if k == 0:
                    lanes = units[u][2]
                    vals[u] = stage(units[u], s_ref[u, :, :lanes.stop - lanes.start])
                    if ahead is not None:
                        s_ref[u] = ahead
                else:
                    vals[u] = stage(units[u], vals[u])


def _causal_sweep(i, tq, tk, stages, strict, s_ref):
    nsub = tq // tk

    def group(j):
        return [(j * nsub + sb, h, slice(0, tq), None) for sb in reversed(range(nsub)) for h in range(2)]

    def sweep(has_history):
        diagonal = []
        for sb in reversed(range(nsub)):
            nl = tq - sb * tk
            key = lax.broadcasted_iota(jnp.int32, (tk, nl), 0)
            qry = lax.broadcasted_iota(jnp.int32, (tk, nl), 1)
            mask = key < qry if strict else key <= qry
            diagonal += [(i * nsub + sb, h, slice(sb * tk, tq), mask) for h in range(2)]
        for u, unit in enumerate(diagonal):
            s_ref[u, :, :tq - unit[2].start] = stages[0](unit, None)
        if not has_history:
            _score_ahead(diagonal, None, stages, s_ref)
            return
        _score_ahead(diagonal, group(i - 1), stages, s_ref)

        def body(n, carry):
            _score_ahead(group(i - 1 - n), group(i - 2 - n), stages, s_ref)
            return carry

        lax.fori_loop(0, i - 1, body, 0)
        _score_ahead(group(0), None, stages, s_ref)

    pl.when(i == 0)(lambda: sweep(False))
    pl.when(i > 0)(lambda: sweep(True))


def _transposed(x):
    return x.astype(F32).T.astype(x.dtype)


def _split_pair(q):
    lo_half = lax.broadcasted_iota(jnp.int32, q.shape, 1) < HEAD_DIM
    zero = jnp.zeros_like(q)
    return jnp.where(lo_half, q, zero), jnp.where(lo_half, zero, q)


def _sb_kernel(q_ref, k_ref, vt_ref, u_ref, o_ref, acc_ref, c_ref, s_ref, *, tq, tk):
    i = pl.program_id(2)
    qts = [_transposed(qm) for qm in _split_pair(q_ref[...])]
    u = u_ref[...]
    acc_ref[...] = jnp.zeros_like(acc_ref)
    c_ref[...] = jnp.zeros_like(c_ref)

    def scores(unit, _):
        jb, h, lanes, _ = unit
        kt = k_ref[pl.ds(pl.multiple_of(jb * tk, tk), tk), :]
        return _dot(kt, qts[h][:, lanes])

    def log_fail(unit, z):
        neg_abs = lax.bitcast_convert_type(lax.bitcast_convert_type(z, jnp.uint32) | jnp.uint32(1 << 31), F32)
        sp = jnp.maximum(z, 0.0) + jnp.log2(1.0 + jnp.exp2(neg_abs))
        if unit[3] is not None:
            sp = jnp.where(unit[3], sp, 0.0)
        return z, sp.astype(BF16)

    def suffix_sum(unit, val):
        z, sp = val
        return z, _dot(u, sp)

    def weights(unit, val):
        _, h, lanes, mask = unit
        z, r = val
        c = c_ref[h:h + 1, lanes]
        la = z - r - c
        if mask is not None:
            la = jnp.where(mask, la, NEG)
        c_ref[h:h + 1, lanes] = c + r[0:1, :]
        return jnp.exp2(la).astype(BF16)

    def values(unit, a):
        jb, h, lanes, _ = unit
        rows = slice(h * HEAD_DIM, (h + 1) * HEAD_DIM)
        acc_ref[rows, lanes] += _dot(vt_ref[jb, rows, :], a)

    _causal_sweep(i, tq, tk, (scores, log_fail, suffix_sum, weights, values), True, s_ref)
    o_ref[...] = acc_ref[...].T.astype(o_ref.dtype)


def _sb_attention(qk, vt, batch, seq):
    t = qk.shape[0]
    tq = _tile(seq, ATT_TQ)
    tk = vt.shape[2]
    nq = seq // tq
    pairs = SB_HEADS // 2
    u = jnp.asarray(np.triu(np.ones((tk, tk), np.float32)), BF16)
    return pl.pallas_call(
        functools.partial(_sb_kernel, tq=tq, tk=tk),
        grid=(batch, pairs, nq),
        in_specs=[pl.BlockSpec((tq, LANES), lambda b, p, i: (b * nq + i, p)),
                  pl.BlockSpec((seq, LANES), lambda b, p, i: (b, pairs + p)),
                  pl.BlockSpec((seq // tk, LANES, tk), lambda b, p, i: (b, p, 0)),
                  pl.BlockSpec((tk, tk), lambda b, p, i: (0, 0))],
        out_specs=pl.BlockSpec((tq, LANES), lambda b, p, i: (b * nq + i, p)),
        out_shape=jax.ShapeDtypeStruct((t, SB_HEADS * HEAD_DIM), BF16),
        scratch_shapes=[pltpu.VMEM((LANES, tq), F32), pltpu.VMEM((8, tq), F32),
                        pltpu.VMEM((2 * (tq // tk), tk, tq), F32)],
        compiler_params=_params("parallel", "parallel", "arbitrary"),
        name="sb_attention",
    )(qk, qk, vt, u)


MLA_QN = MLA_HEADS * MLA_NOPE_DIM
MLA_QR = MLA_HEADS * MLA_ROPE_DIM
MLA_OUT = 2 * MLA_QN + MLA_QR + LANES


def _mla_proj_kernel(x_ref, g_ref, wd_ref, qg_ref, wuq_ref, kvg_ref, wukv_ref, cs_ref, o_ref, vt_ref,
                     *, scale, tk):
    h = _rms(x_ref[...], g_ref[...]).astype(BF16)
    c = _dot(h, wd_ref[...])
    cos = cs_ref[:, :LANES]
    sin = cs_ref[:, LANES:]
    cq = _rms(c[:, :MLA_Q_RANK], qg_ref[...]).astype(BF16)
    q = _dot(cq, wuq_ref[...])
    o_ref[:, :MLA_QN] = (q[:, :MLA_QN] * scale).astype(o_ref.dtype)
    for m in range(MLA_QR // LANES):
        a = MLA_QN + m * LANES
        rot = q[:, a:a + LANES] * cos + q[:, a + MLA_QR:a + MLA_QR + LANES] * sin
        o_ref[:, a:a + LANES] = (rot * scale).astype(o_ref.dtype)
    kv0 = MLA_Q_RANK + MLA_KV_RANK
    ckv = _rms(c[:, MLA_Q_RANK:kv0], kvg_ref[...]).astype(BF16)
    kv = _dot(ckv, wukv_ref[...])
    kn0 = MLA_QN + MLA_QR
    o_ref[:, kn0:kn0 + MLA_QN] = kv[:, :MLA_QN].astype(o_ref.dtype)
    kr = c[:, kv0:kv0 + LANES] * cos + c[:, kv0 + LANES:kv0 + 2 * LANES] * sin
    o_ref[:, MLA_OUT - LANES:] = kr.astype(o_ref.dtype)
    _store_vt(vt_ref, kv[:, MLA_QN:], tk)


def _mla_rope_layout():
    half = MLA_ROPE_DIM // 2
    per = LANES // MLA_ROPE_DIM
    dq = MLA_NOPE_DIM + MLA_ROPE_DIM
    q_nope = np.concatenate([h * dq + np.arange(MLA_NOPE_DIM) for h in range(MLA_HEADS)])
    q_rope = np.zeros(MLA_QR, np.int64)
    q_swap = np.zeros(MLA_QR, np.int64)
    for h in range(MLA_HEADS):
        m, r = divmod(h, per)
        x1 = h * dq + MLA_NOPE_DIM + np.arange(half)
        x2 = x1 + half
        first = m * LANES + r * half + np.arange(half)
        second = first + LANES // 2
        q_rope[first], q_rope[second] = x1, x2
        q_swap[first], q_swap[second] = x2, x1
    k_rope = np.zeros(LANES, np.int64)
    k_swap = np.zeros(LANES, np.int64)
    base = MLA_Q_RANK + MLA_KV_RANK
    for r in range(per):
        first = r * half + np.arange(half)
        second = first + LANES // 2
        k_rope[first], k_rope[second] = base + np.arange(half), base + half + np.arange(half)
        k_swap[first], k_swap[second] = base + half + np.arange(half), base + np.arange(half)
    dkv = MLA_NOPE_DIM + HEAD_DIM
    k_nope = np.concatenate([h * dkv + np.arange(MLA_NOPE_DIM) for h in range(MLA_HEADS)])
    v = k_nope + MLA_NOPE_DIM
    down = np.concatenate([np.arange(base), k_rope, k_swap])
    return np.concatenate([q_nope, q_rope, q_swap]), np.concatenate([k_nope, v]), down


def _mla_rope_tables(seq):
    half = MLA_ROPE_DIM // 2
    inv = ROPE_THETA ** (-jnp.arange(0, MLA_ROPE_DIM, 2, dtype=F32) / MLA_ROPE_DIM)
    ang = jnp.arange(seq, dtype=F32)[:, None] * inv[None, :]
    cos, sin = jnp.cos(ang), jnp.sin(ang)
    reps = LANES // 2 // half
    cos_t = jnp.tile(cos, (1, 2 * reps))
    sin_t = jnp.concatenate([jnp.tile(-sin, (1, reps)), jnp.tile(sin, (1, reps))], axis=1)
    return jnp.concatenate([cos_t, sin_t], axis=1)


def _mla_proj(x, g, w_down, q_norm, w_uq, kv_norm, w_ukv, seq):
    t, d = x.shape
    tm = _tile(seq, 512)
    tk = _tile(seq, ATT_TK)
    ns = seq // tm
    uq_cols, ukv_cols, down_cols = _mla_rope_layout()
    wd = w_down[:, down_cols].astype(BF16)
    wuq = w_uq[:, uq_cols].astype(BF16)
    wukv = w_ukv[:, ukv_cols].astype(BF16)
    cs = _mla_rope_tables(seq)
    scale = (MLA_NOPE_DIM + MLA_ROPE_DIM) ** -0.5 * LOG2E
    full = lambda a: pl.BlockSpec(a.shape, lambda i: (0, 0))
    g2, qg, kvg = g.reshape(1, d), q_norm.reshape(1, -1), kv_norm.reshape(1, -1)
    return pl.pallas_call(
        functools.partial(_mla_proj_kernel, scale=scale, tk=tk),
        grid=(t // tm,),
        in_specs=[pl.BlockSpec((tm, d), lambda i: (i, 0)), full(g2), full(wd), full(qg), full(wuq),
                  full(kvg), full(wukv), pl.BlockSpec((tm, 2 * LANES), lambda i: (i % ns, 0))],
        out_specs=[pl.BlockSpec((tm, MLA_OUT), lambda i: (i, 0)),
                   pl.BlockSpec((tm // tk, MLA_QN, tk), lambda i: (i, 0, 0))],
        out_shape=[jax.ShapeDtypeStruct((t, MLA_OUT), BF16),
                   jax.ShapeDtypeStruct((t // tk, MLA_QN, tk), BF16)],
        compiler_params=_params("parallel"),
        name="mla_proj",
    )(x, g2, wd, qg, wuq, kvg, wukv, cs)


def _mla_kernel(qn_ref, qr_ref, kn_ref, kr_ref, vt_ref, o_ref, acc_ref, ml_ref, s_ref, *, tq, tk):
    p = pl.program_id(1)
    i = pl.program_id(2)
    q = jnp.concatenate([qn_ref[...], qr_ref[...]], axis=1)
    lane = lax.broadcasted_iota(jnp.int32, (tq, 2 * LANES), 1)
    half = MLA_ROPE_DIM // 2
    zero = jnp.zeros_like(q)
    qts = []
    for j in range(2):
        r0 = LANES + (2 * (p % 2) + j) * half
        sel = jnp.logical_and(lane >= j * HEAD_DIM, lane < (j + 1) * HEAD_DIM)
        sel = jnp.logical_or(sel, jnp.logical_and(lane >= r0, lane < r0 + half))
        r1 = r0 + LANES // 2
        sel = jnp.logical_or(sel, jnp.logical_and(lane >= r1, lane < r1 + half))
        qts.append(_transposed(jnp.where(sel, q, zero)))
    acc_ref[...] = jnp.zeros_like(acc_ref)
    row = lax.broadcasted_iota(jnp.int32, ml_ref.shape, 0)
    ml_ref[...] = jnp.where(row < 2, NEG, 0.0)

    def scores(unit, _):
        jb, h, lanes, _ = unit
        start = pl.multiple_of(jb * tk, tk)
        kt = jnp.concatenate([kn_ref[pl.ds(start, tk), :], kr_ref[pl.ds(start, tk), :]], axis=1)
        return _dot(kt, qts[h][:, lanes])

    def softmax(unit, s):
        _, h, lanes, mask = unit
        if mask is not None:
            s = jnp.where(mask, s, NEG)
        m_old = ml_ref[h:h + 1, lanes]
        m_new = jnp.maximum(m_old, jnp.max(s, axis=0, keepdims=True))
        alpha = jnp.exp2(m_old - m_new)
        e = jnp.exp2(s - m_new)
        ml_ref[h:h + 1, lanes] = m_new
        ml_ref[2 + h:3 + h, lanes] = alpha * ml_ref[2 + h:3 + h, lanes] + jnp.sum(e, axis=0, keepdims=True)
        return alpha, e.astype(BF16)

    def values(unit, val):
        jb, h, lanes, _ = unit
        alpha, e = val
        rows = slice(h * HEAD_DIM, (h + 1) * HEAD_DIM)
        acc_ref[rows, lanes] = alpha * acc_ref[rows, lanes] + _dot(vt_ref[jb, rows, :], e)

    _causal_sweep(i, tq, tk, (scores, softmax, values), False, s_ref)
    head = lax.broadcasted_iota(jnp.int32, acc_ref.shape, 0) < HEAD_DIM
    denom = jnp.where(head, ml_ref[2:3, :], ml_ref[3:4, :])
    o_ref[...] = (acc_ref[...] / denom).T.astype(o_ref.dtype)


def _mla_attention(a, vt, batch, seq):
    t = a.shape[0]
    tq = _tile(seq, ATT_TQ)
    tk = vt.shape[2]
    nq = seq // tq
    pairs = MLA_HEADS // 2
    qr0 = MLA_QN // LANES
    kn0 = (MLA_QN + MLA_QR) // LANES
    kr0 = kn0 + pairs
    return pl.pallas_call(
        functools.partial(_mla_kernel, tq=tq, tk=tk),
        grid=(batch, pairs, nq),
        in_specs=[pl.BlockSpec((tq, LANES), lambda b, p, i: (b * nq + i, p)),
                  pl.BlockSpec((tq, LANES), lambda b, p, i: (b * nq + i, qr0 + p // 2)),
                  pl.BlockSpec((seq, LANES), lambda b, p, i: (b, kn0 + p)),
                  pl.BlockSpec((seq, LANES), lambda b, p, i: (b, kr0)),
                  pl.BlockSpec((seq // tk, LANES, tk), lambda b, p, i: (b, p, 0))],
        out_specs=pl.BlockSpec((tq, LANES), lambda b, p, i: (b * nq + i, p)),
        out_shape=jax.ShapeDtypeStruct((t, MLA_HEADS * HEAD_DIM), BF16),
        scratch_shapes=[pltpu.VMEM((LANES, tq), F32), pltpu.VMEM((8, tq), F32),
                        pltpu.VMEM((2 * (tq // tk), tk, tq), F32)],
        compiler_params=_params("parallel", "parallel", "arbitrary"),
        name="mla_attention",
    )(a, a, a, a, vt)


FFN_CHUNK = 256


def _swiglu_into(acc_ref, h, wg, wu, wd):
    def gate_up(cols, _):
        return _dot(h, wg(cols)), _dot(h, wu(cols))

    def activate(cols, gu):
        g, u = gu
        return (g * jax.nn.sigmoid(g) * u).astype(BF16)

    def down(cols, a):
        acc_ref[...] += _dot(a, wd(cols))

    return gate_up, activate, down


def _run_swiglu(acc_ref, h, wg, wu, wd, width):
    chunks = [slice(c, min(c + FFN_CHUNK, width)) for c in range(0, width, FFN_CHUNK)]
    _staggered(chunks, _swiglu_into(acc_ref, h, wg, wu, wd))


def _ffn_kernel(x_ref, a_ref, wo_ref, g_ref, wg_ref, wu_ref, wd_ref, o_ref, h_ref):
    f = pl.program_id(1)

    @pl.when(f == 0)
    def _():
        x = x_ref[...] + _dot(a_ref[...], wo_ref[...])
        h_ref[...] = _rms(x, g_ref[...]).astype(BF16)
        o_ref[...] = x

    _run_swiglu(o_ref, h_ref[...], lambda c: wg_ref[:, c], lambda c: wu_ref[:, c], lambda c: wd_ref[c, :],
                wg_ref.shape[1])


def _ffn(x, a, wo, g, wg, wu, wd):
    t, d = x.shape
    fdim = wg.shape[1]
    tm = _tile(t, 512)
    tf = fdim
    return pl.pallas_call(
        _ffn_kernel,
        grid=(t // tm, fdim // tf),
        in_specs=[pl.BlockSpec((tm, d), lambda i, f: (i, 0)),
                  pl.BlockSpec((tm, a.shape[1]), lambda i, f: (i, 0)),
                  pl.BlockSpec(wo.shape, lambda i, f: (0, 0)),
                  pl.BlockSpec((1, d), lambda i, f: (0, 0)),
                  pl.BlockSpec((d, tf), lambda i, f: (0, f)),
                  pl.BlockSpec((d, tf), lambda i, f: (0, f)),
                  pl.BlockSpec((tf, d), lambda i, f: (f, 0))],
        out_specs=pl.BlockSpec((tm, d), lambda i, f: (i, 0)),
        out_shape=jax.ShapeDtypeStruct((t, d), F32),
        scratch_shapes=[pltpu.VMEM((tm, d), BF16)],
        compiler_params=_params("parallel", "arbitrary"),
        name="dense_ffn",
    )(x, a, wo, g.reshape(1, d), wg, wu, wd)


MOE_TILE = 512
META_ROWS = 8
ROUTE_SUB = 128
COMBINE_SUB = 256


def _pack_halves(x):
    half = x.shape[1] // 2
    bits = lax.bitcast_convert_type(x.astype(BF16).astype(F32), jnp.uint32)
    return bits[:, :half] | (bits[:, half:] >> 16)


def _unpack_halves(w):
    hi = lax.bitcast_convert_type(w & jnp.uint32(0xFFFF0000), F32)
    lo = lax.bitcast_convert_type(w << 16, F32)
    return jnp.concatenate([hi, lo], axis=1)


def _router_kernel(x_ref, g_ref, rt_ref, su_ref, eye_ref, hb_ref, meta_ref, metac_ref, cnt_ref, carry_ref):
    @pl.when(pl.program_id(0) == 0)
    def _():
        carry_ref[...] = jnp.zeros_like(carry_ref)

    h = _rms(x_ref[...], g_ref[...])
    hb_ref[...] = _pack_halves(h)
    ne = rt_ref.shape[0]
    r = rt_ref[...]
    r_hi = r.astype(BF16).astype(F32)
    h_hi = h.astype(BF16)
    h_lo = (h - h_hi.astype(F32)).astype(BF16)
    both_r = _dot_nt(jnp.concatenate([r_hi, r - r_hi], axis=0).astype(BF16), h_hi)
    logits = both_r[:ne] + both_r[ne:] + _dot_nt(r_hi.astype(BF16), h_lo)
    ne, ct = logits.shape
    eio = lax.broadcasted_iota(jnp.int32, (ne, ct), 0).astype(F32)
    m1 = jnp.max(logits, axis=0, keepdims=True)
    i1 = jnp.min(jnp.where(logits == m1, eio, float(ne)), axis=0, keepdims=True)
    rest = jnp.where(eio == i1, -jnp.inf, logits)
    m2 = jnp.max(rest, axis=0, keepdims=True)
    i2 = jnp.min(jnp.where(rest == m2, eio, float(ne)), axis=0, keepdims=True)
    e2 = jnp.exp(m2 - m1)
    g1 = 1.0 / (1.0 + e2)
    g2 = e2 / (1.0 + e2)
    oh1 = (eio == i1).astype(F32)
    oh2 = (eio == i2).astype(F32)
    both = oh1 + oh2
    seen = carry_ref[:, 0:1] + _dot(both.astype(BF16), su_ref[...])
    r1 = jnp.sum(oh1 * seen, axis=0, keepdims=True)
    r2 = jnp.sum(oh2 * seen, axis=0, keepdims=True)
    meta = jnp.zeros((META_ROWS, ct), F32)
    for row, val in enumerate((i1, i2, r1, r2, g1, g2)):
        meta = jnp.where(eio == float(row), val, meta)
    meta_ref[...] = meta
    metac_ref[...] = lax.dot_general(eye_ref[...], meta, _NT, precision=lax.Precision.HIGHEST,
                                     preferred_element_type=F32)
    carry_ref[...] = carry_ref[...] + jnp.sum(both, axis=1, keepdims=True)
    cnt_ref[0] = carry_ref[...]


def _router(x, g, router):
    t, d = x.shape
    ct = _tile(t, MOE_TILE)
    nc = t // ct
    su = jnp.asarray(np.triu(np.ones((ct, ct), np.float32), 1), BF16)
    eye = jnp.eye(ct, dtype=F32)
    return pl.pallas_call(
        _router_kernel,
        grid=(nc,),
        in_specs=[pl.BlockSpec((ct, d), lambda i: (i, 0)),
                  pl.BlockSpec((1, d), lambda i: (0, 0)),
                  pl.BlockSpec((N_EXPERTS, d), lambda i: (0, 0)),
                  pl.BlockSpec((ct, ct), lambda i: (0, 0)),
                  pl.BlockSpec((ct, ct), lambda i: (0, 0))],
        out_specs=[pl.BlockSpec((ct, d // 2), lambda i: (i, 0)),
                   pl.BlockSpec((META_ROWS, ct), lambda i: (0, i)),
                   pl.BlockSpec((ct, META_ROWS), lambda i: (i, 0)),
                   pl.BlockSpec((1, N_EXPERTS, LANES), lambda i: (i, 0, 0))],
        out_shape=[jax.ShapeDtypeStruct((t, d // 2), jnp.uint32),
                   jax.ShapeDtypeStruct((META_ROWS, t), F32),
                   jax.ShapeDtypeStruct((t, META_ROWS), F32),
                   jax.ShapeDtypeStruct((nc, N_EXPERTS, LANES), F32)],
        scratch_shapes=[pltpu.VMEM((N_EXPERTS, LANES), F32)],
        compiler_params=_params("arbitrary"),
        name="moe_router",
    )(x, g.reshape(1, d), router.T, su, eye)


def _group_layout(cnt_after, t, rt):
    rows = 2 * t + N_EXPERTS * rt
    counts = cnt_after[:, :, 0].astype(jnp.int32)
    padded = (counts[-1] + rt - 1) // rt * rt
    off_end = jnp.cumsum(padded)
    off = (off_end - padded).astype(jnp.int32)
    tile_start = jnp.arange(rows // rt, dtype=jnp.int32) * rt
    tile_expert = jnp.minimum(_count_le(off_end, tile_start), N_EXPERTS - 1).astype(jnp.int32)
    tile_valid = (tile_start < off_end[-1]).astype(jnp.int32)
    return rows, off, counts, tile_expert, tile_valid


def _count_le(sorted_vals, x):
    return jnp.sum((sorted_vals[None, :] <= x[:, None]).astype(jnp.int32), axis=1)


def _work_lists(counts, off, rows, t, ct):
    nc = t // ct
    nrt = rows // ct
    before = jnp.concatenate([jnp.zeros((1, N_EXPERTS), jnp.int32), counts[:-1]], axis=0)
    chunk_start = (off[:, None] + before.T).reshape(-1)
    tile_start = jnp.arange(nrt, dtype=jnp.int32) * ct
    starts = jnp.sort(jnp.concatenate([chunk_start, tile_start]))
    ends = jnp.concatenate([starts[1:], jnp.array([rows], jnp.int32)])
    item_r = jnp.minimum(starts // ct, nrt - 1)
    item_c = (_count_le(chunk_start, starts) - 1) % nc
    first = lambda key: jnp.concatenate([jnp.ones((1,), jnp.int32), (key[1:] != key[:-1]).astype(jnp.int32)])
    dispatch = (item_r, item_c, starts, ends, first(item_r))
    order = jnp.argsort(item_c * (rows + 1) + starts)
    c2 = item_c[order]
    combine = (c2, item_r[order], starts[order], ends[order], first(c2))
    return dispatch, combine


def _positions(expert, rank, off_ref):
    base = jnp.zeros_like(rank)
    for e in range(N_EXPERTS):
        base = jnp.where(expert == float(e), off_ref[e].astype(F32), base)
    return base + rank


def _dispatch_kernel(r_ref, c_ref, s_ref, e_ref, first_ref, off_ref, meta_ref, h_ref, xs_ref, gate_ref):
    n = pl.program_id(0)
    rt, ct = xs_ref.shape[0], h_ref.shape[0]

    @pl.when(first_ref[n] == 1)
    def _():
        xs_ref[...] = jnp.zeros_like(xs_ref)
        gate_ref[...] = jnp.zeros_like(gate_ref)

    meta = meta_ref[...]
    pos0 = _positions(meta[0:1], meta[2:3], off_ref)
    pos1 = _positions(meta[1:2], meta[3:4], off_ref)
    sub = min(rt, ROUTE_SUB)
    for sb in range(rt // sub):
        base = r_ref[n] * rt + sb * sub

        @pl.when(jnp.logical_and(base < e_ref[n], base + sub > s_ref[n]))
        def _():
            row = lax.broadcasted_iota(jnp.int32, (sub, ct), 0) + base
            live = jnp.logical_and(row >= s_ref[n], row < e_ref[n])
            rowf = row.astype(F32)
            sel0 = jnp.logical_and(pos0 == rowf, live)
            sel1 = jnp.logical_and(pos1 == rowf, live)
            sel = jnp.logical_or(sel0, sel1).astype(F32).astype(BF16)
            rows = slice(sb * sub, (sb + 1) * sub)
            xs_ref[rows, :] += _pack_halves(_dot(sel, _unpack_halves(h_ref[...]).astype(BF16)))
            gate = jnp.where(sel0, meta[4:5], 0.0) + jnp.where(sel1, meta[5:6], 0.0)
            gate_ref[rows, :] += jnp.sum(gate, axis=1, keepdims=True)


def _dispatch(plan, off, meta, hb, rows, ct):
    t, d = hb.shape
    n_items = plan[0].shape[0]
    return pl.pallas_call(
        _dispatch_kernel,
        grid_spec=pltpu.PrefetchScalarGridSpec(
            num_scalar_prefetch=6,
            grid=(n_items,),
            in_specs=[pl.BlockSpec((META_ROWS, ct), lambda n, r, c, *_: (0, c[n])),
                      pl.BlockSpec((ct, d), lambda n, r, c, *_: (c[n], 0))],
            out_specs=[pl.BlockSpec((ct, d), lambda n, r, c, *_: (r[n], 0)),
                       pl.BlockSpec((ct, 1), lambda n, r, c, *_: (r[n], 0))]),
        out_shape=[jax.ShapeDtypeStruct((rows, d), jnp.uint32),
                   jax.ShapeDtypeStruct((rows, 1), F32)],
        compiler_params=_params("arbitrary"),
        name="moe_dispatch",
    )(*plan, off, meta, hb)


def _expert_kernel(te_ref, tv_ref, xs_ref, gate_ref, wg_ref, wu_ref, wd_ref, y_ref, acc_ref):
    r = pl.program_id(0)
    f = pl.program_id(1)

    @pl.when(f == 0)
    def _():
        acc_ref[...] = jnp.zeros_like(acc_ref)

    @pl.when(tv_ref[r] == 1)
    def _():
        _run_swiglu(acc_ref, _unpack_halves(xs_ref[...]).astype(BF16), lambda c: wg_ref[0, :, c],
                    lambda c: wu_ref[0, :, c], lambda c: wd_ref[0, c, :], wg_ref.shape[2])

    @pl.when(f == pl.num_programs(1) - 1)
    def _():
        y_ref[...] = _pack_halves(acc_ref[...] * gate_ref[...])


def _experts(tile_expert, tile_valid, xs, gate, wg, wu, wd, ct):
    rows, dp = xs.shape
    d = 2 * dp
    fdim = wg.shape[2]
    tf = _tile(fdim, 1792)
    nf = fdim // tf

    def wcol(r, f, te, tv):
        return (te[r], 0, jnp.where(tv[r] == 1, f, nf - 1))

    def wrow(r, f, te, tv):
        return (te[r], jnp.where(tv[r] == 1, f, nf - 1), 0)

    return pl.pallas_call(
        _expert_kernel,
        grid_spec=pltpu.PrefetchScalarGridSpec(
            num_scalar_prefetch=2,
            grid=(rows // ct, nf),
            in_specs=[pl.BlockSpec((ct, dp), lambda r, f, te, tv: (r, 0)),
                      pl.BlockSpec((ct, 1), lambda r, f, te, tv: (r, 0)),
                      pl.BlockSpec((1, d, tf), wcol),
                      pl.BlockSpec((1, d, tf), wcol),
                      pl.BlockSpec((1, tf, d), wrow)],
            out_specs=pl.BlockSpec((ct, dp), lambda r, f, te, tv: (r, 0)),
            scratch_shapes=[pltpu.VMEM((ct, d), F32)]),
        out_shape=jax.ShapeDtypeStruct((rows, dp), jnp.uint32),
        compiler_params=_params("arbitrary", "arbitrary"),
        name="moe_experts",
    )(tile_expert, tile_valid, xs, gate, wg, wu, wd)


def _combine_kernel(c_ref, r_ref, s_ref, e_ref, first_ref, off_ref, x_ref, metac_ref, y_ref, o_ref):
    n = pl.program_id(0)
    ct, rt = x_ref.shape[0], y_ref.shape[0]

    @pl.when(first_ref[n] == 1)
    def _():
        o_ref[...] = x_ref[...]

    metac = metac_ref[...]
    pos0 = _positions(metac[:, 0:1], metac[:, 2:3], off_ref)
    pos1 = _positions(metac[:, 1:2], metac[:, 3:4], off_ref)
    sub = min(rt, COMBINE_SUB)
    for sb in range(rt // sub):
        base = r_ref[n] * rt + sb * sub

        @pl.when(jnp.logical_and(base < e_ref[n], base + sub > s_ref[n]))
        def _():
            row = lax.broadcasted_iota(jnp.int32, (ct, sub), 1) + base
            live = jnp.logical_and(row >= s_ref[n], row < e_ref[n])
            rowf = row.astype(F32)
            sel = jnp.logical_and(jnp.logical_or(pos0 == rowf, pos1 == rowf), live).astype(F32).astype(BF16)
            o_ref[...] += _dot(sel, _unpack_halves(y_ref[sb * sub:(sb + 1) * sub, :]).astype(BF16))


def _combine(plan, off, x, metac, y, ct):
    t, d = x.shape
    n_items = plan[0].shape[0]
    return pl.pallas_call(
        _combine_kernel,
        grid_spec=pltpu.PrefetchScalarGridSpec(
            num_scalar_prefetch=6,
            grid=(n_items,),
            in_specs=[pl.BlockSpec((ct, d), lambda n, c, r, *_: (c[n], 0)),
                      pl.BlockSpec((ct, META_ROWS), lambda n, c, r, *_: (c[n], 0)),
                      pl.BlockSpec((ct, d // 2), lambda n, c, r, *_: (r[n], 0))],
            out_specs=pl.BlockSpec((ct, d), lambda n, c, r, *_: (c[n], 0))),
        out_shape=jax.ShapeDtypeStruct((t, d), F32),
        compiler_params=_params("arbitrary"),
        name="moe_combine",
    )(*plan, off, x, metac, y)


def _cast_kernel(w_ref, o_ref):
    o_ref[...] = w_ref[...].astype(o_ref.dtype)


CAST_BLOCK_BYTES = 8 * 1024 * 1024


def _layer_bf16(w, layer):
    shape = w.shape[1:]
    cols = shape[-1]
    rows = math.prod(shape[:-1])
    tr = rows
    while tr * cols * 4 > CAST_BLOCK_BYTES and tr % 16 == 0:
        tr //= 2
    out = pl.pallas_call(
        _cast_kernel,
        grid=(rows // tr,),
        in_specs=[pl.BlockSpec((None, tr, cols), lambda i: (layer, i, 0))],
        out_specs=pl.BlockSpec((tr, cols), lambda i: (i, 0)),
        out_shape=jax.ShapeDtypeStruct((rows, cols), BF16),
        compiler_params=_params("parallel"),
        name="weights_bf16",
    )(w.reshape(w.shape[0], rows, cols))
    return out.reshape(shape)


SC_CORES = 2
SC_SUBCORES = 16
SC_CHUNK = 128


def _sc_worker_chunks(total):
    per = total // (SC_CORES * SC_SUBCORES)
    wid = lax.axis_index("s") * SC_CORES + lax.axis_index("c")
    return wid * per, per // SC_CHUNK


def _sc_mesh():
    return plsc.VectorSubcoreMesh(core_axis_name="c", subcore_axis_name="s")


def _sc_gather(table, idx):
    v, d = table.shape
    b = idx.shape[0]

    def body(table_hbm, idx_hbm, out_hbm, idx_v, rows_v, sem):
        base, chunks = _sc_worker_chunks(b)

        @pl.loop(0, chunks)
        def _(j):
            start = base + j * SC_CHUNK
            pltpu.sync_copy(idx_hbm.at[pl.ds(start, SC_CHUNK)], idx_v)
            pltpu.async_copy(table_hbm.at[idx_v], rows_v, sem).wait()
            pltpu.sync_copy(rows_v, out_hbm.at[pl.ds(start, SC_CHUNK)])

    return pl.kernel(
        body,
        out_type=jax.ShapeDtypeStruct((b, d), table.dtype),
        mesh=_sc_mesh(),
        scratch_types=[pltpu.VMEM((SC_CHUNK,), jnp.int32),
                       pltpu.VMEM((SC_CHUNK, d), table.dtype),
                       pltpu.SemaphoreType.DMA],
        name="sc_row_gather",
    )(table, idx)


def _sc_scatter_rows(src, idx, n):
    t, d = src.shape
    b = idx.shape[0]

    def body(src_hbm, idx_hbm, out_hbm, idx_v, rows_v):
        base, chunks = _sc_worker_chunks(b)

        @pl.loop(0, chunks)
        def _(j):
            start = base + j * SC_CHUNK
            first = lax.rem(start, t)
            pltpu.sync_copy(idx_hbm.at[pl.ds(start, SC_CHUNK)], idx_v)
            pltpu.sync_copy(src_hbm.at[pl.ds(first, SC_CHUNK)], rows_v)
            pltpu.sync_copy(rows_v, out_hbm.at[idx_v])

    return pl.kernel(
        body,
        out_type=jax.ShapeDtypeStruct((n, d), src.dtype),
        mesh=_sc_mesh(),
        scratch_types=[pltpu.VMEM((SC_CHUNK,), jnp.int32), pltpu.VMEM((SC_CHUNK, d), src.dtype)],
        name="sc_row_scatter",
    )(src, idx)


def _weighted_add_kernel(x_ref, metac_ref, y0_ref, y1_ref, g_ref, o_ref, *, normed):
    gates = metac_ref[...]
    out = (x_ref[...] + gates[:, 4:5] * _unpack_halves(y0_ref[...])
           + gates[:, 5:6] * _unpack_halves(y1_ref[...]))
    o_ref[...] = _rms(out, g_ref[...]) if normed else out


def _weighted_add(x, metac, ysel, out_gain=None):
    t, d = x.shape
    tm = _tile(t, 512)
    nt = t // tm
    gain = jnp.ones((d,), F32) if out_gain is None else out_gain
    return pl.pallas_call(
        functools.partial(_weighted_add_kernel, normed=out_gain is not None),
        grid=(nt,),
        in_specs=[pl.BlockSpec((tm, d), lambda i: (i, 0)),
                  pl.BlockSpec((tm, META_ROWS), lambda i: (i, 0)),
                  pl.BlockSpec((tm, d // 2), lambda i: (i, 0)),
                  pl.BlockSpec((tm, d // 2), lambda i: (nt + i, 0)),
                  pl.BlockSpec((1, d), lambda i: (0, 0))],
        out_specs=pl.BlockSpec((tm, d), lambda i: (i, 0)),
        out_shape=jax.ShapeDtypeStruct((t, d), F32),
        compiler_params=_params("parallel"),
        name="moe_weighted_add",
    )(x, metac, ysel, ysel, gain.reshape(1, d))


def _moe(x, g, router, wg, wu, wd, out_gain=None):
    t = x.shape[0]
    ct = _tile(t, MOE_TILE)
    hb, meta, metac, cnt = _router(x, g, router)
    sc_rows = SC_CORES * SC_SUBCORES * SC_CHUNK
    rows, off, counts, tile_expert, tile_valid = _group_layout(cnt, t, ct)
    if (2 * t) % sc_rows == 0 and rows % sc_rows == 0:
        base = sum(jnp.where(meta[0:2] == float(e), off[e], 0) for e in range(N_EXPERTS))
        pos = (base + meta[2:4].astype(jnp.int32)).reshape(-1)
        xs = _sc_scatter_rows(hb, pos, rows)
        y = _experts(tile_expert, tile_valid, xs, jnp.ones((rows, 1), F32), wg, wu, wd, ct)
        return _weighted_add(x, metac, _sc_gather(y, pos), out_gain)
    dispatch, combine = _work_lists(counts, off, rows, t, ct)
    xs, gate = _dispatch(dispatch, off, meta, hb, rows, ct)
    y = _experts(tile_expert, tile_valid, xs, gate, wg, wu, wd, ct)
    out = _combine(combine, off, x, metac, y, ct)
    return out if out_gain is None else _final_norm(out, out_gain)


def _final_norm_kernel(x_ref, g_ref, o_ref):
    o_ref[...] = _rms(x_ref[...], g_ref[...])


def _final_norm(x, g):
    t, d = x.shape
    tm = _tile(t, 1024)
    return pl.pallas_call(
        _final_norm_kernel,
        grid=(t // tm,),
        in_specs=[pl.BlockSpec((tm, d), lambda i: (i, 0)), pl.BlockSpec((1, d), lambda i: (0, 0))],
        out_specs=pl.BlockSpec((tm, d), lambda i: (i, 0)),
        out_shape=jax.ShapeDtypeStruct((t, d), F32),
        compiler_params=_params("parallel"),
        name="final_norm",
    )(x, g.reshape(1, d))


def kernel(x, rel_bias, attn_norm, ffn_norm, final_norm, swa_w_qkv, swa_b_qkv, swa_sinks, swa_w_o, sb_w_qkv, sb_w_o, mla_w_down, mla_q_norm, mla_w_uq, mla_kv_norm, mla_w_ukv, mla_w_o, ffn_w_gate, ffn_w_up, ffn_w_down, moe_router, moe_w_gate, moe_w_up, moe_w_down):
    batch, seq, d = x.shape
    depth = attn_norm.shape[0]
    xt = x.reshape(batch * seq, d)
    swa_bias = _swa_bias(rel_bias)
    for i in range(depth):
        mixer, j = i % N_MIXERS, i // N_MIXERS
        if mixer == 0:
            w, b, cs = _swa_weights(swa_w_qkv[j], swa_b_qkv[j])
            qkv = _norm_proj(xt, attn_norm[i], w, b, cs, "swa_proj")
            o = _swa_attention(qkv, swa_bias, swa_sinks[j], batch, seq)
            wo = swa_w_o[j].astype(BF16)
        elif mixer == 1:
            n = sb_w_qkv.shape[2] // 3
            cs = jnp.concatenate([jnp.full((n,), HEAD_DIM ** -0.5 * LOG2E, F32), jnp.ones((n,), F32)])
            qk, vt = _norm_proj_vt(xt, attn_norm[i], sb_w_qkv[j].astype(BF16), cs, n, seq, "sb_proj")
            o = _sb_attention(qk, vt, batch, seq)
            wo = sb_w_o[j].astype(BF16)
        else:
            a, vt = _mla_proj(xt, attn_norm[i], mla_w_down[j], mla_q_norm[j], mla_w_uq[j],
                              mla_kv_norm[j], mla_w_ukv[j], seq)
            o = _mla_attention(a, vt, batch, seq)
            wo = mla_w_o[j].astype(BF16)
        f = i // 2
        if i % 2 == 0:
            xt = _ffn(xt, o, wo, ffn_norm[i], _layer_bf16(ffn_w_gate, f), _layer_bf16(ffn_w_up, f),
                      _layer_bf16(ffn_w_down, f))
        else:
            last = final_norm if i == depth - 1 else None
            xt = _out_proj(xt, o, wo, "mixer_out")
            xt = _moe(xt, ffn_norm[i], moe_router[f], _layer_bf16(moe_w_gate, f), _layer_bf16(moe_w_up, f),
                      _layer_bf16(moe_w_down, f), last)
    if depth % 2 == 1:
        xt = _final_norm(xt, final_norm)
    return xt.reshape(batch, seq, d)
```

```python
import functools
import math

import numpy as np
import jax
import jax.numpy as jnp
from jax import lax
from jax.experimental import pallas as pl
from jax.experimental.pallas import tpu as pltpu
from jax.experimental.pallas import tpu_sc as plsc

F32 = jnp.float32
BF16 = jnp.bfloat16

N_MIXERS = 3
RMS_EPS = 1e-6
SWA_HEADS = 16
SWA_KV_HEADS = 2
HEAD_DIM = 64
SWA_BLOCK = 128
SWA_WINDOW = 128
REL_BUCKETS = 32
REL_MAX_DIST = 128
SB_HEADS = 16
MLA_HEADS = 16
MLA_NOPE_DIM = 64
MLA_ROPE_DIM = 32
MLA_Q_RANK = 384
MLA_KV_RANK = 256
ROPE_THETA = 10000.0
N_EXPERTS = 8

LANES = 128
NEG = -1e30
LOG2E = math.log2(math.e)
VMEM_LIMIT = 56 * 1024 * 1024

_NT = (((1,), (1,)), ((), ()))


def _dot(a, b):
    return jnp.dot(a, b, preferred_element_type=F32)


def _dot_nt(a, b):
    return lax.dot_general(a, b, _NT, preferred_element_type=F32)


def _rms(x, g):
    return x * lax.rsqrt(jnp.mean(x * x, axis=-1, keepdims=True) + RMS_EPS) * g


def _params(*sem):
    return pltpu.CompilerParams(dimension_semantics=sem, vmem_limit_bytes=VMEM_LIMIT)


def _tile(n, pref):
    return pref if n % pref == 0 else n


def _norm_proj_kernel(x_ref, g_ref, w_ref, b_ref, cs_ref, o_ref):
    h = _rms(x_ref[...], g_ref[...]).astype(BF16)
    y = (_dot(h, w_ref[...]) + b_ref[...]) * cs_ref[...]
    o_ref[...] = y.astype(o_ref.dtype)


def _norm_proj(x, g, w, b, colscale, name):
    t, d = x.shape
    n = w.shape[1]
    tm = _tile(t, 512)
    return pl.pallas_call(
        _norm_proj_kernel,
        grid=(t // tm,),
        in_specs=[pl.BlockSpec((tm, d), lambda i: (i, 0)),
                  pl.BlockSpec((1, d), lambda i: (0, 0)),
                  pl.BlockSpec((d, n), lambda i: (0, 0)),
                  pl.BlockSpec((1, n), lambda i: (0, 0)),
                  pl.BlockSpec((1, n), lambda i: (0, 0))],
        out_specs=pl.BlockSpec((tm, n), lambda i: (i, 0)),
        out_shape=jax.ShapeDtypeStruct((t, n), BF16),
        compiler_params=_params("parallel"),
        name=name,
    )(x, g.reshape(1, d), w, b.reshape(1, n), colscale.reshape(1, n))


ATT_TQ = 512
ATT_TK = 256


def _store_vt(vt_ref, v, tk):
    for s in range(v.shape[0] // tk):
        vt_ref[s] = v[s * tk:(s + 1) * tk, :].T.astype(vt_ref.dtype)


def _norm_proj_vt_kernel(x_ref, g_ref, w_ref, cs_ref, o_ref, vt_ref, *, tk):
    h = _rms(x_ref[...], g_ref[...]).astype(BF16)
    y = _dot(h, w_ref[...])
    n = o_ref.shape[1]
    o_ref[...] = (y[:, :n] * cs_ref[...]).astype(o_ref.dtype)
    _store_vt(vt_ref, y[:, n:], tk)


def _norm_proj_vt(x, g, w, colscale, nv, seq, name):
    t, d = x.shape
    n = w.shape[1] - nv
    tm = _tile(seq, 512)
    tk = _tile(seq, ATT_TK)
    per = tm // tk
    return pl.pallas_call(
        functools.partial(_norm_proj_vt_kernel, tk=tk),
        grid=(t // tm,),
        in_specs=[pl.BlockSpec((tm, d), lambda i: (i, 0)),
                  pl.BlockSpec((1, d), lambda i: (0, 0)),
                  pl.BlockSpec((d, n + nv), lambda i: (0, 0)),
                  pl.BlockSpec((1, n), lambda i: (0, 0))],
        out_specs=[pl.BlockSpec((tm, n), lambda i: (i, 0)),
                   pl.BlockSpec((per, nv, tk), lambda i: (i, 0, 0))],
        out_shape=[jax.ShapeDtypeStruct((t, n), BF16),
                   jax.ShapeDtypeStruct((t // tk, nv, tk), BF16)],
        compiler_params=_params("parallel"),
        name=name,
    )(x, g.reshape(1, d), w, colscale.reshape(1, n))


def _out_proj_kernel(x_ref, o_ref, w_ref, y_ref):
    y_ref[...] = x_ref[...] + _dot(o_ref[...], w_ref[...])


def _out_proj(x, o, w, name):
    t, d = x.shape
    k = o.shape[1]
    tm = _tile(t, 512)
    return pl.pallas_call(
        _out_proj_kernel,
        grid=(t // tm,),
        in_specs=[pl.BlockSpec((tm, d), lambda i: (i, 0)),
                  pl.BlockSpec((tm, k), lambda i: (i, 0)),
                  pl.BlockSpec((k, d), lambda i: (0, 0))],
        out_specs=pl.BlockSpec((tm, d), lambda i: (i, 0)),
        out_shape=jax.ShapeDtypeStruct((t, d), F32),
        compiler_params=_params("parallel"),
        name=name,
    )(x, o, w)


def _t5_bucket_table():
    qi = np.arange(SWA_BLOCK)[:, None]
    kj = np.arange(2 * SWA_BLOCK)[None, :]
    dist = qi + SWA_BLOCK - kj
    d0 = np.maximum(dist, 0)
    max_exact = REL_BUCKETS // 2
    d = np.maximum(d0, 1).astype(np.float32)
    large = max_exact + (np.log(d / max_exact) / math.log(REL_MAX_DIST / max_exact)
                         * (REL_BUCKETS - max_exact)).astype(np.int32)
    large = np.minimum(large, REL_BUCKETS - 1)
    bucket = np.where(d0 < max_exact, d0, large)
    band = (dist >= 0) & (dist < SWA_WINDOW)
    return np.where(band, bucket, -1).astype(np.int32)


def _swa_bias_kernel(rel_ref, bucket_ref, o_ref):
    h = pl.program_id(0)
    bucket = bucket_ref[...]
    acc = jnp.full(bucket.shape, NEG, F32)
    for b in range(REL_BUCKETS):
        acc = jnp.where(bucket == b, rel_ref[b, h], acc)
    o_ref[0] = acc


def _swa_bias(rel_bias):
    bucket = jnp.asarray(_t5_bucket_table())
    q, k = bucket.shape
    return pl.pallas_call(
        _swa_bias_kernel,
        grid=(SWA_HEADS,),
        in_specs=[pl.BlockSpec(memory_space=pltpu.SMEM),
                  pl.BlockSpec((q, k), lambda h: (0, 0))],
        out_specs=pl.BlockSpec((1, q, k), lambda h: (h, 0, 0)),
        out_shape=jax.ShapeDtypeStruct((SWA_HEADS, q, k), F32),
        compiler_params=_params("arbitrary"),
        name="swa_bias",
    )(rel_bias, bucket)


def _swa_kernel(sink_ref, q_ref, kc_ref, kp_ref, vc_ref, vp_ref, bias_ref, o_ref, *, tq):
    i = pl.program_id(1)
    g = pl.program_id(2)
    blk = SWA_BLOCK
    heads = bias_ref.shape[0]
    lo_half = lax.broadcasted_iota(jnp.int32, (blk, LANES), 1) < HEAD_DIM
    in_prev = lax.broadcasted_iota(jnp.int32, (heads * blk, 2 * blk), 1) < blk
    bias = bias_ref[...].reshape(heads * blk, 2 * blk)

    def window(cur_ref, prev_ref, sb):
        if sb == 0:
            return jnp.concatenate([prev_ref[...], cur_ref[0:blk, :]], axis=0)
        return cur_ref[(sb - 1) * blk:(sb + 1) * blk, :]

    def scores(sb, _):
        q = q_ref[sb * blk:(sb + 1) * blk, :]
        parts = []
        for h in range(heads):
            qp = q[:, (h // 2) * LANES:(h // 2 + 1) * LANES]
            keep = lo_half if h % 2 == 0 else jnp.logical_not(lo_half)
            parts.append(jnp.where(keep, qp, jnp.zeros_like(qp)))
        s = _dot_nt(jnp.concatenate(parts, axis=0), window(kc_ref, kp_ref, sb)) + bias
        if sb == 0:
            s = jnp.where(jnp.logical_and(in_prev, i == 0), NEG, s)
        return s

    def softmax(sb, s):
        ps, inv = [], []
        for h in range(heads):
            sh = s[h * blk:(h + 1) * blk]
            sink = sink_ref[g * heads + h]
            m = jnp.maximum(jnp.max(sh, axis=1, keepdims=True), sink)
            e = jnp.exp(sh - m)
            inv.append(1.0 / (jnp.sum(e, axis=1, keepdims=True) + jnp.exp(sink - m)))
            ps.append(e.astype(BF16))
        return jnp.concatenate(ps, axis=0), inv

    def values(sb, val):
        p, inv = val
        o = _dot(p, window(vc_ref, vp_ref, sb))
        outs = [jnp.where(lo_half, o[h * blk:(h + 1) * blk] * inv[h], o[(h + 1) * blk:(h + 2) * blk] * inv[h + 1])
                for h in range(0, heads, 2)]
        o_ref[sb * blk:(sb + 1) * blk, :] = jnp.concatenate(outs, axis=1).astype(o_ref.dtype)

    _staggered(list(range(tq // blk)), (scores, softmax, values))


def _swa_attention(qkv, bias, sinks, batch, seq):
    t = qkv.shape[0]
    tq = _tile(seq, 512)
    nt = seq // tq
    group = SWA_HEADS // SWA_KV_HEADS
    gw = group * HEAD_DIM
    kcol = SWA_HEADS * HEAD_DIM // LANES
    vcol = kcol + SWA_KV_HEADS
    per = tq // SWA_BLOCK

    def prev_idx(b, i, g):
        return jnp.maximum(b * (seq // SWA_BLOCK) + i * per - 1, 0)

    return pl.pallas_call(
        functools.partial(_swa_kernel, tq=tq),
        grid=(batch, nt, SWA_KV_HEADS),
        in_specs=[pl.BlockSpec(memory_space=pltpu.SMEM),
                  pl.BlockSpec((tq, gw), lambda b, i, g: (b * nt + i, g)),
                  pl.BlockSpec((tq, LANES), lambda b, i, g: (b * nt + i, kcol + g)),
                  pl.BlockSpec((SWA_BLOCK, LANES), lambda b, i, g: (prev_idx(b, i, g), kcol + g)),
                  pl.BlockSpec((tq, LANES), lambda b, i, g: (b * nt + i, vcol + g)),
                  pl.BlockSpec((SWA_BLOCK, LANES), lambda b, i, g: (prev_idx(b, i, g), vcol + g)),
                  pl.BlockSpec((group, SWA_BLOCK, 2 * SWA_BLOCK), lambda b, i, g: (g, 0, 0))],
        out_specs=pl.BlockSpec((tq, gw), lambda b, i, g: (b * nt + i, g)),
        out_shape=jax.ShapeDtypeStruct((t, SWA_HEADS * HEAD_DIM), BF16),
        compiler_params=_params("parallel", "parallel", "arbitrary"),
        name="swa_attention",
    )(sinks, qkv, qkv, qkv, qkv, qkv, bias)


def _swa_weights(w_qkv, b_qkv):
    nq = SWA_HEADS * HEAD_DIM
    kv = SWA_KV_HEADS * HEAD_DIM
    dup = np.concatenate([np.tile(np.arange(HEAD_DIM), 2) + h * HEAD_DIM for h in range(SWA_KV_HEADS)])
    cols = np.concatenate([np.arange(nq), nq + dup, nq + kv + dup])
    scale = np.concatenate([np.full(nq, HEAD_DIM ** -0.5), np.ones(2 * dup.size)]).astype(np.float32)
    return w_qkv[:, cols].astype(BF16), b_qkv[cols], jnp.asarray(scale)


def _staggered(units, stages):
    vals = list(units)
    for t in range(len(units) + len(stages) - 1):
        for k, stage in enumerate(stages):
            u = t - k
            if 0 <= u < len(units):
                vals[u] = stage(units[u], vals[u])


def _score_ahead(units, next_units, stages, s_ref):
    rest = stages[1:]
    vals = [None] * len(units)
    for t in range(len(units) + len(rest) - 1):
        ahead = stages[0](next_units[t], None) if next_units is not None and t < len(units) else None
        for k, stage in enumerate(rest):
            u = t - k
            if 0 <= u < len(units):
                if k == 0:
                    lanes = units[u][2]
                    vals[u] = stage(units[u], s_ref[u, :, :lanes.stop - lanes.start])
                    if ahead is not None:
                        s_ref[u] = ahead
                else:
                    vals[u] = stage(units[u], vals[u])


def _causal_sweep(i, tq, tk, stages, strict, s_ref, heads=2):
    nsub = tq // tk

    def group(j):
        return [(j * nsub + sb, h, slice(0, tq), None) for sb in reversed(range(nsub)) for h in range(heads)]

    def sweep(has_history):
        diagonal = []
        for sb in reversed(range(nsub)):
            nl = tq - sb * tk
            key = lax.broadcasted_iota(jnp.int32, (tk, nl), 0)
            qry = lax.broadcasted_iota(jnp.int32, (tk, nl), 1)
            mask = key < qry if strict else key <= qry
            diagonal += [(i * nsub + sb, h, slice(sb * tk, tq), mask) for h in range(heads)]
        for u, unit in enumerate(diagonal):
            s_ref[u, :, :tq - unit[2].start] = stages[0](unit, None)
        if not has_history:
            _score_ahead(diagonal, None, stages, s_ref)
            return
        _score_ahead(diagonal, group(i - 1), stages, s_ref)

        def body(n, carry):
            _score_ahead(group(i - 1 - n), group(i - 2 - n), stages, s_ref)
            return carry

        lax.fori_loop(0, i - 1, body, 0)
        _score_ahead(group(0), None, stages, s_ref)

    pl.when(i == 0)(lambda: sweep(False))
    pl.when(i > 0)(lambda: sweep(True))


def _transposed(x):
    return x.astype(F32).T.astype(x.dtype)


def _split_pair(q):
    lo_half = lax.broadcasted_iota(jnp.int32, q.shape, 1) < HEAD_DIM
    zero = jnp.zeros_like(q)
    return jnp.where(lo_half, q, zero), jnp.where(lo_half, zero, q)


SB_STEP_HEADS = 4


def _sb_kernel(q_ref, k_ref, vt_ref, u_ref, o_ref, acc_ref, c_ref, s_ref, *, tq, tk, heads):
    i = pl.program_id(2)
    qts = [_transposed(qm) for pair in range(heads // 2)
           for qm in _split_pair(q_ref[:, pair * LANES:(pair + 1) * LANES])]
    u = u_ref[...]
    acc_ref[...] = jnp.zeros_like(acc_ref)
    c_ref[...] = jnp.zeros_like(c_ref)

    def scores(unit, _):
        jb, h, lanes, _ = unit
        kt = k_ref[pl.ds(pl.multiple_of(jb * tk, tk), tk), (h // 2) * LANES:(h // 2 + 1) * LANES]
        return _dot(kt, qts[h][:, lanes])

    def log_fail(unit, z):
        neg_abs = lax.bitcast_convert_type(lax.bitcast_convert_type(z, jnp.uint32) | jnp.uint32(1 << 31), F32)
        sp = jnp.maximum(z, 0.0) + jnp.log2(1.0 + jnp.exp2(neg_abs))
        if unit[3] is not None:
            sp = jnp.where(unit[3], sp, 0.0)
        return z, sp.astype(BF16)

    def suffix_sum(unit, val):
        z, sp = val
        return z, _dot(u, sp)

    def weights(unit, val):
        _, h, lanes, mask = unit
        z, r = val
        c = c_ref[h:h + 1, lanes]
        la = z - r - c
        if mask is not None:
            la = jnp.where(mask, la, NEG)
        c_ref[h:h + 1, lanes] = c + r[0:1, :]
        return jnp.exp2(la).astype(BF16)

    def values(unit, a):
        jb, h, lanes, _ = unit
        rows = slice(h * HEAD_DIM, (h + 1) * HEAD_DIM)
        acc_ref[rows, lanes] += _dot(vt_ref[jb, rows, :], a)

    _causal_sweep(i, tq, tk, (scores, log_fail, suffix_sum, weights, values), True, s_ref, heads)
    o_ref[...] = acc_ref[...].T.astype(o_ref.dtype)


def _sb_attention(qk, vt, batch, seq):
    t = qk.shape[0]
    tq = _tile(seq, ATT_TQ)
    tk = vt.shape[2]
    nq = seq // tq
    groups = SB_HEADS // SB_STEP_HEADS
    gw = SB_STEP_HEADS * HEAD_DIM
    u = jnp.asarray(np.triu(np.ones((tk, tk), np.float32)), BF16)
    return pl.pallas_call(
        functools.partial(_sb_kernel, tq=tq, tk=tk, heads=SB_STEP_HEADS),
        grid=(batch, groups, nq),
        in_specs=[pl.BlockSpec((tq, gw), lambda b, p, i: (b * nq + i, p)),
                  pl.BlockSpec((seq, gw), lambda b, p, i: (b, groups + p)),
                  pl.BlockSpec((seq // tk, gw, tk), lambda b, p, i: (b, p, 0)),
                  pl.BlockSpec((tk, tk), lambda b, p, i: (0, 0))],
        out_specs=pl.BlockSpec((tq, gw), lambda b, p, i: (b * nq + i, p)),
        out_shape=jax.ShapeDtypeStruct((t, SB_HEADS * HEAD_DIM), BF16),
        scratch_shapes=[pltpu.VMEM((gw, tq), F32), pltpu.VMEM((8, tq), F32),
                        pltpu.VMEM((SB_STEP_HEADS * (tq // tk), tk, tq), F32)],
        compiler_params=_params("parallel", "parallel", "arbitrary"),
        name="sb_attention",
    )(qk, qk, vt, u)


MLA_QN = MLA_HEADS * MLA_NOPE_DIM
MLA_QR = MLA_HEADS * MLA_ROPE_DIM
MLA_OUT = 2 * MLA_QN + MLA_QR + LANES


def _mla_proj_kernel(x_ref, g_ref, wd_ref, qg_ref, wuq_ref, kvg_ref, wukv_ref, cs_ref, o_ref, vt_ref,
                     *, scale, tk):
    h = _rms(x_ref[...], g_ref[...]).astype(BF16)
    c = _dot(h, wd_ref[...])
    cos = cs_ref[:, :LANES]
    sin = cs_ref[:, LANES:]
    cq = _rms(c[:, :MLA_Q_RANK], qg_ref[...]).astype(BF16)
    q = _dot(cq, wuq_ref[...])
    o_ref[:, :MLA_QN] = (q[:, :MLA_QN] * scale).astype(o_ref.dtype)
    for m in range(MLA_QR // LANES):
        a = MLA_QN + m * LANES
        rot = q[:, a:a + LANES] * cos + q[:, a + MLA_QR:a + MLA_QR + LANES] * sin
        o_ref[:, a:a + LANES] = (rot * scale).astype(o_ref.dtype)
    kv0 = MLA_Q_RANK + MLA_KV_RANK
    ckv = _rms(c[:, MLA_Q_RANK:kv0], kvg_ref[...]).astype(BF16)
    kv = _dot(ckv, wukv_ref[...])
    kn0 = MLA_QN + MLA_QR
    o_ref[:, kn0:kn0 + MLA_QN] = kv[:, :MLA_QN].astype(o_ref.dtype)
    kr = c[:, kv0:kv0 + LANES] * cos + c[:, kv0 + LANES:kv0 + 2 * LANES] * sin
    o_ref[:, MLA_OUT - LANES:] = kr.astype(o_ref.dtype)
    _store_vt(vt_ref, kv[:, MLA_QN:], tk)


def _mla_rope_layout():
    half = MLA_ROPE_DIM // 2
    per = LANES // MLA_ROPE_DIM
    dq = MLA_NOPE_DIM + MLA_ROPE_DIM
    q_nope = np.concatenate([h * dq + np.arange(MLA_NOPE_DIM) for h in range(MLA_HEADS)])
    q_rope = np.zeros(MLA_QR, np.int64)
    q_swap = np.zeros(MLA_QR, np.int64)
    for h in range(MLA_HEADS):
        m, r = divmod(h, per)
        x1 = h * dq + MLA_NOPE_DIM + np.arange(half)
        x2 = x1 + half
        first = m * LANES + r * half + np.arange(half)
        second = first + LANES // 2
        q_rope[first], q_rope[second] = x1, x2
        q_swap[first], q_swap[second] = x2, x1
    k_rope = np.zeros(LANES, np.int64)
    k_swap = np.zeros(LANES, np.int64)
    base = MLA_Q_RANK + MLA_KV_RANK
    for r in range(per):
        first = r * half + np.arange(half)
        second = first + LANES // 2
        k_rope[first], k_rope[second] = base + np.arange(half), base + half + np.arange(half)
        k_swap[first], k_swap[second] = base + half + np.arange(half), base + np.arange(half)
    dkv = MLA_NOPE_DIM + HEAD_DIM
    k_nope = np.concatenate([h * dkv + np.arange(MLA_NOPE_DIM) for h in range(MLA_HEADS)])
    v = k_nope + MLA_NOPE_DIM
    down = np.concatenate([np.arange(base), k_rope, k_swap])
    return np.concatenate([q_nope, q_rope, q_swap]), np.concatenate([k_nope, v]), down


def _mla_rope_tables(seq):
    half = MLA_ROPE_DIM // 2
    inv = ROPE_THETA ** (-jnp.arange(0, MLA_ROPE_DIM, 2, dtype=F32) / MLA_ROPE_DIM)
    ang = jnp.arange(seq, dtype=F32)[:, None] * inv[None, :]
    cos, sin = jnp.cos(ang), jnp.sin(ang)
    reps = LANES // 2 // half
    cos_t = jnp.tile(cos, (1, 2 * reps))
    sin_t = jnp.concatenate([jnp.tile(-sin, (1, reps)), jnp.tile(sin, (1, reps))], axis=1)
    return jnp.concatenate([cos_t, sin_t], axis=1)


def _mla_proj(x, g, w_down, q_norm, w_uq, kv_norm, w_ukv, seq):
    t, d = x.shape
    tm = _tile(seq, 512)
    tk = _tile(seq, ATT_TK)
    ns = seq // tm
    uq_cols, ukv_cols, down_cols = _mla_rope_layout()
    wd = w_down[:, down_cols].astype(BF16)
    wuq = w_uq[:, uq_cols].astype(BF16)
    wukv = w_ukv[:, ukv_cols].astype(BF16)
    cs = _mla_rope_tables(seq)
    scale = (MLA_NOPE_DIM + MLA_ROPE_DIM) ** -0.5 * LOG2E
    full = lambda a: pl.BlockSpec(a.shape, lambda i: (0, 0))
    g2, qg, kvg = g.reshape(1, d), q_norm.reshape(1, -1), kv_norm.reshape(1, -1)
    return pl.pallas_call(
        functools.partial(_mla_proj_kernel, scale=scale, tk=tk),
        grid=(t // tm,),
        in_specs=[pl.BlockSpec((tm, d), lambda i: (i, 0)), full(g2), full(wd), full(qg), full(wuq),
                  full(kvg), full(wukv), pl.BlockSpec((tm, 2 * LANES), lambda i: (i % ns, 0))],
        out_specs=[pl.BlockSpec((tm, MLA_OUT), lambda i: (i, 0)),
                   pl.BlockSpec((tm // tk, MLA_QN, tk), lambda i: (i, 0, 0))],
        out_shape=[jax.ShapeDtypeStruct((t, MLA_OUT), BF16),
                   jax.ShapeDtypeStruct((t // tk, MLA_QN, tk), BF16)],
        compiler_params=_params("parallel"),
        name="mla_proj",
    )(x, g2, wd, qg, wuq, kvg, wukv, cs)


def _mla_kernel(qn_ref, qr_ref, kn_ref, kr_ref, vt_ref, o_ref, acc_ref, ml_ref, s_ref, *, tq, tk):
    p = pl.program_id(1)
    i = pl.program_id(2)
    q = jnp.concatenate([qn_ref[...], qr_ref[...]], axis=1)
    lane = lax.broadcasted_iota(jnp.int32, (tq, 2 * LANES), 1)
    half = MLA_ROPE_DIM // 2
    zero = jnp.zeros_like(q)
    qts = []
    for j in range(2):
        r0 = LANES + (2 * (p % 2) + j) * half
        sel = jnp.logical_and(lane >= j * HEAD_DIM, lane < (j + 1) * HEAD_DIM)
        sel = jnp.logical_or(sel, jnp.logical_and(lane >= r0, lane < r0 + half))
        r1 = r0 + LANES // 2
        sel = jnp.logical_or(sel, jnp.logical_and(lane >= r1, lane < r1 + half))
        qts.append(_transposed(jnp.where(sel, q, zero)))
    acc_ref[...] = jnp.zeros_like(acc_ref)
    row = lax.broadcasted_iota(jnp.int32, ml_ref.shape, 0)
    ml_ref[...] = jnp.where(row < 2, NEG, 0.0)

    def scores(unit, _):
        jb, h, lanes, _ = unit
        start = pl.multiple_of(jb * tk, tk)
        kt = jnp.concatenate([kn_ref[pl.ds(start, tk), :], kr_ref[pl.ds(start, tk), :]], axis=1)
        return _dot(kt, qts[h][:, lanes])

    def softmax(unit, s):
        _, h, lanes, mask = unit
        if mask is not None:
            s = jnp.where(mask, s, NEG)
        m_old = ml_ref[h:h + 1, lanes]
        m_new = jnp.maximum(m_old, jnp.max(s, axis=0, keepdims=True))
        alpha = jnp.exp2(m_old - m_new)
        e = jnp.exp2(s - m_new)
        ml_ref[h:h + 1, lanes] = m_new
        ml_ref[2 + h:3 + h, lanes] = alpha * ml_ref[2 + h:3 + h, lanes] + jnp.sum(e, axis=0, keepdims=True)
        return alpha, e.astype(BF16)

    def values(unit, val):
        jb, h, lanes, _ = unit
        alpha, e = val
        rows = slice(h * HEAD_DIM, (h + 1) * HEAD_DIM)
        acc_ref[rows, lanes] = alpha * acc_ref[rows, lanes] + _dot(vt_ref[jb, rows, :], e)

    _causal_sweep(i, tq, tk, (scores, softmax, values), False, s_ref)
    head = lax.broadcasted_iota(jnp.int32, acc_ref.shape, 0) < HEAD_DIM
    denom = jnp.where(head, ml_ref[2:3, :], ml_ref[3:4, :])
    o_ref[...] = (acc_ref[...] / denom).T.astype(o_ref.dtype)


def _mla_attention(a, vt, batch, seq):
    t = a.shape[0]
    tq = _tile(seq, ATT_TQ)
    tk = vt.shape[2]
    nq = seq // tq
    pairs = MLA_HEADS // 2
    qr0 = MLA_QN // LANES
    kn0 = (MLA_QN + MLA_QR) // LANES
    kr0 = kn0 + pairs
    return pl.pallas_call(
        functools.partial(_mla_kernel, tq=tq, tk=tk),
        grid=(batch, pairs, nq),
        in_specs=[pl.BlockSpec((tq, LANES), lambda b, p, i: (b * nq + i, p)),
                  pl.BlockSpec((tq, LANES), lambda b, p, i: (b * nq + i, qr0 + p // 2)),
                  pl.BlockSpec((seq, LANES), lambda b, p, i: (b, kn0 + p)),
                  pl.BlockSpec((seq, LANES), lambda b, p, i: (b, kr0)),
                  pl.BlockSpec((seq // tk, LANES, tk), lambda b, p, i: (b, p, 0))],
        out_specs=pl.BlockSpec((tq, LANES), lambda b, p, i: (b * nq + i, p)),
        out_shape=jax.ShapeDtypeStruct((t, MLA_HEADS * HEAD_DIM), BF16),
        scratch_shapes=[pltpu.VMEM((LANES, tq), F32), pltpu.VMEM((8, tq), F32),
                        pltpu.VMEM((2 * (tq // tk), tk, tq), F32)],
        compiler_params=_params("parallel", "parallel", "arbitrary"),
        name="mla_attention",
    )(a, a, a, a, vt)


FFN_CHUNK = 256


def _swiglu_into(acc_ref, h, wg, wu, wd):
    def gate_up(cols, _):
        return _dot(h, wg(cols)), _dot(h, wu(cols))

    def activate(cols, gu):
        g, u = gu
        return (g * jax.nn.sigmoid(g) * u).astype(BF16)

    def down(cols, a):
        acc_ref[...] += _dot(a, wd(cols))

    return gate_up, activate, down


def _run_swiglu(acc_ref, h, wg, wu, wd, width):
    chunks = [slice(c, min(c + FFN_CHUNK, width)) for c in range(0, width, FFN_CHUNK)]
    _staggered(chunks, _swiglu_into(acc_ref, h, wg, wu, wd))


def _ffn_kernel(x_ref, a_ref, wo_ref, g_ref, wg_ref, wu_ref, wd_ref, o_ref, h_ref):
    f = pl.program_id(1)

    @pl.when(f == 0)
    def _():
        x = x_ref[...] + _dot(a_ref[...], wo_ref[...])
        h_ref[...] = _rms(x, g_ref[...]).astype(BF16)
        o_ref[...] = x

    _run_swiglu(o_ref, h_ref[...], lambda c: wg_ref[:, c], lambda c: wu_ref[:, c], lambda c: wd_ref[c, :],
                wg_ref.shape[1])


def _ffn(x, a, wo, g, wg, wu, wd):
    t, d = x.shape
    fdim = wg.shape[1]
    tm = _tile(t, 512)
    tf = fdim
    return pl.pallas_call(
        _ffn_kernel,
        grid=(t // tm, fdim // tf),
        in_specs=[pl.BlockSpec((tm, d), lambda i, f: (i, 0)),
                  pl.BlockSpec((tm, a.shape[1]), lambda i, f: (i, 0)),
                  pl.BlockSpec(wo.shape, lambda i, f: (0, 0)),
                  pl.BlockSpec((1, d), lambda i, f: (0, 0)),
                  pl.BlockSpec((d, tf), lambda i, f: (0, f)),
                  pl.BlockSpec((d, tf), lambda i, f: (0, f)),
                  pl.BlockSpec((tf, d), lambda i, f: (f, 0))],
        out_specs=pl.BlockSpec((tm, d), lambda i, f: (i, 0)),
        out_shape=jax.ShapeDtypeStruct((t, d), F32),
        scratch_shapes=[pltpu.VMEM((tm, d), BF16)],
        compiler_params=_params("parallel", "arbitrary"),
        name="dense_ffn",
    )(x, a, wo, g.reshape(1, d), wg, wu, wd)


MOE_TILE = 512
META_ROWS = 8
ROUTE_SUB = 128
COMBINE_SUB = 256


def _pack_halves(x):
    half = x.shape[1] // 2
    bits = lax.bitcast_convert_type(x.astype(BF16).astype(F32), jnp.uint32)
    return bits[:, :half] | (bits[:, half:] >> 16)


def _unpack_halves(w):
    hi = lax.bitcast_convert_type(w & jnp.uint32(0xFFFF0000), F32)
    lo = lax.bitcast_convert_type(w << 16, F32)
    return jnp.concatenate([hi, lo], axis=1)


def _router_kernel(x_ref, g_ref, rt_ref, su_ref, eye_ref, hb_ref, meta_ref, metac_ref, cnt_ref, carry_ref):
    @pl.when(pl.program_id(0) == 0)
    def _():
        carry_ref[...] = jnp.zeros_like(carry_ref)

    h = _rms(x_ref[...], g_ref[...])
    hb_ref[...] = _pack_halves(h)
    ne = rt_ref.shape[0]
    r = rt_ref[...]
    r_hi = r.astype(BF16).astype(F32)
    h_hi = h.astype(BF16)
    h_lo = (h - h_hi.astype(F32)).astype(BF16)
    both_r = _dot_nt(jnp.concatenate([r_hi, r - r_hi], axis=0).astype(BF16), h_hi)
    logits = both_r[:ne] + both_r[ne:] + _dot_nt(r_hi.astype(BF16), h_lo)
    ne, ct = logits.shape
    eio = lax.broadcasted_iota(jnp.int32, (ne, ct), 0).astype(F32)
    m1 = jnp.max(logits, axis=0, keepdims=True)
    i1 = jnp.min(jnp.where(logits == m1, eio, float(ne)), axis=0, keepdims=True)
    rest = jnp.where(eio == i1, -jnp.inf, logits)
    m2 = jnp.max(rest, axis=0, keepdims=True)
    i2 = jnp.min(jnp.where(rest == m2, eio, float(ne)), axis=0, keepdims=True)
    e2 = jnp.exp(m2 - m1)
    g1 = 1.0 / (1.0 + e2)
    g2 = e2 / (1.0 + e2)
    oh1 = (eio == i1).astype(F32)
    oh2 = (eio == i2).astype(F32)
    both = oh1 + oh2
    seen = carry_ref[:, 0:1] + _dot(both.astype(BF16), su_ref[...])
    r1 = jnp.sum(oh1 * seen, axis=0, keepdims=True)
    r2 = jnp.sum(oh2 * seen, axis=0, keepdims=True)
    meta = jnp.zeros((META_ROWS, ct), F32)
    for row, val in enumerate((i1, i2, r1, r2, g1, g2)):
        meta = jnp.where(eio == float(row), val, meta)
    meta_ref[...] = meta
    metac_ref[...] = lax.dot_general(eye_ref[...], meta, _NT, precision=lax.Precision.HIGHEST,
                                     preferred_element_type=F32)
    carry_ref[...] = carry_ref[...] + jnp.sum(both, axis=1, keepdims=True)
    cnt_ref[0] = carry_ref[...]


def _router(x, g, router):
    t, d = x.shape
    ct = _tile(t, MOE_TILE)
    nc = t // ct
    su = jnp.asarray(np.triu(np.ones((ct, ct), np.float32), 1), BF16)
    eye = jnp.eye(ct, dtype=F32)
    return pl.pallas_call(
        _router_kernel,
        grid=(nc,),
        in_specs=[pl.BlockSpec((ct, d), lambda i: (i, 0)),
                  pl.BlockSpec((1, d), lambda i: (0, 0)),
                  pl.BlockSpec((N_EXPERTS, d), lambda i: (0, 0)),
                  pl.BlockSpec((ct, ct), lambda i: (0, 0)),
                  pl.BlockSpec((ct, ct), lambda i: (0, 0))],
        out_specs=[pl.BlockSpec((ct, d // 2), lambda i: (i, 0)),
                   pl.BlockSpec((META_ROWS, ct), lambda i: (0, i)),
                   pl.BlockSpec((ct, META_ROWS), lambda i: (i, 0)),
                   pl.BlockSpec((1, N_EXPERTS, LANES), lambda i: (i, 0, 0))],
        out_shape=[jax.ShapeDtypeStruct((t, d // 2), jnp.uint32),
                   jax.ShapeDtypeStruct((META_ROWS, t), F32),
                   jax.ShapeDtypeStruct((t, META_ROWS), F32),
                   jax.ShapeDtypeStruct((nc, N_EXPERTS, LANES), F32)],
        scratch_shapes=[pltpu.VMEM((N_EXPERTS, LANES), F32)],
        compiler_params=_params("arbitrary"),
        name="moe_router",
    )(x, g.reshape(1, d), router.T, su, eye)


def _group_layout(cnt_after, t, rt):
    rows = 2 * t + N_EXPERTS * rt
    counts = cnt_after[:, :, 0].astype(jnp.int32)
    padded = (counts[-1] + rt - 1) // rt * rt
    off_end = jnp.cumsum(padded)
    off = (off_end - padded).astype(jnp.int32)
    tile_start = jnp.arange(rows // rt, dtype=jnp.int32) * rt
    tile_expert = jnp.minimum(_count_le(off_end, tile_start), N_EXPERTS - 1).astype(jnp.int32)
    tile_valid = (tile_start < off_end[-1]).astype(jnp.int32)
    return rows, off, counts, tile_expert, tile_valid


def _count_le(sorted_vals, x):
    return jnp.sum((sorted_vals[None, :] <= x[:, None]).astype(jnp.int32), axis=1)


def _work_lists(counts, off, rows, t, ct):
    nc = t // ct
    nrt = rows // ct
    before = jnp.concatenate([jnp.zeros((1, N_EXPERTS), jnp.int32), counts[:-1]], axis=0)
    chunk_start = (off[:, None] + before.T).reshape(-1)
    tile_start = jnp.arange(nrt, dtype=jnp.int32) * ct
    starts = jnp.sort(jnp.concatenate([chunk_start, tile_start]))
    ends = jnp.concatenate([starts[1:], jnp.array([rows], jnp.int32)])
    item_r = jnp.minimum(starts // ct, nrt - 1)
    item_c = (_count_le(chunk_start, starts) - 1) % nc
    first = lambda key: jnp.concatenate([jnp.ones((1,), jnp.int32), (key[1:] != key[:-1]).astype(jnp.int32)])
    dispatch = (item_r, item_c, starts, ends, first(item_r))
    order = jnp.argsort(item_c * (rows + 1) + starts)
    c2 = item_c[order]
    combine = (c2, item_r[order], starts[order], ends[order], first(c2))
    return dispatch, combine


def _positions(expert, rank, off_ref):
    base = jnp.zeros_like(rank)
    for e in range(N_EXPERTS):
        base = jnp.where(expert == float(e), off_ref[e].astype(F32), base)
    return base + rank


def _dispatch_kernel(r_ref, c_ref, s_ref, e_ref, first_ref, off_ref, meta_ref, h_ref, xs_ref, gate_ref):
    n = pl.program_id(0)
    rt, ct = xs_ref.shape[0], h_ref.shape[0]

    @pl.when(first_ref[n] == 1)
    def _():
        xs_ref[...] = jnp.zeros_like(xs_ref)
        gate_ref[...] = jnp.zeros_like(gate_ref)

    meta = meta_ref[...]
    pos0 = _positions(meta[0:1], meta[2:3], off_ref)
    pos1 = _positions(meta[1:2], meta[3:4], off_ref)
    sub = min(rt, ROUTE_SUB)
    for sb in range(rt // sub):
        base = r_ref[n] * rt + sb * sub

        @pl.when(jnp.logical_and(base < e_ref[n], base + sub > s_ref[n]))
        def _():
            row = lax.broadcasted_iota(jnp.int32, (sub, ct), 0) + base
            live = jnp.logical_and(row >= s_ref[n], row < e_ref[n])
            rowf = row.astype(F32)
            sel0 = jnp.logical_and(pos0 == rowf, live)
            sel1 = jnp.logical_and(pos1 == rowf, live)
            sel = jnp.logical_or(sel0, sel1).astype(F32).astype(BF16)
            rows = slice(sb * sub, (sb + 1) * sub)
            xs_ref[rows, :] += _pack_halves(_dot(sel, _unpack_halves(h_ref[...]).astype(BF16)))
            gate = jnp.where(sel0, meta[4:5], 0.0) + jnp.where(sel1, meta[5:6], 0.0)
            gate_ref[rows, :] += jnp.sum(gate, axis=1, keepdims=True)


def _dispatch(plan, off, meta, hb, rows, ct):
    t, d = hb.shape
    n_items = plan[0].shape[0]
    return pl.pallas_call(
        _dispatch_kernel,
        grid_spec=pltpu.PrefetchScalarGridSpec(
            num_scalar_prefetch=6,
            grid=(n_items,),
            in_specs=[pl.BlockSpec((META_ROWS, ct), lambda n, r, c, *_: (0, c[n])),
                      pl.BlockSpec((ct, d), lambda n, r, c, *_: (c[n], 0))],
            out_specs=[pl.BlockSpec((ct, d), lambda n, r, c, *_: (r[n], 0)),
                       pl.BlockSpec((ct, 1), lambda n, r, c, *_: (r[n], 0))]),
        out_shape=[jax.ShapeDtypeStruct((rows, d), jnp.uint32),
                   jax.ShapeDtypeStruct((rows, 1), F32)],
        compiler_params=_params("arbitrary"),
        name="moe_dispatch",
    )(*plan, off, meta, hb)


def _expert_kernel(te_ref, tv_ref, xs_ref, gate_ref, wg_ref, wu_ref, wd_ref, y_ref, acc_ref):
    r = pl.program_id(0)
    f = pl.program_id(1)

    @pl.when(f == 0)
    def _():
        acc_ref[...] = jnp.zeros_like(acc_ref)

    @pl.when(tv_ref[r] == 1)
    def _():
        _run_swiglu(acc_ref, _unpack_halves(xs_ref[...]).astype(BF16), lambda c: wg_ref[0, :, c],
                    lambda c: wu_ref[0, :, c], lambda c: wd_ref[0, c, :], wg_ref.shape[2])

    @pl.when(f == pl.num_programs(1) - 1)
    def _():
        y_ref[...] = _pack_halves(acc_ref[...] * gate_ref[...])


def _experts(tile_expert, tile_valid, xs, gate, wg, wu, wd, ct):
    rows, dp = xs.shape
    d = 2 * dp
    fdim = wg.shape[2]
    tf = _tile(fdim, 1792)
    nf = fdim // tf

    def wcol(r, f, te, tv):
        return (te[r], 0, jnp.where(tv[r] == 1, f, nf - 1))

    def wrow(r, f, te, tv):
        return (te[r], jnp.where(tv[r] == 1, f, nf - 1), 0)

    return pl.pallas_call(
        _expert_kernel,
        grid_spec=pltpu.PrefetchScalarGridSpec(
            num_scalar_prefetch=2,
            grid=(rows // ct, nf),
            in_specs=[pl.BlockSpec((ct, dp), lambda r, f, te, tv: (r, 0)),
                      pl.BlockSpec((ct, 1), lambda r, f, te, tv: (r, 0)),
                      pl.BlockSpec((1, d, tf), wcol),
                      pl.BlockSpec((1, d, tf), wcol),
                      pl.BlockSpec((1, tf, d), wrow)],
            out_specs=pl.BlockSpec((ct, dp), lambda r, f, te, tv: (r, 0)),
            scratch_shapes=[pltpu.VMEM((ct, d), F32)]),
        out_shape=jax.ShapeDtypeStruct((rows, dp), jnp.uint32),
        compiler_params=_params("arbitrary", "arbitrary"),
        name="moe_experts",
    )(tile_expert, tile_valid, xs, gate, wg, wu, wd)


def _combine_kernel(c_ref, r_ref, s_ref, e_ref, first_ref, off_ref, x_ref, metac_ref, y_ref, o_ref):
    n = pl.program_id(0)
    ct, rt = x_ref.shape[0], y_ref.shape[0]

    @pl.when(first_ref[n] == 1)
    def _():
        o_ref[...] = x_ref[...]

    metac = metac_ref[...]
    pos0 = _positions(metac[:, 0:1], metac[:, 2:3], off_ref)
    pos1 = _positions(metac[:, 1:2], metac[:, 3:4], off_ref)
    sub = min(rt, COMBINE_SUB)
    for sb in range(rt // sub):
        base = r_ref[n] * rt + sb * sub

        @pl.when(jnp.logical_and(base < e_ref[n], base + sub > s_ref[n]))
        def _():
            row = lax.broadcasted_iota(jnp.int32, (ct, sub), 1) + base
            live = jnp.logical_and(row >= s_ref[n], row < e_ref[n])
            rowf = row.astype(F32)
            sel = jnp.logical_and(jnp.logical_or(pos0 == rowf, pos1 == rowf), live).astype(F32).astype(BF16)
            o_ref[...] += _dot(sel, _unpack_halves(y_ref[sb * sub:(sb + 1) * sub, :]).astype(BF16))


def _combine(plan, off, x, metac, y, ct):
    t, d = x.shape
    n_items = plan[0].shape[0]
    return pl.pallas_call(
        _combine_kernel,
        grid_spec=pltpu.PrefetchScalarGridSpec(
            num_scalar_prefetch=6,
            grid=(n_items,),
            in_specs=[pl.BlockSpec((ct, d), lambda n, c, r, *_: (c[n], 0)),
                      pl.BlockSpec((ct, META_ROWS), lambda n, c, r, *_: (c[n], 0)),
                      pl.BlockSpec((ct, d // 2), lambda n, c, r, *_: (r[n], 0))],
            out_specs=pl.BlockSpec((ct, d), lambda n, c, r, *_: (c[n], 0))),
        out_shape=jax.ShapeDtypeStruct((t, d), F32),
        compiler_params=_params("arbitrary"),
        name="moe_combine",
    )(*plan, off, x, metac, y)


def _cast_kernel(w_ref, o_ref):
    o_ref[...] = w_ref[...].astype(o_ref.dtype)


CAST_BLOCK_BYTES = 8 * 1024 * 1024


def _layer_bf16(w, layer):
    shape = w.shape[1:]
    cols = shape[-1]
    rows = math.prod(shape[:-1])
    tr = rows
    while tr * cols * 4 > CAST_BLOCK_BYTES and tr % 16 == 0:
        tr //= 2
    out = pl.pallas_call(
        _cast_kernel,
        grid=(rows // tr,),
        in_specs=[pl.BlockSpec((None, tr, cols), lambda i: (layer, i, 0))],
        out_specs=pl.BlockSpec((tr, cols), lambda i: (i, 0)),
        out_shape=jax.ShapeDtypeStruct((rows, cols), BF16),
        compiler_params=_params("parallel"),
        name="weights_bf16",
    )(w.reshape(w.shape[0], rows, cols))
    return out.reshape(shape)


SC_CORES = 2
SC_SUBCORES = 16
SC_CHUNK = 128


def _sc_worker_chunks(total):
    per = total // (SC_CORES * SC_SUBCORES)
    wid = lax.axis_index("s") * SC_CORES + lax.axis_index("c")
    return wid * per, per // SC_CHUNK


def _sc_mesh():
    return plsc.VectorSubcoreMesh(core_axis_name="c", subcore_axis_name="s")


def _sc_gather(table, idx):
    v, d = table.shape
    b = idx.shape[0]

    def body(table_hbm, idx_hbm, out_hbm, idx_v, rows_v, sem):
        base, chunks = _sc_worker_chunks(b)

        @pl.loop(0, chunks)
        def _(j):
            start = base + j * SC_CHUNK
            pltpu.sync_copy(idx_hbm.at[pl.ds(start, SC_CHUNK)], idx_v)
            pltpu.async_copy(table_hbm.at[idx_v], rows_v, sem).wait()
            pltpu.sync_copy(rows_v, out_hbm.at[pl.ds(start, SC_CHUNK)])

    return pl.kernel(
        body,
        out_type=jax.ShapeDtypeStruct((b, d), table.dtype),
        mesh=_sc_mesh(),
        scratch_types=[pltpu.VMEM((SC_CHUNK,), jnp.int32),
                       pltpu.VMEM((SC_CHUNK, d), table.dtype),
                       pltpu.SemaphoreType.DMA],
        name="sc_row_gather",
    )(table, idx)


def _sc_scatter_rows(src, idx, n):
    t, d = src.shape
    b = idx.shape[0]

    def body(src_hbm, idx_hbm, out_hbm, idx_v, rows_v):
        base, chunks = _sc_worker_chunks(b)

        @pl.loop(0, chunks)
        def _(j):
            start = base + j * SC_CHUNK
            first = lax.rem(start, t)
            pltpu.sync_copy(idx_hbm.at[pl.ds(start, SC_CHUNK)], idx_v)
            pltpu.sync_copy(src_hbm.at[pl.ds(first, SC_CHUNK)], rows_v)
            pltpu.sync_copy(rows_v, out_hbm.at[idx_v])

    return pl.kernel(
        body,
        out_type=jax.ShapeDtypeStruct((n, d), src.dtype),
        mesh=_sc_mesh(),
        scratch_types=[pltpu.VMEM((SC_CHUNK,), jnp.int32), pltpu.VMEM((SC_CHUNK, d), src.dtype)],
        name="sc_row_scatter",
    )(src, idx)


def _weighted_add_kernel(x_ref, metac_ref, y0_ref, y1_ref, g_ref, o_ref, *, normed):
    gates = metac_ref[...]
    out = (x_ref[...] + gates[:, 4:5] * _unpack_halves(y0_ref[...])
           + gates[:, 5:6] * _unpack_halves(y1_ref[...]))
    o_ref[...] = _rms(out, g_ref[...]) if normed else out


def _weighted_add(x, metac, ysel, out_gain=None):
    t, d = x.shape
    tm = _tile(t, 512)
    nt = t // tm
    gain = jnp.ones((d,), F32) if out_gain is None else out_gain
    return pl.pallas_call(
        functools.partial(_weighted_add_kernel, normed=out_gain is not None),
        grid=(nt,),
        in_specs=[pl.BlockSpec((tm, d), lambda i: (i, 0)),
                  pl.BlockSpec((tm, META_ROWS), lambda i: (i, 0)),
                  pl.BlockSpec((tm, d // 2), lambda i: (i, 0)),
                  pl.BlockSpec((tm, d // 2), lambda i: (nt + i, 0)),
                  pl.BlockSpec((1, d), lambda i: (0, 0))],
        out_specs=pl.BlockSpec((tm, d), lambda i: (i, 0)),
        out_shape=jax.ShapeDtypeStruct((t, d), F32),
        compiler_params=_params("parallel"),
        name="moe_weighted_add",
    )(x, metac, ysel, ysel, gain.reshape(1, d))


def _moe(x, g, router, wg, wu, wd, out_gain=None):
    t = x.shape[0]
    ct = _tile(t, MOE_TILE)
    hb, meta, metac, cnt = _router(x, g, router)
    sc_rows = SC_CORES * SC_SUBCORES * SC_CHUNK
    rows, off, counts, tile_expert, tile_valid = _group_layout(cnt, t, ct)
    if (2 * t) % sc_rows == 0 and rows % sc_rows == 0:
        base = sum(jnp.where(meta[0:2] == float(e), off[e], 0) for e in range(N_EXPERTS))
        pos = (base + meta[2:4].astype(jnp.int32)).reshape(-1)
        xs = _sc_scatter_rows(hb, pos, rows)
        y = _experts(tile_expert, tile_valid, xs, jnp.ones((rows, 1), F32), wg, wu, wd, ct)
        return _weighted_add(x, metac, _sc_gather(y, pos), out_gain)
    dispatch, combine = _work_lists(counts, off, rows, t, ct)
    xs, gate = _dispatch(dispatch, off, meta, hb, rows, ct)
    y = _experts(tile_expert, tile_valid, xs, gate, wg, wu, wd, ct)
    out = _combine(combine, off, x, metac, y, ct)
    return out if out_gain is None else _final_norm(out, out_gain)


def _final_norm_kernel(x_ref, g_ref, o_ref):
    o_ref[...] = _rms(x_ref[...], g_ref[...])


def _final_norm(x, g):
    t, d = x.shape
    tm = _tile(t, 1024)
    return pl.pallas_call(
        _final_norm_kernel,
        grid=(t // tm,),
        in_specs=[pl.BlockSpec((tm, d), lambda i: (i, 0)), pl.BlockSpec((1, d), lambda i: (0, 0))],
        out_specs=pl.BlockSpec((tm, d), lambda i: (i, 0)),
        out_shape=jax.ShapeDtypeStruct((t, d), F32),
        compiler_params=_params("parallel"),
        name="final_norm",
    )(x, g.reshape(1, d))


def kernel(x, rel_bias, attn_norm, ffn_norm, final_norm, swa_w_qkv, swa_b_qkv, swa_sinks, swa_w_o, sb_w_qkv, sb_w_o, mla_w_down, mla_q_norm, mla_w_uq, mla_kv_norm, mla_w_ukv, mla_w_o, ffn_w_gate, ffn_w_up, ffn_w_down, moe_router, moe_w_gate, moe_w_up, moe_w_down):
    batch, seq, d = x.shape
    depth = attn_norm.shape[0]
    xt = x.reshape(batch * seq, d)
    swa_bias = _swa_bias(rel_bias)
    for i in range(depth):
        mixer, j = i % N_MIXERS, i // N_MIXERS
        if mixer == 0:
            w, b, cs = _swa_weights(swa_w_qkv[j], swa_b_qkv[j])
            qkv = _norm_proj(xt, attn_norm[i], w, b, cs, "swa_proj")
            o = _swa_attention(qkv, swa_bias, swa_sinks[j], batch, seq)
            wo = swa_w_o[j].astype(BF16)
        elif mixer == 1:
            n = sb_w_qkv.shape[2] // 3
            cs = jnp.concatenate([jnp.full((n,), HEAD_DIM ** -0.5 * LOG2E, F32), jnp.ones((n,), F32)])
            qk, vt = _norm_proj_vt(xt, attn_norm[i], sb_w_qkv[j].astype(BF16), cs, n, seq, "sb_proj")
            o = _sb_attention(qk, vt, batch, seq)
            wo = sb_w_o[j].astype(BF16)
        else:
            a, vt = _mla_proj(xt, attn_norm[i], mla_w_down[j], mla_q_norm[j], mla_w_uq[j],
                              mla_kv_norm[j], mla_w_ukv[j], seq)
            o = _mla_attention(a, vt, batch, seq)
            wo = mla_w_o[j].astype(BF16)
        f = i // 2
        if i % 2 == 0:
            xt = _ffn(xt, o, wo, ffn_norm[i], _layer_bf16(ffn_w_gate, f), _layer_bf16(ffn_w_up, f),
                      _layer_bf16(ffn_w_down, f))
        else:
            last = final_norm if i == depth - 1 else None
            xt = _out_proj(xt, o, wo, "mixer_out")
            xt = _moe(xt, ffn_norm[i], moe_router[f], _layer_bf16(moe_w_gate, f), _layer_bf16(moe_w_up, f),
                      _layer_bf16(moe_w_down, f), last)
    if depth % 2 == 1:
        xt = _final_norm(xt, final_norm)
    return xt.reshape(batch, seq, d)
```

```python
import functools
import math

import numpy as np
import jax
import jax.numpy as jnp
from jax import lax
from jax.experimental import pallas as pl
from jax.experimental.pallas import tpu as pltpu
from jax.experimental.pallas import tpu_sc as plsc

F32 = jnp.float32
BF16 = jnp.bfloat16

N_MIXERS = 3
RMS_EPS = 1e-6
SWA_HEADS = 16
SWA_KV_HEADS = 2
HEAD_DIM = 64
SWA_BLOCK = 128
SWA_WINDOW = 128
REL_BUCKETS = 32
REL_MAX_DIST = 128
SB_HEADS = 16
MLA_HEADS = 16
MLA_NOPE_DIM = 64
MLA_ROPE_DIM = 32
MLA_Q_RANK = 384
MLA_KV_RANK = 256
ROPE_THETA = 10000.0
N_EXPERTS = 8

LANES = 128
NEG = -1e30
LOG2E = math.log2(math.e)
VMEM_LIMIT = 56 * 1024 * 1024

_NT = (((1,), (1,)), ((), ()))


def _dot(a, b):
    return jnp.dot(a, b, preferred_element_type=F32)


def _dot_nt(a, b):
    return lax.dot_general(a, b, _NT, preferred_element_type=F32)


def _rms(x, g):
    return x * lax.rsqrt(jnp.mean(x * x, axis=-1, keepdims=True) + RMS_EPS) * g


def _params(*sem):
    return pltpu.CompilerParams(dimension_semantics=sem, vmem_limit_bytes=VMEM_LIMIT)


def _tile(n, pref):
    return pref if n % pref == 0 else n


def _norm_proj_kernel(x_ref, g_ref, w_ref, b_ref, cs_ref, o_ref):
    h = _rms(x_ref[...], g_ref[...]).astype(BF16)
    y = (_dot(h, w_ref[...]) + b_ref[...]) * cs_ref[...]
    o_ref[...] = y.astype(o_ref.dtype)


def _norm_proj(x, g, w, b, colscale, name):
    t, d = x.shape
    n = w.shape[1]
    tm = _tile(t, 512)
    return pl.pallas_call(
        _norm_proj_kernel,
        grid=(t // tm,),
        in_specs=[pl.BlockSpec((tm, d), lambda i: (i, 0)),
                  pl.BlockSpec((1, d), lambda i: (0, 0)),
                  pl.BlockSpec((d, n), lambda i: (0, 0)),
                  pl.BlockSpec((1, n), lambda i: (0, 0)),
                  pl.BlockSpec((1, n), lambda i: (0, 0))],
        out_specs=pl.BlockSpec((tm, n), lambda i: (i, 0)),
        out_shape=jax.ShapeDtypeStruct((t, n), BF16),
        compiler_params=_params("parallel"),
        name=name,
    )(x, g.reshape(1, d), w, b.reshape(1, n), colscale.reshape(1, n))


ATT_TQ = 512
ATT_TK = 256


def _store_vt(vt_ref, v, tk):
    for s in range(v.shape[0] // tk):
        vt_ref[s] = v[s * tk:(s + 1) * tk, :].T.astype(vt_ref.dtype)


def _norm_proj_vt_kernel(x_ref, g_ref, w_ref, cs_ref, o_ref, vt_ref, *, tk):
    h = _rms(x_ref[...], g_ref[...]).astype(BF16)
    y = _dot(h, w_ref[...])
    n = o_ref.shape[1]
    o_ref[...] = (y[:, :n] * cs_ref[...]).astype(o_ref.dtype)
    _store_vt(vt_ref, y[:, n:], tk)


def _norm_proj_vt(x, g, w, colscale, nv, seq, name):
    t, d = x.shape
    n = w.shape[1] - nv
    tm = _tile(seq, 512)
    tk = _tile(seq, ATT_TK)
    per = tm // tk
    return pl.pallas_call(
        functools.partial(_norm_proj_vt_kernel, tk=tk),
        grid=(t // tm,),
        in_specs=[pl.BlockSpec((tm, d), lambda i: (i, 0)),
                  pl.BlockSpec((1, d), lambda i: (0, 0)),
                  pl.BlockSpec((d, n + nv), lambda i: (0, 0)),
                  pl.BlockSpec((1, n), lambda i: (0, 0))],
        out_specs=[pl.BlockSpec((tm, n), lambda i: (i, 0)),
                   pl.BlockSpec((per, nv, tk), lambda i: (i, 0, 0))],
        out_shape=[jax.ShapeDtypeStruct((t, n), BF16),
                   jax.ShapeDtypeStruct((t // tk, nv, tk), BF16)],
        compiler_params=_params("parallel"),
        name=name,
    )(x, g.reshape(1, d), w, colscale.reshape(1, n))


def _out_proj_kernel(x_ref, o_ref, w_ref, y_ref):
    y_ref[...] = x_ref[...] + _dot(o_ref[...], w_ref[...])


def _out_proj(x, o, w, name):
    t, d = x.shape
    k = o.shape[1]
    tm = _tile(t, 512)
    return pl.pallas_call(
        _out_proj_kernel,
        grid=(t // tm,),
        in_specs=[pl.BlockSpec((tm, d), lambda i: (i, 0)),
                  pl.BlockSpec((tm, k), lambda i: (i, 0)),
                  pl.BlockSpec((k, d), lambda i: (0, 0))],
        out_specs=pl.BlockSpec((tm, d), lambda i: (i, 0)),
        out_shape=jax.ShapeDtypeStruct((t, d), F32),
        compiler_params=_params("parallel"),
        name=name,
    )(x, o, w)


def _t5_bucket_table():
    qi = np.arange(SWA_BLOCK)[:, None]
    kj = np.arange(2 * SWA_BLOCK)[None, :]
    dist = qi + SWA_BLOCK - kj
    d0 = np.maximum(dist, 0)
    max_exact = REL_BUCKETS // 2
    d = np.maximum(d0, 1).astype(np.float32)
    large = max_exact + (np.log(d / max_exact) / math.log(REL_MAX_DIST / max_exact)
                         * (REL_BUCKETS - max_exact)).astype(np.int32)
    large = np.minimum(large, REL_BUCKETS - 1)
    bucket = np.where(d0 < max_exact, d0, large)
    band = (dist >= 0) & (dist < SWA_WINDOW)
    return np.where(band, bucket, -1).astype(np.int32)


def _swa_bias_kernel(rel_ref, bucket_ref, o_ref):
    h = pl.program_id(0)
    bucket = bucket_ref[...]
    acc = jnp.full(bucket.shape, NEG, F32)
    for b in range(REL_BUCKETS):
        acc = jnp.where(bucket == b, rel_ref[b, h], acc)
    o_ref[0] = acc


def _swa_bias(rel_bias):
    bucket = jnp.asarray(_t5_bucket_table())
    q, k = bucket.shape
    return pl.pallas_call(
        _swa_bias_kernel,
        grid=(SWA_HEADS,),
        in_specs=[pl.BlockSpec(memory_space=pltpu.SMEM),
                  pl.BlockSpec((q, k), lambda h: (0, 0))],
        out_specs=pl.BlockSpec((1, q, k), lambda h: (h, 0, 0)),
        out_shape=jax.ShapeDtypeStruct((SWA_HEADS, q, k), F32),
        compiler_params=_params("arbitrary"),
        name="swa_bias",
    )(rel_bias, bucket)


def _swa_kernel(sink_ref, q_ref, kc_ref, kp_ref, vc_ref, vp_ref, bias_ref, o_ref, *, tq):
    i = pl.program_id(1)
    g = pl.program_id(2)
    blk = SWA_BLOCK
    heads = bias_ref.shape[0]
    lo_half = lax.broadcasted_iota(jnp.int32, (blk, LANES), 1) < HEAD_DIM
    in_prev = lax.broadcasted_iota(jnp.int32, (heads * blk, 2 * blk), 1) < blk
    bias = bias_ref[...].reshape(heads * blk, 2 * blk)

    def window(cur_ref, prev_ref, sb):
        if sb == 0:
            return jnp.concatenate([prev_ref[...], cur_ref[0:blk, :]], axis=0)
        return cur_ref[(sb - 1) * blk:(sb + 1) * blk, :]

    def scores(sb, _):
        q = q_ref[sb * blk:(sb + 1) * blk, :]
        parts = []
        for h in range(heads):
            qp = q[:, (h // 2) * LANES:(h // 2 + 1) * LANES]
            keep = lo_half if h % 2 == 0 else jnp.logical_not(lo_half)
            parts.append(jnp.where(keep, qp, jnp.zeros_like(qp)))
        s = _dot_nt(jnp.concatenate(parts, axis=0), window(kc_ref, kp_ref, sb)) + bias
        if sb == 0:
            s = jnp.where(jnp.logical_and(in_prev, i == 0), NEG, s)
        return s

    def softmax(sb, s):
        ps, inv = [], []
        for h in range(heads):
            sh = s[h * blk:(h + 1) * blk]
            sink = sink_ref[g * heads + h]
            m = jnp.maximum(jnp.max(sh, axis=1, keepdims=True), sink)
            e = jnp.exp(sh - m)
            inv.append(1.0 / (jnp.sum(e, axis=1, keepdims=True) + jnp.exp(sink - m)))
            ps.append(e.astype(BF16))
        return jnp.concatenate(ps, axis=0), inv

    def values(sb, val):
        p, inv = val
        o = _dot(p, window(vc_ref, vp_ref, sb))
        outs = [jnp.where(lo_half, o[h * blk:(h + 1) * blk] * inv[h], o[(h + 1) * blk:(h + 2) * blk] * inv[h + 1])
                for h in range(0, heads, 2)]
        o_ref[sb * blk:(sb + 1) * blk, :] = jnp.concatenate(outs, axis=1).astype(o_ref.dtype)

    _staggered(list(range(tq // blk)), (scores, softmax, values))


def _swa_attention(qkv, bias, sinks, batch, seq):
    t = qkv.shape[0]
    tq = _tile(seq, 512)
    nt = seq // tq
    group = SWA_HEADS // SWA_KV_HEADS
    gw = group * HEAD_DIM
    kcol = SWA_HEADS * HEAD_DIM // LANES
    vcol = kcol + SWA_KV_HEADS
    per = tq // SWA_BLOCK

    def prev_idx(b, i, g):
        return jnp.maximum(b * (seq // SWA_BLOCK) + i * per - 1, 0)

    return pl.pallas_call(
        functools.partial(_swa_kernel, tq=tq),
        grid=(batch, nt, SWA_KV_HEADS),
        in_specs=[pl.BlockSpec(memory_space=pltpu.SMEM),
                  pl.BlockSpec((tq, gw), lambda b, i, g: (b * nt + i, g)),
                  pl.BlockSpec((tq, LANES), lambda b, i, g: (b * nt + i, kcol + g)),
                  pl.BlockSpec((SWA_BLOCK, LANES), lambda b, i, g: (prev_idx(b, i, g), kcol + g)),
                  pl.BlockSpec((tq, LANES), lambda b, i, g: (b * nt + i, vcol + g)),
                  pl.BlockSpec((SWA_BLOCK, LANES), lambda b, i, g: (prev_idx(b, i, g), vcol + g)),
                  pl.BlockSpec((group, SWA_BLOCK, 2 * SWA_BLOCK), lambda b, i, g: (g, 0, 0))],
        out_specs=pl.BlockSpec((tq, gw), lambda b, i, g: (b * nt + i, g)),
        out_shape=jax.ShapeDtypeStruct((t, SWA_HEADS * HEAD_DIM), BF16),
        compiler_params=_params("parallel", "parallel", "arbitrary"),
        name="swa_attention",
    )(sinks, qkv, qkv, qkv, qkv, qkv, bias)


def _swa_weights(w_qkv, b_qkv):
    nq = SWA_HEADS * HEAD_DIM
    kv = SWA_KV_HEADS * HEAD_DIM
    dup = np.concatenate([np.tile(np.arange(HEAD_DIM), 2) + h * HEAD_DIM for h in range(SWA_KV_HEADS)])
    cols = np.concatenate([np.arange(nq), nq + dup, nq + kv + dup])
    scale = np.concatenate([np.full(nq, HEAD_DIM ** -0.5), np.ones(2 * dup.size)]).astype(np.float32)
    return w_qkv[:, cols].astype(BF16), b_qkv[cols], jnp.asarray(scale)


def _staggered(units, stages):
    vals = list(units)
    for t in range(len(units) + len(stages) - 1):
        for k, stage in enumerate(stages):
            u = t - k
            if 0 <= u < len(units):
                vals[u] = stage(units[u], vals[u])


def _score_ahead(units, next_units, stages, s_ref):
    rest = stages[1:]
    vals = [None] * len(units)
    for t in range(len(units) + len(rest) - 1):
        ahead = stages[0](next_units[t], None) if next_units is not None and t < len(units) else None
        for k, stage in enumerate(rest):
            u = t - k
            if 0 <= u < len(units):
                if k == 0:
                    lanes = units[u][2]
                    vals[u] = stage(units[u], s_ref[u, :, :lanes.stop - lanes.start])
                    if ahead is not None:
                        s_ref[u] = ahead
                else:
                    vals[u] = stage(units[u], vals[u])


def _causal_sweep(i, tq, tk, stages, strict, s_ref, heads=2):
    nsub = tq // tk

    def group(j):
        return [(j * nsub + sb, h, slice(0, tq), None) for sb in reversed(range(nsub)) for h in range(heads)]

    def sweep(has_history):
        diagonal = []
        for sb in reversed(range(nsub)):
            nl = tq - sb * tk
            key = lax.broadcasted_iota(jnp.int32, (tk, nl), 0)
            qry = lax.broadcasted_iota(jnp.int32, (tk, nl), 1)
            mask = key < qry if strict else key <= qry
            diagonal += [(i * nsub + sb, h, slice(sb * tk, tq), mask) for h in range(heads)]
        for u, unit in enumerate(diagonal):
            s_ref[u, :, :tq - unit[2].start] = stages[0](unit, None)
        if not has_history:
            _score_ahead(diagonal, None, stages, s_ref)
            return
        _score_ahead(diagonal, group(i - 1), stages, s_ref)

        def body(n, carry):
            _score_ahead(group(i - 1 - n), group(i - 2 - n), stages, s_ref)
            return carry

        lax.fori_loop(0, i - 1, body, 0)
        _score_ahead(group(0), None, stages, s_ref)

    pl.when(i == 0)(lambda: sweep(False))
    pl.when(i > 0)(lambda: sweep(True))


def _transposed(x):
    return x.astype(F32).T.astype(x.dtype)


def _split_pair(q):
    lo_half = lax.broadcasted_iota(jnp.int32, q.shape, 1) < HEAD_DIM
    zero = jnp.zeros_like(q)
    return jnp.where(lo_half, q, zero), jnp.where(lo_half, zero, q)


SB_STEP_HEADS = 4


def _sb_kernel(q_ref, k_ref, vt_ref, u_ref, o_ref, acc_ref, c_ref, s_ref, *, tq, tk, heads):
    i = pl.program_id(2)
    qts = [_transposed(qm) for pair in range(heads // 2)
           for qm in _split_pair(q_ref[:, pair * LANES:(pair + 1) * LANES])]
    u = u_ref[...]
    acc_ref[...] = jnp.zeros_like(acc_ref)
    c_ref[...] = jnp.zeros_like(c_ref)

    def scores(unit, _):
        jb, h, lanes, _ = unit
        kt = k_ref[pl.ds(pl.multiple_of(jb * tk, tk), tk), (h // 2) * LANES:(h // 2 + 1) * LANES]
        return _dot(kt, qts[h][:, lanes])

    def log_fail(unit, z):
        neg_abs = lax.bitcast_convert_type(lax.bitcast_convert_type(z, jnp.uint32) | jnp.uint32(1 << 31), F32)
        sp = jnp.maximum(z, 0.0) + jnp.log2(1.0 + jnp.exp2(neg_abs))
        if unit[3] is not None:
            sp = jnp.where(unit[3], sp, 0.0)
        return z, sp.astype(BF16)

    def suffix_sum(unit, val):
        z, sp = val
        return z, _dot(u, sp)

    def weights(unit, val):
        _, h, lanes, mask = unit
        z, r = val
        c = c_ref[h:h + 1, lanes]
        la = z - r - c
        if mask is not None:
            la = jnp.where(mask, la, NEG)
        c_ref[h:h + 1, lanes] = c + r[0:1, :]
        return jnp.exp2(la).astype(BF16)

    def values(unit, a):
        jb, h, lanes, _ = unit
        rows = slice(h * HEAD_DIM, (h + 1) * HEAD_DIM)
        acc_ref[rows, lanes] += _dot(vt_ref[jb, rows, :], a)

    _causal_sweep(i, tq, tk, (scores, log_fail, suffix_sum, weights, values), True, s_ref, heads)
    o_ref[...] = acc_ref[...].T.astype(o_ref.dtype)


def _sb_attention(qk, vt, batch, seq):
    t = qk.shape[0]
    tq = _tile(seq, ATT_TQ)
    tk = vt.shape[2]
    nq = seq // tq
    groups = SB_HEADS // SB_STEP_HEADS
    gw = SB_STEP_HEADS * HEAD_DIM
    u = jnp.asarray(np.triu(np.ones((tk, tk), np.float32)), BF16)
    return pl.pallas_call(
        functools.partial(_sb_kernel, tq=tq, tk=tk, heads=SB_STEP_HEADS),
        grid=(batch, groups, nq),
        in_specs=[pl.BlockSpec((tq, gw), lambda b, p, i: (b * nq + i, p)),
                  pl.BlockSpec((seq, gw), lambda b, p, i: (b, groups + p)),
                  pl.BlockSpec((seq // tk, gw, tk), lambda b, p, i: (b, p, 0)),
                  pl.BlockSpec((tk, tk), lambda b, p, i: (0, 0))],
        out_specs=pl.BlockSpec((tq, gw), lambda b, p, i: (b * nq + i, p)),
        out_shape=jax.ShapeDtypeStruct((t, SB_HEADS * HEAD_DIM), BF16),
        scratch_shapes=[pltpu.VMEM((gw, tq), F32), pltpu.VMEM((8, tq), F32),
                        pltpu.VMEM((SB_STEP_HEADS * (tq // tk), tk, tq), F32)],
        compiler_params=_params("parallel", "parallel", "arbitrary"),
        name="sb_attention",
    )(qk, qk, vt, u)


MLA_QN = MLA_HEADS * MLA_NOPE_DIM
MLA_QR = MLA_HEADS * MLA_ROPE_DIM
MLA_OUT = 2 * MLA_QN + MLA_QR + LANES


def _mla_proj_kernel(x_ref, g_ref, wd_ref, qg_ref, wuq_ref, kvg_ref, wukv_ref, cs_ref, o_ref, vt_ref,
                     *, scale, tk):
    h = _rms(x_ref[...], g_ref[...]).astype(BF16)
    c = _dot(h, wd_ref[...])
    cos = cs_ref[:, :LANES]
    sin = cs_ref[:, LANES:]
    cq = _rms(c[:, :MLA_Q_RANK], qg_ref[...]).astype(BF16)
    q = _dot(cq, wuq_ref[...])
    o_ref[:, :MLA_QN] = (q[:, :MLA_QN] * scale).astype(o_ref.dtype)
    for m in range(MLA_QR // LANES):
        a = MLA_QN + m * LANES
        rot = q[:, a:a + LANES] * cos + q[:, a + MLA_QR:a + MLA_QR + LANES] * sin
        o_ref[:, a:a + LANES] = (rot * scale).astype(o_ref.dtype)
    kv0 = MLA_Q_RANK + MLA_KV_RANK
    ckv = _rms(c[:, MLA_Q_RANK:kv0], kvg_ref[...]).astype(BF16)
    kv = _dot(ckv, wukv_ref[...])
    kn0 = MLA_QN + MLA_QR
    o_ref[:, kn0:kn0 + MLA_QN] = kv[:, :MLA_QN].astype(o_ref.dtype)
    kr = c[:, kv0:kv0 + LANES] * cos + c[:, kv0 + LANES:kv0 + 2 * LANES] * sin
    o_ref[:, MLA_OUT - LANES:] = kr.astype(o_ref.dtype)
    _store_vt(vt_ref, kv[:, MLA_QN:], tk)


def _mla_rope_layout():
    half = MLA_ROPE_DIM // 2
    per = LANES // MLA_ROPE_DIM
    dq = MLA_NOPE_DIM + MLA_ROPE_DIM
    q_nope = np.concatenate([h * dq + np.arange(MLA_NOPE_DIM) for h in range(MLA_HEADS)])
    q_rope = np.zeros(MLA_QR, np.int64)
    q_swap = np.zeros(MLA_QR, np.int64)
    for h in range(MLA_HEADS):
        m, r = divmod(h, per)
        x1 = h * dq + MLA_NOPE_DIM + np.arange(half)
        x2 = x1 + half
        first = m * LANES + r * half + np.arange(half)
        second = first + LANES // 2
        q_rope[first], q_rope[second] = x1, x2
        q_swap[first], q_swap[second] = x2, x1
    k_rope = np.zeros(LANES, np.int64)
    k_swap = np.zeros(LANES, np.int64)
    base = MLA_Q_RANK + MLA_KV_RANK
    for r in range(per):
        first = r * half + np.arange(half)
        second = first + LANES // 2
        k_rope[first], k_rope[second] = base + np.arange(half), base + half + np.arange(half)
        k_swap[first], k_swap[second] = base + half + np.arange(half), base + np.arange(half)
    dkv = MLA_NOPE_DIM + HEAD_DIM
    k_nope = np.concatenate([h * dkv + np.arange(MLA_NOPE_DIM) for h in range(MLA_HEADS)])
    v = k_nope + MLA_NOPE_DIM
    down = np.concatenate([np.arange(base), k_rope, k_swap])
    return np.concatenate([q_nope, q_rope, q_swap]), np.concatenate([k_nope, v]), down


def _mla_rope_tables(seq):
    half = MLA_ROPE_DIM // 2
    inv = ROPE_THETA ** (-jnp.arange(0, MLA_ROPE_DIM, 2, dtype=F32) / MLA_ROPE_DIM)
    ang = jnp.arange(seq, dtype=F32)[:, None] * inv[None, :]
    cos, sin = jnp.cos(ang), jnp.sin(ang)
    reps = LANES // 2 // half
    cos_t = jnp.tile(cos, (1, 2 * reps))
    sin_t = jnp.concatenate([jnp.tile(-sin, (1, reps)), jnp.tile(sin, (1, reps))], axis=1)
    return jnp.concatenate([cos_t, sin_t], axis=1)


def _mla_proj(x, g, w_down, q_norm, w_uq, kv_norm, w_ukv, seq):
    t, d = x.shape
    tm = _tile(seq, 512)
    tk = _tile(seq, ATT_TK)
    ns = seq // tm
    uq_cols, ukv_cols, down_cols = _mla_rope_layout()
    wd = w_down[:, down_cols].astype(BF16)
    wuq = w_uq[:, uq_cols].astype(BF16)
    wukv = w_ukv[:, ukv_cols].astype(BF16)
    cs = _mla_rope_tables(seq)
    scale = (MLA_NOPE_DIM + MLA_ROPE_DIM) ** -0.5 * LOG2E
    full = lambda a: pl.BlockSpec(a.shape, lambda i: (0, 0))
    g2, qg, kvg = g.reshape(1, d), q_norm.reshape(1, -1), kv_norm.reshape(1, -1)
    return pl.pallas_call(
        functools.partial(_mla_proj_kernel, scale=scale, tk=tk),
        grid=(t // tm,),
        in_specs=[pl.BlockSpec((tm, d), lambda i: (i, 0)), full(g2), full(wd), full(qg), full(wuq),
                  full(kvg), full(wukv), pl.BlockSpec((tm, 2 * LANES), lambda i: (i % ns, 0))],
        out_specs=[pl.BlockSpec((tm, MLA_OUT), lambda i: (i, 0)),
                   pl.BlockSpec((tm // tk, MLA_QN, tk), lambda i: (i, 0, 0))],
        out_shape=[jax.ShapeDtypeStruct((t, MLA_OUT), BF16),
                   jax.ShapeDtypeStruct((t // tk, MLA_QN, tk), BF16)],
        compiler_params=_params("parallel"),
        name="mla_proj",
    )(x, g2, wd, qg, wuq, kvg, wukv, cs)


MLA_STEP_HEADS = LANES // MLA_ROPE_DIM


def _mla_kernel(qn_ref, qr_ref, kn_ref, kr_ref, vt_ref, o_ref, acc_ref, ml_ref, s_ref, *, tq, tk):
    i = pl.program_id(2)
    heads = MLA_STEP_HEADS
    lane = lax.broadcasted_iota(jnp.int32, (tq, 2 * LANES), 1)
    half = MLA_ROPE_DIM // 2
    qts = []
    for h in range(heads):
        pair, j = divmod(h, 2)
        q = jnp.concatenate([qn_ref[:, pair * LANES:(pair + 1) * LANES], qr_ref[...]], axis=1)
        r0 = LANES + h * half
        sel = jnp.logical_and(lane >= j * HEAD_DIM, lane < (j + 1) * HEAD_DIM)
        sel = jnp.logical_or(sel, jnp.logical_and(lane >= r0, lane < r0 + half))
        r1 = r0 + LANES // 2
        sel = jnp.logical_or(sel, jnp.logical_and(lane >= r1, lane < r1 + half))
        qts.append(_transposed(jnp.where(sel, q, jnp.zeros_like(q))))
    acc_ref[...] = jnp.zeros_like(acc_ref)
    row = lax.broadcasted_iota(jnp.int32, ml_ref.shape, 0)
    ml_ref[...] = jnp.where(row < heads, NEG, 0.0)

    def scores(unit, _):
        jb, h, lanes, _ = unit
        start = pl.multiple_of(jb * tk, tk)
        pair = h // 2
        kt = jnp.concatenate([kn_ref[pl.ds(start, tk), pair * LANES:(pair + 1) * LANES],
                              kr_ref[pl.ds(start, tk), :]], axis=1)
        return _dot(kt, qts[h][:, lanes])

    def softmax(unit, s):
        _, h, lanes, mask = unit
        if mask is not None:
            s = jnp.where(mask, s, NEG)
        m_old = ml_ref[h:h + 1, lanes]
        m_new = jnp.maximum(m_old, jnp.max(s, axis=0, keepdims=True))
        alpha = jnp.exp2(m_old - m_new)
        e = jnp.exp2(s - m_new)
        ml_ref[h:h + 1, lanes] = m_new
        l_row = slice(heads + h, heads + h + 1)
        ml_ref[l_row, lanes] = alpha * ml_ref[l_row, lanes] + jnp.sum(e, axis=0, keepdims=True)
        return alpha, e.astype(BF16)

    def values(unit, val):
        jb, h, lanes, _ = unit
        alpha, e = val
        rows = slice(h * HEAD_DIM, (h + 1) * HEAD_DIM)
        acc_ref[rows, lanes] = alpha * acc_ref[rows, lanes] + _dot(vt_ref[jb, rows, :], e)

    _causal_sweep(i, tq, tk, (scores, softmax, values), False, s_ref, heads)
    head = lax.broadcasted_iota(jnp.int32, acc_ref.shape, 0) // HEAD_DIM
    denom = ml_ref[heads:heads + 1, :]
    for h in range(1, heads):
        denom = jnp.where(head == h, ml_ref[heads + h:heads + h + 1, :], denom)
    o_ref[...] = (acc_ref[...] / denom).T.astype(o_ref.dtype)


def _mla_attention(a, vt, batch, seq):
    t = a.shape[0]
    tq = _tile(seq, ATT_TQ)
    tk = vt.shape[2]
    nq = seq // tq
    groups = MLA_HEADS // MLA_STEP_HEADS
    gw = MLA_STEP_HEADS * HEAD_DIM
    qr0 = MLA_QN // LANES
    kn0 = (MLA_QN + MLA_QR) // gw
    kr0 = (MLA_QN + MLA_QR + MLA_QN) // LANES
    return pl.pallas_call(
        functools.partial(_mla_kernel, tq=tq, tk=tk),
        grid=(batch, groups, nq),
        in_specs=[pl.BlockSpec((tq, gw), lambda b, p, i: (b * nq + i, p)),
                  pl.BlockSpec((tq, LANES), lambda b, p, i: (b * nq + i, qr0 + p)),
                  pl.BlockSpec((seq, gw), lambda b, p, i: (b, kn0 + p)),
                  pl.BlockSpec((seq, LANES), lambda b, p, i: (b, kr0)),
                  pl.BlockSpec((seq // tk, gw, tk), lambda b, p, i: (b, p, 0))],
        out_specs=pl.BlockSpec((tq, gw), lambda b, p, i: (b * nq + i, p)),
        out_shape=jax.ShapeDtypeStruct((t, MLA_HEADS * HEAD_DIM), BF16),
        scratch_shapes=[pltpu.VMEM((gw, tq), F32), pltpu.VMEM((2 * MLA_STEP_HEADS, tq), F32),
                        pltpu.VMEM((MLA_STEP_HEADS * (tq // tk), tk, tq), F32)],
        compiler_params=_params("parallel", "parallel", "arbitrary"),
        name="mla_attention",
    )(a, a, a, a, vt)


FFN_CHUNK = 256


def _swiglu_into(acc_ref, h, wg, wu, wd):
    def gate_up(cols, _):
        return _dot(h, wg(cols)), _dot(h, wu(cols))

    def activate(cols, gu):
        g, u = gu
        return (g * jax.nn.sigmoid(g) * u).astype(BF16)

    def down(cols, a):
        acc_ref[...] += _dot(a, wd(cols))

    return gate_up, activate, down


def _run_swiglu(acc_ref, h, wg, wu, wd, width):
    chunks = [slice(c, min(c + FFN_CHUNK, width)) for c in range(0, width, FFN_CHUNK)]
    _staggered(chunks, _swiglu_into(acc_ref, h, wg, wu, wd))


def _ffn_kernel(x_ref, a_ref, wo_ref, g_ref, wg_ref, wu_ref, wd_ref, o_ref, h_ref):
    f = pl.program_id(1)

    @pl.when(f == 0)
    def _():
        x = x_ref[...] + _dot(a_ref[...], wo_ref[...])
        h_ref[...] = _rms(x, g_ref[...]).astype(BF16)
        o_ref[...] = x

    _run_swiglu(o_ref, h_ref[...], lambda c: wg_ref[:, c], lambda c: wu_ref[:, c], lambda c: wd_ref[c, :],
                wg_ref.shape[1])


def _ffn(x, a, wo, g, wg, wu, wd):
    t, d = x.shape
    fdim = wg.shape[1]
    tm = _tile(t, 512)
    tf = fdim
    return pl.pallas_call(
        _ffn_kernel,
        grid=(t // tm, fdim // tf),
        in_specs=[pl.BlockSpec((tm, d), lambda i, f: (i, 0)),
                  pl.BlockSpec((tm, a.shape[1]), lambda i, f: (i, 0)),
                  pl.BlockSpec(wo.shape, lambda i, f: (0, 0)),
                  pl.BlockSpec((1, d), lambda i, f: (0, 0)),
                  pl.BlockSpec((d, tf), lambda i, f: (0, f)),
                  pl.BlockSpec((d, tf), lambda i, f: (0, f)),
                  pl.BlockSpec((tf, d), lambda i, f: (f, 0))],
        out_specs=pl.BlockSpec((tm, d), lambda i, f: (i, 0)),
        out_shape=jax.ShapeDtypeStruct((t, d), F32),
        scratch_shapes=[pltpu.VMEM((tm, d), BF16)],
        compiler_params=_params("parallel", "arbitrary"),
        name="dense_ffn",
    )(x, a, wo, g.reshape(1, d), wg, wu, wd)


MOE_TILE = 512
META_ROWS = 8
ROUTE_SUB = 128
COMBINE_SUB = 256


def _pack_halves(x):
    half = x.shape[1] // 2
    bits = lax.bitcast_convert_type(x.astype(BF16).astype(F32), jnp.uint32)
    return bits[:, :half] | (bits[:, half:] >> 16)


def _unpack_halves(w):
    hi = lax.bitcast_convert_type(w & jnp.uint32(0xFFFF0000), F32)
    lo = lax.bitcast_convert_type(w << 16, F32)
    return jnp.concatenate([hi, lo], axis=1)


def _router_kernel(x_ref, g_ref, rt_ref, su_ref, eye_ref, hb_ref, meta_ref, metac_ref, cnt_ref, carry_ref):
    @pl.when(pl.program_id(0) == 0)
    def _():
        carry_ref[...] = jnp.zeros_like(carry_ref)

    h = _rms(x_ref[...], g_ref[...])
    hb_ref[...] = _pack_halves(h)
    ne = rt_ref.shape[0]
    r = rt_ref[...]
    r_hi = r.astype(BF16).astype(F32)
    h_hi = h.astype(BF16)
    h_lo = (h - h_hi.astype(F32)).astype(BF16)
    both_r = _dot_nt(jnp.concatenate([r_hi, r - r_hi], axis=0).astype(BF16), h_hi)
    logits = both_r[:ne] + both_r[ne:] + _dot_nt(r_hi.astype(BF16), h_lo)
    ne, ct = logits.shape
    eio = lax.broadcasted_iota(jnp.int32, (ne, ct), 0).astype(F32)
    m1 = jnp.max(logits, axis=0, keepdims=True)
    i1 = jnp.min(jnp.where(logits == m1, eio, float(ne)), axis=0, keepdims=True)
    rest = jnp.where(eio == i1, -jnp.inf, logits)
    m2 = jnp.max(rest, axis=0, keepdims=True)
    i2 = jnp.min(jnp.where(rest == m2, eio, float(ne)), axis=0, keepdims=True)
    e2 = jnp.exp(m2 - m1)
    g1 = 1.0 / (1.0 + e2)
    g2 = e2 / (1.0 + e2)
    oh1 = (eio == i1).astype(F32)
    oh2 = (eio == i2).astype(F32)
    both = oh1 + oh2
    seen = carry_ref[:, 0:1] + _dot(both.astype(BF16), su_ref[...])
    r1 = jnp.sum(oh1 * seen, axis=0, keepdims=True)
    r2 = jnp.sum(oh2 * seen, axis=0, keepdims=True)
    meta = jnp.zeros((META_ROWS, ct), F32)
    for row, val in enumerate((i1, i2, r1, r2, g1, g2)):
        meta = jnp.where(eio == float(row), val, meta)
    meta_ref[...] = meta
    metac_ref[...] = lax.dot_general(eye_ref[...], meta, _NT, precision=lax.Precision.HIGHEST,
                                     preferred_element_type=F32)
    carry_ref[...] = carry_ref[...] + jnp.sum(both, axis=1, keepdims=True)
    cnt_ref[0] = carry_ref[...]


def _router(x, g, router):
    t, d = x.shape
    ct = _tile(t, MOE_TILE)
    nc = t // ct
    su = jnp.asarray(np.triu(np.ones((ct, ct), np.float32), 1), BF16)
    eye = jnp.eye(ct, dtype=F32)
    return pl.pallas_call(
        _router_kernel,
        grid=(nc,),
        in_specs=[pl.BlockSpec((ct, d), lambda i: (i, 0)),
                  pl.BlockSpec((1, d), lambda i: (0, 0)),
                  pl.BlockSpec((N_EXPERTS, d), lambda i: (0, 0)),
                  pl.BlockSpec((ct, ct), lambda i: (0, 0)),
                  pl.BlockSpec((ct, ct), lambda i: (0, 0))],
        out_specs=[pl.BlockSpec((ct, d // 2), lambda i: (i, 0)),
                   pl.BlockSpec((META_ROWS, ct), lambda i: (0, i)),
                   pl.BlockSpec((ct, META_ROWS), lambda i: (i, 0)),
                   pl.BlockSpec((1, N_EXPERTS, LANES), lambda i: (i, 0, 0))],
        out_shape=[jax.ShapeDtypeStruct((t, d // 2), jnp.uint32),
                   jax.ShapeDtypeStruct((META_ROWS, t), F32),
                   jax.ShapeDtypeStruct((t, META_ROWS), F32),
                   jax.ShapeDtypeStruct((nc, N_EXPERTS, LANES), F32)],
        scratch_shapes=[pltpu.VMEM((N_EXPERTS, LANES), F32)],
        compiler_params=_params("arbitrary"),
        name="moe_router",
    )(x, g.reshape(1, d), router.T, su, eye)


def _group_layout(cnt_after, t, rt):
    rows = 2 * t + N_EXPERTS * rt
    counts = cnt_after[:, :, 0].astype(jnp.int32)
    padded = (counts[-1] + rt - 1) // rt * rt
    off_end = jnp.cumsum(padded)
    off = (off_end - padded).astype(jnp.int32)
    tile_start = jnp.arange(rows // rt, dtype=jnp.int32) * rt
    tile_expert = jnp.minimum(_count_le(off_end, tile_start), N_EXPERTS - 1).astype(jnp.int32)
    tile_valid = (tile_start < off_end[-1]).astype(jnp.int32)
    return rows, off, counts, tile_expert, tile_valid


def _count_le(sorted_vals, x):
    return jnp.sum((sorted_vals[None, :] <= x[:, None]).astype(jnp.int32), axis=1)


def _work_lists(counts, off, rows, t, ct):
    nc = t // ct
    nrt = rows // ct
    before = jnp.concatenate([jnp.zeros((1, N_EXPERTS), jnp.int32), counts[:-1]], axis=0)
    chunk_start = (off[:, None] + before.T).reshape(-1)
    tile_start = jnp.arange(nrt, dtype=jnp.int32) * ct
    starts = jnp.sort(jnp.concatenate([chunk_start, tile_start]))
    ends = jnp.concatenate([starts[1:], jnp.array([rows], jnp.int32)])
    item_r = jnp.minimum(starts // ct, nrt - 1)
    item_c = (_count_le(chunk_start, starts) - 1) % nc
    first = lambda key: jnp.concatenate([jnp.ones((1,), jnp.int32), (key[1:] != key[:-1]).astype(jnp.int32)])
    dispatch = (item_r, item_c, starts, ends, first(item_r))
    order = jnp.argsort(item_c * (rows + 1) + starts)
    c2 = item_c[order]
    combine = (c2, item_r[order], starts[order], ends[order], first(c2))
    return dispatch, combine


def _positions(expert, rank, off_ref):
    base = jnp.zeros_like(rank)
    for e in range(N_EXPERTS):
        base = jnp.where(expert == float(e), off_ref[e].astype(F32), base)
    return base + rank


def _dispatch_kernel(r_ref, c_ref, s_ref, e_ref, first_ref, off_ref, meta_ref, h_ref, xs_ref, gate_ref):
    n = pl.program_id(0)
    rt, ct = xs_ref.shape[0], h_ref.shape[0]

    @pl.when(first_ref[n] == 1)
    def _():
        xs_ref[...] = jnp.zeros_like(xs_ref)
        gate_ref[...] = jnp.zeros_like(gate_ref)

    meta = meta_ref[...]
    pos0 = _positions(meta[0:1], meta[2:3], off_ref)
    pos1 = _positions(meta[1:2], meta[3:4], off_ref)
    sub = min(rt, ROUTE_SUB)
    for sb in range(rt // sub):
        base = r_ref[n] * rt + sb * sub

        @pl.when(jnp.logical_and(base < e_ref[n], base + sub > s_ref[n]))
        def _():
            row = lax.broadcasted_iota(jnp.int32, (sub, ct), 0) + base
            live = jnp.logical_and(row >= s_ref[n], row < e_ref[n])
            rowf = row.astype(F32)
            sel0 = jnp.logical_and(pos0 == rowf, live)
            sel1 = jnp.logical_and(pos1 == rowf, live)
            sel = jnp.logical_or(sel0, sel1).astype(F32).astype(BF16)
            rows = slice(sb * sub, (sb + 1) * sub)
            xs_ref[rows, :] += _pack_halves(_dot(sel, _unpack_halves(h_ref[...]).astype(BF16)))
            gate = jnp.where(sel0, meta[4:5], 0.0) + jnp.where(sel1, meta[5:6], 0.0)
            gate_ref[rows, :] += jnp.sum(gate, axis=1, keepdims=True)


def _dispatch(plan, off, meta, hb, rows, ct):
    t, d = hb.shape
    n_items = plan[0].shape[0]
    return pl.pallas_call(
        _dispatch_kernel,
        grid_spec=pltpu.PrefetchScalarGridSpec(
            num_scalar_prefetch=6,
            grid=(n_items,),
            in_specs=[pl.BlockSpec((META_ROWS, ct), lambda n, r, c, *_: (0, c[n])),
                      pl.BlockSpec((ct, d), lambda n, r, c, *_: (c[n], 0))],
            out_specs=[pl.BlockSpec((ct, d), lambda n, r, c, *_: (r[n], 0)),
                       pl.BlockSpec((ct, 1), lambda n, r, c, *_: (r[n], 0))]),
        out_shape=[jax.ShapeDtypeStruct((rows, d), jnp.uint32),
                   jax.ShapeDtypeStruct((rows, 1), F32)],
        compiler_params=_params("arbitrary"),
        name="moe_dispatch",
    )(*plan, off, meta, hb)


def _expert_kernel(te_ref, tv_ref, xs_ref, gate_ref, wg_ref, wu_ref, wd_ref, y_ref, acc_ref):
    r = pl.program_id(0)
    f = pl.program_id(1)

    @pl.when(f == 0)
    def _():
        acc_ref[...] = jnp.zeros_like(acc_ref)

    @pl.when(tv_ref[r] == 1)
    def _():
        _run_swiglu(acc_ref, _unpack_halves(xs_ref[...]).astype(BF16), lambda c: wg_ref[0, :, c],
                    lambda c: wu_ref[0, :, c], lambda c: wd_ref[0, c, :], wg_ref.shape[2])

    @pl.when(f == pl.num_programs(1) - 1)
    def _():
        y_ref[...] = _pack_halves(acc_ref[...] * gate_ref[...])


def _experts(tile_expert, tile_valid, xs, gate, wg, wu, wd, ct):
    rows, dp = xs.shape
    d = 2 * dp
    fdim = wg.shape[2]
    tf = _tile(fdim, 1792)
    nf = fdim // tf

    def wcol(r, f, te, tv):
        return (te[r], 0, jnp.where(tv[r] == 1, f, nf - 1))

    def wrow(r, f, te, tv):
        return (te[r], jnp.where(tv[r] == 1, f, nf - 1), 0)

    return pl.pallas_call(
        _expert_kernel,
        grid_spec=pltpu.PrefetchScalarGridSpec(
            num_scalar_prefetch=2,
            grid=(rows // ct, nf),
            in_specs=[pl.BlockSpec((ct, dp), lambda r, f, te, tv: (r, 0)),
                      pl.BlockSpec((ct, 1), lambda r, f, te, tv: (r, 0)),
                      pl.BlockSpec((1, d, tf), wcol),
                      pl.BlockSpec((1, d, tf), wcol),
                      pl.BlockSpec((1, tf, d), wrow)],
            out_specs=pl.BlockSpec((ct, dp), lambda r, f, te, tv: (r, 0)),
            scratch_shapes=[pltpu.VMEM((ct, d), F32)]),
        out_shape=jax.ShapeDtypeStruct((rows, dp), jnp.uint32),
        compiler_params=_params("arbitrary", "arbitrary"),
        name="moe_experts",
    )(tile_expert, tile_valid, xs, gate, wg, wu, wd)


def _combine_kernel(c_ref, r_ref, s_ref, e_ref, first_ref, off_ref, x_ref, metac_ref, y_ref, o_ref):
    n = pl.program_id(0)
    ct, rt = x_ref.shape[0], y_ref.shape[0]

    @pl.when(first_ref[n] == 1)
    def _():
        o_ref[...] = x_ref[...]

    metac = metac_ref[...]
    pos0 = _positions(metac[:, 0:1], metac[:, 2:3], off_ref)
    pos1 = _positions(metac[:, 1:2], metac[:, 3:4], off_ref)
    sub = min(rt, COMBINE_SUB)
    for sb in range(rt // sub):
        base = r_ref[n] * rt + sb * sub

        @pl.when(jnp.logical_and(base < e_ref[n], base + sub > s_ref[n]))
        def _():
            row = lax.broadcasted_iota(jnp.int32, (ct, sub), 1) + base
            live = jnp.logical_and(row >= s_ref[n], row < e_ref[n])
            rowf = row.astype(F32)
            sel = jnp.logical_and(jnp.logical_or(pos0 == rowf, pos1 == rowf), live).astype(F32).astype(BF16)
            o_ref[...] += _dot(sel, _unpack_halves(y_ref[sb * sub:(sb + 1) * sub, :]).astype(BF16))


def _combine(plan, off, x, metac, y, ct):
    t, d = x.shape
    n_items = plan[0].shape[0]
    return pl.pallas_call(
        _combine_kernel,
        grid_spec=pltpu.PrefetchScalarGridSpec(
            num_scalar_prefetch=6,
            grid=(n_items,),
            in_specs=[pl.BlockSpec((ct, d), lambda n, c, r, *_: (c[n], 0)),
                      pl.BlockSpec((ct, META_ROWS), lambda n, c, r, *_: (c[n], 0)),
                      pl.BlockSpec((ct, d // 2), lambda n, c, r, *_: (r[n], 0))],
            out_specs=pl.BlockSpec((ct, d), lambda n, c, r, *_: (c[n], 0))),
        out_shape=jax.ShapeDtypeStruct((t, d), F32),
        compiler_params=_params("arbitrary"),
        name="moe_combine",
    )(*plan, off, x, metac, y)


def _cast_kernel(w_ref, o_ref):
    o_ref[...] = w_ref[...].astype(o_ref.dtype)


CAST_BLOCK_BYTES = 8 * 1024 * 1024


def _layer_bf16(w, layer):
    shape = w.shape[1:]
    cols = shape[-1]
    rows = math.prod(shape[:-1])
    tr = rows
    while tr * cols * 4 > CAST_BLOCK_BYTES and tr % 16 == 0:
        tr //= 2
    out = pl.pallas_call(
        _cast_kernel,
        grid=(rows // tr,),
        in_specs=[pl.BlockSpec((None, tr, cols), lambda i: (layer, i, 0))],
        out_specs=pl.BlockSpec((tr, cols), lambda i: (i, 0)),
        out_shape=jax.ShapeDtypeStruct((rows, cols), BF16),
        compiler_params=_params("parallel"),
        name="weights_bf16",
    )(w.reshape(w.shape[0], rows, cols))
    return out.reshape(shape)


SC_CORES = 2
SC_SUBCORES = 16
SC_CHUNK = 128


def _sc_worker_chunks(total):
    per = total // (SC_CORES * SC_SUBCORES)
    wid = lax.axis_index("s") * SC_CORES + lax.axis_index("c")
    return wid * per, per // SC_CHUNK


def _sc_mesh():
    return plsc.VectorSubcoreMesh(core_axis_name="c", subcore_axis_name="s")


def _sc_gather(table, idx):
    v, d = table.shape
    b = idx.shape[0]

    def body(table_hbm, idx_hbm, out_hbm, idx_v, rows_v, sem):
        base, chunks = _sc_worker_chunks(b)

        @pl.loop(0, chunks)
        def _(j):
            start = base + j * SC_CHUNK
            pltpu.sync_copy(idx_hbm.at[pl.ds(start, SC_CHUNK)], idx_v)
            pltpu.async_copy(table_hbm.at[idx_v], rows_v, sem).wait()
            pltpu.sync_copy(rows_v, out_hbm.at[pl.ds(start, SC_CHUNK)])

    return pl.kernel(
        body,
        out_type=jax.ShapeDtypeStruct((b, d), table.dtype),
        mesh=_sc_mesh(),
        scratch_types=[pltpu.VMEM((SC_CHUNK,), jnp.int32),
                       pltpu.VMEM((SC_CHUNK, d), table.dtype),
                       pltpu.SemaphoreType.DMA],
        name="sc_row_gather",
    )(table, idx)


def _sc_scatter_rows(src, idx, n):
    t, d = src.shape
    b = idx.shape[0]

    def body(src_hbm, idx_hbm, out_hbm, idx_v, rows_v):
        base, chunks = _sc_worker_chunks(b)

        @pl.loop(0, chunks)
        def _(j):
            start = base + j * SC_CHUNK
            first = lax.rem(start, t)
            pltpu.sync_copy(idx_hbm.at[pl.ds(start, SC_CHUNK)], idx_v)
            pltpu.sync_copy(src_hbm.at[pl.ds(first, SC_CHUNK)], rows_v)
            pltpu.sync_copy(rows_v, out_hbm.at[idx_v])

    return pl.kernel(
        body,
        out_type=jax.ShapeDtypeStruct((n, d), src.dtype),
        mesh=_sc_mesh(),
        scratch_types=[pltpu.VMEM((SC_CHUNK,), jnp.int32), pltpu.VMEM((SC_CHUNK, d), src.dtype)],
        name="sc_row_scatter",
    )(src, idx)


def _weighted_add_kernel(x_ref, metac_ref, y0_ref, y1_ref, g_ref, o_ref, *, normed):
    gates = metac_ref[...]
    out = (x_ref[...] + gates[:, 4:5] * _unpack_halves(y0_ref[...])
           + gates[:, 5:6] * _unpack_halves(y1_ref[...]))
    o_ref[...] = _rms(out, g_ref[...]) if normed else out


def _weighted_add(x, metac, ysel, out_gain=None):
    t, d = x.shape
    tm = _tile(t, 512)
    nt = t // tm
    gain = jnp.ones((d,), F32) if out_gain is None else out_gain
    return pl.pallas_call(
        functools.partial(_weighted_add_kernel, normed=out_gain is not None),
        grid=(nt,),
        in_specs=[pl.BlockSpec((tm, d), lambda i: (i, 0)),
                  pl.BlockSpec((tm, META_ROWS), lambda i: (i, 0)),
                  pl.BlockSpec((tm, d // 2), lambda i: (i, 0)),
                  pl.BlockSpec((tm, d // 2), lambda i: (nt + i, 0)),
                  pl.BlockSpec((1, d), lambda i: (0, 0))],
        out_specs=pl.BlockSpec((tm, d), lambda i: (i, 0)),
        out_shape=jax.ShapeDtypeStruct((t, d), F32),
        compiler_params=_params("parallel"),
        name="moe_weighted_add",
    )(x, metac, ysel, ysel, gain.reshape(1, d))


def _moe(x, g, router, wg, wu, wd, out_gain=None):
    t = x.shape[0]
    ct = _tile(t, MOE_TILE)
    hb, meta, metac, cnt = _router(x, g, router)
    sc_rows = SC_CORES * SC_SUBCORES * SC_CHUNK
    rows, off, counts, tile_expert, tile_valid = _group_layout(cnt, t, ct)
    if (2 * t) % sc_rows == 0 and rows % sc_rows == 0:
        base = sum(jnp.where(meta[0:2] == float(e), off[e], 0) for e in range(N_EXPERTS))
        pos = (base + meta[2:4].astype(jnp.int32)).reshape(-1)
        xs = _sc_scatter_rows(hb, pos, rows)
        y = _experts(tile_expert, tile_valid, xs, jnp.ones((rows, 1), F32), wg, wu, wd, ct)
        return _weighted_add(x, metac, _sc_gather(y, pos), out_gain)
    dispatch, combine = _work_lists(counts, off, rows, t, ct)
    xs, gate = _dispatch(dispatch, off, meta, hb, rows, ct)
    y = _experts(tile_expert, tile_valid, xs, gate, wg, wu, wd, ct)
    out = _combine(combine, off, x, metac, y, ct)
    return out if out_gain is None else _final_norm(out, out_gain)


def _final_norm_kernel(x_ref, g_ref, o_ref):
    o_ref[...] = _rms(x_ref[...], g_ref[...])


def _final_norm(x, g):
    t, d = x.shape
    tm = _tile(t, 1024)
    return pl.pallas_call(
        _final_norm_kernel,
        grid=(t // tm,),
        in_specs=[pl.BlockSpec((tm, d), lambda i: (i, 0)), pl.BlockSpec((1, d), lambda i: (0, 0))],
        out_specs=pl.BlockSpec((tm, d), lambda i: (i, 0)),
        out_shape=jax.ShapeDtypeStruct((t, d), F32),
        compiler_params=_params("parallel"),
        name="final_norm",
    )(x, g.reshape(1, d))


def kernel(x, rel_bias, attn_norm, ffn_norm, final_norm, swa_w_qkv, swa_b_qkv, swa_sinks, swa_w_o, sb_w_qkv, sb_w_o, mla_w_down, mla_q_norm, mla_w_uq, mla_kv_norm, mla_w_ukv, mla_w_o, ffn_w_gate, ffn_w_up, ffn_w_down, moe_router, moe_w_gate, moe_w_up, moe_w_down):
    batch, seq, d = x.shape
    depth = attn_norm.shape[0]
    xt = x.reshape(batch * seq, d)
    swa_bias = _swa_bias(rel_bias)
    for i in range(depth):
        mixer, j = i % N_MIXERS, i // N_MIXERS
        if mixer == 0:
            w, b, cs = _swa_weights(swa_w_qkv[j], swa_b_qkv[j])
            qkv = _norm_proj(xt, attn_norm[i], w, b, cs, "swa_proj")
            o = _swa_attention(qkv, swa_bias, swa_sinks[j], batch, seq)
            wo = swa_w_o[j].astype(BF16)
        elif mixer == 1:
            n = sb_w_qkv.shape[2] // 3
            cs = jnp.concatenate([jnp.full((n,), HEAD_DIM ** -0.5 * LOG2E, F32), jnp.ones((n,), F32)])
            qk, vt = _norm_proj_vt(xt, attn_norm[i], sb_w_qkv[j].astype(BF16), cs, n, seq, "sb_proj")
            o = _sb_attention(qk, vt, batch, seq)
            wo = sb_w_o[j].astype(BF16)
        else:
            a, vt = _mla_proj(xt, attn_norm[i], mla_w_down[j], mla_q_norm[j], mla_w_uq[j],
                              mla_kv_norm[j], mla_w_ukv[j], seq)
            o = _mla_attention(a, vt, batch, seq)
            wo = mla_w_o[j].astype(BF16)
        f = i // 2
        if i % 2 == 0:
            xt = _ffn(xt, o, wo, ffn_norm[i], _layer_bf16(ffn_w_gate, f), _layer_bf16(ffn_w_up, f),
                      _layer_bf16(ffn_w_down, f))
        else:
            last = final_norm if i == depth - 1 else None
            xt = _out_proj(xt, o, wo, "mixer_out")
            xt = _moe(xt, ffn_norm[i], moe_router[f], _layer_bf16(moe_w_gate, f), _layer_bf16(moe_w_up, f),
                      _layer_bf16(moe_w_down, f), last)
    if depth % 2 == 1:
        xt = _final_norm(xt, final_norm)
    return xt.reshape(batch, seq, d)
```

```python
import functools
import math

import numpy as np
import jax
import jax.numpy as jnp
from jax import lax
from jax.experimental import pallas as pl
from jax.experimental.pallas import tpu as pltpu
from jax.experimental.pallas import tpu_sc as plsc

F32 = jnp.float32
BF16 = jnp.bfloat16

N_MIXERS = 3
RMS_EPS = 1e-6
SWA_HEADS = 16
SWA_KV_HEADS = 2
HEAD_DIM = 64
SWA_BLOCK = 128
SWA_WINDOW = 128
REL_BUCKETS = 32
REL_MAX_DIST = 128
SB_HEADS = 16
MLA_HEADS = 16
MLA_NOPE_DIM = 64
MLA_ROPE_DIM = 32
MLA_Q_RANK = 384
MLA_KV_RANK = 256
ROPE_THETA = 10000.0
N_EXPERTS = 8

LANES = 128
NEG = -1e30
LOG2E = math.log2(math.e)
VMEM_LIMIT = 56 * 1024 * 1024

_NT = (((1,), (1,)), ((), ()))


def _dot(a, b):
    return jnp.dot(a, b, preferred_element_type=F32)


def _dot_nt(a, b):
    return lax.dot_general(a, b, _NT, preferred_element_type=F32)


def _rms(x, g):
    return x * lax.rsqrt(jnp.mean(x * x, axis=-1, keepdims=True) + RMS_EPS) * g


def _params(*sem):
    return pltpu.CompilerParams(dimension_semantics=sem, vmem_limit_bytes=VMEM_LIMIT)


def _tile(n, pref):
    return pref if n % pref == 0 else n


def _norm_proj_kernel(x_ref, g_ref, w_ref, b_ref, cs_ref, o_ref):
    h = _rms(x_ref[...], g_ref[...]).astype(BF16)
    y = (_dot(h, w_ref[...]) + b_ref[...]) * cs_ref[...]
    o_ref[...] = y.astype(o_ref.dtype)


def _norm_proj(x, g, w, b, colscale, name):
    t, d = x.shape
    n = w.shape[1]
    tm = _tile(t, 512)
    return pl.pallas_call(
        _norm_proj_kernel,
        grid=(t // tm,),
        in_specs=[pl.BlockSpec((tm, d), lambda i: (i, 0)),
                  pl.BlockSpec((1, d), lambda i: (0, 0)),
                  pl.BlockSpec((d, n), lambda i: (0, 0)),
                  pl.BlockSpec((1, n), lambda i: (0, 0)),
                  pl.BlockSpec((1, n), lambda i: (0, 0))],
        out_specs=pl.BlockSpec((tm, n), lambda i: (i, 0)),
        out_shape=jax.ShapeDtypeStruct((t, n), BF16),
        compiler_params=_params("parallel"),
        name=name,
    )(x, g.reshape(1, d), w, b.reshape(1, n), colscale.reshape(1, n))


ATT_TQ = 512
ATT_TK = 256


def _store_vt(vt_ref, v, tk):
    for s in range(v.shape[0] // tk):
        vt_ref[s] = v[s * tk:(s + 1) * tk, :].T.astype(vt_ref.dtype)


def _norm_proj_vt_kernel(x_ref, g_ref, w_ref, cs_ref, o_ref, vt_ref, *, tk):
    h = _rms(x_ref[...], g_ref[...]).astype(BF16)
    y = _dot(h, w_ref[...])
    n = o_ref.shape[1]
    o_ref[...] = (y[:, :n] * cs_ref[...]).astype(o_ref.dtype)
    _store_vt(vt_ref, y[:, n:], tk)


def _norm_proj_vt(x, g, w, colscale, nv, seq, name):
    t, d = x.shape
    n = w.shape[1] - nv
    tm = _tile(seq, 512)
    tk = _tile(seq, ATT_TK)
    per = tm // tk
    return pl.pallas_call(
        functools.partial(_norm_proj_vt_kernel, tk=tk),
        grid=(t // tm,),
        in_specs=[pl.BlockSpec((tm, d), lambda i: (i, 0)),
                  pl.BlockSpec((1, d), lambda i: (0, 0)),
                  pl.BlockSpec((d, n + nv), lambda i: (0, 0)),
                  pl.BlockSpec((1, n), lambda i: (0, 0))],
        out_specs=[pl.BlockSpec((tm, n), lambda i: (i, 0)),
                   pl.BlockSpec((per, nv, tk), lambda i: (i, 0, 0))],
        out_shape=[jax.ShapeDtypeStruct((t, n), BF16),
                   jax.ShapeDtypeStruct((t // tk, nv, tk), BF16)],
        compiler_params=_params("parallel"),
        name=name,
    )(x, g.reshape(1, d), w, colscale.reshape(1, n))


def _t5_bucket_table():
    qi = np.arange(SWA_BLOCK)[:, None]
    kj = np.arange(2 * SWA_BLOCK)[None, :]
    dist = qi + SWA_BLOCK - kj
    d0 = np.maximum(dist, 0)
    max_exact = REL_BUCKETS // 2
    d = np.maximum(d0, 1).astype(np.float32)
    large = max_exact + (np.log(d / max_exact) / math.log(REL_MAX_DIST / max_exact)
                         * (REL_BUCKETS - max_exact)).astype(np.int32)
    large = np.minimum(large, REL_BUCKETS - 1)
    bucket = np.where(d0 < max_exact, d0, large)
    band = (dist >= 0) & (dist < SWA_WINDOW)
    return np.where(band, bucket, -1).astype(np.int32)


def _swa_bias_kernel(rel_ref, bucket_ref, o_ref):
    h = pl.program_id(0)
    bucket = bucket_ref[...]
    acc = jnp.full(bucket.shape, NEG, F32)
    for b in range(REL_BUCKETS):
        acc = jnp.where(bucket == b, rel_ref[b, h], acc)
    o_ref[0] = acc


def _swa_bias(rel_bias):
    bucket = jnp.asarray(_t5_bucket_table())
    q, k = bucket.shape
    return pl.pallas_call(
        _swa_bias_kernel,
        grid=(SWA_HEADS,),
        in_specs=[pl.BlockSpec(memory_space=pltpu.SMEM),
                  pl.BlockSpec((q, k), lambda h: (0, 0))],
        out_specs=pl.BlockSpec((1, q, k), lambda h: (h, 0, 0)),
        out_shape=jax.ShapeDtypeStruct((SWA_HEADS, q, k), F32),
        compiler_params=_params("arbitrary"),
        name="swa_bias",
    )(rel_bias, bucket)


def _swa_kernel(sink_ref, q_ref, kc_ref, kp_ref, vc_ref, vp_ref, bias_ref, x_ref, wo_ref, o_ref, *, tq):
    i = pl.program_id(1)
    g = pl.program_id(2)

    @pl.when(g == 0)
    def _():
        o_ref[...] = x_ref[...]

    blk = SWA_BLOCK
    heads = bias_ref.shape[0]
    lo_half = lax.broadcasted_iota(jnp.int32, (blk, LANES), 1) < HEAD_DIM
    in_prev = lax.broadcasted_iota(jnp.int32, (heads * blk, 2 * blk), 1) < blk
    bias = bias_ref[...].reshape(heads * blk, 2 * blk)

    def window(cur_ref, prev_ref, sb):
        if sb == 0:
            return jnp.concatenate([prev_ref[...], cur_ref[0:blk, :]], axis=0)
        return cur_ref[(sb - 1) * blk:(sb + 1) * blk, :]

    def scores(sb, _):
        q = q_ref[sb * blk:(sb + 1) * blk, :]
        parts = []
        for h in range(heads):
            qp = q[:, (h // 2) * LANES:(h // 2 + 1) * LANES]
            keep = lo_half if h % 2 == 0 else jnp.logical_not(lo_half)
            parts.append(jnp.where(keep, qp, jnp.zeros_like(qp)))
        s = _dot_nt(jnp.concatenate(parts, axis=0), window(kc_ref, kp_ref, sb)) + bias
        if sb == 0:
            s = jnp.where(jnp.logical_and(in_prev, i == 0), NEG, s)
        return s

    def softmax(sb, s):
        ps, inv = [], []
        for h in range(heads):
            sh = s[h * blk:(h + 1) * blk]
            sink = sink_ref[g * heads + h]
            m = jnp.maximum(jnp.max(sh, axis=1, keepdims=True), sink)
            e = jnp.exp(sh - m)
            inv.append(1.0 / (jnp.sum(e, axis=1, keepdims=True) + jnp.exp(sink - m)))
            ps.append(e.astype(BF16))
        return jnp.concatenate(ps, axis=0), inv

    def values(sb, val):
        p, inv = val
        o = _dot(p, window(vc_ref, vp_ref, sb))
        outs = [jnp.where(lo_half, o[h * blk:(h + 1) * blk] * inv[h], o[(h + 1) * blk:(h + 2) * blk] * inv[h + 1])
                for h in range(0, heads, 2)]
        o_ref[sb * blk:(sb + 1) * blk, :] += _dot(jnp.concatenate(outs, axis=1).astype(BF16), wo_ref[...])

    _staggered(list(range(tq // blk)), (scores, softmax, values))


def _swa_attention(qkv, bias, sinks, x, wo, batch, seq):
    t, d = x.shape
    tq = _tile(seq, 512)
    nt = seq // tq
    group = SWA_HEADS // SWA_KV_HEADS
    gw = group * HEAD_DIM
    kcol = SWA_HEADS * HEAD_DIM // LANES
    vcol = kcol + SWA_KV_HEADS
    per = tq // SWA_BLOCK

    def prev_idx(b, i, g):
        return jnp.maximum(b * (seq // SWA_BLOCK) + i * per - 1, 0)

    return pl.pallas_call(
        functools.partial(_swa_kernel, tq=tq),
        grid=(batch, nt, SWA_KV_HEADS),
        in_specs=[pl.BlockSpec(memory_space=pltpu.SMEM),
                  pl.BlockSpec((tq, gw), lambda b, i, g: (b * nt + i, g)),
                  pl.BlockSpec((tq, LANES), lambda b, i, g: (b * nt + i, kcol + g)),
                  pl.BlockSpec((SWA_BLOCK, LANES), lambda b, i, g: (prev_idx(b, i, g), kcol + g)),
                  pl.BlockSpec((tq, LANES), lambda b, i, g: (b * nt + i, vcol + g)),
                  pl.BlockSpec((SWA_BLOCK, LANES), lambda b, i, g: (prev_idx(b, i, g), vcol + g)),
                  pl.BlockSpec((group, SWA_BLOCK, 2 * SWA_BLOCK), lambda b, i, g: (g, 0, 0)),
                  pl.BlockSpec((tq, d), lambda b, i, g: (b * nt + i, 0)),
                  pl.BlockSpec((gw, d), lambda b, i, g: (g, 0))],
        out_specs=pl.BlockSpec((tq, d), lambda b, i, g: (b * nt + i, 0)),
        out_shape=jax.ShapeDtypeStruct((t, d), F32),
        compiler_params=_params("parallel", "parallel", "arbitrary"),
        name="swa_attention",
    )(sinks, qkv, qkv, qkv, qkv, qkv, bias, x, wo)


def _swa_weights(w_qkv, b_qkv):
    nq = SWA_HEADS * HEAD_DIM
    kv = SWA_KV_HEADS * HEAD_DIM
    dup = np.concatenate([np.tile(np.arange(HEAD_DIM), 2) + h * HEAD_DIM for h in range(SWA_KV_HEADS)])
    cols = np.concatenate([np.arange(nq), nq + dup, nq + kv + dup])
    scale = np.concatenate([np.full(nq, HEAD_DIM ** -0.5), np.ones(2 * dup.size)]).astype(np.float32)
    return w_qkv[:, cols].astype(BF16), b_qkv[cols], jnp.asarray(scale)


def _staggered(units, stages):
    vals = list(units)
    for t in range(len(units) + len(stages) - 1):
        for k, stage in enumerate(stages):
            u = t - k
            if 0 <= u < len(units):
                vals[u] = stage(units[u], vals[u])


def _score_ahead(units, next_units, stages, s_ref):
    rest = stages[1:]
    vals = [None] * len(units)
    for t in range(len(units) + len(rest) - 1):
        ahead = stages[0](next_units[t], None) if next_units is not None and t < len(units) else None
        for k, stage in enumerate(rest):
            u = t - k
            if 0 <= u < len(units):
                if k == 0:
                    lanes = units[u][2]
                    vals[u] = stage(units[u], s_ref[u, :, :lanes.stop - lanes.start])
                    if ahead is not None:
                        s_ref[u] = ahead
                else:
                    vals[u] = stage(units[u], vals[u])


def _causal_sweep(i, tq, tk, stages, strict, s_ref, heads=2):
    nsub = tq // tk

    def group(j):
        return [(j * nsub + sb, h, slice(0, tq), None) for sb in reversed(range(nsub)) for h in range(heads)]

    def sweep(has_history):
        diagonal = []
        for sb in reversed(range(nsub)):
            nl = tq - sb * tk
            key = lax.broadcasted_iota(jnp.int32, (tk, nl), 0)
            qry = lax.broadcasted_iota(jnp.int32, (tk, nl), 1)
            mask = key < qry if strict else key <= qry
            diagonal += [(i * nsub + sb, h, slice(sb * tk, tq), mask) for h in range(heads)]
        for u, unit in enumerate(diagonal):
            s_ref[u, :, :tq - unit[2].start] = stages[0](unit, None)
        if not has_history:
            _score_ahead(diagonal, None, stages, s_ref)
            return
        _score_ahead(diagonal, group(i - 1), stages, s_ref)

        def body(n, carry):
            _score_ahead(group(i - 1 - n), group(i - 2 - n), stages, s_ref)
            return carry

        lax.fori_loop(0, i - 1, body, 0)
        _score_ahead(group(0), None, stages, s_ref)

    pl.when(i == 0)(lambda: sweep(False))
    pl.when(i > 0)(lambda: sweep(True))


def _add_projected(x_ref, wo_ref, o_ref, heads_out, first):
    @pl.when(first)
    def _():
        o_ref[...] = x_ref[...]

    o_ref[...] += _dot(heads_out, wo_ref[...])


def _transposed(x):
    return x.astype(F32).T.astype(x.dtype)


def _split_pair(q):
    lo_half = lax.broadcasted_iota(jnp.int32, q.shape, 1) < HEAD_DIM
    zero = jnp.zeros_like(q)
    return jnp.where(lo_half, q, zero), jnp.where(lo_half, zero, q)


SB_STEP_HEADS = 4


def _sb_kernel(q_ref, k_ref, vt_ref, u_ref, x_ref, wo_ref, o_ref, acc_ref, c_ref, s_ref, *, tq, tk, heads):
    i = pl.program_id(1)
    qts = [_transposed(qm) for pair in range(heads // 2)
           for qm in _split_pair(q_ref[:, pair * LANES:(pair + 1) * LANES])]
    u = u_ref[...]
    acc_ref[...] = jnp.zeros_like(acc_ref)
    c_ref[...] = jnp.zeros_like(c_ref)

    def scores(unit, _):
        jb, h, lanes, _ = unit
        kt = k_ref[pl.ds(pl.multiple_of(jb * tk, tk), tk), (h // 2) * LANES:(h // 2 + 1) * LANES]
        return _dot(kt, qts[h][:, lanes])

    def log_fail(unit, z):
        neg_abs = lax.bitcast_convert_type(lax.bitcast_convert_type(z, jnp.uint32) | jnp.uint32(1 << 31), F32)
        sp = jnp.maximum(z, 0.0) + jnp.log2(1.0 + jnp.exp2(neg_abs))
        if unit[3] is not None:
            sp = jnp.where(unit[3], sp, 0.0)
        return z, sp.astype(BF16)

    def suffix_sum(unit, val):
        z, sp = val
        return z, _dot(u, sp)

    def weights(unit, val):
        _, h, lanes, mask = unit
        z, r = val
        c = c_ref[h:h + 1, lanes]
        la = z - r - c
        if mask is not None:
            la = jnp.where(mask, la, NEG)
        c_ref[h:h + 1, lanes] = c + r[0:1, :]
        return jnp.exp2(la).astype(BF16)

    def values(unit, a):
        jb, h, lanes, _ = unit
        rows = slice(h * HEAD_DIM, (h + 1) * HEAD_DIM)
        acc_ref[rows, lanes] += _dot(vt_ref[jb, rows, :], a)

    _causal_sweep(i, tq, tk, (scores, log_fail, suffix_sum, weights, values), True, s_ref, heads)
    _add_projected(x_ref, wo_ref, o_ref, acc_ref[...].T.astype(BF16), pl.program_id(2) == 0)


def _sb_attention(qk, vt, x, wo, batch, seq):
    t, d = x.shape
    tq = _tile(seq, ATT_TQ)
    tk = vt.shape[2]
    nq = seq // tq
    groups = SB_HEADS // SB_STEP_HEADS
    gw = SB_STEP_HEADS * HEAD_DIM
    u = jnp.asarray(np.triu(np.ones((tk, tk), np.float32)), BF16)
    return pl.pallas_call(
        functools.partial(_sb_kernel, tq=tq, tk=tk, heads=SB_STEP_HEADS),
        grid=(batch, nq, groups),
        in_specs=[pl.BlockSpec((tq, gw), lambda b, i, p: (b * nq + i, p)),
                  pl.BlockSpec((seq, gw), lambda b, i, p: (b, groups + p)),
                  pl.BlockSpec((seq // tk, gw, tk), lambda b, i, p: (b, p, 0)),
                  pl.BlockSpec((tk, tk), lambda b, i, p: (0, 0)),
                  pl.BlockSpec((tq, d), lambda b, i, p: (b * nq + i, 0)),
                  pl.BlockSpec((gw, d), lambda b, i, p: (p, 0))],
        out_specs=pl.BlockSpec((tq, d), lambda b, i, p: (b * nq + i, 0)),
        out_shape=jax.ShapeDtypeStruct((t, d), F32),
        scratch_shapes=[pltpu.VMEM((gw, tq), F32), pltpu.VMEM((8, tq), F32),
                        pltpu.VMEM((SB_STEP_HEADS * (tq // tk), tk, tq), F32)],
        compiler_params=_params("parallel", "parallel", "arbitrary"),
        name="sb_attention",
    )(qk, qk, vt, u, x, wo)


MLA_QN = MLA_HEADS * MLA_NOPE_DIM
MLA_QR = MLA_HEADS * MLA_ROPE_DIM
MLA_OUT = 2 * MLA_QN + MLA_QR + LANES


def _mla_proj_kernel(x_ref, g_ref, wd_ref, qg_ref, wuq_ref, kvg_ref, wukv_ref, cs_ref, o_ref, vt_ref,
                     *, scale, tk):
    h = _rms(x_ref[...], g_ref[...]).astype(BF16)
    c = _dot(h, wd_ref[...])
    cos = cs_ref[:, :LANES]
    sin = cs_ref[:, LANES:]
    cq = _rms(c[:, :MLA_Q_RANK], qg_ref[...]).astype(BF16)
    q = _dot(cq, wuq_ref[...])
    o_ref[:, :MLA_QN] = (q[:, :MLA_QN] * scale).astype(o_ref.dtype)
    for m in range(MLA_QR // LANES):
        a = MLA_QN + m * LANES
        rot = q[:, a:a + LANES] * cos + q[:, a + MLA_QR:a + MLA_QR + LANES] * sin
        o_ref[:, a:a + LANES] = (rot * scale).astype(o_ref.dtype)
    kv0 = MLA_Q_RANK + MLA_KV_RANK
    ckv = _rms(c[:, MLA_Q_RANK:kv0], kvg_ref[...]).astype(BF16)
    kv = _dot(ckv, wukv_ref[...])
    kn0 = MLA_QN + MLA_QR
    o_ref[:, kn0:kn0 + MLA_QN] = kv[:, :MLA_QN].astype(o_ref.dtype)
    kr = c[:, kv0:kv0 + LANES] * cos + c[:, kv0 + LANES:kv0 + 2 * LANES] * sin
    o_ref[:, MLA_OUT - LANES:] = kr.astype(o_ref.dtype)
    _store_vt(vt_ref, kv[:, MLA_QN:], tk)


def _mla_rope_layout():
    half = MLA_ROPE_DIM // 2
    per = LANES // MLA_ROPE_DIM
    dq = MLA_NOPE_DIM + MLA_ROPE_DIM
    q_nope = np.concatenate([h * dq + np.arange(MLA_NOPE_DIM) for h in range(MLA_HEADS)])
    q_rope = np.zeros(MLA_QR, np.int64)
    q_swap = np.zeros(MLA_QR, np.int64)
    for h in range(MLA_HEADS):
        m, r = divmod(h, per)
        x1 = h * dq + MLA_NOPE_DIM + np.arange(half)
        x2 = x1 + half
        first = m * LANES + r * half + np.arange(half)
        second = first + LANES // 2
        q_rope[first], q_rope[second] = x1, x2
        q_swap[first], q_swap[second] = x2, x1
    k_rope = np.zeros(LANES, np.int64)
    k_swap = np.zeros(LANES, np.int64)
    base = MLA_Q_RANK + MLA_KV_RANK
    for r in range(per):
        first = r * half + np.arange(half)
        second = first + LANES // 2
        k_rope[first], k_rope[second] = base + np.arange(half), base + half + np.arange(half)
        k_swap[first], k_swap[second] = base + half + np.arange(half), base + np.arange(half)
    dkv = MLA_NOPE_DIM + HEAD_DIM
    k_nope = np.concatenate([h * dkv + np.arange(MLA_NOPE_DIM) for h in range(MLA_HEADS)])
    v = k_nope + MLA_NOPE_DIM
    down = np.concatenate([np.arange(base), k_rope, k_swap])
    return np.concatenate([q_nope, q_rope, q_swap]), np.concatenate([k_nope, v]), down


def _mla_rope_tables(seq):
    half = MLA_ROPE_DIM // 2
    inv = ROPE_THETA ** (-jnp.arange(0, MLA_ROPE_DIM, 2, dtype=F32) / MLA_ROPE_DIM)
    ang = jnp.arange(seq, dtype=F32)[:, None] * inv[None, :]
    cos, sin = jnp.cos(ang), jnp.sin(ang)
    reps = LANES // 2 // half
    cos_t = jnp.tile(cos, (1, 2 * reps))
    sin_t = jnp.concatenate([jnp.tile(-sin, (1, reps)), jnp.tile(sin, (1, reps))], axis=1)
    return jnp.concatenate([cos_t, sin_t], axis=1)


def _mla_proj(x, g, w_down, q_norm, w_uq, kv_norm, w_ukv, seq):
    t, d = x.shape
    tm = _tile(seq, 512)
    tk = _tile(seq, ATT_TK)
    ns = seq // tm
    uq_cols, ukv_cols, down_cols = _mla_rope_layout()
    wd = w_down[:, down_cols].astype(BF16)
    wuq = w_uq[:, uq_cols].astype(BF16)
    wukv = w_ukv[:, ukv_cols].astype(BF16)
    cs = _mla_rope_tables(seq)
    scale = (MLA_NOPE_DIM + MLA_ROPE_DIM) ** -0.5 * LOG2E
    full = lambda a: pl.BlockSpec(a.shape, lambda i: (0, 0))
    g2, qg, kvg = g.reshape(1, d), q_norm.reshape(1, -1), kv_norm.reshape(1, -1)
    return pl.pallas_call(
        functools.partial(_mla_proj_kernel, scale=scale, tk=tk),
        grid=(t // tm,),
        in_specs=[pl.BlockSpec((tm, d), lambda i: (i, 0)), full(g2), full(wd), full(qg), full(wuq),
                  full(kvg), full(wukv), pl.BlockSpec((tm, 2 * LANES), lambda i: (i % ns, 0))],
        out_specs=[pl.BlockSpec((tm, MLA_OUT), lambda i: (i, 0)),
                   pl.BlockSpec((tm // tk, MLA_QN, tk), lambda i: (i, 0, 0))],
        out_shape=[jax.ShapeDtypeStruct((t, MLA_OUT), BF16),
                   jax.ShapeDtypeStruct((t // tk, MLA_QN, tk), BF16)],
        compiler_params=_params("parallel"),
        name="mla_proj",
    )(x, g2, wd, qg, wuq, kvg, wukv, cs)


MLA_STEP_HEADS = LANES // MLA_ROPE_DIM


def _mla_kernel(qn_ref, qr_ref, kn_ref, kr_ref, vt_ref, x_ref, wo_ref, o_ref, acc_ref, ml_ref, s_ref, *, tq, tk):
    i = pl.program_id(1)
    heads = MLA_STEP_HEADS
    lane = lax.broadcasted_iota(jnp.int32, (tq, 2 * LANES), 1)
    half = MLA_ROPE_DIM // 2
    qts = []
    for h in range(heads):
        pair, j = divmod(h, 2)
        q = jnp.concatenate([qn_ref[:, pair * LANES:(pair + 1) * LANES], qr_ref[...]], axis=1)
        r0 = LANES + h * half
        sel = jnp.logical_and(lane >= j * HEAD_DIM, lane < (j + 1) * HEAD_DIM)
        sel = jnp.logical_or(sel, jnp.logical_and(lane >= r0, lane < r0 + half))
        r1 = r0 + LANES // 2
        sel = jnp.logical_or(sel, jnp.logical_and(lane >= r1, lane < r1 + half))
        qts.append(_transposed(jnp.where(sel, q, jnp.zeros_like(q))))
    acc_ref[...] = jnp.zeros_like(acc_ref)
    row = lax.broadcasted_iota(jnp.int32, ml_ref.shape, 0)
    ml_ref[...] = jnp.where(row < heads, NEG, 0.0)

    def scores(unit, _):
        jb, h, lanes, _ = unit
        start = pl.multiple_of(jb * tk, tk)
        pair = h // 2
        kt = jnp.concatenate([kn_ref[pl.ds(start, tk), pair * LANES:(pair + 1) * LANES],
                              kr_ref[pl.ds(start, tk), :]], axis=1)
        return _dot(kt, qts[h][:, lanes])

    def softmax(unit, s):
        _, h, lanes, mask = unit
        if mask is not None:
            s = jnp.where(mask, s, NEG)
        m_old = ml_ref[h:h + 1, lanes]
        m_new = jnp.maximum(m_old, jnp.max(s, axis=0, keepdims=True))
        alpha = jnp.exp2(m_old - m_new)
        e = jnp.exp2(s - m_new)
        ml_ref[h:h + 1, lanes] = m_new
        l_row = slice(heads + h, heads + h + 1)
        ml_ref[l_row, lanes] = alpha * ml_ref[l_row, lanes] + jnp.sum(e, axis=0, keepdims=True)
        return alpha, e.astype(BF16)

    def values(unit, val):
        jb, h, lanes, _ = unit
        alpha, e = val
        rows = slice(h * HEAD_DIM, (h + 1) * HEAD_DIM)
        acc_ref[rows, lanes] = alpha * acc_ref[rows, lanes] + _dot(vt_ref[jb, rows, :], e)

    _causal_sweep(i, tq, tk, (scores, softmax, values), False, s_ref, heads)
    head = lax.broadcasted_iota(jnp.int32, acc_ref.shape, 0) // HEAD_DIM
    denom = ml_ref[heads:heads + 1, :]
    for h in range(1, heads):
        denom = jnp.where(head == h, ml_ref[heads + h:heads + h + 1, :], denom)
    _add_projected(x_ref, wo_ref, o_ref, (acc_ref[...] / denom).T.astype(BF16), pl.program_id(2) == 0)


def _mla_attention(a, vt, x, wo, batch, seq):
    t, d = x.shape
    tq = _tile(seq, ATT_TQ)
    tk = vt.shape[2]
    nq = seq // tq
    groups = MLA_HEADS // MLA_STEP_HEADS
    gw = MLA_STEP_HEADS * HEAD_DIM
    qr0 = MLA_QN // LANES
    kn0 = (MLA_QN + MLA_QR) // gw
    kr0 = (MLA_QN + MLA_QR + MLA_QN) // LANES
    return pl.pallas_call(
        functools.partial(_mla_kernel, tq=tq, tk=tk),
        grid=(batch, nq, groups),
        in_specs=[pl.BlockSpec((tq, gw), lambda b, i, p: (b * nq + i, p)),
                  pl.BlockSpec((tq, LANES), lambda b, i, p: (b * nq + i, qr0 + p)),
                  pl.BlockSpec((seq, gw), lambda b, i, p: (b, kn0 + p)),
                  pl.BlockSpec((seq, LANES), lambda b, i, p: (b, kr0)),
                  pl.BlockSpec((seq // tk, gw, tk), lambda b, i, p: (b, p, 0)),
                  pl.BlockSpec((tq, d), lambda b, i, p: (b * nq + i, 0)),
                  pl.BlockSpec((gw, d), lambda b, i, p: (p, 0))],
        out_specs=pl.BlockSpec((tq, d), lambda b, i, p: (b * nq + i, 0)),
        out_shape=jax.ShapeDtypeStruct((t, d), F32),
        scratch_shapes=[pltpu.VMEM((gw, tq), F32), pltpu.VMEM((2 * MLA_STEP_HEADS, tq), F32),
                        pltpu.VMEM((MLA_STEP_HEADS * (tq // tk), tk, tq), F32)],
        compiler_params=_params("parallel", "parallel", "arbitrary"),
        name="mla_attention",
    )(a, a, a, a, vt, x, wo)


FFN_CHUNK = 256


def _swiglu_into(acc_ref, h, wg, wu, wd):
    def gate_up(cols, _):
        return _dot(h, wg(cols)), _dot(h, wu(cols))

    def activate(cols, gu):
        g, u = gu
        return (g * jax.nn.sigmoid(g) * u).astype(BF16)

    def down(cols, a):
        acc_ref[...] += _dot(a, wd(cols))

    return gate_up, activate, down


def _run_swiglu(acc_ref, h, wg, wu, wd, width):
    chunks = [slice(c, min(c + FFN_CHUNK, width)) for c in range(0, width, FFN_CHUNK)]
    _staggered(chunks, _swiglu_into(acc_ref, h, wg, wu, wd))


def _ffn_kernel(x_ref, g_ref, wg_ref, wu_ref, wd_ref, o_ref, h_ref):
    f = pl.program_id(1)

    @pl.when(f == 0)
    def _():
        x = x_ref[...]
        h_ref[...] = _rms(x, g_ref[...]).astype(BF16)
        o_ref[...] = x

    _run_swiglu(o_ref, h_ref[...], lambda c: wg_ref[:, c], lambda c: wu_ref[:, c], lambda c: wd_ref[c, :],
                wg_ref.shape[1])


def _ffn(x, g, wg, wu, wd):
    t, d = x.shape
    fdim = wg.shape[1]
    tm = _tile(t, 512)
    tf = fdim
    return pl.pallas_call(
        _ffn_kernel,
        grid=(t // tm, fdim // tf),
        in_specs=[pl.BlockSpec((tm, d), lambda i, f: (i, 0)),
                  pl.BlockSpec((1, d), lambda i, f: (0, 0)),
                  pl.BlockSpec((d, tf), lambda i, f: (0, f)),
                  pl.BlockSpec((d, tf), lambda i, f: (0, f)),
                  pl.BlockSpec((tf, d), lambda i, f: (f, 0))],
        out_specs=pl.BlockSpec((tm, d), lambda i, f: (i, 0)),
        out_shape=jax.ShapeDtypeStruct((t, d), F32),
        scratch_shapes=[pltpu.VMEM((tm, d), BF16)],
        compiler_params=_params("parallel", "arbitrary"),
        name="dense_ffn",
    )(x, g.reshape(1, d), wg, wu, wd)


MOE_TILE = 512
META_ROWS = 8
ROUTE_SUB = 128
COMBINE_SUB = 256


def _pack_halves(x):
    half = x.shape[1] // 2
    bits = lax.bitcast_convert_type(x.astype(BF16).astype(F32), jnp.uint32)
    return bits[:, :half] | (bits[:, half:] >> 16)


def _unpack_halves(w):
    hi = lax.bitcast_convert_type(w & jnp.uint32(0xFFFF0000), F32)
    lo = lax.bitcast_convert_type(w << 16, F32)
    return jnp.concatenate([hi, lo], axis=1)


def _router_kernel(x_ref, g_ref, rt_ref, su_ref, eye_ref, hb_ref, meta_ref, metac_ref, cnt_ref, carry_ref):
    @pl.when(pl.program_id(0) == 0)
    def _():
        carry_ref[...] = jnp.zeros_like(carry_ref)

    h = _rms(x_ref[...], g_ref[...])
    hb_ref[...] = _pack_halves(h)
    ne = rt_ref.shape[0]
    r = rt_ref[...]
    r_hi = r.astype(BF16).astype(F32)
    h_hi = h.astype(BF16)
    h_lo = (h - h_hi.astype(F32)).astype(BF16)
    both_r = _dot_nt(jnp.concatenate([r_hi, r - r_hi], axis=0).astype(BF16), h_hi)
    logits = both_r[:ne] + both_r[ne:] + _dot_nt(r_hi.astype(BF16), h_lo)
    ne, ct = logits.shape
    eio = lax.broadcasted_iota(jnp.int32, (ne, ct), 0).astype(F32)
    m1 = jnp.max(logits, axis=0, keepdims=True)
    i1 = jnp.min(jnp.where(logits == m1, eio, float(ne)), axis=0, keepdims=True)
    rest = jnp.where(eio == i1, -jnp.inf, logits)
    m2 = jnp.max(rest, axis=0, keepdims=True)
    i2 = jnp.min(jnp.where(rest == m2, eio, float(ne)), axis=0, keepdims=True)
    e2 = jnp.exp(m2 - m1)
    g1 = 1.0 / (1.0 + e2)
    g2 = e2 / (1.0 + e2)
    oh1 = (eio == i1).astype(F32)
    oh2 = (eio == i2).astype(F32)
    both = oh1 + oh2
    seen = carry_ref[:, 0:1] + _dot(both.astype(BF16), su_ref[...])
    r1 = jnp.sum(oh1 * seen, axis=0, keepdims=True)
    r2 = jnp.sum(oh2 * seen, axis=0, keepdims=True)
    meta = jnp.zeros((META_ROWS, ct), F32)
    for row, val in enumerate((i1, i2, r1, r2, g1, g2)):
        meta = jnp.where(eio == float(row), val, meta)
    meta_ref[...] = meta
    metac_ref[...] = lax.dot_general(eye_ref[...], meta, _NT, precision=lax.Precision.HIGHEST,
                                     preferred_element_type=F32)
    carry_ref[...] = carry_ref[...] + jnp.sum(both, axis=1, keepdims=True)
    cnt_ref[0] = carry_ref[...]


def _router(x, g, router):
    t, d = x.shape
    ct = _tile(t, MOE_TILE)
    nc = t // ct
    su = jnp.asarray(np.triu(np.ones((ct, ct), np.float32), 1), BF16)
    eye = jnp.eye(ct, dtype=F32)
    return pl.pallas_call(
        _router_kernel,
        grid=(nc,),
        in_specs=[pl.BlockSpec((ct, d), lambda i: (i, 0)),
                  pl.BlockSpec((1, d), lambda i: (0, 0)),
                  pl.BlockSpec((N_EXPERTS, d), lambda i: (0, 0)),
                  pl.BlockSpec((ct, ct), lambda i: (0, 0)),
                  pl.BlockSpec((ct, ct), lambda i: (0, 0))],
        out_specs=[pl.BlockSpec((ct, d // 2), lambda i: (i, 0)),
                   pl.BlockSpec((META_ROWS, ct), lambda i: (0, i)),
                   pl.BlockSpec((ct, META_ROWS), lambda i: (i, 0)),
                   pl.BlockSpec((1, N_EXPERTS, LANES), lambda i: (i, 0, 0))],
        out_shape=[jax.ShapeDtypeStruct((t, d // 2), jnp.uint32),
                   jax.ShapeDtypeStruct((META_ROWS, t), F32),
                   jax.ShapeDtypeStruct((t, META_ROWS), F32),
                   jax.ShapeDtypeStruct((nc, N_EXPERTS, LANES), F32)],
        scratch_shapes=[pltpu.VMEM((N_EXPERTS, LANES), F32)],
        compiler_params=_params("arbitrary"),
        name="moe_router",
    )(x, g.reshape(1, d), router.T, su, eye)


def _group_layout(cnt_after, t, rt):
    rows = 2 * t + N_EXPERTS * rt
    counts = cnt_after[:, :, 0].astype(jnp.int32)
    padded = (counts[-1] + rt - 1) // rt * rt
    off_end = jnp.cumsum(padded)
    off = (off_end - padded).astype(jnp.int32)
    tile_start = jnp.arange(rows // rt, dtype=jnp.int32) * rt
    tile_expert = jnp.minimum(_count_le(off_end, tile_start), N_EXPERTS - 1).astype(jnp.int32)
    tile_valid = (tile_start < off_end[-1]).astype(jnp.int32)
    return rows, off, counts, tile_expert, tile_valid


def _count_le(sorted_vals, x):
    return jnp.sum((sorted_vals[None, :] <= x[:, None]).astype(jnp.int32), axis=1)


def _work_lists(counts, off, rows, t, ct):
    nc = t // ct
    nrt = rows // ct
    before = jnp.concatenate([jnp.zeros((1, N_EXPERTS), jnp.int32), counts[:-1]], axis=0)
    chunk_start = (off[:, None] + before.T).reshape(-1)
    tile_start = jnp.arange(nrt, dtype=jnp.int32) * ct
    starts = jnp.sort(jnp.concatenate([chunk_start, tile_start]))
    ends = jnp.concatenate([starts[1:], jnp.array([rows], jnp.int32)])
    item_r = jnp.minimum(starts // ct, nrt - 1)
    item_c = (_count_le(chunk_start, starts) - 1) % nc
    first = lambda key: jnp.concatenate([jnp.ones((1,), jnp.int32), (key[1:] != key[:-1]).astype(jnp.int32)])
    dispatch = (item_r, item_c, starts, ends, first(item_r))
    order = jnp.argsort(item_c * (rows + 1) + starts)
    c2 = item_c[order]
    combine = (c2, item_r[order], starts[order], ends[order], first(c2))
    return dispatch, combine


def _positions(expert, rank, off_ref):
    base = jnp.zeros_like(rank)
    for e in range(N_EXPERTS):
        base = jnp.where(expert == float(e), off_ref[e].astype(F32), base)
    return base + rank


def _dispatch_kernel(r_ref, c_ref, s_ref, e_ref, first_ref, off_ref, meta_ref, h_ref, xs_ref, gate_ref):
    n = pl.program_id(0)
    rt, ct = xs_ref.shape[0], h_ref.shape[0]

    @pl.when(first_ref[n] == 1)
    def _():
        xs_ref[...] = jnp.zeros_like(xs_ref)
        gate_ref[...] = jnp.zeros_like(gate_ref)

    meta = meta_ref[...]
    pos0 = _positions(meta[0:1], meta[2:3], off_ref)
    pos1 = _positions(meta[1:2], meta[3:4], off_ref)
    sub = min(rt, ROUTE_SUB)
    for sb in range(rt // sub):
        base = r_ref[n] * rt + sb * sub

        @pl.when(jnp.logical_and(base < e_ref[n], base + sub > s_ref[n]))
        def _():
            row = lax.broadcasted_iota(jnp.int32, (sub, ct), 0) + base
            live = jnp.logical_and(row >= s_ref[n], row < e_ref[n])
            rowf = row.astype(F32)
            sel0 = jnp.logical_and(pos0 == rowf, live)
            sel1 = jnp.logical_and(pos1 == rowf, live)
            sel = jnp.logical_or(sel0, sel1).astype(F32).astype(BF16)
            rows = slice(sb * sub, (sb + 1) * sub)
            xs_ref[rows, :] += _pack_halves(_dot(sel, _unpack_halves(h_ref[...]).astype(BF16)))
            gate = jnp.where(sel0, meta[4:5], 0.0) + jnp.where(sel1, meta[5:6], 0.0)
            gate_ref[rows, :] += jnp.sum(gate, axis=1, keepdims=True)


def _dispatch(plan, off, meta, hb, rows, ct):
    t, d = hb.shape
    n_items = plan[0].shape[0]
    return pl.pallas_call(
        _dispatch_kernel,
        grid_spec=pltpu.PrefetchScalarGridSpec(
            num_scalar_prefetch=6,
            grid=(n_items,),
            in_specs=[pl.BlockSpec((META_ROWS, ct), lambda n, r, c, *_: (0, c[n])),
                      pl.BlockSpec((ct, d), lambda n, r, c, *_: (c[n], 0))],
            out_specs=[pl.BlockSpec((ct, d), lambda n, r, c, *_: (r[n], 0)),
                       pl.BlockSpec((ct, 1), lambda n, r, c, *_: (r[n], 0))]),
        out_shape=[jax.ShapeDtypeStruct((rows, d), jnp.uint32),
                   jax.ShapeDtypeStruct((rows, 1), F32)],
        compiler_params=_params("arbitrary"),
        name="moe_dispatch",
    )(*plan, off, meta, hb)


def _expert_kernel(te_ref, tv_ref, xs_ref, gate_ref, wg_ref, wu_ref, wd_ref, y_ref, acc_ref):
    r = pl.program_id(0)
    f = pl.program_id(1)

    @pl.when(f == 0)
    def _():
        acc_ref[...] = jnp.zeros_like(acc_ref)

    @pl.when(tv_ref[r] == 1)
    def _():
        _run_swiglu(acc_ref, _unpack_halves(xs_ref[...]).astype(BF16), lambda c: wg_ref[0, :, c],
                    lambda c: wu_ref[0, :, c], lambda c: wd_ref[0, c, :], wg_ref.shape[2])

    @pl.when(f == pl.num_programs(1) - 1)
    def _():
        y_ref[...] = _pack_halves(acc_ref[...] * gate_ref[...])


def _experts(tile_expert, tile_valid, xs, gate, wg, wu, wd, ct):
    rows, dp = xs.shape
    d = 2 * dp
    fdim = wg.shape[2]
    tf = _tile(fdim, 1792)
    nf = fdim // tf

    def wcol(r, f, te, tv):
        return (te[r], 0, jnp.where(tv[r] == 1, f, nf - 1))

    def wrow(r, f, te, tv):
        return (te[r], jnp.where(tv[r] == 1, f, nf - 1), 0)

    return pl.pallas_call(
        _expert_kernel,
        grid_spec=pltpu.PrefetchScalarGridSpec(
            num_scalar_prefetch=2,
            grid=(rows // ct, nf),
            in_specs=[pl.BlockSpec((ct, dp), lambda r, f, te, tv: (r, 0)),
                      pl.BlockSpec((ct, 1), lambda r, f, te, tv: (r, 0)),
                      pl.BlockSpec((1, d, tf), wcol),
                      pl.BlockSpec((1, d, tf), wcol),
                      pl.BlockSpec((1, tf, d), wrow)],
            out_specs=pl.BlockSpec((ct, dp), lambda r, f, te, tv: (r, 0)),
            scratch_shapes=[pltpu.VMEM((ct, d), F32)]),
        out_shape=jax.ShapeDtypeStruct((rows, dp), jnp.uint32),
        compiler_params=_params("arbitrary", "arbitrary"),
        name="moe_experts",
    )(tile_expert, tile_valid, xs, gate, wg, wu, wd)


def _combine_kernel(c_ref, r_ref, s_ref, e_ref, first_ref, off_ref, x_ref, metac_ref, y_ref, o_ref):
    n = pl.program_id(0)
    ct, rt = x_ref.shape[0], y_ref.shape[0]

    @pl.when(first_ref[n] == 1)
    def _():
        o_ref[...] = x_ref[...]

    metac = metac_ref[...]
    pos0 = _positions(metac[:, 0:1], metac[:, 2:3], off_ref)
    pos1 = _positions(metac[:, 1:2], metac[:, 3:4], off_ref)
    sub = min(rt, COMBINE_SUB)
    for sb in range(rt // sub):
        base = r_ref[n] * rt + sb * sub

        @pl.when(jnp.logical_and(base < e_ref[n], base + sub > s_ref[n]))
        def _():
            row = lax.broadcasted_iota(jnp.int32, (ct, sub), 1) + base
            live = jnp.logical_and(row >= s_ref[n], row < e_ref[n])
            rowf = row.astype(F32)
            sel = jnp.logical_and(jnp.logical_or(pos0 == rowf, pos1 == rowf), live).astype(F32).astype(BF16)
            o_ref[...] += _dot(sel, _unpack_halves(y_ref[sb * sub:(sb + 1) * sub, :]).astype(BF16))


def _combine(plan, off, x, metac, y, ct):
    t, d = x.shape
    n_items = plan[0].shape[0]
    return pl.pallas_call(
        _combine_kernel,
        grid_spec=pltpu.PrefetchScalarGridSpec(
            num_scalar_prefetch=6,
            grid=(n_items,),
            in_specs=[pl.BlockSpec((ct, d), lambda n, c, r, *_: (c[n], 0)),
                      pl.BlockSpec((ct, META_ROWS), lambda n, c, r, *_: (c[n], 0)),
                      pl.BlockSpec((ct, d // 2), lambda n, c, r, *_: (r[n], 0))],
            out_specs=pl.BlockSpec((ct, d), lambda n, c, r, *_: (c[n], 0))),
        out_shape=jax.ShapeDtypeStruct((t, d), F32),
        compiler_params=_params("arbitrary"),
        name="moe_combine",
    )(*plan, off, x, metac, y)


def _cast_kernel(w_ref, o_ref):
    o_ref[...] = w_ref[...].astype(o_ref.dtype)


CAST_BLOCK_BYTES = 8 * 1024 * 1024


def _layer_bf16(w, layer):
    shape = w.shape[1:]
    cols = shape[-1]
    rows = math.prod(shape[:-1])
    tr = rows
    while tr * cols * 4 > CAST_BLOCK_BYTES and tr % 16 == 0:
        tr //= 2
    out = pl.pallas_call(
        _cast_kernel,
        grid=(rows // tr,),
        in_specs=[pl.BlockSpec((None, tr, cols), lambda i: (layer, i, 0))],
        out_specs=pl.BlockSpec((tr, cols), lambda i: (i, 0)),
        out_shape=jax.ShapeDtypeStruct((rows, cols), BF16),
        compiler_params=_params("parallel"),
        name="weights_bf16",
    )(w.reshape(w.shape[0], rows, cols))
    return out.reshape(shape)


SC_CORES = 2
SC_SUBCORES = 16
SC_CHUNK = 128


def _sc_worker_chunks(total):
    per = total // (SC_CORES * SC_SUBCORES)
    wid = lax.axis_index("s") * SC_CORES + lax.axis_index("c")
    return wid * per, per // SC_CHUNK


def _sc_mesh():
    return plsc.VectorSubcoreMesh(core_axis_name="c", subcore_axis_name="s")


def _sc_gather(table, idx):
    v, d = table.shape
    b = idx.shape[0]

    def body(table_hbm, idx_hbm, out_hbm, idx_v, rows_v, sem):
        base, chunks = _sc_worker_chunks(b)

        @pl.loop(0, chunks)
        def _(j):
            start = base + j * SC_CHUNK
            pltpu.sync_copy(idx_hbm.at[pl.ds(start, SC_CHUNK)], idx_v)
            pltpu.async_copy(table_hbm.at[idx_v], rows_v, sem).wait()
            pltpu.sync_copy(rows_v, out_hbm.at[pl.ds(start, SC_CHUNK)])

    return pl.kernel(
        body,
        out_type=jax.ShapeDtypeStruct((b, d), table.dtype),
        mesh=_sc_mesh(),
        scratch_types=[pltpu.VMEM((SC_CHUNK,), jnp.int32),
                       pltpu.VMEM((SC_CHUNK, d), table.dtype),
                       pltpu.SemaphoreType.DMA],
        name="sc_row_gather",
    )(table, idx)


def _sc_scatter_rows(src, idx, n):
    t, d = src.shape
    b = idx.shape[0]

    def body(src_hbm, idx_hbm, out_hbm, idx_v, rows_v):
        base, chunks = _sc_worker_chunks(b)

        @pl.loop(0, chunks)
        def _(j):
            start = base + j * SC_CHUNK
            first = lax.rem(start, t)
            pltpu.sync_copy(idx_hbm.at[pl.ds(start, SC_CHUNK)], idx_v)
            pltpu.sync_copy(src_hbm.at[pl.ds(first, SC_CHUNK)], rows_v)
            pltpu.sync_copy(rows_v, out_hbm.at[idx_v])

    return pl.kernel(
        body,
        out_type=jax.ShapeDtypeStruct((n, d), src.dtype),
        mesh=_sc_mesh(),
        scratch_types=[pltpu.VMEM((SC_CHUNK,), jnp.int32), pltpu.VMEM((SC_CHUNK, d), src.dtype)],
        name="sc_row_scatter",
    )(src, idx)


def _weighted_add_kernel(x_ref, metac_ref, y0_ref, y1_ref, g_ref, o_ref, *, normed):
    gates = metac_ref[...]
    out = (x_ref[...] + gates[:, 4:5] * _unpack_halves(y0_ref[...])
           + gates[:, 5:6] * _unpack_halves(y1_ref[...]))
    o_ref[...] = _rms(out, g_ref[...]) if normed else out


def _weighted_add(x, metac, ysel, out_gain=None):
    t, d = x.shape
    tm = _tile(t, 512)
    nt = t // tm
    gain = jnp.ones((d,), F32) if out_gain is None else out_gain
    return pl.pallas_call(
        functools.partial(_weighted_add_kernel, normed=out_gain is not None),
        grid=(nt,),
        in_specs=[pl.BlockSpec((tm, d), lambda i: (i, 0)),
                  pl.BlockSpec((tm, META_ROWS), lambda i: (i, 0)),
                  pl.BlockSpec((tm, d // 2), lambda i: (i, 0)),
                  pl.BlockSpec((tm, d // 2), lambda i: (nt + i, 0)),
                  pl.BlockSpec((1, d), lambda i: (0, 0))],
        out_specs=pl.BlockSpec((tm, d), lambda i: (i, 0)),
        out_shape=jax.ShapeDtypeStruct((t, d), F32),
        compiler_params=_params("parallel"),
        name="moe_weighted_add",
    )(x, metac, ysel, ysel, gain.reshape(1, d))


def _moe(x, g, router, wg, wu, wd, out_gain=None):
    t = x.shape[0]
    ct = _tile(t, MOE_TILE)
    hb, meta, metac, cnt = _router(x, g, router)
    sc_rows = SC_CORES * SC_SUBCORES * SC_CHUNK
    rows, off, counts, tile_expert, tile_valid = _group_layout(cnt, t, ct)
    if (2 * t) % sc_rows == 0 and rows % sc_rows == 0:
        base = sum(jnp.where(meta[0:2] == float(e), off[e], 0) for e in range(N_EXPERTS))
        pos = (base + meta[2:4].astype(jnp.int32)).reshape(-1)
        xs = _sc_scatter_rows(hb, pos, rows)
        y = _experts(tile_expert, tile_valid, xs, jnp.ones((rows, 1), F32), wg, wu, wd, ct)
        return _weighted_add(x, metac, _sc_gather(y, pos), out_gain)
    dispatch, combine = _work_lists(counts, off, rows, t, ct)
    xs, gate = _dispatch(dispatch, off, meta, hb, rows, ct)
    y = _experts(tile_expert, tile_valid, xs, gate, wg, wu, wd, ct)
    out = _combine(combine, off, x, metac, y, ct)
    return out if out_gain is None else _final_norm(out, out_gain)


def _final_norm_kernel(x_ref, g_ref, o_ref):
    o_ref[...] = _rms(x_ref[...], g_ref[...])


def _final_norm(x, g):
    t, d = x.shape
    tm = _tile(t, 1024)
    return pl.pallas_call(
        _final_norm_kernel,
        grid=(t // tm,),
        in_specs=[pl.BlockSpec((tm, d), lambda i: (i, 0)), pl.BlockSpec((1, d), lambda i: (0, 0))],
        out_specs=pl.BlockSpec((tm, d), lambda i: (i, 0)),
        out_shape=jax.ShapeDtypeStruct((t, d), F32),
        compiler_params=_params("parallel"),
        name="final_norm",
    )(x, g.reshape(1, d))


def kernel(x, rel_bias, attn_norm, ffn_norm, final_norm, swa_w_qkv, swa_b_qkv, swa_sinks, swa_w_o, sb_w_qkv, sb_w_o, mla_w_down, mla_q_norm, mla_w_uq, mla_kv_norm, mla_w_ukv, mla_w_o, ffn_w_gate, ffn_w_up, ffn_w_down, moe_router, moe_w_gate, moe_w_up, moe_w_down):
    batch, seq, d = x.shape
    depth = attn_norm.shape[0]
    xt = x.reshape(batch * seq, d)
    swa_bias = _swa_bias(rel_bias)
    for i in range(depth):
        mixer, j = i % N_MIXERS, i // N_MIXERS
        if mixer == 0:
            w, b, cs = _swa_weights(swa_w_qkv[j], swa_b_qkv[j])
            qkv = _norm_proj(xt, attn_norm[i], w, b, cs, "swa_proj")
            xt = _swa_attention(qkv, swa_bias, swa_sinks[j], xt, swa_w_o[j].astype(BF16), batch, seq)
        elif mixer == 1:
            n = sb_w_qkv.shape[2] // 3
            cs = jnp.concatenate([jnp.full((n,), HEAD_DIM ** -0.5 * LOG2E, F32), jnp.ones((n,), F32)])
            qk, vt = _norm_proj_vt(xt, attn_norm[i], sb_w_qkv[j].astype(BF16), cs, n, seq, "sb_proj")
            xt = _sb_attention(qk, vt, xt, sb_w_o[j].astype(BF16), batch, seq)
        else:
            a, vt = _mla_proj(xt, attn_norm[i], mla_w_down[j], mla_q_norm[j], mla_w_uq[j],
                              mla_kv_norm[j], mla_w_ukv[j], seq)
            xt = _mla_attention(a, vt, xt, mla_w_o[j].astype(BF16), batch, seq)
        f = i // 2
        if i % 2 == 0:
            xt = _ffn(xt, ffn_norm[i], _layer_bf16(ffn_w_gate, f), _layer_bf16(ffn_w_up, f),
                      _layer_bf16(ffn_w_down, f))
        else:
            last = final_norm if i == depth - 1 else None
            xt = _moe(xt, ffn_norm[i], moe_router[f], _layer_bf16(moe_w_gate, f), _layer_bf16(moe_w_up, f),
                      _layer_bf16(moe_w_down, f), last)
    if depth % 2 == 1:
        xt = _final_norm(xt, final_norm)
    return xt.reshape(batch, seq, d)
```

```python
import functools
import math

import numpy as np
import jax
import jax.numpy as jnp
from jax import lax
from jax.experimental import pallas as pl
from jax.experimental.pallas import tpu as pltpu
from jax.experimental.pallas import tpu_sc as plsc

F32 = jnp.float32
BF16 = jnp.bfloat16

N_MIXERS = 3
RMS_EPS = 1e-6
SWA_HEADS = 16
SWA_KV_HEADS = 2
HEAD_DIM = 64
SWA_BLOCK = 128
SWA_WINDOW = 128
REL_BUCKETS = 32
REL_MAX_DIST = 128
SB_HEADS = 16
MLA_HEADS = 16
MLA_NOPE_DIM = 64
MLA_ROPE_DIM = 32
MLA_Q_RANK = 384
MLA_KV_RANK = 256
ROPE_THETA = 10000.0
N_EXPERTS = 8

LANES = 128
NEG = -1e30
LOG2E = math.log2(math.e)
VMEM_LIMIT = 56 * 1024 * 1024

_NT = (((1,), (1,)), ((), ()))


def _dot(a, b):
    return jnp.dot(a, b, preferred_element_type=F32)


def _dot_nt(a, b):
    return lax.dot_general(a, b, _NT, preferred_element_type=F32)


def _rms(x, g):
    return x * lax.rsqrt(jnp.mean(x * x, axis=-1, keepdims=True) + RMS_EPS) * g


def _params(*sem):
    return pltpu.CompilerParams(dimension_semantics=sem, vmem_limit_bytes=VMEM_LIMIT)


def _tile(n, pref):
    return pref if n % pref == 0 else n


def _norm_proj_kernel(x_ref, g_ref, w_ref, b_ref, cs_ref, o_ref):
    h = _rms(x_ref[...], g_ref[...]).astype(BF16)
    y = (_dot(h, w_ref[...]) + b_ref[...]) * cs_ref[...]
    o_ref[...] = y.astype(o_ref.dtype)


def _norm_proj(x, g, w, b, colscale, name):
    t, d = x.shape
    n = w.shape[1]
    tm = _tile(t, 512)
    return pl.pallas_call(
        _norm_proj_kernel,
        grid=(t // tm,),
        in_specs=[pl.BlockSpec((tm, d), lambda i: (i, 0)),
                  pl.BlockSpec((1, d), lambda i: (0, 0)),
                  pl.BlockSpec((d, n), lambda i: (0, 0)),
                  pl.BlockSpec((1, n), lambda i: (0, 0)),
                  pl.BlockSpec((1, n), lambda i: (0, 0))],
        out_specs=pl.BlockSpec((tm, n), lambda i: (i, 0)),
        out_shape=jax.ShapeDtypeStruct((t, n), BF16),
        compiler_params=_params("parallel"),
        name=name,
    )(x, g.reshape(1, d), w, b.reshape(1, n), colscale.reshape(1, n))


ATT_TQ = 512
ATT_TK = 256


def _store_vt(vt_ref, v, tk):
    for s in range(v.shape[0] // tk):
        vt_ref[s] = v[s * tk:(s + 1) * tk, :].T.astype(vt_ref.dtype)


def _norm_proj_vt_kernel(x_ref, g_ref, w_ref, cs_ref, o_ref, vt_ref, *, tk):
    h = _rms(x_ref[...], g_ref[...]).astype(BF16)
    y = _dot(h, w_ref[...])
    n = o_ref.shape[1]
    o_ref[...] = (y[:, :n] * cs_ref[...]).astype(o_ref.dtype)
    _store_vt(vt_ref, y[:, n:], tk)


def _norm_proj_vt(x, g, w, colscale, nv, seq, name):
    t, d = x.shape
    n = w.shape[1] - nv
    tm = _tile(seq, 512)
    tk = _tile(seq, ATT_TK)
    per = tm // tk
    return pl.pallas_call(
        functools.partial(_norm_proj_vt_kernel, tk=tk),
        grid=(t // tm,),
        in_specs=[pl.BlockSpec((tm, d), lambda i: (i, 0)),
                  pl.BlockSpec((1, d), lambda i: (0, 0)),
                  pl.BlockSpec((d, n + nv), lambda i: (0, 0)),
                  pl.BlockSpec((1, n), lambda i: (0, 0))],
        out_specs=[pl.BlockSpec((tm, n), lambda i: (i, 0)),
                   pl.BlockSpec((per, nv, tk), lambda i: (i, 0, 0))],
        out_shape=[jax.ShapeDtypeStruct((t, n), BF16),
                   jax.ShapeDtypeStruct((t // tk, nv, tk), BF16)],
        compiler_params=_params("parallel"),
        name=name,
    )(x, g.reshape(1, d), w, colscale.reshape(1, n))


def _out_proj_kernel(x_ref, o_ref, w_ref, y_ref):
    y_ref[...] = x_ref[...] + _dot(o_ref[...], w_ref[...])


def _out_proj(x, o, w, name):
    t, d = x.shape
    k = o.shape[1]
    tm = _tile(t, 512)
    return pl.pallas_call(
        _out_proj_kernel,
        grid=(t // tm,),
        in_specs=[pl.BlockSpec((tm, d), lambda i: (i, 0)),
                  pl.BlockSpec((tm, k), lambda i: (i, 0)),
                  pl.BlockSpec((k, d), lambda i: (0, 0))],
        out_specs=pl.BlockSpec((tm, d), lambda i: (i, 0)),
        out_shape=jax.ShapeDtypeStruct((t, d), F32),
        compiler_params=_params("parallel"),
        name=name,
    )(x, o, w)


def _t5_bucket_table():
    qi = np.arange(SWA_BLOCK)[:, None]
    kj = np.arange(2 * SWA_BLOCK)[None, :]
    dist = qi + SWA_BLOCK - kj
    d0 = np.maximum(dist, 0)
    max_exact = REL_BUCKETS // 2
    d = np.maximum(d0, 1).astype(np.float32)
    large = max_exact + (np.log(d / max_exact) / math.log(REL_MAX_DIST / max_exact)
                         * (REL_BUCKETS - max_exact)).astype(np.int32)
    large = np.minimum(large, REL_BUCKETS - 1)
    bucket = np.where(d0 < max_exact, d0, large)
    band = (dist >= 0) & (dist < SWA_WINDOW)
    return np.where(band, bucket, -1).astype(np.int32)


def _swa_bias_kernel(rel_ref, bucket_ref, o_ref):
    h = pl.program_id(0)
    bucket = bucket_ref[...]
    acc = jnp.full(bucket.shape, NEG, F32)
    for b in range(REL_BUCKETS):
        acc = jnp.where(bucket == b, rel_ref[b, h] * LOG2E, acc)
    o_ref[0] = acc


def _swa_bias(rel_bias):
    bucket = jnp.asarray(_t5_bucket_table())
    q, k = bucket.shape
    return pl.pallas_call(
        _swa_bias_kernel,
        grid=(SWA_HEADS,),
        in_specs=[pl.BlockSpec(memory_space=pltpu.SMEM),
                  pl.BlockSpec((q, k), lambda h: (0, 0))],
        out_specs=pl.BlockSpec((1, q, k), lambda h: (h, 0, 0)),
        out_shape=jax.ShapeDtypeStruct((SWA_HEADS, q, k), F32),
        compiler_params=_params("arbitrary"),
        name="swa_bias",
    )(rel_bias, bucket)


def _swa_kernel(sink_ref, q_ref, kc_ref, kp_ref, vc_ref, vp_ref, bias_ref, o_ref, *, tq):
    i = pl.program_id(1)
    g = pl.program_id(2)
    blk = SWA_BLOCK
    heads = bias_ref.shape[0]
    lo_half = lax.broadcasted_iota(jnp.int32, (blk, LANES), 1) < HEAD_DIM
    in_prev = lax.broadcasted_iota(jnp.int32, (heads * blk, 2 * blk), 1) < blk
    bias = bias_ref[...].reshape(heads * blk, 2 * blk)

    def window(cur_ref, prev_ref, sb):
        if sb == 0:
            return jnp.concatenate([prev_ref[...], cur_ref[0:blk, :]], axis=0)
        return cur_ref[(sb - 1) * blk:(sb + 1) * blk, :]

    def scores(sb, _):
        q = q_ref[sb * blk:(sb + 1) * blk, :]
        parts = []
        for h in range(heads):
            qp = q[:, (h // 2) * LANES:(h // 2 + 1) * LANES]
            keep = lo_half if h % 2 == 0 else jnp.logical_not(lo_half)
            parts.append(jnp.where(keep, qp, jnp.zeros_like(qp)))
        s = _dot_nt(jnp.concatenate(parts, axis=0), window(kc_ref, kp_ref, sb)) + bias
        if sb == 0:
            s = jnp.where(jnp.logical_and(in_prev, i == 0), NEG, s)
        return s

    def softmax(sb, s):
        ps, inv = [], []
        for h in range(heads):
            sh = s[h * blk:(h + 1) * blk]
            sink = sink_ref[g * heads + h]
            m = jnp.maximum(jnp.max(sh, axis=1, keepdims=True), sink)
            e = jnp.exp2(sh - m)
            inv.append(1.0 / (jnp.sum(e, axis=1, keepdims=True) + jnp.exp2(sink - m)))
            ps.append(e.astype(BF16))
        return jnp.concatenate(ps, axis=0), inv

    def values(sb, val):
        p, inv = val
        o = _dot(p, window(vc_ref, vp_ref, sb))
        outs = [jnp.where(lo_half, o[h * blk:(h + 1) * blk] * inv[h], o[(h + 1) * blk:(h + 2) * blk] * inv[h + 1])
                for h in range(0, heads, 2)]
        o_ref[sb * blk:(sb + 1) * blk, :] = jnp.concatenate(outs, axis=1).astype(o_ref.dtype)

    _staggered(list(range(tq // blk)), (scores, softmax, values))


def _swa_attention(qkv, bias, sinks, batch, seq):
    t = qkv.shape[0]
    tq = _tile(seq, 512)
    nt = seq // tq
    group = SWA_HEADS // SWA_KV_HEADS
    gw = group * HEAD_DIM
    kcol = SWA_HEADS * HEAD_DIM // LANES
    vcol = kcol + SWA_KV_HEADS
    per = tq // SWA_BLOCK

    def prev_idx(b, i, g):
        return jnp.maximum(b * (seq // SWA_BLOCK) + i * per - 1, 0)

    return pl.pallas_call(
        functools.partial(_swa_kernel, tq=tq),
        grid=(batch, nt, SWA_KV_HEADS),
        in_specs=[pl.BlockSpec(memory_space=pltpu.SMEM),
                  pl.BlockSpec((tq, gw), lambda b, i, g: (b * nt + i, g)),
                  pl.BlockSpec((tq, LANES), lambda b, i, g: (b * nt + i, kcol + g)),
                  pl.BlockSpec((SWA_BLOCK, LANES), lambda b, i, g: (prev_idx(b, i, g), kcol + g)),
                  pl.BlockSpec((tq, LANES), lambda b, i, g: (b * nt + i, vcol + g)),
                  pl.BlockSpec((SWA_BLOCK, LANES), lambda b, i, g: (prev_idx(b, i, g), vcol + g)),
                  pl.BlockSpec((group, SWA_BLOCK, 2 * SWA_BLOCK), lambda b, i, g: (g, 0, 0))],
        out_specs=pl.BlockSpec((tq, gw), lambda b, i, g: (b * nt + i, g)),
        out_shape=jax.ShapeDtypeStruct((t, SWA_HEADS * HEAD_DIM), BF16),
        compiler_params=_params("parallel", "parallel", "arbitrary"),
        name="swa_attention",
    )(sinks, qkv, qkv, qkv, qkv, qkv, bias)


def _swa_weights(w_qkv, b_qkv):
    nq = SWA_HEADS * HEAD_DIM
    kv = SWA_KV_HEADS * HEAD_DIM
    dup = np.concatenate([np.tile(np.arange(HEAD_DIM), 2) + h * HEAD_DIM for h in range(SWA_KV_HEADS)])
    cols = np.concatenate([np.arange(nq), nq + dup, nq + kv + dup])
    scale = np.concatenate([np.full(nq, HEAD_DIM ** -0.5 * LOG2E), np.ones(2 * dup.size)]).astype(np.float32)
    return w_qkv[:, cols].astype(BF16), b_qkv[cols], jnp.asarray(scale)


def _staggered(units, stages):
    vals = list(units)
    for t in range(len(units) + len(stages) - 1):
        for k, stage in enumerate(stages):
            u = t - k
            if 0 <= u < len(units):
                vals[u] = stage(units[u], vals[u])


def _score_ahead(units, next_units, stages, s_ref):
    rest = stages[1:]
    vals = [None] * len(units)
    for t in range(len(units) + len(rest) - 1):
        ahead = stages[0](next_units[t], None) if next_units is not None and t < len(units) else None
        for k, stage in enumerate(rest):
            u = t - k
            if 0 <= u < len(units):
                if k == 0:
                    lanes = units[u][2]
                    vals[u] = stage(units[u], s_ref[u, :, :lanes.stop - lanes.start])
                    if ahead is not None:
                        s_ref[u] = ahead
                else:
                    vals[u] = stage(units[u], vals[u])


def _causal_sweep(i, tq, tk, stages, strict, s_ref, heads=2):
    nsub = tq // tk

    def group(j):
        return [(j * nsub + sb, h, slice(0, tq), None) for sb in reversed(range(nsub)) for h in range(heads)]

    def sweep(has_history):
        diagonal = []
        for sb in reversed(range(nsub)):
            nl = tq - sb * tk
            key = lax.broadcasted_iota(jnp.int32, (tk, nl), 0)
            qry = lax.broadcasted_iota(jnp.int32, (tk, nl), 1)
            mask = key < qry if strict else key <= qry
            diagonal += [(i * nsub + sb, h, slice(sb * tk, tq), mask) for h in range(heads)]
        for u, unit in enumerate(diagonal):
            s_ref[u, :, :tq - unit[2].start] = stages[0](unit, None)
        if not has_history:
            _score_ahead(diagonal, None, stages, s_ref)
            return
        _score_ahead(diagonal, group(i - 1), stages, s_ref)

        def body(n, carry):
            _score_ahead(group(i - 1 - n), group(i - 2 - n), stages, s_ref)
            return carry

        lax.fori_loop(0, i - 1, body, 0)
        _score_ahead(group(0), None, stages, s_ref)

    pl.when(i == 0)(lambda: sweep(False))
    pl.when(i > 0)(lambda: sweep(True))


def _transposed(x):
    return x.astype(F32).T.astype(x.dtype)


def _split_pair(q):
    lo_half = lax.broadcasted_iota(jnp.int32, q.shape, 1) < HEAD_DIM
    zero = jnp.zeros_like(q)
    return jnp.where(lo_half, q, zero), jnp.where(lo_half, zero, q)


SB_STEP_HEADS = 4


def _sb_kernel(q_ref, k_ref, vt_ref, u_ref, o_ref, acc_ref, c_ref, s_ref, *, tq, tk, heads):
    i = pl.program_id(2)
    qts = [_transposed(qm) for pair in range(heads // 2)
           for qm in _split_pair(q_ref[:, pair * LANES:(pair + 1) * LANES])]
    u = u_ref[...]
    acc_ref[...] = jnp.zeros_like(acc_ref)
    c_ref[...] = jnp.zeros_like(c_ref)

    def scores(unit, _):
        jb, h, lanes, _ = unit
        kt = k_ref[pl.ds(pl.multiple_of(jb * tk, tk), tk), (h // 2) * LANES:(h // 2 + 1) * LANES]
        return _dot(kt, qts[h][:, lanes])

    def log_fail(unit, z):
        neg_abs = lax.bitcast_convert_type(lax.bitcast_convert_type(z, jnp.uint32) | jnp.uint32(1 << 31), F32)
        sp = jnp.maximum(z, 0.0) + jnp.log2(1.0 + jnp.exp2(neg_abs))
        if unit[3] is not None:
            sp = jnp.where(unit[3], sp, 0.0)
        return z, sp.astype(BF16)

    def suffix_sum(unit, val):
        z, sp = val
        return z, _dot(u, sp)

    def weights(unit, val):
        _, h, lanes, mask = unit
        z, r = val
        c = c_ref[h:h + 1, lanes]
        la = z - r - c
        if mask is not None:
            la = jnp.where(mask, la, NEG)
        c_ref[h:h + 1, lanes] = c + r[0:1, :]
        return jnp.exp2(la).astype(BF16)

    def values(unit, a):
        jb, h, lanes, _ = unit
        rows = slice(h * HEAD_DIM, (h + 1) * HEAD_DIM)
        acc_ref[rows, lanes] += _dot(vt_ref[jb, rows, :], a)

    _causal_sweep(i, tq, tk, (scores, log_fail, suffix_sum, weights, values), True, s_ref, heads)
    o_ref[...] = acc_ref[...].T.astype(o_ref.dtype)


def _sb_attention(qk, vt, batch, seq):
    t = qk.shape[0]
    tq = _tile(seq, ATT_TQ)
    tk = vt.shape[2]
    nq = seq // tq
    groups = SB_HEADS // SB_STEP_HEADS
    gw = SB_STEP_HEADS * HEAD_DIM
    u = jnp.asarray(np.triu(np.ones((tk, tk), np.float32)), BF16)
    return pl.pallas_call(
        functools.partial(_sb_kernel, tq=tq, tk=tk, heads=SB_STEP_HEADS),
        grid=(batch, groups, nq),
        in_specs=[pl.BlockSpec((tq, gw), lambda b, p, i: (b * nq + i, p)),
                  pl.BlockSpec((seq, gw), lambda b, p, i: (b, groups + p)),
                  pl.BlockSpec((seq // tk, gw, tk), lambda b, p, i: (b, p, 0)),
                  pl.BlockSpec((tk, tk), lambda b, p, i: (0, 0))],
        out_specs=pl.BlockSpec((tq, gw), lambda b, p, i: (b * nq + i, p)),
        out_shape=jax.ShapeDtypeStruct((t, SB_HEADS * HEAD_DIM), BF16),
        scratch_shapes=[pltpu.VMEM((gw, tq), F32), pltpu.VMEM((8, tq), F32),
                        pltpu.VMEM((SB_STEP_HEADS * (tq // tk), tk, tq), F32)],
        compiler_params=_params("parallel", "parallel", "arbitrary"),
        name="sb_attention",
    )(qk, qk, vt, u)


MLA_QN = MLA_HEADS * MLA_NOPE_DIM
MLA_QR = MLA_HEADS * MLA_ROPE_DIM
MLA_OUT = 2 * MLA_QN + MLA_QR + LANES


def _mla_proj_kernel(x_ref, g_ref, wd_ref, qg_ref, wuq_ref, kvg_ref, wukv_ref, cs_ref, o_ref, vt_ref,
                     *, scale, tk):
    h = _rms(x_ref[...], g_ref[...]).astype(BF16)
    c = _dot(h, wd_ref[...])
    cos = cs_ref[:, :LANES]
    sin = cs_ref[:, LANES:]
    cq = _rms(c[:, :MLA_Q_RANK], qg_ref[...]).astype(BF16)
    q = _dot(cq, wuq_ref[...])
    o_ref[:, :MLA_QN] = (q[:, :MLA_QN] * scale).astype(o_ref.dtype)
    for m in range(MLA_QR // LANES):
        a = MLA_QN + m * LANES
        rot = q[:, a:a + LANES] * cos + q[:, a + MLA_QR:a + MLA_QR + LANES] * sin
        o_ref[:, a:a + LANES] = (rot * scale).astype(o_ref.dtype)
    kv0 = MLA_Q_RANK + MLA_KV_RANK
    ckv = _rms(c[:, MLA_Q_RANK:kv0], kvg_ref[...]).astype(BF16)
    kv = _dot(ckv, wukv_ref[...])
    kn0 = MLA_QN + MLA_QR
    o_ref[:, kn0:kn0 + MLA_QN] = kv[:, :MLA_QN].astype(o_ref.dtype)
    kr = c[:, kv0:kv0 + LANES] * cos + c[:, kv0 + LANES:kv0 + 2 * LANES] * sin
    o_ref[:, MLA_OUT - LANES:] = kr.astype(o_ref.dtype)
    _store_vt(vt_ref, kv[:, MLA_QN:], tk)


def _mla_rope_layout():
    half = MLA_ROPE_DIM // 2
    per = LANES // MLA_ROPE_DIM
    dq = MLA_NOPE_DIM + MLA_ROPE_DIM
    q_nope = np.concatenate([h * dq + np.arange(MLA_NOPE_DIM) for h in range(MLA_HEADS)])
    q_rope = np.zeros(MLA_QR, np.int64)
    q_swap = np.zeros(MLA_QR, np.int64)
    for h in range(MLA_HEADS):
        m, r = divmod(h, per)
        x1 = h * dq + MLA_NOPE_DIM + np.arange(half)
        x2 = x1 + half
        first = m * LANES + r * half + np.arange(half)
        second = first + LANES // 2
        q_rope[first], q_rope[second] = x1, x2
        q_swap[first], q_swap[second] = x2, x1
    k_rope = np.zeros(LANES, np.int64)
    k_swap = np.zeros(LANES, np.int64)
    base = MLA_Q_RANK + MLA_KV_RANK
    for r in range(per):
        first = r * half + np.arange(half)
        second = first + LANES // 2
        k_rope[first], k_rope[second] = base + np.arange(half), base + half + np.arange(half)
        k_swap[first], k_swap[second] = base + half + np.arange(half), base + np.arange(half)
    dkv = MLA_NOPE_DIM + HEAD_DIM
    k_nope = np.concatenate([h * dkv + np.arange(MLA_NOPE_DIM) for h in range(MLA_HEADS)])
    v = k_nope + MLA_NOPE_DIM
    down = np.concatenate([np.arange(base), k_rope, k_swap])
    return np.concatenate([q_nope, q_rope, q_swap]), np.concatenate([k_nope, v]), down


def _mla_rope_tables(seq):
    half = MLA_ROPE_DIM // 2
    inv = ROPE_THETA ** (-jnp.arange(0, MLA_ROPE_DIM, 2, dtype=F32) / MLA_ROPE_DIM)
    ang = jnp.arange(seq, dtype=F32)[:, None] * inv[None, :]
    cos, sin = jnp.cos(ang), jnp.sin(ang)
    reps = LANES // 2 // half
    cos_t = jnp.tile(cos, (1, 2 * reps))
    sin_t = jnp.concatenate([jnp.tile(-sin, (1, reps)), jnp.tile(sin, (1, reps))], axis=1)
    return jnp.concatenate([cos_t, sin_t], axis=1)


def _mla_proj(x, g, w_down, q_norm, w_uq, kv_norm, w_ukv, seq):
    t, d = x.shape
    tm = _tile(seq, 512)
    tk = _tile(seq, ATT_TK)
    ns = seq // tm
    uq_cols, ukv_cols, down_cols = _mla_rope_layout()
    wd = w_down[:, down_cols].astype(BF16)
    wuq = w_uq[:, uq_cols].astype(BF16)
    wukv = w_ukv[:, ukv_cols].astype(BF16)
    cs = _mla_rope_tables(seq)
    scale = (MLA_NOPE_DIM + MLA_ROPE_DIM) ** -0.5 * LOG2E
    full = lambda a: pl.BlockSpec(a.shape, lambda i: (0, 0))
    g2, qg, kvg = g.reshape(1, d), q_norm.reshape(1, -1), kv_norm.reshape(1, -1)
    return pl.pallas_call(
        functools.partial(_mla_proj_kernel, scale=scale, tk=tk),
        grid=(t // tm,),
        in_specs=[pl.BlockSpec((tm, d), lambda i: (i, 0)), full(g2), full(wd), full(qg), full(wuq),
                  full(kvg), full(wukv), pl.BlockSpec((tm, 2 * LANES), lambda i: (i % ns, 0))],
        out_specs=[pl.BlockSpec((tm, MLA_OUT), lambda i: (i, 0)),
                   pl.BlockSpec((tm // tk, MLA_QN, tk), lambda i: (i, 0, 0))],
        out_shape=[jax.ShapeDtypeStruct((t, MLA_OUT), BF16),
                   jax.ShapeDtypeStruct((t // tk, MLA_QN, tk), BF16)],
        compiler_params=_params("parallel"),
        name="mla_proj",
    )(x, g2, wd, qg, wuq, kvg, wukv, cs)


MLA_STEP_HEADS = LANES // MLA_ROPE_DIM


def _mla_kernel(qn_ref, qr_ref, kn_ref, kr_ref, vt_ref, o_ref, acc_ref, ml_ref, s_ref, *, tq, tk):
    i = pl.program_id(2)
    heads = MLA_STEP_HEADS
    lane = lax.broadcasted_iota(jnp.int32, (tq, 2 * LANES), 1)
    half = MLA_ROPE_DIM // 2
    qts = []
    for h in range(heads):
        pair, j = divmod(h, 2)
        q = jnp.concatenate([qn_ref[:, pair * LANES:(pair + 1) * LANES], qr_ref[...]], axis=1)
        r0 = LANES + h * half
        sel = jnp.logical_and(lane >= j * HEAD_DIM, lane < (j + 1) * HEAD_DIM)
        sel = jnp.logical_or(sel, jnp.logical_and(lane >= r0, lane < r0 + half))
        r1 = r0 + LANES // 2
        sel = jnp.logical_or(sel, jnp.logical_and(lane >= r1, lane < r1 + half))
        qts.append(_transposed(jnp.where(sel, q, jnp.zeros_like(q))))
    acc_ref[...] = jnp.zeros_like(acc_ref)
    row = lax.broadcasted_iota(jnp.int32, ml_ref.shape, 0)
    ml_ref[...] = jnp.where(row < heads, NEG, 0.0)

    def scores(unit, _):
        jb, h, lanes, _ = unit
        start = pl.multiple_of(jb * tk, tk)
        pair = h // 2
        kt = jnp.concatenate([kn_ref[pl.ds(start, tk), pair * LANES:(pair + 1) * LANES],
                              kr_ref[pl.ds(start, tk), :]], axis=1)
        return _dot(kt, qts[h][:, lanes])

    def softmax(unit, s):
        _, h, lanes, mask = unit
        if mask is not None:
            s = jnp.where(mask, s, NEG)
        m_old = ml_ref[h:h + 1, lanes]
        m_new = jnp.maximum(m_old, jnp.max(s, axis=0, keepdims=True))
        alpha = jnp.exp2(m_old - m_new)
        e = jnp.exp2(s - m_new)
        ml_ref[h:h + 1, lanes] = m_new
        l_row = slice(heads + h, heads + h + 1)
        ml_ref[l_row, lanes] = alpha * ml_ref[l_row, lanes] + jnp.sum(e, axis=0, keepdims=True)
        return alpha, e.astype(BF16)

    def values(unit, val):
        jb, h, lanes, _ = unit
        alpha, e = val
        rows = slice(h * HEAD_DIM, (h + 1) * HEAD_DIM)
        acc_ref[rows, lanes] = alpha * acc_ref[rows, lanes] + _dot(vt_ref[jb, rows, :], e)

    _causal_sweep(i, tq, tk, (scores, softmax, values), False, s_ref, heads)
    head = lax.broadcasted_iota(jnp.int32, acc_ref.shape, 0) // HEAD_DIM
    denom = ml_ref[heads:heads + 1, :]
    for h in range(1, heads):
        denom = jnp.where(head == h, ml_ref[heads + h:heads + h + 1, :], denom)
    o_ref[...] = (acc_ref[...] / denom).T.astype(o_ref.dtype)


def _mla_attention(a, vt, batch, seq):
    t = a.shape[0]
    tq = _tile(seq, ATT_TQ)
    tk = vt.shape[2]
    nq = seq // tq
    groups = MLA_HEADS // MLA_STEP_HEADS
    gw = MLA_STEP_HEADS * HEAD_DIM
    qr0 = MLA_QN // LANES
    kn0 = (MLA_QN + MLA_QR) // gw
    kr0 = (MLA_QN + MLA_QR + MLA_QN) // LANES
    return pl.pallas_call(
        functools.partial(_mla_kernel, tq=tq, tk=tk),
        grid=(batch, groups, nq),
        in_specs=[pl.BlockSpec((tq, gw), lambda b, p, i: (b * nq + i, p)),
                  pl.BlockSpec((tq, LANES), lambda b, p, i: (b * nq + i, qr0 + p)),
                  pl.BlockSpec((seq, gw), lambda b, p, i: (b, kn0 + p)),
                  pl.BlockSpec((seq, LANES), lambda b, p, i: (b, kr0)),
                  pl.BlockSpec((seq // tk, gw, tk), lambda b, p, i: (b, p, 0))],
        out_specs=pl.BlockSpec((tq, gw), lambda b, p, i: (b * nq + i, p)),
        out_shape=jax.ShapeDtypeStruct((t, MLA_HEADS * HEAD_DIM), BF16),
        scratch_shapes=[pltpu.VMEM((gw, tq), F32), pltpu.VMEM((2 * MLA_STEP_HEADS, tq), F32),
                        pltpu.VMEM((MLA_STEP_HEADS * (tq // tk), tk, tq), F32)],
        compiler_params=_params("parallel", "parallel", "arbitrary"),
        name="mla_attention",
    )(a, a, a, a, vt)


FFN_CHUNK = 256


def _swiglu_into(acc_ref, h, wg, wu, wd):
    def gate_up(cols, _):
        return _dot(h, wg(cols)), _dot(h, wu(cols))

    def activate(cols, gu):
        g, u = gu
        return (g * jax.nn.sigmoid(g) * u).astype(BF16)

    def down(cols, a):
        acc_ref[...] += _dot(a, wd(cols))

    return gate_up, activate, down


def _run_swiglu(acc_ref, h, wg, wu, wd, width):
    chunks = [slice(c, min(c + FFN_CHUNK, width)) for c in range(0, width, FFN_CHUNK)]
    _staggered(chunks, _swiglu_into(acc_ref, h, wg, wu, wd))


def _ffn_kernel(x_ref, a_ref, wo_ref, g_ref, wg_ref, wu_ref, wd_ref, o_ref, h_ref):
    f = pl.program_id(1)

    @pl.when(f == 0)
    def _():
        x = x_ref[...] + _dot(a_ref[...], wo_ref[...])
        h_ref[...] = _rms(x, g_ref[...]).astype(BF16)
        o_ref[...] = x

    _run_swiglu(o_ref, h_ref[...], lambda c: wg_ref[:, c], lambda c: wu_ref[:, c], lambda c: wd_ref[c, :],
                wg_ref.shape[1])


def _ffn(x, a, wo, g, wg, wu, wd):
    t, d = x.shape
    fdim = wg.shape[1]
    tm = _tile(t, 512)
    tf = fdim
    return pl.pallas_call(
        _ffn_kernel,
        grid=(t // tm, fdim // tf),
        in_specs=[pl.BlockSpec((tm, d), lambda i, f: (i, 0)),
                  pl.BlockSpec((tm, a.shape[1]), lambda i, f: (i, 0)),
                  pl.BlockSpec(wo.shape, lambda i, f: (0, 0)),
                  pl.BlockSpec((1, d), lambda i, f: (0, 0)),
                  pl.BlockSpec((d, tf), lambda i, f: (0, f)),
                  pl.BlockSpec((d, tf), lambda i, f: (0, f)),
                  pl.BlockSpec((tf, d), lambda i, f: (f, 0))],
        out_specs=pl.BlockSpec((tm, d), lambda i, f: (i, 0)),
        out_shape=jax.ShapeDtypeStruct((t, d), F32),
        scratch_shapes=[pltpu.VMEM((tm, d), BF16)],
        compiler_params=_params("parallel", "arbitrary"),
        name="dense_ffn",
    )(x, a, wo, g.reshape(1, d), wg, wu, wd)


MOE_TILE = 512
META_ROWS = 8
ROUTE_SUB = 128
COMBINE_SUB = 256


def _pack_halves(x):
    half = x.shape[1] // 2
    bits = lax.bitcast_convert_type(x.astype(BF16).astype(F32), jnp.uint32)
    return bits[:, :half] | (bits[:, half:] >> 16)


def _unpack_halves(w):
    hi = lax.bitcast_convert_type(w & jnp.uint32(0xFFFF0000), F32)
    lo = lax.bitcast_convert_type(w << 16, F32)
    return jnp.concatenate([hi, lo], axis=1)


def _router_kernel(x_ref, g_ref, rt_ref, su_ref, eye_ref, hb_ref, meta_ref, metac_ref, cnt_ref, carry_ref):
    @pl.when(pl.program_id(0) == 0)
    def _():
        carry_ref[...] = jnp.zeros_like(carry_ref)

    h = _rms(x_ref[...], g_ref[...])
    hb_ref[...] = _pack_halves(h)
    ne = rt_ref.shape[0]
    r = rt_ref[...]
    r_hi = r.astype(BF16).astype(F32)
    h_hi = h.astype(BF16)
    h_lo = (h - h_hi.astype(F32)).astype(BF16)
    both_r = _dot_nt(jnp.concatenate([r_hi, r - r_hi], axis=0).astype(BF16), h_hi)
    logits = both_r[:ne] + both_r[ne:] + _dot_nt(r_hi.astype(BF16), h_lo)
    ne, ct = logits.shape
    eio = lax.broadcasted_iota(jnp.int32, (ne, ct), 0).astype(F32)
    m1 = jnp.max(logits, axis=0, keepdims=True)
    i1 = jnp.min(jnp.where(logits == m1, eio, float(ne)), axis=0, keepdims=True)
    rest = jnp.where(eio == i1, -jnp.inf, logits)
    m2 = jnp.max(rest, axis=0, keepdims=True)
    i2 = jnp.min(jnp.where(rest == m2, eio, float(ne)), axis=0, keepdims=True)
    e2 = jnp.exp(m2 - m1)
    g1 = 1.0 / (1.0 + e2)
    g2 = e2 / (1.0 + e2)
    oh1 = (eio == i1).astype(F32)
    oh2 = (eio == i2).astype(F32)
    both = oh1 + oh2
    seen = carry_ref[:, 0:1] + _dot(both.astype(BF16), su_ref[...])
    r1 = jnp.sum(oh1 * seen, axis=0, keepdims=True)
    r2 = jnp.sum(oh2 * seen, axis=0, keepdims=True)
    meta = jnp.zeros((META_ROWS, ct), F32)
    for row, val in enumerate((i1, i2, r1, r2, g1, g2)):
        meta = jnp.where(eio == float(row), val, meta)
    meta_ref[...] = meta
    metac = jnp.zeros((ct, META_ROWS), F32)
    rest_bits = meta
    for _ in range(3):
        piece = rest_bits.astype(BF16)
        metac = metac + _dot_nt(eye_ref[...], piece)
        rest_bits = rest_bits - piece.astype(F32)
    metac_ref[...] = metac
    carry_ref[...] = carry_ref[...] + jnp.sum(both, axis=1, keepdims=True)
    cnt_ref[0] = carry_ref[...]


def _router(x, g, router):
    t, d = x.shape
    ct = _tile(t, MOE_TILE)
    nc = t // ct
    su = jnp.asarray(np.triu(np.ones((ct, ct), np.float32), 1), BF16)
    eye = jnp.eye(ct, dtype=BF16)
    return pl.pallas_call(
        _router_kernel,
        grid=(nc,),
        in_specs=[pl.BlockSpec((ct, d), lambda i: (i, 0)),
                  pl.BlockSpec((1, d), lambda i: (0, 0)),
                  pl.BlockSpec((N_EXPERTS, d), lambda i: (0, 0)),
                  pl.BlockSpec((ct, ct), lambda i: (0, 0)),
                  pl.BlockSpec((ct, ct), lambda i: (0, 0))],
        out_specs=[pl.BlockSpec((ct, d // 2), lambda i: (i, 0)),
                   pl.BlockSpec((META_ROWS, ct), lambda i: (0, i)),
                   pl.BlockSpec((ct, META_ROWS), lambda i: (i, 0)),
                   pl.BlockSpec((1, N_EXPERTS, LANES), lambda i: (i, 0, 0))],
        out_shape=[jax.ShapeDtypeStruct((t, d // 2), jnp.uint32),
                   jax.ShapeDtypeStruct((META_ROWS, t), F32),
                   jax.ShapeDtypeStruct((t, META_ROWS), F32),
                   jax.ShapeDtypeStruct((nc, N_EXPERTS, LANES), F32)],
        scratch_shapes=[pltpu.VMEM((N_EXPERTS, LANES), F32)],
        compiler_params=_params("arbitrary"),
        name="moe_router",
    )(x, g.reshape(1, d), router.T, su, eye)


def _group_layout(cnt_after, t, rt):
    rows = 2 * t + N_EXPERTS * rt
    counts = cnt_after[:, :, 0].astype(jnp.int32)
    padded = (counts[-1] + rt - 1) // rt * rt
    off_end = jnp.cumsum(padded)
    off = (off_end - padded).astype(jnp.int32)
    tile_start = jnp.arange(rows // rt, dtype=jnp.int32) * rt
    tile_expert = jnp.minimum(_count_le(off_end, tile_start), N_EXPERTS - 1).astype(jnp.int32)
    tile_valid = (tile_start < off_end[-1]).astype(jnp.int32)
    return rows, off, counts, tile_expert, tile_valid


def _count_le(sorted_vals, x):
    return jnp.sum((sorted_vals[None, :] <= x[:, None]).astype(jnp.int32), axis=1)


def _work_lists(counts, off, rows, t, ct):
    nc = t // ct
    nrt = rows // ct
    before = jnp.concatenate([jnp.zeros((1, N_EXPERTS), jnp.int32), counts[:-1]], axis=0)
    chunk_start = (off[:, None] + before.T).reshape(-1)
    tile_start = jnp.arange(nrt, dtype=jnp.int32) * ct
    starts = jnp.sort(jnp.concatenate([chunk_start, tile_start]))
    ends = jnp.concatenate([starts[1:], jnp.array([rows], jnp.int32)])
    item_r = jnp.minimum(starts // ct, nrt - 1)
    item_c = (_count_le(chunk_start, starts) - 1) % nc
    first = lambda key: jnp.concatenate([jnp.ones((1,), jnp.int32), (key[1:] != key[:-1]).astype(jnp.int32)])
    dispatch = (item_r, item_c, starts, ends, first(item_r))
    order = jnp.argsort(item_c * (rows + 1) + starts)
    c2 = item_c[order]
    combine = (c2, item_r[order], starts[order], ends[order], first(c2))
    return dispatch, combine


def _positions(expert, rank, off_ref):
    base = jnp.zeros_like(rank)
    for e in range(N_EXPERTS):
        base = jnp.where(expert == float(e), off_ref[e].astype(F32), base)
    return base + rank


def _dispatch_kernel(r_ref, c_ref, s_ref, e_ref, first_ref, off_ref, meta_ref, h_ref, xs_ref, gate_ref):
    n = pl.program_id(0)
    rt, ct = xs_ref.shape[0], h_ref.shape[0]

    @pl.when(first_ref[n] == 1)
    def _():
        xs_ref[...] = jnp.zeros_like(xs_ref)
        gate_ref[...] = jnp.zeros_like(gate_ref)

    meta = meta_ref[...]
    pos0 = _positions(meta[0:1], meta[2:3], off_ref)
    pos1 = _positions(meta[1:2], meta[3:4], off_ref)
    sub = min(rt, ROUTE_SUB)
    for sb in range(rt // sub):
        base = r_ref[n] * rt + sb * sub

        @pl.when(jnp.logical_and(base < e_ref[n], base + sub > s_ref[n]))
        def _():
            row = lax.broadcasted_iota(jnp.int32, (sub, ct), 0) + base
            live = jnp.logical_and(row >= s_ref[n], row < e_ref[n])
            rowf = row.astype(F32)
            sel0 = jnp.logical_and(pos0 == rowf, live)
            sel1 = jnp.logical_and(pos1 == rowf, live)
            sel = jnp.logical_or(sel0, sel1).astype(F32).astype(BF16)
            rows = slice(sb * sub, (sb + 1) * sub)
            xs_ref[rows, :] += _pack_halves(_dot(sel, _unpack_halves(h_ref[...]).astype(BF16)))
            gate = jnp.where(sel0, meta[4:5], 0.0) + jnp.where(sel1, meta[5:6], 0.0)
            gate_ref[rows, :] += jnp.sum(gate, axis=1, keepdims=True)


def _dispatch(plan, off, meta, hb, rows, ct):
    t, d = hb.shape
    n_items = plan[0].shape[0]
    return pl.pallas_call(
        _dispatch_kernel,
        grid_spec=pltpu.PrefetchScalarGridSpec(
            num_scalar_prefetch=6,
            grid=(n_items,),
            in_specs=[pl.BlockSpec((META_ROWS, ct), lambda n, r, c, *_: (0, c[n])),
                      pl.BlockSpec((ct, d), lambda n, r, c, *_: (c[n], 0))],
            out_specs=[pl.BlockSpec((ct, d), lambda n, r, c, *_: (r[n], 0)),
                       pl.BlockSpec((ct, 1), lambda n, r, c, *_: (r[n], 0))]),
        out_shape=[jax.ShapeDtypeStruct((rows, d), jnp.uint32),
                   jax.ShapeDtypeStruct((rows, 1), F32)],
        compiler_params=_params("arbitrary"),
        name="moe_dispatch",
    )(*plan, off, meta, hb)


def _expert_kernel(te_ref, tv_ref, xs_ref, gate_ref, wg_ref, wu_ref, wd_ref, y_ref, acc_ref):
    r = pl.program_id(0)
    f = pl.program_id(1)

    @pl.when(f == 0)
    def _():
        acc_ref[...] = jnp.zeros_like(acc_ref)

    @pl.when(tv_ref[r] == 1)
    def _():
        _run_swiglu(acc_ref, _unpack_halves(xs_ref[...]).astype(BF16), lambda c: wg_ref[0, :, c],
                    lambda c: wu_ref[0, :, c], lambda c: wd_ref[0, c, :], wg_ref.shape[2])

    @pl.when(f == pl.num_programs(1) - 1)
    def _():
        y_ref[...] = _pack_halves(acc_ref[...] * gate_ref[...])


def _experts(tile_expert, tile_valid, xs, gate, wg, wu, wd, ct):
    rows, dp = xs.shape
    d = 2 * dp
    fdim = wg.shape[2]
    tf = _tile(fdim, 1792)
    nf = fdim // tf

    def wcol(r, f, te, tv):
        return (te[r], 0, jnp.where(tv[r] == 1, f, nf - 1))

    def wrow(r, f, te, tv):
        return (te[r], jnp.where(tv[r] == 1, f, nf - 1), 0)

    return pl.pallas_call(
        _expert_kernel,
        grid_spec=pltpu.PrefetchScalarGridSpec(
            num_scalar_prefetch=2,
            grid=(rows // ct, nf),
            in_specs=[pl.BlockSpec((ct, dp), lambda r, f, te, tv: (r, 0)),
                      pl.BlockSpec((ct, 1), lambda r, f, te, tv: (r, 0)),
                      pl.BlockSpec((1, d, tf), wcol),
                      pl.BlockSpec((1, d, tf), wcol),
                      pl.BlockSpec((1, tf, d), wrow)],
            out_specs=pl.BlockSpec((ct, dp), lambda r, f, te, tv: (r, 0)),
            scratch_shapes=[pltpu.VMEM((ct, d), F32)]),
        out_shape=jax.ShapeDtypeStruct((rows, dp), jnp.uint32),
        compiler_params=_params("arbitrary", "arbitrary"),
        name="moe_experts",
    )(tile_expert, tile_valid, xs, gate, wg, wu, wd)


def _combine_kernel(c_ref, r_ref, s_ref, e_ref, first_ref, off_ref, x_ref, metac_ref, y_ref, o_ref):
    n = pl.program_id(0)
    ct, rt = x_ref.shape[0], y_ref.shape[0]

    @pl.when(first_ref[n] == 1)
    def _():
        o_ref[...] = x_ref[...]

    metac = metac_ref[...]
    pos0 = _positions(metac[:, 0:1], metac[:, 2:3], off_ref)
    pos1 = _positions(metac[:, 1:2], metac[:, 3:4], off_ref)
    sub = min(rt, COMBINE_SUB)
    for sb in range(rt // sub):
        base = r_ref[n] * rt + sb * sub

        @pl.when(jnp.logical_and(base < e_ref[n], base + sub > s_ref[n]))
        def _():
            row = lax.broadcasted_iota(jnp.int32, (ct, sub), 1) + base
            live = jnp.logical_and(row >= s_ref[n], row < e_ref[n])
            rowf = row.astype(F32)
            sel = jnp.logical_and(jnp.logical_or(pos0 == rowf, pos1 == rowf), live).astype(F32).astype(BF16)
            o_ref[...] += _dot(sel, _unpack_halves(y_ref[sb * sub:(sb + 1) * sub, :]).astype(BF16))


def _combine(plan, off, x, metac, y, ct):
    t, d = x.shape
    n_items = plan[0].shape[0]
    return pl.pallas_call(
        _combine_kernel,
        grid_spec=pltpu.PrefetchScalarGridSpec(
            num_scalar_prefetch=6,
            grid=(n_items,),
            in_specs=[pl.BlockSpec((ct, d), lambda n, c, r, *_: (c[n], 0)),
                      pl.BlockSpec((ct, META_ROWS), lambda n, c, r, *_: (c[n], 0)),
                      pl.BlockSpec((ct, d // 2), lambda n, c, r, *_: (r[n], 0))],
            out_specs=pl.BlockSpec((ct, d), lambda n, c, r, *_: (c[n], 0))),
        out_shape=jax.ShapeDtypeStruct((t, d), F32),
        compiler_params=_params("arbitrary"),
        name="moe_combine",
    )(*plan, off, x, metac, y)


def _cast_kernel(w_ref, o_ref):
    o_ref[...] = w_ref[...].astype(o_ref.dtype)


CAST_BLOCK_BYTES = 8 * 1024 * 1024


def _layer_bf16(w, layer):
    shape = w.shape[1:]
    cols = shape[-1]
    rows = math.prod(shape[:-1])
    tr = rows
    while tr * cols * 4 > CAST_BLOCK_BYTES and tr % 16 == 0:
        tr //= 2
    out = pl.pallas_call(
        _cast_kernel,
        grid=(rows // tr,),
        in_specs=[pl.BlockSpec((None, tr, cols), lambda i: (layer, i, 0))],
        out_specs=pl.BlockSpec((tr, cols), lambda i: (i, 0)),
        out_shape=jax.ShapeDtypeStruct((rows, cols), BF16),
        compiler_params=_params("parallel"),
        name="weights_bf16",
    )(w.reshape(w.shape[0], rows, cols))
    return out.reshape(shape)


SC_CORES = 2
SC_SUBCORES = 16
SC_CHUNK = 128


def _sc_worker_chunks(total):
    per = total // (SC_CORES * SC_SUBCORES)
    wid = lax.axis_index("s") * SC_CORES + lax.axis_index("c")
    return wid * per, per // SC_CHUNK


def _sc_mesh():
    return plsc.VectorSubcoreMesh(core_axis_name="c", subcore_axis_name="s")


def _sc_gather(table, idx):
    v, d = table.shape
    b = idx.shape[0]

    def body(table_hbm, idx_hbm, out_hbm, idx_v, rows_v, sem):
        base, chunks = _sc_worker_chunks(b)

        @pl.loop(0, chunks)
        def _(j):
            start = base + j * SC_CHUNK
            pltpu.sync_copy(idx_hbm.at[pl.ds(start, SC_CHUNK)], idx_v)
            pltpu.async_copy(table_hbm.at[idx_v], rows_v, sem).wait()
            pltpu.sync_copy(rows_v, out_hbm.at[pl.ds(start, SC_CHUNK)])

    return pl.kernel(
        body,
        out_type=jax.ShapeDtypeStruct((b, d), table.dtype),
        mesh=_sc_mesh(),
        scratch_types=[pltpu.VMEM((SC_CHUNK,), jnp.int32),
                       pltpu.VMEM((SC_CHUNK, d), table.dtype),
                       pltpu.SemaphoreType.DMA],
        name="sc_row_gather",
    )(table, idx)


def _sc_scatter_rows(src, idx, n):
    t, d = src.shape
    b = idx.shape[0]

    def body(src_hbm, idx_hbm, out_hbm, idx_v, rows_v):
        base, chunks = _sc_worker_chunks(b)

        @pl.loop(0, chunks)
        def _(j):
            start = base + j * SC_CHUNK
            first = lax.rem(start, t)
            pltpu.sync_copy(idx_hbm.at[pl.ds(start, SC_CHUNK)], idx_v)
            pltpu.sync_copy(src_hbm.at[pl.ds(first, SC_CHUNK)], rows_v)
            pltpu.sync_copy(rows_v, out_hbm.at[idx_v])

    return pl.kernel(
        body,
        out_type=jax.ShapeDtypeStruct((n, d), src.dtype),
        mesh=_sc_mesh(),
        scratch_types=[pltpu.VMEM((SC_CHUNK,), jnp.int32), pltpu.VMEM((SC_CHUNK, d), src.dtype)],
        name="sc_row_scatter",
    )(src, idx)


def _weighted_add_kernel(x_ref, metac_ref, y0_ref, y1_ref, g_ref, o_ref, *, normed):
    gates = metac_ref[...]
    out = (x_ref[...] + gates[:, 4:5] * _unpack_halves(y0_ref[...])
           + gates[:, 5:6] * _unpack_halves(y1_ref[...]))
    o_ref[...] = _rms(out, g_ref[...]) if normed else out


def _weighted_add(x, metac, ysel, out_gain=None):
    t, d = x.shape
    tm = _tile(t, 512)
    nt = t // tm
    gain = jnp.ones((d,), F32) if out_gain is None else out_gain
    return pl.pallas_call(
        functools.partial(_weighted_add_kernel, normed=out_gain is not None),
        grid=(nt,),
        in_specs=[pl.BlockSpec((tm, d), lambda i: (i, 0)),
                  pl.BlockSpec((tm, META_ROWS), lambda i: (i, 0)),
                  pl.BlockSpec((tm, d // 2), lambda i: (i, 0)),
                  pl.BlockSpec((tm, d // 2), lambda i: (nt + i, 0)),
                  pl.BlockSpec((1, d), lambda i: (0, 0))],
        out_specs=pl.BlockSpec((tm, d), lambda i: (i, 0)),
        out_shape=jax.ShapeDtypeStruct((t, d), F32),
        compiler_params=_params("parallel"),
        name="moe_weighted_add",
    )(x, metac, ysel, ysel, gain.reshape(1, d))


def _moe(x, g, router, wg, wu, wd, out_gain=None):
    t = x.shape[0]
    ct = _tile(t, MOE_TILE)
    hb, meta, metac, cnt = _router(x, g, router)
    sc_rows = SC_CORES * SC_SUBCORES * SC_CHUNK
    rows, off, counts, tile_expert, tile_valid = _group_layout(cnt, t, ct)
    if (2 * t) % sc_rows == 0 and rows % sc_rows == 0:
        base = sum(jnp.where(meta[0:2] == float(e), off[e], 0) for e in range(N_EXPERTS))
        pos = (base + meta[2:4].astype(jnp.int32)).reshape(-1)
        xs = _sc_scatter_rows(hb, pos, rows)
        y = _experts(tile_expert, tile_valid, xs, jnp.ones((rows, 1), F32), wg, wu, wd, ct)
        return _weighted_add(x, metac, _sc_gather(y, pos), out_gain)
    dispatch, combine = _work_lists(counts, off, rows, t, ct)
    xs, gate = _dispatch(dispatch, off, meta, hb, rows, ct)
    y = _experts(tile_expert, tile_valid, xs, gate, wg, wu, wd, ct)
    out = _combine(combine, off, x, metac, y, ct)
    return out if out_gain is None else _final_norm(out, out_gain)


def _final_norm_kernel(x_ref, g_ref, o_ref):
    o_ref[...] = _rms(x_ref[...], g_ref[...])


def _final_norm(x, g):
    t, d = x.shape
    tm = _tile(t, 1024)
    return pl.pallas_call(
        _final_norm_kernel,
        grid=(t // tm,),
        in_specs=[pl.BlockSpec((tm, d), lambda i: (i, 0)), pl.BlockSpec((1, d), lambda i: (0, 0))],
        out_specs=pl.BlockSpec((tm, d), lambda i: (i, 0)),
        out_shape=jax.ShapeDtypeStruct((t, d), F32),
        compiler_params=_params("parallel"),
        name="final_norm",
    )(x, g.reshape(1, d))


def kernel(x, rel_bias, attn_norm, ffn_norm, final_norm, swa_w_qkv, swa_b_qkv, swa_sinks, swa_w_o, sb_w_qkv, sb_w_o, mla_w_down, mla_q_norm, mla_w_uq, mla_kv_norm, mla_w_ukv, mla_w_o, ffn_w_gate, ffn_w_up, ffn_w_down, moe_router, moe_w_gate, moe_w_up, moe_w_down):
    batch, seq, d = x.shape
    depth = attn_norm.shape[0]
    xt = x.reshape(batch * seq, d)
    swa_bias = _swa_bias(rel_bias)
    for i in range(depth):
        mixer, j = i % N_MIXERS, i // N_MIXERS
        if mixer == 0:
            w, b, cs = _swa_weights(swa_w_qkv[j], swa_b_qkv[j])
            qkv = _norm_proj(xt, attn_norm[i], w, b, cs, "swa_proj")
            o = _swa_attention(qkv, swa_bias, swa_sinks[j] * LOG2E, batch, seq)
            wo = swa_w_o[j].astype(BF16)
        elif mixer == 1:
            n = sb_w_qkv.shape[2] // 3
            cs = jnp.concatenate([jnp.full((n,), HEAD_DIM ** -0.5 * LOG2E, F32), jnp.ones((n,), F32)])
            qk, vt = _norm_proj_vt(xt, attn_norm[i], sb_w_qkv[j].astype(BF16), cs, n, seq, "sb_proj")
            o = _sb_attention(qk, vt, batch, seq)
            wo = sb_w_o[j].astype(BF16)
        else:
            a, vt = _mla_proj(xt, attn_norm[i], mla_w_down[j], mla_q_norm[j], mla_w_uq[j],
                              mla_kv_norm[j], mla_w_ukv[j], seq)
            o = _mla_attention(a, vt, batch, seq)
            wo = mla_w_o[j].astype(BF16)
        f = i // 2
        if i % 2 == 0:
            xt = _ffn(xt, o, wo, ffn_norm[i], _layer_bf16(ffn_w_gate, f), _layer_bf16(ffn_w_up, f),
                      _layer_bf16(ffn_w_down, f))
        else:
            last = final_norm if i == depth - 1 else None
            xt = _out_proj(xt, o, wo, "mixer_out")
            xt = _moe(xt, ffn_norm[i], moe_router[f], _layer_bf16(moe_w_gate, f), _layer_bf16(moe_w_up, f),
                      _layer_bf16(moe_w_down, f), last)
    if depth % 2 == 1:
        xt = _final_norm(xt, final_norm)
    return xt.reshape(batch, seq, d)
```

```python
import functools
import math

import numpy as np
import jax
import jax.numpy as jnp
from jax import lax
from jax.experimental import pallas as pl
from jax.experimental.pallas import tpu as pltpu
from jax.experimental.pallas import tpu_sc as plsc

F32 = jnp.float32
BF16 = jnp.bfloat16

N_MIXERS = 3
RMS_EPS = 1e-6
SWA_HEADS = 16
SWA_KV_HEADS = 2
HEAD_DIM = 64
SWA_BLOCK = 128
SWA_WINDOW = 128
REL_BUCKETS = 32
REL_MAX_DIST = 128
SB_HEADS = 16
MLA_HEADS = 16
MLA_NOPE_DIM = 64
MLA_ROPE_DIM = 32
MLA_Q_RANK = 384
MLA_KV_RANK = 256
ROPE_THETA = 10000.0
N_EXPERTS = 8

LANES = 128
NEG = -1e30
LOG2E = math.log2(math.e)
VMEM_LIMIT = 56 * 1024 * 1024

_NT = (((1,), (1,)), ((), ()))


def _dot(a, b):
    return jnp.dot(a, b, preferred_element_type=F32)


def _dot_nt(a, b):
    return lax.dot_general(a, b, _NT, preferred_element_type=F32)


def _rms(x, g):
    return x * lax.rsqrt(jnp.mean(x * x, axis=-1, keepdims=True) + RMS_EPS) * g


def _params(*sem):
    return pltpu.CompilerParams(dimension_semantics=sem, vmem_limit_bytes=VMEM_LIMIT)


def _tile(n, pref):
    return pref if n % pref == 0 else n


PROJ_CHUNK = 512


def _col_chunks(width):
    return [slice(c, min(c + PROJ_CHUNK, width)) for c in range(0, width, PROJ_CHUNK)]


def _norm_proj_kernel(x_ref, g_ref, w_ref, b_ref, cs_ref, o_ref):
    h = _rms(x_ref[...], g_ref[...]).astype(BF16)

    def project(cols, _):
        return _dot(h, w_ref[:, cols])

    def finish(cols, y):
        o_ref[:, cols] = ((y + b_ref[:, cols]) * cs_ref[:, cols]).astype(o_ref.dtype)

    _staggered(_col_chunks(o_ref.shape[1]), (project, finish))


def _norm_proj(x, g, w, b, colscale, name):
    t, d = x.shape
    n = w.shape[1]
    tm = _tile(t, 512)
    return pl.pallas_call(
        _norm_proj_kernel,
        grid=(t // tm,),
        in_specs=[pl.BlockSpec((tm, d), lambda i: (i, 0)),
                  pl.BlockSpec((1, d), lambda i: (0, 0)),
                  pl.BlockSpec((d, n), lambda i: (0, 0)),
                  pl.BlockSpec((1, n), lambda i: (0, 0)),
                  pl.BlockSpec((1, n), lambda i: (0, 0))],
        out_specs=pl.BlockSpec((tm, n), lambda i: (i, 0)),
        out_shape=jax.ShapeDtypeStruct((t, n), BF16),
        compiler_params=_params("parallel"),
        name=name,
    )(x, g.reshape(1, d), w, b.reshape(1, n), colscale.reshape(1, n))


ATT_TQ = 512
ATT_TK = 256


def _store_vt(vt_ref, v, tk, first):
    for s in range(v.shape[0] // tk):
        vt_ref[s, first:first + v.shape[1], :] = v[s * tk:(s + 1) * tk, :].T.astype(vt_ref.dtype)


def _norm_proj_vt_kernel(x_ref, g_ref, w_ref, cs_ref, o_ref, vt_ref, *, tk):
    h = _rms(x_ref[...], g_ref[...]).astype(BF16)
    n = o_ref.shape[1]

    def project(cols, _):
        return _dot(h, w_ref[:, cols])

    def finish(cols, y):
        if cols.start < n:
            o_ref[:, cols] = (y * cs_ref[:, cols]).astype(o_ref.dtype)
        else:
            _store_vt(vt_ref, y, tk, cols.start - n)

    _staggered(_col_chunks(n) + [slice(n + c.start, n + c.stop) for c in _col_chunks(w_ref.shape[1] - n)],
               (project, finish))


def _norm_proj_vt(x, g, w, colscale, nv, seq, name):
    t, d = x.shape
    n = w.shape[1] - nv
    tm = _tile(seq, 512)
    tk = _tile(seq, ATT_TK)
    per = tm // tk
    return pl.pallas_call(
        functools.partial(_norm_proj_vt_kernel, tk=tk),
        grid=(t // tm,),
        in_specs=[pl.BlockSpec((tm, d), lambda i: (i, 0)),
                  pl.BlockSpec((1, d), lambda i: (0, 0)),
                  pl.BlockSpec((d, n + nv), lambda i: (0, 0)),
                  pl.BlockSpec((1, n), lambda i: (0, 0))],
        out_specs=[pl.BlockSpec((tm, n), lambda i: (i, 0)),
                   pl.BlockSpec((per, nv, tk), lambda i: (i, 0, 0))],
        out_shape=[jax.ShapeDtypeStruct((t, n), BF16),
                   jax.ShapeDtypeStruct((t // tk, nv, tk), BF16)],
        compiler_params=_params("parallel"),
        name=name,
    )(x, g.reshape(1, d), w, colscale.reshape(1, n))


def _out_proj_kernel(x_ref, o_ref, w_ref, y_ref):
    y_ref[...] = x_ref[...] + _dot(o_ref[...], w_ref[...])


def _out_proj(x, o, w, name):
    t, d = x.shape
    k = o.shape[1]
    tm = _tile(t, 512)
    return pl.pallas_call(
        _out_proj_kernel,
        grid=(t // tm,),
        in_specs=[pl.BlockSpec((tm, d), lambda i: (i, 0)),
                  pl.BlockSpec((tm, k), lambda i: (i, 0)),
                  pl.BlockSpec((k, d), lambda i: (0, 0))],
        out_specs=pl.BlockSpec((tm, d), lambda i: (i, 0)),
        out_shape=jax.ShapeDtypeStruct((t, d), F32),
        compiler_params=_params("parallel"),
        name=name,
    )(x, o, w)


def _t5_bucket_table():
    qi = np.arange(SWA_BLOCK)[:, None]
    kj = np.arange(2 * SWA_BLOCK)[None, :]
    dist = qi + SWA_BLOCK - kj
    d0 = np.maximum(dist, 0)
    max_exact = REL_BUCKETS // 2
    d = np.maximum(d0, 1).astype(np.float32)
    large = max_exact + (np.log(d / max_exact) / math.log(REL_MAX_DIST / max_exact)
                         * (REL_BUCKETS - max_exact)).astype(np.int32)
    large = np.minimum(large, REL_BUCKETS - 1)
    bucket = np.where(d0 < max_exact, d0, large)
    band = (dist >= 0) & (dist < SWA_WINDOW)
    return np.where(band, bucket, -1).astype(np.int32)


def _swa_bias_kernel(rel_ref, bucket_ref, o_ref):
    h = pl.program_id(0)
    bucket = bucket_ref[...]
    acc = jnp.full(bucket.shape, NEG, F32)
    for b in range(REL_BUCKETS):
        acc = jnp.where(bucket == b, rel_ref[b, h] * LOG2E, acc)
    o_ref[0] = acc


def _swa_bias(rel_bias):
    bucket = jnp.asarray(_t5_bucket_table())
    q, k = bucket.shape
    return pl.pallas_call(
        _swa_bias_kernel,
        grid=(SWA_HEADS,),
        in_specs=[pl.BlockSpec(memory_space=pltpu.SMEM),
                  pl.BlockSpec((q, k), lambda h: (0, 0))],
        out_specs=pl.BlockSpec((1, q, k), lambda h: (h, 0, 0)),
        out_shape=jax.ShapeDtypeStruct((SWA_HEADS, q, k), F32),
        compiler_params=_params("arbitrary"),
        name="swa_bias",
    )(rel_bias, bucket)


def _swa_kernel(sink_ref, q_ref, kc_ref, kp_ref, vc_ref, vp_ref, bias_ref, o_ref, *, tq):
    i = pl.program_id(1)
    g = pl.program_id(2)
    blk = SWA_BLOCK
    heads = bias_ref.shape[0]
    lo_half = lax.broadcasted_iota(jnp.int32, (blk, LANES), 1) < HEAD_DIM
    in_prev = lax.broadcasted_iota(jnp.int32, (heads * blk, 2 * blk), 1) < blk
    bias = bias_ref[...].reshape(heads * blk, 2 * blk)

    def window(cur_ref, prev_ref, sb):
        if sb == 0:
            return jnp.concatenate([prev_ref[...], cur_ref[0:blk, :]], axis=0)
        return cur_ref[(sb - 1) * blk:(sb + 1) * blk, :]

    def scores(sb, _):
        q = q_ref[sb * blk:(sb + 1) * blk, :]
        parts = []
        for h in range(heads):
            qp = q[:, (h // 2) * LANES:(h // 2 + 1) * LANES]
            keep = lo_half if h % 2 == 0 else jnp.logical_not(lo_half)
            parts.append(jnp.where(keep, qp, jnp.zeros_like(qp)))
        s = _dot_nt(jnp.concatenate(parts, axis=0), window(kc_ref, kp_ref, sb)) + bias
        if sb == 0:
            s = jnp.where(jnp.logical_and(in_prev, i == 0), NEG, s)
        return s

    def softmax(sb, s):
        ps, inv = [], []
        for h in range(heads):
            sh = s[h * blk:(h + 1) * blk]
            sink = sink_ref[g * heads + h]
            m = jnp.maximum(jnp.max(sh, axis=1, keepdims=True), sink)
            e = jnp.exp2(sh - m)
            inv.append(1.0 / (jnp.sum(e, axis=1, keepdims=True) + jnp.exp2(sink - m)))
            ps.append(e.astype(BF16))
        return jnp.concatenate(ps, axis=0), inv

    def values(sb, val):
        p, inv = val
        o = _dot(p, window(vc_ref, vp_ref, sb))
        outs = [jnp.where(lo_half, o[h * blk:(h + 1) * blk] * inv[h], o[(h + 1) * blk:(h + 2) * blk] * inv[h + 1])
                for h in range(0, heads, 2)]
        o_ref[sb * blk:(sb + 1) * blk, :] = jnp.concatenate(outs, axis=1).astype(o_ref.dtype)

    _staggered(list(range(tq // blk)), (scores, softmax, values))


def _swa_attention(qkv, bias, sinks, batch, seq):
    t = qkv.shape[0]
    tq = _tile(seq, 512)
    nt = seq // tq
    group = SWA_HEADS // SWA_KV_HEADS
    gw = group * HEAD_DIM
    kcol = SWA_HEADS * HEAD_DIM // LANES
    vcol = kcol + SWA_KV_HEADS
    per = tq // SWA_BLOCK

    def prev_idx(b, i, g):
        return jnp.maximum(b * (seq // SWA_BLOCK) + i * per - 1, 0)

    return pl.pallas_call(
        functools.partial(_swa_kernel, tq=tq),
        grid=(batch, nt, SWA_KV_HEADS),
        in_specs=[pl.BlockSpec(memory_space=pltpu.SMEM),
                  pl.BlockSpec((tq, gw), lambda b, i, g: (b * nt + i, g)),
                  pl.BlockSpec((tq, LANES), lambda b, i, g: (b * nt + i, kcol + g)),
                  pl.BlockSpec((SWA_BLOCK, LANES), lambda b, i, g: (prev_idx(b, i, g), kcol + g)),
                  pl.BlockSpec((tq, LANES), lambda b, i, g: (b * nt + i, vcol + g)),
                  pl.BlockSpec((SWA_BLOCK, LANES), lambda b, i, g: (prev_idx(b, i, g), vcol + g)),
                  pl.BlockSpec((group, SWA_BLOCK, 2 * SWA_BLOCK), lambda b, i, g: (g, 0, 0))],
        out_specs=pl.BlockSpec((tq, gw), lambda b, i, g: (b * nt + i, g)),
        out_shape=jax.ShapeDtypeStruct((t, SWA_HEADS * HEAD_DIM), BF16),
        compiler_params=_params("parallel", "parallel", "arbitrary"),
        name="swa_attention",
    )(sinks, qkv, qkv, qkv, qkv, qkv, bias)


def _swa_weights(w_qkv, b_qkv):
    nq = SWA_HEADS * HEAD_DIM
    kv = SWA_KV_HEADS * HEAD_DIM
    dup = np.concatenate([np.tile(np.arange(HEAD_DIM), 2) + h * HEAD_DIM for h in range(SWA_KV_HEADS)])
    cols = np.concatenate([np.arange(nq), nq + dup, nq + kv + dup])
    scale = np.concatenate([np.full(nq, HEAD_DIM ** -0.5 * LOG2E), np.ones(2 * dup.size)]).astype(np.float32)
    return w_qkv[:, cols].astype(BF16), b_qkv[cols], jnp.asarray(scale)


def _staggered(units, stages):
    vals = list(units)
    for t in range(len(units) + len(stages) - 1):
        for k, stage in enumerate(stages):
            u = t - k
            if 0 <= u < len(units):
                vals[u] = stage(units[u], vals[u])


def _score_ahead(units, next_units, stages, s_ref):
    rest = stages[1:]
    vals = [None] * len(units)
    for t in range(len(units) + len(rest) - 1):
        ahead = stages[0](next_units[t], None) if next_units is not None and t < len(units) else None
        for k, stage in enumerate(rest):
            u = t - k
            if 0 <= u < len(units):
                if k == 0:
                    lanes = units[u][2]
                    vals[u] = stage(units[u], s_ref[u, :, :lanes.stop - lanes.start])
                    if ahead is not None:
                        s_ref[u] = ahead
                else:
                    vals[u] = stage(units[u], vals[u])


def _causal_sweep(i, tq, tk, stages, strict, s_ref, heads=2):
    nsub = tq // tk

    def group(j):
        return [(j * nsub + sb, h, slice(0, tq), None) for sb in reversed(range(nsub)) for h in range(heads)]

    def sweep(has_history):
        diagonal = []
        for sb in reversed(range(nsub)):
            nl = tq - sb * tk
            key = lax.broadcasted_iota(jnp.int32, (tk, nl), 0)
            qry = lax.broadcasted_iota(jnp.int32, (tk, nl), 1)
            mask = key < qry if strict else key <= qry
            diagonal += [(i * nsub + sb, h, slice(sb * tk, tq), mask) for h in range(heads)]
        for u, unit in enumerate(diagonal):
            s_ref[u, :, :tq - unit[2].start] = stages[0](unit, None)
        if not has_history:
            _score_ahead(diagonal, None, stages, s_ref)
            return
        _score_ahead(diagonal, group(i - 1), stages, s_ref)

        def body(n, carry):
            _score_ahead(group(i - 1 - n), group(i - 2 - n), stages, s_ref)
            return carry

        lax.fori_loop(0, i - 1, body, 0)
        _score_ahead(group(0), None, stages, s_ref)

    pl.when(i == 0)(lambda: sweep(False))
    pl.when(i > 0)(lambda: sweep(True))


def _transposed(x):
    return x.astype(F32).T.astype(x.dtype)


def _split_pair(q):
    lo_half = lax.broadcasted_iota(jnp.int32, q.shape, 1) < HEAD_DIM
    zero = jnp.zeros_like(q)
    return jnp.where(lo_half, q, zero), jnp.where(lo_half, zero, q)


SB_STEP_HEADS = 4


def _sb_kernel(q_ref, k_ref, vt_ref, u_ref, o_ref, acc_ref, c_ref, s_ref, *, tq, tk, heads):
    i = pl.program_id(2)
    qts = [_transposed(qm) for pair in range(heads // 2)
           for qm in _split_pair(q_ref[:, pair * LANES:(pair + 1) * LANES])]
    u = u_ref[...]
    acc_ref[...] = jnp.zeros_like(acc_ref)
    c_ref[...] = jnp.zeros_like(c_ref)

    def scores(unit, _):
        jb, h, lanes, _ = unit
        kt = k_ref[pl.ds(pl.multiple_of(jb * tk, tk), tk), (h // 2) * LANES:(h // 2 + 1) * LANES]
        return _dot(kt, qts[h][:, lanes])

    def log_fail(unit, z):
        neg_abs = lax.bitcast_convert_type(lax.bitcast_convert_type(z, jnp.uint32) | jnp.uint32(1 << 31), F32)
        sp = jnp.maximum(z, 0.0) + jnp.log2(1.0 + jnp.exp2(neg_abs))
        if unit[3] is not None:
            sp = jnp.where(unit[3], sp, 0.0)
        return z, sp.astype(BF16)

    def suffix_sum(unit, val):
        z, sp = val
        return z, _dot(u, sp)

    def weights(unit, val):
        _, h, lanes, mask = unit
        z, r = val
        c = c_ref[h:h + 1, lanes]
        la = z - r - c
        if mask is not None:
            la = jnp.where(mask, la, NEG)
        c_ref[h:h + 1, lanes] = c + r[0:1, :]
        return jnp.exp2(la).astype(BF16)

    def values(unit, a):
        jb, h, lanes, _ = unit
        rows = slice(h * HEAD_DIM, (h + 1) * HEAD_DIM)
        acc_ref[rows, lanes] += _dot(vt_ref[jb, rows, :], a)

    _causal_sweep(i, tq, tk, (scores, log_fail, suffix_sum, weights, values), True, s_ref, heads)
    o_ref[...] = acc_ref[...].T.astype(o_ref.dtype)


def _sb_attention(qk, vt, batch, seq):
    t = qk.shape[0]
    tq = _tile(seq, ATT_TQ)
    tk = vt.shape[2]
    nq = seq // tq
    groups = SB_HEADS // SB_STEP_HEADS
    gw = SB_STEP_HEADS * HEAD_DIM
    u = jnp.asarray(np.triu(np.ones((tk, tk), np.float32)), BF16)
    return pl.pallas_call(
        functools.partial(_sb_kernel, tq=tq, tk=tk, heads=SB_STEP_HEADS),
        grid=(batch, groups, nq),
        in_specs=[pl.BlockSpec((tq, gw), lambda b, p, i: (b * nq + i, p)),
                  pl.BlockSpec((seq, gw), lambda b, p, i: (b, groups + p)),
                  pl.BlockSpec((seq // tk, gw, tk), lambda b, p, i: (b, p, 0)),
                  pl.BlockSpec((tk, tk), lambda b, p, i: (0, 0))],
        out_specs=pl.BlockSpec((tq, gw), lambda b, p, i: (b * nq + i, p)),
        out_shape=jax.ShapeDtypeStruct((t, SB_HEADS * HEAD_DIM), BF16),
        scratch_shapes=[pltpu.VMEM((gw, tq), F32), pltpu.VMEM((8, tq), F32),
                        pltpu.VMEM((SB_STEP_HEADS * (tq // tk), tk, tq), F32)],
        compiler_params=_params("parallel", "parallel", "arbitrary"),
        name="sb_attention",
    )(qk, qk, vt, u)


MLA_QN = MLA_HEADS * MLA_NOPE_DIM
MLA_QR = MLA_HEADS * MLA_ROPE_DIM
MLA_OUT = 2 * MLA_QN + MLA_QR + LANES


def _mla_proj_kernel(x_ref, g_ref, wd_ref, qg_ref, wuq_ref, kvg_ref, wukv_ref, cs_ref, o_ref, vt_ref,
                     *, scale, tk):
    h = _rms(x_ref[...], g_ref[...]).astype(BF16)
    c = _dot(h, wd_ref[...])
    cos = cs_ref[:, :LANES]
    sin = cs_ref[:, LANES:]
    cq = _rms(c[:, :MLA_Q_RANK], qg_ref[...]).astype(BF16)
    kv0 = MLA_Q_RANK + MLA_KV_RANK
    ckv = _rms(c[:, MLA_Q_RANK:kv0], kvg_ref[...]).astype(BF16)
    kr = c[:, kv0:kv0 + LANES] * cos + c[:, kv0 + LANES:kv0 + 2 * LANES] * sin
    o_ref[:, MLA_OUT - LANES:] = kr.astype(o_ref.dtype)
    kn0 = MLA_QN + MLA_QR

    units = ([("q_nope", c) for c in _col_chunks(MLA_QN)] + [("q_rope", slice(MLA_QN, MLA_QN + 2 * MLA_QR))]
             + [("k_nope", c) for c in _col_chunks(MLA_QN)]
             + [("v", slice(MLA_QN + c.start, MLA_QN + c.stop)) for c in _col_chunks(MLA_QN)])

    def project(unit, _):
        kind, cols = unit
        return _dot(cq, wuq_ref[:, cols]) if kind.startswith("q") else _dot(ckv, wukv_ref[:, cols])

    def finish(unit, y):
        kind, cols = unit
        if kind == "q_nope":
            o_ref[:, cols] = (y * scale).astype(o_ref.dtype)
        elif kind == "q_rope":
            for m in range(MLA_QR // LANES):
                rot = y[:, m * LANES:(m + 1) * LANES] * cos + y[:, MLA_QR + m * LANES:MLA_QR + (m + 1) * LANES] * sin
                o_ref[:, MLA_QN + m * LANES:MLA_QN + (m + 1) * LANES] = (rot * scale).astype(o_ref.dtype)
        elif kind == "k_nope":
            o_ref[:, kn0 + cols.start:kn0 + cols.stop] = y.astype(o_ref.dtype)
        else:
            _store_vt(vt_ref, y, tk, cols.start - MLA_QN)

    _staggered(units, (project, finish))


def _mla_rope_layout():
    half = MLA_ROPE_DIM // 2
    per = LANES // MLA_ROPE_DIM
    dq = MLA_NOPE_DIM + MLA_ROPE_DIM
    q_nope = np.concatenate([h * dq + np.arange(MLA_NOPE_DIM) for h in range(MLA_HEADS)])
    q_rope = np.zeros(MLA_QR, np.int64)
    q_swap = np.zeros(MLA_QR, np.int64)
    for h in range(MLA_HEADS):
        m, r = divmod(h, per)
        x1 = h * dq + MLA_NOPE_DIM + np.arange(half)
        x2 = x1 + half
        first = m * LANES + r * half + np.arange(half)
        second = first + LANES // 2
        q_rope[first], q_rope[second] = x1, x2
        q_swap[first], q_swap[second] = x2, x1
    k_rope = np.zeros(LANES, np.int64)
    k_swap = np.zeros(LANES, np.int64)
    base = MLA_Q_RANK + MLA_KV_RANK
    for r in range(per):
        first = r * half + np.arange(half)
        second = first + LANES // 2
        k_rope[first], k_rope[second] = base + np.arange(half), base + half + np.arange(half)
        k_swap[first], k_swap[second] = base + half + np.arange(half), base + np.arange(half)
    dkv = MLA_NOPE_DIM + HEAD_DIM
    k_nope = np.concatenate([h * dkv + np.arange(MLA_NOPE_DIM) for h in range(MLA_HEADS)])
    v = k_nope + MLA_NOPE_DIM
    down = np.concatenate([np.arange(base), k_rope, k_swap])
    return np.concatenate([q_nope, q_rope, q_swap]), np.concatenate([k_nope, v]), down


def _mla_rope_tables(seq):
    half = MLA_ROPE_DIM // 2
    inv = ROPE_THETA ** (-jnp.arange(0, MLA_ROPE_DIM, 2, dtype=F32) / MLA_ROPE_DIM)
    ang = jnp.arange(seq, dtype=F32)[:, None] * inv[None, :]
    cos, sin = jnp.cos(ang), jnp.sin(ang)
    reps = LANES // 2 // half
    cos_t = jnp.tile(cos, (1, 2 * reps))
    sin_t = jnp.concatenate([jnp.tile(-sin, (1, reps)), jnp.tile(sin, (1, reps))], axis=1)
    return jnp.concatenate([cos_t, sin_t], axis=1)


def _mla_proj(x, g, w_down, q_norm, w_uq, kv_norm, w_ukv, seq):
    t, d = x.shape
    tm = _tile(seq, 512)
    tk = _tile(seq, ATT_TK)
    ns = seq // tm
    uq_cols, ukv_cols, down_cols = _mla_rope_layout()
    wd = w_down[:, down_cols].astype(BF16)
    wuq = w_uq[:, uq_cols].astype(BF16)
    wukv = w_ukv[:, ukv_cols].astype(BF16)
    cs = _mla_rope_tables(seq)
    scale = (MLA_NOPE_DIM + MLA_ROPE_DIM) ** -0.5 * LOG2E
    full = lambda a: pl.BlockSpec(a.shape, lambda i: (0, 0))
    g2, qg, kvg = g.reshape(1, d), q_norm.reshape(1, -1), kv_norm.reshape(1, -1)
    return pl.pallas_call(
        functools.partial(_mla_proj_kernel, scale=scale, tk=tk),
        grid=(t // tm,),
        in_specs=[pl.BlockSpec((tm, d), lambda i: (i, 0)), full(g2), full(wd), full(qg), full(wuq),
                  full(kvg), full(wukv), pl.BlockSpec((tm, 2 * LANES), lambda i: (i % ns, 0))],
        out_specs=[pl.BlockSpec((tm, MLA_OUT), lambda i: (i, 0)),
                   pl.BlockSpec((tm // tk, MLA_QN, tk), lambda i: (i, 0, 0))],
        out_shape=[jax.ShapeDtypeStruct((t, MLA_OUT), BF16),
                   jax.ShapeDtypeStruct((t // tk, MLA_QN, tk), BF16)],
        compiler_params=_params("parallel"),
        name="mla_proj",
    )(x, g2, wd, qg, wuq, kvg, wukv, cs)


MLA_STEP_HEADS = LANES // MLA_ROPE_DIM


def _mla_kernel(qn_ref, qr_ref, kn_ref, kr_ref, vt_ref, o_ref, acc_ref, ml_ref, s_ref, *, tq, tk):
    i = pl.program_id(2)
    heads = MLA_STEP_HEADS
    lane = lax.broadcasted_iota(jnp.int32, (tq, 2 * LANES), 1)
    half = MLA_ROPE_DIM // 2
    qts = []
    for h in range(heads):
        pair, j = divmod(h, 2)
        q = jnp.concatenate([qn_ref[:, pair * LANES:(pair + 1) * LANES], qr_ref[...]], axis=1)
        r0 = LANES + h * half
        sel = jnp.logical_and(lane >= j * HEAD_DIM, lane < (j + 1) * HEAD_DIM)
        sel = jnp.logical_or(sel, jnp.logical_and(lane >= r0, lane < r0 + half))
        r1 = r0 + LANES // 2
        sel = jnp.logical_or(sel, jnp.logical_and(lane >= r1, lane < r1 + half))
        qts.append(_transposed(jnp.where(sel, q, jnp.zeros_like(q))))
    acc_ref[...] = jnp.zeros_like(acc_ref)
    row = lax.broadcasted_iota(jnp.int32, ml_ref.shape, 0)
    ml_ref[...] = jnp.where(row < heads, NEG, 0.0)

    def scores(unit, _):
        jb, h, lanes, _ = unit
        start = pl.multiple_of(jb * tk, tk)
        pair = h // 2
        kt = jnp.concatenate([kn_ref[pl.ds(start, tk), pair * LANES:(pair + 1) * LANES],
                              kr_ref[pl.ds(start, tk), :]], axis=1)
        return _dot(kt, qts[h][:, lanes])

    def softmax(unit, s):
        _, h, lanes, mask = unit
        if mask is not None:
            s = jnp.where(mask, s, NEG)
        m_old = ml_ref[h:h + 1, lanes]
        m_new = jnp.maximum(m_old, jnp.max(s, axis=0, keepdims=True))
        alpha = jnp.exp2(m_old - m_new)
        e = jnp.exp2(s - m_new)
        ml_ref[h:h + 1, lanes] = m_new
        l_row = slice(heads + h, heads + h + 1)
        ml_ref[l_row, lanes] = alpha * ml_ref[l_row, lanes] + jnp.sum(e, axis=0, keepdims=True)
        return alpha, e.astype(BF16)

    def values(unit, val):
        jb, h, lanes, _ = unit
        alpha, e = val
        rows = slice(h * HEAD_DIM, (h + 1) * HEAD_DIM)
        acc_ref[rows, lanes] = alpha * acc_ref[rows, lanes] + _dot(vt_ref[jb, rows, :], e)

    _causal_sweep(i, tq, tk, (scores, softmax, values), False, s_ref, heads)
    head = lax.broadcasted_iota(jnp.int32, acc_ref.shape, 0) // HEAD_DIM
    denom = ml_ref[heads:heads + 1, :]
    for h in range(1, heads):
        denom = jnp.where(head == h, ml_ref[heads + h:heads + h + 1, :], denom)
    o_ref[...] = (acc_ref[...] / denom).T.astype(o_ref.dtype)


def _mla_attention(a, vt, batch, seq):
    t = a.shape[0]
    tq = _tile(seq, ATT_TQ)
    tk = vt.shape[2]
    nq = seq // tq
    groups = MLA_HEADS // MLA_STEP_HEADS
    gw = MLA_STEP_HEADS * HEAD_DIM
    qr0 = MLA_QN // LANES
    kn0 = (MLA_QN + MLA_QR) // gw
    kr0 = (MLA_QN + MLA_QR + MLA_QN) // LANES
    return pl.pallas_call(
        functools.partial(_mla_kernel, tq=tq, tk=tk),
        grid=(batch, groups, nq),
        in_specs=[pl.BlockSpec((tq, gw), lambda b, p, i: (b * nq + i, p)),
                  pl.BlockSpec((tq, LANES), lambda b, p, i: (b * nq + i, qr0 + p)),
                  pl.BlockSpec((seq, gw), lambda b, p, i: (b, kn0 + p)),
                  pl.BlockSpec((seq, LANES), lambda b, p, i: (b, kr0)),
                  pl.BlockSpec((seq // tk, gw, tk), lambda b, p, i: (b, p, 0))],
        out_specs=pl.BlockSpec((tq, gw), lambda b, p, i: (b * nq + i, p)),
        out_shape=jax.ShapeDtypeStruct((t, MLA_HEADS * HEAD_DIM), BF16),
        scratch_shapes=[pltpu.VMEM((gw, tq), F32), pltpu.VMEM((2 * MLA_STEP_HEADS, tq), F32),
                        pltpu.VMEM((MLA_STEP_HEADS * (tq // tk), tk, tq), F32)],
        compiler_params=_params("parallel", "parallel", "arbitrary"),
        name="mla_attention",
    )(a, a, a, a, vt)


FFN_CHUNK = 256


def _swiglu_into(acc_ref, h, wg, wu, wd):
    def gate_up(cols, _):
        return _dot(h, wg(cols)), _dot(h, wu(cols))

    def activate(cols, gu):
        g, u = gu
        return (g * jax.nn.sigmoid(g) * u).astype(BF16)

    def down(cols, a):
        acc_ref[...] += _dot(a, wd(cols))

    return gate_up, activate, down


def _run_swiglu(acc_ref, h, wg, wu, wd, width):
    chunks = [slice(c, min(c + FFN_CHUNK, width)) for c in range(0, width, FFN_CHUNK)]
    _staggered(chunks, _swiglu_into(acc_ref, h, wg, wu, wd))


def _ffn_kernel(x_ref, a_ref, wo_ref, g_ref, wg_ref, wu_ref, wd_ref, o_ref, h_ref):
    f = pl.program_id(1)

    @pl.when(f == 0)
    def _():
        x = x_ref[...] + _dot(a_ref[...], wo_ref[...])
        h_ref[...] = _rms(x, g_ref[...]).astype(BF16)
        o_ref[...] = x

    _run_swiglu(o_ref, h_ref[...], lambda c: wg_ref[:, c], lambda c: wu_ref[:, c], lambda c: wd_ref[c, :],
                wg_ref.shape[1])


def _ffn(x, a, wo, g, wg, wu, wd):
    t, d = x.shape
    fdim = wg.shape[1]
    tm = _tile(t, 512)
    tf = fdim
    return pl.pallas_call(
        _ffn_kernel,
        grid=(t // tm, fdim // tf),
        in_specs=[pl.BlockSpec((tm, d), lambda i, f: (i, 0)),
                  pl.BlockSpec((tm, a.shape[1]), lambda i, f: (i, 0)),
                  pl.BlockSpec(wo.shape, lambda i, f: (0, 0)),
                  pl.BlockSpec((1, d), lambda i, f: (0, 0)),
                  pl.BlockSpec((d, tf), lambda i, f: (0, f)),
                  pl.BlockSpec((d, tf), lambda i, f: (0, f)),
                  pl.BlockSpec((tf, d), lambda i, f: (f, 0))],
        out_specs=pl.BlockSpec((tm, d), lambda i, f: (i, 0)),
        out_shape=jax.ShapeDtypeStruct((t, d), F32),
        scratch_shapes=[pltpu.VMEM((tm, d), BF16)],
        compiler_params=_params("parallel", "arbitrary"),
        name="dense_ffn",
    )(x, a, wo, g.reshape(1, d), wg, wu, wd)


MOE_TILE = 512
META_ROWS = 8
ROUTE_SUB = 128
COMBINE_SUB = 256


def _pack_halves(x):
    half = x.shape[1] // 2
    bits = lax.bitcast_convert_type(x.astype(BF16).astype(F32), jnp.uint32)
    return bits[:, :half] | (bits[:, half:] >> 16)


def _unpack_halves(w):
    hi = lax.bitcast_convert_type(w & jnp.uint32(0xFFFF0000), F32)
    lo = lax.bitcast_convert_type(w << 16, F32)
    return jnp.concatenate([hi, lo], axis=1)


def _router_kernel(x_ref, g_ref, rt_ref, su_ref, eye_ref, hb_ref, meta_ref, metac_ref, cnt_ref, carry_ref):
    @pl.when(pl.program_id(0) == 0)
    def _():
        carry_ref[...] = jnp.zeros_like(carry_ref)

    h = _rms(x_ref[...], g_ref[...])
    hb_ref[...] = _pack_halves(h)
    ne = rt_ref.shape[0]
    r = rt_ref[...]
    r_hi = r.astype(BF16).astype(F32)
    h_hi = h.astype(BF16)
    h_lo = (h - h_hi.astype(F32)).astype(BF16)
    both_r = _dot_nt(jnp.concatenate([r_hi, r - r_hi], axis=0).astype(BF16), h_hi)
    logits = both_r[:ne] + both_r[ne:] + _dot_nt(r_hi.astype(BF16), h_lo)
    ne, ct = logits.shape
    eio = lax.broadcasted_iota(jnp.int32, (ne, ct), 0).astype(F32)
    m1 = jnp.max(logits, axis=0, keepdims=True)
    i1 = jnp.min(jnp.where(logits == m1, eio, float(ne)), axis=0, keepdims=True)
    rest = jnp.where(eio == i1, -jnp.inf, logits)
    m2 = jnp.max(rest, axis=0, keepdims=True)
    i2 = jnp.min(jnp.where(rest == m2, eio, float(ne)), axis=0, keepdims=True)
    e2 = jnp.exp(m2 - m1)
    g1 = 1.0 / (1.0 + e2)
    g2 = e2 / (1.0 + e2)
    oh1 = (eio == i1).astype(F32)
    oh2 = (eio == i2).astype(F32)
    both = oh1 + oh2
    seen = carry_ref[:, 0:1] + _dot(both.astype(BF16), su_ref[...])
    r1 = jnp.sum(oh1 * seen, axis=0, keepdims=True)
    r2 = jnp.sum(oh2 * seen, axis=0, keepdims=True)
    meta = jnp.zeros((META_ROWS, ct), F32)
    for row, val in enumerate((i1, i2, r1, r2, g1, g2)):
        meta = jnp.where(eio == float(row), val, meta)
    meta_ref[...] = meta
    metac = jnp.zeros((ct, META_ROWS), F32)
    rest_bits = meta
    for _ in range(3):
        piece = rest_bits.astype(BF16)
        metac = metac + _dot_nt(eye_ref[...], piece)
        rest_bits = rest_bits - piece.astype(F32)
    metac_ref[...] = metac
    carry_ref[...] = carry_ref[...] + jnp.sum(both, axis=1, keepdims=True)
    cnt_ref[0] = carry_ref[...]


def _router(x, g, router):
    t, d = x.shape
    ct = _tile(t, MOE_TILE)
    nc = t // ct
    su = jnp.asarray(np.triu(np.ones((ct, ct), np.float32), 1), BF16)
    eye = jnp.eye(ct, dtype=BF16)
    return pl.pallas_call(
        _router_kernel,
        grid=(nc,),
        in_specs=[pl.BlockSpec((ct, d), lambda i: (i, 0)),
                  pl.BlockSpec((1, d), lambda i: (0, 0)),
                  pl.BlockSpec((N_EXPERTS, d), lambda i: (0, 0)),
                  pl.BlockSpec((ct, ct), lambda i: (0, 0)),
                  pl.BlockSpec((ct, ct), lambda i: (0, 0))],
        out_specs=[pl.BlockSpec((ct, d // 2), lambda i: (i, 0)),
                   pl.BlockSpec((META_ROWS, ct), lambda i: (0, i)),
                   pl.BlockSpec((ct, META_ROWS), lambda i: (i, 0)),
                   pl.BlockSpec((1, N_EXPERTS, LANES), lambda i: (i, 0, 0))],
        out_shape=[jax.ShapeDtypeStruct((t, d // 2), jnp.uint32),
                   jax.ShapeDtypeStruct((META_ROWS, t), F32),
                   jax.ShapeDtypeStruct((t, META_ROWS), F32),
                   jax.ShapeDtypeStruct((nc, N_EXPERTS, LANES), F32)],
        scratch_shapes=[pltpu.VMEM((N_EXPERTS, LANES), F32)],
        compiler_params=_params("arbitrary"),
        name="moe_router",
    )(x, g.reshape(1, d), router.T, su, eye)


def _group_layout(cnt_after, t, rt):
    rows = 2 * t + N_EXPERTS * rt
    counts = cnt_after[:, :, 0].astype(jnp.int32)
    padded = (counts[-1] + rt - 1) // rt * rt
    off_end = jnp.cumsum(padded)
    off = (off_end - padded).astype(jnp.int32)
    tile_start = jnp.arange(rows // rt, dtype=jnp.int32) * rt
    tile_expert = jnp.minimum(_count_le(off_end, tile_start), N_EXPERTS - 1).astype(jnp.int32)
    tile_valid = (tile_start < off_end[-1]).astype(jnp.int32)
    return rows, off, counts, tile_expert, tile_valid


def _count_le(sorted_vals, x):
    return jnp.sum((sorted_vals[None, :] <= x[:, None]).astype(jnp.int32), axis=1)


def _work_lists(counts, off, rows, t, ct):
    nc = t // ct
    nrt = rows // ct
    before = jnp.concatenate([jnp.zeros((1, N_EXPERTS), jnp.int32), counts[:-1]], axis=0)
    chunk_start = (off[:, None] + before.T).reshape(-1)
    tile_start = jnp.arange(nrt, dtype=jnp.int32) * ct
    starts = jnp.sort(jnp.concatenate([chunk_start, tile_start]))
    ends = jnp.concatenate([starts[1:], jnp.array([rows], jnp.int32)])
    item_r = jnp.minimum(starts // ct, nrt - 1)
    item_c = (_count_le(chunk_start, starts) - 1) % nc
    first = lambda key: jnp.concatenate([jnp.ones((1,), jnp.int32), (key[1:] != key[:-1]).astype(jnp.int32)])
    dispatch = (item_r, item_c, starts, ends, first(item_r))
    order = jnp.argsort(item_c * (rows + 1) + starts)
    c2 = item_c[order]
    combine = (c2, item_r[order], starts[order], ends[order], first(c2))
    return dispatch, combine


def _positions(expert, rank, off_ref):
    base = jnp.zeros_like(rank)
    for e in range(N_EXPERTS):
        base = jnp.where(expert == float(e), off_ref[e].astype(F32), base)
    return base + rank


def _dispatch_kernel(r_ref, c_ref, s_ref, e_ref, first_ref, off_ref, meta_ref, h_ref, xs_ref, gate_ref):
    n = pl.program_id(0)
    rt, ct = xs_ref.shape[0], h_ref.shape[0]

    @pl.when(first_ref[n] == 1)
    def _():
        xs_ref[...] = jnp.zeros_like(xs_ref)
        gate_ref[...] = jnp.zeros_like(gate_ref)

    meta = meta_ref[...]
    pos0 = _positions(meta[0:1], meta[2:3], off_ref)
    pos1 = _positions(meta[1:2], meta[3:4], off_ref)
    sub = min(rt, ROUTE_SUB)
    for sb in range(rt // sub):
        base = r_ref[n] * rt + sb * sub

        @pl.when(jnp.logical_and(base < e_ref[n], base + sub > s_ref[n]))
        def _():
            row = lax.broadcasted_iota(jnp.int32, (sub, ct), 0) + base
            live = jnp.logical_and(row >= s_ref[n], row < e_ref[n])
            rowf = row.astype(F32)
            sel0 = jnp.logical_and(pos0 == rowf, live)
            sel1 = jnp.logical_and(pos1 == rowf, live)
            sel = jnp.logical_or(sel0, sel1).astype(F32).astype(BF16)
            rows = slice(sb * sub, (sb + 1) * sub)
            xs_ref[rows, :] += _pack_halves(_dot(sel, _unpack_halves(h_ref[...]).astype(BF16)))
            gate = jnp.where(sel0, meta[4:5], 0.0) + jnp.where(sel1, meta[5:6], 0.0)
            gate_ref[rows, :] += jnp.sum(gate, axis=1, keepdims=True)


def _dispatch(plan, off, meta, hb, rows, ct):
    t, d = hb.shape
    n_items = plan[0].shape[0]
    return pl.pallas_call(
        _dispatch_kernel,
        grid_spec=pltpu.PrefetchScalarGridSpec(
            num_scalar_prefetch=6,
            grid=(n_items,),
            in_specs=[pl.BlockSpec((META_ROWS, ct), lambda n, r, c, *_: (0, c[n])),
                      pl.BlockSpec((ct, d), lambda n, r, c, *_: (c[n], 0))],
            out_specs=[pl.BlockSpec((ct, d), lambda n, r, c, *_: (r[n], 0)),
                       pl.BlockSpec((ct, 1), lambda n, r, c, *_: (r[n], 0))]),
        out_shape=[jax.ShapeDtypeStruct((rows, d), jnp.uint32),
                   jax.ShapeDtypeStruct((rows, 1), F32)],
        compiler_params=_params("arbitrary"),
        name="moe_dispatch",
    )(*plan, off, meta, hb)


def _expert_kernel(te_ref, tv_ref, xs_ref, gate_ref, wg_ref, wu_ref, wd_ref, y_ref, acc_ref):
    r = pl.program_id(0)
    f = pl.program_id(1)

    @pl.when(f == 0)
    def _():
        acc_ref[...] = jnp.zeros_like(acc_ref)

    @pl.when(tv_ref[r] == 1)
    def _():
        _run_swiglu(acc_ref, _unpack_halves(xs_ref[...]).astype(BF16), lambda c: wg_ref[0, :, c],
                    lambda c: wu_ref[0, :, c], lambda c: wd_ref[0, c, :], wg_ref.shape[2])

    @pl.when(f == pl.num_programs(1) - 1)
    def _():
        y_ref[...] = _pack_halves(acc_ref[...] * gate_ref[...])


def _experts(tile_expert, tile_valid, xs, gate, wg, wu, wd, ct):
    rows, dp = xs.shape
    d = 2 * dp
    fdim = wg.shape[2]
    tf = _tile(fdim, 1792)
    nf = fdim // tf

    def wcol(r, f, te, tv):
        return (te[r], 0, jnp.where(tv[r] == 1, f, nf - 1))

    def wrow(r, f, te, tv):
        return (te[r], jnp.where(tv[r] == 1, f, nf - 1), 0)

    return pl.pallas_call(
        _expert_kernel,
        grid_spec=pltpu.PrefetchScalarGridSpec(
            num_scalar_prefetch=2,
            grid=(rows // ct, nf),
            in_specs=[pl.BlockSpec((ct, dp), lambda r, f, te, tv: (r, 0)),
                      pl.BlockSpec((ct, 1), lambda r, f, te, tv: (r, 0)),
                      pl.BlockSpec((1, d, tf), wcol),
                      pl.BlockSpec((1, d, tf), wcol),
                      pl.BlockSpec((1, tf, d), wrow)],
            out_specs=pl.BlockSpec((ct, dp), lambda r, f, te, tv: (r, 0)),
            scratch_shapes=[pltpu.VMEM((ct, d), F32)]),
        out_shape=jax.ShapeDtypeStruct((rows, dp), jnp.uint32),
        compiler_params=_params("arbitrary", "arbitrary"),
        name="moe_experts",
    )(tile_expert, tile_valid, xs, gate, wg, wu, wd)


def _combine_kernel(c_ref, r_ref, s_ref, e_ref, first_ref, off_ref, x_ref, metac_ref, y_ref, o_ref):
    n = pl.program_id(0)
    ct, rt = x_ref.shape[0], y_ref.shape[0]

    @pl.when(first_ref[n] == 1)
    def _():
        o_ref[...] = x_ref[...]

    metac = metac_ref[...]
    pos0 = _positions(metac[:, 0:1], metac[:, 2:3], off_ref)
    pos1 = _positions(metac[:, 1:2], metac[:, 3:4], off_ref)
    sub = min(rt, COMBINE_SUB)
    for sb in range(rt // sub):
        base = r_ref[n] * rt + sb * sub

        @pl.when(jnp.logical_and(base < e_ref[n], base + sub > s_ref[n]))
        def _():
            row = lax.broadcasted_iota(jnp.int32, (ct, sub), 1) + base
            live = jnp.logical_and(row >= s_ref[n], row < e_ref[n])
            rowf = row.astype(F32)
            sel = jnp.logical_and(jnp.logical_or(pos0 == rowf, pos1 == rowf), live).astype(F32).astype(BF16)
            o_ref[...] += _dot(sel, _unpack_halves(y_ref[sb * sub:(sb + 1) * sub, :]).astype(BF16))


def _combine(plan, off, x, metac, y, ct):
    t, d = x.shape
    n_items = plan[0].shape[0]
    return pl.pallas_call(
        _combine_kernel,
        grid_spec=pltpu.PrefetchScalarGridSpec(
            num_scalar_prefetch=6,
            grid=(n_items,),
            in_specs=[pl.BlockSpec((ct, d), lambda n, c, r, *_: (c[n], 0)),
                      pl.BlockSpec((ct, META_ROWS), lambda n, c, r, *_: (c[n], 0)),
                      pl.BlockSpec((ct, d // 2), lambda n, c, r, *_: (r[n], 0))],
            out_specs=pl.BlockSpec((ct, d), lambda n, c, r, *_: (c[n], 0))),
        out_shape=jax.ShapeDtypeStruct((t, d), F32),
        compiler_params=_params("arbitrary"),
        name="moe_combine",
    )(*plan, off, x, metac, y)


def _cast_kernel(w_ref, o_ref):
    o_ref[...] = w_ref[...].astype(o_ref.dtype)


CAST_BLOCK_BYTES = 8 * 1024 * 1024


def _layer_bf16(w, layer):
    shape = w.shape[1:]
    cols = shape[-1]
    rows = math.prod(shape[:-1])
    tr = rows
    while tr * cols * 4 > CAST_BLOCK_BYTES and tr % 16 == 0:
        tr //= 2
    out = pl.pallas_call(
        _cast_kernel,
        grid=(rows // tr,),
        in_specs=[pl.BlockSpec((None, tr, cols), lambda i: (layer, i, 0))],
        out_specs=pl.BlockSpec((tr, cols), lambda i: (i, 0)),
        out_shape=jax.ShapeDtypeStruct((rows, cols), BF16),
        compiler_params=_params("parallel"),
        name="weights_bf16",
    )(w.reshape(w.shape[0], rows, cols))
    return out.reshape(shape)


SC_CORES = 2
SC_SUBCORES = 16
SC_CHUNK = 128


def _sc_worker_chunks(total):
    per = total // (SC_CORES * SC_SUBCORES)
    wid = lax.axis_index("s") * SC_CORES + lax.axis_index("c")
    return wid * per, per // SC_CHUNK


def _sc_mesh():
    return plsc.VectorSubcoreMesh(core_axis_name="c", subcore_axis_name="s")


def _sc_gather(table, idx):
    v, d = table.shape
    b = idx.shape[0]

    def body(table_hbm, idx_hbm, out_hbm, idx_v, rows_v, sem):
        base, chunks = _sc_worker_chunks(b)

        @pl.loop(0, chunks)
        def _(j):
            start = base + j * SC_CHUNK
            pltpu.sync_copy(idx_hbm.at[pl.ds(start, SC_CHUNK)], idx_v)
            pltpu.async_copy(table_hbm.at[idx_v], rows_v, sem).wait()
            pltpu.sync_copy(rows_v, out_hbm.at[pl.ds(start, SC_CHUNK)])

    return pl.kernel(
        body,
        out_type=jax.ShapeDtypeStruct((b, d), table.dtype),
        mesh=_sc_mesh(),
        scratch_types=[pltpu.VMEM((SC_CHUNK,), jnp.int32),
                       pltpu.VMEM((SC_CHUNK, d), table.dtype),
                       pltpu.SemaphoreType.DMA],
        name="sc_row_gather",
    )(table, idx)


def _sc_scatter_rows(src, idx, n):
    t, d = src.shape
    b = idx.shape[0]

    def body(src_hbm, idx_hbm, out_hbm, idx_v, rows_v):
        base, chunks = _sc_worker_chunks(b)

        @pl.loop(0, chunks)
        def _(j):
            start = base + j * SC_CHUNK
            first = lax.rem(start, t)
            pltpu.sync_copy(idx_hbm.at[pl.ds(start, SC_CHUNK)], idx_v)
            pltpu.sync_copy(src_hbm.at[pl.ds(first, SC_CHUNK)], rows_v)
            pltpu.sync_copy(rows_v, out_hbm.at[idx_v])

    return pl.kernel(
        body,
        out_type=jax.ShapeDtypeStruct((n, d), src.dtype),
        mesh=_sc_mesh(),
        scratch_types=[pltpu.VMEM((SC_CHUNK,), jnp.int32), pltpu.VMEM((SC_CHUNK, d), src.dtype)],
        name="sc_row_scatter",
    )(src, idx)


def _weighted_add_kernel(x_ref, metac_ref, y0_ref, y1_ref, g_ref, o_ref, *, normed):
    gates = metac_ref[...]
    out = (x_ref[...] + gates[:, 4:5] * _unpack_halves(y0_ref[...])
           + gates[:, 5:6] * _unpack_halves(y1_ref[...]))
    o_ref[...] = _rms(out, g_ref[...]) if normed else out


def _weighted_add(x, metac, ysel, out_gain=None):
    t, d = x.shape
    tm = _tile(t, 512)
    nt = t // tm
    gain = jnp.ones((d,), F32) if out_gain is None else out_gain
    return pl.pallas_call(
        functools.partial(_weighted_add_kernel, normed=out_gain is not None),
        grid=(nt,),
        in_specs=[pl.BlockSpec((tm, d), lambda i: (i, 0)),
                  pl.BlockSpec((tm, META_ROWS), lambda i: (i, 0)),
                  pl.BlockSpec((tm, d // 2), lambda i: (i, 0)),
                  pl.BlockSpec((tm, d // 2), lambda i: (nt + i, 0)),
                  pl.BlockSpec((1, d), lambda i: (0, 0))],
        out_specs=pl.BlockSpec((tm, d), lambda i: (i, 0)),
        out_shape=jax.ShapeDtypeStruct((t, d), F32),
        compiler_params=_params("parallel"),
        name="moe_weighted_add",
    )(x, metac, ysel, ysel, gain.reshape(1, d))


def _moe(x, g, router, wg, wu, wd, out_gain=None):
    t = x.shape[0]
    ct = _tile(t, MOE_TILE)
    hb, meta, metac, cnt = _router(x, g, router)
    sc_rows = SC_CORES * SC_SUBCORES * SC_CHUNK
    rows, off, counts, tile_expert, tile_valid = _group_layout(cnt, t, ct)
    if (2 * t) % sc_rows == 0 and rows % sc_rows == 0:
        base = sum(jnp.where(meta[0:2] == float(e), off[e], 0) for e in range(N_EXPERTS))
        pos = (base + meta[2:4].astype(jnp.int32)).reshape(-1)
        xs = _sc_scatter_rows(hb, pos, rows)
        y = _experts(tile_expert, tile_valid, xs, jnp.ones((rows, 1), F32), wg, wu, wd, ct)
        return _weighted_add(x, metac, _sc_gather(y, pos), out_gain)
    dispatch, combine = _work_lists(counts, off, rows, t, ct)
    xs, gate = _dispatch(dispatch, off, meta, hb, rows, ct)
    y = _experts(tile_expert, tile_valid, xs, gate, wg, wu, wd, ct)
    out = _combine(combine, off, x, metac, y, ct)
    return out if out_gain is None else _final_norm(out, out_gain)


def _final_norm_kernel(x_ref, g_ref, o_ref):
    o_ref[...] = _rms(x_ref[...], g_ref[...])


def _final_norm(x, g):
    t, d = x.shape
    tm = _tile(t, 1024)
    return pl.pallas_call(
        _final_norm_kernel,
        grid=(t // tm,),
        in_specs=[pl.BlockSpec((tm, d), lambda i: (i, 0)), pl.BlockSpec((1, d), lambda i: (0, 0))],
        out_specs=pl.BlockSpec((tm, d), lambda i: (i, 0)),
        out_shape=jax.ShapeDtypeStruct((t, d), F32),
        compiler_params=_params("parallel"),
        name="final_norm",
    )(x, g.reshape(1, d))


def kernel(x, rel_bias, attn_norm, ffn_norm, final_norm, swa_w_qkv, swa_b_qkv, swa_sinks, swa_w_o, sb_w_qkv, sb_w_o, mla_w_down, mla_q_norm, mla_w_uq, mla_kv_norm, mla_w_ukv, mla_w_o, ffn_w_gate, ffn_w_up, ffn_w_down, moe_router, moe_w_gate, moe_w_up, moe_w_down):
    batch, seq, d = x.shape
    depth = attn_norm.shape[0]
    xt = x.reshape(batch * seq, d)
    swa_bias = _swa_bias(rel_bias)
    for i in range(depth):
        mixer, j = i % N_MIXERS, i // N_MIXERS
        if mixer == 0:
            w, b, cs = _swa_weights(swa_w_qkv[j], swa_b_qkv[j])
            qkv = _norm_proj(xt, attn_norm[i], w, b, cs, "swa_proj")
            o = _swa_attention(qkv, swa_bias, swa_sinks[j] * LOG2E, batch, seq)
            wo = swa_w_o[j].astype(BF16)
        elif mixer == 1:
            n = sb_w_qkv.shape[2] // 3
            cs = jnp.concatenate([jnp.full((n,), HEAD_DIM ** -0.5 * LOG2E, F32), jnp.ones((n,), F32)])
            qk, vt = _norm_proj_vt(xt, attn_norm[i], sb_w_qkv[j].astype(BF16), cs, n, seq, "sb_proj")
            o = _sb_attention(qk, vt, batch, seq)
            wo = sb_w_o[j].astype(BF16)
        else:
            a, vt = _mla_proj(xt, attn_norm[i], mla_w_down[j], mla_q_norm[j], mla_w_uq[j],
                              mla_kv_norm[j], mla_w_ukv[j], seq)
            o = _mla_attention(a, vt, batch, seq)
            wo = mla_w_o[j].astype(BF16)
        f = i // 2
        if i % 2 == 0:
            xt = _ffn(xt, o, wo, ffn_norm[i], _layer_bf16(ffn_w_gate, f), _layer_bf16(ffn_w_up, f),
                      _layer_bf16(ffn_w_down, f))
        else:
            last = final_norm if i == depth - 1 else None
            xt = _out_proj(xt, o, wo, "mixer_out")
            xt = _moe(xt, ffn_norm[i], moe_router[f], _layer_bf16(moe_w_gate, f), _layer_bf16(moe_w_up, f),
                      _layer_bf16(moe_w_down, f), last)
    if depth % 2 == 1:
        xt = _final_norm(xt, final_norm)
    return xt.reshape(batch, seq, d)
```

```python
import functools
import math

import numpy as np
import jax
import jax.numpy as jnp
from jax import lax
from jax.experimental import pallas as pl
from jax.experimental.pallas import tpu as pltpu
from jax.experimental.pallas import tpu_sc as plsc

F32 = jnp.float32
BF16 = jnp.bfloat16

N_MIXERS = 3
RMS_EPS = 1e-6
SWA_HEADS = 16
SWA_KV_HEADS = 2
HEAD_DIM = 64
SWA_BLOCK = 128
SWA_WINDOW = 128
REL_BUCKETS = 32
REL_MAX_DIST = 128
SB_HEADS = 16
MLA_HEADS = 16
MLA_NOPE_DIM = 64
MLA_ROPE_DIM = 32
MLA_Q_RANK = 384
MLA_KV_RANK = 256
ROPE_THETA = 10000.0
N_EXPERTS = 8

LANES = 128
NEG = -1e30
LOG2E = math.log2(math.e)
VMEM_LIMIT = 56 * 1024 * 1024

_NT = (((1,), (1,)), ((), ()))


def _dot(a, b):
    return jnp.dot(a, b, preferred_element_type=F32)


def _dot_nt(a, b):
    return lax.dot_general(a, b, _NT, preferred_element_type=F32)


def _rms(x, g):
    return x * lax.rsqrt(jnp.mean(x * x, axis=-1, keepdims=True) + RMS_EPS) * g


def _params(*sem):
    return pltpu.CompilerParams(dimension_semantics=sem, vmem_limit_bytes=VMEM_LIMIT)


def _tile(n, pref):
    return pref if n % pref == 0 else n


def _norm_proj_kernel(x_ref, g_ref, w_ref, b_ref, cs_ref, o_ref):
    h = _rms(x_ref[...], g_ref[...]).astype(BF16)
    y = (_dot(h, w_ref[...]) + b_ref[...]) * cs_ref[...]
    o_ref[...] = y.astype(o_ref.dtype)


def _norm_proj(x, g, w, b, colscale, name):
    t, d = x.shape
    n = w.shape[1]
    tm = _tile(t, 512)
    return pl.pallas_call(
        _norm_proj_kernel,
        grid=(t // tm,),
        in_specs=[pl.BlockSpec((tm, d), lambda i: (i, 0)),
                  pl.BlockSpec((1, d), lambda i: (0, 0)),
                  pl.BlockSpec((d, n), lambda i: (0, 0)),
                  pl.BlockSpec((1, n), lambda i: (0, 0)),
                  pl.BlockSpec((1, n), lambda i: (0, 0))],
        out_specs=pl.BlockSpec((tm, n), lambda i: (i, 0)),
        out_shape=jax.ShapeDtypeStruct((t, n), BF16),
        compiler_params=_params("parallel"),
        name=name,
    )(x, g.reshape(1, d), w, b.reshape(1, n), colscale.reshape(1, n))


ATT_TQ = 512
ATT_TK = 256


def _store_vt(vt_ref, v, tk):
    for s in range(v.shape[0] // tk):
        vt_ref[s] = v[s * tk:(s + 1) * tk, :].T.astype(vt_ref.dtype)


def _norm_proj_vt_kernel(x_ref, g_ref, w_ref, cs_ref, o_ref, vt_ref, *, tk):
    h = _rms(x_ref[...], g_ref[...]).astype(BF16)
    y = _dot(h, w_ref[...])
    n = o_ref.shape[1]
    o_ref[...] = (y[:, :n] * cs_ref[...]).astype(o_ref.dtype)
    _store_vt(vt_ref, y[:, n:], tk)


def _norm_proj_vt(x, g, w, colscale, nv, seq, name):
    t, d = x.shape
    n = w.shape[1] - nv
    tm = _tile(seq, 512)
    tk = _tile(seq, ATT_TK)
    per = tm // tk
    return pl.pallas_call(
        functools.partial(_norm_proj_vt_kernel, tk=tk),
        grid=(t // tm,),
        in_specs=[pl.BlockSpec((tm, d), lambda i: (i, 0)),
                  pl.BlockSpec((1, d), lambda i: (0, 0)),
                  pl.BlockSpec((d, n + nv), lambda i: (0, 0)),
                  pl.BlockSpec((1, n), lambda i: (0, 0))],
        out_specs=[pl.BlockSpec((tm, n), lambda i: (i, 0)),
                   pl.BlockSpec((per, nv, tk), lambda i: (i, 0, 0))],
        out_shape=[jax.ShapeDtypeStruct((t, n), BF16),
                   jax.ShapeDtypeStruct((t // tk, nv, tk), BF16)],
        compiler_params=_params("parallel"),
        name=name,
    )(x, g.reshape(1, d), w, colscale.reshape(1, n))


def _out_proj_kernel(x_ref, o_ref, w_ref, y_ref):
    y_ref[...] = x_ref[...] + _dot(o_ref[...], w_ref[...])


def _out_proj(x, o, w, name):
    t, d = x.shape
    k = o.shape[1]
    tm = _tile(t, 512)
    return pl.pallas_call(
        _out_proj_kernel,
        grid=(t // tm,),
        in_specs=[pl.BlockSpec((tm, d), lambda i: (i, 0)),
                  pl.BlockSpec((tm, k), lambda i: (i, 0)),
                  pl.BlockSpec((k, d), lambda i: (0, 0))],
        out_specs=pl.BlockSpec((tm, d), lambda i: (i, 0)),
        out_shape=jax.ShapeDtypeStruct((t, d), F32),
        compiler_params=_params("parallel"),
        name=name,
    )(x, o, w)


def _t5_bucket_table():
    qi = np.arange(SWA_BLOCK)[:, None]
    kj = np.arange(2 * SWA_BLOCK)[None, :]
    dist = qi + SWA_BLOCK - kj
    d0 = np.maximum(dist, 0)
    max_exact = REL_BUCKETS // 2
    d = np.maximum(d0, 1).astype(np.float32)
    large = max_exact + (np.log(d / max_exact) / math.log(REL_MAX_DIST / max_exact)
                         * (REL_BUCKETS - max_exact)).astype(np.int32)
    large = np.minimum(large, REL_BUCKETS - 1)
    bucket = np.where(d0 < max_exact, d0, large)
    band = (dist >= 0) & (dist < SWA_WINDOW)
    return np.where(band, bucket, -1).astype(np.int32)


def _swa_bias_kernel(rel_ref, bucket_ref, o_ref):
    h = pl.program_id(0)
    bucket = bucket_ref[...]
    acc = jnp.full(bucket.shape, NEG, F32)
    for b in range(REL_BUCKETS):
        acc = jnp.where(bucket == b, rel_ref[b, h] * LOG2E, acc)
    o_ref[0] = acc


def _swa_bias(rel_bias):
    bucket = jnp.asarray(_t5_bucket_table())
    q, k = bucket.shape
    return pl.pallas_call(
        _swa_bias_kernel,
        grid=(SWA_HEADS,),
        in_specs=[pl.BlockSpec(memory_space=pltpu.SMEM),
                  pl.BlockSpec((q, k), lambda h: (0, 0))],
        out_specs=pl.BlockSpec((1, q, k), lambda h: (h, 0, 0)),
        out_shape=jax.ShapeDtypeStruct((SWA_HEADS, q, k), F32),
        compiler_params=_params("arbitrary"),
        name="swa_bias",
    )(rel_bias, bucket)


def _swa_kernel(sink_ref, q_ref, kc_ref, kp_ref, vc_ref, vp_ref, bias_ref, o_ref, *, tq):
    i = pl.program_id(1)
    g = pl.program_id(2)
    blk = SWA_BLOCK
    heads = bias_ref.shape[0]
    lo_half = lax.broadcasted_iota(jnp.int32, (blk, LANES), 1) < HEAD_DIM
    in_prev = lax.broadcasted_iota(jnp.int32, (heads * blk, 2 * blk), 1) < blk
    bias = bias_ref[...].reshape(heads * blk, 2 * blk)

    def window(cur_ref, prev_ref, sb):
        if sb == 0:
            return jnp.concatenate([prev_ref[...], cur_ref[0:blk, :]], axis=0)
        return cur_ref[(sb - 1) * blk:(sb + 1) * blk, :]

    def scores(sb, _):
        q = q_ref[sb * blk:(sb + 1) * blk, :]
        parts = []
        for h in range(heads):
            qp = q[:, (h // 2) * LANES:(h // 2 + 1) * LANES]
            keep = lo_half if h % 2 == 0 else jnp.logical_not(lo_half)
            parts.append(jnp.where(keep, qp, jnp.zeros_like(qp)))
        s = _dot_nt(jnp.concatenate(parts, axis=0), window(kc_ref, kp_ref, sb)) + bias
        if sb == 0:
            s = jnp.where(jnp.logical_and(in_prev, i == 0), NEG, s)
        return s

    def softmax(sb, s):
        ps, inv = [], []
        for h in range(heads):
            sh = s[h * blk:(h + 1) * blk]
            sink = sink_ref[g * heads + h]
            m = jnp.maximum(jnp.max(sh, axis=1, keepdims=True), sink)
            e = jnp.exp2(sh - m)
            inv.append(1.0 / (jnp.sum(e, axis=1, keepdims=True) + jnp.exp2(sink - m)))
            ps.append(e.astype(BF16))
        return jnp.concatenate(ps, axis=0), inv

    def values(sb, val):
        p, inv = val
        o = _dot(p, window(vc_ref, vp_ref, sb))
        outs = [jnp.where(lo_half, o[h * blk:(h + 1) * blk] * inv[h], o[(h + 1) * blk:(h + 2) * blk] * inv[h + 1])
                for h in range(0, heads, 2)]
        o_ref[sb * blk:(sb + 1) * blk, :] = jnp.concatenate(outs, axis=1).astype(o_ref.dtype)

    _staggered(list(range(tq // blk)), (scores, softmax, values))


def _swa_attention(qkv, bias, sinks, batch, seq):
    t = qkv.shape[0]
    tq = _tile(seq, 512)
    nt = seq // tq
    group = SWA_HEADS // SWA_KV_HEADS
    gw = group * HEAD_DIM
    kcol = SWA_HEADS * HEAD_DIM // LANES
    vcol = kcol + SWA_KV_HEADS
    per = tq // SWA_BLOCK

    def prev_idx(b, i, g):
        return jnp.maximum(b * (seq // SWA_BLOCK) + i * per - 1, 0)

    return pl.pallas_call(
        functools.partial(_swa_kernel, tq=tq),
        grid=(batch, nt, SWA_KV_HEADS),
        in_specs=[pl.BlockSpec(memory_space=pltpu.SMEM),
                  pl.BlockSpec((tq, gw), lambda b, i, g: (b * nt + i, g)),
                  pl.BlockSpec((tq, LANES), lambda b, i, g: (b * nt + i, kcol + g)),
                  pl.BlockSpec((SWA_BLOCK, LANES), lambda b, i, g: (prev_idx(b, i, g), kcol + g)),
                  pl.BlockSpec((tq, LANES), lambda b, i, g: (b * nt + i, vcol + g)),
                  pl.BlockSpec((SWA_BLOCK, LANES), lambda b, i, g: (prev_idx(b, i, g), vcol + g)),
                  pl.BlockSpec((group, SWA_BLOCK, 2 * SWA_BLOCK), lambda b, i, g: (g, 0, 0))],
        out_specs=pl.BlockSpec((tq, gw), lambda b, i, g: (b * nt + i, g)),
        out_shape=jax.ShapeDtypeStruct((t, SWA_HEADS * HEAD_DIM), BF16),
        compiler_params=_params("parallel", "parallel", "arbitrary"),
        name="swa_attention",
    )(sinks, qkv, qkv, qkv, qkv, qkv, bias)


def _swa_weights(w_qkv, b_qkv):
    nq = SWA_HEADS * HEAD_DIM
    kv = SWA_KV_HEADS * HEAD_DIM
    dup = np.concatenate([np.tile(np.arange(HEAD_DIM), 2) + h * HEAD_DIM for h in range(SWA_KV_HEADS)])
    cols = np.concatenate([np.arange(nq), nq + dup, nq + kv + dup])
    scale = np.concatenate([np.full(nq, HEAD_DIM ** -0.5 * LOG2E), np.ones(2 * dup.size)]).astype(np.float32)
    return w_qkv[:, cols].astype(BF16), b_qkv[cols], jnp.asarray(scale)


def _staggered(units, stages):
    vals = list(units)
    for t in range(len(units) + len(stages) - 1):
        for k, stage in enumerate(stages):
            u = t - k
            if 0 <= u < len(units):
                vals[u] = stage(units[u], vals[u])


def _score_ahead(units, next_units, stages, s_ref):
    rest = stages[1:]
    vals = [None] * len(units)
    for t in range(len(units) + len(rest) - 1):
        ahead = stages[0](next_units[t], None) if next_units is not None and t < len(units) else None
        for k, stage in enumerate(rest):
            u = t - k
            if 0 <= u < len(units):
                if k == 0:
                    lanes = units[u][2]
                    vals[u] = stage(units[u], s_ref[u, :, :lanes.stop - lanes.start])
                    if ahead is not None:
                        s_ref[u] = ahead
                else:
                    vals[u] = stage(units[u], vals[u])


def _causal_sweep(i, tq, tk, stages, strict, s_ref, heads=2):
    nsub = tq // tk

    def group(j):
        return [(j * nsub + sb, h, slice(0, tq), None) for sb in reversed(range(nsub)) for h in range(heads)]

    def sweep(has_history):
        diagonal = []
        for sb in reversed(range(nsub)):
            nl = tq - sb * tk
            key = lax.broadcasted_iota(jnp.int32, (tk, nl), 0)
            qry = lax.broadcasted_iota(jnp.int32, (tk, nl), 1)
            mask = key < qry if strict else key <= qry
            diagonal += [(i * nsub + sb, h, slice(sb * tk, tq), mask) for h in range(heads)]
        for u, unit in enumerate(diagonal):
            s_ref[u, :, :tq - unit[2].start] = stages[0](unit, None)
        if not has_history:
            _score_ahead(diagonal, None, stages, s_ref)
            return
        _score_ahead(diagonal, group(i - 1), stages, s_ref)

        def body(n, carry):
            _score_ahead(group(i - 1 - n), group(i - 2 - n), stages, s_ref)
            return carry

        lax.fori_loop(0, i - 1, body, 0)
        _score_ahead(group(0), None, stages, s_ref)

    pl.when(i == 0)(lambda: sweep(False))
    pl.when(i > 0)(lambda: sweep(True))


def _transposed(x):
    return x.astype(F32).T.astype(x.dtype)


def _split_pair(q):
    lo_half = lax.broadcasted_iota(jnp.int32, q.shape, 1) < HEAD_DIM
    zero = jnp.zeros_like(q)
    return jnp.where(lo_half, q, zero), jnp.where(lo_half, zero, q)


SB_STEP_HEADS = 4


def _sb_kernel(q_ref, k_ref, vt_ref, u_ref, o_ref, acc_ref, c_ref, s_ref, *, tq, tk, heads):
    i = pl.program_id(2)
    qts = [_transposed(qm) for pair in range(heads // 2)
           for qm in _split_pair(q_ref[:, pair * LANES:(pair + 1) * LANES])]
    u = u_ref[...]
    acc_ref[...] = jnp.zeros_like(acc_ref)
    c_ref[...] = jnp.zeros_like(c_ref)

    def scores(unit, _):
        jb, h, lanes, _ = unit
        kt = k_ref[pl.ds(pl.multiple_of(jb * tk, tk), tk), (h // 2) * LANES:(h // 2 + 1) * LANES]
        return _dot(kt, qts[h][:, lanes])

    def log_fail(unit, z):
        neg_abs = lax.bitcast_convert_type(lax.bitcast_convert_type(z, jnp.uint32) | jnp.uint32(1 << 31), F32)
        sp = jnp.maximum(z, 0.0) + jnp.log2(1.0 + jnp.exp2(neg_abs))
        if unit[3] is not None:
            sp = jnp.where(unit[3], sp, 0.0)
        return z, sp.astype(BF16)

    def suffix_sum(unit, val):
        z, sp = val
        return z, _dot(u, sp)

    def weights(unit, val):
        _, h, lanes, mask = unit
        z, r = val
        c = c_ref[h:h + 1, lanes]
        la = z - r - c
        if mask is not None:
            la = jnp.where(mask, la, NEG)
        c_ref[h:h + 1, lanes] = c + r[0:1, :]
        return jnp.exp2(la.astype(BF16))

    def values(unit, a):
        jb, h, lanes, _ = unit
        rows = slice(h * HEAD_DIM, (h + 1) * HEAD_DIM)
        acc_ref[rows, lanes] += _dot(vt_ref[jb, rows, :], a)

    _causal_sweep(i, tq, tk, (scores, log_fail, suffix_sum, weights, values), True, s_ref, heads)
    o_ref[...] = acc_ref[...].T.astype(o_ref.dtype)


def _sb_attention(qk, vt, batch, seq):
    t = qk.shape[0]
    tq = _tile(seq, ATT_TQ)
    tk = vt.shape[2]
    nq = seq // tq
    groups = SB_HEADS // SB_STEP_HEADS
    gw = SB_STEP_HEADS * HEAD_DIM
    u = jnp.asarray(np.triu(np.ones((tk, tk), np.float32)), BF16)
    return pl.pallas_call(
        functools.partial(_sb_kernel, tq=tq, tk=tk, heads=SB_STEP_HEADS),
        grid=(batch, groups, nq),
        in_specs=[pl.BlockSpec((tq, gw), lambda b, p, i: (b * nq + i, p)),
                  pl.BlockSpec((seq, gw), lambda b, p, i: (b, groups + p)),
                  pl.BlockSpec((seq // tk, gw, tk), lambda b, p, i: (b, p, 0)),
                  pl.BlockSpec((tk, tk), lambda b, p, i: (0, 0))],
        out_specs=pl.BlockSpec((tq, gw), lambda b, p, i: (b * nq + i, p)),
        out_shape=jax.ShapeDtypeStruct((t, SB_HEADS * HEAD_DIM), BF16),
        scratch_shapes=[pltpu.VMEM((gw, tq), F32), pltpu.VMEM((8, tq), F32),
                        pltpu.VMEM((SB_STEP_HEADS * (tq // tk), tk, tq), F32)],
        compiler_params=_params("parallel", "parallel", "arbitrary"),
        name="sb_attention",
    )(qk, qk, vt, u)


MLA_QN = MLA_HEADS * MLA_NOPE_DIM
MLA_QR = MLA_HEADS * MLA_ROPE_DIM
MLA_OUT = 2 * MLA_QN + MLA_QR + LANES


def _mla_proj_kernel(x_ref, g_ref, wd_ref, qg_ref, wuq_ref, kvg_ref, wukv_ref, cs_ref, o_ref, vt_ref,
                     *, scale, tk):
    h = _rms(x_ref[...], g_ref[...]).astype(BF16)
    c = _dot(h, wd_ref[...])
    cos = cs_ref[:, :LANES]
    sin = cs_ref[:, LANES:]
    cq = _rms(c[:, :MLA_Q_RANK], qg_ref[...]).astype(BF16)
    q = _dot(cq, wuq_ref[...])
    o_ref[:, :MLA_QN] = (q[:, :MLA_QN] * scale).astype(o_ref.dtype)
    for m in range(MLA_QR // LANES):
        a = MLA_QN + m * LANES
        rot = q[:, a:a + LANES] * cos + q[:, a + MLA_QR:a + MLA_QR + LANES] * sin
        o_ref[:, a:a + LANES] = (rot * scale).astype(o_ref.dtype)
    kv0 = MLA_Q_RANK + MLA_KV_RANK
    ckv = _rms(c[:, MLA_Q_RANK:kv0], kvg_ref[...]).astype(BF16)
    kv = _dot(ckv, wukv_ref[...])
    kn0 = MLA_QN + MLA_QR
    o_ref[:, kn0:kn0 + MLA_QN] = kv[:, :MLA_QN].astype(o_ref.dtype)
    kr = c[:, kv0:kv0 + LANES] * cos + c[:, kv0 + LANES:kv0 + 2 * LANES] * sin
    o_ref[:, MLA_OUT - LANES:] = kr.astype(o_ref.dtype)
    _store_vt(vt_ref, kv[:, MLA_QN:], tk)


def _mla_rope_layout():
    half = MLA_ROPE_DIM // 2
    per = LANES // MLA_ROPE_DIM
    dq = MLA_NOPE_DIM + MLA_ROPE_DIM
    q_nope = np.concatenate([h * dq + np.arange(MLA_NOPE_DIM) for h in range(MLA_HEADS)])
    q_rope = np.zeros(MLA_QR, np.int64)
    q_swap = np.zeros(MLA_QR, np.int64)
    for h in range(MLA_HEADS):
        m, r = divmod(h, per)
        x1 = h * dq + MLA_NOPE_DIM + np.arange(half)
        x2 = x1 + half
        first = m * LANES + r * half + np.arange(half)
        second = first + LANES // 2
        q_rope[first], q_rope[second] = x1, x2
        q_swap[first], q_swap[second] = x2, x1
    k_rope = np.zeros(LANES, np.int64)
    k_swap = np.zeros(LANES, np.int64)
    base = MLA_Q_RANK + MLA_KV_RANK
    for r in range(per):
        first = r * half + np.arange(half)
        second = first + LANES // 2
        k_rope[first], k_rope[second] = base + np.arange(half), base + half + np.arange(half)
        k_swap[first], k_swap[second] = base + half + np.arange(half), base + np.arange(half)
    dkv = MLA_NOPE_DIM + HEAD_DIM
    k_nope = np.concatenate([h * dkv + np.arange(MLA_NOPE_DIM) for h in range(MLA_HEADS)])
    v = k_nope + MLA_NOPE_DIM
    down = np.concatenate([np.arange(base), k_rope, k_swap])
    return np.concatenate([q_nope, q_rope, q_swap]), np.concatenate([k_nope, v]), down


def _mla_rope_tables(seq):
    half = MLA_ROPE_DIM // 2
    inv = ROPE_THETA ** (-jnp.arange(0, MLA_ROPE_DIM, 2, dtype=F32) / MLA_ROPE_DIM)
    ang = jnp.arange(seq, dtype=F32)[:, None] * inv[None, :]
    cos, sin = jnp.cos(ang), jnp.sin(ang)
    reps = LANES // 2 // half
    cos_t = jnp.tile(cos, (1, 2 * reps))
    sin_t = jnp.concatenate([jnp.tile(-sin, (1, reps)), jnp.tile(sin, (1, reps))], axis=1)
    return jnp.concatenate([cos_t, sin_t], axis=1)


def _mla_proj(x, g, w_down, q_norm, w_uq, kv_norm, w_ukv, seq):
    t, d = x.shape
    tm = _tile(seq, 512)
    tk = _tile(seq, ATT_TK)
    ns = seq // tm
    uq_cols, ukv_cols, down_cols = _mla_rope_layout()
    wd = w_down[:, down_cols].astype(BF16)
    wuq = w_uq[:, uq_cols].astype(BF16)
    wukv = w_ukv[:, ukv_cols].astype(BF16)
    cs = _mla_rope_tables(seq)
    scale = (MLA_NOPE_DIM + MLA_ROPE_DIM) ** -0.5 * LOG2E
    full = lambda a: pl.BlockSpec(a.shape, lambda i: (0, 0))
    g2, qg, kvg = g.reshape(1, d), q_norm.reshape(1, -1), kv_norm.reshape(1, -1)
    return pl.pallas_call(
        functools.partial(_mla_proj_kernel, scale=scale, tk=tk),
        grid=(t // tm,),
        in_specs=[pl.BlockSpec((tm, d), lambda i: (i, 0)), full(g2), full(wd), full(qg), full(wuq),
                  full(kvg), full(wukv), pl.BlockSpec((tm, 2 * LANES), lambda i: (i % ns, 0))],
        out_specs=[pl.BlockSpec((tm, MLA_OUT), lambda i: (i, 0)),
                   pl.BlockSpec((tm // tk, MLA_QN, tk), lambda i: (i, 0, 0))],
        out_shape=[jax.ShapeDtypeStruct((t, MLA_OUT), BF16),
                   jax.ShapeDtypeStruct((t // tk, MLA_QN, tk), BF16)],
        compiler_params=_params("parallel"),
        name="mla_proj",
    )(x, g2, wd, qg, wuq, kvg, wukv, cs)


MLA_STEP_HEADS = LANES // MLA_ROPE_DIM


def _mla_kernel(qn_ref, qr_ref, kn_ref, kr_ref, vt_ref, o_ref, acc_ref, ml_ref, s_ref, *, tq, tk):
    i = pl.program_id(2)
    heads = MLA_STEP_HEADS
    lane = lax.broadcasted_iota(jnp.int32, (tq, 2 * LANES), 1)
    half = MLA_ROPE_DIM // 2
    qts = []
    for h in range(heads):
        pair, j = divmod(h, 2)
        q = jnp.concatenate([qn_ref[:, pair * LANES:(pair + 1) * LANES], qr_ref[...]], axis=1)
        r0 = LANES + h * half
        sel = jnp.logical_and(lane >= j * HEAD_DIM, lane < (j + 1) * HEAD_DIM)
        sel = jnp.logical_or(sel, jnp.logical_and(lane >= r0, lane < r0 + half))
        r1 = r0 + LANES // 2
        sel = jnp.logical_or(sel, jnp.logical_and(lane >= r1, lane < r1 + half))
        qts.append(_transposed(jnp.where(sel, q, jnp.zeros_like(q))))
    acc_ref[...] = jnp.zeros_like(acc_ref)
    row = lax.broadcasted_iota(jnp.int32, ml_ref.shape, 0)
    ml_ref[...] = jnp.where(row < heads, NEG, 0.0)

    def scores(unit, _):
        jb, h, lanes, _ = unit
        start = pl.multiple_of(jb * tk, tk)
        pair = h // 2
        kt = jnp.concatenate([kn_ref[pl.ds(start, tk), pair * LANES:(pair + 1) * LANES],
                              kr_ref[pl.ds(start, tk), :]], axis=1)
        return _dot(kt, qts[h][:, lanes])

    def softmax(unit, s):
        _, h, lanes, mask = unit
        if mask is not None:
            s = jnp.where(mask, s, NEG)
        m_old = ml_ref[h:h + 1, lanes]
        m_new = jnp.maximum(m_old, jnp.max(s, axis=0, keepdims=True))
        alpha = jnp.exp2(m_old - m_new)
        e = jnp.exp2(s - m_new)
        ml_ref[h:h + 1, lanes] = m_new
        l_row = slice(heads + h, heads + h + 1)
        ml_ref[l_row, lanes] = alpha * ml_ref[l_row, lanes] + jnp.sum(e, axis=0, keepdims=True)
        return alpha, e.astype(BF16)

    def values(unit, val):
        jb, h, lanes, _ = unit
        alpha, e = val
        rows = slice(h * HEAD_DIM, (h + 1) * HEAD_DIM)
        acc_ref[rows, lanes] = alpha * acc_ref[rows, lanes] + _dot(vt_ref[jb, rows, :], e)

    _causal_sweep(i, tq, tk, (scores, softmax, values), False, s_ref, heads)
    head = lax.broadcasted_iota(jnp.int32, acc_ref.shape, 0) // HEAD_DIM
    denom = ml_ref[heads:heads + 1, :]
    for h in range(1, heads):
        denom = jnp.where(head == h, ml_ref[heads + h:heads + h + 1, :], denom)
    o_ref[...] = (acc_ref[...] / denom).T.astype(o_ref.dtype)


def _mla_attention(a, vt, batch, seq):
    t = a.shape[0]
    tq = _tile(seq, ATT_TQ)
    tk = vt.shape[2]
    nq = seq // tq
    groups = MLA_HEADS // MLA_STEP_HEADS
    gw = MLA_STEP_HEADS * HEAD_DIM
    qr0 = MLA_QN // LANES
    kn0 = (MLA_QN + MLA_QR) // gw
    kr0 = (MLA_QN + MLA_QR + MLA_QN) // LANES
    return pl.pallas_call(
        functools.partial(_mla_kernel, tq=tq, tk=tk),
        grid=(batch, groups, nq),
        in_specs=[pl.BlockSpec((tq, gw), lambda b, p, i: (b * nq + i, p)),
                  pl.BlockSpec((tq, LANES), lambda b, p, i: (b * nq + i, qr0 + p)),
                  pl.BlockSpec((seq, gw), lambda b, p, i: (b, kn0 + p)),
                  pl.BlockSpec((seq, LANES), lambda b, p, i: (b, kr0)),
                  pl.BlockSpec((seq // tk, gw, tk), lambda b, p, i: (b, p, 0))],
        out_specs=pl.BlockSpec((tq, gw), lambda b, p, i: (b * nq + i, p)),
        out_shape=jax.ShapeDtypeStruct((t, MLA_HEADS * HEAD_DIM), BF16),
        scratch_shapes=[pltpu.VMEM((gw, tq), F32), pltpu.VMEM((2 * MLA_STEP_HEADS, tq), F32),
                        pltpu.VMEM((MLA_STEP_HEADS * (tq // tk), tk, tq), F32)],
        compiler_params=_params("parallel", "parallel", "arbitrary"),
        name="mla_attention",
    )(a, a, a, a, vt)


FFN_CHUNK = 256


def _swiglu_into(acc_ref, h, wg, wu, wd):
    def gate_up(cols, _):
        return _dot(h, wg(cols)), _dot(h, wu(cols))

    def activate(cols, gu):
        g, u = gu
        return (g * jax.nn.sigmoid(g) * u).astype(BF16)

    def down(cols, a):
        acc_ref[...] += _dot(a, wd(cols))

    return gate_up, activate, down


def _run_swiglu(acc_ref, h, wg, wu, wd, width):
    chunks = [slice(c, min(c + FFN_CHUNK, width)) for c in range(0, width, FFN_CHUNK)]
    _staggered(chunks, _swiglu_into(acc_ref, h, wg, wu, wd))


def _ffn_kernel(x_ref, a_ref, wo_ref, g_ref, wg_ref, wu_ref, wd_ref, o_ref, h_ref):
    f = pl.program_id(1)

    @pl.when(f == 0)
    def _():
        x = x_ref[...] + _dot(a_ref[...], wo_ref[...])
        h_ref[...] = _rms(x, g_ref[...]).astype(BF16)
        o_ref[...] = x

    _run_swiglu(o_ref, h_ref[...], lambda c: wg_ref[:, c], lambda c: wu_ref[:, c], lambda c: wd_ref[c, :],
                wg_ref.shape[1])


def _ffn(x, a, wo, g, wg, wu, wd):
    t, d = x.shape
    fdim = wg.shape[1]
    tm = _tile(t, 512)
    tf = fdim
    return pl.pallas_call(
        _ffn_kernel,
        grid=(t // tm, fdim // tf),
        in_specs=[pl.BlockSpec((tm, d), lambda i, f: (i, 0)),
                  pl.BlockSpec((tm, a.shape[1]), lambda i, f: (i, 0)),
                  pl.BlockSpec(wo.shape, lambda i, f: (0, 0)),
                  pl.BlockSpec((1, d), lambda i, f: (0, 0)),
                  pl.BlockSpec((d, tf), lambda i, f: (0, f)),
                  pl.BlockSpec((d, tf), lambda i, f: (0, f)),
                  pl.BlockSpec((tf, d), lambda i, f: (f, 0))],
        out_specs=pl.BlockSpec((tm, d), lambda i, f: (i, 0)),
        out_shape=jax.ShapeDtypeStruct((t, d), F32),
        scratch_shapes=[pltpu.VMEM((tm, d), BF16)],
        compiler_params=_params("parallel", "arbitrary"),
        name="dense_ffn",
    )(x, a, wo, g.reshape(1, d), wg, wu, wd)


MOE_TILE = 512
META_ROWS = 8
ROUTE_SUB = 128
COMBINE_SUB = 256


def _pack_halves(x):
    half = x.shape[1] // 2
    bits = lax.bitcast_convert_type(x.astype(BF16).astype(F32), jnp.uint32)
    return bits[:, :half] | (bits[:, half:] >> 16)


def _unpack_halves(w):
    hi = lax.bitcast_convert_type(w & jnp.uint32(0xFFFF0000), F32)
    lo = lax.bitcast_convert_type(w << 16, F32)
    return jnp.concatenate([hi, lo], axis=1)


def _router_kernel(x_ref, g_ref, rt_ref, su_ref, eye_ref, hb_ref, meta_ref, metac_ref, cnt_ref, carry_ref):
    @pl.when(pl.program_id(0) == 0)
    def _():
        carry_ref[...] = jnp.zeros_like(carry_ref)

    h = _rms(x_ref[...], g_ref[...])
    hb_ref[...] = _pack_halves(h)
    ne = rt_ref.shape[0]
    r = rt_ref[...]
    r_hi = r.astype(BF16).astype(F32)
    h_hi = h.astype(BF16)
    h_lo = (h - h_hi.astype(F32)).astype(BF16)
    both_r = _dot_nt(jnp.concatenate([r_hi, r - r_hi], axis=0).astype(BF16), h_hi)
    logits = both_r[:ne] + both_r[ne:] + _dot_nt(r_hi.astype(BF16), h_lo)
    ne, ct = logits.shape
    eio = lax.broadcasted_iota(jnp.int32, (ne, ct), 0).astype(F32)
    m1 = jnp.max(logits, axis=0, keepdims=True)
    i1 = jnp.min(jnp.where(logits == m1, eio, float(ne)), axis=0, keepdims=True)
    rest = jnp.where(eio == i1, -jnp.inf, logits)
    m2 = jnp.max(rest, axis=0, keepdims=True)
    i2 = jnp.min(jnp.where(rest == m2, eio, float(ne)), axis=0, keepdims=True)
    e2 = jnp.exp(m2 - m1)
    g1 = 1.0 / (1.0 + e2)
    g2 = e2 / (1.0 + e2)
    oh1 = (eio == i1).astype(F32)
    oh2 = (eio == i2).astype(F32)
    both = oh1 + oh2
    seen = carry_ref[:, 0:1] + _dot(both.astype(BF16), su_ref[...])
    r1 = jnp.sum(oh1 * seen, axis=0, keepdims=True)
    r2 = jnp.sum(oh2 * seen, axis=0, keepdims=True)
    meta = jnp.zeros((META_ROWS, ct), F32)
    for row, val in enumerate((i1, i2, r1, r2, g1, g2)):
        meta = jnp.where(eio == float(row), val, meta)
    meta_ref[...] = meta
    metac = jnp.zeros((ct, META_ROWS), F32)
    rest_bits = meta
    for _ in range(3):
        piece = rest_bits.astype(BF16)
        metac = metac + _dot_nt(eye_ref[...], piece)
        rest_bits = rest_bits - piece.astype(F32)
    metac_ref[...] = metac
    carry_ref[...] = carry_ref[...] + jnp.sum(both, axis=1, keepdims=True)
    cnt_ref[0] = carry_ref[...]


def _router(x, g, router):
    t, d = x.shape
    ct = _tile(t, MOE_TILE)
    nc = t // ct
    su = jnp.asarray(np.triu(np.ones((ct, ct), np.float32), 1), BF16)
    eye = jnp.eye(ct, dtype=BF16)
    return pl.pallas_call(
        _router_kernel,
        grid=(nc,),
        in_specs=[pl.BlockSpec((ct, d), lambda i: (i, 0)),
                  pl.BlockSpec((1, d), lambda i: (0, 0)),
                  pl.BlockSpec((N_EXPERTS, d), lambda i: (0, 0)),
                  pl.BlockSpec((ct, ct), lambda i: (0, 0)),
                  pl.BlockSpec((ct, ct), lambda i: (0, 0))],
        out_specs=[pl.BlockSpec((ct, d // 2), lambda i: (i, 0)),
                   pl.BlockSpec((META_ROWS, ct), lambda i: (0, i)),
                   pl.BlockSpec((ct, META_ROWS), lambda i: (i, 0)),
                   pl.BlockSpec((1, N_EXPERTS, LANES), lambda i: (i, 0, 0))],
        out_shape=[jax.ShapeDtypeStruct((t, d // 2), jnp.uint32),
                   jax.ShapeDtypeStruct((META_ROWS, t), F32),
                   jax.ShapeDtypeStruct((t, META_ROWS), F32),
                   jax.ShapeDtypeStruct((nc, N_EXPERTS, LANES), F32)],
        scratch_shapes=[pltpu.VMEM((N_EXPERTS, LANES), F32)],
        compiler_params=_params("arbitrary"),
        name="moe_router",
    )(x, g.reshape(1, d), router.T, su, eye)


def _group_layout(cnt_after, t, rt):
    rows = 2 * t + N_EXPERTS * rt
    counts = cnt_after[:, :, 0].astype(jnp.int32)
    padded = (counts[-1] + rt - 1) // rt * rt
    off_end = jnp.cumsum(padded)
    off = (off_end - padded).astype(jnp.int32)
    tile_start = jnp.arange(rows // rt, dtype=jnp.int32) * rt
    tile_expert = jnp.minimum(_count_le(off_end, tile_start), N_EXPERTS - 1).astype(jnp.int32)
    tile_valid = (tile_start < off_end[-1]).astype(jnp.int32)
    return rows, off, counts, tile_expert, tile_valid


def _count_le(sorted_vals, x):
    return jnp.sum((sorted_vals[None, :] <= x[:, None]).astype(jnp.int32), axis=1)


def _work_lists(counts, off, rows, t, ct):
    nc = t // ct
    nrt = rows // ct
    before = jnp.concatenate([jnp.zeros((1, N_EXPERTS), jnp.int32), counts[:-1]], axis=0)
    chunk_start = (off[:, None] + before.T).reshape(-1)
    tile_start = jnp.arange(nrt, dtype=jnp.int32) * ct
    starts = jnp.sort(jnp.concatenate([chunk_start, tile_start]))
    ends = jnp.concatenate([starts[1:], jnp.array([rows], jnp.int32)])
    item_r = jnp.minimum(starts // ct, nrt - 1)
    item_c = (_count_le(chunk_start, starts) - 1) % nc
    first = lambda key: jnp.concatenate([jnp.ones((1,), jnp.int32), (key[1:] != key[:-1]).astype(jnp.int32)])
    dispatch = (item_r, item_c, starts, ends, first(item_r))
    order = jnp.argsort(item_c * (rows + 1) + starts)
    c2 = item_c[order]
    combine = (c2, item_r[order], starts[order], ends[order], first(c2))
    return dispatch, combine


def _positions(expert, rank, off_ref):
    base = jnp.zeros_like(rank)
    for e in range(N_EXPERTS):
        base = jnp.where(expert == float(e), off_ref[e].astype(F32), base)
    return base + rank


def _dispatch_kernel(r_ref, c_ref, s_ref, e_ref, first_ref, off_ref, meta_ref, h_ref, xs_ref, gate_ref):
    n = pl.program_id(0)
    rt, ct = xs_ref.shape[0], h_ref.shape[0]

    @pl.when(first_ref[n] == 1)
    def _():
        xs_ref[...] = jnp.zeros_like(xs_ref)
        gate_ref[...] = jnp.zeros_like(gate_ref)

    meta = meta_ref[...]
    pos0 = _positions(meta[0:1], meta[2:3], off_ref)
    pos1 = _positions(meta[1:2], meta[3:4], off_ref)
    sub = min(rt, ROUTE_SUB)
    for sb in range(rt // sub):
        base = r_ref[n] * rt + sb * sub

        @pl.when(jnp.logical_and(base < e_ref[n], base + sub > s_ref[n]))
        def _():
            row = lax.broadcasted_iota(jnp.int32, (sub, ct), 0) + base
            live = jnp.logical_and(row >= s_ref[n], row < e_ref[n])
            rowf = row.astype(F32)
            sel0 = jnp.logical_and(pos0 == rowf, live)
            sel1 = jnp.logical_and(pos1 == rowf, live)
            sel = jnp.logical_or(sel0, sel1).astype(F32).astype(BF16)
            rows = slice(sb * sub, (sb + 1) * sub)
            xs_ref[rows, :] += _pack_halves(_dot(sel, _unpack_halves(h_ref[...]).astype(BF16)))
            gate = jnp.where(sel0, meta[4:5], 0.0) + jnp.where(sel1, meta[5:6], 0.0)
            gate_ref[rows, :] += jnp.sum(gate, axis=1, keepdims=True)


def _dispatch(plan, off, meta, hb, rows, ct):
    t, d = hb.shape
    n_items = plan[0].shape[0]
    return pl.pallas_call(
        _dispatch_kernel,
        grid_spec=pltpu.PrefetchScalarGridSpec(
            num_scalar_prefetch=6,
            grid=(n_items,),
            in_specs=[pl.BlockSpec((META_ROWS, ct), lambda n, r, c, *_: (0, c[n])),
                      pl.BlockSpec((ct, d), lambda n, r, c, *_: (c[n], 0))],
            out_specs=[pl.BlockSpec((ct, d), lambda n, r, c, *_: (r[n], 0)),
                       pl.BlockSpec((ct, 1), lambda n, r, c, *_: (r[n], 0))]),
        out_shape=[jax.ShapeDtypeStruct((rows, d), jnp.uint32),
                   jax.ShapeDtypeStruct((rows, 1), F32)],
        compiler_params=_params("arbitrary"),
        name="moe_dispatch",
    )(*plan, off, meta, hb)


def _expert_kernel(te_ref, tv_ref, xs_ref, gate_ref, wg_ref, wu_ref, wd_ref, y_ref, acc_ref):
    r = pl.program_id(0)
    f = pl.program_id(1)

    @pl.when(f == 0)
    def _():
        acc_ref[...] = jnp.zeros_like(acc_ref)

    @pl.when(tv_ref[r] == 1)
    def _():
        _run_swiglu(acc_ref, _unpack_halves(xs_ref[...]).astype(BF16), lambda c: wg_ref[0, :, c],
                    lambda c: wu_ref[0, :, c], lambda c: wd_ref[0, c, :], wg_ref.shape[2])

    @pl.when(f == pl.num_programs(1) - 1)
    def _():
        y_ref[...] = _pack_halves(acc_ref[...] * gate_ref[...])


def _experts(tile_expert, tile_valid, xs, gate, wg, wu, wd, ct):
    rows, dp = xs.shape
    d = 2 * dp
    fdim = wg.shape[2]
    tf = _tile(fdim, 1792)
    nf = fdim // tf

    def wcol(r, f, te, tv):
        return (te[r], 0, jnp.where(tv[r] == 1, f, nf - 1))

    def wrow(r, f, te, tv):
        return (te[r], jnp.where(tv[r] == 1, f, nf - 1), 0)

    return pl.pallas_call(
        _expert_kernel,
        grid_spec=pltpu.PrefetchScalarGridSpec(
            num_scalar_prefetch=2,
            grid=(rows // ct, nf),
            in_specs=[pl.BlockSpec((ct, dp), lambda r, f, te, tv: (r, 0)),
                      pl.BlockSpec((ct, 1), lambda r, f, te, tv: (r, 0)),
                      pl.BlockSpec((1, d, tf), wcol),
                      pl.BlockSpec((1, d, tf), wcol),
                      pl.BlockSpec((1, tf, d), wrow)],
            out_specs=pl.BlockSpec((ct, dp), lambda r, f, te, tv: (r, 0)),
            scratch_shapes=[pltpu.VMEM((ct, d), F32)]),
        out_shape=jax.ShapeDtypeStruct((rows, dp), jnp.uint32),
        compiler_params=_params("arbitrary", "arbitrary"),
        name="moe_experts",
    )(tile_expert, tile_valid, xs, gate, wg, wu, wd)


def _combine_kernel(c_ref, r_ref, s_ref, e_ref, first_ref, off_ref, x_ref, metac_ref, y_ref, o_ref):
    n = pl.program_id(0)
    ct, rt = x_ref.shape[0], y_ref.shape[0]

    @pl.when(first_ref[n] == 1)
    def _():
        o_ref[...] = x_ref[...]

    metac = metac_ref[...]
    pos0 = _positions(metac[:, 0:1], metac[:, 2:3], off_ref)
    pos1 = _positions(metac[:, 1:2], metac[:, 3:4], off_ref)
    sub = min(rt, COMBINE_SUB)
    for sb in range(rt // sub):
        base = r_ref[n] * rt + sb * sub

        @pl.when(jnp.logical_and(base < e_ref[n], base + sub > s_ref[n]))
        def _():
            row = lax.broadcasted_iota(jnp.int32, (ct, sub), 1) + base
            live = jnp.logical_and(row >= s_ref[n], row < e_ref[n])
            rowf = row.astype(F32)
            sel = jnp.logical_and(jnp.logical_or(pos0 == rowf, pos1 == rowf), live).astype(F32).astype(BF16)
            o_ref[...] += _dot(sel, _unpack_halves(y_ref[sb * sub:(sb + 1) * sub, :]).astype(BF16))


def _combine(plan, off, x, metac, y, ct):
    t, d = x.shape
    n_items = plan[0].shape[0]
    return pl.pallas_call(
        _combine_kernel,
        grid_spec=pltpu.PrefetchScalarGridSpec(
            num_scalar_prefetch=6,
            grid=(n_items,),
            in_specs=[pl.BlockSpec((ct, d), lambda n, c, r, *_: (c[n], 0)),
                      pl.BlockSpec((ct, META_ROWS), lambda n, c, r, *_: (c[n], 0)),
                      pl.BlockSpec((ct, d // 2), lambda n, c, r, *_: (r[n], 0))],
            out_specs=pl.BlockSpec((ct, d), lambda n, c, r, *_: (c[n], 0))),
        out_shape=jax.ShapeDtypeStruct((t, d), F32),
        compiler_params=_params("arbitrary"),
        name="moe_combine",
    )(*plan, off, x, metac, y)


def _cast_kernel(w_ref, o_ref):
    o_ref[...] = w_ref[...].astype(o_ref.dtype)


CAST_BLOCK_BYTES = 8 * 1024 * 1024


def _layer_bf16(w, layer):
    shape = w.shape[1:]
    cols = shape[-1]
    rows = math.prod(shape[:-1])
    tr = rows
    while tr * cols * 4 > CAST_BLOCK_BYTES and tr % 16 == 0:
        tr //= 2
    out = pl.pallas_call(
        _cast_kernel,
        grid=(rows // tr,),
        in_specs=[pl.BlockSpec((None, tr, cols), lambda i: (layer, i, 0))],
        out_specs=pl.BlockSpec((tr, cols), lambda i: (i, 0)),
        out_shape=jax.ShapeDtypeStruct((rows, cols), BF16),
        compiler_params=_params("parallel"),
        name="weights_bf16",
    )(w.reshape(w.shape[0], rows, cols))
    return out.reshape(shape)


SC_CORES = 2
SC_SUBCORES = 16
SC_CHUNK = 128


def _sc_worker_chunks(total):
    per = total // (SC_CORES * SC_SUBCORES)
    wid = lax.axis_index("s") * SC_CORES + lax.axis_index("c")
    return wid * per, per // SC_CHUNK


def _sc_mesh():
    return plsc.VectorSubcoreMesh(core_axis_name="c", subcore_axis_name="s")


def _sc_gather(table, idx):
    v, d = table.shape
    b = idx.shape[0]

    def body(table_hbm, idx_hbm, out_hbm, idx_v, rows_v, sem):
        base, chunks = _sc_worker_chunks(b)

        @pl.loop(0, chunks)
        def _(j):
            start = base + j * SC_CHUNK
            pltpu.sync_copy(idx_hbm.at[pl.ds(start, SC_CHUNK)], idx_v)
            pltpu.async_copy(table_hbm.at[idx_v], rows_v, sem).wait()
            pltpu.sync_copy(rows_v, out_hbm.at[pl.ds(start, SC_CHUNK)])

    return pl.kernel(
        body,
        out_type=jax.ShapeDtypeStruct((b, d), table.dtype),
        mesh=_sc_mesh(),
        scratch_types=[pltpu.VMEM((SC_CHUNK,), jnp.int32),
                       pltpu.VMEM((SC_CHUNK, d), table.dtype),
                       pltpu.SemaphoreType.DMA],
        name="sc_row_gather",
    )(table, idx)


def _sc_scatter_rows(src, idx, n):
    t, d = src.shape
    b = idx.shape[0]

    def body(src_hbm, idx_hbm, out_hbm, idx_v, rows_v):
        base, chunks = _sc_worker_chunks(b)

        @pl.loop(0, chunks)
        def _(j):
            start = base + j * SC_CHUNK
            first = lax.rem(start, t)
            pltpu.sync_copy(idx_hbm.at[pl.ds(start, SC_CHUNK)], idx_v)
            pltpu.sync_copy(src_hbm.at[pl.ds(first, SC_CHUNK)], rows_v)
            pltpu.sync_copy(rows_v, out_hbm.at[idx_v])

    return pl.kernel(
        body,
        out_type=jax.ShapeDtypeStruct((n, d), src.dtype),
        mesh=_sc_mesh(),
        scratch_types=[pltpu.VMEM((SC_CHUNK,), jnp.int32), pltpu.VMEM((SC_CHUNK, d), src.dtype)],
        name="sc_row_scatter",
    )(src, idx)


def _weighted_add_kernel(x_ref, metac_ref, y0_ref, y1_ref, g_ref, o_ref, *, normed):
    gates = metac_ref[...]
    out = (x_ref[...] + gates[:, 4:5] * _unpack_halves(y0_ref[...])
           + gates[:, 5:6] * _unpack_halves(y1_ref[...]))
    o_ref[...] = _rms(out, g_ref[...]) if normed else out


def _weighted_add(x, metac, ysel, out_gain=None):
    t, d = x.shape
    tm = _tile(t, 512)
    nt = t // tm
    gain = jnp.ones((d,), F32) if out_gain is None else out_gain
    return pl.pallas_call(
        functools.partial(_weighted_add_kernel, normed=out_gain is not None),
        grid=(nt,),
        in_specs=[pl.BlockSpec((tm, d), lambda i: (i, 0)),
                  pl.BlockSpec((tm, META_ROWS), lambda i: (i, 0)),
                  pl.BlockSpec((tm, d // 2), lambda i: (i, 0)),
                  pl.BlockSpec((tm, d // 2), lambda i: (nt + i, 0)),
                  pl.BlockSpec((1, d), lambda i: (0, 0))],
        out_specs=pl.BlockSpec((tm, d), lambda i: (i, 0)),
        out_shape=jax.ShapeDtypeStruct((t, d), F32),
        compiler_params=_params("parallel"),
        name="moe_weighted_add",
    )(x, metac, ysel, ysel, gain.reshape(1, d))


def _moe(x, g, router, wg, wu, wd, out_gain=None):
    t = x.shape[0]
    ct = _tile(t, MOE_TILE)
    hb, meta, metac, cnt = _router(x, g, router)
    sc_rows = SC_CORES * SC_SUBCORES * SC_CHUNK
    rows, off, counts, tile_expert, tile_valid = _group_layout(cnt, t, ct)
    if (2 * t) % sc_rows == 0 and rows % sc_rows == 0:
        base = sum(jnp.where(meta[0:2] == float(e), off[e], 0) for e in range(N_EXPERTS))
        pos = (base + meta[2:4].astype(jnp.int32)).reshape(-1)
        xs = _sc_scatter_rows(hb, pos, rows)
        y = _experts(tile_expert, tile_valid, xs, jnp.ones((rows, 1), F32), wg, wu, wd, ct)
        return _weighted_add(x, metac, _sc_gather(y, pos), out_gain)
    dispatch, combine = _work_lists(counts, off, rows, t, ct)
    xs, gate = _dispatch(dispatch, off, meta, hb, rows, ct)
    y = _experts(tile_expert, tile_valid, xs, gate, wg, wu, wd, ct)
    out = _combine(combine, off, x, metac, y, ct)
    return out if out_gain is None else _final_norm(out, out_gain)


def _final_norm_kernel(x_ref, g_ref, o_ref):
    o_ref[...] = _rms(x_ref[...], g_ref[...])


def _final_norm(x, g):
    t, d = x.shape
    tm = _tile(t, 1024)
    return pl.pallas_call(
        _final_norm_kernel,
        grid=(t // tm,),
        in_specs=[pl.BlockSpec((tm, d), lambda i: (i, 0)), pl.BlockSpec((1, d), lambda i: (0, 0))],
        out_specs=pl.BlockSpec((tm, d), lambda i: (i, 0)),
        out_shape=jax.ShapeDtypeStruct((t, d), F32),
        compiler_params=_params("parallel"),
        name="final_norm",
    )(x, g.reshape(1, d))


def kernel(x, rel_bias, attn_norm, ffn_norm, final_norm, swa_w_qkv, swa_b_qkv, swa_sinks, swa_w_o, sb_w_qkv, sb_w_o, mla_w_down, mla_q_norm, mla_w_uq, mla_kv_norm, mla_w_ukv, mla_w_o, ffn_w_gate, ffn_w_up, ffn_w_down, moe_router, moe_w_gate, moe_w_up, moe_w_down):
    batch, seq, d = x.shape
    depth = attn_norm.shape[0]
    xt = x.reshape(batch * seq, d)
    swa_bias = _swa_bias(rel_bias)
    for i in range(depth):
        mixer, j = i % N_MIXERS, i // N_MIXERS
        if mixer == 0:
            w, b, cs = _swa_weights(swa_w_qkv[j], swa_b_qkv[j])
            qkv = _norm_proj(xt, attn_norm[i], w, b, cs, "swa_proj")
            o = _swa_attention(qkv, swa_bias, swa_sinks[j] * LOG2E, batch, seq)
            wo = swa_w_o[j].astype(BF16)
        elif mixer == 1:
            n = sb_w_qkv.shape[2] // 3
            cs = jnp.concatenate([jnp.full((n,), HEAD_DIM ** -0.5 * LOG2E, F32), jnp.ones((n,), F32)])
            qk, vt = _norm_proj_vt(xt, attn_norm[i], sb_w_qkv[j].astype(BF16), cs, n, seq, "sb_proj")
            o = _sb_attention(qk, vt, batch, seq)
            wo = sb_w_o[j].astype(BF16)
        else:
            a, vt = _mla_proj(xt, attn_norm[i], mla_w_down[j], mla_q_norm[j], mla_w_uq[j],
                              mla_kv_norm[j], mla_w_ukv[j], seq)
            o = _mla_attention(a, vt, batch, seq)
            wo = mla_w_o[j].astype(BF16)
        f = i // 2
        if i % 2 == 0:
            xt = _ffn(xt, o, wo, ffn_norm[i], _layer_bf16(ffn_w_gate, f), _layer_bf16(ffn_w_up, f),
                      _layer_bf16(ffn_w_down, f))
        else:
            last = final_norm if i == depth - 1 else None
            xt = _out_proj(xt, o, wo, "mixer_out")
            xt = _moe(xt, ffn_norm[i], moe_router[f], _layer_bf16(moe_w_gate, f), _layer_bf16(moe_w_up, f),
                      _layer_bf16(moe_w_down, f), last)
    if depth % 2 == 1:
        xt = _final_norm(xt, final_norm)
    return xt.reshape(batch, seq, d)
```

```python
import functools
import math

import numpy as np
import jax
import jax.numpy as jnp
from jax import lax
from jax.experimental import pallas as pl
from jax.experimental.pallas import tpu as pltpu
from jax.experimental.pallas import tpu_sc as plsc

F32 = jnp.float32
BF16 = jnp.bfloat16

N_MIXERS = 3
RMS_EPS = 1e-6
SWA_HEADS = 16
SWA_KV_HEADS = 2
HEAD_DIM = 64
SWA_BLOCK = 128
SWA_WINDOW = 128
REL_BUCKETS = 32
REL_MAX_DIST = 128
SB_HEADS = 16
MLA_HEADS = 16
MLA_NOPE_DIM = 64
MLA_ROPE_DIM = 32
MLA_Q_RANK = 384
MLA_KV_RANK = 256
ROPE_THETA = 10000.0
N_EXPERTS = 8

LANES = 128
NEG = -1e30
LOG2E = math.log2(math.e)
VMEM_LIMIT = 56 * 1024 * 1024

_NT = (((1,), (1,)), ((), ()))


def _dot(a, b):
    return jnp.dot(a, b, preferred_element_type=F32)


def _dot_nt(a, b):
    return lax.dot_general(a, b, _NT, preferred_element_type=F32)


def _rms(x, g):
    return x * lax.rsqrt(jnp.mean(x * x, axis=-1, keepdims=True) + RMS_EPS) * g


def _params(*sem):
    return pltpu.CompilerParams(dimension_semantics=sem, vmem_limit_bytes=VMEM_LIMIT)


def _tile(n, pref):
    return pref if n % pref == 0 else n


def _norm_proj_kernel(x_ref, g_ref, w_ref, b_ref, cs_ref, o_ref):
    h = _rms(x_ref[...], g_ref[...]).astype(BF16)
    y = (_dot(h, w_ref[...]) + b_ref[...]) * cs_ref[...]
    o_ref[...] = y.astype(o_ref.dtype)


def _norm_proj(x, g, w, b, colscale, name):
    t, d = x.shape
    n = w.shape[1]
    tm = _tile(t, 512)
    return pl.pallas_call(
        _norm_proj_kernel,
        grid=(t // tm,),
        in_specs=[pl.BlockSpec((tm, d), lambda i: (i, 0)),
                  pl.BlockSpec((1, d), lambda i: (0, 0)),
                  pl.BlockSpec((d, n), lambda i: (0, 0)),
                  pl.BlockSpec((1, n), lambda i: (0, 0)),
                  pl.BlockSpec((1, n), lambda i: (0, 0))],
        out_specs=pl.BlockSpec((tm, n), lambda i: (i, 0)),
        out_shape=jax.ShapeDtypeStruct((t, n), BF16),
        compiler_params=_params("parallel"),
        name=name,
    )(x, g.reshape(1, d), w, b.reshape(1, n), colscale.reshape(1, n))


ATT_TQ = 512
ATT_TK = 256


def _store_vt(vt_ref, v, tk):
    for s in range(v.shape[0] // tk):
        vt_ref[s] = v[s * tk:(s + 1) * tk, :].T.astype(vt_ref.dtype)


def _norm_proj_vt_kernel(x_ref, g_ref, w_ref, cs_ref, o_ref, vt_ref, *, tk):
    h = _rms(x_ref[...], g_ref[...]).astype(BF16)
    y = _dot(h, w_ref[...])
    n = o_ref.shape[1]
    o_ref[...] = (y[:, :n] * cs_ref[...]).astype(o_ref.dtype)
    _store_vt(vt_ref, y[:, n:], tk)


def _norm_proj_vt(x, g, w, colscale, nv, seq, name):
    t, d = x.shape
    n = w.shape[1] - nv
    tm = _tile(seq, 512)
    tk = _tile(seq, ATT_TK)
    per = tm // tk
    return pl.pallas_call(
        functools.partial(_norm_proj_vt_kernel, tk=tk),
        grid=(t // tm,),
        in_specs=[pl.BlockSpec((tm, d), lambda i: (i, 0)),
                  pl.BlockSpec((1, d), lambda i: (0, 0)),
                  pl.BlockSpec((d, n + nv), lambda i: (0, 0)),
                  pl.BlockSpec((1, n), lambda i: (0, 0))],
        out_specs=[pl.BlockSpec((tm, n), lambda i: (i, 0)),
                   pl.BlockSpec((per, nv, tk), lambda i: (i, 0, 0))],
        out_shape=[jax.ShapeDtypeStruct((t, n), BF16),
                   jax.ShapeDtypeStruct((t // tk, nv, tk), BF16)],
        compiler_params=_params("parallel"),
        name=name,
    )(x, g.reshape(1, d), w, colscale.reshape(1, n))


def _out_proj_kernel(x_ref, o_ref, w_ref, y_ref):
    y_ref[...] = x_ref[...] + _dot(o_ref[...], w_ref[...])


def _out_proj(x, o, w, name):
    t, d = x.shape
    k = o.shape[1]
    tm = _tile(t, 512)
    return pl.pallas_call(
        _out_proj_kernel,
        grid=(t // tm,),
        in_specs=[pl.BlockSpec((tm, d), lambda i: (i, 0)),
                  pl.BlockSpec((tm, k), lambda i: (i, 0)),
                  pl.BlockSpec((k, d), lambda i: (0, 0))],
        out_specs=pl.BlockSpec((tm, d), lambda i: (i, 0)),
        out_shape=jax.ShapeDtypeStruct((t, d), F32),
        compiler_params=_params("parallel"),
        name=name,
    )(x, o, w)


def _t5_bucket_table():
    qi = np.arange(SWA_BLOCK)[:, None]
    kj = np.arange(2 * SWA_BLOCK)[None, :]
    dist = qi + SWA_BLOCK - kj
    d0 = np.maximum(dist, 0)
    max_exact = REL_BUCKETS // 2
    d = np.maximum(d0, 1).astype(np.float32)
    large = max_exact + (np.log(d / max_exact) / math.log(REL_MAX_DIST / max_exact)
                         * (REL_BUCKETS - max_exact)).astype(np.int32)
    large = np.minimum(large, REL_BUCKETS - 1)
    bucket = np.where(d0 < max_exact, d0, large)
    band = (dist >= 0) & (dist < SWA_WINDOW)
    return np.where(band, bucket, -1).astype(np.int32)


def _swa_bias_kernel(rel_ref, bucket_ref, o_ref):
    h = pl.program_id(0)
    bucket = bucket_ref[...]
    acc = jnp.full(bucket.shape, NEG, F32)
    for b in range(REL_BUCKETS):
        acc = jnp.where(bucket == b, rel_ref[b, h] * LOG2E, acc)
    o_ref[0] = acc


def _swa_bias(rel_bias):
    bucket = jnp.asarray(_t5_bucket_table())
    q, k = bucket.shape
    return pl.pallas_call(
        _swa_bias_kernel,
        grid=(SWA_HEADS,),
        in_specs=[pl.BlockSpec(memory_space=pltpu.SMEM),
                  pl.BlockSpec((q, k), lambda h: (0, 0))],
        out_specs=pl.BlockSpec((1, q, k), lambda h: (h, 0, 0)),
        out_shape=jax.ShapeDtypeStruct((SWA_HEADS, q, k), F32),
        compiler_params=_params("arbitrary"),
        name="swa_bias",
    )(rel_bias, bucket)


def _swa_kernel(sink_ref, q_ref, kc_ref, kp_ref, vc_ref, vp_ref, bias_ref, o_ref, *, tq):
    i = pl.program_id(1)
    g = pl.program_id(2)
    blk = SWA_BLOCK
    heads = bias_ref.shape[0]
    lo_half = lax.broadcasted_iota(jnp.int32, (blk, LANES), 1) < HEAD_DIM
    in_prev = lax.broadcasted_iota(jnp.int32, (heads * blk, 2 * blk), 1) < blk
    bias = bias_ref[...].reshape(heads * blk, 2 * blk)

    def window(cur_ref, prev_ref, sb):
        if sb == 0:
            return jnp.concatenate([prev_ref[...], cur_ref[0:blk, :]], axis=0)
        return cur_ref[(sb - 1) * blk:(sb + 1) * blk, :]

    def scores(sb, _):
        q = q_ref[sb * blk:(sb + 1) * blk, :]
        parts = []
        for h in range(heads):
            qp = q[:, (h // 2) * LANES:(h // 2 + 1) * LANES]
            keep = lo_half if h % 2 == 0 else jnp.logical_not(lo_half)
            parts.append(jnp.where(keep, qp, jnp.zeros_like(qp)))
        s = _dot_nt(jnp.concatenate(parts, axis=0), window(kc_ref, kp_ref, sb)) + bias
        if sb == 0:
            s = jnp.where(jnp.logical_and(in_prev, i == 0), NEG, s)
        return s

    def softmax(sb, s):
        ps, inv = [], []
        for h in range(heads):
            sh = s[h * blk:(h + 1) * blk]
            sink = sink_ref[g * heads + h]
            m = jnp.maximum(jnp.max(sh, axis=1, keepdims=True), sink)
            e = jnp.exp2(sh - m)
            inv.append(1.0 / (jnp.sum(e, axis=1, keepdims=True) + jnp.exp2(sink - m)))
            ps.append(e.astype(BF16))
        return jnp.concatenate(ps, axis=0), inv

    def values(sb, val):
        p, inv = val
        o = _dot(p, window(vc_ref, vp_ref, sb))
        outs = [jnp.where(lo_half, o[h * blk:(h + 1) * blk] * inv[h], o[(h + 1) * blk:(h + 2) * blk] * inv[h + 1])
                for h in range(0, heads, 2)]
        o_ref[sb * blk:(sb + 1) * blk, :] = jnp.concatenate(outs, axis=1).astype(o_ref.dtype)

    _staggered(list(range(tq // blk)), (scores, softmax, values))


def _swa_attention(qkv, bias, sinks, batch, seq):
    t = qkv.shape[0]
    tq = _tile(seq, 512)
    nt = seq // tq
    group = SWA_HEADS // SWA_KV_HEADS
    gw = group * HEAD_DIM
    kcol = SWA_HEADS * HEAD_DIM // LANES
    vcol = kcol + SWA_KV_HEADS
    per = tq // SWA_BLOCK

    def prev_idx(b, i, g):
        return jnp.maximum(b * (seq // SWA_BLOCK) + i * per - 1, 0)

    return pl.pallas_call(
        functools.partial(_swa_kernel, tq=tq),
        grid=(batch, nt, SWA_KV_HEADS),
        in_specs=[pl.BlockSpec(memory_space=pltpu.SMEM),
                  pl.BlockSpec((tq, gw), lambda b, i, g: (b * nt + i, g)),
                  pl.BlockSpec((tq, LANES), lambda b, i, g: (b * nt + i, kcol + g)),
                  pl.BlockSpec((SWA_BLOCK, LANES), lambda b, i, g: (prev_idx(b, i, g), kcol + g)),
                  pl.BlockSpec((tq, LANES), lambda b, i, g: (b * nt + i, vcol + g)),
                  pl.BlockSpec((SWA_BLOCK, LANES), lambda b, i, g: (prev_idx(b, i, g), vcol + g)),
                  pl.BlockSpec((group, SWA_BLOCK, 2 * SWA_BLOCK), lambda b, i, g: (g, 0, 0))],
        out_specs=pl.BlockSpec((tq, gw), lambda b, i, g: (b * nt + i, g)),
        out_shape=jax.ShapeDtypeStruct((t, SWA_HEADS * HEAD_DIM), BF16),
        compiler_params=_params("parallel", "parallel", "arbitrary"),
        name="swa_attention",
    )(sinks, qkv, qkv, qkv, qkv, qkv, bias)


def _swa_weights(w_qkv, b_qkv):
    nq = SWA_HEADS * HEAD_DIM
    kv = SWA_KV_HEADS * HEAD_DIM
    dup = np.concatenate([np.tile(np.arange(HEAD_DIM), 2) + h * HEAD_DIM for h in range(SWA_KV_HEADS)])
    cols = np.concatenate([np.arange(nq), nq + dup, nq + kv + dup])
    scale = np.concatenate([np.full(nq, HEAD_DIM ** -0.5 * LOG2E), np.ones(2 * dup.size)]).astype(np.float32)
    return w_qkv[:, cols].astype(BF16), b_qkv[cols], jnp.asarray(scale)


def _staggered(units, stages):
    vals = list(units)
    for t in range(len(units) + len(stages) - 1):
        for k, stage in enumerate(stages):
            u = t - k
            if 0 <= u < len(units):
                vals[u] = stage(units[u], vals[u])


def _score_ahead(units, next_units, stages, s_ref):
    rest = stages[1:]
    vals = [None] * len(units)
    for t in range(len(units) + len(rest) - 1):
        ahead = stages[0](next_units[t], None) if next_units is not None and t < len(units) else None
        for k, stage in enumerate(rest):
            u = t - k
            if 0 <= u < len(units):
                if k == 0:
                    lanes = units[u][2]
                    vals[u] = stage(units[u], s_ref[u, :, :lanes.stop - lanes.start])
                    if ahead is not None:
                        s_ref[u] = ahead
                else:
                    vals[u] = stage(units[u], vals[u])


def _causal_sweep(i, tq, tk, stages, strict, s_ref, heads=2):
    nsub = tq // tk

    def group(j):
        return [(j * nsub + sb, h, slice(0, tq), None) for sb in reversed(range(nsub)) for h in range(heads)]

    def sweep(has_history):
        diagonal = []
        for sb in reversed(range(nsub)):
            nl = tq - sb * tk
            key = lax.broadcasted_iota(jnp.int32, (tk, nl), 0)
            qry = lax.broadcasted_iota(jnp.int32, (tk, nl), 1)
            mask = key < qry if strict else key <= qry
            diagonal += [(i * nsub + sb, h, slice(sb * tk, tq), mask) for h in range(heads)]
        for u, unit in enumerate(diagonal):
            s_ref[u, :, :tq - unit[2].start] = stages[0](unit, None)
        if not has_history:
            _score_ahead(diagonal, None, stages, s_ref)
            return
        _score_ahead(diagonal, group(i - 1), stages, s_ref)

        def body(n, carry):
            _score_ahead(group(i - 1 - n), group(i - 2 - n), stages, s_ref)
            return carry

        lax.fori_loop(0, i - 1, body, 0)
        _score_ahead(group(0), None, stages, s_ref)

    pl.when(i == 0)(lambda: sweep(False))
    pl.when(i > 0)(lambda: sweep(True))


def _transposed(x):
    return x.astype(F32).T.astype(x.dtype)


def _split_pair(q):
    lo_half = lax.broadcasted_iota(jnp.int32, q.shape, 1) < HEAD_DIM
    zero = jnp.zeros_like(q)
    return jnp.where(lo_half, q, zero), jnp.where(lo_half, zero, q)


SB_STEP_HEADS = 4


def _sb_kernel(q_ref, k_ref, vt_ref, u_ref, o_ref, acc_ref, c_ref, s_ref, *, tq, tk, heads):
    i = pl.program_id(2)
    qts = [_transposed(qm) for pair in range(heads // 2)
           for qm in _split_pair(q_ref[:, pair * LANES:(pair + 1) * LANES])]
    u = u_ref[...]
    acc_ref[...] = jnp.zeros_like(acc_ref)
    c_ref[...] = jnp.zeros_like(c_ref)

    def scores(unit, _):
        jb, h, lanes, _ = unit
        kt = k_ref[pl.ds(pl.multiple_of(jb * tk, tk), tk), (h // 2) * LANES:(h // 2 + 1) * LANES]
        return _dot(kt, qts[h][:, lanes])

    def log_fail(unit, z):
        neg_abs = lax.bitcast_convert_type(lax.bitcast_convert_type(z, jnp.uint32) | jnp.uint32(1 << 31), F32)
        sp = jnp.maximum(z, 0.0) + jnp.log2(1.0 + jnp.exp2(neg_abs))
        if unit[3] is not None:
            sp = jnp.where(unit[3], sp, 0.0)
        return z, sp.astype(BF16)

    def suffix_sum(unit, val):
        z, sp = val
        return z, _dot(u, sp)

    def weights(unit, val):
        _, h, lanes, mask = unit
        z, r = val
        c = c_ref[h:h + 1, lanes]
        la = z - r - c
        if mask is not None:
            la = jnp.where(mask, la, NEG)
        c_ref[h:h + 1, lanes] = c + r[0:1, :]
        return jnp.exp2(la.astype(BF16))

    def values(unit, a):
        jb, h, lanes, _ = unit
        rows = slice(h * HEAD_DIM, (h + 1) * HEAD_DIM)
        acc_ref[rows, lanes] += _dot(vt_ref[jb, rows, :], a)

    _causal_sweep(i, tq, tk, (scores, log_fail, suffix_sum, weights, values), True, s_ref, heads)
    o_ref[...] = acc_ref[...].T.astype(o_ref.dtype)


def _sb_attention(qk, vt, batch, seq):
    t = qk.shape[0]
    tq = _tile(seq, ATT_TQ)
    tk = vt.shape[2]
    nq = seq // tq
    groups = SB_HEADS // SB_STEP_HEADS
    gw = SB_STEP_HEADS * HEAD_DIM
    u = jnp.asarray(np.triu(np.ones((tk, tk), np.float32)), BF16)
    return pl.pallas_call(
        functools.partial(_sb_kernel, tq=tq, tk=tk, heads=SB_STEP_HEADS),
        grid=(batch, groups, nq),
        in_specs=[pl.BlockSpec((tq, gw), lambda b, p, i: (b * nq + i, p)),
                  pl.BlockSpec((seq, gw), lambda b, p, i: (b, groups + p)),
                  pl.BlockSpec((seq // tk, gw, tk), lambda b, p, i: (b, p, 0)),
                  pl.BlockSpec((tk, tk), lambda b, p, i: (0, 0))],
        out_specs=pl.BlockSpec((tq, gw), lambda b, p, i: (b * nq + i, p)),
        out_shape=jax.ShapeDtypeStruct((t, SB_HEADS * HEAD_DIM), BF16),
        scratch_shapes=[pltpu.VMEM((gw, tq), F32), pltpu.VMEM((8, tq), F32),
                        pltpu.VMEM((SB_STEP_HEADS * (tq // tk), tk, tq), F32)],
        compiler_params=_params("parallel", "parallel", "arbitrary"),
        name="sb_attention",
    )(qk, qk, vt, u)


MLA_QN = MLA_HEADS * MLA_NOPE_DIM
MLA_QR = MLA_HEADS * MLA_ROPE_DIM
MLA_OUT = 2 * MLA_QN + MLA_QR + LANES


def _mla_proj_kernel(x_ref, g_ref, wd_ref, qg_ref, wuq_ref, kvg_ref, wukv_ref, cs_ref, o_ref, vt_ref,
                     *, scale, tk):
    h = _rms(x_ref[...], g_ref[...]).astype(BF16)
    c = _dot(h, wd_ref[...])
    cos = cs_ref[:, :LANES]
    sin = cs_ref[:, LANES:]
    cq = _rms(c[:, :MLA_Q_RANK], qg_ref[...]).astype(BF16)
    q = _dot(cq, wuq_ref[...])
    o_ref[:, :MLA_QN] = (q[:, :MLA_QN] * scale).astype(o_ref.dtype)
    for m in range(MLA_QR // LANES):
        a = MLA_QN + m * LANES
        rot = q[:, a:a + LANES] * cos + q[:, a + MLA_QR:a + MLA_QR + LANES] * sin
        o_ref[:, a:a + LANES] = (rot * scale).astype(o_ref.dtype)
    kv0 = MLA_Q_RANK + MLA_KV_RANK
    ckv = _rms(c[:, MLA_Q_RANK:kv0], kvg_ref[...]).astype(BF16)
    kv = _dot(ckv, wukv_ref[...])
    kn0 = MLA_QN + MLA_QR
    o_ref[:, kn0:kn0 + MLA_QN] = kv[:, :MLA_QN].astype(o_ref.dtype)
    kr = c[:, kv0:kv0 + LANES] * cos + c[:, kv0 + LANES:kv0 + 2 * LANES] * sin
    o_ref[:, MLA_OUT - LANES:] = kr.astype(o_ref.dtype)
    _store_vt(vt_ref, kv[:, MLA_QN:], tk)


def _mla_rope_layout():
    half = MLA_ROPE_DIM // 2
    per = LANES // MLA_ROPE_DIM
    dq = MLA_NOPE_DIM + MLA_ROPE_DIM
    q_nope = np.concatenate([h * dq + np.arange(MLA_NOPE_DIM) for h in range(MLA_HEADS)])
    q_rope = np.zeros(MLA_QR, np.int64)
    q_swap = np.zeros(MLA_QR, np.int64)
    for h in range(MLA_HEADS):
        m, r = divmod(h, per)
        x1 = h * dq + MLA_NOPE_DIM + np.arange(half)
        x2 = x1 + half
        first = m * LANES + r * half + np.arange(half)
        second = first + LANES // 2
        q_rope[first], q_rope[second] = x1, x2
        q_swap[first], q_swap[second] = x2, x1
    k_rope = np.zeros(LANES, np.int64)
    k_swap = np.zeros(LANES, np.int64)
    base = MLA_Q_RANK + MLA_KV_RANK
    for r in range(per):
        first = r * half + np.arange(half)
        second = first + LANES // 2
        k_rope[first], k_rope[second] = base + np.arange(half), base + half + np.arange(half)
        k_swap[first], k_swap[second] = base + half + np.arange(half), base + np.arange(half)
    dkv = MLA_NOPE_DIM + HEAD_DIM
    k_nope = np.concatenate([h * dkv + np.arange(MLA_NOPE_DIM) for h in range(MLA_HEADS)])
    v = k_nope + MLA_NOPE_DIM
    down = np.concatenate([np.arange(base), k_rope, k_swap])
    return np.concatenate([q_nope, q_rope, q_swap]), np.concatenate([k_nope, v]), down


def _mla_rope_tables(seq):
    half = MLA_ROPE_DIM // 2
    inv = ROPE_THETA ** (-jnp.arange(0, MLA_ROPE_DIM, 2, dtype=F32) / MLA_ROPE_DIM)
    ang = jnp.arange(seq, dtype=F32)[:, None] * inv[None, :]
    cos, sin = jnp.cos(ang), jnp.sin(ang)
    reps = LANES // 2 // half
    cos_t = jnp.tile(cos, (1, 2 * reps))
    sin_t = jnp.concatenate([jnp.tile(-sin, (1, reps)), jnp.tile(sin, (1, reps))], axis=1)
    return jnp.concatenate([cos_t, sin_t], axis=1)


def _mla_proj(x, g, w_down, q_norm, w_uq, kv_norm, w_ukv, seq):
    t, d = x.shape
    tm = _tile(seq, 512)
    tk = _tile(seq, ATT_TK)
    ns = seq // tm
    uq_cols, ukv_cols, down_cols = _mla_rope_layout()
    wd = w_down[:, down_cols].astype(BF16)
    wuq = w_uq[:, uq_cols].astype(BF16)
    wukv = w_ukv[:, ukv_cols].astype(BF16)
    cs = _mla_rope_tables(seq)
    scale = (MLA_NOPE_DIM + MLA_ROPE_DIM) ** -0.5 * LOG2E
    full = lambda a: pl.BlockSpec(a.shape, lambda i: (0, 0))
    g2, qg, kvg = g.reshape(1, d), q_norm.reshape(1, -1), kv_norm.reshape(1, -1)
    return pl.pallas_call(
        functools.partial(_mla_proj_kernel, scale=scale, tk=tk),
        grid=(t // tm,),
        in_specs=[pl.BlockSpec((tm, d), lambda i: (i, 0)), full(g2), full(wd), full(qg), full(wuq),
                  full(kvg), full(wukv), pl.BlockSpec((tm, 2 * LANES), lambda i: (i % ns, 0))],
        out_specs=[pl.BlockSpec((tm, MLA_OUT), lambda i: (i, 0)),
                   pl.BlockSpec((tm // tk, MLA_QN, tk), lambda i: (i, 0, 0))],
        out_shape=[jax.ShapeDtypeStruct((t, MLA_OUT), BF16),
                   jax.ShapeDtypeStruct((t // tk, MLA_QN, tk), BF16)],
        compiler_params=_params("parallel"),
        name="mla_proj",
    )(x, g2, wd, qg, wuq, kvg, wukv, cs)


MLA_STEP_HEADS = LANES // MLA_ROPE_DIM


def _mla_kernel(qn_ref, qr_ref, kn_ref, kr_ref, vt_ref, o_ref, acc_ref, ml_ref, s_ref, *, tq, tk):
    i = pl.program_id(2)
    heads = MLA_STEP_HEADS
    lane = lax.broadcasted_iota(jnp.int32, (tq, 2 * LANES), 1)
    half = MLA_ROPE_DIM // 2
    qts = []
    for h in range(heads):
        pair, j = divmod(h, 2)
        q = jnp.concatenate([qn_ref[:, pair * LANES:(pair + 1) * LANES], qr_ref[...]], axis=1)
        r0 = LANES + h * half
        sel = jnp.logical_and(lane >= j * HEAD_DIM, lane < (j + 1) * HEAD_DIM)
        sel = jnp.logical_or(sel, jnp.logical_and(lane >= r0, lane < r0 + half))
        r1 = r0 + LANES // 2
        sel = jnp.logical_or(sel, jnp.logical_and(lane >= r1, lane < r1 + half))
        qts.append(_transposed(jnp.where(sel, q, jnp.zeros_like(q))))
    acc_ref[...] = jnp.zeros_like(acc_ref)
    row = lax.broadcasted_iota(jnp.int32, ml_ref.shape, 0)
    ml_ref[...] = jnp.where(row < heads, NEG, 0.0)

    def scores(unit, _):
        jb, h, lanes, _ = unit
        start = pl.multiple_of(jb * tk, tk)
        pair = h // 2
        kt = jnp.concatenate([kn_ref[pl.ds(start, tk), pair * LANES:(pair + 1) * LANES],
                              kr_ref[pl.ds(start, tk), :]], axis=1)
        return _dot(kt, qts[h][:, lanes])

    def softmax(unit, s):
        _, h, lanes, mask = unit
        if mask is not None:
            s = jnp.where(mask, s, NEG)
        m_old = ml_ref[h:h + 1, lanes]
        m_new = jnp.maximum(m_old, jnp.max(s, axis=0, keepdims=True))
        alpha = jnp.exp2(m_old - m_new)
        ml_ref[h:h + 1, lanes] = m_new
        return alpha, jnp.exp2((s - m_new).astype(BF16))

    ones = jnp.ones((16, tk), BF16)

    def values(unit, val):
        jb, h, lanes, _ = unit
        alpha, e = val
        rows = slice(h * HEAD_DIM, (h + 1) * HEAD_DIM)
        both = _dot(jnp.concatenate([vt_ref[jb, rows, :], ones], axis=0), e)
        acc_ref[rows, lanes] = alpha * acc_ref[rows, lanes] + both[:HEAD_DIM]
        l_row = slice(heads + h, heads + h + 1)
        ml_ref[l_row, lanes] = alpha * ml_ref[l_row, lanes] + both[HEAD_DIM:HEAD_DIM + 1]

    _causal_sweep(i, tq, tk, (scores, softmax, values), False, s_ref, heads)
    head = lax.broadcasted_iota(jnp.int32, acc_ref.shape, 0) // HEAD_DIM
    denom = ml_ref[heads:heads + 1, :]
    for h in range(1, heads):
        denom = jnp.where(head == h, ml_ref[heads + h:heads + h + 1, :], denom)
    o_ref[...] = (acc_ref[...] / denom).T.astype(o_ref.dtype)


def _mla_attention(a, vt, batch, seq):
    t = a.shape[0]
    tq = _tile(seq, ATT_TQ)
    tk = vt.shape[2]
    nq = seq // tq
    groups = MLA_HEADS // MLA_STEP_HEADS
    gw = MLA_STEP_HEADS * HEAD_DIM
    qr0 = MLA_QN // LANES
    kn0 = (MLA_QN + MLA_QR) // gw
    kr0 = (MLA_QN + MLA_QR + MLA_QN) // LANES
    return pl.pallas_call(
        functools.partial(_mla_kernel, tq=tq, tk=tk),
        grid=(batch, groups, nq),
        in_specs=[pl.BlockSpec((tq, gw), lambda b, p, i: (b * nq + i, p)),
                  pl.BlockSpec((tq, LANES), lambda b, p, i: (b * nq + i, qr0 + p)),
                  pl.BlockSpec((seq, gw), lambda b, p, i: (b, kn0 + p)),
                  pl.BlockSpec((seq, LANES), lambda b, p, i: (b, kr0)),
                  pl.BlockSpec((seq // tk, gw, tk), lambda b, p, i: (b, p, 0))],
        out_specs=pl.BlockSpec((tq, gw), lambda b, p, i: (b * nq + i, p)),
        out_shape=jax.ShapeDtypeStruct((t, MLA_HEADS * HEAD_DIM), BF16),
        scratch_shapes=[pltpu.VMEM((gw, tq), F32), pltpu.VMEM((2 * MLA_STEP_HEADS, tq), F32),
                        pltpu.VMEM((MLA_STEP_HEADS * (tq // tk), tk, tq), F32)],
        compiler_params=_params("parallel", "parallel", "arbitrary"),
        name="mla_attention",
    )(a, a, a, a, vt)


FFN_CHUNK = 256


def _swiglu_into(acc_ref, h, wg, wu, wd):
    def gate_up(cols, _):
        return _dot(h, wg(cols)), _dot(h, wu(cols))

    def activate(cols, gu):
        g, u = gu
        return (g * jax.nn.sigmoid(g) * u).astype(BF16)

    def down(cols, a):
        acc_ref[...] += _dot(a, wd(cols))

    return gate_up, activate, down


def _run_swiglu(acc_ref, h, wg, wu, wd, width):
    chunks = [slice(c, min(c + FFN_CHUNK, width)) for c in range(0, width, FFN_CHUNK)]
    _staggered(chunks, _swiglu_into(acc_ref, h, wg, wu, wd))


def _ffn_kernel(x_ref, a_ref, wo_ref, g_ref, wg_ref, wu_ref, wd_ref, o_ref, h_ref):
    f = pl.program_id(1)

    @pl.when(f == 0)
    def _():
        x = x_ref[...] + _dot(a_ref[...], wo_ref[...])
        h_ref[...] = _rms(x, g_ref[...]).astype(BF16)
        o_ref[...] = x

    _run_swiglu(o_ref, h_ref[...], lambda c: wg_ref[:, c], lambda c: wu_ref[:, c], lambda c: wd_ref[c, :],
                wg_ref.shape[1])


def _ffn(x, a, wo, g, wg, wu, wd):
    t, d = x.shape
    fdim = wg.shape[1]
    tm = _tile(t, 512)
    tf = fdim
    return pl.pallas_call(
        _ffn_kernel,
        grid=(t // tm, fdim // tf),
        in_specs=[pl.BlockSpec((tm, d), lambda i, f: (i, 0)),
                  pl.BlockSpec((tm, a.shape[1]), lambda i, f: (i, 0)),
                  pl.BlockSpec(wo.shape, lambda i, f: (0, 0)),
                  pl.BlockSpec((1, d), lambda i, f: (0, 0)),
                  pl.BlockSpec((d, tf), lambda i, f: (0, f)),
                  pl.BlockSpec((d, tf), lambda i, f: (0, f)),
                  pl.BlockSpec((tf, d), lambda i, f: (f, 0))],
        out_specs=pl.BlockSpec((tm, d), lambda i, f: (i, 0)),
        out_shape=jax.ShapeDtypeStruct((t, d), F32),
        scratch_shapes=[pltpu.VMEM((tm, d), BF16)],
        compiler_params=_params("parallel", "arbitrary"),
        name="dense_ffn",
    )(x, a, wo, g.reshape(1, d), wg, wu, wd)


MOE_TILE = 512
META_ROWS = 8
ROUTE_SUB = 128
COMBINE_SUB = 256


def _pack_halves(x):
    half = x.shape[1] // 2
    bits = lax.bitcast_convert_type(x.astype(BF16).astype(F32), jnp.uint32)
    return bits[:, :half] | (bits[:, half:] >> 16)


def _unpack_halves(w):
    hi = lax.bitcast_convert_type(w & jnp.uint32(0xFFFF0000), F32)
    lo = lax.bitcast_convert_type(w << 16, F32)
    return jnp.concatenate([hi, lo], axis=1)


def _router_kernel(x_ref, g_ref, rt_ref, su_ref, eye_ref, hb_ref, meta_ref, metac_ref, cnt_ref, carry_ref):
    @pl.when(pl.program_id(0) == 0)
    def _():
        carry_ref[...] = jnp.zeros_like(carry_ref)

    h = _rms(x_ref[...], g_ref[...])
    hb_ref[...] = _pack_halves(h)
    ne = rt_ref.shape[0]
    r = rt_ref[...]
    r_hi = r.astype(BF16).astype(F32)
    h_hi = h.astype(BF16)
    h_lo = (h - h_hi.astype(F32)).astype(BF16)
    both_r = _dot_nt(jnp.concatenate([r_hi, r - r_hi], axis=0).astype(BF16), h_hi)
    logits = both_r[:ne] + both_r[ne:] + _dot_nt(r_hi.astype(BF16), h_lo)
    ne, ct = logits.shape
    eio = lax.broadcasted_iota(jnp.int32, (ne, ct), 0).astype(F32)
    m1 = jnp.max(logits, axis=0, keepdims=True)
    i1 = jnp.min(jnp.where(logits == m1, eio, float(ne)), axis=0, keepdims=True)
    rest = jnp.where(eio == i1, -jnp.inf, logits)
    m2 = jnp.max(rest, axis=0, keepdims=True)
    i2 = jnp.min(jnp.where(rest == m2, eio, float(ne)), axis=0, keepdims=True)
    e2 = jnp.exp(m2 - m1)
    g1 = 1.0 / (1.0 + e2)
    g2 = e2 / (1.0 + e2)
    oh1 = (eio == i1).astype(F32)
    oh2 = (eio == i2).astype(F32)
    both = oh1 + oh2
    seen = carry_ref[:, 0:1] + _dot(both.astype(BF16), su_ref[...])
    r1 = jnp.sum(oh1 * seen, axis=0, keepdims=True)
    r2 = jnp.sum(oh2 * seen, axis=0, keepdims=True)
    meta = jnp.zeros((META_ROWS, ct), F32)
    for row, val in enumerate((i1, i2, r1, r2, g1, g2)):
        meta = jnp.where(eio == float(row), val, meta)
    meta_ref[...] = meta
    metac = jnp.zeros((ct, META_ROWS), F32)
    rest_bits = meta
    for _ in range(3):
        piece = rest_bits.astype(BF16)
        metac = metac + _dot_nt(eye_ref[...], piece)
        rest_bits = rest_bits - piece.astype(F32)
    metac_ref[...] = metac
    carry_ref[...] = carry_ref[...] + jnp.sum(both, axis=1, keepdims=True)
    cnt_ref[0] = carry_ref[...]


def _router(x, g, router):
    t, d = x.shape
    ct = _tile(t, MOE_TILE)
    nc = t // ct
    su = jnp.asarray(np.triu(np.ones((ct, ct), np.float32), 1), BF16)
    eye = jnp.eye(ct, dtype=BF16)
    return pl.pallas_call(
        _router_kernel,
        grid=(nc,),
        in_specs=[pl.BlockSpec((ct, d), lambda i: (i, 0)),
                  pl.BlockSpec((1, d), lambda i: (0, 0)),
                  pl.BlockSpec((N_EXPERTS, d), lambda i: (0, 0)),
                  pl.BlockSpec((ct, ct), lambda i: (0, 0)),
                  pl.BlockSpec((ct, ct), lambda i: (0, 0))],
        out_specs=[pl.BlockSpec((ct, d // 2), lambda i: (i, 0)),
                   pl.BlockSpec((META_ROWS, ct), lambda i: (0, i)),
                   pl.BlockSpec((ct, META_ROWS), lambda i: (i, 0)),
                   pl.BlockSpec((1, N_EXPERTS, LANES), lambda i: (i, 0, 0))],
        out_shape=[jax.ShapeDtypeStruct((t, d // 2), jnp.uint32),
                   jax.ShapeDtypeStruct((META_ROWS, t), F32),
                   jax.ShapeDtypeStruct((t, META_ROWS), F32),
                   jax.ShapeDtypeStruct((nc, N_EXPERTS, LANES), F32)],
        scratch_shapes=[pltpu.VMEM((N_EXPERTS, LANES), F32)],
        compiler_params=_params("arbitrary"),
        name="moe_router",
    )(x, g.reshape(1, d), router.T, su, eye)


def _group_layout(cnt_after, t, rt):
    rows = 2 * t + N_EXPERTS * rt
    counts = cnt_after[:, :, 0].astype(jnp.int32)
    padded = (counts[-1] + rt - 1) // rt * rt
    off_end = jnp.cumsum(padded)
    off = (off_end - padded).astype(jnp.int32)
    tile_start = jnp.arange(rows // rt, dtype=jnp.int32) * rt
    tile_expert = jnp.minimum(_count_le(off_end, tile_start), N_EXPERTS - 1).astype(jnp.int32)
    tile_valid = (tile_start < off_end[-1]).astype(jnp.int32)
    return rows, off, counts, tile_expert, tile_valid


def _count_le(sorted_vals, x):
    return jnp.sum((sorted_vals[None, :] <= x[:, None]).astype(jnp.int32), axis=1)


def _work_lists(counts, off, rows, t, ct):
    nc = t // ct
    nrt = rows // ct
    before = jnp.concatenate([jnp.zeros((1, N_EXPERTS), jnp.int32), counts[:-1]], axis=0)
    chunk_start = (off[:, None] + before.T).reshape(-1)
    tile_start = jnp.arange(nrt, dtype=jnp.int32) * ct
    starts = jnp.sort(jnp.concatenate([chunk_start, tile_start]))
    ends = jnp.concatenate([starts[1:], jnp.array([rows], jnp.int32)])
    item_r = jnp.minimum(starts // ct, nrt - 1)
    item_c = (_count_le(chunk_start, starts) - 1) % nc
    first = lambda key: jnp.concatenate([jnp.ones((1,), jnp.int32), (key[1:] != key[:-1]).astype(jnp.int32)])
    dispatch = (item_r, item_c, starts, ends, first(item_r))
    order = jnp.argsort(item_c * (rows + 1) + starts)
    c2 = item_c[order]
    combine = (c2, item_r[order], starts[order], ends[order], first(c2))
    return dispatch, combine


def _positions(expert, rank, off_ref):
    base = jnp.zeros_like(rank)
    for e in range(N_EXPERTS):
        base = jnp.where(expert == float(e), off_ref[e].astype(F32), base)
    return base + rank


def _dispatch_kernel(r_ref, c_ref, s_ref, e_ref, first_ref, off_ref, meta_ref, h_ref, xs_ref, gate_ref):
    n = pl.program_id(0)
    rt, ct = xs_ref.shape[0], h_ref.shape[0]

    @pl.when(first_ref[n] == 1)
    def _():
        xs_ref[...] = jnp.zeros_like(xs_ref)
        gate_ref[...] = jnp.zeros_like(gate_ref)

    meta = meta_ref[...]
    pos0 = _positions(meta[0:1], meta[2:3], off_ref)
    pos1 = _positions(meta[1:2], meta[3:4], off_ref)
    sub = min(rt, ROUTE_SUB)
    for sb in range(rt // sub):
        base = r_ref[n] * rt + sb * sub

        @pl.when(jnp.logical_and(base < e_ref[n], base + sub > s_ref[n]))
        def _():
            row = lax.broadcasted_iota(jnp.int32, (sub, ct), 0) + base
            live = jnp.logical_and(row >= s_ref[n], row < e_ref[n])
            rowf = row.astype(F32)
            sel0 = jnp.logical_and(pos0 == rowf, live)
            sel1 = jnp.logical_and(pos1 == rowf, live)
            sel = jnp.logical_or(sel0, sel1).astype(F32).astype(BF16)
            rows = slice(sb * sub, (sb + 1) * sub)
            xs_ref[rows, :] += _pack_halves(_dot(sel, _unpack_halves(h_ref[...]).astype(BF16)))
            gate = jnp.where(sel0, meta[4:5], 0.0) + jnp.where(sel1, meta[5:6], 0.0)
            gate_ref[rows, :] += jnp.sum(gate, axis=1, keepdims=True)


def _dispatch(plan, off, meta, hb, rows, ct):
    t, d = hb.shape
    n_items = plan[0].shape[0]
    return pl.pallas_call(
        _dispatch_kernel,
        grid_spec=pltpu.PrefetchScalarGridSpec(
            num_scalar_prefetch=6,
            grid=(n_items,),
            in_specs=[pl.BlockSpec((META_ROWS, ct), lambda n, r, c, *_: (0, c[n])),
                      pl.BlockSpec((ct, d), lambda n, r, c, *_: (c[n], 0))],
            out_specs=[pl.BlockSpec((ct, d), lambda n, r, c, *_: (r[n], 0)),
                       pl.BlockSpec((ct, 1), lambda n, r, c, *_: (r[n], 0))]),
        out_shape=[jax.ShapeDtypeStruct((rows, d), jnp.uint32),
                   jax.ShapeDtypeStruct((rows, 1), F32)],
        compiler_params=_params("arbitrary"),
        name="moe_dispatch",
    )(*plan, off, meta, hb)


def _expert_kernel(te_ref, tv_ref, xs_ref, gate_ref, wg_ref, wu_ref, wd_ref, y_ref, acc_ref):
    r = pl.program_id(0)
    f = pl.program_id(1)

    @pl.when(f == 0)
    def _():
        acc_ref[...] = jnp.zeros_like(acc_ref)

    @pl.when(tv_ref[r] == 1)
    def _():
        _run_swiglu(acc_ref, _unpack_halves(xs_ref[...]).astype(BF16), lambda c: wg_ref[0, :, c],
                    lambda c: wu_ref[0, :, c], lambda c: wd_ref[0, c, :], wg_ref.shape[2])

    @pl.when(f == pl.num_programs(1) - 1)
    def _():
        y_ref[...] = _pack_halves(acc_ref[...] * gate_ref[...])


def _experts(tile_expert, tile_valid, xs, gate, wg, wu, wd, ct):
    rows, dp = xs.shape
    d = 2 * dp
    fdim = wg.shape[2]
    tf = _tile(fdim, 1792)
    nf = fdim // tf

    def wcol(r, f, te, tv):
        return (te[r], 0, jnp.where(tv[r] == 1, f, nf - 1))

    def wrow(r, f, te, tv):
        return (te[r], jnp.where(tv[r] == 1, f, nf - 1), 0)

    return pl.pallas_call(
        _expert_kernel,
        grid_spec=pltpu.PrefetchScalarGridSpec(
            num_scalar_prefetch=2,
            grid=(rows // ct, nf),
            in_specs=[pl.BlockSpec((ct, dp), lambda r, f, te, tv: (r, 0)),
                      pl.BlockSpec((ct, 1), lambda r, f, te, tv: (r, 0)),
                      pl.BlockSpec((1, d, tf), wcol),
                      pl.BlockSpec((1, d, tf), wcol),
                      pl.BlockSpec((1, tf, d), wrow)],
            out_specs=pl.BlockSpec((ct, dp), lambda r, f, te, tv: (r, 0)),
            scratch_shapes=[pltpu.VMEM((ct, d), F32)]),
        out_shape=jax.ShapeDtypeStruct((rows, dp), jnp.uint32),
        compiler_params=_params("arbitrary", "arbitrary"),
        name="moe_experts",
    )(tile_expert, tile_valid, xs, gate, wg, wu, wd)


def _combine_kernel(c_ref, r_ref, s_ref, e_ref, first_ref, off_ref, x_ref, metac_ref, y_ref, o_ref):
    n = pl.program_id(0)
    ct, rt = x_ref.shape[0], y_ref.shape[0]

    @pl.when(first_ref[n] == 1)
    def _():
        o_ref[...] = x_ref[...]

    metac = metac_ref[...]
    pos0 = _positions(metac[:, 0:1], metac[:, 2:3], off_ref)
    pos1 = _positions(metac[:, 1:2], metac[:, 3:4], off_ref)
    sub = min(rt, COMBINE_SUB)
    for sb in range(rt // sub):
        base = r_ref[n] * rt + sb * sub

        @pl.when(jnp.logical_and(base < e_ref[n], base + sub > s_ref[n]))
        def _():
            row = lax.broadcasted_iota(jnp.int32, (ct, sub), 1) + base
            live = jnp.logical_and(row >= s_ref[n], row < e_ref[n])
            rowf = row.astype(F32)
            sel = jnp.logical_and(jnp.logical_or(pos0 == rowf, pos1 == rowf), live).astype(F32).astype(BF16)
            o_ref[...] += _dot(sel, _unpack_halves(y_ref[sb * sub:(sb + 1) * sub, :]).astype(BF16))


def _combine(plan, off, x, metac, y, ct):
    t, d = x.shape
    n_items = plan[0].shape[0]
    return pl.pallas_call(
        _combine_kernel,
        grid_spec=pltpu.PrefetchScalarGridSpec(
            num_scalar_prefetch=6,
            grid=(n_items,),
            in_specs=[pl.BlockSpec((ct, d), lambda n, c, r, *_: (c[n], 0)),
                      pl.BlockSpec((ct, META_ROWS), lambda n, c, r, *_: (c[n], 0)),
                      pl.BlockSpec((ct, d // 2), lambda n, c, r, *_: (r[n], 0))],
            out_specs=pl.BlockSpec((ct, d), lambda n, c, r, *_: (c[n], 0))),
        out_shape=jax.ShapeDtypeStruct((t, d), F32),
        compiler_params=_params("arbitrary"),
        name="moe_combine",
    )(*plan, off, x, metac, y)


def _cast_kernel(w_ref, o_ref):
    o_ref[...] = w_ref[...].astype(o_ref.dtype)


CAST_BLOCK_BYTES = 8 * 1024 * 1024


def _layer_bf16(w, layer):
    shape = w.shape[1:]
    cols = shape[-1]
    rows = math.prod(shape[:-1])
    tr = rows
    while tr * cols * 4 > CAST_BLOCK_BYTES and tr % 16 == 0:
        tr //= 2
    out = pl.pallas_call(
        _cast_kernel,
        grid=(rows // tr,),
        in_specs=[pl.BlockSpec((None, tr, cols), lambda i: (layer, i, 0))],
        out_specs=pl.BlockSpec((tr, cols), lambda i: (i, 0)),
        out_shape=jax.ShapeDtypeStruct((rows, cols), BF16),
        compiler_params=_params("parallel"),
        name="weights_bf16",
    )(w.reshape(w.shape[0], rows, cols))
    return out.reshape(shape)


SC_CORES = 2
SC_SUBCORES = 16
SC_CHUNK = 128


def _sc_worker_chunks(total):
    per = total // (SC_CORES * SC_SUBCORES)
    wid = lax.axis_index("s") * SC_CORES + lax.axis_index("c")
    return wid * per, per // SC_CHUNK


def _sc_mesh():
    return plsc.VectorSubcoreMesh(core_axis_name="c", subcore_axis_name="s")


def _sc_gather(table, idx):
    v, d = table.shape
    b = idx.shape[0]

    def body(table_hbm, idx_hbm, out_hbm, idx_v, rows_v, sem):
        base, chunks = _sc_worker_chunks(b)

        @pl.loop(0, chunks)
        def _(j):
            start = base + j * SC_CHUNK
            pltpu.sync_copy(idx_hbm.at[pl.ds(start, SC_CHUNK)], idx_v)
            pltpu.async_copy(table_hbm.at[idx_v], rows_v, sem).wait()
            pltpu.sync_copy(rows_v, out_hbm.at[pl.ds(start, SC_CHUNK)])

    return pl.kernel(
        body,
        out_type=jax.ShapeDtypeStruct((b, d), table.dtype),
        mesh=_sc_mesh(),
        scratch_types=[pltpu.VMEM((SC_CHUNK,), jnp.int32),
                       pltpu.VMEM((SC_CHUNK, d), table.dtype),
                       pltpu.SemaphoreType.DMA],
        name="sc_row_gather",
    )(table, idx)


def _sc_scatter_rows(src, idx, n):
    t, d = src.shape
    b = idx.shape[0]

    def body(src_hbm, idx_hbm, out_hbm, idx_v, rows_v):
        base, chunks = _sc_worker_chunks(b)

        @pl.loop(0, chunks)
        def _(j):
            start = base + j * SC_CHUNK
            first = lax.rem(start, t)
            pltpu.sync_copy(idx_hbm.at[pl.ds(start, SC_CHUNK)], idx_v)
            pltpu.sync_copy(src_hbm.at[pl.ds(first, SC_CHUNK)], rows_v)
            pltpu.sync_copy(rows_v, out_hbm.at[idx_v])

    return pl.kernel(
        body,
        out_type=jax.ShapeDtypeStruct((n, d), src.dtype),
        mesh=_sc_mesh(),
        scratch_types=[pltpu.VMEM((SC_CHUNK,), jnp.int32), pltpu.VMEM((SC_CHUNK, d), src.dtype)],
        name="sc_row_scatter",
    )(src, idx)


def _weighted_add_kernel(x_ref, metac_ref, y0_ref, y1_ref, g_ref, o_ref, *, normed):
    gates = metac_ref[...]
    out = (x_ref[...] + gates[:, 4:5] * _unpack_halves(y0_ref[...])
           + gates[:, 5:6] * _unpack_halves(y1_ref[...]))
    o_ref[...] = _rms(out, g_ref[...]) if normed else out


def _weighted_add(x, metac, ysel, out_gain=None):
    t, d = x.shape
    tm = _tile(t, 512)
    nt = t // tm
    gain = jnp.ones((d,), F32) if out_gain is None else out_gain
    return pl.pallas_call(
        functools.partial(_weighted_add_kernel, normed=out_gain is not None),
        grid=(nt,),
        in_specs=[pl.BlockSpec((tm, d), lambda i: (i, 0)),
                  pl.BlockSpec((tm, META_ROWS), lambda i: (i, 0)),
                  pl.BlockSpec((tm, d // 2), lambda i: (i, 0)),
                  pl.BlockSpec((tm, d // 2), lambda i: (nt + i, 0)),
                  pl.BlockSpec((1, d), lambda i: (0, 0))],
        out_specs=pl.BlockSpec((tm, d), lambda i: (i, 0)),
        out_shape=jax.ShapeDtypeStruct((t, d), F32),
        compiler_params=_params("parallel"),
        name="moe_weighted_add",
    )(x, metac, ysel, ysel, gain.reshape(1, d))


def _moe(x, g, router, wg, wu, wd, out_gain=None):
    t = x.shape[0]
    ct = _tile(t, MOE_TILE)
    hb, meta, metac, cnt = _router(x, g, router)
    sc_rows = SC_CORES * SC_SUBCORES * SC_CHUNK
    rows, off, counts, tile_expert, tile_valid = _group_layout(cnt, t, ct)
    if (2 * t) % sc_rows == 0 and rows % sc_rows == 0:
        base = sum(jnp.where(meta[0:2] == float(e), off[e], 0) for e in range(N_EXPERTS))
        pos = (base + meta[2:4].astype(jnp.int32)).reshape(-1)
        xs = _sc_scatter_rows(hb, pos, rows)
        y = _experts(tile_expert, tile_valid, xs, jnp.ones((rows, 1), F32), wg, wu, wd, ct)
        return _weighted_add(x, metac, _sc_gather(y, pos), out_gain)
    dispatch, combine = _work_lists(counts, off, rows, t, ct)
    xs, gate = _dispatch(dispatch, off, meta, hb, rows, ct)
    y = _experts(tile_expert, tile_valid, xs, gate, wg, wu, wd, ct)
    out = _combine(combine, off, x, metac, y, ct)
    return out if out_gain is None else _final_norm(out, out_gain)


def _final_norm_kernel(x_ref, g_ref, o_ref):
    o_ref[...] = _rms(x_ref[...], g_ref[...])


def _final_norm(x, g):
    t, d = x.shape
    tm = _tile(t, 1024)
    return pl.pallas_call(
        _final_norm_kernel,
        grid=(t // tm,),
        in_specs=[pl.BlockSpec((tm, d), lambda i: (i, 0)), pl.BlockSpec((1, d), lambda i: (0, 0))],
        out_specs=pl.BlockSpec((tm, d), lambda i: (i, 0)),
        out_shape=jax.ShapeDtypeStruct((t, d), F32),
        compiler_params=_params("parallel"),
        name="final_norm",
    )(x, g.reshape(1, d))


def kernel(x, rel_bias, attn_norm, ffn_norm, final_norm, swa_w_qkv, swa_b_qkv, swa_sinks, swa_w_o, sb_w_qkv, sb_w_o, mla_w_down, mla_q_norm, mla_w_uq, mla_kv_norm, mla_w_ukv, mla_w_o, ffn_w_gate, ffn_w_up, ffn_w_down, moe_router, moe_w_gate, moe_w_up, moe_w_down):
    batch, seq, d = x.shape
    depth = attn_norm.shape[0]
    xt = x.reshape(batch * seq, d)
    swa_bias = _swa_bias(rel_bias)
    for i in range(depth):
        mixer, j = i % N_MIXERS, i // N_MIXERS
        if mixer == 0:
            w, b, cs = _swa_weights(swa_w_qkv[j], swa_b_qkv[j])
            qkv = _norm_proj(xt, attn_norm[i], w, b, cs, "swa_proj")
            o = _swa_attention(qkv, swa_bias, swa_sinks[j] * LOG2E, batch, seq)
            wo = swa_w_o[j].astype(BF16)
        elif mixer == 1:
            n = sb_w_qkv.shape[2] // 3
            cs = jnp.concatenate([jnp.full((n,), HEAD_DIM ** -0.5 * LOG2E, F32), jnp.ones((n,), F32)])
            qk, vt = _norm_proj_vt(xt, attn_norm[i], sb_w_qkv[j].astype(BF16), cs, n, seq, "sb_proj")
            o = _sb_attention(qk, vt, batch, seq)
            wo = sb_w_o[j].astype(BF16)
        else:
            a, vt = _mla_proj(xt, attn_norm[i], mla_w_down[j], mla_q_norm[j], mla_w_uq[j],
                              mla_kv_norm[j], mla_w_ukv[j], seq)
            o = _mla_attention(a, vt, batch, seq)
            wo = mla_w_o[j].astype(BF16)
        f = i // 2
        if i % 2 == 0:
            xt = _ffn(xt, o, wo, ffn_norm[i], _layer_bf16(ffn_w_gate, f), _layer_bf16(ffn_w_up, f),
                      _layer_bf16(ffn_w_down, f))
        else:
            last = final_norm if i == depth - 1 else None
            xt = _out_proj(xt, o, wo, "mixer_out")
            xt = _moe(xt, ffn_norm[i], moe_router[f], _layer_bf16(moe_w_gate, f), _layer_bf16(moe_w_up, f),
                      _layer_bf16(moe_w_down, f), last)
    if depth % 2 == 1:
        xt = _final_norm(xt, final_norm)
    return xt.reshape(batch, seq, d)
```

```python
import functools
import math

import numpy as np
import jax
import jax.numpy as jnp
from jax import lax
from jax.experimental import pallas as pl
from jax.experimental.pallas import tpu as pltpu
from jax.experimental.pallas import tpu_sc as plsc

F32 = jnp.float32
BF16 = jnp.bfloat16

N_MIXERS = 3
RMS_EPS = 1e-6
SWA_HEADS = 16
SWA_KV_HEADS = 2
HEAD_DIM = 64
SWA_BLOCK = 128
SWA_WINDOW = 128
REL_BUCKETS = 32
REL_MAX_DIST = 128
SB_HEADS = 16
MLA_HEADS = 16
MLA_NOPE_DIM = 64
MLA_ROPE_DIM = 32
MLA_Q_RANK = 384
MLA_KV_RANK = 256
ROPE_THETA = 10000.0
N_EXPERTS = 8

LANES = 128
NEG = -1e30
LOG2E = math.log2(math.e)
VMEM_LIMIT = 56 * 1024 * 1024

_NT = (((1,), (1,)), ((), ()))


def _dot(a, b):
    return jnp.dot(a, b, preferred_element_type=F32)


def _dot_nt(a, b):
    return lax.dot_general(a, b, _NT, preferred_element_type=F32)


def _rms(x, g):
    return x * lax.rsqrt(jnp.mean(x * x, axis=-1, keepdims=True) + RMS_EPS) * g


def _params(*sem):
    return pltpu.CompilerParams(dimension_semantics=sem, vmem_limit_bytes=VMEM_LIMIT)


def _tile(n, pref):
    return pref if n % pref == 0 else n


def _norm_proj_kernel(x_ref, g_ref, w_ref, b_ref, cs_ref, o_ref):
    h = _rms(x_ref[...], g_ref[...]).astype(BF16)
    y = (_dot(h, w_ref[...]) + b_ref[...]) * cs_ref[...]
    o_ref[...] = y.astype(o_ref.dtype)


def _norm_proj(x, g, w, b, colscale, name):
    t, d = x.shape
    n = w.shape[1]
    tm = _tile(t, 512)
    return pl.pallas_call(
        _norm_proj_kernel,
        grid=(t // tm,),
        in_specs=[pl.BlockSpec((tm, d), lambda i: (i, 0)),
                  pl.BlockSpec((1, d), lambda i: (0, 0)),
                  pl.BlockSpec((d, n), lambda i: (0, 0)),
                  pl.BlockSpec((1, n), lambda i: (0, 0)),
                  pl.BlockSpec((1, n), lambda i: (0, 0))],
        out_specs=pl.BlockSpec((tm, n), lambda i: (i, 0)),
        out_shape=jax.ShapeDtypeStruct((t, n), BF16),
        compiler_params=_params("parallel"),
        name=name,
    )(x, g.reshape(1, d), w, b.reshape(1, n), colscale.reshape(1, n))


ATT_TQ = 512
ATT_TK = 256


def _store_vt(vt_ref, v, tk):
    for s in range(v.shape[0] // tk):
        vt_ref[s] = v[s * tk:(s + 1) * tk, :].T.astype(vt_ref.dtype)


def _norm_proj_vt_kernel(x_ref, g_ref, w_ref, cs_ref, o_ref, vt_ref, *, tk):
    h = _rms(x_ref[...], g_ref[...]).astype(BF16)
    y = _dot(h, w_ref[...])
    n = o_ref.shape[1]
    o_ref[...] = (y[:, :n] * cs_ref[...]).astype(o_ref.dtype)
    _store_vt(vt_ref, y[:, n:], tk)


def _norm_proj_vt(x, g, w, colscale, nv, seq, name):
    t, d = x.shape
    n = w.shape[1] - nv
    tm = _tile(seq, 512)
    tk = _tile(seq, ATT_TK)
    per = tm // tk
    return pl.pallas_call(
        functools.partial(_norm_proj_vt_kernel, tk=tk),
        grid=(t // tm,),
        in_specs=[pl.BlockSpec((tm, d), lambda i: (i, 0)),
                  pl.BlockSpec((1, d), lambda i: (0, 0)),
                  pl.BlockSpec((d, n + nv), lambda i: (0, 0)),
                  pl.BlockSpec((1, n), lambda i: (0, 0))],
        out_specs=[pl.BlockSpec((tm, n), lambda i: (i, 0)),
                   pl.BlockSpec((per, nv, tk), lambda i: (i, 0, 0))],
        out_shape=[jax.ShapeDtypeStruct((t, n), BF16),
                   jax.ShapeDtypeStruct((t // tk, nv, tk), BF16)],
        compiler_params=_params("parallel"),
        name=name,
    )(x, g.reshape(1, d), w, colscale.reshape(1, n))


def _out_proj_kernel(x_ref, o_ref, w_ref, y_ref):
    y_ref[...] = x_ref[...] + _dot(o_ref[...], w_ref[...])


def _out_proj(x, o, w, name):
    t, d = x.shape
    k = o.shape[1]
    tm = _tile(t, 512)
    return pl.pallas_call(
        _out_proj_kernel,
        grid=(t // tm,),
        in_specs=[pl.BlockSpec((tm, d), lambda i: (i, 0)),
                  pl.BlockSpec((tm, k), lambda i: (i, 0)),
                  pl.BlockSpec((k, d), lambda i: (0, 0))],
        out_specs=pl.BlockSpec((tm, d), lambda i: (i, 0)),
        out_shape=jax.ShapeDtypeStruct((t, d), F32),
        compiler_params=_params("parallel"),
        name=name,
    )(x, o, w)


def _t5_bucket_table():
    qi = np.arange(SWA_BLOCK)[:, None]
    kj = np.arange(2 * SWA_BLOCK)[None, :]
    dist = qi + SWA_BLOCK - kj
    d0 = np.maximum(dist, 0)
    max_exact = REL_BUCKETS // 2
    d = np.maximum(d0, 1).astype(np.float32)
    large = max_exact + (np.log(d / max_exact) / math.log(REL_MAX_DIST / max_exact)
                         * (REL_BUCKETS - max_exact)).astype(np.int32)
    large = np.minimum(large, REL_BUCKETS - 1)
    bucket = np.where(d0 < max_exact, d0, large)
    band = (dist >= 0) & (dist < SWA_WINDOW)
    return np.where(band, bucket, -1).astype(np.int32)


def _swa_bias_kernel(rel_ref, bucket_ref, o_ref):
    h = pl.program_id(0)
    bucket = bucket_ref[...]
    acc = jnp.full(bucket.shape, NEG, F32)
    for b in range(REL_BUCKETS):
        acc = jnp.where(bucket == b, rel_ref[b, h] * LOG2E, acc)
    o_ref[0] = acc


def _swa_bias(rel_bias):
    bucket = jnp.asarray(_t5_bucket_table())
    q, k = bucket.shape
    return pl.pallas_call(
        _swa_bias_kernel,
        grid=(SWA_HEADS,),
        in_specs=[pl.BlockSpec(memory_space=pltpu.SMEM),
                  pl.BlockSpec((q, k), lambda h: (0, 0))],
        out_specs=pl.BlockSpec((1, q, k), lambda h: (h, 0, 0)),
        out_shape=jax.ShapeDtypeStruct((SWA_HEADS, q, k), F32),
        compiler_params=_params("arbitrary"),
        name="swa_bias",
    )(rel_bias, bucket)


def _swa_kernel(sink_ref, q_ref, kc_ref, kp_ref, vc_ref, vp_ref, bias_ref, o_ref, *, tq):
    i = pl.program_id(1)
    g = pl.program_id(2)
    blk = SWA_BLOCK
    heads = bias_ref.shape[0]
    lo_half = lax.broadcasted_iota(jnp.int32, (blk, LANES), 1) < HEAD_DIM
    in_prev = lax.broadcasted_iota(jnp.int32, (heads * blk, 2 * blk), 1) < blk
    bias = bias_ref[...].reshape(heads * blk, 2 * blk)

    def window(cur_ref, prev_ref, sb):
        if sb == 0:
            return jnp.concatenate([prev_ref[...], cur_ref[0:blk, :]], axis=0)
        return cur_ref[(sb - 1) * blk:(sb + 1) * blk, :]

    def scores(sb, _):
        q = q_ref[sb * blk:(sb + 1) * blk, :]
        parts = []
        for h in range(heads):
            qp = q[:, (h // 2) * LANES:(h // 2 + 1) * LANES]
            keep = lo_half if h % 2 == 0 else jnp.logical_not(lo_half)
            parts.append(jnp.where(keep, qp, jnp.zeros_like(qp)))
        s = _dot_nt(jnp.concatenate(parts, axis=0), window(kc_ref, kp_ref, sb)) + bias
        if sb == 0:
            s = jnp.where(jnp.logical_and(in_prev, i == 0), NEG, s)
        return s

    def softmax(sb, s):
        ps, inv = [], []
        for h in range(heads):
            sh = s[h * blk:(h + 1) * blk]
            sink = sink_ref[g * heads + h]
            m = jnp.maximum(jnp.max(sh, axis=1, keepdims=True), sink)
            e = jnp.exp2(sh - m)
            inv.append(1.0 / (jnp.sum(e, axis=1, keepdims=True) + jnp.exp2(sink - m)))
            ps.append(e.astype(BF16))
        return jnp.concatenate(ps, axis=0), inv

    def values(sb, val):
        p, inv = val
        o = _dot(p, window(vc_ref, vp_ref, sb))
        outs = [jnp.where(lo_half, o[h * blk:(h + 1) * blk] * inv[h], o[(h + 1) * blk:(h + 2) * blk] * inv[h + 1])
                for h in range(0, heads, 2)]
        o_ref[sb * blk:(sb + 1) * blk, :] = jnp.concatenate(outs, axis=1).astype(o_ref.dtype)

    _staggered(list(range(tq // blk)), (scores, softmax, values))


def _swa_attention(qkv, bias, sinks, batch, seq):
    t = qkv.shape[0]
    tq = _tile(seq, 512)
    nt = seq // tq
    group = SWA_HEADS // SWA_KV_HEADS
    gw = group * HEAD_DIM
    kcol = SWA_HEADS * HEAD_DIM // LANES
    vcol = kcol + SWA_KV_HEADS
    per = tq // SWA_BLOCK

    def prev_idx(b, i, g):
        return jnp.maximum(b * (seq // SWA_BLOCK) + i * per - 1, 0)

    return pl.pallas_call(
        functools.partial(_swa_kernel, tq=tq),
        grid=(batch, nt, SWA_KV_HEADS),
        in_specs=[pl.BlockSpec(memory_space=pltpu.SMEM),
                  pl.BlockSpec((tq, gw), lambda b, i, g: (b * nt + i, g)),
                  pl.BlockSpec((tq, LANES), lambda b, i, g: (b * nt + i, kcol + g)),
                  pl.BlockSpec((SWA_BLOCK, LANES), lambda b, i, g: (prev_idx(b, i, g), kcol + g)),
                  pl.BlockSpec((tq, LANES), lambda b, i, g: (b * nt + i, vcol + g)),
                  pl.BlockSpec((SWA_BLOCK, LANES), lambda b, i, g: (prev_idx(b, i, g), vcol + g)),
                  pl.BlockSpec((group, SWA_BLOCK, 2 * SWA_BLOCK), lambda b, i, g: (g, 0, 0))],
        out_specs=pl.BlockSpec((tq, gw), lambda b, i, g: (b * nt + i, g)),
        out_shape=jax.ShapeDtypeStruct((t, SWA_HEADS * HEAD_DIM), BF16),
        compiler_params=_params("parallel", "parallel", "arbitrary"),
        name="swa_attention",
    )(sinks, qkv, qkv, qkv, qkv, qkv, bias)


def _swa_weights(w_qkv, b_qkv):
    nq = SWA_HEADS * HEAD_DIM
    kv = SWA_KV_HEADS * HEAD_DIM
    dup = np.concatenate([np.tile(np.arange(HEAD_DIM), 2) + h * HEAD_DIM for h in range(SWA_KV_HEADS)])
    cols = np.concatenate([np.arange(nq), nq + dup, nq + kv + dup])
    scale = np.concatenate([np.full(nq, HEAD_DIM ** -0.5 * LOG2E), np.ones(2 * dup.size)]).astype(np.float32)
    return w_qkv[:, cols].astype(BF16), b_qkv[cols], jnp.asarray(scale)


def _staggered(units, stages):
    vals = list(units)
    for t in range(len(units) + len(stages) - 1):
        for k, stage in enumerate(stages):
            u = t - k
            if 0 <= u < len(units):
                vals[u] = stage(units[u], vals[u])


def _score_ahead(units, next_units, stages, s_ref):
    rest = stages[1:]
    vals = [None] * len(units)
    for t in range(len(units) + len(rest) - 1):
        ahead = stages[0](next_units[t], None) if next_units is not None and t < len(units) else None
        for k, stage in enumerate(rest):
            u = t - k
            if 0 <= u < len(units):
                if k == 0:
                    lanes = units[u][2]
                    vals[u] = stage(units[u], s_ref[u, :, :lanes.stop - lanes.start])
                    if ahead is not None:
                        s_ref[u] = ahead
                else:
                    vals[u] = stage(units[u], vals[u])


def _causal_sweep(i, tq, tk, stages, strict, s_ref, heads=2):
    nsub = tq // tk

    def group(j):
        return [(j * nsub + sb, h, slice(0, tq), None) for sb in reversed(range(nsub)) for h in range(heads)]

    def sweep(has_history):
        diagonal = []
        for sb in reversed(range(nsub)):
            nl = tq - sb * tk
            key = lax.broadcasted_iota(jnp.int32, (tk, nl), 0)
            qry = lax.broadcasted_iota(jnp.int32, (tk, nl), 1)
            mask = key < qry if strict else key <= qry
            diagonal += [(i * nsub + sb, h, slice(sb * tk, tq), mask) for h in range(heads)]
        for u, unit in enumerate(diagonal):
            s_ref[u, :, :tq - unit[2].start] = stages[0](unit, None)
        if not has_history:
            _score_ahead(diagonal, None, stages, s_ref)
            return
        _score_ahead(diagonal, group(i - 1), stages, s_ref)

        def body(n, carry):
            _score_ahead(group(i - 1 - n), group(i - 2 - n), stages, s_ref)
            return carry

        lax.fori_loop(0, i - 1, body, 0)
        _score_ahead(group(0), None, stages, s_ref)

    pl.when(i == 0)(lambda: sweep(False))
    pl.when(i > 0)(lambda: sweep(True))


def _transposed(x):
    return x.astype(F32).T.astype(x.dtype)


def _split_pair(q):
    lo_half = lax.broadcasted_iota(jnp.int32, q.shape, 1) < HEAD_DIM
    zero = jnp.zeros_like(q)
    return jnp.where(lo_half, q, zero), jnp.where(lo_half, zero, q)


SB_STEP_HEADS = 4


def _sb_kernel(q_ref, k_ref, vt_ref, u_ref, o_ref, acc_ref, c_ref, s_ref, *, tq, tk, heads):
    i = pl.program_id(2)
    qts = [_transposed(qm) for pair in range(heads // 2)
           for qm in _split_pair(q_ref[:, pair * LANES:(pair + 1) * LANES])]
    u = u_ref[...]
    acc_ref[...] = jnp.zeros_like(acc_ref)
    c_ref[...] = jnp.zeros_like(c_ref)

    def scores(unit, _):
        jb, h, lanes, _ = unit
        kt = k_ref[pl.ds(pl.multiple_of(jb * tk, tk), tk), (h // 2) * LANES:(h // 2 + 1) * LANES]
        return _dot(kt, qts[h][:, lanes])

    def log_fail(unit, z):
        neg_abs = lax.bitcast_convert_type(lax.bitcast_convert_type(z, jnp.uint32) | jnp.uint32(1 << 31), F32)
        sp = jnp.maximum(z, 0.0) + jnp.log2(1.0 + jnp.exp2(neg_abs))
        if unit[3] is not None:
            sp = jnp.where(unit[3], sp, 0.0)
        return z, sp.astype(BF16)

    def suffix_sum(unit, val):
        z, sp = val
        return z, _dot(u, sp)

    def weights(unit, val):
        _, h, lanes, mask = unit
        z, r = val
        c = c_ref[h:h + 1, lanes]
        la = z - r - c
        if mask is not None:
            la = jnp.where(mask, la, NEG)
        c_ref[h:h + 1, lanes] = c + r[0:1, :]
        return jnp.exp2(la.astype(BF16))

    def values(unit, a):
        jb, h, lanes, _ = unit
        rows = slice(h * HEAD_DIM, (h + 1) * HEAD_DIM)
        acc_ref[rows, lanes] += _dot(vt_ref[jb, rows, :], a)

    _causal_sweep(i, tq, tk, (scores, log_fail, suffix_sum, weights, values), True, s_ref, heads)
    o_ref[...] = acc_ref[...].T.astype(o_ref.dtype)


def _sb_attention(qk, vt, batch, seq):
    t = qk.shape[0]
    tq = _tile(seq, ATT_TQ)
    tk = vt.shape[2]
    nq = seq // tq
    groups = SB_HEADS // SB_STEP_HEADS
    gw = SB_STEP_HEADS * HEAD_DIM
    u = jnp.asarray(np.triu(np.ones((tk, tk), np.float32)), BF16)
    return pl.pallas_call(
        functools.partial(_sb_kernel, tq=tq, tk=tk, heads=SB_STEP_HEADS),
        grid=(batch, groups, nq),
        in_specs=[pl.BlockSpec((tq, gw), lambda b, p, i: (b * nq + i, p)),
                  pl.BlockSpec((seq, gw), lambda b, p, i: (b, groups + p)),
                  pl.BlockSpec((seq // tk, gw, tk), lambda b, p, i: (b, p, 0)),
                  pl.BlockSpec((tk, tk), lambda b, p, i: (0, 0))],
        out_specs=pl.BlockSpec((tq, gw), lambda b, p, i: (b * nq + i, p)),
        out_shape=jax.ShapeDtypeStruct((t, SB_HEADS * HEAD_DIM), BF16),
        scratch_shapes=[pltpu.VMEM((gw, tq), F32), pltpu.VMEM((8, tq), F32),
                        pltpu.VMEM((SB_STEP_HEADS * (tq // tk), tk, tq), F32)],
        compiler_params=_params("parallel", "parallel", "arbitrary"),
        name="sb_attention",
    )(qk, qk, vt, u)


MLA_QN = MLA_HEADS * MLA_NOPE_DIM
MLA_QR = MLA_HEADS * MLA_ROPE_DIM
MLA_OUT = 2 * MLA_QN + MLA_QR + LANES


def _mla_proj_kernel(x_ref, g_ref, wd_ref, qg_ref, wuq_ref, kvg_ref, wukv_ref, cs_ref, o_ref, vt_ref,
                     *, scale, tk):
    h = _rms(x_ref[...], g_ref[...]).astype(BF16)
    c = _dot(h, wd_ref[...])
    cos = cs_ref[:, :LANES]
    sin = cs_ref[:, LANES:]
    cq = _rms(c[:, :MLA_Q_RANK], qg_ref[...]).astype(BF16)
    q = _dot(cq, wuq_ref[...])
    o_ref[:, :MLA_QN] = (q[:, :MLA_QN] * scale).astype(o_ref.dtype)
    for m in range(MLA_QR // LANES):
        a = MLA_QN + m * LANES
        rot = q[:, a:a + LANES] * cos + q[:, a + MLA_QR:a + MLA_QR + LANES] * sin
        o_ref[:, a:a + LANES] = (rot * scale).astype(o_ref.dtype)
    kv0 = MLA_Q_RANK + MLA_KV_RANK
    ckv = _rms(c[:, MLA_Q_RANK:kv0], kvg_ref[...]).astype(BF16)
    kv = _dot(ckv, wukv_ref[...])
    kn0 = MLA_QN + MLA_QR
    o_ref[:, kn0:kn0 + MLA_QN] = kv[:, :MLA_QN].astype(o_ref.dtype)
    kr = c[:, kv0:kv0 + LANES] * cos + c[:, kv0 + LANES:kv0 + 2 * LANES] * sin
    o_ref[:, MLA_OUT - LANES:] = kr.astype(o_ref.dtype)
    _store_vt(vt_ref, kv[:, MLA_QN:], tk)


def _mla_rope_layout():
    half = MLA_ROPE_DIM // 2
    per = LANES // MLA_ROPE_DIM
    dq = MLA_NOPE_DIM + MLA_ROPE_DIM
    q_nope = np.concatenate([h * dq + np.arange(MLA_NOPE_DIM) for h in range(MLA_HEADS)])
    q_rope = np.zeros(MLA_QR, np.int64)
    q_swap = np.zeros(MLA_QR, np.int64)
    for h in range(MLA_HEADS):
        m, r = divmod(h, per)
        x1 = h * dq + MLA_NOPE_DIM + np.arange(half)
        x2 = x1 + half
        first = m * LANES + r * half + np.arange(half)
        second = first + LANES // 2
        q_rope[first], q_rope[second] = x1, x2
        q_swap[first], q_swap[second] = x2, x1
    k_rope = np.zeros(LANES, np.int64)
    k_swap = np.zeros(LANES, np.int64)
    base = MLA_Q_RANK + MLA_KV_RANK
    for r in range(per):
        first = r * half + np.arange(half)
        second = first + LANES // 2
        k_rope[first], k_rope[second] = base + np.arange(half), base + half + np.arange(half)
        k_swap[first], k_swap[second] = base + half + np.arange(half), base + np.arange(half)
    dkv = MLA_NOPE_DIM + HEAD_DIM
    k_nope = np.concatenate([h * dkv + np.arange(MLA_NOPE_DIM) for h in range(MLA_HEADS)])
    v = k_nope + MLA_NOPE_DIM
    down = np.concatenate([np.arange(base), k_rope, k_swap])
    return np.concatenate([q_nope, q_rope, q_swap]), np.concatenate([k_nope, v]), down


def _mla_rope_tables(seq):
    half = MLA_ROPE_DIM // 2
    inv = ROPE_THETA ** (-jnp.arange(0, MLA_ROPE_DIM, 2, dtype=F32) / MLA_ROPE_DIM)
    ang = jnp.arange(seq, dtype=F32)[:, None] * inv[None, :]
    cos, sin = jnp.cos(ang), jnp.sin(ang)
    reps = LANES // 2 // half
    cos_t = jnp.tile(cos, (1, 2 * reps))
    sin_t = jnp.concatenate([jnp.tile(-sin, (1, reps)), jnp.tile(sin, (1, reps))], axis=1)
    return jnp.concatenate([cos_t, sin_t], axis=1)


def _mla_proj(x, g, w_down, q_norm, w_uq, kv_norm, w_ukv, seq):
    t, d = x.shape
    tm = _tile(seq, 512)
    tk = _tile(seq, ATT_TK)
    ns = seq // tm
    uq_cols, ukv_cols, down_cols = _mla_rope_layout()
    wd = w_down[:, down_cols].astype(BF16)
    wuq = w_uq[:, uq_cols].astype(BF16)
    wukv = w_ukv[:, ukv_cols].astype(BF16)
    cs = _mla_rope_tables(seq)
    scale = (MLA_NOPE_DIM + MLA_ROPE_DIM) ** -0.5 * LOG2E
    full = lambda a: pl.BlockSpec(a.shape, lambda i: (0, 0))
    g2, qg, kvg = g.reshape(1, d), q_norm.reshape(1, -1), kv_norm.reshape(1, -1)
    return pl.pallas_call(
        functools.partial(_mla_proj_kernel, scale=scale, tk=tk),
        grid=(t // tm,),
        in_specs=[pl.BlockSpec((tm, d), lambda i: (i, 0)), full(g2), full(wd), full(qg), full(wuq),
                  full(kvg), full(wukv), pl.BlockSpec((tm, 2 * LANES), lambda i: (i % ns, 0))],
        out_specs=[pl.BlockSpec((tm, MLA_OUT), lambda i: (i, 0)),
                   pl.BlockSpec((tm // tk, MLA_QN, tk), lambda i: (i, 0, 0))],
        out_shape=[jax.ShapeDtypeStruct((t, MLA_OUT), BF16),
                   jax.ShapeDtypeStruct((t // tk, MLA_QN, tk), BF16)],
        compiler_params=_params("parallel"),
        name="mla_proj",
    )(x, g2, wd, qg, wuq, kvg, wukv, cs)


MLA_STEP_HEADS = LANES // MLA_ROPE_DIM


def _mla_kernel(qn_ref, qr_ref, kn_ref, kr_ref, vt_ref, o_ref, acc_ref, ml_ref, s_ref, *, tq, tk):
    i = pl.program_id(2)
    heads = MLA_STEP_HEADS
    lane = lax.broadcasted_iota(jnp.int32, (tq, 2 * LANES), 1)
    half = MLA_ROPE_DIM // 2
    qts = []
    for h in range(heads):
        pair, j = divmod(h, 2)
        q = jnp.concatenate([qn_ref[:, pair * LANES:(pair + 1) * LANES], qr_ref[...]], axis=1)
        r0 = LANES + h * half
        sel = jnp.logical_and(lane >= j * HEAD_DIM, lane < (j + 1) * HEAD_DIM)
        sel = jnp.logical_or(sel, jnp.logical_and(lane >= r0, lane < r0 + half))
        r1 = r0 + LANES // 2
        sel = jnp.logical_or(sel, jnp.logical_and(lane >= r1, lane < r1 + half))
        qts.append(_transposed(jnp.where(sel, q, jnp.zeros_like(q))))
    acc_ref[...] = jnp.zeros_like(acc_ref)
    row = lax.broadcasted_iota(jnp.int32, ml_ref.shape, 0)
    ml_ref[...] = jnp.where(row < heads, NEG, 0.0)

    def scores(unit, _):
        jb, h, lanes, _ = unit
        start = pl.multiple_of(jb * tk, tk)
        pair = h // 2
        kt = jnp.concatenate([kn_ref[pl.ds(start, tk), pair * LANES:(pair + 1) * LANES],
                              kr_ref[pl.ds(start, tk), :]], axis=1)
        return _dot(kt, qts[h][:, lanes])

    def softmax(unit, s):
        _, h, lanes, mask = unit
        if mask is not None:
            s = jnp.where(mask, s, NEG)
        m_old = ml_ref[h:h + 1, lanes]
        m_new = jnp.maximum(m_old, jnp.max(s, axis=0, keepdims=True))
        alpha = jnp.exp2(m_old - m_new)
        ml_ref[h:h + 1, lanes] = m_new
        return alpha, jnp.exp2((s - m_new).astype(BF16))

    ones = jnp.ones((16, tk), BF16)

    def values(unit, val):
        jb, h, lanes, _ = unit
        alpha, e = val
        rows = slice(h * HEAD_DIM, (h + 1) * HEAD_DIM)
        both = _dot(jnp.concatenate([vt_ref[jb, rows, :], ones], axis=0), e)
        acc_ref[rows, lanes] = alpha * acc_ref[rows, lanes] + both[:HEAD_DIM]
        l_row = slice(heads + h, heads + h + 1)
        ml_ref[l_row, lanes] = alpha * ml_ref[l_row, lanes] + both[HEAD_DIM:HEAD_DIM + 1]

    _causal_sweep(i, tq, tk, (scores, softmax, values), False, s_ref, heads)
    head = lax.broadcasted_iota(jnp.int32, acc_ref.shape, 0) // HEAD_DIM
    denom = ml_ref[heads:heads + 1, :]
    for h in range(1, heads):
        denom = jnp.where(head == h, ml_ref[heads + h:heads + h + 1, :], denom)
    o_ref[...] = (acc_ref[...] / denom).T.astype(o_ref.dtype)


def _mla_attention(a, vt, batch, seq):
    t = a.shape[0]
    tq = _tile(seq, ATT_TQ)
    tk = vt.shape[2]
    nq = seq // tq
    groups = MLA_HEADS // MLA_STEP_HEADS
    gw = MLA_STEP_HEADS * HEAD_DIM
    qr0 = MLA_QN // LANES
    kn0 = (MLA_QN + MLA_QR) // gw
    kr0 = (MLA_QN + MLA_QR + MLA_QN) // LANES
    return pl.pallas_call(
        functools.partial(_mla_kernel, tq=tq, tk=tk),
        grid=(batch, groups, nq),
        in_specs=[pl.BlockSpec((tq, gw), lambda b, p, i: (b * nq + i, p)),
                  pl.BlockSpec((tq, LANES), lambda b, p, i: (b * nq + i, qr0 + p)),
                  pl.BlockSpec((seq, gw), lambda b, p, i: (b, kn0 + p)),
                  pl.BlockSpec((seq, LANES), lambda b, p, i: (b, kr0)),
                  pl.BlockSpec((seq // tk, gw, tk), lambda b, p, i: (b, p, 0))],
        out_specs=pl.BlockSpec((tq, gw), lambda b, p, i: (b * nq + i, p)),
        out_shape=jax.ShapeDtypeStruct((t, MLA_HEADS * HEAD_DIM), BF16),
        scratch_shapes=[pltpu.VMEM((gw, tq), F32), pltpu.VMEM((2 * MLA_STEP_HEADS, tq), F32),
                        pltpu.VMEM((MLA_STEP_HEADS * (tq // tk), tk, tq), F32)],
        compiler_params=_params("parallel", "parallel", "arbitrary"),
        name="mla_attention",
    )(a, a, a, a, vt)


FFN_CHUNK = 256


def _swiglu_into(acc_ref, h, wg, wu, wd):
    def gate_up(cols, _):
        return _dot(h, wg(cols)), _dot(h, wu(cols))

    def activate(cols, gu):
        g, u = gu
        return (g * jax.nn.sigmoid(g) * u).astype(BF16)

    def down(cols, a):
        acc_ref[...] += _dot(a, wd(cols))

    return gate_up, activate, down


def _run_swiglu(acc_ref, h, wg, wu, wd, width):
    chunks = [slice(c, min(c + FFN_CHUNK, width)) for c in range(0, width, FFN_CHUNK)]
    _staggered(chunks, _swiglu_into(acc_ref, h, wg, wu, wd))


def _ffn_kernel(x_ref, a_ref, wo_ref, g_ref, wg_ref, wu_ref, wd_ref, o_ref, h_ref):
    f = pl.program_id(1)

    @pl.when(f == 0)
    def _():
        x = x_ref[...] + _dot(a_ref[...], wo_ref[...])
        h_ref[...] = _rms(x, g_ref[...]).astype(BF16)
        o_ref[...] = x

    _run_swiglu(o_ref, h_ref[...], lambda c: wg_ref[:, c], lambda c: wu_ref[:, c], lambda c: wd_ref[c, :],
                wg_ref.shape[1])


def _ffn(x, a, wo, g, wg, wu, wd):
    t, d = x.shape
    fdim = wg.shape[1]
    tm = _tile(t, 512)
    tf = fdim
    return pl.pallas_call(
        _ffn_kernel,
        grid=(t // tm, fdim // tf),
        in_specs=[pl.BlockSpec((tm, d), lambda i, f: (i, 0)),
                  pl.BlockSpec((tm, a.shape[1]), lambda i, f: (i, 0)),
                  pl.BlockSpec(wo.shape, lambda i, f: (0, 0)),
                  pl.BlockSpec((1, d), lambda i, f: (0, 0)),
                  pl.BlockSpec((d, tf), lambda i, f: (0, f)),
                  pl.BlockSpec((d, tf), lambda i, f: (0, f)),
                  pl.BlockSpec((tf, d), lambda i, f: (f, 0))],
        out_specs=pl.BlockSpec((tm, d), lambda i, f: (i, 0)),
        out_shape=jax.ShapeDtypeStruct((t, d), F32),
        scratch_shapes=[pltpu.VMEM((tm, d), BF16)],
        compiler_params=_params("parallel", "arbitrary"),
        name="dense_ffn",
    )(x, a, wo, g.reshape(1, d), wg, wu, wd)


MOE_TILE = 512
META_ROWS = 8
ROUTE_SUB = 128
COMBINE_SUB = 256


def _pack_halves(x):
    half = x.shape[1] // 2
    bits = lax.bitcast_convert_type(x.astype(BF16).astype(F32), jnp.uint32)
    return bits[:, :half] | (bits[:, half:] >> 16)


def _unpack_halves(w):
    hi = lax.bitcast_convert_type(w & jnp.uint32(0xFFFF0000), F32)
    lo = lax.bitcast_convert_type(w << 16, F32)
    return jnp.concatenate([hi, lo], axis=1)


def _router_kernel(x_ref, g_ref, rt_ref, su_ref, eye_ref, hb_ref, meta_ref, metac_ref, cnt_ref, carry_ref):
    @pl.when(pl.program_id(0) == 0)
    def _():
        carry_ref[...] = jnp.zeros_like(carry_ref)

    h = _rms(x_ref[...], g_ref[...])
    hb_ref[...] = _pack_halves(h)
    ne = rt_ref.shape[0]
    r = rt_ref[...]
    r_hi = r.astype(BF16).astype(F32)
    h_hi = h.astype(BF16)
    h_lo = (h - h_hi.astype(F32)).astype(BF16)
    both_r = _dot_nt(jnp.concatenate([r_hi, r - r_hi], axis=0).astype(BF16), h_hi)
    logits = both_r[:ne] + both_r[ne:] + _dot_nt(r_hi.astype(BF16), h_lo)
    ne, ct = logits.shape
    eio = lax.broadcasted_iota(jnp.int32, (ne, ct), 0).astype(F32)
    m1 = jnp.max(logits, axis=0, keepdims=True)
    i1 = jnp.min(jnp.where(logits == m1, eio, float(ne)), axis=0, keepdims=True)
    rest = jnp.where(eio == i1, -jnp.inf, logits)
    m2 = jnp.max(rest, axis=0, keepdims=True)
    i2 = jnp.min(jnp.where(rest == m2, eio, float(ne)), axis=0, keepdims=True)
    e2 = jnp.exp(m2 - m1)
    g1 = 1.0 / (1.0 + e2)
    g2 = e2 / (1.0 + e2)
    oh1 = (eio == i1).astype(F32)
    oh2 = (eio == i2).astype(F32)
    both = oh1 + oh2
    seen = carry_ref[:, 0:1] + _dot(both.astype(BF16), su_ref[...])
    r1 = jnp.sum(oh1 * seen, axis=0, keepdims=True)
    r2 = jnp.sum(oh2 * seen, axis=0, keepdims=True)
    meta = jnp.zeros((META_ROWS, ct), F32)
    for row, val in enumerate((i1, i2, r1, r2, g1, g2)):
        meta = jnp.where(eio == float(row), val, meta)
    meta_ref[...] = meta
    metac = jnp.zeros((ct, META_ROWS), F32)
    rest_bits = meta
    for _ in range(3):
        piece = rest_bits.astype(BF16)
        metac = metac + _dot_nt(eye_ref[...], piece)
        rest_bits = rest_bits - piece.astype(F32)
    metac_ref[...] = metac
    carry_ref[...] = carry_ref[...] + jnp.sum(both, axis=1, keepdims=True)
    cnt_ref[0] = carry_ref[...]


def _router(x, g, router):
    t, d = x.shape
    ct = _tile(t, MOE_TILE)
    nc = t // ct
    su = jnp.asarray(np.triu(np.ones((ct, ct), np.float32), 1), BF16)
    eye = jnp.eye(ct, dtype=BF16)
    return pl.pallas_call(
        _router_kernel,
        grid=(nc,),
        in_specs=[pl.BlockSpec((ct, d), lambda i: (i, 0)),
                  pl.BlockSpec((1, d), lambda i: (0, 0)),
                  pl.BlockSpec((N_EXPERTS, d), lambda i: (0, 0)),
                  pl.BlockSpec((ct, ct), lambda i: (0, 0)),
                  pl.BlockSpec((ct, ct), lambda i: (0, 0))],
        out_specs=[pl.BlockSpec((ct, d // 2), lambda i: (i, 0)),
                   pl.BlockSpec((META_ROWS, ct), lambda i: (0, i)),
                   pl.BlockSpec((ct, META_ROWS), lambda i: (i, 0)),
                   pl.BlockSpec((1, N_EXPERTS, LANES), lambda i: (i, 0, 0))],
        out_shape=[jax.ShapeDtypeStruct((t, d // 2), jnp.uint32),
                   jax.ShapeDtypeStruct((META_ROWS, t), F32),
                   jax.ShapeDtypeStruct((t, META_ROWS), F32),
                   jax.ShapeDtypeStruct((nc, N_EXPERTS, LANES), F32)],
        scratch_shapes=[pltpu.VMEM((N_EXPERTS, LANES), F32)],
        compiler_params=_params("arbitrary"),
        name="moe_router",
    )(x, g.reshape(1, d), router.T, su, eye)


def _group_layout(cnt_after, t, rt):
    rows = 2 * t + N_EXPERTS * rt
    counts = cnt_after[:, :, 0].astype(jnp.int32)
    padded = (counts[-1] + rt - 1) // rt * rt
    off_end = jnp.cumsum(padded)
    off = (off_end - padded).astype(jnp.int32)
    tile_start = jnp.arange(rows // rt, dtype=jnp.int32) * rt
    tile_expert = jnp.minimum(_count_le(off_end, tile_start), N_EXPERTS - 1).astype(jnp.int32)
    tile_valid = (tile_start < off_end[-1]).astype(jnp.int32)
    return rows, off, counts, tile_expert, tile_valid


def _count_le(sorted_vals, x):
    return jnp.sum((sorted_vals[None, :] <= x[:, None]).astype(jnp.int32), axis=1)


def _work_lists(counts, off, rows, t, ct):
    nc = t // ct
    nrt = rows // ct
    before = jnp.concatenate([jnp.zeros((1, N_EXPERTS), jnp.int32), counts[:-1]], axis=0)
    chunk_start = (off[:, None] + before.T).reshape(-1)
    tile_start = jnp.arange(nrt, dtype=jnp.int32) * ct
    starts = jnp.sort(jnp.concatenate([chunk_start, tile_start]))
    ends = jnp.concatenate([starts[1:], jnp.array([rows], jnp.int32)])
    item_r = jnp.minimum(starts // ct, nrt - 1)
    item_c = (_count_le(chunk_start, starts) - 1) % nc
    first = lambda key: jnp.concatenate([jnp.ones((1,), jnp.int32), (key[1:] != key[:-1]).astype(jnp.int32)])
    dispatch = (item_r, item_c, starts, ends, first(item_r))
    order = jnp.argsort(item_c * (rows + 1) + starts)
    c2 = item_c[order]
    combine = (c2, item_r[order], starts[order], ends[order], first(c2))
    return dispatch, combine


def _positions(expert, rank, off_ref):
    base = jnp.zeros_like(rank)
    for e in range(N_EXPERTS):
        base = jnp.where(expert == float(e), off_ref[e].astype(F32), base)
    return base + rank


def _dispatch_kernel(r_ref, c_ref, s_ref, e_ref, first_ref, off_ref, meta_ref, h_ref, xs_ref, gate_ref):
    n = pl.program_id(0)
    rt, ct = xs_ref.shape[0], h_ref.shape[0]

    @pl.when(first_ref[n] == 1)
    def _():
        xs_ref[...] = jnp.zeros_like(xs_ref)
        gate_ref[...] = jnp.zeros_like(gate_ref)

    meta = meta_ref[...]
    pos0 = _positions(meta[0:1], meta[2:3], off_ref)
    pos1 = _positions(meta[1:2], meta[3:4], off_ref)
    sub = min(rt, ROUTE_SUB)
    for sb in range(rt // sub):
        base = r_ref[n] * rt + sb * sub

        @pl.when(jnp.logical_and(base < e_ref[n], base + sub > s_ref[n]))
        def _():
            row = lax.broadcasted_iota(jnp.int32, (sub, ct), 0) + base
            live = jnp.logical_and(row >= s_ref[n], row < e_ref[n])
            rowf = row.astype(F32)
            sel0 = jnp.logical_and(pos0 == rowf, live)
            sel1 = jnp.logical_and(pos1 == rowf, live)
            sel = jnp.logical_or(sel0, sel1).astype(F32).astype(BF16)
            rows = slice(sb * sub, (sb + 1) * sub)
            xs_ref[rows, :] += _pack_halves(_dot(sel, _unpack_halves(h_ref[...]).astype(BF16)))
            gate = jnp.where(sel0, meta[4:5], 0.0) + jnp.where(sel1, meta[5:6], 0.0)
            gate_ref[rows, :] += jnp.sum(gate, axis=1, keepdims=True)


def _dispatch(plan, off, meta, hb, rows, ct):
    t, d = hb.shape
    n_items = plan[0].shape[0]
    return pl.pallas_call(
        _dispatch_kernel,
        grid_spec=pltpu.PrefetchScalarGridSpec(
            num_scalar_prefetch=6,
            grid=(n_items,),
            in_specs=[pl.BlockSpec((META_ROWS, ct), lambda n, r, c, *_: (0, c[n])),
                      pl.BlockSpec((ct, d), lambda n, r, c, *_: (c[n], 0))],
            out_specs=[pl.BlockSpec((ct, d), lambda n, r, c, *_: (r[n], 0)),
                       pl.BlockSpec((ct, 1), lambda n, r, c, *_: (r[n], 0))]),
        out_shape=[jax.ShapeDtypeStruct((rows, d), jnp.uint32),
                   jax.ShapeDtypeStruct((rows, 1), F32)],
        compiler_params=_params("arbitrary"),
        name="moe_dispatch",
    )(*plan, off, meta, hb)


def _expert_kernel(te_ref, tv_ref, xs_ref, gate_ref, wg_ref, wu_ref, wd_ref, y_ref, acc_ref):
    r = pl.program_id(0)
    f = pl.program_id(1)

    @pl.when(f == 0)
    def _():
        acc_ref[...] = jnp.zeros_like(acc_ref)

    @pl.when(tv_ref[r] == 1)
    def _():
        _run_swiglu(acc_ref, _unpack_halves(xs_ref[...]).astype(BF16), lambda c: wg_ref[0, :, c],
                    lambda c: wu_ref[0, :, c], lambda c: wd_ref[0, c, :].astype(BF16), wg_ref.shape[2])

    @pl.when(f == pl.num_programs(1) - 1)
    def _():
        y_ref[...] = _pack_halves(acc_ref[...] * gate_ref[...])


def _experts(tile_expert, tile_valid, xs, gate, wg, wu, wd, wd_first, ct):
    rows, dp = xs.shape
    d = 2 * dp
    fdim = wg.shape[2]
    tf = _tile(fdim, 1792)
    nf = fdim // tf

    def wcol(r, f, te, tv):
        return (te[r], 0, jnp.where(tv[r] == 1, f, nf - 1))

    def wrow(r, f, te, tv):
        return (wd_first + te[r], jnp.where(tv[r] == 1, f, nf - 1), 0)

    return pl.pallas_call(
        _expert_kernel,
        grid_spec=pltpu.PrefetchScalarGridSpec(
            num_scalar_prefetch=2,
            grid=(rows // ct, nf),
            in_specs=[pl.BlockSpec((ct, dp), lambda r, f, te, tv: (r, 0)),
                      pl.BlockSpec((ct, 1), lambda r, f, te, tv: (r, 0)),
                      pl.BlockSpec((1, d, tf), wcol),
                      pl.BlockSpec((1, d, tf), wcol),
                      pl.BlockSpec((1, tf, d), wrow)],
            out_specs=pl.BlockSpec((ct, dp), lambda r, f, te, tv: (r, 0)),
            scratch_shapes=[pltpu.VMEM((ct, d), F32)]),
        out_shape=jax.ShapeDtypeStruct((rows, dp), jnp.uint32),
        compiler_params=_params("arbitrary", "arbitrary"),
        name="moe_experts",
    )(tile_expert, tile_valid, xs, gate, wg, wu, wd)


def _combine_kernel(c_ref, r_ref, s_ref, e_ref, first_ref, off_ref, x_ref, metac_ref, y_ref, o_ref):
    n = pl.program_id(0)
    ct, rt = x_ref.shape[0], y_ref.shape[0]

    @pl.when(first_ref[n] == 1)
    def _():
        o_ref[...] = x_ref[...]

    metac = metac_ref[...]
    pos0 = _positions(metac[:, 0:1], metac[:, 2:3], off_ref)
    pos1 = _positions(metac[:, 1:2], metac[:, 3:4], off_ref)
    sub = min(rt, COMBINE_SUB)
    for sb in range(rt // sub):
        base = r_ref[n] * rt + sb * sub

        @pl.when(jnp.logical_and(base < e_ref[n], base + sub > s_ref[n]))
        def _():
            row = lax.broadcasted_iota(jnp.int32, (ct, sub), 1) + base
            live = jnp.logical_and(row >= s_ref[n], row < e_ref[n])
            rowf = row.astype(F32)
            sel = jnp.logical_and(jnp.logical_or(pos0 == rowf, pos1 == rowf), live).astype(F32).astype(BF16)
            o_ref[...] += _dot(sel, _unpack_halves(y_ref[sb * sub:(sb + 1) * sub, :]).astype(BF16))


def _combine(plan, off, x, metac, y, ct):
    t, d = x.shape
    n_items = plan[0].shape[0]
    return pl.pallas_call(
        _combine_kernel,
        grid_spec=pltpu.PrefetchScalarGridSpec(
            num_scalar_prefetch=6,
            grid=(n_items,),
            in_specs=[pl.BlockSpec((ct, d), lambda n, c, r, *_: (c[n], 0)),
                      pl.BlockSpec((ct, META_ROWS), lambda n, c, r, *_: (c[n], 0)),
                      pl.BlockSpec((ct, d // 2), lambda n, c, r, *_: (r[n], 0))],
            out_specs=pl.BlockSpec((ct, d), lambda n, c, r, *_: (c[n], 0))),
        out_shape=jax.ShapeDtypeStruct((t, d), F32),
        compiler_params=_params("arbitrary"),
        name="moe_combine",
    )(*plan, off, x, metac, y)


def _cast_kernel(w_ref, o_ref):
    o_ref[...] = w_ref[...].astype(o_ref.dtype)


CAST_BLOCK_BYTES = 8 * 1024 * 1024


def _layer_bf16(w, layer):
    shape = w.shape[1:]
    cols = shape[-1]
    rows = math.prod(shape[:-1])
    tr = rows
    while tr * cols * 4 > CAST_BLOCK_BYTES and tr % 16 == 0:
        tr //= 2
    out = pl.pallas_call(
        _cast_kernel,
        grid=(rows // tr,),
        in_specs=[pl.BlockSpec((None, tr, cols), lambda i: (layer, i, 0))],
        out_specs=pl.BlockSpec((tr, cols), lambda i: (i, 0)),
        out_shape=jax.ShapeDtypeStruct((rows, cols), BF16),
        compiler_params=_params("parallel"),
        name="weights_bf16",
    )(w.reshape(w.shape[0], rows, cols))
    return out.reshape(shape)


SC_CORES = 2
SC_SUBCORES = 16
SC_CHUNK = 128


def _sc_worker_chunks(total):
    per = total // (SC_CORES * SC_SUBCORES)
    wid = lax.axis_index("s") * SC_CORES + lax.axis_index("c")
    return wid * per, per // SC_CHUNK


def _sc_mesh():
    return plsc.VectorSubcoreMesh(core_axis_name="c", subcore_axis_name="s")


def _sc_gather(table, idx):
    v, d = table.shape
    b = idx.shape[0]

    def body(table_hbm, idx_hbm, out_hbm, idx_v, rows_v, sem):
        base, chunks = _sc_worker_chunks(b)

        @pl.loop(0, chunks)
        def _(j):
            start = base + j * SC_CHUNK
            pltpu.sync_copy(idx_hbm.at[pl.ds(start, SC_CHUNK)], idx_v)
            pltpu.async_copy(table_hbm.at[idx_v], rows_v, sem).wait()
            pltpu.sync_copy(rows_v, out_hbm.at[pl.ds(start, SC_CHUNK)])

    return pl.kernel(
        body,
        out_type=jax.ShapeDtypeStruct((b, d), table.dtype),
        mesh=_sc_mesh(),
        scratch_types=[pltpu.VMEM((SC_CHUNK,), jnp.int32),
                       pltpu.VMEM((SC_CHUNK, d), table.dtype),
                       pltpu.SemaphoreType.DMA],
        name="sc_row_gather",
    )(table, idx)


def _sc_scatter_rows(src, idx, n):
    t, d = src.shape
    b = idx.shape[0]

    def body(src_hbm, idx_hbm, out_hbm, idx_v, rows_v):
        base, chunks = _sc_worker_chunks(b)

        @pl.loop(0, chunks)
        def _(j):
            start = base + j * SC_CHUNK
            first = lax.rem(start, t)
            pltpu.sync_copy(idx_hbm.at[pl.ds(start, SC_CHUNK)], idx_v)
            pltpu.sync_copy(src_hbm.at[pl.ds(first, SC_CHUNK)], rows_v)
            pltpu.sync_copy(rows_v, out_hbm.at[idx_v])

    return pl.kernel(
        body,
        out_type=jax.ShapeDtypeStruct((n, d), src.dtype),
        mesh=_sc_mesh(),
        scratch_types=[pltpu.VMEM((SC_CHUNK,), jnp.int32), pltpu.VMEM((SC_CHUNK, d), src.dtype)],
        name="sc_row_scatter",
    )(src, idx)


def _weighted_add_kernel(x_ref, metac_ref, y0_ref, y1_ref, g_ref, o_ref, *, normed):
    gates = metac_ref[...]
    out = (x_ref[...] + gates[:, 4:5] * _unpack_halves(y0_ref[...])
           + gates[:, 5:6] * _unpack_halves(y1_ref[...]))
    o_ref[...] = _rms(out, g_ref[...]) if normed else out


def _weighted_add(x, metac, ysel, out_gain=None):
    t, d = x.shape
    tm = _tile(t, 512)
    nt = t // tm
    gain = jnp.ones((d,), F32) if out_gain is None else out_gain
    return pl.pallas_call(
        functools.partial(_weighted_add_kernel, normed=out_gain is not None),
        grid=(nt,),
        in_specs=[pl.BlockSpec((tm, d), lambda i: (i, 0)),
                  pl.BlockSpec((tm, META_ROWS), lambda i: (i, 0)),
                  pl.BlockSpec((tm, d // 2), lambda i: (i, 0)),
                  pl.BlockSpec((tm, d // 2), lambda i: (nt + i, 0)),
                  pl.BlockSpec((1, d), lambda i: (0, 0))],
        out_specs=pl.BlockSpec((tm, d), lambda i: (i, 0)),
        out_shape=jax.ShapeDtypeStruct((t, d), F32),
        compiler_params=_params("parallel"),
        name="moe_weighted_add",
    )(x, metac, ysel, ysel, gain.reshape(1, d))


def _moe(x, g, router, wg, wu, wd, wd_first, out_gain=None):
    t = x.shape[0]
    ct = _tile(t, MOE_TILE)
    hb, meta, metac, cnt = _router(x, g, router)
    sc_rows = SC_CORES * SC_SUBCORES * SC_CHUNK
    rows, off, counts, tile_expert, tile_valid = _group_layout(cnt, t, ct)
    if (2 * t) % sc_rows == 0 and rows % sc_rows == 0:
        base = sum(jnp.where(meta[0:2] == float(e), off[e], 0) for e in range(N_EXPERTS))
        pos = (base + meta[2:4].astype(jnp.int32)).reshape(-1)
        xs = _sc_scatter_rows(hb, pos, rows)
        y = _experts(tile_expert, tile_valid, xs, jnp.ones((rows, 1), F32), wg, wu, wd, wd_first, ct)
        return _weighted_add(x, metac, _sc_gather(y, pos), out_gain)
    dispatch, combine = _work_lists(counts, off, rows, t, ct)
    xs, gate = _dispatch(dispatch, off, meta, hb, rows, ct)
    y = _experts(tile_expert, tile_valid, xs, gate, wg, wu, wd, wd_first, ct)
    out = _combine(combine, off, x, metac, y, ct)
    return out if out_gain is None else _final_norm(out, out_gain)


def _final_norm_kernel(x_ref, g_ref, o_ref):
    o_ref[...] = _rms(x_ref[...], g_ref[...])


def _final_norm(x, g):
    t, d = x.shape
    tm = _tile(t, 1024)
    return pl.pallas_call(
        _final_norm_kernel,
        grid=(t // tm,),
        in_specs=[pl.BlockSpec((tm, d), lambda i: (i, 0)), pl.BlockSpec((1, d), lambda i: (0, 0))],
        out_specs=pl.BlockSpec((tm, d), lambda i: (i, 0)),
        out_shape=jax.ShapeDtypeStruct((t, d), F32),
        compiler_params=_params("parallel"),
        name="final_norm",
    )(x, g.reshape(1, d))


def kernel(x, rel_bias, attn_norm, ffn_norm, final_norm, swa_w_qkv, swa_b_qkv, swa_sinks, swa_w_o, sb_w_qkv, sb_w_o, mla_w_down, mla_q_norm, mla_w_uq, mla_kv_norm, mla_w_ukv, mla_w_o, ffn_w_gate, ffn_w_up, ffn_w_down, moe_router, moe_w_gate, moe_w_up, moe_w_down):
    batch, seq, d = x.shape
    depth = attn_norm.shape[0]
    xt = x.reshape(batch * seq, d)
    swa_bias = _swa_bias(rel_bias)
    for i in range(depth):
        mixer, j = i % N_MIXERS, i // N_MIXERS
        if mixer == 0:
            w, b, cs = _swa_weights(swa_w_qkv[j], swa_b_qkv[j])
            qkv = _norm_proj(xt, attn_norm[i], w, b, cs, "swa_proj")
            o = _swa_attention(qkv, swa_bias, swa_sinks[j] * LOG2E, batch, seq)
            wo = swa_w_o[j].astype(BF16)
        elif mixer == 1:
            n = sb_w_qkv.shape[2] // 3
            cs = jnp.concatenate([jnp.full((n,), HEAD_DIM ** -0.5 * LOG2E, F32), jnp.ones((n,), F32)])
            qk, vt = _norm_proj_vt(xt, attn_norm[i], sb_w_qkv[j].astype(BF16), cs, n, seq, "sb_proj")
            o = _sb_attention(qk, vt, batch, seq)
            wo = sb_w_o[j].astype(BF16)
        else:
            a, vt = _mla_proj(xt, attn_norm[i], mla_w_down[j], mla_q_norm[j], mla_w_uq[j],
                              mla_kv_norm[j], mla_w_ukv[j], seq)
            o = _mla_attention(a, vt, batch, seq)
            wo = mla_w_o[j].astype(BF16)
        f = i // 2
        if i % 2 == 0:
            xt = _ffn(xt, o, wo, ffn_norm[i], _layer_bf16(ffn_w_gate, f), _layer_bf16(ffn_w_up, f),
                      _layer_bf16(ffn_w_down, f))
        else:
            last = final_norm if i == depth - 1 else None
            xt = _out_proj(xt, o, wo, "mixer_out")
            xt = _moe(xt, ffn_norm[i], moe_router[f], _layer_bf16(moe_w_gate, f), _layer_bf16(moe_w_up, f),
                      moe_w_down.reshape((-1,) + moe_w_down.shape[2:]), f * N_EXPERTS, last)
    if depth % 2 == 1:
        xt = _final_norm(xt, final_norm)
    return xt.reshape(batch, seq, d)
```
